```python
import math
import jax
import jax.numpy as jnp
from jax import lax
import numpy as np


D_MODEL = 1024
BATCH = 8
SEQ = 4096
DEPTH = 4

HEAD_DIM = 64
ROPE_THETA = 10000.0
NEG_INF = -1e30
BIG_SCORE = 1e30
LN_EPS = 1e-5
Q_BLOCK = 128

A_HEADS = 4
A_VDIM = 2 * HEAD_DIM
B_HEADS = 8
B_KV_HEADS = 2
IDX_HEADS = 4
IDX_DIM = 64
DSA_TOPK = 256
DSA_Q_CHUNK = 64
C_HEADS = 8
MOBA_BLOCK = 256
MOBA_TOPK = 3
MOBA_Q_CHUNK = 16
D_HEADS = 8
D_KV_HEADS = 2
NSA_CMP_LEN = 32
NSA_CMP_STRIDE = 16
NSA_SLC_BLOCK = 64
NSA_SLC_TOPK = 16
NSA_WINDOW = 512
NSA_PHI_HIDDEN = 256
NSA_Q_CHUNK = 32

F_DENSE = 2816
N_EXPERTS = 8
TOP_K = 2
F_EXPERT = 3584
MOE_BLOCK = 256

DN_ALPHA = (2 * DEPTH) ** 0.25
DN_BETA = (8 * DEPTH) ** -0.25
N_EVEN = (DEPTH + 1) // 2
N_ODD = DEPTH // 2

A_QK = A_HEADS * 2 * HEAD_DIM
A_V = A_HEADS * A_VDIM
B_Q = B_HEADS * HEAD_DIM
B_KV = B_KV_HEADS * HEAD_DIM
I_Q = IDX_HEADS * IDX_DIM
EVEN_SIZES = (A_QK, A_QK, A_V, B_Q, B_KV, B_KV, I_Q, IDX_DIM, IDX_HEADS)
P_EVEN = sum(EVEN_SIZES)
D_MIX_EVEN = A_V + B_Q
C_QKV = C_HEADS * HEAD_DIM
D_Q = D_HEADS * HEAD_DIM
D_KV = D_KV_HEADS * HEAD_DIM
ODD_SIZES = (C_QKV, C_QKV, C_QKV, D_Q) + (D_KV,) * 6 + (D_HEADS * 3,)
P_ODD = sum(ODD_SIZES)
D_MIX_ODD = C_QKV + D_Q

kernel_name = 'hybrid_diff_dsa_moba_nsa_moe'

F32 = jnp.float32


def split_cols(h, sizes):
    out, off = [], 0
    for s in sizes:
        out.append(h[..., off:off + s])
        off += s
    return out


def layer_norm(x, g, b):
    xf = x.astype(F32)
    mu = jnp.mean(xf, axis=-1, keepdims=True)
    var = jnp.mean(jnp.square(xf - mu), axis=-1, keepdims=True)
    return ((xf - mu) * lax.rsqrt(var + LN_EPS) * g.astype(F32) + b.astype(F32)).astype(x.dtype)


def rope(x, pos):
    d = x.shape[-1]
    inv = ROPE_THETA ** (-jnp.arange(0, d, 2, dtype=F32) / d)
    ang = pos.astype(F32)[:, None] * inv[None, :]
    cos = jnp.cos(ang)[None, :, None, :]
    sin = jnp.sin(ang)[None, :, None, :]
    xf = x.astype(F32)
    x1, x2 = xf[..., : d // 2], xf[..., d // 2:]
    return jnp.concatenate([x1 * cos - x2 * sin, x2 * cos + x1 * sin], axis=-1).astype(x.dtype)


def masked_softmax(s, mask):
    p = jax.nn.softmax(jnp.where(mask, s, NEG_INF), axis=-1)
    return jnp.where(mask, p, 0.0)


def map_query_chunks(fn, seq_len, chunk):
    starts = jnp.arange(seq_len // chunk, dtype=jnp.int32) * chunk
    out = lax.map(fn, starts)
    out = jnp.moveaxis(out, 0, 1)
    return out.reshape(out.shape[0], seq_len, *out.shape[3:])


def swiglu(x, wg, wu, wd):
    return (jax.nn.silu(x @ wg) * (x @ wu)) @ wd


def diff_attention(q, k, v, lam_params, subln_w, lam_init):
    B, S, Ha, _, dh = q.shape
    scale = dh ** -0.5
    lp = lam_params.astype(F32)
    lam = jnp.exp(jnp.sum(lp[0] * lp[1])) - jnp.exp(jnp.sum(lp[2] * lp[3])) + lam_init
    key_pos = jnp.arange(S)

    def block(c0):
        qb = lax.dynamic_slice_in_dim(q, c0, Q_BLOCK, axis=1)
        s = jnp.einsum('bqhjd,bshjd->bhjqs', qb, k, preferred_element_type=F32) * scale
        mask = key_pos[None, :] <= (c0 + jnp.arange(Q_BLOCK))[:, None]
        p = masked_softmax(s, mask)
        w = p[:, :, 0] - lam * p[:, :, 1]
        return jnp.einsum('bhqs,bshe->bqhe', w.astype(v.dtype), v)

    o = map_query_chunks(block, S, Q_BLOCK).astype(F32)
    o = o * lax.rsqrt(jnp.mean(o * o, axis=-1, keepdims=True) + LN_EPS)
    o = o * subln_w.astype(F32) * (1.0 - lam_init)
    return o.astype(v.dtype).reshape(B, S, Ha * v.shape[-1])


def dsa_attention(q, k, v, iq, ik, iw):
    B, S, H, dh = q.shape
    G = k.shape[2]
    R = H // G
    dt = q.dtype
    scale = dh ** -0.5
    q5 = q.reshape(B, S, G, R, dh)
    k_sel = min(DSA_TOPK, S // 4)
    key_pos = jnp.arange(S)
    bi = jnp.arange(B)[:, None, None]

    def chunk(c0):
        n = DSA_Q_CHUNK
        qc = lax.dynamic_slice_in_dim(q5, c0, n, axis=1)
        iqc = lax.dynamic_slice_in_dim(iq, c0, n, axis=1)
        iwc = lax.dynamic_slice_in_dim(iw, c0, n, axis=1)
        tq = c0 + jnp.arange(n)
        logits = jnp.einsum('bqhd,bsd->bqhs', iqc, ik, preferred_element_type=F32)
        score = jnp.einsum('bqh,bqhs->bqs', iwc.astype(F32), jax.nn.relu(logits))
        score = jnp.where(key_pos[None, None, :] <= tq[None, :, None], score, NEG_INF)
        _, idx = lax.top_k(score, k_sel)
        ok = idx <= tq[None, :, None]
        kg = k[bi, idx]
        vg = v[bi, idx]
        s = jnp.einsum('bqgrd,bqkgd->bgrqk', qc, kg, preferred_element_type=F32) * scale
        p = masked_softmax(s, ok[:, None, None]).astype(dt)
        return jnp.einsum('bgrqk,bqkgd->bqgrd', p, vg)

    o = map_query_chunks(chunk, S, DSA_Q_CHUNK)
    return o.reshape(B, S, H * dh)


def moba_attention(q, k, v):
    B, S, H, dh = q.shape
    dt = q.dtype
    scale = dh ** -0.5
    blk_len = MOBA_BLOCK
    nb = -(-S // blk_len)
    pad = nb * blk_len - S
    k_pad = jnp.pad(k, ((0, 0), (0, pad), (0, 0), (0, 0)))
    v_pad = jnp.pad(v, ((0, 0), (0, pad), (0, 0), (0, 0)))
    k_blk = k_pad.reshape(B, nb, blk_len, H, dh)
    k_mean = jnp.mean(k_blk.astype(F32), axis=2).astype(dt)
    k_bt = k_blk.transpose(0, 3, 1, 2, 4)
    v_bt = v_pad.reshape(B, nb, blk_len, H, dh).transpose(0, 3, 1, 2, 4)
    n_sel = max(1, min(MOBA_TOPK, nb - 1))
    bi = jnp.arange(B)[:, None, None, None]
    hi = jnp.arange(H)[None, :, None, None]
    blk_ids = jnp.arange(nb)
    own_off = jnp.arange(blk_len)
    n = MOBA_Q_CHUNK

    def chunk(c0):
        qc = lax.dynamic_slice_in_dim(q, c0, n, axis=1)
        tq = c0 + jnp.arange(n)
        blk = c0 // blk_len
        gate = jnp.einsum('bqhd,bnhd->bhqn', qc, k_mean, preferred_element_type=F32)
        gate = jnp.where(blk_ids < blk, gate, NEG_INF)
        _, idx = lax.top_k(gate, n_sel)
        sel_ok = idx < blk
        kg = k_bt[bi, hi, idx]
        vg = v_bt[bi, hi, idx]
        s_sel = jnp.einsum('bqhd,bhqmkd->bhqmk', qc, kg, preferred_element_type=F32) * scale
        own_k = lax.dynamic_slice_in_dim(k_pad, blk * blk_len, blk_len, axis=1)
        own_v = lax.dynamic_slice_in_dim(v_pad, blk * blk_len, blk_len, axis=1)
        s_own = jnp.einsum('bqhd,bkhd->bhqk', qc, own_k, preferred_element_type=F32) * scale
        own_ok = (blk * blk_len + own_off)[None, :] <= tq[:, None]
        s = jnp.concatenate([s_sel.reshape(B, H, n, n_sel * blk_len), s_own], axis=-1)
        mask = jnp.concatenate([
            jnp.broadcast_to(sel_ok[..., None], (B, H, n, n_sel, blk_len)).reshape(B, H, n, n_sel * blk_len),
            jnp.broadcast_to(own_ok, (B, H, n, blk_len))], axis=-1)
        p = masked_softmax(s, mask).astype(dt)
        p_sel = p[..., : n_sel * blk_len].reshape(B, H, n, n_sel, blk_len)
        p_own = p[..., n_sel * blk_len:]
        return (jnp.einsum('bhqmk,bhqmkd->bqhd', p_sel, vg)
                + jnp.einsum('bhqk,bkhd->bqhd', p_own, own_v))

    o = map_query_chunks(chunk, S, n)
    return o.reshape(B, S, H * dh)


def compress_blocks(blocks, pe, w1, w2):
    B, Nc, L, G, dh = blocks.shape
    h = blocks + pe[:, None, :].astype(blocks.dtype)
    h = jnp.moveaxis(h, 3, 2).reshape(B, Nc, G, L * dh)
    return jax.nn.gelu(h @ w1) @ w2


def nsa_attention(q, kc_tok, vc_tok, ks, vs, kw, vw, gates, pe, phi_w1, phi_w2, pos):
    B, S, H, dh = q.shape
    G = ks.shape[2]
    R = H // G
    dt = q.dtype
    scale = dh ** -0.5
    q_raw = q.reshape(B, S, G, R, dh)
    q_rot = rope(q, pos).reshape(B, S, G, R, dh)
    ks = rope(ks, pos)
    kw = rope(kw, pos)
    n_cmp = (S - NSA_CMP_LEN) // NSA_CMP_STRIDE + 1
    cmp_start = np.arange(n_cmp) * NSA_CMP_STRIDE
    tok_idx = cmp_start[:, None] + np.arange(NSA_CMP_LEN)[None, :]
    k_cmp = compress_blocks(kc_tok[:, tok_idx], pe[0], phi_w1[0], phi_w2[0])
    v_cmp = compress_blocks(vc_tok[:, tok_idx], pe[1], phi_w1[1], phi_w2[1])
    cmp_end = jnp.asarray(cmp_start + NSA_CMP_LEN - 1, dtype=jnp.int32)
    sb = NSA_SLC_BLOCK
    n_sb = S // sb
    sb_start = np.arange(n_sb) * sb
    shares = ((cmp_start[:, None] <= sb_start[None, :] + sb - 1)
              & (cmp_start[:, None] + NSA_CMP_LEN - 1 >= sb_start[None, :]))
    cmp_to_slc = jnp.asarray(shares, dtype=F32)
    n_sel = min(NSA_SLC_TOPK, n_sb)
    ks_blk = ks.reshape(B, n_sb, sb, G, dh).transpose(0, 3, 1, 2, 4)
    vs_blk = vs.reshape(B, n_sb, sb, G, dh).transpose(0, 3, 1, 2, 4)
    win = NSA_WINDOW
    kw_pad = jnp.pad(kw, ((0, 0), (win, 0), (0, 0), (0, 0)))
    vw_pad = jnp.pad(vw, ((0, 0), (win, 0), (0, 0), (0, 0)))
    bi = jnp.arange(B)[:, None, None, None]
    gi = jnp.arange(G)[None, :, None, None]
    sb_ids = jnp.arange(n_sb)
    sb_off = jnp.arange(sb)
    n = NSA_Q_CHUNK
    win_off = jnp.arange(win + n)

    def chunk(c0):
        tq = c0 + jnp.arange(n)
        qr = lax.dynamic_slice_in_dim(q_rot, c0, n, axis=1)
        qraw = lax.dynamic_slice_in_dim(q_raw, c0, n, axis=1)
        g = lax.dynamic_slice_in_dim(gates, c0, n, axis=1)
        s_c = jnp.einsum('bqgrd,bngd->bgrqn', qraw, k_cmp, preferred_element_type=F32) * scale
        p_c = masked_softmax(s_c, cmp_end[None, :] <= tq[:, None])
        o_cmp = jnp.einsum('bgrqn,bngd->bqgrd', p_c.astype(dt), v_cmp)
        imp = jnp.einsum('bgrqn,nj->bgqj', p_c, cmp_to_slc)
        cur = (tq // sb)[:, None]
        causal = sb_ids[None, :] <= cur
        forced = (sb_ids[None, :] == 0) | ((sb_ids[None, :] >= cur - 1) & causal)
        imp = jnp.where(forced, BIG_SCORE, jnp.where(causal, imp, NEG_INF))
        _, idx = lax.top_k(imp, n_sel)
        kg = ks_blk[bi, gi, idx].reshape(B, G, n, n_sel * sb, dh)
        vg = vs_blk[bi, gi, idx].reshape(B, G, n, n_sel * sb, dh)
        key_pos = (idx[..., None] * sb + sb_off).reshape(B, G, n, n_sel * sb)
        s_s = jnp.einsum('bqgrd,bgqmd->bgrqm', qr, kg, preferred_element_type=F32) * scale
        p_s = masked_softmax(s_s, (key_pos <= tq[None, None, :, None])[:, :, None]).astype(dt)
        o_slc = jnp.einsum('bgrqm,bgqmd->bqgrd', p_s, vg)
        kwc = lax.dynamic_slice_in_dim(kw_pad, c0, win + n, axis=1)
        vwc = lax.dynamic_slice_in_dim(vw_pad, c0, win + n, axis=1)
        kpos = c0 - win + win_off
        dist = tq[:, None] - kpos[None, :]
        s_w = jnp.einsum('bqgrd,bkgd->bgrqk', qr, kwc, preferred_element_type=F32) * scale
        p_w = masked_softmax(s_w, (dist >= 0) & (dist < win) & (kpos[None, :] >= 0)).astype(dt)
        o_win = jnp.einsum('bgrqk,bkgd->bqgrd', p_w, vwc)
        return g[..., 0:1] * o_cmp + g[..., 1:2] * o_slc + g[..., 2:3] * o_win

    o = map_query_chunks(chunk, S, n)
    return o.reshape(B, S, H * dh)


def even_mixer(x, w_in, w_out, lam_params, subln_w, lam_init, pos):
    B, S, _ = x.shape
    aq, ak, av, bq, bk, bv, iq, ik, iw = split_cols(x @ w_in, EVEN_SIZES)
    aq = rope(aq.reshape(B, S, 2 * A_HEADS, HEAD_DIM), pos).reshape(B, S, A_HEADS, 2, HEAD_DIM)
    ak = rope(ak.reshape(B, S, 2 * A_HEADS, HEAD_DIM), pos).reshape(B, S, A_HEADS, 2, HEAD_DIM)
    o_a = diff_attention(aq, ak, av.reshape(B, S, A_HEADS, A_VDIM), lam_params, subln_w, lam_init)
    bq = rope(bq.reshape(B, S, B_HEADS, HEAD_DIM), pos)
    bk = rope(bk.reshape(B, S, B_KV_HEADS, HEAD_DIM), pos)
    bv = bv.reshape(B, S, B_KV_HEADS, HEAD_DIM)
    iq = rope(iq.reshape(B, S, IDX_HEADS, IDX_DIM), pos)
    ik = rope(ik[:, :, None, :], pos)[:, :, 0]
    o_b = dsa_attention(bq, bk, bv, iq, ik, iw)
    return jnp.concatenate([o_a, o_b], axis=-1) @ w_out


def odd_mixer(x, w_in, w_out, gate_b, pe, phi_w1, phi_w2, pos):
    B, S, _ = x.shape
    cq, ck, cv, dq, dkc, dvc, dks, dvs, dkw, dvw, dg = split_cols(x @ w_in, ODD_SIZES)
    hs = (B, S, C_HEADS, HEAD_DIM)
    o_c = moba_attention(rope(cq.reshape(hs), pos), rope(ck.reshape(hs), pos), cv.reshape(hs))
    kv = (B, S, D_KV_HEADS, HEAD_DIM)
    gates = jax.nn.sigmoid(dg.astype(F32) + gate_b.astype(F32)).astype(x.dtype)
    gates = gates.reshape(B, S, D_KV_HEADS, D_HEADS // D_KV_HEADS, 3)
    o_d = nsa_attention(dq.reshape(B, S, D_HEADS, HEAD_DIM), dkc.reshape(kv), dvc.reshape(kv),
                        dks.reshape(kv), dvs.reshape(kv), dkw.reshape(kv), dvw.reshape(kv),
                        gates, pe, phi_w1, phi_w2, pos)
    return jnp.concatenate([o_c, o_d], axis=-1) @ w_out


def moe_swiglu(x, w_router, b_router, w_gate, w_up, w_down):
    B, S, D = x.shape
    xt = x.reshape(-1, D)
    N = xt.shape[0]
    logits = jnp.dot(xt, w_router, preferred_element_type=F32) + b_router.astype(F32)
    top_logit, top_e = lax.top_k(logits, TOP_K)
    gate = jax.nn.softmax(top_logit, axis=-1)
    nk = N * TOP_K
    e_flat = top_e.reshape(-1)
    tok_flat = jnp.repeat(jnp.arange(N, dtype=jnp.int32), TOP_K)
    g_flat = gate.reshape(-1)
    order = jnp.argsort(e_flat)
    e_sorted = e_flat[order]
    counts = jnp.bincount(e_flat, length=N_EXPERTS)
    padded = (counts + MOE_BLOCK - 1) // MOE_BLOCK * MOE_BLOCK
    pad_end = jnp.cumsum(padded)
    pad_start = pad_end - padded
    grp_start = jnp.cumsum(counts) - counts
    slot = pad_start[e_sorted] + jnp.arange(nk) - grp_start[e_sorted]
    n_blocks = -(-nk // MOE_BLOCK) + N_EXPERTS
    n_slots = n_blocks * MOE_BLOCK
    slot_tok = jnp.full((n_slots,), N, jnp.int32).at[slot].set(tok_flat[order])
    slot_gate = jnp.zeros((n_slots,), F32).at[slot].set(g_flat[order])
    block_e = jnp.minimum(jnp.searchsorted(pad_end, jnp.arange(n_blocks) * MOE_BLOCK, side='right'),
                          N_EXPERTS - 1)
    x_pad = jnp.concatenate([xt, jnp.zeros((1, D), xt.dtype)], axis=0)

    def expert_block(args):
        toks, e = args
        h = x_pad[toks]
        return (jax.nn.silu(h @ w_gate[e]) * (h @ w_up[e])) @ w_down[e]

    y_slots = lax.map(expert_block, (slot_tok.reshape(n_blocks, MOE_BLOCK), block_e))
    y_slots = y_slots.reshape(n_slots, D).astype(F32) * slot_gate[:, None]
    y = jnp.zeros((N + 1, D), F32).at[slot_tok].add(y_slots)[:N]
    return y.astype(x.dtype).reshape(B, S, D)


def setup_inputs(seed: int = 0) -> dict:
    key = jax.random.key(seed)
    keys = iter(jax.random.split(key, 32))

    def nrm(shape, scale):
        return jax.random.normal(next(keys), shape, F32) * scale

    d = D_MODEL
    return {
        'x': nrm((BATCH, SEQ, d), 1.0),
        'ev_w_in': nrm((N_EVEN, d, P_EVEN), d ** -0.5),
        'ev_w_out': nrm((N_EVEN, D_MIX_EVEN, d), D_MIX_EVEN ** -0.5 * DN_BETA),
        'dif_lambda': nrm((N_EVEN, 4, HEAD_DIM), 0.1),
        'dif_subln': 1.0 + nrm((N_EVEN, A_VDIM), 0.05),
        'ffd_w_gate': nrm((N_EVEN, d, F_DENSE), d ** -0.5),
        'ffd_w_up': nrm((N_EVEN, d, F_DENSE), d ** -0.5),
        'ffd_w_down': nrm((N_EVEN, F_DENSE, d), F_DENSE ** -0.5 * DN_BETA),
        'od_w_in': nrm((N_ODD, d, P_ODD), d ** -0.5),
        'od_w_out': nrm((N_ODD, D_MIX_ODD, d), D_MIX_ODD ** -0.5 * DN_BETA),
        'nsa_gate_b': nrm((N_ODD, D_HEADS * 3), 0.01),
        'nsa_pe': nrm((N_ODD, 2, NSA_CMP_LEN, HEAD_DIM), 0.1),
        'nsa_phi_w1': nrm((N_ODD, 2, NSA_CMP_LEN * HEAD_DIM, NSA_PHI_HIDDEN), (NSA_CMP_LEN * HEAD_DIM) ** -0.5),
        'nsa_phi_w2': nrm((N_ODD, 2, NSA_PHI_HIDDEN, HEAD_DIM), NSA_PHI_HIDDEN ** -0.5),
        'moe_w_router': nrm((N_ODD, d, N_EXPERTS), d ** -0.5),
        'moe_b_router': nrm((N_ODD, N_EXPERTS), 0.01),
        'moe_w_gate': nrm((N_ODD, N_EXPERTS, d, F_EXPERT), d ** -0.5),
        'moe_w_up': nrm((N_ODD, N_EXPERTS, d, F_EXPERT), d ** -0.5),
        'moe_w_down': nrm((N_ODD, N_EXPERTS, F_EXPERT, d), F_EXPERT ** -0.5 * DN_BETA),
        'ln_mix_g': 1.0 + nrm((DEPTH, d), 0.05),
        'ln_mix_b': nrm((DEPTH, d), 0.02),
        'ln_ffn_g': 1.0 + nrm((DEPTH, d), 0.05),
        'ln_ffn_b': nrm((DEPTH, d), 0.02),
    }


def reference(x, ev_w_in, ev_w_out, dif_lambda, dif_subln, ffd_w_gate, ffd_w_up, ffd_w_down,
              od_w_in, od_w_out, nsa_gate_b, nsa_pe, nsa_phi_w1, nsa_phi_w2,
              moe_w_router, moe_b_router, moe_w_gate, moe_w_up, moe_w_down,
              ln_mix_g, ln_mix_b, ln_ffn_g, ln_ffn_b):
    pos = jnp.arange(x.shape[1], dtype=jnp.int32)
    for l in range(DEPTH):
        i = l // 2
        if l % 2 == 0:
            lam_init = 0.8 - 0.6 * math.exp(-0.3 * l)
            mix = even_mixer(x, ev_w_in[i], ev_w_out[i], dif_lambda[i], dif_subln[i], lam_init, pos)
        else:
            mix = odd_mixer(x, od_w_in[i], od_w_out[i], nsa_gate_b[i], nsa_pe[i],
                            nsa_phi_w1[i], nsa_phi_w2[i], pos)
        x = layer_norm(DN_ALPHA * x + mix, ln_mix_g[l], ln_mix_b[l])
        if l % 2 == 0:
            ffn = swiglu(x, ffd_w_gate[i], ffd_w_up[i], ffd_w_down[i])
        else:
            ffn = moe_swiglu(x, moe_w_router[i], moe_b_router[i], moe_w_gate[i], moe_w_up[i], moe_w_down[i])
        x = layer_norm(DN_ALPHA * x + ffn, ln_ffn_g[l], ln_ffn_b[l])
    return x
```

```python
import functools
import math

import numpy as np
import jax
import jax.numpy as jnp
from jax import lax
from jax.experimental import pallas as pl
from jax.experimental.pallas import tpu as pltpu

F32 = jnp.float32
BF16 = jnp.bfloat16
I32 = jnp.int32

LANES = 128
VMEM_LIMIT = 56 * 1024 * 1024

DEPTH = 4
HEAD_DIM = 64
ROPE_THETA = 10000.0
LN_EPS = 1e-5
DN_ALPHA = (2 * DEPTH) ** 0.25
SCALE = HEAD_DIM ** -0.5
NEG = -1e30
BIG = 1e30
M_INIT = -1e30
NINF = float("-inf")

A_HEADS = 4
B_HEADS = 8
IDX_HEADS = 4
DSA_TOPK = 256
C_HEADS = 8
MOBA_BLOCK = 256
MOBA_TOPK = 3
D_HEADS = 8
NSA_CMP_LEN = 32
NSA_CMP_STRIDE = 16
NSA_SLC_BLOCK = 64
NSA_SLC_TOPK = 16
NSA_WINDOW = 512
NSA_PHI_HIDDEN = 256
N_EXPERTS = 8
TOP_K = 2
MOE_TM = 512


def _cparams(sem):
    return pltpu.CompilerParams(dimension_semantics=sem, vmem_limit_bytes=VMEM_LIMIT)


def _dot_nt(a, b):
    return lax.dot_general(a, b, (((1,), (1,)), ((), ())), preferred_element_type=F32)


def _layer_norm(y, g, b):
    mu = jnp.mean(y, axis=-1, keepdims=True)
    yc = y - mu
    var = jnp.mean(yc * yc, axis=-1, keepdims=True)
    return yc * lax.rsqrt(var + LN_EPS) * g + b


def _flash_step(carry, s, v):
    m, l, acc = carry
    m_new = jnp.maximum(m, jnp.max(s, axis=1, keepdims=True))
    alpha = jnp.exp(m - m_new)
    p = jnp.exp(s - m_new)
    l = alpha * l + jnp.sum(p, axis=1, keepdims=True)
    acc = alpha * acc + jnp.dot(p.astype(BF16), v, preferred_element_type=F32)
    return m_new, l, acc


def _flash_init(tq, dv):
    return (jnp.full((tq, 1), M_INIT, F32), jnp.zeros((tq, 1), F32), jnp.zeros((tq, dv), F32))


def _flash_out(carry):
    _, l, acc = carry
    return acc * jnp.where(l > 0.0, 1.0 / jnp.where(l > 0.0, l, 1.0), 0.0)


def _half_masks():
    lane = lax.broadcasted_iota(I32, (1, LANES), 1)
    return lane < HEAD_DIM


def _split_halves(t):
    lo = _half_masks()
    z = jnp.zeros_like(t)
    return jnp.where(lo, t, z), jnp.where(lo, z, t)


def _mm_kernel(x_ref, w_ref, *rest, rope):
    o_ref = rest[-1]
    acc = jnp.dot(x_ref[...].astype(BF16), w_ref[...], preferred_element_type=F32)
    if not rope:
        o_ref[...] = acc.astype(o_ref.dtype)
        return
    cos = rest[0][...]
    sin = rest[1][...]
    lane = lax.broadcasted_iota(I32, (1, LANES), 1)
    first = (lane % HEAD_DIM) < (HEAD_DIM // 2)
    for c in range(acc.shape[1] // LANES):
        a = acc[:, c * LANES:(c + 1) * LANES]
        rot = jnp.where(first, pltpu.roll(a, LANES - HEAD_DIM // 2, 1), pltpu.roll(a, HEAD_DIM // 2, 1))
        o_ref[:, c * LANES:(c + 1) * LANES] = (a * cos + rot * sin).astype(o_ref.dtype)


def _matmul(x, w, out_dtype, tm, tn, seq_len=None, rope_tabs=None):
    n, d = x.shape
    p = w.shape[1]
    rope = rope_tabs is not None
    in_specs = [pl.BlockSpec((tm, d), lambda i, j: (i, 0)),
                pl.BlockSpec((d, tn), lambda i, j: (0, j))]
    args = [x, w]
    if rope:
        nt = seq_len // tm
        in_specs += [pl.BlockSpec((tm, LANES), lambda i, j: (i % nt, 0))] * 2
        args += list(rope_tabs)
    return pl.pallas_call(
        functools.partial(_mm_kernel, rope=rope),
        grid=(n // tm, p // tn),
        in_specs=in_specs,
        out_specs=pl.BlockSpec((tm, tn), lambda i, j: (i, j)),
        out_shape=jax.ShapeDtypeStruct((n, p), out_dtype),
        compiler_params=_cparams(("parallel", "arbitrary")),
        name="proj_rope" if rope else "proj",
    )(*args)


def _rope_tables(seq_len):
    d = HEAD_DIM
    inv = ROPE_THETA ** (-jnp.arange(0, d, 2, dtype=F32) / d)
    ang = jnp.arange(seq_len, dtype=I32).astype(F32)[:, None] * inv[None, :]
    cos = jnp.cos(ang)
    sin = jnp.sin(ang)
    cos128 = jnp.tile(cos, (1, LANES // (d // 2)))
    sin128 = jnp.tile(jnp.concatenate([-sin, sin], axis=1), (1, LANES // d))
    return cos128, sin128


def _outproj_ln_kernel(x_ref, a_ref, b_ref, wa_ref, wb_ref, g_ref, bb_ref, xo_ref, xb_ref):
    mix = (jnp.dot(a_ref[...], wa_ref[...], preferred_element_type=F32)
           + jnp.dot(b_ref[...], wb_ref[...], preferred_element_type=F32))
    y = _layer_norm(DN_ALPHA * x_ref[...] + mix, g_ref[...], bb_ref[...])
    xo_ref[...] = y
    xb_ref[...] = y.astype(BF16)


def _outproj_ln(x, oa, ob, wa, wb, g, b, tm=512):
    n, d = x.shape
    ka, kb = oa.shape[1], ob.shape[1]
    row = lambda i: (i, 0)
    fixed = lambda i: (0, 0)
    return pl.pallas_call(
        _outproj_ln_kernel,
        grid=(n // tm,),
        in_specs=[pl.BlockSpec((tm, d), row), pl.BlockSpec((tm, ka), row), pl.BlockSpec((tm, kb), row),
                  pl.BlockSpec((ka, d), fixed), pl.BlockSpec((kb, d), fixed),
                  pl.BlockSpec((1, d), fixed), pl.BlockSpec((1, d), fixed)],
        out_specs=[pl.BlockSpec((tm, d), row), pl.BlockSpec((tm, d), row)],
        out_shape=[jax.ShapeDtypeStruct((n, d), F32), jax.ShapeDtypeStruct((n, d), BF16)],
        compiler_params=_cparams(("parallel",)),
        name="outproj_ln",
    )(x, oa, ob, wa, wb, g.reshape(1, d), b.reshape(1, d))


def _ffn_ln_kernel(x_ref, xb_ref, wg_ref, wu_ref, wd_ref, g_ref, b_ref, xo_ref, xob_ref, acc_ref):
    f = pl.program_id(1)
    xb = xb_ref[...]
    h = jax.nn.silu(jnp.dot(xb, wg_ref[...], preferred_element_type=F32)) * jnp.dot(
        xb, wu_ref[...], preferred_element_type=F32)
    contrib = jnp.dot(h.astype(BF16), wd_ref[...], preferred_element_type=F32)

    @pl.when(f == 0)
    def _():
        acc_ref[...] = contrib

    @pl.when(f > 0)
    def _():
        acc_ref[...] += contrib

    @pl.when(f == pl.num_programs(1) - 1)
    def _():
        y = _layer_norm(DN_ALPHA * x_ref[...] + acc_ref[...], g_ref[...], b_ref[...])
        xo_ref[...] = y
        xob_ref[...] = y.astype(BF16)


def _ffn_ln(x, xb, wg, wu, wd, g, b, tm=512, tf=1408):
    n, d = x.shape
    fdim = wg.shape[1]
    return pl.pallas_call(
        _ffn_ln_kernel,
        grid=(n // tm, fdim // tf),
        in_specs=[pl.BlockSpec((tm, d), lambda i, f: (i, 0)), pl.BlockSpec((tm, d), lambda i, f: (i, 0)),
                  pl.BlockSpec((d, tf), lambda i, f: (0, f)), pl.BlockSpec((d, tf), lambda i, f: (0, f)),
                  pl.BlockSpec((tf, d), lambda i, f: (f, 0)),
                  pl.BlockSpec((1, d), lambda i, f: (0, 0)), pl.BlockSpec((1, d), lambda i, f: (0, 0))],
        out_specs=[pl.BlockSpec((tm, d), lambda i, f: (i, 0)), pl.BlockSpec((tm, d), lambda i, f: (i, 0))],
        out_shape=[jax.ShapeDtypeStruct((n, d), F32), jax.ShapeDtypeStruct((n, d), BF16)],
        scratch_shapes=[pltpu.VMEM((tm, d), F32)],
        compiler_params=_cparams(("parallel", "arbitrary")),
        name="ffn_ln",
    )(x, xb, wg, wu, wd, g.reshape(1, d), b.reshape(1, d))


def _diff_kernel(lam_ref, sub_ref, q_ref, k_ref, v_ref, o_ref, *, tq, tk, lam_init):
    i = pl.program_id(2)
    lp = lam_ref[...]
    lam = (jnp.exp(jnp.sum(lp[0:1] * lp[1:2], axis=1, keepdims=True))
           - jnp.exp(jnp.sum(lp[2:3] * lp[3:4], axis=1, keepdims=True)) + lam_init)
    q0, q1 = _split_halves(q_ref[...])
    row = i * tq + lax.broadcasted_iota(I32, (tq, 1), 0)

    def body(c, carry):
        c0, c1 = carry
        off = pl.multiple_of(c * tk, tk)
        k = k_ref[pl.ds(off, tk), :]
        v = v_ref[pl.ds(off, tk), :]
        col = c * tk + lax.broadcasted_iota(I32, (1, tk), 1)
        bias = jnp.where(col <= row, 0.0, NINF)
        s0 = _dot_nt(q0, k) * SCALE + bias
        s1 = _dot_nt(q1, k) * SCALE + bias
        return _flash_step(c0, s0, v), _flash_step(c1, s1, v)

    n_blk = (i * tq + tq + tk - 1) // tk
    init = _flash_init(tq, LANES)
    c0, c1 = lax.fori_loop(0, n_blk, body, (init, init))
    o = _flash_out(c0) - lam * _flash_out(c1)
    o = o * lax.rsqrt(jnp.mean(o * o, axis=-1, keepdims=True) + LN_EPS)
    o = o * sub_ref[...] * (1.0 - lam_init)
    o_ref[...] = o.astype(o_ref.dtype)


def _diff_attention(rp, pp, lam_params, subln, lam_init, tq=256, tk=256):
    b, s, _ = rp.shape
    nq = A_HEADS
    return pl.pallas_call(
        functools.partial(_diff_kernel, tq=tq, tk=tk, lam_init=lam_init),
        grid=(b, A_HEADS, s // tq),
        in_specs=[pl.BlockSpec((4, HEAD_DIM), lambda bi, h, i: (0, 0)),
                  pl.BlockSpec((1, LANES), lambda bi, h, i: (0, 0)),
                  pl.BlockSpec((None, tq, LANES), lambda bi, h, i: (bi, i, h)),
                  pl.BlockSpec((None, s, LANES), lambda bi, h, i: (bi, 0, nq + h)),
                  pl.BlockSpec((None, s, LANES), lambda bi, h, i: (bi, 0, h))],
        out_specs=pl.BlockSpec((None, tq, LANES), lambda bi, h, i: (bi, i, h)),
        out_shape=jax.ShapeDtypeStruct((b, s, A_HEADS * LANES), BF16),
        compiler_params=_cparams(("parallel", "parallel", "arbitrary")),
        name="diff_attn",
    )(lam_params, subln.reshape(1, LANES), rp, rp, pp)


def _dsa_kernel(iq_ref, ikk_ref, iw_ref, q_ref, k_ref, v_ref, o_ref, s_ref, j_ref, *, tq, tc, ksel, seq_len):
    i = pl.program_id(1)
    q0 = i * tq
    nch = (q0 + tq + tc - 1) // tc
    row = q0 + lax.broadcasted_iota(I32, (tq, 1), 0)
    ksel_f = float(ksel)

    iq = iq_ref[...]
    iw = iw_ref[...]
    iqh = []
    for pair in range(IDX_HEADS // 2):
        iqh += list(_split_halves(iq[:, pair * LANES:(pair + 1) * LANES]))

    def col_of(c):
        return c * tc + lax.broadcasted_iota(I32, (1, tc), 1)

    def score_body(c, carry):
        smax, smin = carry
        off = pl.multiple_of(c * tc, tc)
        kk = ikk_ref[pl.ds(off, tc), :]
        sc = None
        for h in range(IDX_HEADS):
            term = iw[:, h:h + 1] * jnp.maximum(_dot_nt(iqh[h], kk), 0.0)
            sc = term if sc is None else sc + term
        sc = jnp.where(sc == 0.0, 0.0, sc)
        causal = col_of(c) <= row
        s_ref[c] = jnp.where(causal, sc, NEG)
        smax = jnp.maximum(smax, jnp.max(jnp.where(causal, sc, -BIG), axis=1, keepdims=True))
        smin = jnp.minimum(smin, jnp.min(jnp.where(causal, sc, BIG), axis=1, keepdims=True))
        return smax, smin

    smax, smin = lax.fori_loop(0, nch, score_body,
                               (jnp.full((tq, 1), -BIG, F32), jnp.full((tq, 1), BIG, F32)))

    def count(pred):
        def body(c, acc):
            return acc + jnp.sum(jnp.where(pred(s_ref[c], c), 1.0, 0.0), axis=1, keepdims=True)
        return lax.fori_loop(0, nch, body, jnp.zeros((tq, 1), F32))

    def count_ge(th):
        return count(lambda s, c: s >= th)

    def max_below(th):
        def body(c, acc):
            s = s_ref[c]
            return jnp.maximum(acc, jnp.max(jnp.where(s < th, s, NINF), axis=1, keepdims=True))
        return lax.fori_loop(0, nch, body, jnp.full((tq, 1), NINF, F32))

    n_causal = (row + 1).astype(F32)
    done0 = jnp.where((n_causal <= ksel_f) | (count_ge(smax) >= ksel_f), 1.0, 0.0)

    def bis_cond(carry):
        return carry[4] > 0.0

    def bis_body(carry):
        lo, hi, th, done, _ = carry
        mid = lo + (hi - lo) * 0.5
        ge = count_ge(mid) >= ksel_f
        lo = jnp.where(ge, mid, lo)
        hi = jnp.where(ge, hi, mid)
        t1 = max_below(hi)
        ok = count_ge(t1) >= ksel_f
        th = jnp.where(done > 0.0, th, t1)
        hi = jnp.where(ok, hi, t1)
        done = jnp.where(ok, 1.0, done)
        return lo, hi, th, done, jnp.sum(1.0 - done)

    _, _, th, _, _ = lax.while_loop(bis_cond, bis_body, (smin, smax, smax, done0, jnp.sum(1.0 - done0)))

    c_ge = count_ge(th)
    c_gt = count(lambda s, c: s > th)
    need = ksel_f - c_gt
    need_tb = (n_causal > ksel_f) & (c_ge > ksel_f)
    j_ref[...] = jnp.full((tq, LANES), seq_len - 1, I32)

    @pl.when(jnp.sum(jnp.where(need_tb, 1.0, 0.0)) > 0.0)
    def _():
        def jb(_, carry):
            lo, hi = carry
            mid = (lo + hi) // 2
            cnt = count(lambda s, c: (s == th) & (col_of(c) <= mid))
            ge = cnt >= need
            return jnp.where(ge, lo, mid), jnp.where(ge, mid, hi)
        n_it = int(math.ceil(math.log2(seq_len))) + 1
        _, hi = lax.fori_loop(0, n_it, jb, (jnp.full((tq, 1), -1, I32), jnp.full((tq, 1), seq_len - 1, I32)))
        j_ref[...] = jnp.broadcast_to(hi, (tq, LANES))

    jsel = j_ref[:, 0:1]
    take_all = n_causal <= ksel_f

    def bias_body(c, _):
        s = s_ref[c]
        col = col_of(c)
        sel = (col <= row) & (take_all | (s > th) | ((s == th) & (col <= jsel)))
        s_ref[c] = jnp.where(sel, 0.0, NINF)
        return 0

    lax.fori_loop(0, nch, bias_body, 0)

    lo_mask = _half_masks()
    for p in range(B_HEADS // 2):
        qa, qb = _split_halves(q_ref[:, p * LANES:(p + 1) * LANES])

        def att_body(c, carry, qa=qa, qb=qb):
            ca, cb = carry
            off = pl.multiple_of(c * tc, tc)
            k = k_ref[pl.ds(off, tc), :]
            v = v_ref[pl.ds(off, tc), :]
            bias = s_ref[c]
            sa = _dot_nt(qa, k) * SCALE + bias
            sb = _dot_nt(qb, k) * SCALE + bias
            return _flash_step(ca, sa, v), _flash_step(cb, sb, v)

        init = _flash_init(tq, LANES)
        ca, cb = lax.fori_loop(0, nch, att_body, (init, init))
        o = jnp.where(lo_mask, _flash_out(ca), _flash_out(cb))
        o_ref[:, p * LANES:(p + 1) * LANES] = o.astype(o_ref.dtype)


def _dsa_attention(rp, pp, iw, tq=128, tc=512):
    b, s, _ = rp.shape
    ksel = min(DSA_TOPK, s // 4)
    tc = min(tc, s)
    return pl.pallas_call(
        functools.partial(_dsa_kernel, tq=tq, tc=tc, ksel=ksel, seq_len=s),
        grid=(b, s // tq),
        in_specs=[pl.BlockSpec((None, tq, 2 * LANES), lambda bi, i: (bi, i, 6)),
                  pl.BlockSpec((None, s, LANES), lambda bi, i: (bi, 0, 15)),
                  pl.BlockSpec((None, tq, LANES), lambda bi, i: (bi, i, 0)),
                  pl.BlockSpec((None, tq, 4 * LANES), lambda bi, i: (bi, i, 2)),
                  pl.BlockSpec((None, s, LANES), lambda bi, i: (bi, 0, 14)),
                  pl.BlockSpec((None, s, LANES), lambda bi, i: (bi, 0, 4))],
        out_specs=pl.BlockSpec((None, tq, 4 * LANES), lambda bi, i: (bi, i, 0)),
        out_shape=jax.ShapeDtypeStruct((b, s, 4 * LANES), BF16),
        scratch_shapes=[pltpu.VMEM((s // tc, tq, tc), F32), pltpu.VMEM((tq, LANES), I32)],
        compiler_params=_cparams(("parallel", "arbitrary")),
        name="dsa_attn",
    )(rp, rp, iw, rp, rp, pp)


def _top_n_mask(v, n):
    lanef = lax.broadcasted_iota(I32, (1, LANES), 1).astype(F32)
    sel = jnp.zeros(v.shape, F32)
    for _ in range(n):
        mx = jnp.max(v, axis=1, keepdims=True)
        first = jnp.min(jnp.where(v == mx, lanef, float(LANES)), axis=1, keepdims=True)
        pick = lanef == first
        sel = jnp.where(pick, 1.0, sel)
        v = jnp.where(pick, NINF, v)
    return sel


def _moba_kernel(q_ref, k_ref, v_ref, o_ref, km_ref, *, seq_len, n_sel):
    tq = MOBA_BLOCK
    qb = pl.program_id(2)

    @pl.when(qb == 0)
    def _():
        j = lax.broadcasted_iota(I32, (LANES, seq_len), 0)
        s = lax.broadcasted_iota(I32, (LANES, seq_len), 1)
        avg = jnp.where(s // MOBA_BLOCK == j, 1.0 / MOBA_BLOCK, 0.0).astype(BF16)
        km_ref[...] = jnp.dot(avg, k_ref[...], preferred_element_type=F32)

    km = km_ref[...]
    lane = lax.broadcasted_iota(I32, (1, LANES), 1)
    row = lax.broadcasted_iota(I32, (tq, 1), 0)
    col = lax.broadcasted_iota(I32, (1, tq), 1)
    own_bias = jnp.where(col <= row, 0.0, NINF)
    past = lane < qb
    qa, qb_ = _split_halves(q_ref[...])
    sels = []
    for qh in (qa, qb_):
        gate = lax.dot_general(qh.astype(F32), km, (((1,), (1,)), ((), ())),
                               precision=lax.Precision.HIGHEST, preferred_element_type=F32)
        sel = _top_n_mask(jnp.where(past, gate, NEG), n_sel)
        sels.append(jnp.where(past, sel, 0.0))

    def block_scores(c, qh, bias):
        off = pl.multiple_of(c * tq, tq)
        return _dot_nt(qh, k_ref[pl.ds(off, tq), :]) * SCALE + bias, v_ref[pl.ds(off, tq), :]

    def body(c, carry):
        out = []
        for qh, sel, cr in zip((qa, qb_), sels, carry):
            chosen = jnp.sum(jnp.where(lane == c, sel, 0.0), axis=1, keepdims=True) > 0.0
            s, v = block_scores(c, qh, jnp.where(chosen, 0.0, NINF))
            out.append(_flash_step(cr, s, v))
        return tuple(out)

    init = _flash_init(tq, LANES)
    carry = lax.fori_loop(0, qb, body, (init, init))
    outs = []
    for qh, cr in zip((qa, qb_), carry):
        s, v = block_scores(qb, qh, own_bias)
        outs.append(_flash_out(_flash_step(cr, s, v)))
    o_ref[...] = jnp.where(_half_masks(), outs[0], outs[1]).astype(o_ref.dtype)


def _moba_attention(rp, pp):
    b, s, _ = rp.shape
    nb = s // MOBA_BLOCK
    n_sel = max(1, min(MOBA_TOPK, nb - 1))
    npair = C_HEADS // 2
    return pl.pallas_call(
        functools.partial(_moba_kernel, seq_len=s, n_sel=n_sel),
        grid=(b, npair, nb),
        in_specs=[pl.BlockSpec((None, MOBA_BLOCK, LANES), lambda bi, h, i: (bi, i, h)),
                  pl.BlockSpec((None, s, LANES), lambda bi, h, i: (bi, 0, npair + h)),
                  pl.BlockSpec((None, s, LANES), lambda bi, h, i: (bi, 0, h))],
        out_specs=pl.BlockSpec((None, MOBA_BLOCK, LANES), lambda bi, h, i: (bi, i, h)),
        out_shape=jax.ShapeDtypeStruct((b, s, npair * LANES), BF16),
        scratch_shapes=[pltpu.VMEM((LANES, LANES), F32)],
        compiler_params=_cparams(("parallel", "parallel", "arbitrary")),
        name="moba_attn",
    )(rp, rp, pp)


def _cmp_kernel(r_ref, pe_ref, w1_ref, w2_ref, o_ref):
    r = r_ref[...]
    w1 = w1_ref[...]
    half = r.shape[1]
    u = jnp.dot(r, w1[:half], preferred_element_type=F32)
    v = jnp.dot(r, w1[half:], preferred_element_type=F32)
    c = jnp.dot(pe_ref[...], w1, preferred_element_type=F32)[0:1]
    pre = u + pltpu.roll(v, r.shape[0] - 1, 0) + c
    o_ref[...] = jnp.dot(jax.nn.gelu(pre).astype(BF16), w2_ref[...],
                         preferred_element_type=F32).astype(o_ref.dtype)


def _nsa_compress(r, pe, w1, w2):
    b, _, nc, wdt = r.shape
    hid = w1.shape[2]
    return pl.pallas_call(
        _cmp_kernel,
        grid=(b, 4),
        in_specs=[pl.BlockSpec((None, None, nc, wdt), lambda bi, t: (bi, t, 0, 0)),
                  pl.BlockSpec((None, 8, 2 * wdt), lambda bi, t: (t // 2, 0, 0)),
                  pl.BlockSpec((None, 2 * wdt, hid), lambda bi, t: (t // 2, 0, 0)),
                  pl.BlockSpec((None, hid, HEAD_DIM), lambda bi, t: (t // 2, 0, 0))],
        out_specs=pl.BlockSpec((None, None, nc, HEAD_DIM), lambda bi, t: (bi, t, 0, 0)),
        out_shape=jax.ShapeDtypeStruct((b, 4, nc, HEAD_DIM), BF16),
        compiler_params=_cparams(("parallel", "arbitrary")),
        name="nsa_compress",
    )(r, pe, w1, w2)


def _nsa_kernel(qr_ref, qw_ref, dg_ref, gb_ref, kc_ref, vc_ref, ks_ref, vs_ref, kw_ref, vw_ref,
                o_ref, b_ref, *, tq, tc, tw, seq_len):
    i = pl.program_id(1)
    q0 = i * tq
    nch = (q0 + tq + tc - 1) // tc
    row = q0 + lax.broadcasted_iota(I32, (tq, 1), 0)
    nc = seq_len // NSA_CMP_STRIDE
    n_sb = seq_len // NSA_SLC_BLOCK
    n_sel = min(NSA_SLC_TOPK, n_sb)
    lo_mask = _half_masks()
    lane = lax.broadcasted_iota(I32, (1, LANES), 1)

    gates = jax.nn.sigmoid(dg_ref[...] + gb_ref[...])
    kc = kc_ref[...]
    vc = vc_ref[...]
    cmp_end = lax.broadcasted_iota(I32, (1, nc), 1) * NSA_CMP_STRIDE + (NSA_CMP_LEN - 1)
    cbias = jnp.where(cmp_end <= row, 0.0, NINF)

    q_rot, o_cmp = [], []
    psum = [jnp.zeros((tq, nc), F32), jnp.zeros((tq, nc), F32)]
    for p in range(D_HEADS // 2):
        q_rot.append(_split_halves(qr_ref[:, p * LANES:(p + 1) * LANES]))
        for g, qh in enumerate(_split_halves(qw_ref[:, p * LANES:(p + 1) * LANES])):
            s = _dot_nt(qh, kc) * SCALE + cbias
            m = jnp.max(s, axis=1, keepdims=True)
            e = jnp.exp(s - jnp.where(m == NINF, 0.0, m))
            l = jnp.sum(e, axis=1, keepdims=True)
            pc = e * jnp.where(l > 0.0, 1.0 / jnp.where(l > 0.0, l, 1.0), 0.0)
            psum[g] = psum[g] + pc
            o_cmp.append(jnp.dot(pc.astype(BF16), vc, preferred_element_type=F32))

    cn = lax.broadcasted_iota(I32, (nc, LANES), 0) * NSA_CMP_STRIDE
    sj = lax.broadcasted_iota(I32, (nc, LANES), 1) * NSA_SLC_BLOCK
    shares = jnp.where((cn <= sj + NSA_SLC_BLOCK - 1) & (cn + NSA_CMP_LEN - 1 >= sj), 1.0, 0.0)
    cur = row // NSA_SLC_BLOCK
    causal_b = lane <= cur
    forced = (lane == 0) | ((lane >= cur - 1) & causal_b)
    for g in range(2):
        imp = jnp.dot(psum[g], shares, precision=lax.Precision.HIGHEST, preferred_element_type=F32)
        val = jnp.where(forced, BIG, jnp.where(causal_b, imp, NEG))
        val = jnp.where(lane < n_sb, val, NINF)
        selb = _top_n_mask(val, n_sel).astype(BF16)

        def expand(c, _, selb=selb, g=g):
            col = c * tc + lax.broadcasted_iota(I32, (1, tc), 1)
            blk = (c * tc + lax.broadcasted_iota(I32, (LANES, tc), 1)) // NSA_SLC_BLOCK
            e = jnp.where(blk == lax.broadcasted_iota(I32, (LANES, tc), 0), 1.0, 0.0).astype(BF16)
            selk = jnp.dot(selb, e, preferred_element_type=F32)
            b_ref[g, c] = jnp.where((selk > 0.5) & (col <= row), 0.0, NINF)
            return 0

        lax.fori_loop(0, nch, expand, 0)

    w_lo = jnp.maximum(q0 - NSA_WINDOW, 0) // tw
    w_hi = (q0 + tq) // tw
    for p in range(D_HEADS // 2):
        outs = []
        for g in range(2):
            qh = q_rot[p][g]
            h = g * (D_HEADS // 2) + p

            def slc_body(c, carry, qh=qh, g=g):
                off = pl.multiple_of(c * tc, tc)
                s = _dot_nt(qh, ks_ref[pl.ds(off, tc), :]) * SCALE + b_ref[g, c]
                return _flash_step(carry, s, vs_ref[pl.ds(off, tc), :])

            o_slc = _flash_out(lax.fori_loop(0, nch, slc_body, _flash_init(tq, LANES)))

            def win_body(c, carry, qh=qh):
                off = pl.multiple_of(c * tw, tw)
                dist = row - (c * tw + lax.broadcasted_iota(I32, (1, tw), 1))
                bias = jnp.where((dist >= 0) & (dist < NSA_WINDOW), 0.0, NINF)
                s = _dot_nt(qh, kw_ref[pl.ds(off, tw), :]) * SCALE + bias
                return _flash_step(carry, s, vw_ref[pl.ds(off, tw), :])

            o_win = _flash_out(lax.fori_loop(w_lo, w_hi, win_body, _flash_init(tq, LANES)))
            outs.append(gates[:, 3 * h:3 * h + 1] * o_cmp[2 * p + g]
                        + gates[:, 3 * h + 1:3 * h + 2] * o_slc
                        + gates[:, 3 * h + 2:3 * h + 3] * o_win)
        o_ref[:, p * LANES:(p + 1) * LANES] = jnp.where(lo_mask, outs[0], outs[1]).astype(o_ref.dtype)


def _nsa_attention(rp, pp, dg, gate_b, kcmp, vcmp, tq=128, tc=512, tw=128):
    b, s, _ = rp.shape
    tc = min(tc, s)
    nc = s // NSA_CMP_STRIDE
    assert s // NSA_SLC_BLOCK <= LANES
    full = lambda t: pl.BlockSpec((None, s, LANES), lambda bi, i: (bi, 0, t))
    return pl.pallas_call(
        functools.partial(_nsa_kernel, tq=tq, tc=tc, tw=tw, seq_len=s),
        grid=(b, s // tq),
        in_specs=[pl.BlockSpec((None, tq, 4 * LANES), lambda bi, i: (bi, i, 2)),
                  pl.BlockSpec((None, tq, 4 * LANES), lambda bi, i: (bi, i, 1)),
                  pl.BlockSpec((None, tq, LANES), lambda bi, i: (bi, i, 0)),
                  pl.BlockSpec((1, LANES), lambda bi, i: (0, 0)),
                  pl.BlockSpec((None, nc, LANES), lambda bi, i: (bi, 0, 0)),
                  pl.BlockSpec((None, nc, LANES), lambda bi, i: (bi, 0, 0)),
                  full(12), pl.BlockSpec((None, s, LANES), lambda bi, i: (bi, 0, 10)),
                  full(13), pl.BlockSpec((None, s, LANES), lambda bi, i: (bi, 0, 11))],
        out_specs=pl.BlockSpec((None, tq, 4 * LANES), lambda bi, i: (bi, i, 0)),
        out_shape=jax.ShapeDtypeStruct((b, s, 4 * LANES), BF16),
        scratch_shapes=[pltpu.VMEM((2, s // tc, tq, tc), F32)],
        compiler_params=_cparams(("parallel", "arbitrary")),
        name="nsa_attn",
    )(rp, pp, dg, gate_b, kcmp, vcmp, rp, pp, rp, pp)


def _router_kernel(x_ref, w_ref, b_ref, o_ref):
    logits = jnp.dot(x_ref[...], w_ref[...], precision=lax.Precision.HIGHEST,
                     preferred_element_type=F32) + b_ref[...]
    lane = lax.broadcasted_iota(I32, (1, LANES), 1)
    lanef = lane.astype(F32)
    v = jnp.where(lane < N_EXPERTS, logits, NINF)
    l0 = jnp.max(v, axis=1, keepdims=True)
    i0 = jnp.min(jnp.where(v == l0, lanef, float(LANES)), axis=1, keepdims=True)
    v = jnp.where(lanef == i0, NINF, v)
    l1 = jnp.max(v, axis=1, keepdims=True)
    i1 = jnp.min(jnp.where(v == l1, lanef, float(LANES)), axis=1, keepdims=True)
    e1 = jnp.exp(l1 - l0)
    g0 = 1.0 / (1.0 + e1)
    g1 = e1 / (1.0 + e1)
    o_ref[...] = jnp.where(lane == 0, i0, jnp.where(lane == 1, i1, jnp.where(lane == 2, g0, jnp.where(
        lane == 3, g1, 0.0))))


def _router(x, w, b, tm=512):
    n, d = x.shape
    return pl.pallas_call(
        _router_kernel,
        grid=(n // tm,),
        in_specs=[pl.BlockSpec((tm, d), lambda i: (i, 0)), pl.BlockSpec((d, LANES), lambda i: (0, 0)),
                  pl.BlockSpec((1, LANES), lambda i: (0, 0))],
        out_specs=pl.BlockSpec((tm, LANES), lambda i: (i, 0)),
        out_shape=jax.ShapeDtypeStruct((n, LANES), F32),
        compiler_params=_cparams(("parallel",)),
        name="moe_router",
    )(x, w, b)


def _moe_ffn_kernel(be_ref, nu_ref, x_ref, wg_ref, wu_ref, wd_ref, o_ref, acc_ref):
    i = pl.program_id(0)
    f = pl.program_id(1)
    last = pl.num_programs(1) - 1
    used = i < nu_ref[0]

    @pl.when(used)
    def _():
        x = x_ref[...]
        h = jax.nn.silu(jnp.dot(x, wg_ref[...], preferred_element_type=F32)) * jnp.dot(
            x, wu_ref[...], preferred_element_type=F32)
        contrib = jnp.dot(h.astype(BF16), wd_ref[...], preferred_element_type=F32)

        @pl.when(f == 0)
        def _():
            acc_ref[...] = contrib

        @pl.when(f > 0)
        def _():
            acc_ref[...] += contrib

        @pl.when(f == last)
        def _():
            o_ref[...] = acc_ref[...]

    @pl.when(jnp.logical_not(used) & (f == last))
    def _():
        o_ref[...] = jnp.zeros(o_ref.shape, o_ref.dtype)


def _moe_ffn(x_sorted, block_e, n_used, wg, wu, wd, tf=896):
    ns, d = x_sorted.shape
    fdim = wg.shape[2]
    nf = fdim // tf
    n_blocks = ns // MOE_TM

    def row_map(i, f, be, nu):
        return (jnp.minimum(i, nu[0] - 1), 0)

    def f_of(i, f, nu):
        return jnp.where(i < nu[0], f, nf - 1)

    grid_spec = pltpu.PrefetchScalarGridSpec(
        num_scalar_prefetch=2,
        grid=(n_blocks, nf),
        in_specs=[pl.BlockSpec((MOE_TM, d), row_map),
                  pl.BlockSpec((None, d, tf), lambda i, f, be, nu: (be[i], 0, f_of(i, f, nu))),
                  pl.BlockSpec((None, d, tf), lambda i, f, be, nu: (be[i], 0, f_of(i, f, nu))),
                  pl.BlockSpec((None, tf, d), lambda i, f, be, nu: (be[i], f_of(i, f, nu), 0))],
        out_specs=pl.BlockSpec((MOE_TM, d), lambda i, f, be, nu: (i, 0)),
        scratch_shapes=[pltpu.VMEM((MOE_TM, d), F32)],
    )
    return pl.pallas_call(
        _moe_ffn_kernel,
        grid_spec=grid_spec,
        out_shape=jax.ShapeDtypeStruct((ns, d), F32),
        compiler_params=_cparams(("arbitrary", "arbitrary")),
        name="moe_ffn",
    )(block_e, n_used, x_sorted, wg, wu, wd)


def _combine_ln_kernel(x_ref, y0_ref, y1_ref, rt_ref, g_ref, b_ref, xo_ref, xb_ref):
    rt = rt_ref[...]
    ffn = rt[:, 2:3] * y0_ref[...] + rt[:, 3:4] * y1_ref[...]
    y = _layer_norm(DN_ALPHA * x_ref[...] + ffn, g_ref[...], b_ref[...])
    xo_ref[...] = y
    xb_ref[...] = y.astype(BF16)


def _combine_ln(x, y0, y1, rt, g, b, tm=512):
    n, d = x.shape
    row = lambda i: (i, 0)
    fixed = lambda i: (0, 0)
    return pl.pallas_call(
        _combine_ln_kernel,
        grid=(n // tm,),
        in_specs=[pl.BlockSpec((tm, d), row), pl.BlockSpec((tm, d), row), pl.BlockSpec((tm, d), row),
                  pl.BlockSpec((tm, LANES), row), pl.BlockSpec((1, d), fixed), pl.BlockSpec((1, d), fixed)],
        out_specs=[pl.BlockSpec((tm, d), row), pl.BlockSpec((tm, d), row)],
        out_shape=[jax.ShapeDtypeStruct((n, d), F32), jax.ShapeDtypeStruct((n, d), BF16)],
        compiler_params=_cparams(("parallel",)),
        name="moe_combine_ln",
    )(x, y0, y1, rt, g.reshape(1, d), b.reshape(1, d))


def _moe_layout(rt, n):
    e_flat = rt[:, 0:TOP_K].astype(I32).reshape(-1)
    nk = n * TOP_K
    onehot = (e_flat[:, None] == jnp.arange(N_EXPERTS, dtype=I32)[None, :]).astype(I32)
    rank = jnp.take_along_axis(jnp.cumsum(onehot, axis=0), e_flat[:, None], axis=1)[:, 0] - 1
    counts = jnp.sum(onehot, axis=0)
    padded = (counts + MOE_TM - 1) // MOE_TM * MOE_TM
    pad_end = jnp.cumsum(padded)
    pad_start = pad_end - padded
    grp_start = jnp.cumsum(counts) - counts
    slot = pad_start[e_flat] + rank
    n_blocks = -(-nk // MOE_TM) + N_EXPERTS
    n_slots = n_blocks * MOE_TM
    order = jnp.argsort(e_flat, stable=True).astype(I32)
    sl = jnp.arange(n_slots, dtype=I32)
    slot_e = jnp.minimum(jnp.searchsorted(pad_end, sl, side='right'), N_EXPERTS - 1).astype(I32)
    within = sl - pad_start[slot_e]
    valid = within < counts[slot_e]
    src = jnp.where(valid, grp_start[slot_e] + within, 0)
    slot_tok = jnp.where(valid, order[src] // TOP_K, 0)
    n_used = (pad_end[-1] // MOE_TM).astype(I32).reshape(1)
    blk = jnp.arange(n_blocks, dtype=I32)
    block_e = slot_e[jnp.minimum(blk, n_used[0] - 1) * MOE_TM]
    return slot_tok, slot.reshape(n, TOP_K), block_e, n_used


def _pair_perm(n_heads):
    half = n_heads // 2
    cols = []
    for p in range(half):
        cols += list(range(p * HEAD_DIM, (p + 1) * HEAD_DIM))
        cols += list(range((half + p) * HEAD_DIM, (half + p + 1) * HEAD_DIM))
    return np.asarray(cols, dtype=np.int32)


def _pad_cols(w, width):
    return jnp.pad(w, ((0, 0), (0, width - w.shape[1])))


def _even_layer(x, xb, w_in, w_out, lam_params, subln, lam_init, wg, wu, wd, ln, tabs, bsz, seq_len):
    n, d = x.shape
    perm = _pair_perm(B_HEADS)
    aq, ak, av = w_in[:, 0:512], w_in[:, 512:1024], w_in[:, 1024:1536]
    bq, bk, bv = w_in[:, 1536:2048], w_in[:, 2048:2176], w_in[:, 2176:2304]
    iq, ik, iw = w_in[:, 2304:2560], w_in[:, 2560:2624], w_in[:, 2624:2628]
    w_rope = jnp.concatenate([aq, ak, bq[:, perm], iq, bk, ik, ik], axis=1).astype(BF16)
    w_plain = jnp.concatenate([av, bv], axis=1).astype(BF16)
    w_iw = _pad_cols(iw, LANES).astype(BF16)
    rp = _matmul(xb, w_rope, BF16, 512, 1024, seq_len, tabs).reshape(bsz, seq_len, -1)
    pp = _matmul(xb, w_plain, BF16, 512, w_plain.shape[1]).reshape(bsz, seq_len, -1)
    iwv = _matmul(xb, w_iw, F32, 512, LANES).reshape(bsz, seq_len, LANES)
    o_a = _diff_attention(rp, pp, lam_params, subln, lam_init)
    o_b = _dsa_attention(rp, pp, iwv)
    wo_a = w_out[0:512].astype(BF16)
    wo_b = w_out[512:1024][perm].astype(BF16)
    g_mix, b_mix, g_ffn, b_ffn = ln
    x1, x1b = _outproj_ln(x, o_a.reshape(n, -1), o_b.reshape(n, -1), wo_a, wo_b, g_mix, b_mix)
    return _ffn_ln(x1, x1b, wg.astype(BF16), wu.astype(BF16), wd.astype(BF16), g_ffn, b_ffn)


def _odd_layer(x, xb, w_in, w_out, gate_b, pe, phi_w1, phi_w2, w_router, b_router, wg, wu, wd, ln, tabs,
               bsz, seq_len):
    n, d = x.shape
    perm = _pair_perm(D_HEADS)
    cq, ck, cv = w_in[:, 0:512], w_in[:, 512:1024], w_in[:, 1024:1536]
    dq = w_in[:, 1536:2048][:, perm]
    dkc, dvc, dks = w_in[:, 2048:2176], w_in[:, 2176:2304], w_in[:, 2304:2432]
    dvs, dkw, dvw = w_in[:, 2432:2560], w_in[:, 2560:2688], w_in[:, 2688:2816]
    dg = w_in[:, 2816:2840]
    w_rope = jnp.concatenate([cq, ck, dq, dks, dkw], axis=1).astype(BF16)
    w_plain = jnp.concatenate([cv, dq, dkc, dvc, dvs, dvw], axis=1).astype(BF16)
    w_dg = _pad_cols(dg, LANES).astype(BF16)
    rp = _matmul(xb, w_rope, BF16, 512, w_rope.shape[1] // 2, seq_len, tabs).reshape(bsz, seq_len, -1)
    pp = _matmul(xb, w_plain, BF16, 512, w_plain.shape[1] // 2).reshape(bsz, seq_len, -1)
    dgv = _matmul(xb, w_dg, F32, 512, LANES).reshape(bsz, seq_len, LANES)

    o_c = _moba_attention(rp, pp)

    nc = seq_len // NSA_CMP_STRIDE
    tok = pp[:, :, 8 * LANES:10 * LANES].reshape(bsz, nc, NSA_CMP_STRIDE, 4, HEAD_DIM)
    r = tok.transpose(0, 3, 1, 2, 4).reshape(bsz, 4, nc, NSA_CMP_STRIDE * HEAD_DIM)
    pe_flat = jnp.pad(pe.reshape(2, 1, -1), ((0, 0), (0, 7), (0, 0))).astype(BF16)
    cmp = _nsa_compress(r, pe_flat, phi_w1.astype(BF16), phi_w2.astype(BF16))
    kcmp = jnp.concatenate([cmp[:, 0], cmp[:, 1]], axis=-1)
    vcmp = jnp.concatenate([cmp[:, 2], cmp[:, 3]], axis=-1)
    gb = _pad_cols(gate_b.reshape(1, -1), LANES)
    o_d = _nsa_attention(rp, pp, dgv, gb, kcmp, vcmp)

    wo_c = w_out[0:512].astype(BF16)
    wo_d = w_out[512:1024][perm].astype(BF16)
    g_mix, b_mix, g_ffn, b_ffn = ln
    x1, x1b = _outproj_ln(x, o_c.reshape(n, -1), o_d.reshape(n, -1), wo_c, wo_d, g_mix, b_mix)

    rt = _router(x1, _pad_cols(w_router, LANES), _pad_cols(b_router.reshape(1, -1), LANES))
    slot_tok, slot, block_e, n_used = _moe_layout(rt, n)
    y_slots = _moe_ffn(x1b[slot_tok], block_e, n_used, wg.astype(BF16), wu.astype(BF16), wd.astype(BF16))
    return _combine_ln(x1, y_slots[slot[:, 0]], y_slots[slot[:, 1]], rt, g_ffn, b_ffn)


@jax.jit
def kernel(x, ev_w_in, ev_w_out, dif_lambda, dif_subln, ffd_w_gate, ffd_w_up, ffd_w_down, od_w_in, od_w_out,
           nsa_gate_b, nsa_pe, nsa_phi_w1, nsa_phi_w2, moe_w_router, moe_b_router, moe_w_gate, moe_w_up,
           moe_w_down, ln_mix_g, ln_mix_b, ln_ffn_g, ln_ffn_b):
    bsz, seq_len, d = x.shape
    tabs = _rope_tables(seq_len)
    xf = x.reshape(bsz * seq_len, d)
    xb = xf.astype(BF16)
    for l in range(DEPTH):
        i = l // 2
        ln = (ln_mix_g[l], ln_mix_b[l], ln_ffn_g[l], ln_ffn_b[l])
        if l % 2 == 0:
            lam_init = 0.8 - 0.6 * math.exp(-0.3 * l)
            xf, xb = _even_layer(xf, xb, ev_w_in[i], ev_w_out[i], dif_lambda[i], dif_subln[i], lam_init,
                                 ffd_w_gate[i], ffd_w_up[i], ffd_w_down[i], ln, tabs, bsz, seq_len)
        else:
            xf, xb = _odd_layer(xf, xb, od_w_in[i], od_w_out[i], nsa_gate_b[i], nsa_pe[i], nsa_phi_w1[i],
                                nsa_phi_w2[i], moe_w_router[i], moe_b_router[i], moe_w_gate[i], moe_w_up[i],
                                moe_w_down[i], ln, tabs, bsz, seq_len)
    return xf.reshape(bsz, seq_len, d)
```

```python
import functools
import math

import numpy as np
import jax
import jax.numpy as jnp
from jax import lax
from jax.experimental import pallas as pl
from jax.experimental.pallas import tpu as pltpu

F32 = jnp.float32
BF16 = jnp.bfloat16
I32 = jnp.int32

LANES = 128
VMEM_LIMIT = 56 * 1024 * 1024

DEPTH = 4
HEAD_DIM = 64
ROPE_THETA = 10000.0
LN_EPS = 1e-5
DN_ALPHA = (2 * DEPTH) ** 0.25
SCALE = HEAD_DIM ** -0.5 * math.log2(math.e)
NEG = -1e30
BIG = 1e30
M_INIT = -1e30
NINF = float("-inf")

A_HEADS = 4
B_HEADS = 8
IDX_HEADS = 4
DSA_TOPK = 256
C_HEADS = 8
MOBA_BLOCK = 256
MOBA_TOPK = 3
D_HEADS = 8
NSA_CMP_LEN = 32
NSA_CMP_STRIDE = 16
NSA_SLC_BLOCK = 64
NSA_SLC_TOPK = 16
NSA_WINDOW = 512
NSA_PHI_HIDDEN = 256
N_EXPERTS = 8
TOP_K = 2
MOE_TM = 512
N_BISECT = 12


def _cparams(sem):
    return pltpu.CompilerParams(dimension_semantics=sem, vmem_limit_bytes=VMEM_LIMIT)


def _dot_nt(a, b):
    return lax.dot_general(a, b, (((1,), (1,)), ((), ())), preferred_element_type=F32)


def _layer_norm(y, g, b):
    mu = jnp.mean(y, axis=-1, keepdims=True)
    yc = y - mu
    var = jnp.mean(yc * yc, axis=-1, keepdims=True)
    return yc * lax.rsqrt(var + LN_EPS) * g + b


def _flash_step(carry, s, v):
    m, l, acc = carry
    m_new = jnp.maximum(m, jnp.max(s, axis=1, keepdims=True))
    alpha = jnp.exp2(m - m_new)
    p = jnp.exp2(s - m_new)
    l = alpha * l + jnp.sum(p, axis=1, keepdims=True)
    acc = alpha * acc + jnp.dot(p.astype(BF16), v, preferred_element_type=F32)
    return m_new, l, acc


def _flash_init(tq, dv):
    return (jnp.full((tq, 1), M_INIT, F32), jnp.zeros((tq, 1), F32), jnp.zeros((tq, dv), F32))


def _safe_recip(l):
    return jnp.where(l > 0.0, 1.0 / jnp.where(l > 0.0, l, 1.0), 0.0)


def _flash_out(carry):
    _, l, acc = carry
    return acc * _safe_recip(l)


def _half_masks():
    lane = lax.broadcasted_iota(I32, (1, LANES), 1)
    return lane < HEAD_DIM


def _split_halves(t):
    lo = _half_masks()
    z = jnp.zeros_like(t)
    return jnp.where(lo, t, z), jnp.where(lo, z, t)


def _mm_kernel(x_ref, w_ref, *rest, rope):
    o_ref = rest[-1]
    acc = jnp.dot(x_ref[...].astype(BF16), w_ref[...], preferred_element_type=F32)
    if not rope:
        o_ref[...] = acc.astype(o_ref.dtype)
        return
    cos = rest[0][...]
    sin = rest[1][...]
    lane = lax.broadcasted_iota(I32, (1, LANES), 1)
    first = (lane % HEAD_DIM) < (HEAD_DIM // 2)
    for c in range(acc.shape[1] // LANES):
        a = acc[:, c * LANES:(c + 1) * LANES]
        rot = jnp.where(first, pltpu.roll(a, LANES - HEAD_DIM // 2, 1), pltpu.roll(a, HEAD_DIM // 2, 1))
        o_ref[:, c * LANES:(c + 1) * LANES] = (a * cos + rot * sin).astype(o_ref.dtype)


def _matmul(x, w, out_dtype, tm, tn, seq_len=None, rope_tabs=None):
    n, d = x.shape
    p = w.shape[1]
    rope = rope_tabs is not None
    in_specs = [pl.BlockSpec((tm, d), lambda i, j: (i, 0)),
                pl.BlockSpec((d, tn), lambda i, j: (0, j))]
    args = [x, w]
    if rope:
        nt = seq_len // tm
        in_specs += [pl.BlockSpec((tm, LANES), lambda i, j: (i % nt, 0))] * 2
        args += list(rope_tabs)
    return pl.pallas_call(
        functools.partial(_mm_kernel, rope=rope),
        grid=(n // tm, p // tn),
        in_specs=in_specs,
        out_specs=pl.BlockSpec((tm, tn), lambda i, j: (i, j)),
        out_shape=jax.ShapeDtypeStruct((n, p), out_dtype),
        compiler_params=_cparams(("parallel", "arbitrary")),
        name="proj_rope" if rope else "proj",
    )(*args)


def _rope_tables(seq_len):
    d = HEAD_DIM
    inv = ROPE_THETA ** (-jnp.arange(0, d, 2, dtype=F32) / d)
    ang = jnp.arange(seq_len, dtype=I32).astype(F32)[:, None] * inv[None, :]
    cos = jnp.cos(ang)
    sin = jnp.sin(ang)
    cos128 = jnp.tile(cos, (1, LANES // (d // 2)))
    sin128 = jnp.tile(jnp.concatenate([-sin, sin], axis=1), (1, LANES // d))
    return cos128, sin128


def _outproj_ln_kernel(x_ref, a_ref, b_ref, wa_ref, wb_ref, g_ref, bb_ref, xo_ref, xb_ref):
    mix = (jnp.dot(a_ref[...], wa_ref[...], preferred_element_type=F32)
           + jnp.dot(b_ref[...], wb_ref[...], preferred_element_type=F32))
    y = _layer_norm(DN_ALPHA * x_ref[...] + mix, g_ref[...], bb_ref[...])
    xo_ref[...] = y
    xb_ref[...] = y.astype(BF16)


def _outproj_ln(x, oa, ob, wa, wb, g, b, tm=512):
    n, d = x.shape
    ka, kb = oa.shape[1], ob.shape[1]
    row = lambda i: (i, 0)
    fixed = lambda i: (0, 0)
    return pl.pallas_call(
        _outproj_ln_kernel,
        grid=(n // tm,),
        in_specs=[pl.BlockSpec((tm, d), row), pl.BlockSpec((tm, ka), row), pl.BlockSpec((tm, kb), row),
                  pl.BlockSpec((ka, d), fixed), pl.BlockSpec((kb, d), fixed),
                  pl.BlockSpec((1, d), fixed), pl.BlockSpec((1, d), fixed)],
        out_specs=[pl.BlockSpec((tm, d), row), pl.BlockSpec((tm, d), row)],
        out_shape=[jax.ShapeDtypeStruct((n, d), F32), jax.ShapeDtypeStruct((n, d), BF16)],
        compiler_params=_cparams(("parallel",)),
        name="outproj_ln",
    )(x, oa, ob, wa, wb, g.reshape(1, d), b.reshape(1, d))


def _ffn_ln_kernel(x_ref, xb_ref, wg_ref, wu_ref, wd_ref, g_ref, b_ref, xo_ref, xob_ref, acc_ref):
    f = pl.program_id(1)
    xb = xb_ref[...]
    h = jax.nn.silu(jnp.dot(xb, wg_ref[...], preferred_element_type=F32)) * jnp.dot(
        xb, wu_ref[...], preferred_element_type=F32)
    contrib = jnp.dot(h.astype(BF16), wd_ref[...], preferred_element_type=F32)

    @pl.when(f == 0)
    def _():
        acc_ref[...] = contrib

    @pl.when(f > 0)
    def _():
        acc_ref[...] += contrib

    @pl.when(f == pl.num_programs(1) - 1)
    def _():
        y = _layer_norm(DN_ALPHA * x_ref[...] + acc_ref[...], g_ref[...], b_ref[...])
        xo_ref[...] = y
        xob_ref[...] = y.astype(BF16)


def _ffn_ln(x, xb, wg, wu, wd, g, b, tm=512, tf=1408):
    n, d = x.shape
    fdim = wg.shape[1]
    return pl.pallas_call(
        _ffn_ln_kernel,
        grid=(n // tm, fdim // tf),
        in_specs=[pl.BlockSpec((tm, d), lambda i, f: (i, 0)), pl.BlockSpec((tm, d), lambda i, f: (i, 0)),
                  pl.BlockSpec((d, tf), lambda i, f: (0, f)), pl.BlockSpec((d, tf), lambda i, f: (0, f)),
                  pl.BlockSpec((tf, d), lambda i, f: (f, 0)),
                  pl.BlockSpec((1, d), lambda i, f: (0, 0)), pl.BlockSpec((1, d), lambda i, f: (0, 0))],
        out_specs=[pl.BlockSpec((tm, d), lambda i, f: (i, 0)), pl.BlockSpec((tm, d), lambda i, f: (i, 0))],
        out_shape=[jax.ShapeDtypeStruct((n, d), F32), jax.ShapeDtypeStruct((n, d), BF16)],
        scratch_shapes=[pltpu.VMEM((tm, d), F32)],
        compiler_params=_cparams(("parallel", "arbitrary")),
        name="ffn_ln",
    )(x, xb, wg, wu, wd, g.reshape(1, d), b.reshape(1, d))


def _diff_kernel(lam_ref, sub_ref, q_ref, k_ref, v_ref, o_ref, *, tq, lam_init):
    i = pl.program_id(2)
    lp = lam_ref[...]
    lam = (jnp.exp(jnp.sum(lp[0:1] * lp[1:2], axis=1, keepdims=True))
           - jnp.exp(jnp.sum(lp[2:3] * lp[3:4], axis=1, keepdims=True)) + lam_init)
    q0, q1 = _split_halves(q_ref[...])

    def step(c, carry, bias):
        c0, c1 = carry
        off = pl.multiple_of(c * tq, tq)
        k = k_ref[pl.ds(off, tq), :]
        v = v_ref[pl.ds(off, tq), :]
        s0 = _dot_nt(q0, k) * SCALE
        s1 = _dot_nt(q1, k) * SCALE
        if bias is not None:
            s0, s1 = s0 + bias, s1 + bias
        return _flash_step(c0, s0, v), _flash_step(c1, s1, v)

    init = _flash_init(tq, LANES)
    carry = lax.fori_loop(0, i, lambda c, cr: step(c, cr, None), (init, init))
    diag = jnp.where(lax.broadcasted_iota(I32, (1, tq), 1) <= lax.broadcasted_iota(I32, (tq, 1), 0), 0.0, NINF)
    c0, c1 = step(i, carry, diag)
    o = _flash_out(c0) - lam * _flash_out(c1)
    o = o * lax.rsqrt(jnp.mean(o * o, axis=-1, keepdims=True) + LN_EPS)
    o = o * sub_ref[...] * (1.0 - lam_init)
    o_ref[...] = o.astype(o_ref.dtype)


def _diff_attention(rp, pp, lam_params, subln, lam_init, tq=256):
    b, s, _ = rp.shape
    nq = A_HEADS
    return pl.pallas_call(
        functools.partial(_diff_kernel, tq=tq, lam_init=lam_init),
        grid=(b, A_HEADS, s // tq),
        in_specs=[pl.BlockSpec((4, HEAD_DIM), lambda bi, h, i: (0, 0)),
                  pl.BlockSpec((1, LANES), lambda bi, h, i: (0, 0)),
                  pl.BlockSpec((None, tq, LANES), lambda bi, h, i: (bi, i, h)),
                  pl.BlockSpec((None, s, LANES), lambda bi, h, i: (bi, 0, nq + h)),
                  pl.BlockSpec((None, s, LANES), lambda bi, h, i: (bi, 0, h))],
        out_specs=pl.BlockSpec((None, tq, LANES), lambda bi, h, i: (bi, i, h)),
        out_shape=jax.ShapeDtypeStruct((b, s, A_HEADS * LANES), BF16),
        compiler_params=_cparams(("parallel", "parallel", "arbitrary")),
        name="diff_attn",
    )(lam_params, subln.reshape(1, LANES), rp, rp, pp)


def _dsa_kernel(iq_ref, ikk_ref, iw_ref, q_ref, k_ref, v_ref, o_ref, s_ref, j_ref, *, tq, ksel, seq_len):
    tc = tq
    i = pl.program_id(1)
    nch = i + 1
    nlt = tc // LANES
    ksel_f = float(ksel)
    lane_i = lax.broadcasted_iota(I32, (1, LANES), 1)
    r_loc = lax.broadcasted_iota(I32, (tq, LANES), 0)
    rowb = i * tq + r_loc

    def rep(x):
        return jnp.broadcast_to(x, (tq, LANES))

    iq = iq_ref[...]
    iw = iw_ref[...]
    iqh = []
    for pair in range(IDX_HEADS // 2):
        iqh += list(_split_halves(iq[:, pair * LANES:(pair + 1) * LANES]))
    iwb = [rep(iw[:, h:h + 1]) for h in range(IDX_HEADS)]

    def scores(c):
        off = pl.multiple_of(c * tc, tc)
        kk = ikk_ref[pl.ds(off, tc), :]
        lg = [jnp.maximum(_dot_nt(iqh[h], kk), 0.0) for h in range(IDX_HEADS)]
        out = []
        for t in range(nlt):
            sc = iwb[0] * lg[0][:, t * LANES:(t + 1) * LANES]
            for h in range(1, IDX_HEADS):
                sc = sc + iwb[h] * lg[h][:, t * LANES:(t + 1) * LANES]
            out.append(sc)
        return out

    def full_body(c, carry):
        mx, mn = carry
        for t, sc in enumerate(scores(c)):
            s_ref[c, :, t * LANES:(t + 1) * LANES] = sc
            mx = jnp.maximum(mx, sc)
            mn = jnp.minimum(mn, sc)
        return mx, mn

    mx, mn = lax.fori_loop(0, i, full_body, (jnp.full((tq, LANES), -BIG, F32), jnp.full((tq, LANES), BIG, F32)))
    for t, sc in enumerate(scores(i)):
        causal = (lane_i + t * LANES) <= r_loc
        s_ref[i, :, t * LANES:(t + 1) * LANES] = jnp.where(causal, sc, NEG)
        mx = jnp.maximum(mx, jnp.where(causal, sc, -BIG))
        mn = jnp.minimum(mn, jnp.where(causal, sc, BIG))
    smax = rep(jnp.max(mx, axis=1, keepdims=True))
    smin = rep(jnp.min(mn, axis=1, keepdims=True))

    def count_where(ind):
        def body(c, acc):
            for t in range(nlt):
                acc = acc + ind(s_ref[c, :, t * LANES:(t + 1) * LANES], c * tc + t * LANES + lane_i)
            return acc
        acc = lax.fori_loop(0, nch, body, jnp.zeros((tq, LANES), F32))
        return rep(jnp.sum(acc, axis=1, keepdims=True))

    def count_ge(th):
        return count_where(lambda x, col: jnp.where(x >= th, 1.0, 0.0))

    def max_below(th):
        def body(c, acc):
            for t in range(nlt):
                x = s_ref[c, :, t * LANES:(t + 1) * LANES]
                acc = jnp.maximum(acc, jnp.where(x < th, x, NINF))
            return acc
        acc = lax.fori_loop(0, nch, body, jnp.full((tq, LANES), NINF, F32))
        return rep(jnp.max(acc, axis=1, keepdims=True))

    n_causal = (rowb + 1).astype(F32)
    take_all = n_causal <= ksel_f
    done0 = jnp.where(take_all | (count_ge(smax) >= ksel_f), 1.0, 0.0)

    def bisect(lo, hi):
        mid = lo + (hi - lo) * 0.5
        ge = count_ge(mid) >= ksel_f
        return jnp.where(ge, mid, lo), jnp.where(ge, hi, mid)

    lo, hi = lax.fori_loop(0, N_BISECT, lambda _, c: bisect(*c), (smin, smax))

    def snap_body(carry):
        lo, hi, th, done, _ = carry
        lo, hi = bisect(lo, hi)
        t1 = max_below(hi)
        ok = count_ge(t1) >= ksel_f
        th = jnp.where(done > 0.0, th, t1)
        hi = jnp.where(ok, hi, t1)
        done = jnp.where(ok, 1.0, done)
        return lo, hi, th, done, jnp.sum(1.0 - done)

    _, _, th, _, _ = lax.while_loop(lambda c: c[4] > 0.0, snap_body,
                                    (lo, hi, smax, done0, jnp.sum(1.0 - done0)))

    c_ge = count_ge(th)
    need_tb = jnp.where(take_all, 0.0, jnp.where(c_ge > ksel_f, 1.0, 0.0))
    j_ref[...] = jnp.full((tq, LANES), seq_len - 1, I32)

    @pl.when(jnp.sum(need_tb) > 0.0)
    def _():
        need = ksel_f - count_where(lambda x, col: jnp.where(x > th, 1.0, 0.0))

        def jb(_, carry):
            lo_j, hi_j = carry
            mid = (lo_j + hi_j) // 2
            cnt = count_where(lambda x, col: jnp.where(x == th, jnp.where(col <= mid, 1.0, 0.0), 0.0))
            ge = cnt >= need
            return jnp.where(ge, lo_j, mid), jnp.where(ge, mid, hi_j)

        n_it = int(math.ceil(math.log2(seq_len))) + 1
        _, hi_j = lax.fori_loop(0, n_it, jb, (jnp.full((tq, LANES), -1, I32),
                                              jnp.full((tq, LANES), seq_len - 1, I32)))
        j_ref[...] = hi_j

    jsel = j_ref[...]

    def bias_body(c, _):
        for t in range(nlt):
            x = s_ref[c, :, t * LANES:(t + 1) * LANES]
            col = c * tc + t * LANES + lane_i
            keep = jnp.where(x > th, 0.0, jnp.where(x == th, jnp.where(col <= jsel, 0.0, NINF), NINF))
            keep = jnp.where(take_all, 0.0, keep)
            s_ref[c, :, t * LANES:(t + 1) * LANES] = jnp.where(col <= rowb, keep, NINF)
        return 0

    lax.fori_loop(0, nch, bias_body, 0)

    lo_mask = _half_masks()
    for p in range(B_HEADS // 2):
        qa, qb = _split_halves(q_ref[:, p * LANES:(p + 1) * LANES])

        def att_body(c, carry, qa=qa, qb=qb):
            ca, cb = carry
            off = pl.multiple_of(c * tc, tc)
            k = k_ref[pl.ds(off, tc), :]
            v = v_ref[pl.ds(off, tc), :]
            bias = s_ref[c]
            sa = _dot_nt(qa, k) * SCALE + bias
            sb = _dot_nt(qb, k) * SCALE + bias
            return _flash_step(ca, sa, v), _flash_step(cb, sb, v)

        init = _flash_init(tq, LANES)
        ca, cb = lax.fori_loop(0, nch, att_body, (init, init))
        o = jnp.where(lo_mask, _flash_out(ca), _flash_out(cb))
        o_ref[:, p * LANES:(p + 1) * LANES] = o.astype(o_ref.dtype)


def _dsa_attention(rp, pp, iw, tq=256):
    b, s, _ = rp.shape
    ksel = min(DSA_TOPK, s // 4)
    return pl.pallas_call(
        functools.partial(_dsa_kernel, tq=tq, ksel=ksel, seq_len=s),
        grid=(b, s // tq),
        in_specs=[pl.BlockSpec((None, tq, 2 * LANES), lambda bi, i: (bi, i, 6)),
                  pl.BlockSpec((None, s, LANES), lambda bi, i: (bi, 0, 15)),
                  pl.BlockSpec((None, tq, LANES), lambda bi, i: (bi, i, 0)),
                  pl.BlockSpec((None, tq, 4 * LANES), lambda bi, i: (bi, i, 2)),
                  pl.BlockSpec((None, s, LANES), lambda bi, i: (bi, 0, 14)),
                  pl.BlockSpec((None, s, LANES), lambda bi, i: (bi, 0, 4))],
        out_specs=pl.BlockSpec((None, tq, 4 * LANES), lambda bi, i: (bi, i, 0)),
        out_shape=jax.ShapeDtypeStruct((b, s, 4 * LANES), BF16),
        scratch_shapes=[pltpu.VMEM((s // tq, tq, tq), F32), pltpu.VMEM((tq, LANES), I32)],
        compiler_params=_cparams(("parallel", "arbitrary")),
        name="dsa_attn",
    )(rp, rp, iw, rp, rp, pp)


def _top_n_mask(v, n, axis):
    idx = lax.broadcasted_iota(I32, v.shape, axis).astype(F32)
    sel = jnp.zeros(v.shape, F32)
    for _ in range(n):
        mx = jnp.max(v, axis=axis, keepdims=True)
        first = jnp.min(jnp.where(v == mx, idx, float(v.shape[axis])), axis=axis, keepdims=True)
        pick = idx == first
        sel = jnp.where(pick, 1.0, sel)
        v = jnp.where(pick, NINF, v)
    return sel


def _moba_kernel(q_ref, k_ref, v_ref, o_ref, km_ref, *, seq_len, n_sel):
    tq = MOBA_BLOCK
    qb = pl.program_id(2)

    @pl.when(qb == 0)
    def _():
        j = lax.broadcasted_iota(I32, (LANES, seq_len), 0)
        s = lax.broadcasted_iota(I32, (LANES, seq_len), 1)
        avg = jnp.where(s // MOBA_BLOCK == j, 1.0 / MOBA_BLOCK, 0.0).astype(BF16)
        km_ref[...] = jnp.dot(avg, k_ref[...], preferred_element_type=F32)

    km = km_ref[...]
    lane = lax.broadcasted_iota(I32, (1, LANES), 1)
    row = lax.broadcasted_iota(I32, (tq, 1), 0)
    col = lax.broadcasted_iota(I32, (1, tq), 1)
    own_bias = jnp.where(col <= row, 0.0, NINF)
    past = lane < qb
    qa, qb_ = _split_halves(q_ref[...])
    sels = []
    for qh in (qa, qb_):
        gate = lax.dot_general(qh.astype(F32), km, (((1,), (1,)), ((), ())),
                               precision=lax.Precision.HIGHEST, preferred_element_type=F32)
        sel = _top_n_mask(jnp.where(past, gate, NEG), n_sel, 1)
        sels.append(jnp.where(past, sel, 0.0))

    def block_scores(c, qh, bias):
        off = pl.multiple_of(c * tq, tq)
        return _dot_nt(qh, k_ref[pl.ds(off, tq), :]) * SCALE + bias, v_ref[pl.ds(off, tq), :]

    def body(c, carry):
        out = []
        for qh, sel, cr in zip((qa, qb_), sels, carry):
            chosen = jnp.sum(jnp.where(lane == c, sel, 0.0), axis=1, keepdims=True) > 0.0
            s, v = block_scores(c, qh, jnp.where(chosen, 0.0, NINF))
            out.append(_flash_step(cr, s, v))
        return tuple(out)

    init = _flash_init(tq, LANES)
    carry = lax.fori_loop(0, qb, body, (init, init))
    outs = []
    for qh, cr in zip((qa, qb_), carry):
        s, v = block_scores(qb, qh, own_bias)
        outs.append(_flash_out(_flash_step(cr, s, v)))
    o_ref[...] = jnp.where(_half_masks(), outs[0], outs[1]).astype(o_ref.dtype)


def _moba_attention(rp, pp):
    b, s, _ = rp.shape
    nb = s // MOBA_BLOCK
    n_sel = max(1, min(MOBA_TOPK, nb - 1))
    npair = C_HEADS // 2
    return pl.pallas_call(
        functools.partial(_moba_kernel, seq_len=s, n_sel=n_sel),
        grid=(b, npair, nb),
        in_specs=[pl.BlockSpec((None, MOBA_BLOCK, LANES), lambda bi, h, i: (bi, i, h)),
                  pl.BlockSpec((None, s, LANES), lambda bi, h, i: (bi, 0, npair + h)),
                  pl.BlockSpec((None, s, LANES), lambda bi, h, i: (bi, 0, h))],
        out_specs=pl.BlockSpec((None, MOBA_BLOCK, LANES), lambda bi, h, i: (bi, i, h)),
        out_shape=jax.ShapeDtypeStruct((b, s, npair * LANES), BF16),
        scratch_shapes=[pltpu.VMEM((LANES, LANES), F32)],
        compiler_params=_cparams(("parallel", "parallel", "arbitrary")),
        name="moba_attn",
    )(rp, rp, pp)


def _cmp_kernel(r_ref, pe_ref, w1_ref, w2_ref, o_ref):
    r = r_ref[...]
    w1 = w1_ref[...]
    half = r.shape[1]
    u = jnp.dot(r, w1[:half], preferred_element_type=F32)
    v = jnp.dot(r, w1[half:], preferred_element_type=F32)
    c = jnp.dot(pe_ref[...], w1, preferred_element_type=F32)[0:1]
    pre = u + pltpu.roll(v, r.shape[0] - 1, 0) + c
    o_ref[...] = jnp.dot(jax.nn.gelu(pre).astype(BF16), w2_ref[...],
                         preferred_element_type=F32).astype(o_ref.dtype)


def _nsa_compress(r, pe, w1, w2):
    b, _, nc, wdt = r.shape
    hid = w1.shape[2]
    return pl.pallas_call(
        _cmp_kernel,
        grid=(b, 4),
        in_specs=[pl.BlockSpec((None, None, nc, wdt), lambda bi, t: (bi, t, 0, 0)),
                  pl.BlockSpec((None, 8, 2 * wdt), lambda bi, t: (t // 2, 0, 0)),
                  pl.BlockSpec((None, 2 * wdt, hid), lambda bi, t: (t // 2, 0, 0)),
                  pl.BlockSpec((None, hid, HEAD_DIM), lambda bi, t: (t // 2, 0, 0))],
        out_specs=pl.BlockSpec((None, None, nc, HEAD_DIM), lambda bi, t: (bi, t, 0, 0)),
        out_shape=jax.ShapeDtypeStruct((b, 4, nc, HEAD_DIM), BF16),
        compiler_params=_cparams(("parallel", "arbitrary")),
        name="nsa_compress",
    )(r, pe, w1, w2)


def _nsa_kernel(qr_ref, qw_ref, dg_ref, gb_ref, kc_ref, vc_ref, ks_ref, vs_ref, kw_ref, vw_ref,
                o_ref, b_ref, *, tq, tc, wlen, seq_len):
    i = pl.program_id(1)
    q0 = i * tq
    nch = (q0 + tq + tc - 1) // tc
    row = q0 + lax.broadcasted_iota(I32, (tq, 1), 0)
    nc = seq_len // NSA_CMP_STRIDE
    n_sb = seq_len // NSA_SLC_BLOCK
    n_sel = min(NSA_SLC_TOPK, n_sb)
    lo_mask = _half_masks()

    gates = jax.nn.sigmoid(dg_ref[...] + gb_ref[...])
    kc = kc_ref[...]
    vc = vc_ref[...]
    cmp_end = lax.broadcasted_iota(I32, (1, nc), 1) * NSA_CMP_STRIDE + (NSA_CMP_LEN - 1)
    cbias = jnp.where(cmp_end <= row, 0.0, NINF)

    q_rot, o_cmp = [], []
    psum = [jnp.zeros((tq, nc), F32), jnp.zeros((tq, nc), F32)]
    for p in range(D_HEADS // 2):
        q_rot.append(_split_halves(qr_ref[:, p * LANES:(p + 1) * LANES]))
        for g, qh in enumerate(_split_halves(qw_ref[:, p * LANES:(p + 1) * LANES])):
            s = _dot_nt(qh, kc) * SCALE + cbias
            m = jnp.max(s, axis=1, keepdims=True)
            e = jnp.exp2(s - jnp.where(m == NINF, 0.0, m))
            pc = e * _safe_recip(jnp.sum(e, axis=1, keepdims=True))
            psum[g] = psum[g] + pc
            o_cmp.append(jnp.dot(pc.astype(BF16), vc, preferred_element_type=F32))

    cn = lax.broadcasted_iota(I32, (LANES, nc), 1) * NSA_CMP_STRIDE
    sj = lax.broadcasted_iota(I32, (LANES, nc), 0) * NSA_SLC_BLOCK
    shares_t = jnp.where((cn <= sj + NSA_SLC_BLOCK - 1) & (cn + NSA_CMP_LEN - 1 >= sj), 1.0, 0.0)
    blk_t = lax.broadcasted_iota(I32, (LANES, tq), 0)
    cur_t = (q0 + lax.broadcasted_iota(I32, (LANES, tq), 1)) // NSA_SLC_BLOCK
    causal_t = blk_t <= cur_t
    forced_t = (blk_t == 0) | ((blk_t >= cur_t - 1) & causal_t)
    for g in range(2):
        imp_t = lax.dot_general(shares_t, psum[g], (((1,), (1,)), ((), ())),
                                precision=lax.Precision.HIGHEST, preferred_element_type=F32)
        val = jnp.where(forced_t, BIG, jnp.where(causal_t, imp_t, NEG))
        val = jnp.where(blk_t < n_sb, val, NINF)
        selb = _top_n_mask(val, n_sel, 0).T.astype(BF16)

        def expand(c, _, selb=selb, g=g):
            col = c * tc + lax.broadcasted_iota(I32, (1, tc), 1)
            blk = (c * tc + lax.broadcasted_iota(I32, (LANES, tc), 1)) // NSA_SLC_BLOCK
            e = jnp.where(blk == lax.broadcasted_iota(I32, (LANES, tc), 0), 1.0, 0.0).astype(BF16)
            selk = jnp.dot(selb, e, preferred_element_type=F32)
            b_ref[g, c] = jnp.where(selk > 0.5, jnp.where(col <= row, 0.0, NINF), NINF)
            return 0

        lax.fori_loop(0, nch, expand, 0)

    w0 = pl.multiple_of(jnp.clip(q0 - NSA_WINDOW, 0, seq_len - wlen), LANES)
    dist = row - (w0 + lax.broadcasted_iota(I32, (1, wlen), 1))
    wbias = jnp.where(dist >= 0, jnp.where(dist < NSA_WINDOW, 0.0, NINF), NINF)
    kwin = kw_ref[pl.ds(w0, wlen), :]
    vwin = vw_ref[pl.ds(w0, wlen), :]

    for p in range(D_HEADS // 2):
        outs = []
        for g in range(2):
            qh = q_rot[p][g]
            h = g * (D_HEADS // 2) + p

            def slc_body(c, carry, qh=qh, g=g):
                off = pl.multiple_of(c * tc, tc)
                s = _dot_nt(qh, ks_ref[pl.ds(off, tc), :]) * SCALE + b_ref[g, c]
                return _flash_step(carry, s, vs_ref[pl.ds(off, tc), :])

            o_slc = _flash_out(lax.fori_loop(0, nch, slc_body, _flash_init(tq, LANES)))

            s = _dot_nt(qh, kwin) * SCALE + wbias
            e = jnp.exp2(s - jnp.max(s, axis=1, keepdims=True))
            o_win = jnp.dot(e.astype(BF16), vwin, preferred_element_type=F32) * _safe_recip(
                jnp.sum(e, axis=1, keepdims=True))
            outs.append(gates[:, 3 * h:3 * h + 1] * o_cmp[2 * p + g]
                        + gates[:, 3 * h + 1:3 * h + 2] * o_slc
                        + gates[:, 3 * h + 2:3 * h + 3] * o_win)
        o_ref[:, p * LANES:(p + 1) * LANES] = jnp.where(lo_mask, outs[0], outs[1]).astype(o_ref.dtype)


def _nsa_attention(rp, pp, dg, gate_b, kcmp, vcmp, tq=128, tc=512):
    b, s, _ = rp.shape
    tc = min(tc, s)
    wlen = min(NSA_WINDOW + tq, s)
    nc = s // NSA_CMP_STRIDE
    assert s // NSA_SLC_BLOCK <= LANES
    full = lambda t: pl.BlockSpec((None, s, LANES), lambda bi, i: (bi, 0, t))
    return pl.pallas_call(
        functools.partial(_nsa_kernel, tq=tq, tc=tc, wlen=wlen, seq_len=s),
        grid=(b, s // tq),
        in_specs=[pl.BlockSpec((None, tq, 4 * LANES), lambda bi, i: (bi, i, 2)),
                  pl.BlockSpec((None, tq, 4 * LANES), lambda bi, i: (bi, i, 1)),
                  pl.BlockSpec((None, tq, LANES), lambda bi, i: (bi, i, 0)),
                  pl.BlockSpec((1, LANES), lambda bi, i: (0, 0)),
                  pl.BlockSpec((None, nc, LANES), lambda bi, i: (bi, 0, 0)),
                  pl.BlockSpec((None, nc, LANES), lambda bi, i: (bi, 0, 0)),
                  full(12), full(10), full(13), full(11)],
        out_specs=pl.BlockSpec((None, tq, 4 * LANES), lambda bi, i: (bi, i, 0)),
        out_shape=jax.ShapeDtypeStruct((b, s, 4 * LANES), BF16),
        scratch_shapes=[pltpu.VMEM((2, s // tc, tq, tc), F32)],
        compiler_params=_cparams(("parallel", "arbitrary")),
        name="nsa_attn",
    )(rp, pp, dg, gate_b, kcmp, vcmp, rp, pp, rp, pp)


def _router_kernel(x_ref, w_ref, b_ref, o_ref):
    logits = jnp.dot(x_ref[...], w_ref[...], precision=lax.Precision.HIGHEST,
                     preferred_element_type=F32) + b_ref[...]
    lane = lax.broadcasted_iota(I32, (1, LANES), 1)
    lanef = lane.astype(F32)
    v = jnp.where(lane < N_EXPERTS, logits, NINF)
    l0 = jnp.max(v, axis=1, keepdims=True)
    i0 = jnp.min(jnp.where(v == l0, lanef, float(LANES)), axis=1, keepdims=True)
    v = jnp.where(lanef == i0, NINF, v)
    l1 = jnp.max(v, axis=1, keepdims=True)
    i1 = jnp.min(jnp.where(v == l1, lanef, float(LANES)), axis=1, keepdims=True)
    e1 = jnp.exp(l1 - l0)
    g0 = 1.0 / (1.0 + e1)
    g1 = e1 / (1.0 + e1)
    o_ref[...] = jnp.where(lane == 0, i0, jnp.where(lane == 1, i1, jnp.where(lane == 2, g0, jnp.where(
        lane == 3, g1, 0.0))))


def _router(x, w, b, tm=512):
    n, d = x.shape
    return pl.pallas_call(
        _router_kernel,
        grid=(n // tm,),
        in_specs=[pl.BlockSpec((tm, d), lambda i: (i, 0)), pl.BlockSpec((d, LANES), lambda i: (0, 0)),
                  pl.BlockSpec((1, LANES), lambda i: (0, 0))],
        out_specs=pl.BlockSpec((tm, LANES), lambda i: (i, 0)),
        out_shape=jax.ShapeDtypeStruct((n, LANES), F32),
        compiler_params=_cparams(("parallel",)),
        name="moe_router",
    )(x, w, b)


def _moe_ffn_kernel(be_ref, nu_ref, x_ref, wg_ref, wu_ref, wd_ref, o_ref, acc_ref):
    i = pl.program_id(0)
    f = pl.program_id(1)
    last = pl.num_programs(1) - 1
    used = i < nu_ref[0]

    @pl.when(used)
    def _():
        x = x_ref[...]
        h = jax.nn.silu(jnp.dot(x, wg_ref[...], preferred_element_type=F32)) * jnp.dot(
            x, wu_ref[...], preferred_element_type=F32)
        contrib = jnp.dot(h.astype(BF16), wd_ref[...], preferred_element_type=F32)

        @pl.when(f == 0)
        def _():
            acc_ref[...] = contrib

        @pl.when(f > 0)
        def _():
            acc_ref[...] += contrib

        @pl.when(f == last)
        def _():
            o_ref[...] = acc_ref[...]

    @pl.when(jnp.logical_not(used) & (f == last))
    def _():
        o_ref[...] = jnp.zeros(o_ref.shape, o_ref.dtype)


def _moe_ffn(x_sorted, block_e, n_used, wg, wu, wd, tf=896):
    ns, d = x_sorted.shape
    fdim = wg.shape[2]
    nf = fdim // tf
    n_blocks = ns // MOE_TM

    def row_map(i, f, be, nu):
        return (jnp.minimum(i, nu[0] - 1), 0)

    def f_of(i, f, nu):
        return jnp.where(i < nu[0], f, nf - 1)

    grid_spec = pltpu.PrefetchScalarGridSpec(
        num_scalar_prefetch=2,
        grid=(n_blocks, nf),
        in_specs=[pl.BlockSpec((MOE_TM, d), row_map),
                  pl.BlockSpec((None, d, tf), lambda i, f, be, nu: (be[i], 0, f_of(i, f, nu))),
                  pl.BlockSpec((None, d, tf), lambda i, f, be, nu: (be[i], 0, f_of(i, f, nu))),
                  pl.BlockSpec((None, tf, d), lambda i, f, be, nu: (be[i], f_of(i, f, nu), 0))],
        out_specs=pl.BlockSpec((MOE_TM, d), lambda i, f, be, nu: (i, 0)),
        scratch_shapes=[pltpu.VMEM((MOE_TM, d), F32)],
    )
    return pl.pallas_call(
        _moe_ffn_kernel,
        grid_spec=grid_spec,
        out_shape=jax.ShapeDtypeStruct((ns, d), F32),
        compiler_params=_cparams(("arbitrary", "arbitrary")),
        name="moe_ffn",
    )(block_e, n_used, x_sorted, wg, wu, wd)


def _combine_ln_kernel(x_ref, y0_ref, y1_ref, rt_ref, g_ref, b_ref, xo_ref, xb_ref):
    rt = rt_ref[...]
    ffn = rt[:, 2:3] * y0_ref[...] + rt[:, 3:4] * y1_ref[...]
    y = _layer_norm(DN_ALPHA * x_ref[...] + ffn, g_ref[...], b_ref[...])
    xo_ref[...] = y
    xb_ref[...] = y.astype(BF16)


def _combine_ln(x, y0, y1, rt, g, b, tm=512):
    n, d = x.shape
    row = lambda i: (i, 0)
    fixed = lambda i: (0, 0)
    return pl.pallas_call(
        _combine_ln_kernel,
        grid=(n // tm,),
        in_specs=[pl.BlockSpec((tm, d), row), pl.BlockSpec((tm, d), row), pl.BlockSpec((tm, d), row),
                  pl.BlockSpec((tm, LANES), row), pl.BlockSpec((1, d), fixed), pl.BlockSpec((1, d), fixed)],
        out_specs=[pl.BlockSpec((tm, d), row), pl.BlockSpec((tm, d), row)],
        out_shape=[jax.ShapeDtypeStruct((n, d), F32), jax.ShapeDtypeStruct((n, d), BF16)],
        compiler_params=_cparams(("parallel",)),
        name="moe_combine_ln",
    )(x, y0, y1, rt, g.reshape(1, d), b.reshape(1, d))


def _moe_layout(rt, n):
    e_flat = rt[:, 0:TOP_K].astype(I32).reshape(-1)
    nk = n * TOP_K
    onehot = (e_flat[:, None] == jnp.arange(N_EXPERTS, dtype=I32)[None, :]).astype(I32)
    rank = jnp.take_along_axis(jnp.cumsum(onehot, axis=0), e_flat[:, None], axis=1)[:, 0] - 1
    counts = jnp.sum(onehot, axis=0)
    padded = (counts + MOE_TM - 1) // MOE_TM * MOE_TM
    pad_end = jnp.cumsum(padded)
    pad_start = pad_end - padded
    grp_start = jnp.cumsum(counts) - counts
    slot = pad_start[e_flat] + rank
    n_blocks = -(-nk // MOE_TM) + N_EXPERTS
    n_slots = n_blocks * MOE_TM
    order = jnp.argsort(e_flat, stable=True).astype(I32)
    sl = jnp.arange(n_slots, dtype=I32)
    slot_e = jnp.minimum(jnp.searchsorted(pad_end, sl, side='right'), N_EXPERTS - 1).astype(I32)
    within = sl - pad_start[slot_e]
    valid = within < counts[slot_e]
    src = jnp.where(valid, grp_start[slot_e] + within, 0)
    slot_tok = jnp.where(valid, order[src] // TOP_K, 0)
    n_used = (pad_end[-1] // MOE_TM).astype(I32).reshape(1)
    blk = jnp.arange(n_blocks, dtype=I32)
    block_e = slot_e[jnp.minimum(blk, n_used[0] - 1) * MOE_TM]
    return slot_tok, slot.reshape(n, TOP_K), block_e, n_used


def _pair_perm(n_heads):
    half = n_heads // 2
    cols = []
    for p in range(half):
        cols += list(range(p * HEAD_DIM, (p + 1) * HEAD_DIM))
        cols += list(range((half + p) * HEAD_DIM, (half + p + 1) * HEAD_DIM))
    return np.asarray(cols, dtype=np.int32)


def _pad_cols(w, width):
    return jnp.pad(w, ((0, 0), (0, width - w.shape[1])))


def _even_layer(x, xb, w_in, w_out, lam_params, subln, lam_init, wg, wu, wd, ln, tabs, bsz, seq_len):
    n, d = x.shape
    perm = _pair_perm(B_HEADS)
    aq, ak, av = w_in[:, 0:512], w_in[:, 512:1024], w_in[:, 1024:1536]
    bq, bk, bv = w_in[:, 1536:2048], w_in[:, 2048:2176], w_in[:, 2176:2304]
    iq, ik, iw = w_in[:, 2304:2560], w_in[:, 2560:2624], w_in[:, 2624:2628]
    w_rope = jnp.concatenate([aq, ak, bq[:, perm], iq, bk, ik, ik], axis=1).astype(BF16)
    w_plain = jnp.concatenate([av, bv], axis=1).astype(BF16)
    w_iw = _pad_cols(iw, LANES).astype(BF16)
    rp = _matmul(xb, w_rope, BF16, 512, 1024, seq_len, tabs).reshape(bsz, seq_len, -1)
    pp = _matmul(xb, w_plain, BF16, 512, w_plain.shape[1]).reshape(bsz, seq_len, -1)
    iwv = _matmul(xb, w_iw, F32, 512, LANES).reshape(bsz, seq_len, LANES)
    o_a = _diff_attention(rp, pp, lam_params, subln, lam_init)
    o_b = _dsa_attention(rp, pp, iwv)
    wo_a = w_out[0:512].astype(BF16)
    wo_b = w_out[512:1024][perm].astype(BF16)
    g_mix, b_mix, g_ffn, b_ffn = ln
    x1, x1b = _outproj_ln(x, o_a.reshape(n, -1), o_b.reshape(n, -1), wo_a, wo_b, g_mix, b_mix)
    return _ffn_ln(x1, x1b, wg.astype(BF16), wu.astype(BF16), wd.astype(BF16), g_ffn, b_ffn)


def _odd_layer(x, xb, w_in, w_out, gate_b, pe, phi_w1, phi_w2, w_router, b_router, wg, wu, wd, ln, tabs,
               bsz, seq_len):
    n, d = x.shape
    perm = _pair_perm(D_HEADS)
    cq, ck, cv = w_in[:, 0:512], w_in[:, 512:1024], w_in[:, 1024:1536]
    dq = w_in[:, 1536:2048][:, perm]
    dkc, dvc, dks = w_in[:, 2048:2176], w_in[:, 2176:2304], w_in[:, 2304:2432]
    dvs, dkw, dvw = w_in[:, 2432:2560], w_in[:, 2560:2688], w_in[:, 2688:2816]
    dg = w_in[:, 2816:2840]
    w_rope = jnp.concatenate([cq, ck, dq, dks, dkw], axis=1).astype(BF16)
    w_plain = jnp.concatenate([cv, dq, dkc, dvc, dvs, dvw], axis=1).astype(BF16)
    w_dg = _pad_cols(dg, LANES).astype(BF16)
    rp = _matmul(xb, w_rope, BF16, 512, w_rope.shape[1] // 2, seq_len, tabs).reshape(bsz, seq_len, -1)
    pp = _matmul(xb, w_plain, BF16, 512, w_plain.shape[1] // 2).reshape(bsz, seq_len, -1)
    dgv = _matmul(xb, w_dg, F32, 512, LANES).reshape(bsz, seq_len, LANES)

    o_c = _moba_attention(rp, pp)

    nc = seq_len // NSA_CMP_STRIDE
    tok = pp[:, :, 8 * LANES:10 * LANES].reshape(bsz, nc, NSA_CMP_STRIDE, 4, HEAD_DIM)
    r = tok.transpose(0, 3, 1, 2, 4).reshape(bsz, 4, nc, NSA_CMP_STRIDE * HEAD_DIM)
    pe_flat = jnp.pad(pe.reshape(2, 1, -1), ((0, 0), (0, 7), (0, 0))).astype(BF16)
    cmp = _nsa_compress(r, pe_flat, phi_w1.astype(BF16), phi_w2.astype(BF16))
    kcmp = jnp.concatenate([cmp[:, 0], cmp[:, 1]], axis=-1)
    vcmp = jnp.concatenate([cmp[:, 2], cmp[:, 3]], axis=-1)
    gb = _pad_cols(gate_b.reshape(1, -1), LANES)
    o_d = _nsa_attention(rp, pp, dgv, gb, kcmp, vcmp)

    wo_c = w_out[0:512].astype(BF16)
    wo_d = w_out[512:1024][perm].astype(BF16)
    g_mix, b_mix, g_ffn, b_ffn = ln
    x1, x1b = _outproj_ln(x, o_c.reshape(n, -1), o_d.reshape(n, -1), wo_c, wo_d, g_mix, b_mix)

    rt = _router(x1, _pad_cols(w_router, LANES), _pad_cols(b_router.reshape(1, -1), LANES))
    slot_tok, slot, block_e, n_used = _moe_layout(rt, n)
    y_slots = _moe_ffn(x1b[slot_tok], block_e, n_used, wg.astype(BF16), wu.astype(BF16), wd.astype(BF16))
    return _combine_ln(x1, y_slots[slot[:, 0]], y_slots[slot[:, 1]], rt, g_ffn, b_ffn)


@jax.jit
def kernel(x, ev_w_in, ev_w_out, dif_lambda, dif_subln, ffd_w_gate, ffd_w_up, ffd_w_down, od_w_in, od_w_out,
           nsa_gate_b, nsa_pe, nsa_phi_w1, nsa_phi_w2, moe_w_router, moe_b_router, moe_w_gate, moe_w_up,
           moe_w_down, ln_mix_g, ln_mix_b, ln_ffn_g, ln_ffn_b):
    bsz, seq_len, d = x.shape
    tabs = _rope_tables(seq_len)
    xf = x.reshape(bsz * seq_len, d)
    xb = xf.astype(BF16)
    for l in range(DEPTH):
        i = l // 2
        ln = (ln_mix_g[l], ln_mix_b[l], ln_ffn_g[l], ln_ffn_b[l])
        if l % 2 == 0:
            lam_init = 0.8 - 0.6 * math.exp(-0.3 * l)
            xf, xb = _even_layer(xf, xb, ev_w_in[i], ev_w_out[i], dif_lambda[i], dif_subln[i], lam_init,
                                 ffd_w_gate[i], ffd_w_up[i], ffd_w_down[i], ln, tabs, bsz, seq_len)
        else:
            xf, xb = _odd_layer(xf, xb, od_w_in[i], od_w_out[i], nsa_gate_b[i], nsa_pe[i], nsa_phi_w1[i],
                                nsa_phi_w2[i], moe_w_router[i], moe_b_router[i], moe_w_gate[i], moe_w_up[i],
                                moe_w_down[i], ln, tabs, bsz, seq_len)
    return xf.reshape(bsz, seq_len, d)
```

```python
import functools
import math

import numpy as np
import jax
import jax.numpy as jnp
from jax import lax
from jax.experimental import pallas as pl
from jax.experimental.pallas import tpu as pltpu

F32 = jnp.float32
BF16 = jnp.bfloat16
I32 = jnp.int32

LANES = 128
VMEM_LIMIT = 56 * 1024 * 1024

DEPTH = 4
HEAD_DIM = 64
ROPE_THETA = 10000.0
LN_EPS = 1e-5
DN_ALPHA = (2 * DEPTH) ** 0.25
SCALE = HEAD_DIM ** -0.5 * math.log2(math.e)
NEG = -1e30
BIG = 1e30
M_INIT = -1e30
NINF = float("-inf")

A_HEADS = 4
B_HEADS = 8
IDX_HEADS = 4
DSA_TOPK = 256
C_HEADS = 8
MOBA_BLOCK = 256
MOBA_TOPK = 3
D_HEADS = 8
NSA_CMP_LEN = 32
NSA_CMP_STRIDE = 16
NSA_SLC_BLOCK = 64
NSA_SLC_TOPK = 16
NSA_WINDOW = 512
NSA_PHI_HIDDEN = 256
N_EXPERTS = 8
TOP_K = 2
MOE_TM = 512
N_BISECT = 12


def _cparams(sem):
    return pltpu.CompilerParams(dimension_semantics=sem, vmem_limit_bytes=VMEM_LIMIT)


def _dot_nt(a, b):
    return lax.dot_general(a, b, (((1,), (1,)), ((), ())), preferred_element_type=F32)


def _layer_norm(y, g, b):
    mu = jnp.mean(y, axis=-1, keepdims=True)
    yc = y - mu
    var = jnp.mean(yc * yc, axis=-1, keepdims=True)
    return yc * lax.rsqrt(var + LN_EPS) * g + b


def _safe_recip(l):
    return jnp.where(l > 0.0, 1.0 / jnp.where(l > 0.0, l, 1.0), 0.0)


def _half_masks():
    lane = lax.broadcasted_iota(I32, (1, LANES), 1)
    return lane < HEAD_DIM


def _split_halves(t):
    lo = _half_masks()
    z = jnp.zeros_like(t)
    return jnp.where(lo, t, z), jnp.where(lo, z, t)


def _mm_kernel(x_ref, w_ref, *rest, rope):
    o_ref = rest[-1]
    acc = jnp.dot(x_ref[...].astype(BF16), w_ref[...], preferred_element_type=F32)
    if not rope:
        o_ref[...] = acc.astype(o_ref.dtype)
        return
    cos = rest[0][...]
    sin = rest[1][...]
    lane = lax.broadcasted_iota(I32, (1, LANES), 1)
    first = (lane % HEAD_DIM) < (HEAD_DIM // 2)
    for c in range(acc.shape[1] // LANES):
        a = acc[:, c * LANES:(c + 1) * LANES]
        rot = jnp.where(first, pltpu.roll(a, LANES - HEAD_DIM // 2, 1), pltpu.roll(a, HEAD_DIM // 2, 1))
        o_ref[:, c * LANES:(c + 1) * LANES] = (a * cos + rot * sin).astype(o_ref.dtype)


def _matmul(x, w, out_dtype, tm, tn, seq_len=None, rope_tabs=None):
    n, d = x.shape
    p = w.shape[1]
    rope = rope_tabs is not None
    in_specs = [pl.BlockSpec((tm, d), lambda i, j: (i, 0)),
                pl.BlockSpec((d, tn), lambda i, j: (0, j))]
    args = [x, w]
    if rope:
        nt = seq_len // tm
        in_specs += [pl.BlockSpec((tm, LANES), lambda i, j: (i % nt, 0))] * 2
        args += list(rope_tabs)
    return pl.pallas_call(
        functools.partial(_mm_kernel, rope=rope),
        grid=(n // tm, p // tn),
        in_specs=in_specs,
        out_specs=pl.BlockSpec((tm, tn), lambda i, j: (i, j)),
        out_shape=jax.ShapeDtypeStruct((n, p), out_dtype),
        compiler_params=_cparams(("parallel", "arbitrary")),
        name="proj_rope" if rope else "proj",
    )(*args)


def _rope_tables(seq_len):
    d = HEAD_DIM
    inv = ROPE_THETA ** (-jnp.arange(0, d, 2, dtype=F32) / d)
    ang = jnp.arange(seq_len, dtype=I32).astype(F32)[:, None] * inv[None, :]
    cos = jnp.cos(ang)
    sin = jnp.sin(ang)
    cos128 = jnp.tile(cos, (1, LANES // (d // 2)))
    sin128 = jnp.tile(jnp.concatenate([-sin, sin], axis=1), (1, LANES // d))
    return cos128, sin128


def _outproj_ln_kernel(x_ref, a_ref, b_ref, wa_ref, wb_ref, g_ref, bb_ref, xo_ref, xb_ref):
    mix = (jnp.dot(a_ref[...], wa_ref[...], preferred_element_type=F32)
           + jnp.dot(b_ref[...], wb_ref[...], preferred_element_type=F32))
    y = _layer_norm(DN_ALPHA * x_ref[...] + mix, g_ref[...], bb_ref[...])
    xo_ref[...] = y
    xb_ref[...] = y.astype(BF16)


def _outproj_ln(x, oa, ob, wa, wb, g, b, tm=512):
    n, d = x.shape
    ka, kb = oa.shape[1], ob.shape[1]
    row = lambda i: (i, 0)
    fixed = lambda i: (0, 0)
    return pl.pallas_call(
        _outproj_ln_kernel,
        grid=(n // tm,),
        in_specs=[pl.BlockSpec((tm, d), row), pl.BlockSpec((tm, ka), row), pl.BlockSpec((tm, kb), row),
                  pl.BlockSpec((ka, d), fixed), pl.BlockSpec((kb, d), fixed),
                  pl.BlockSpec((1, d), fixed), pl.BlockSpec((1, d), fixed)],
        out_specs=[pl.BlockSpec((tm, d), row), pl.BlockSpec((tm, d), row)],
        out_shape=[jax.ShapeDtypeStruct((n, d), F32), jax.ShapeDtypeStruct((n, d), BF16)],
        compiler_params=_cparams(("parallel",)),
        name="outproj_ln",
    )(x, oa, ob, wa, wb, g.reshape(1, d), b.reshape(1, d))


def _ffn_ln_kernel(x_ref, xb_ref, wg_ref, wu_ref, wd_ref, g_ref, b_ref, xo_ref, xob_ref, acc_ref):
    f = pl.program_id(1)
    xb = xb_ref[...]
    h = jax.nn.silu(jnp.dot(xb, wg_ref[...], preferred_element_type=F32)) * jnp.dot(
        xb, wu_ref[...], preferred_element_type=F32)
    contrib = jnp.dot(h.astype(BF16), wd_ref[...], preferred_element_type=F32)

    @pl.when(f == 0)
    def _():
        acc_ref[...] = contrib

    @pl.when(f > 0)
    def _():
        acc_ref[...] += contrib

    @pl.when(f == pl.num_programs(1) - 1)
    def _():
        y = _layer_norm(DN_ALPHA * x_ref[...] + acc_ref[...], g_ref[...], b_ref[...])
        xo_ref[...] = y
        xob_ref[...] = y.astype(BF16)


def _ffn_ln(x, xb, wg, wu, wd, g, b, tm=512, tf=1408):
    n, d = x.shape
    fdim = wg.shape[1]
    return pl.pallas_call(
        _ffn_ln_kernel,
        grid=(n // tm, fdim // tf),
        in_specs=[pl.BlockSpec((tm, d), lambda i, f: (i, 0)), pl.BlockSpec((tm, d), lambda i, f: (i, 0)),
                  pl.BlockSpec((d, tf), lambda i, f: (0, f)), pl.BlockSpec((d, tf), lambda i, f: (0, f)),
                  pl.BlockSpec((tf, d), lambda i, f: (f, 0)),
                  pl.BlockSpec((1, d), lambda i, f: (0, 0)), pl.BlockSpec((1, d), lambda i, f: (0, 0))],
        out_specs=[pl.BlockSpec((tm, d), lambda i, f: (i, 0)), pl.BlockSpec((tm, d), lambda i, f: (i, 0))],
        out_shape=[jax.ShapeDtypeStruct((n, d), F32), jax.ShapeDtypeStruct((n, d), BF16)],
        scratch_shapes=[pltpu.VMEM((tm, d), F32)],
        compiler_params=_cparams(("parallel", "arbitrary")),
        name="ffn_ln",
    )(x, xb, wg, wu, wd, g.reshape(1, d), b.reshape(1, d))


def _top_n_mask(v, n, axis):
    idx = lax.broadcasted_iota(I32, v.shape, axis).astype(F32)
    sel = jnp.zeros(v.shape, F32)
    for _ in range(n):
        mx = jnp.max(v, axis=axis, keepdims=True)
        first = jnp.min(jnp.where(v == mx, idx, float(v.shape[axis])), axis=axis, keepdims=True)
        pick = idx == first
        sel = jnp.where(pick, 1.0, sel)
        v = jnp.where(pick, NINF, v)
    return sel


DV_PAD = 16
TQ = 256


def _values_t(v, n_heads, width, tk):
    b, s, _ = v.shape
    vt = v.reshape(b, s // tk, tk, n_heads, width).transpose(0, 3, 1, 4, 2)
    return jnp.concatenate([vt, jnp.ones((b, n_heads, s // tk, DV_PAD, tk), v.dtype)], axis=3)


def _normalize_t(acc, width):
    return acc[:width] * _safe_recip(acc[width:width + 1])


def _pipe_flash(n, ns, qk, vt_at, bias_at, bufs, dv, tk, tq):
    sa, sb, pa, pb = bufs
    for j in range(ns):
        sa[j] = qk(0, j)
        pb[j] = jnp.zeros((tk, tq), BF16)

    def half(c, carry, s_cur, s_nxt, p_prev, p_cur):
        nxt = jnp.minimum(c + 1, n - 1)
        for j in range(ns):
            s_nxt[j] = qk(nxt, j)
        cp = jnp.clip(c - 1, 0, n - 1)
        out = []
        for j in range(ns):
            m, acc, alpha = carry[j]
            acc = alpha * acc + jnp.dot(vt_at(cp, j), p_prev[j], preferred_element_type=F32)
            st = s_cur[j] + bias_at(c, j)
            m_new = jnp.maximum(m, jnp.max(st, axis=0, keepdims=True))
            alpha = jnp.exp2(m - m_new)
            p_cur[j] = jnp.exp2((st - m_new).astype(BF16))
            out.append((m_new, acc, alpha))
        return tuple(out)

    def body(t, carry):
        carry = half(2 * t, carry, sa, sb, pb, pa)
        return half(2 * t + 1, carry, sb, sa, pa, pb)

    init = (jnp.full((1, tq), M_INIT, F32), jnp.zeros((dv, tq), F32), jnp.ones((1, tq), F32))
    trips = (n + 1) // 2
    carry = lax.fori_loop(0, trips, body, (init,) * ns)
    cl = jnp.minimum(2 * trips - 1, n - 1)
    outs = []
    for j in range(ns):
        _, acc, alpha = carry[j]
        outs.append(alpha * acc + jnp.dot(vt_at(cl, j), pb[j], preferred_element_type=F32))
    return outs


def _pipe_scratch(ns, tk, tq):
    return [pltpu.VMEM((ns, tk, tq), F32)] * 2 + [pltpu.VMEM((ns, tk, tq), BF16)] * 2


def _causal_t(t):
    return jnp.where(lax.broadcasted_iota(I32, (t, t), 0) <= lax.broadcasted_iota(I32, (t, t), 1), 0.0, NINF)


def _diff_kernel(lam_ref, sub_ref, q_ref, k_ref, vt_ref, o_ref, *bufs, tq, lam_init):
    i = pl.program_id(2)
    lp = lam_ref[...]
    lam = (jnp.exp(jnp.sum(lp[0:1] * lp[1:2], axis=1, keepdims=True))
           - jnp.exp(jnp.sum(lp[2:3] * lp[3:4], axis=1, keepdims=True)) + lam_init)
    qs = _split_halves(q_ref[...])
    n = i + 1
    diag = _causal_t(tq)

    def qk(c, j):
        off = pl.multiple_of(c * tq, tq)
        return _dot_nt(k_ref[pl.ds(off, tq), :], qs[j]) * SCALE

    def bias_at(c, j):
        return jnp.where(c == i, diag, jnp.where(c < n, 0.0, NINF))

    outs = _pipe_flash(n, 2, qk, lambda c, j: vt_ref[c], bias_at, bufs, LANES + DV_PAD, tq, tq)
    o = _normalize_t(outs[0], LANES) - lam * _normalize_t(outs[1], LANES)
    o = o * lax.rsqrt(jnp.mean(o * o, axis=0, keepdims=True) + LN_EPS)
    o = o * sub_ref[...] * (1.0 - lam_init)
    o_ref[...] = o.T.astype(o_ref.dtype)


def _diff_attention(rp, pp, lam_params, subln, lam_init, tq=TQ):
    b, s, _ = rp.shape
    nk = s // tq
    vt = _values_t(pp[:, :, :A_HEADS * LANES], A_HEADS, LANES, tq)
    return pl.pallas_call(
        functools.partial(_diff_kernel, tq=tq, lam_init=lam_init),
        grid=(b, A_HEADS, nk),
        in_specs=[pl.BlockSpec((4, HEAD_DIM), lambda bi, h, i: (0, 0)),
                  pl.BlockSpec((LANES, 1), lambda bi, h, i: (0, 0)),
                  pl.BlockSpec((None, tq, LANES), lambda bi, h, i: (bi, i, h)),
                  pl.BlockSpec((None, s, LANES), lambda bi, h, i: (bi, 0, A_HEADS + h)),
                  pl.BlockSpec((None, None, nk, LANES + DV_PAD, tq), lambda bi, h, i: (bi, h, 0, 0, 0))],
        out_specs=pl.BlockSpec((None, tq, LANES), lambda bi, h, i: (bi, i, h)),
        out_shape=jax.ShapeDtypeStruct((b, s, A_HEADS * LANES), BF16),
        scratch_shapes=_pipe_scratch(2, tq, tq),
        compiler_params=_cparams(("parallel", "parallel", "arbitrary")),
        name="diff_attn",
    )(lam_params, subln.reshape(LANES, 1), rp, rp, vt)


def _fold8(x, op):
    acc = x[0:8]
    for r in range(1, x.shape[0] // 8):
        acc = op(acc, x[r * 8:(r + 1) * 8])
    return acc


def _dsa_kernel(iq_ref, ikk_ref, iw_ref, q_ref, k_ref, vt_ref, o_ref, s_ref, j_ref, *bufs, tq, ksel, seq_len):
    tk = tq
    i = pl.program_id(1)
    nch = i + 1
    ksel_f = float(ksel)
    k_loc = lax.broadcasted_iota(I32, (tk, tq), 0)
    q_loc = lax.broadcasted_iota(I32, (tk, tq), 1)
    qpos = i * tq + lax.broadcasted_iota(I32, (1, tq), 1)

    iq = iq_ref[...]
    iwt = iw_ref[...].T
    iqh = []
    for pair in range(IDX_HEADS // 2):
        iqh += list(_split_halves(iq[:, pair * LANES:(pair + 1) * LANES]))

    def scores(c):
        off = pl.multiple_of(c * tk, tk)
        kk = ikk_ref[pl.ds(off, tk), :]
        sc = iwt[0:1] * jnp.maximum(_dot_nt(kk, iqh[0]), 0.0)
        for h in range(1, IDX_HEADS):
            sc = sc + iwt[h:h + 1] * jnp.maximum(_dot_nt(kk, iqh[h]), 0.0)
        return sc

    def full_body(c, carry):
        mx, mn = carry
        sc = scores(c)
        s_ref[c] = sc
        return jnp.maximum(mx, _fold8(sc, jnp.maximum)), jnp.minimum(mn, _fold8(sc, jnp.minimum))

    mx, mn = lax.fori_loop(0, i, full_body, (jnp.full((8, tq), -BIG, F32), jnp.full((8, tq), BIG, F32)))
    sc = scores(i)
    causal = k_loc <= q_loc
    s_ref[i] = jnp.where(causal, sc, NEG)
    mx = jnp.maximum(mx, _fold8(jnp.where(causal, sc, -BIG), jnp.maximum))
    mn = jnp.minimum(mn, _fold8(jnp.where(causal, sc, BIG), jnp.minimum))
    smax = jnp.max(mx, axis=0, keepdims=True)
    smin = jnp.min(mn, axis=0, keepdims=True)
    s_ref[nch] = jnp.full((tk, tq), NINF, F32)

    def count_where(ind):
        def body(c, acc):
            return acc + _fold8(ind(s_ref[c], c * tk + k_loc), jnp.add)
        acc = lax.fori_loop(0, nch, body, jnp.zeros((8, tq), F32))
        return jnp.sum(acc, axis=0, keepdims=True)

    def count_ge(th):
        return count_where(lambda x, kidx: jnp.where(x >= th, 1.0, 0.0))

    def max_below(th):
        def body(c, acc):
            x = s_ref[c]
            return jnp.maximum(acc, _fold8(jnp.where(x < th, x, NINF), jnp.maximum))
        acc = lax.fori_loop(0, nch, body, jnp.full((8, tq), NINF, F32))
        return jnp.max(acc, axis=0, keepdims=True)

    n_causal = (qpos + 1).astype(F32)
    take_all = n_causal <= ksel_f
    done0 = jnp.where(take_all | (count_ge(smax) >= ksel_f), 1.0, 0.0)

    def bisect(lo, hi):
        mid = lo + (hi - lo) * 0.5
        ge = count_ge(mid) >= ksel_f
        return jnp.where(ge, mid, lo), jnp.where(ge, hi, mid)

    lo, hi = lax.fori_loop(0, N_BISECT, lambda _, c: bisect(*c), (smin, smax))

    def snap_body(carry):
        lo, hi, th, done, _ = carry
        lo, hi = bisect(lo, hi)
        t1 = max_below(hi)
        ok = count_ge(t1) >= ksel_f
        th = jnp.where(done > 0.0, th, t1)
        hi = jnp.where(ok, hi, t1)
        done = jnp.where(ok, 1.0, done)
        return lo, hi, th, done, jnp.sum(1.0 - done)

    _, _, th, _, _ = lax.while_loop(lambda c: c[4] > 0.0, snap_body,
                                    (lo, hi, smax, done0, jnp.sum(1.0 - done0)))

    c_ge = count_ge(th)
    need_tb = jnp.where(take_all, 0.0, jnp.where(c_ge > ksel_f, 1.0, 0.0))
    j_ref[...] = jnp.full((8, tq), seq_len - 1, I32)

    @pl.when(jnp.sum(need_tb) > 0.0)
    def _():
        need = ksel_f - count_where(lambda x, kidx: jnp.where(x > th, 1.0, 0.0))

        def jb(_, carry):
            lo_j, hi_j = carry
            mid = (lo_j + hi_j) // 2
            cnt = count_where(lambda x, kidx: jnp.where(x == th, jnp.where(kidx <= mid, 1.0, 0.0), 0.0))
            ge = cnt >= need
            return jnp.where(ge, lo_j, mid), jnp.where(ge, mid, hi_j)

        n_it = int(math.ceil(math.log2(seq_len))) + 1
        _, hi_j = lax.fori_loop(0, n_it, jb, (jnp.full((1, tq), -1, I32), jnp.full((1, tq), seq_len - 1, I32)))
        j_ref[...] = jnp.broadcast_to(hi_j, (8, tq))

    jsel = j_ref[0:1, :]

    def bias_body(c, _):
        x = s_ref[c]
        kidx = c * tk + k_loc
        keep = jnp.where(x > th, 0.0, jnp.where(x == th, jnp.where(kidx <= jsel, 0.0, NINF), NINF))
        keep = jnp.where(take_all, 0.0, keep)
        s_ref[c] = jnp.where(kidx <= qpos, keep, NINF)
        return 0

    lax.fori_loop(0, nch, bias_body, 0)

    for p in range(B_HEADS // 2):
        qs = _split_halves(q_ref[:, p * LANES:(p + 1) * LANES])

        def qk(c, j, qs=qs):
            off = pl.multiple_of(c * tk, tk)
            return _dot_nt(k_ref[pl.ds(off, tk), :], qs[j]) * SCALE

        outs = _pipe_flash(nch, 2, qk, lambda c, j: vt_ref[j, c], lambda c, j: s_ref[jnp.minimum(c, nch)],
                           bufs, HEAD_DIM + DV_PAD, tk, tq)
        o = jnp.concatenate([_normalize_t(outs[0], HEAD_DIM), _normalize_t(outs[1], HEAD_DIM)], axis=0)
        o_ref[:, p * LANES:(p + 1) * LANES] = o.T.astype(o_ref.dtype)


def _dsa_attention(rp, pp, iw, tq=TQ):
    b, s, _ = rp.shape
    ksel = min(DSA_TOPK, s // 4)
    nk = s // tq
    vt = _values_t(pp[:, :, 4 * LANES:5 * LANES], 2, HEAD_DIM, tq)
    return pl.pallas_call(
        functools.partial(_dsa_kernel, tq=tq, ksel=ksel, seq_len=s),
        grid=(b, nk),
        in_specs=[pl.BlockSpec((None, tq, 2 * LANES), lambda bi, i: (bi, i, 6)),
                  pl.BlockSpec((None, s, LANES), lambda bi, i: (bi, 0, 15)),
                  pl.BlockSpec((None, tq, LANES), lambda bi, i: (bi, i, 0)),
                  pl.BlockSpec((None, tq, 4 * LANES), lambda bi, i: (bi, i, 2)),
                  pl.BlockSpec((None, s, LANES), lambda bi, i: (bi, 0, 14)),
                  pl.BlockSpec((None, 2, nk, HEAD_DIM + DV_PAD, tq), lambda bi, i: (bi, 0, 0, 0, 0))],
        out_specs=pl.BlockSpec((None, tq, 4 * LANES), lambda bi, i: (bi, i, 0)),
        out_shape=jax.ShapeDtypeStruct((b, s, 4 * LANES), BF16),
        scratch_shapes=[pltpu.VMEM((nk + 1, tq, tq), F32), pltpu.VMEM((8, tq), I32)] + _pipe_scratch(2, tq, tq),
        compiler_params=_cparams(("parallel", "arbitrary")),
        name="dsa_attn",
    )(rp, rp, iw, rp, rp, vt)


def _moba_kernel(q_ref, k_ref, vt_ref, o_ref, km_ref, sel_ref, *bufs, seq_len, n_sel):
    tq = MOBA_BLOCK
    qb = pl.program_id(2)

    @pl.when(qb == 0)
    def _():
        j = lax.broadcasted_iota(I32, (LANES, seq_len), 0)
        s = lax.broadcasted_iota(I32, (LANES, seq_len), 1)
        avg = jnp.where(s // MOBA_BLOCK == j, 1.0 / MOBA_BLOCK, 0.0).astype(BF16)
        km_ref[...] = jnp.dot(avg, k_ref[...], preferred_element_type=F32)

    km = km_ref[...]
    qs = _split_halves(q_ref[...])
    past = lax.broadcasted_iota(I32, (LANES, tq), 0) < qb
    for j in range(2):
        gate = lax.dot_general(km, qs[j].astype(F32), (((1,), (1,)), ((), ())),
                               precision=lax.Precision.HIGHEST, preferred_element_type=F32)
        sel = _top_n_mask(jnp.where(past, gate, NEG), n_sel, 0)
        sel_ref[j] = jnp.where(past, jnp.where(sel > 0.5, 0.0, NINF), NINF)
    own = _causal_t(tq)
    n = qb + 1

    def qk(c, j):
        off = pl.multiple_of(c * tq, tq)
        return _dot_nt(k_ref[pl.ds(off, tq), :], qs[j]) * SCALE

    def bias_at(c, j):
        chosen = sel_ref[j, pl.ds(jnp.minimum(c, LANES - 1), 1), :]
        return jnp.where(c == qb, own, jnp.where(c < n, chosen, NINF))

    outs = _pipe_flash(n, 2, qk, lambda c, j: vt_ref[j, c], bias_at, bufs, HEAD_DIM + DV_PAD, tq, tq)
    o = jnp.concatenate([_normalize_t(outs[0], HEAD_DIM), _normalize_t(outs[1], HEAD_DIM)], axis=0)
    o_ref[...] = o.T.astype(o_ref.dtype)


def _moba_attention(rp, pp):
    b, s, _ = rp.shape
    tq = MOBA_BLOCK
    nb = s // tq
    n_sel = max(1, min(MOBA_TOPK, nb - 1))
    npair = C_HEADS // 2
    vt = _values_t(pp[:, :, :C_HEADS * HEAD_DIM], C_HEADS, HEAD_DIM, tq)
    return pl.pallas_call(
        functools.partial(_moba_kernel, seq_len=s, n_sel=n_sel),
        grid=(b, npair, nb),
        in_specs=[pl.BlockSpec((None, tq, LANES), lambda bi, h, i: (bi, i, h)),
                  pl.BlockSpec((None, s, LANES), lambda bi, h, i: (bi, 0, npair + h)),
                  pl.BlockSpec((None, 2, nb, HEAD_DIM + DV_PAD, tq), lambda bi, h, i: (bi, h, 0, 0, 0))],
        out_specs=pl.BlockSpec((None, tq, LANES), lambda bi, h, i: (bi, i, h)),
        out_shape=jax.ShapeDtypeStruct((b, s, npair * LANES), BF16),
        scratch_shapes=[pltpu.VMEM((LANES, LANES), F32), pltpu.VMEM((2, LANES, tq), F32)] + _pipe_scratch(2, tq, tq),
        compiler_params=_cparams(("parallel", "parallel", "arbitrary")),
        name="moba_attn",
    )(rp, rp, vt)


def _cmp_kernel(r_ref, pe_ref, w1_ref, w2_ref, o_ref):
    r = r_ref[...]
    w1 = w1_ref[...]
    half = r.shape[1]
    u = jnp.dot(r, w1[:half], preferred_element_type=F32)
    v = jnp.dot(r, w1[half:], preferred_element_type=F32)
    c = jnp.dot(pe_ref[...], w1, preferred_element_type=F32)[0:1]
    pre = u + pltpu.roll(v, r.shape[0] - 1, 0) + c
    o_ref[...] = jnp.dot(jax.nn.gelu(pre).astype(BF16), w2_ref[...],
                         preferred_element_type=F32).astype(o_ref.dtype)


def _nsa_compress(r, pe, w1, w2):
    b, _, nc, wdt = r.shape
    hid = w1.shape[2]
    return pl.pallas_call(
        _cmp_kernel,
        grid=(b, 4),
        in_specs=[pl.BlockSpec((None, None, nc, wdt), lambda bi, t: (bi, t, 0, 0)),
                  pl.BlockSpec((None, 8, 2 * wdt), lambda bi, t: (t // 2, 0, 0)),
                  pl.BlockSpec((None, 2 * wdt, hid), lambda bi, t: (t // 2, 0, 0)),
                  pl.BlockSpec((None, hid, HEAD_DIM), lambda bi, t: (t // 2, 0, 0))],
        out_specs=pl.BlockSpec((None, None, nc, HEAD_DIM), lambda bi, t: (bi, t, 0, 0)),
        out_shape=jax.ShapeDtypeStruct((b, 4, nc, HEAD_DIM), BF16),
        compiler_params=_cparams(("parallel", "arbitrary")),
        name="nsa_compress",
    )(r, pe, w1, w2)


def _nsa_kernel(qr_ref, qw_ref, dg_ref, gb_ref, kc_ref, vct_ref, ks_ref, vst_ref, kw_ref, vwt_ref,
                o_ref, b_ref, wb_ref, *bufs, tq, seq_len):
    tk = tq
    i = pl.program_id(1)
    nch = i + 1
    nc = seq_len // NSA_CMP_STRIDE
    n_sb = seq_len // NSA_SLC_BLOCK
    n_sel = min(NSA_SLC_TOPK, n_sb)
    k_loc = lax.broadcasted_iota(I32, (tk, tq), 0)
    q_loc = lax.broadcasted_iota(I32, (tk, tq), 1)
    qpos = i * tq + lax.broadcasted_iota(I32, (1, tq), 1)

    gates_t = jax.nn.sigmoid(dg_ref[...] + gb_ref[...]).T
    kc = kc_ref[...]
    cmp_end = lax.broadcasted_iota(I32, (nc, 1), 0) * NSA_CMP_STRIDE + (NSA_CMP_LEN - 1)
    cbias = jnp.where(cmp_end <= qpos, 0.0, NINF)

    q_rot, o_cmp = [], []
    psum = [jnp.zeros((nc, tq), F32), jnp.zeros((nc, tq), F32)]
    for p in range(D_HEADS // 2):
        q_rot.append(_split_halves(qr_ref[:, p * LANES:(p + 1) * LANES]))
        for g, qh in enumerate(_split_halves(qw_ref[:, p * LANES:(p + 1) * LANES])):
            s = _dot_nt(kc, qh) * SCALE + cbias
            m = jnp.max(s, axis=0, keepdims=True)
            e = jnp.exp2(s - jnp.where(m == NINF, 0.0, m))
            pc = e * _safe_recip(jnp.sum(e, axis=0, keepdims=True))
            psum[g] = psum[g] + pc
            o_cmp.append(jnp.dot(vct_ref[g, :HEAD_DIM, :], pc.astype(BF16), preferred_element_type=F32))

    cn = lax.broadcasted_iota(I32, (LANES, nc), 1) * NSA_CMP_STRIDE
    sj = lax.broadcasted_iota(I32, (LANES, nc), 0) * NSA_SLC_BLOCK
    shares = jnp.where((cn <= sj + NSA_SLC_BLOCK - 1) & (cn + NSA_CMP_LEN - 1 >= sj), 1.0, 0.0)
    blk = lax.broadcasted_iota(I32, (LANES, tq), 0)
    cur = qpos // NSA_SLC_BLOCK
    causal_b = blk <= cur
    forced = (blk == 0) | ((blk >= cur - 1) & causal_b)
    for g in range(2):
        imp = jnp.dot(shares, psum[g], precision=lax.Precision.HIGHEST, preferred_element_type=F32)
        val = jnp.where(forced, BIG, jnp.where(causal_b, imp, NEG))
        val = jnp.where(blk < n_sb, val, NINF)
        selb = _top_n_mask(val, n_sel, 0).astype(BF16)

        def expand(c, _, selb=selb, g=g):
            kidx = c * tk + k_loc
            e = jnp.where((c * tk + lax.broadcasted_iota(I32, (tk, LANES), 0)) // NSA_SLC_BLOCK
                          == lax.broadcasted_iota(I32, (tk, LANES), 1), 1.0, 0.0).astype(BF16)
            selk = jnp.dot(e, selb, preferred_element_type=F32)
            b_ref[g, c] = jnp.where(selk > 0.5, jnp.where(kidx <= qpos, 0.0, NINF), NINF)
            return 0

        lax.fori_loop(0, nch, expand, 0)
        b_ref[g, nch] = jnp.full((tk, tq), NINF, F32)

    wb_ref[0] = jnp.where(k_loc <= q_loc, 0.0, NINF)
    wb_ref[1] = jnp.zeros((tk, tq), F32)
    wb_ref[2] = jnp.where(k_loc > q_loc, 0.0, NINF)
    wb_ref[3] = jnp.full((tk, tq), NINF, F32)
    n_wc = NSA_WINDOW // tk + 1
    w_first = jnp.maximum(i - (n_wc - 1), 0)
    n_w = i - w_first + 1

    for p in range(D_HEADS // 2):
        qs = q_rot[p]

        def qk_s(c, j, qs=qs):
            off = pl.multiple_of(c * tk, tk)
            return _dot_nt(ks_ref[pl.ds(off, tk), :], qs[j]) * SCALE

        o_slc = _pipe_flash(nch, 2, qk_s, lambda c, j: vst_ref[j, c],
                            lambda c, j: b_ref[j, jnp.minimum(c, nch)], bufs, HEAD_DIM + DV_PAD, tk, tq)

        def qk_w(c, j, qs=qs):
            off = pl.multiple_of((w_first + c) * tk, tk)
            return _dot_nt(kw_ref[pl.ds(off, tk), :], qs[j]) * SCALE

        def bias_w(c, j):
            d = i - (w_first + c)
            return wb_ref[jnp.where(c < n_w, d, n_wc)]

        o_win = _pipe_flash(n_w, 2, qk_w, lambda c, j: vwt_ref[j, w_first + c], bias_w, bufs,
                            HEAD_DIM + DV_PAD, tk, tq)
        outs = []
        for g in range(2):
            h = g * (D_HEADS // 2) + p
            outs.append(gates_t[3 * h:3 * h + 1] * o_cmp[2 * p + g]
                        + gates_t[3 * h + 1:3 * h + 2] * _normalize_t(o_slc[g], HEAD_DIM)
                        + gates_t[3 * h + 2:3 * h + 3] * _normalize_t(o_win[g], HEAD_DIM))
        o_ref[:, p * LANES:(p + 1) * LANES] = jnp.concatenate(outs, axis=0).T.astype(o_ref.dtype)


def _nsa_attention(rp, pp, dg, gate_b, kcmp, vcmp, tq=TQ):
    b, s, _ = rp.shape
    nk = s // tq
    nc = s // NSA_CMP_STRIDE
    assert s // NSA_SLC_BLOCK <= LANES and NSA_WINDOW % tq == 0
    n_wc = NSA_WINDOW // tq + 1
    vct = _values_t(vcmp, 2, HEAD_DIM, nc)[:, :, 0]
    vst = _values_t(pp[:, :, 10 * LANES:11 * LANES], 2, HEAD_DIM, tq)
    vwt = _values_t(pp[:, :, 11 * LANES:12 * LANES], 2, HEAD_DIM, tq)
    full = lambda t: pl.BlockSpec((None, s, LANES), lambda bi, i: (bi, 0, t))
    vspec = pl.BlockSpec((None, 2, nk, HEAD_DIM + DV_PAD, tq), lambda bi, i: (bi, 0, 0, 0, 0))
    return pl.pallas_call(
        functools.partial(_nsa_kernel, tq=tq, seq_len=s),
        grid=(b, nk),
        in_specs=[pl.BlockSpec((None, tq, 4 * LANES), lambda bi, i: (bi, i, 2)),
                  pl.BlockSpec((None, tq, 4 * LANES), lambda bi, i: (bi, i, 1)),
                  pl.BlockSpec((None, tq, LANES), lambda bi, i: (bi, i, 0)),
                  pl.BlockSpec((1, LANES), lambda bi, i: (0, 0)),
                  pl.BlockSpec((None, nc, LANES), lambda bi, i: (bi, 0, 0)),
                  pl.BlockSpec((None, 2, HEAD_DIM + DV_PAD, nc), lambda bi, i: (bi, 0, 0, 0)),
                  full(12), vspec, full(13), vspec],
        out_specs=pl.BlockSpec((None, tq, 4 * LANES), lambda bi, i: (bi, i, 0)),
        out_shape=jax.ShapeDtypeStruct((b, s, 4 * LANES), BF16),
        scratch_shapes=[pltpu.VMEM((2, nk + 1, tq, tq), F32), pltpu.VMEM((n_wc + 1, tq, tq), F32)]
        + _pipe_scratch(2, tq, tq),
        compiler_params=_cparams(("parallel", "arbitrary")),
        name="nsa_attn",
    )(rp, pp, dg, gate_b, kcmp, vct, rp, vst, rp, vwt)


def _router_kernel(x_ref, w_ref, b_ref, o_ref):
    logits = jnp.dot(x_ref[...], w_ref[...], precision=lax.Precision.HIGHEST,
                     preferred_element_type=F32) + b_ref[...]
    lane = lax.broadcasted_iota(I32, (1, LANES), 1)
    lanef = lane.astype(F32)
    v = jnp.where(lane < N_EXPERTS, logits, NINF)
    l0 = jnp.max(v, axis=1, keepdims=True)
    i0 = jnp.min(jnp.where(v == l0, lanef, float(LANES)), axis=1, keepdims=True)
    v = jnp.where(lanef == i0, NINF, v)
    l1 = jnp.max(v, axis=1, keepdims=True)
    i1 = jnp.min(jnp.where(v == l1, lanef, float(LANES)), axis=1, keepdims=True)
    e1 = jnp.exp(l1 - l0)
    g0 = 1.0 / (1.0 + e1)
    g1 = e1 / (1.0 + e1)
    o_ref[...] = jnp.where(lane == 0, i0, jnp.where(lane == 1, i1, jnp.where(lane == 2, g0, jnp.where(
        lane == 3, g1, 0.0))))


def _router(x, w, b, tm=512):
    n, d = x.shape
    return pl.pallas_call(
        _router_kernel,
        grid=(n // tm,),
        in_specs=[pl.BlockSpec((tm, d), lambda i: (i, 0)), pl.BlockSpec((d, LANES), lambda i: (0, 0)),
                  pl.BlockSpec((1, LANES), lambda i: (0, 0))],
        out_specs=pl.BlockSpec((tm, LANES), lambda i: (i, 0)),
        out_shape=jax.ShapeDtypeStruct((n, LANES), F32),
        compiler_params=_cparams(("parallel",)),
        name="moe_router",
    )(x, w, b)


def _moe_ffn_kernel(be_ref, nu_ref, x_ref, wg_ref, wu_ref, wd_ref, o_ref, acc_ref):
    i = pl.program_id(0)
    f = pl.program_id(1)
    last = pl.num_programs(1) - 1
    used = i < nu_ref[0]

    @pl.when(used)
    def _():
        x = x_ref[...]
        h = jax.nn.silu(jnp.dot(x, wg_ref[...], preferred_element_type=F32)) * jnp.dot(
            x, wu_ref[...], preferred_element_type=F32)
        contrib = jnp.dot(h.astype(BF16), wd_ref[...], preferred_element_type=F32)

        @pl.when(f == 0)
        def _():
            acc_ref[...] = contrib

        @pl.when(f > 0)
        def _():
            acc_ref[...] += contrib

        @pl.when(f == last)
        def _():
            o_ref[...] = acc_ref[...]

    @pl.when(jnp.logical_not(used) & (f == last))
    def _():
        o_ref[...] = jnp.zeros(o_ref.shape, o_ref.dtype)


def _moe_ffn(x_sorted, block_e, n_used, wg, wu, wd, tf=896):
    ns, d = x_sorted.shape
    fdim = wg.shape[2]
    nf = fdim // tf
    n_blocks = ns // MOE_TM

    def row_map(i, f, be, nu):
        return (jnp.minimum(i, nu[0] - 1), 0)

    def f_of(i, f, nu):
        return jnp.where(i < nu[0], f, nf - 1)

    grid_spec = pltpu.PrefetchScalarGridSpec(
        num_scalar_prefetch=2,
        grid=(n_blocks, nf),
        in_specs=[pl.BlockSpec((MOE_TM, d), row_map),
                  pl.BlockSpec((None, d, tf), lambda i, f, be, nu: (be[i], 0, f_of(i, f, nu))),
                  pl.BlockSpec((None, d, tf), lambda i, f, be, nu: (be[i], 0, f_of(i, f, nu))),
                  pl.BlockSpec((None, tf, d), lambda i, f, be, nu: (be[i], f_of(i, f, nu), 0))],
        out_specs=pl.BlockSpec((MOE_TM, d), lambda i, f, be, nu: (i, 0)),
        scratch_shapes=[pltpu.VMEM((MOE_TM, d), F32)],
    )
    return pl.pallas_call(
        _moe_ffn_kernel,
        grid_spec=grid_spec,
        out_shape=jax.ShapeDtypeStruct((ns, d), F32),
        compiler_params=_cparams(("arbitrary", "arbitrary")),
        name="moe_ffn",
    )(block_e, n_used, x_sorted, wg, wu, wd)


def _combine_ln_kernel(x_ref, y0_ref, y1_ref, rt_ref, g_ref, b_ref, xo_ref, xb_ref):
    rt = rt_ref[...]
    ffn = rt[:, 2:3] * y0_ref[...] + rt[:, 3:4] * y1_ref[...]
    y = _layer_norm(DN_ALPHA * x_ref[...] + ffn, g_ref[...], b_ref[...])
    xo_ref[...] = y
    xb_ref[...] = y.astype(BF16)


def _combine_ln(x, y0, y1, rt, g, b, tm=512):
    n, d = x.shape
    row = lambda i: (i, 0)
    fixed = lambda i: (0, 0)
    return pl.pallas_call(
        _combine_ln_kernel,
        grid=(n // tm,),
        in_specs=[pl.BlockSpec((tm, d), row), pl.BlockSpec((tm, d), row), pl.BlockSpec((tm, d), row),
                  pl.BlockSpec((tm, LANES), row), pl.BlockSpec((1, d), fixed), pl.BlockSpec((1, d), fixed)],
        out_specs=[pl.BlockSpec((tm, d), row), pl.BlockSpec((tm, d), row)],
        out_shape=[jax.ShapeDtypeStruct((n, d), F32), jax.ShapeDtypeStruct((n, d), BF16)],
        compiler_params=_cparams(("parallel",)),
        name="moe_combine_ln",
    )(x, y0, y1, rt, g.reshape(1, d), b.reshape(1, d))


def _moe_layout(rt, n):
    e_flat = rt[:, 0:TOP_K].astype(I32).reshape(-1)
    nk = n * TOP_K
    onehot = (e_flat[:, None] == jnp.arange(N_EXPERTS, dtype=I32)[None, :]).astype(I32)
    rank = jnp.take_along_axis(jnp.cumsum(onehot, axis=0), e_flat[:, None], axis=1)[:, 0] - 1
    counts = jnp.sum(onehot, axis=0)
    padded = (counts + MOE_TM - 1) // MOE_TM * MOE_TM
    pad_end = jnp.cumsum(padded)
    pad_start = pad_end - padded
    grp_start = jnp.cumsum(counts) - counts
    slot = pad_start[e_flat] + rank
    n_blocks = -(-nk // MOE_TM) + N_EXPERTS
    n_slots = n_blocks * MOE_TM
    order = jnp.argsort(e_flat, stable=True).astype(I32)
    sl = jnp.arange(n_slots, dtype=I32)
    slot_e = jnp.minimum(jnp.searchsorted(pad_end, sl, side='right'), N_EXPERTS - 1).astype(I32)
    within = sl - pad_start[slot_e]
    valid = within < counts[slot_e]
    src = jnp.where(valid, grp_start[slot_e] + within, 0)
    slot_tok = jnp.where(valid, order[src] // TOP_K, 0)
    n_used = (pad_end[-1] // MOE_TM).astype(I32).reshape(1)
    blk = jnp.arange(n_blocks, dtype=I32)
    block_e = slot_e[jnp.minimum(blk, n_used[0] - 1) * MOE_TM]
    return slot_tok, slot.reshape(n, TOP_K), block_e, n_used


def _pair_perm(n_heads):
    half = n_heads // 2
    cols = []
    for p in range(half):
        cols += list(range(p * HEAD_DIM, (p + 1) * HEAD_DIM))
        cols += list(range((half + p) * HEAD_DIM, (half + p + 1) * HEAD_DIM))
    return np.asarray(cols, dtype=np.int32)


def _pad_cols(w, width):
    return jnp.pad(w, ((0, 0), (0, width - w.shape[1])))


def _even_layer(x, xb, w_in, w_out, lam_params, subln, lam_init, wg, wu, wd, ln, tabs, bsz, seq_len):
    n, d = x.shape
    perm = _pair_perm(B_HEADS)
    aq, ak, av = w_in[:, 0:512], w_in[:, 512:1024], w_in[:, 1024:1536]
    bq, bk, bv = w_in[:, 1536:2048], w_in[:, 2048:2176], w_in[:, 2176:2304]
    iq, ik, iw = w_in[:, 2304:2560], w_in[:, 2560:2624], w_in[:, 2624:2628]
    w_rope = jnp.concatenate([aq, ak, bq[:, perm], iq, bk, ik, ik], axis=1).astype(BF16)
    w_plain = jnp.concatenate([av, bv], axis=1).astype(BF16)
    w_iw = _pad_cols(iw, LANES).astype(BF16)
    rp = _matmul(xb, w_rope, BF16, 512, 1024, seq_len, tabs).reshape(bsz, seq_len, -1)
    pp = _matmul(xb, w_plain, BF16, 512, w_plain.shape[1]).reshape(bsz, seq_len, -1)
    iwv = _matmul(xb, w_iw, F32, 512, LANES).reshape(bsz, seq_len, LANES)
    o_a = _diff_attention(rp, pp, lam_params, subln, lam_init)
    o_b = _dsa_attention(rp, pp, iwv)
    wo_a = w_out[0:512].astype(BF16)
    wo_b = w_out[512:1024][perm].astype(BF16)
    g_mix, b_mix, g_ffn, b_ffn = ln
    x1, x1b = _outproj_ln(x, o_a.reshape(n, -1), o_b.reshape(n, -1), wo_a, wo_b, g_mix, b_mix)
    return _ffn_ln(x1, x1b, wg.astype(BF16), wu.astype(BF16), wd.astype(BF16), g_ffn, b_ffn)


def _odd_layer(x, xb, w_in, w_out, gate_b, pe, phi_w1, phi_w2, w_router, b_router, wg, wu, wd, ln, tabs,
               bsz, seq_len):
    n, d = x.shape
    perm = _pair_perm(D_HEADS)
    cq, ck, cv = w_in[:, 0:512], w_in[:, 512:1024], w_in[:, 1024:1536]
    dq = w_in[:, 1536:2048][:, perm]
    dkc, dvc, dks = w_in[:, 2048:2176], w_in[:, 2176:2304], w_in[:, 2304:2432]
    dvs, dkw, dvw = w_in[:, 2432:2560], w_in[:, 2560:2688], w_in[:, 2688:2816]
    dg = w_in[:, 2816:2840]
    w_rope = jnp.concatenate([cq, ck, dq, dks, dkw], axis=1).astype(BF16)
    w_plain = jnp.concatenate([cv, dq, dkc, dvc, dvs, dvw], axis=1).astype(BF16)
    w_dg = _pad_cols(dg, LANES).astype(BF16)
    rp = _matmul(xb, w_rope, BF16, 512, w_rope.shape[1] // 2, seq_len, tabs).reshape(bsz, seq_len, -1)
    pp = _matmul(xb, w_plain, BF16, 512, w_plain.shape[1] // 2).reshape(bsz, seq_len, -1)
    dgv = _matmul(xb, w_dg, F32, 512, LANES).reshape(bsz, seq_len, LANES)

    o_c = _moba_attention(rp, pp)

    nc = seq_len // NSA_CMP_STRIDE
    tok = pp[:, :, 8 * LANES:10 * LANES].reshape(bsz, nc, NSA_CMP_STRIDE, 4, HEAD_DIM)
    r = tok.transpose(0, 3, 1, 2, 4).reshape(bsz, 4, nc, NSA_CMP_STRIDE * HEAD_DIM)
    pe_flat = jnp.pad(pe.reshape(2, 1, -1), ((0, 0), (0, 7), (0, 0))).astype(BF16)
    cmp = _nsa_compress(r, pe_flat, phi_w1.astype(BF16), phi_w2.astype(BF16))
    kcmp = jnp.concatenate([cmp[:, 0], cmp[:, 1]], axis=-1)
    vcmp = jnp.concatenate([cmp[:, 2], cmp[:, 3]], axis=-1)
    gb = _pad_cols(gate_b.reshape(1, -1), LANES)
    o_d = _nsa_attention(rp, pp, dgv, gb, kcmp, vcmp)

    wo_c = w_out[0:512].astype(BF16)
    wo_d = w_out[512:1024][perm].astype(BF16)
    g_mix, b_mix, g_ffn, b_ffn = ln
    x1, x1b = _outproj_ln(x, o_c.reshape(n, -1), o_d.reshape(n, -1), wo_c, wo_d, g_mix, b_mix)

    rt = _router(x1, _pad_cols(w_router, LANES), _pad_cols(b_router.reshape(1, -1), LANES))
    slot_tok, slot, block_e, n_used = _moe_layout(rt, n)
    y_slots = _moe_ffn(x1b[slot_tok], block_e, n_used, wg.astype(BF16), wu.astype(BF16), wd.astype(BF16))
    return _combine_ln(x1, y_slots[slot[:, 0]], y_slots[slot[:, 1]], rt, g_ffn, b_ffn)


@jax.jit
def kernel(x, ev_w_in, ev_w_out, dif_lambda, dif_subln, ffd_w_gate, ffd_w_up, ffd_w_down, od_w_in, od_w_out,
           nsa_gate_b, nsa_pe, nsa_phi_w1, nsa_phi_w2, moe_w_router, moe_b_router, moe_w_gate, moe_w_up,
           moe_w_down, ln_mix_g, ln_mix_b, ln_ffn_g, ln_ffn_b):
    bsz, seq_len, d = x.shape
    tabs = _rope_tables(seq_len)
    xf = x.reshape(bsz * seq_len, d)
    xb = xf.astype(BF16)
    for l in range(DEPTH):
        i = l // 2
        ln = (ln_mix_g[l], ln_mix_b[l], ln_ffn_g[l], ln_ffn_b[l])
        if l % 2 == 0:
            lam_init = 0.8 - 0.6 * math.exp(-0.3 * l)
            xf, xb = _even_layer(xf, xb, ev_w_in[i], ev_w_out[i], dif_lambda[i], dif_subln[i], lam_init,
                                 ffd_w_gate[i], ffd_w_up[i], ffd_w_down[i], ln, tabs, bsz, seq_len)
        else:
            xf, xb = _odd_layer(xf, xb, od_w_in[i], od_w_out[i], nsa_gate_b[i], nsa_pe[i], nsa_phi_w1[i],
                                nsa_phi_w2[i], moe_w_router[i], moe_b_router[i], moe_w_gate[i], moe_w_up[i],
                                moe_w_down[i], ln, tabs, bsz, seq_len)
    return xf.reshape(bsz, seq_len, d)
```

```python
import functools
import math

import numpy as np
import jax
import jax.numpy as jnp
from jax import lax
from jax.experimental import pallas as pl
from jax.experimental.pallas import tpu as pltpu

F32 = jnp.float32
BF16 = jnp.bfloat16
I32 = jnp.int32

LANES = 128
VMEM_LIMIT = 56 * 1024 * 1024

DEPTH = 4
HEAD_DIM = 64
ROPE_THETA = 10000.0
LN_EPS = 1e-5
DN_ALPHA = (2 * DEPTH) ** 0.25
SCALE = HEAD_DIM ** -0.5 * math.log2(math.e)
NEG = -1e30
BIG = 1e30
M_INIT = -1e30
NINF = float("-inf")

A_HEADS = 4
B_HEADS = 8
IDX_HEADS = 4
DSA_TOPK = 256
C_HEADS = 8
MOBA_BLOCK = 256
MOBA_TOPK = 3
D_HEADS = 8
NSA_CMP_LEN = 32
NSA_CMP_STRIDE = 16
NSA_SLC_BLOCK = 64
NSA_SLC_TOPK = 16
NSA_WINDOW = 512
NSA_PHI_HIDDEN = 256
N_EXPERTS = 8
TOP_K = 2
MOE_TM = 512
N_BISECT = 12


def _cparams(sem):
    return pltpu.CompilerParams(dimension_semantics=sem, vmem_limit_bytes=VMEM_LIMIT)


def _dot_nt(a, b):
    return lax.dot_general(a, b, (((1,), (1,)), ((), ())), preferred_element_type=F32)


def _layer_norm(y, g, b):
    mu = jnp.mean(y, axis=-1, keepdims=True)
    yc = y - mu
    var = jnp.mean(yc * yc, axis=-1, keepdims=True)
    return yc * lax.rsqrt(var + LN_EPS) * g + b


def _safe_recip(l):
    return jnp.where(l > 0.0, 1.0 / jnp.where(l > 0.0, l, 1.0), 0.0)


def _half_masks():
    lane = lax.broadcasted_iota(I32, (1, LANES), 1)
    return lane < HEAD_DIM


def _split_halves(t):
    lo = _half_masks()
    z = jnp.zeros_like(t)
    return jnp.where(lo, t, z), jnp.where(lo, z, t)


def _mm_kernel(x_ref, w_ref, *rest, rope):
    o_ref = rest[-1]
    acc = jnp.dot(x_ref[...].astype(BF16), w_ref[...], preferred_element_type=F32)
    if not rope:
        o_ref[...] = acc.astype(o_ref.dtype)
        return
    cos = rest[0][...]
    sin = rest[1][...]
    lane = lax.broadcasted_iota(I32, (1, LANES), 1)
    first = (lane % HEAD_DIM) < (HEAD_DIM // 2)
    for c in range(acc.shape[1] // LANES):
        a = acc[:, c * LANES:(c + 1) * LANES]
        rot = jnp.where(first, pltpu.roll(a, LANES - HEAD_DIM // 2, 1), pltpu.roll(a, HEAD_DIM // 2, 1))
        o_ref[:, c * LANES:(c + 1) * LANES] = (a * cos + rot * sin).astype(o_ref.dtype)


def _matmul(x, w, out_dtype, tm, tn, seq_len=None, rope_tabs=None):
    n, d = x.shape
    p = w.shape[1]
    rope = rope_tabs is not None
    in_specs = [pl.BlockSpec((tm, d), lambda i, j: (i, 0)),
                pl.BlockSpec((d, tn), lambda i, j: (0, j))]
    args = [x, w]
    if rope:
        nt = seq_len // tm
        in_specs += [pl.BlockSpec((tm, LANES), lambda i, j: (i % nt, 0))] * 2
        args += list(rope_tabs)
    return pl.pallas_call(
        functools.partial(_mm_kernel, rope=rope),
        grid=(n // tm, p // tn),
        in_specs=in_specs,
        out_specs=pl.BlockSpec((tm, tn), lambda i, j: (i, j)),
        out_shape=jax.ShapeDtypeStruct((n, p), out_dtype),
        compiler_params=_cparams(("parallel", "arbitrary")),
        name="proj_rope" if rope else "proj",
    )(*args)


def _rope_tables(seq_len):
    d = HEAD_DIM
    inv = ROPE_THETA ** (-jnp.arange(0, d, 2, dtype=F32) / d)
    ang = jnp.arange(seq_len, dtype=I32).astype(F32)[:, None] * inv[None, :]
    cos = jnp.cos(ang)
    sin = jnp.sin(ang)
    cos128 = jnp.tile(cos, (1, LANES // (d // 2)))
    sin128 = jnp.tile(jnp.concatenate([-sin, sin], axis=1), (1, LANES // d))
    return cos128, sin128


def _outproj_ln_kernel(x_ref, a_ref, b_ref, wa_ref, wb_ref, g_ref, bb_ref, xo_ref, xb_ref):
    mix = (jnp.dot(a_ref[...], wa_ref[...], preferred_element_type=F32)
           + jnp.dot(b_ref[...], wb_ref[...], preferred_element_type=F32))
    y = _layer_norm(DN_ALPHA * x_ref[...] + mix, g_ref[...], bb_ref[...])
    xo_ref[...] = y
    xb_ref[...] = y.astype(BF16)


def _outproj_ln(x, oa, ob, wa, wb, g, b, tm=512):
    n, d = x.shape
    ka, kb = oa.shape[1], ob.shape[1]
    row = lambda i: (i, 0)
    fixed = lambda i: (0, 0)
    return pl.pallas_call(
        _outproj_ln_kernel,
        grid=(n // tm,),
        in_specs=[pl.BlockSpec((tm, d), row), pl.BlockSpec((tm, ka), row), pl.BlockSpec((tm, kb), row),
                  pl.BlockSpec((ka, d), fixed), pl.BlockSpec((kb, d), fixed),
                  pl.BlockSpec((1, d), fixed), pl.BlockSpec((1, d), fixed)],
        out_specs=[pl.BlockSpec((tm, d), row), pl.BlockSpec((tm, d), row)],
        out_shape=[jax.ShapeDtypeStruct((n, d), F32), jax.ShapeDtypeStruct((n, d), BF16)],
        compiler_params=_cparams(("parallel",)),
        name="outproj_ln",
    )(x, oa, ob, wa, wb, g.reshape(1, d), b.reshape(1, d))


def _ffn_ln_kernel(x_ref, xb_ref, wg_ref, wu_ref, wd_ref, g_ref, b_ref, xo_ref, xob_ref, acc_ref):
    f = pl.program_id(1)
    xb = xb_ref[...]
    h = jax.nn.silu(jnp.dot(xb, wg_ref[...], preferred_element_type=F32)) * jnp.dot(
        xb, wu_ref[...], preferred_element_type=F32)
    contrib = jnp.dot(h.astype(BF16), wd_ref[...], preferred_element_type=F32)

    @pl.when(f == 0)
    def _():
        acc_ref[...] = contrib

    @pl.when(f > 0)
    def _():
        acc_ref[...] += contrib

    @pl.when(f == pl.num_programs(1) - 1)
    def _():
        y = _layer_norm(DN_ALPHA * x_ref[...] + acc_ref[...], g_ref[...], b_ref[...])
        xo_ref[...] = y
        xob_ref[...] = y.astype(BF16)


def _ffn_ln(x, xb, wg, wu, wd, g, b, tm=512, tf=1408):
    n, d = x.shape
    fdim = wg.shape[1]
    return pl.pallas_call(
        _ffn_ln_kernel,
        grid=(n // tm, fdim // tf),
        in_specs=[pl.BlockSpec((tm, d), lambda i, f: (i, 0)), pl.BlockSpec((tm, d), lambda i, f: (i, 0)),
                  pl.BlockSpec((d, tf), lambda i, f: (0, f)), pl.BlockSpec((d, tf), lambda i, f: (0, f)),
                  pl.BlockSpec((tf, d), lambda i, f: (f, 0)),
                  pl.BlockSpec((1, d), lambda i, f: (0, 0)), pl.BlockSpec((1, d), lambda i, f: (0, 0))],
        out_specs=[pl.BlockSpec((tm, d), lambda i, f: (i, 0)), pl.BlockSpec((tm, d), lambda i, f: (i, 0))],
        out_shape=[jax.ShapeDtypeStruct((n, d), F32), jax.ShapeDtypeStruct((n, d), BF16)],
        scratch_shapes=[pltpu.VMEM((tm, d), F32)],
        compiler_params=_cparams(("parallel", "arbitrary")),
        name="ffn_ln",
    )(x, xb, wg, wu, wd, g.reshape(1, d), b.reshape(1, d))


def _top_n_mask(v, n, axis):
    idx = lax.broadcasted_iota(I32, v.shape, axis).astype(F32)
    sel = jnp.zeros(v.shape, F32)
    for _ in range(n):
        mx = jnp.max(v, axis=axis, keepdims=True)
        first = jnp.min(jnp.where(v == mx, idx, float(v.shape[axis])), axis=axis, keepdims=True)
        pick = idx == first
        sel = jnp.where(pick, 1.0, sel)
        v = jnp.where(pick, NINF, v)
    return sel


DV_PAD = 16
TQ = 256


def _with_ones(vt):
    return jnp.concatenate([vt, jnp.ones((DV_PAD, vt.shape[1]), vt.dtype)], axis=0)


def _mm_t_kernel(x_ref, w_ref, o_ref, *, tk):
    acc = jnp.dot(x_ref[...].astype(BF16), w_ref[...], preferred_element_type=F32)
    for cc in range(acc.shape[0] // tk):
        o_ref[cc] = acc[cc * tk:(cc + 1) * tk, :].T.astype(o_ref.dtype)


def _matmul_t(x, w, tm, tk):
    n, d = x.shape
    p = w.shape[1]
    return pl.pallas_call(
        functools.partial(_mm_t_kernel, tk=tk),
        grid=(n // tm,),
        in_specs=[pl.BlockSpec((tm, d), lambda i: (i, 0)), pl.BlockSpec((d, p), lambda i: (0, 0))],
        out_specs=pl.BlockSpec((tm // tk, p, tk), lambda i: (i, 0, 0)),
        out_shape=jax.ShapeDtypeStruct((n // tk, p, tk), BF16),
        compiler_params=_cparams(("parallel",)),
        name="proj_t",
    )(x, w)


def _normalize_t(acc, width):
    return acc[:width] * _safe_recip(acc[width:width + 1])


def _pipe_flash(n, ns, qk, vt_at, bias_at, bufs, dv, tk, tq):
    sa, sb, pa, pb = bufs
    for j in range(ns):
        sa[j] = qk(0, j)
        pb[j] = jnp.zeros((tk, tq), BF16)

    def half(c, carry, s_cur, s_nxt, p_prev, p_cur):
        nxt = jnp.minimum(c + 1, n - 1)
        for j in range(ns):
            s_nxt[j] = qk(nxt, j)
        cp = jnp.clip(c - 1, 0, n - 1)
        out = []
        for j in range(ns):
            m, acc, alpha = carry[j]
            acc = alpha * acc + jnp.dot(vt_at(cp, j), p_prev[j], preferred_element_type=F32)
            st = s_cur[j] + bias_at(c, j)
            m_new = jnp.maximum(m, jnp.max(st, axis=0, keepdims=True))
            alpha = jnp.exp2(m - m_new)
            p_cur[j] = jnp.exp2((st - m_new).astype(BF16))
            out.append((m_new, acc, alpha))
        return tuple(out)

    def body(t, carry):
        carry = half(2 * t, carry, sa, sb, pb, pa)
        return half(2 * t + 1, carry, sb, sa, pa, pb)

    init = (jnp.full((1, tq), M_INIT, F32), jnp.zeros((dv, tq), F32), jnp.ones((1, tq), F32))
    trips = (n + 1) // 2
    carry = lax.fori_loop(0, trips, body, (init,) * ns)
    cl = jnp.minimum(2 * trips - 1, n - 1)
    outs = []
    for j in range(ns):
        _, acc, alpha = carry[j]
        outs.append(alpha * acc + jnp.dot(vt_at(cl, j), pb[j], preferred_element_type=F32))
    return outs


def _pipe_scratch(ns, tk, tq):
    return [pltpu.VMEM((ns, tk, tq), F32)] * 2 + [pltpu.VMEM((ns, tk, tq), BF16)] * 2


def _causal_t(t):
    return jnp.where(lax.broadcasted_iota(I32, (t, t), 0) <= lax.broadcasted_iota(I32, (t, t), 1), 0.0, NINF)


def _diff_kernel(lam_ref, sub_ref, q_ref, k_ref, vt_ref, o_ref, *bufs, tq, lam_init):
    i = pl.program_id(2)
    lp = lam_ref[...]
    lam = (jnp.exp(jnp.sum(lp[0:1] * lp[1:2], axis=1, keepdims=True))
           - jnp.exp(jnp.sum(lp[2:3] * lp[3:4], axis=1, keepdims=True)) + lam_init)
    qs = _split_halves(q_ref[...])
    n = i + 1
    bufs, tab_ref = bufs[:4], bufs[4]
    tab_ref[0] = jnp.zeros((tq, tq), F32)
    tab_ref[1] = _causal_t(tq)
    tab_ref[2] = jnp.full((tq, tq), NINF, F32)

    def qk(c, j):
        off = pl.multiple_of(c * tq, tq)
        return _dot_nt(k_ref[pl.ds(off, tq), :], qs[j])

    def bias_at(c, j):
        return tab_ref[jnp.where(c == i, 1, jnp.where(c < n, 0, 2))]

    outs = _pipe_flash(n, 2, qk, lambda c, j: _with_ones(vt_ref[c]), bias_at, bufs, LANES + DV_PAD, tq, tq)
    o = _normalize_t(outs[0], LANES) - lam * _normalize_t(outs[1], LANES)
    o = o * lax.rsqrt(jnp.mean(o * o, axis=0, keepdims=True) + LN_EPS)
    o = o * sub_ref[...] * (1.0 - lam_init)
    o_ref[...] = o.T.astype(o_ref.dtype)


def _diff_attention(rp, vt, lam_params, subln, lam_init, tq=TQ):
    b, s, _ = rp.shape
    nk = s // tq
    return pl.pallas_call(
        functools.partial(_diff_kernel, tq=tq, lam_init=lam_init),
        grid=(b, A_HEADS, nk),
        in_specs=[pl.BlockSpec((4, HEAD_DIM), lambda bi, h, i: (0, 0)),
                  pl.BlockSpec((LANES, 1), lambda bi, h, i: (0, 0)),
                  pl.BlockSpec((None, tq, LANES), lambda bi, h, i: (bi, i, h)),
                  pl.BlockSpec((None, s, LANES), lambda bi, h, i: (bi, 0, A_HEADS + h)),
                  pl.BlockSpec((None, nk, LANES, tq), lambda bi, h, i: (bi, 0, h, 0))],
        out_specs=pl.BlockSpec((None, tq, LANES), lambda bi, h, i: (bi, i, h)),
        out_shape=jax.ShapeDtypeStruct((b, s, A_HEADS * LANES), BF16),
        scratch_shapes=_pipe_scratch(2, tq, tq) + [pltpu.VMEM((3, tq, tq), F32)],
        compiler_params=_cparams(("parallel", "parallel", "arbitrary")),
        name="diff_attn",
    )(lam_params, subln.reshape(LANES, 1), rp, rp, vt)


def _fold8(x, op):
    acc = x[0:8]
    for r in range(1, x.shape[0] // 8):
        acc = op(acc, x[r * 8:(r + 1) * 8])
    return acc


def _dsa_kernel(iq_ref, ikk_ref, iw_ref, q_ref, k_ref, vt_ref, o_ref, s_ref, j_ref, *bufs, tq, ksel, seq_len):
    tk = tq
    i = pl.program_id(1)
    nch = i + 1
    ksel_f = float(ksel)
    k_loc = lax.broadcasted_iota(I32, (tk, tq), 0)
    q_loc = lax.broadcasted_iota(I32, (tk, tq), 1)
    qpos = i * tq + lax.broadcasted_iota(I32, (1, tq), 1)

    iq = iq_ref[...]
    iwt = iw_ref[...].T
    iqh = []
    for pair in range(IDX_HEADS // 2):
        iqh += list(_split_halves(iq[:, pair * LANES:(pair + 1) * LANES]))

    def scores(c):
        off = pl.multiple_of(c * tk, tk)
        kk = ikk_ref[pl.ds(off, tk), :]
        sc = iwt[0:1] * jnp.maximum(_dot_nt(kk, iqh[0]), 0.0)
        for h in range(1, IDX_HEADS):
            sc = sc + iwt[h:h + 1] * jnp.maximum(_dot_nt(kk, iqh[h]), 0.0)
        return sc

    def full_body(c, carry):
        mx, mn = carry
        sc = scores(c)
        s_ref[c] = sc
        return jnp.maximum(mx, _fold8(sc, jnp.maximum)), jnp.minimum(mn, _fold8(sc, jnp.minimum))

    mx, mn = lax.fori_loop(0, i, full_body, (jnp.full((8, tq), -BIG, F32), jnp.full((8, tq), BIG, F32)))
    sc = scores(i)
    causal = k_loc <= q_loc
    s_ref[i] = jnp.where(causal, sc, NEG)
    mx = jnp.maximum(mx, _fold8(jnp.where(causal, sc, -BIG), jnp.maximum))
    mn = jnp.minimum(mn, _fold8(jnp.where(causal, sc, BIG), jnp.minimum))
    smax = jnp.max(mx, axis=0, keepdims=True)
    smin = jnp.min(mn, axis=0, keepdims=True)
    s_ref[nch] = jnp.full((tk, tq), NINF, F32)

    def count_where(ind):
        def body(c, acc):
            return acc + _fold8(ind(s_ref[c], c * tk + k_loc), jnp.add)
        acc = lax.fori_loop(0, nch, body, jnp.zeros((8, tq), F32))
        return jnp.sum(acc, axis=0, keepdims=True)

    def count_ge(th):
        return count_where(lambda x, kidx: jnp.where(x >= th, 1.0, 0.0))

    def max_below(th):
        def body(c, acc):
            x = s_ref[c]
            return jnp.maximum(acc, _fold8(jnp.where(x < th, x, NINF), jnp.maximum))
        acc = lax.fori_loop(0, nch, body, jnp.full((8, tq), NINF, F32))
        return jnp.max(acc, axis=0, keepdims=True)

    n_causal = (qpos + 1).astype(F32)
    take_all = n_causal <= ksel_f
    done0 = jnp.where(take_all | (count_ge(smax) >= ksel_f), 1.0, 0.0)

    def bisect(lo, hi):
        mid = lo + (hi - lo) * 0.5
        ge = count_ge(mid) >= ksel_f
        return jnp.where(ge, mid, lo), jnp.where(ge, hi, mid)

    lo, hi = lax.fori_loop(0, N_BISECT, lambda _, c: bisect(*c), (smin, smax))

    def snap_body(carry):
        lo, hi, th, done, _ = carry
        lo, hi = bisect(lo, hi)
        t1 = max_below(hi)
        ok = count_ge(t1) >= ksel_f
        th = jnp.where(done > 0.0, th, t1)
        hi = jnp.where(ok, hi, t1)
        done = jnp.where(ok, 1.0, done)
        return lo, hi, th, done, jnp.sum(1.0 - done)

    _, _, th, _, _ = lax.while_loop(lambda c: c[4] > 0.0, snap_body,
                                    (lo, hi, smax, done0, jnp.sum(1.0 - done0)))

    c_ge = count_ge(th)
    need_tb = jnp.where(take_all, 0.0, jnp.where(c_ge > ksel_f, 1.0, 0.0))
    j_ref[...] = jnp.full((8, tq), seq_len - 1, I32)

    @pl.when(jnp.sum(need_tb) > 0.0)
    def _():
        need = ksel_f - count_where(lambda x, kidx: jnp.where(x > th, 1.0, 0.0))

        def jb(_, carry):
            lo_j, hi_j = carry
            mid = (lo_j + hi_j) // 2
            cnt = count_where(lambda x, kidx: jnp.where(x == th, jnp.where(kidx <= mid, 1.0, 0.0), 0.0))
            ge = cnt >= need
            return jnp.where(ge, lo_j, mid), jnp.where(ge, mid, hi_j)

        n_it = int(math.ceil(math.log2(seq_len))) + 1
        _, hi_j = lax.fori_loop(0, n_it, jb, (jnp.full((1, tq), -1, I32), jnp.full((1, tq), seq_len - 1, I32)))
        j_ref[...] = jnp.broadcast_to(hi_j, (8, tq))

    jsel = j_ref[0:1, :]

    def bias_body(c, _):
        x = s_ref[c]
        kidx = c * tk + k_loc
        keep = jnp.where(x > th, 0.0, jnp.where(x == th, jnp.where(kidx <= jsel, 0.0, NINF), NINF))
        keep = jnp.where(take_all, 0.0, keep)
        s_ref[c] = jnp.where(kidx <= qpos, keep, NINF)
        return 0

    lax.fori_loop(0, nch, bias_body, 0)

    for p in range(B_HEADS // 2):
        qs = _split_halves(q_ref[:, p * LANES:(p + 1) * LANES])

        def qk(c, j, qs=qs):
            off = pl.multiple_of(c * tk, tk)
            return _dot_nt(k_ref[pl.ds(off, tk), :], qs[j])

        outs = _pipe_flash(nch, 2, qk, lambda c, j: _with_ones(vt_ref[c, j * HEAD_DIM:(j + 1) * HEAD_DIM, :]),
                           lambda c, j: s_ref[jnp.minimum(c, nch)], bufs, HEAD_DIM + DV_PAD, tk, tq)
        o = jnp.concatenate([_normalize_t(outs[0], HEAD_DIM), _normalize_t(outs[1], HEAD_DIM)], axis=0)
        o_ref[:, p * LANES:(p + 1) * LANES] = o.T.astype(o_ref.dtype)


def _dsa_attention(rp, vt, iw, tq=TQ):
    b, s, _ = rp.shape
    ksel = min(DSA_TOPK, s // 4)
    nk = s // tq
    return pl.pallas_call(
        functools.partial(_dsa_kernel, tq=tq, ksel=ksel, seq_len=s),
        grid=(b, nk),
        in_specs=[pl.BlockSpec((None, tq, 2 * LANES), lambda bi, i: (bi, i, 6)),
                  pl.BlockSpec((None, s, LANES), lambda bi, i: (bi, 0, 15)),
                  pl.BlockSpec((None, tq, LANES), lambda bi, i: (bi, i, 0)),
                  pl.BlockSpec((None, tq, 4 * LANES), lambda bi, i: (bi, i, 2)),
                  pl.BlockSpec((None, s, LANES), lambda bi, i: (bi, 0, 14)),
                  pl.BlockSpec((None, nk, LANES, tq), lambda bi, i: (bi, 0, 4, 0))],
        out_specs=pl.BlockSpec((None, tq, 4 * LANES), lambda bi, i: (bi, i, 0)),
        out_shape=jax.ShapeDtypeStruct((b, s, 4 * LANES), BF16),
        scratch_shapes=[pltpu.VMEM((nk + 1, tq, tq), F32), pltpu.VMEM((8, tq), I32)] + _pipe_scratch(2, tq, tq),
        compiler_params=_cparams(("parallel", "arbitrary")),
        name="dsa_attn",
    )(rp, rp, iw, rp, rp, vt)


def _moba_kernel(q_ref, k_ref, vt_ref, o_ref, km_ref, sel_ref, *bufs, seq_len, n_sel):
    tq = MOBA_BLOCK
    qb = pl.program_id(2)

    @pl.when(qb == 0)
    def _():
        j = lax.broadcasted_iota(I32, (LANES, seq_len), 0)
        s = lax.broadcasted_iota(I32, (LANES, seq_len), 1)
        avg = jnp.where(s // MOBA_BLOCK == j, 1.0 / MOBA_BLOCK, 0.0).astype(BF16)
        km_ref[...] = jnp.dot(avg, k_ref[...], preferred_element_type=F32)

    nbp = sel_ref.shape[1]
    km = km_ref[0:nbp, :]
    qs = _split_halves(q_ref[...])
    blk = lax.broadcasted_iota(I32, (nbp, tq), 0)
    past = blk < qb
    for j in range(2):
        gate = lax.dot_general(km, qs[j].astype(F32), (((1,), (1,)), ((), ())),
                               precision=lax.Precision.HIGHEST, preferred_element_type=F32)
        gate = jnp.where(blk < seq_len // MOBA_BLOCK, jnp.where(past, gate, NEG), NINF)
        sel = _top_n_mask(gate, n_sel, 0)
        sel_ref[j] = jnp.where(past, jnp.where(sel > 0.5, 0.0, NINF), NINF)
    own = _causal_t(tq)
    n = qb + 1

    def qk(c, j):
        off = pl.multiple_of(c * tq, tq)
        return _dot_nt(k_ref[pl.ds(off, tq), :], qs[j])

    def bias_at(c, j):
        chosen = sel_ref[j, pl.ds(jnp.minimum(c, nbp - 1), 1), :]
        return jnp.where(c == qb, own, jnp.where(c < n, chosen, NINF))

    outs = _pipe_flash(n, 2, qk, lambda c, j: _with_ones(vt_ref[c, j * HEAD_DIM:(j + 1) * HEAD_DIM, :]), bias_at, bufs,
                       HEAD_DIM + DV_PAD, tq, tq)
    o = jnp.concatenate([_normalize_t(outs[0], HEAD_DIM), _normalize_t(outs[1], HEAD_DIM)], axis=0)
    o_ref[...] = o.T.astype(o_ref.dtype)


def _moba_attention(rp, vt):
    b, s, _ = rp.shape
    tq = MOBA_BLOCK
    nb = s // tq
    n_sel = max(1, min(MOBA_TOPK, nb - 1))
    npair = C_HEADS // 2
    return pl.pallas_call(
        functools.partial(_moba_kernel, seq_len=s, n_sel=n_sel),
        grid=(b, npair, nb),
        in_specs=[pl.BlockSpec((None, tq, LANES), lambda bi, h, i: (bi, i, h)),
                  pl.BlockSpec((None, s, LANES), lambda bi, h, i: (bi, 0, npair + h)),
                  pl.BlockSpec((None, nb, LANES, tq), lambda bi, h, i: (bi, 0, h, 0))],
        out_specs=pl.BlockSpec((None, tq, LANES), lambda bi, h, i: (bi, i, h)),
        out_shape=jax.ShapeDtypeStruct((b, s, npair * LANES), BF16),
        scratch_shapes=[pltpu.VMEM((LANES, LANES), F32), pltpu.VMEM((2, -(-nb // 8) * 8, tq), F32)]
        + _pipe_scratch(2, tq, tq),
        compiler_params=_cparams(("parallel", "parallel", "arbitrary")),
        name="moba_attn",
    )(rp, rp, vt)


def _cmp_kernel(r_ref, pe_ref, w1_ref, w2_ref, o_ref):
    r = r_ref[...]
    w1 = w1_ref[...]
    half = r.shape[1]
    u = jnp.dot(r, w1[:half], preferred_element_type=F32)
    v = jnp.dot(r, w1[half:], preferred_element_type=F32)
    c = jnp.dot(pe_ref[...], w1, preferred_element_type=F32)[0:1]
    pre = u + pltpu.roll(v, r.shape[0] - 1, 0) + c
    o_ref[...] = jnp.dot(jax.nn.gelu(pre).astype(BF16), w2_ref[...],
                         preferred_element_type=F32).astype(o_ref.dtype)


def _nsa_compress(r, pe, w1, w2):
    b, _, nc, wdt = r.shape
    hid = w1.shape[2]
    return pl.pallas_call(
        _cmp_kernel,
        grid=(b, 4),
        in_specs=[pl.BlockSpec((None, None, nc, wdt), lambda bi, t: (bi, t, 0, 0)),
                  pl.BlockSpec((None, 8, 2 * wdt), lambda bi, t: (t // 2, 0, 0)),
                  pl.BlockSpec((None, 2 * wdt, hid), lambda bi, t: (t // 2, 0, 0)),
                  pl.BlockSpec((None, hid, HEAD_DIM), lambda bi, t: (t // 2, 0, 0))],
        out_specs=pl.BlockSpec((None, None, nc, HEAD_DIM), lambda bi, t: (bi, t, 0, 0)),
        out_shape=jax.ShapeDtypeStruct((b, 4, nc, HEAD_DIM), BF16),
        compiler_params=_cparams(("parallel", "arbitrary")),
        name="nsa_compress",
    )(r, pe, w1, w2)


def _nsa_kernel(qr_ref, qw_ref, dg_ref, gb_ref, kc_ref, vct_ref, ks_ref, vst_ref, kw_ref, vwt_ref,
                o_ref, b_ref, wb_ref, *bufs, tq, seq_len):
    tk = tq
    i = pl.program_id(1)
    nch = i + 1
    nc = seq_len // NSA_CMP_STRIDE
    n_sb = seq_len // NSA_SLC_BLOCK
    n_sel = min(NSA_SLC_TOPK, n_sb)
    k_loc = lax.broadcasted_iota(I32, (tk, tq), 0)
    q_loc = lax.broadcasted_iota(I32, (tk, tq), 1)
    qpos = i * tq + lax.broadcasted_iota(I32, (1, tq), 1)

    gates_t = jax.nn.sigmoid(dg_ref[...] + gb_ref[...]).T
    kc = kc_ref[...]
    cmp_end = lax.broadcasted_iota(I32, (nc, 1), 0) * NSA_CMP_STRIDE + (NSA_CMP_LEN - 1)
    cbias = jnp.where(cmp_end <= qpos, 0.0, NINF)

    q_rot, o_cmp = [], []
    psum = [jnp.zeros((nc, tq), F32), jnp.zeros((nc, tq), F32)]
    for p in range(D_HEADS // 2):
        q_rot.append(_split_halves(qr_ref[:, p * LANES:(p + 1) * LANES]))
        for g, qh in enumerate(_split_halves(qw_ref[:, p * LANES:(p + 1) * LANES])):
            s = _dot_nt(kc, qh) + cbias
            m = jnp.max(s, axis=0, keepdims=True)
            e = jnp.exp2(s - jnp.where(m == NINF, 0.0, m))
            pc = e * _safe_recip(jnp.sum(e, axis=0, keepdims=True))
            psum[g] = psum[g] + pc
            o_cmp.append(jnp.dot(vct_ref[g], pc.astype(BF16), preferred_element_type=F32))

    nbp = -(-n_sb // 8) * 8
    cn = lax.broadcasted_iota(I32, (nbp, nc), 1) * NSA_CMP_STRIDE
    sj = lax.broadcasted_iota(I32, (nbp, nc), 0) * NSA_SLC_BLOCK
    shares = jnp.where((cn <= sj + NSA_SLC_BLOCK - 1) & (cn + NSA_CMP_LEN - 1 >= sj), 1.0, 0.0)
    blk = lax.broadcasted_iota(I32, (nbp, tq), 0)
    cur = qpos // NSA_SLC_BLOCK
    causal_b = blk <= cur
    forced = (blk == 0) | ((blk >= cur - 1) & causal_b)
    for g in range(2):
        imp = jnp.dot(shares, psum[g], precision=lax.Precision.HIGHEST, preferred_element_type=F32)
        val = jnp.where(forced, BIG, jnp.where(causal_b, imp, NEG))
        val = jnp.where(blk < n_sb, val, NINF)
        selb = _top_n_mask(val, n_sel, 0).astype(BF16)

        def expand(c, _, selb=selb, g=g):
            kidx = c * tk + k_loc
            e = jnp.where((c * tk + lax.broadcasted_iota(I32, (tk, nbp), 0)) // NSA_SLC_BLOCK
                          == lax.broadcasted_iota(I32, (tk, nbp), 1), 1.0, 0.0).astype(BF16)
            selk = jnp.dot(e, selb, preferred_element_type=F32)
            b_ref[g, c] = jnp.where(selk > 0.5, jnp.where(kidx <= qpos, 0.0, NINF), NINF)
            return 0

        lax.fori_loop(0, nch, expand, 0)
        b_ref[g, nch] = jnp.full((tk, tq), NINF, F32)

    wb_ref[0] = jnp.where(k_loc <= q_loc, 0.0, NINF)
    wb_ref[1] = jnp.zeros((tk, tq), F32)
    wb_ref[2] = jnp.where(k_loc > q_loc, 0.0, NINF)
    wb_ref[3] = jnp.full((tk, tq), NINF, F32)
    n_wc = NSA_WINDOW // tk + 1
    w_first = jnp.maximum(i - (n_wc - 1), 0)
    n_w = i - w_first + 1

    for p in range(D_HEADS // 2):
        qs = q_rot[p]

        def qk_s(c, j, qs=qs):
            off = pl.multiple_of(c * tk, tk)
            return _dot_nt(ks_ref[pl.ds(off, tk), :], qs[j])

        o_slc = _pipe_flash(nch, 2, qk_s, lambda c, j: _with_ones(vst_ref[c, j * HEAD_DIM:(j + 1) * HEAD_DIM, :]),
                            lambda c, j: b_ref[j, jnp.minimum(c, nch)], bufs, HEAD_DIM + DV_PAD, tk, tq)

        def qk_w(c, j, qs=qs):
            off = pl.multiple_of((w_first + c) * tk, tk)
            return _dot_nt(kw_ref[pl.ds(off, tk), :], qs[j])

        def bias_w(c, j):
            d = i - (w_first + c)
            return wb_ref[jnp.where(c < n_w, d, n_wc)]

        o_win = _pipe_flash(n_w, 2, qk_w,
                            lambda c, j: _with_ones(vwt_ref[w_first + c, j * HEAD_DIM:(j + 1) * HEAD_DIM, :]), bias_w, bufs,
                            HEAD_DIM + DV_PAD, tk, tq)
        outs = []
        for g in range(2):
            h = g * (D_HEADS // 2) + p
            outs.append(gates_t[3 * h:3 * h + 1] * o_cmp[2 * p + g]
                        + gates_t[3 * h + 1:3 * h + 2] * _normalize_t(o_slc[g], HEAD_DIM)
                        + gates_t[3 * h + 2:3 * h + 3] * _normalize_t(o_win[g], HEAD_DIM))
        o_ref[:, p * LANES:(p + 1) * LANES] = jnp.concatenate(outs, axis=0).T.astype(o_ref.dtype)


def _nsa_attention(rp, pp, vt, dg, gate_b, kcmp, vcmp, tq=TQ):
    b, s, _ = rp.shape
    nk = s // tq
    nc = s // NSA_CMP_STRIDE
    assert NSA_WINDOW == 2 * tq
    n_wc = NSA_WINDOW // tq + 1
    vct = vcmp.reshape(b, nc, 2, HEAD_DIM).transpose(0, 2, 3, 1)
    full = lambda t: pl.BlockSpec((None, s, LANES), lambda bi, i: (bi, 0, t))
    vspec = lambda t: pl.BlockSpec((None, nk, LANES, tq), lambda bi, i: (bi, 0, t, 0))
    return pl.pallas_call(
        functools.partial(_nsa_kernel, tq=tq, seq_len=s),
        grid=(b, nk),
        in_specs=[pl.BlockSpec((None, tq, 4 * LANES), lambda bi, i: (bi, i, 2)),
                  pl.BlockSpec((None, tq, 4 * LANES), lambda bi, i: (bi, i, 0)),
                  pl.BlockSpec((None, tq, LANES), lambda bi, i: (bi, i, 0)),
                  pl.BlockSpec((1, LANES), lambda bi, i: (0, 0)),
                  pl.BlockSpec((None, nc, LANES), lambda bi, i: (bi, 0, 0)),
                  pl.BlockSpec((None, 2, HEAD_DIM, nc), lambda bi, i: (bi, 0, 0, 0)),
                  full(12), vspec(4), full(13), vspec(5)],
        out_specs=pl.BlockSpec((None, tq, 4 * LANES), lambda bi, i: (bi, i, 0)),
        out_shape=jax.ShapeDtypeStruct((b, s, 4 * LANES), BF16),
        scratch_shapes=[pltpu.VMEM((2, nk + 1, tq, tq), F32), pltpu.VMEM((n_wc + 1, tq, tq), F32)]
        + _pipe_scratch(2, tq, tq),
        compiler_params=_cparams(("parallel", "arbitrary")),
        name="nsa_attn",
    )(rp, pp, dg, gate_b, kcmp, vct, rp, vt, rp, vt)


def _router_kernel(x_ref, w_ref, b_ref, o_ref):
    logits = jnp.dot(x_ref[...], w_ref[...], precision=lax.Precision.HIGHEST,
                     preferred_element_type=F32) + b_ref[...]
    lane = lax.broadcasted_iota(I32, (1, LANES), 1)
    lanef = lane.astype(F32)
    v = jnp.where(lane < N_EXPERTS, logits, NINF)
    l0 = jnp.max(v, axis=1, keepdims=True)
    i0 = jnp.min(jnp.where(v == l0, lanef, float(LANES)), axis=1, keepdims=True)
    v = jnp.where(lanef == i0, NINF, v)
    l1 = jnp.max(v, axis=1, keepdims=True)
    i1 = jnp.min(jnp.where(v == l1, lanef, float(LANES)), axis=1, keepdims=True)
    e1 = jnp.exp(l1 - l0)
    g0 = 1.0 / (1.0 + e1)
    g1 = e1 / (1.0 + e1)
    o_ref[...] = jnp.where(lane == 0, i0, jnp.where(lane == 1, i1, jnp.where(lane == 2, g0, jnp.where(
        lane == 3, g1, 0.0))))


def _router(x, w, b, tm=512):
    n, d = x.shape
    return pl.pallas_call(
        _router_kernel,
        grid=(n // tm,),
        in_specs=[pl.BlockSpec((tm, d), lambda i: (i, 0)), pl.BlockSpec((d, LANES), lambda i: (0, 0)),
                  pl.BlockSpec((1, LANES), lambda i: (0, 0))],
        out_specs=pl.BlockSpec((tm, LANES), lambda i: (i, 0)),
        out_shape=jax.ShapeDtypeStruct((n, LANES), F32),
        compiler_params=_cparams(("parallel",)),
        name="moe_router",
    )(x, w, b)


def _moe_ffn_kernel(be_ref, nu_ref, x_ref, wg_ref, wu_ref, wd_ref, o_ref, acc_ref):
    i = pl.program_id(0)
    f = pl.program_id(1)
    last = pl.num_programs(1) - 1
    used = i < nu_ref[0]

    @pl.when(used)
    def _():
        x = x_ref[...]
        h = jax.nn.silu(jnp.dot(x, wg_ref[...], preferred_element_type=F32)) * jnp.dot(
            x, wu_ref[...], preferred_element_type=F32)
        contrib = jnp.dot(h.astype(BF16), wd_ref[...], preferred_element_type=F32)

        @pl.when(f == 0)
        def _():
            acc_ref[...] = contrib

        @pl.when(f > 0)
        def _():
            acc_ref[...] += contrib

        @pl.when(f == last)
        def _():
            o_ref[...] = acc_ref[...]

    @pl.when(jnp.logical_not(used) & (f == last))
    def _():
        o_ref[...] = jnp.zeros(o_ref.shape, o_ref.dtype)


def _moe_ffn(x_sorted, block_e, n_used, wg, wu, wd, tf=1792):
    ns, d = x_sorted.shape
    fdim = wg.shape[2]
    nf = fdim // tf
    n_blocks = ns // MOE_TM

    def row_map(i, f, be, nu):
        return (jnp.minimum(i, nu[0] - 1), 0)

    def f_of(i, f, nu):
        return jnp.where(i < nu[0], f, nf - 1)

    grid_spec = pltpu.PrefetchScalarGridSpec(
        num_scalar_prefetch=2,
        grid=(n_blocks, nf),
        in_specs=[pl.BlockSpec((MOE_TM, d), row_map),
                  pl.BlockSpec((None, d, tf), lambda i, f, be, nu: (be[i], 0, f_of(i, f, nu))),
                  pl.BlockSpec((None, d, tf), lambda i, f, be, nu: (be[i], 0, f_of(i, f, nu))),
                  pl.BlockSpec((None, tf, d), lambda i, f, be, nu: (be[i], f_of(i, f, nu), 0))],
        out_specs=pl.BlockSpec((MOE_TM, d), lambda i, f, be, nu: (i, 0)),
        scratch_shapes=[pltpu.VMEM((MOE_TM, d), F32)],
    )
    return pl.pallas_call(
        _moe_ffn_kernel,
        grid_spec=grid_spec,
        out_shape=jax.ShapeDtypeStruct((ns, d), F32),
        compiler_params=_cparams(("arbitrary", "arbitrary")),
        name="moe_ffn",
    )(block_e, n_used, x_sorted, wg, wu, wd)


def _combine_ln_kernel(x_ref, y0_ref, y1_ref, rt_ref, g_ref, b_ref, xo_ref, xb_ref):
    rt = rt_ref[...]
    ffn = rt[:, 2:3] * y0_ref[...] + rt[:, 3:4] * y1_ref[...]
    y = _layer_norm(DN_ALPHA * x_ref[...] + ffn, g_ref[...], b_ref[...])
    xo_ref[...] = y
    xb_ref[...] = y.astype(BF16)


def _combine_ln(x, y0, y1, rt, g, b, tm=512):
    n, d = x.shape
    row = lambda i: (i, 0)
    fixed = lambda i: (0, 0)
    return pl.pallas_call(
        _combine_ln_kernel,
        grid=(n // tm,),
        in_specs=[pl.BlockSpec((tm, d), row), pl.BlockSpec((tm, d), row), pl.BlockSpec((tm, d), row),
                  pl.BlockSpec((tm, LANES), row), pl.BlockSpec((1, d), fixed), pl.BlockSpec((1, d), fixed)],
        out_specs=[pl.BlockSpec((tm, d), row), pl.BlockSpec((tm, d), row)],
        out_shape=[jax.ShapeDtypeStruct((n, d), F32), jax.ShapeDtypeStruct((n, d), BF16)],
        compiler_params=_cparams(("parallel",)),
        name="moe_combine_ln",
    )(x, y0, y1, rt, g.reshape(1, d), b.reshape(1, d))


def _moe_layout(rt, n):
    e_flat = rt[:, 0:TOP_K].astype(I32).reshape(-1)
    nk = n * TOP_K
    onehot = (e_flat[:, None] == jnp.arange(N_EXPERTS, dtype=I32)[None, :]).astype(I32)
    rank = jnp.take_along_axis(jnp.cumsum(onehot, axis=0), e_flat[:, None], axis=1)[:, 0] - 1
    counts = jnp.sum(onehot, axis=0)
    padded = (counts + MOE_TM - 1) // MOE_TM * MOE_TM
    pad_end = jnp.cumsum(padded)
    pad_start = pad_end - padded
    grp_start = jnp.cumsum(counts) - counts
    slot = pad_start[e_flat] + rank
    n_blocks = -(-nk // MOE_TM) + N_EXPERTS
    n_slots = n_blocks * MOE_TM
    order = jnp.argsort(e_flat, stable=True).astype(I32)
    sl = jnp.arange(n_slots, dtype=I32)
    slot_e = jnp.minimum(jnp.searchsorted(pad_end, sl, side='right'), N_EXPERTS - 1).astype(I32)
    within = sl - pad_start[slot_e]
    valid = within < counts[slot_e]
    src = jnp.where(valid, grp_start[slot_e] + within, 0)
    slot_tok = jnp.where(valid, order[src] // TOP_K, 0)
    n_used = (pad_end[-1] // MOE_TM).astype(I32).reshape(1)
    blk = jnp.arange(n_blocks, dtype=I32)
    block_e = slot_e[jnp.minimum(blk, n_used[0] - 1) * MOE_TM]
    return slot_tok, slot.reshape(n, TOP_K), block_e, n_used


def _pair_perm(n_heads):
    half = n_heads // 2
    cols = []
    for p in range(half):
        cols += list(range(p * HEAD_DIM, (p + 1) * HEAD_DIM))
        cols += list(range((half + p) * HEAD_DIM, (half + p + 1) * HEAD_DIM))
    return np.asarray(cols, dtype=np.int32)


def _pad_cols(w, width):
    return jnp.pad(w, ((0, 0), (0, width - w.shape[1])))


def _even_layer(x, xb, w_in, w_out, lam_params, subln, lam_init, wg, wu, wd, ln, tabs, bsz, seq_len):
    n, d = x.shape
    perm = _pair_perm(B_HEADS)
    aq, ak, av = w_in[:, 0:512], w_in[:, 512:1024], w_in[:, 1024:1536]
    bq, bk, bv = w_in[:, 1536:2048], w_in[:, 2048:2176], w_in[:, 2176:2304]
    iq, ik, iw = w_in[:, 2304:2560], w_in[:, 2560:2624], w_in[:, 2624:2628]
    w_rope = jnp.concatenate([aq * SCALE, ak, bq[:, perm] * SCALE, iq, bk, ik, ik], axis=1).astype(BF16)
    w_val = jnp.concatenate([av, bv], axis=1).astype(BF16)
    w_iw = _pad_cols(iw, LANES).astype(BF16)
    rp = _matmul(xb, w_rope, BF16, 512, 1024, seq_len, tabs).reshape(bsz, seq_len, -1)
    vt = _matmul_t(xb, w_val, 512, TQ).reshape(bsz, seq_len // TQ, -1, TQ)
    iwv = _matmul(xb, w_iw, F32, 512, LANES).reshape(bsz, seq_len, LANES)
    o_a = _diff_attention(rp, vt, lam_params, subln, lam_init)
    o_b = _dsa_attention(rp, vt, iwv)
    wo_a = w_out[0:512].astype(BF16)
    wo_b = w_out[512:1024][perm].astype(BF16)
    g_mix, b_mix, g_ffn, b_ffn = ln
    x1, x1b = _outproj_ln(x, o_a.reshape(n, -1), o_b.reshape(n, -1), wo_a, wo_b, g_mix, b_mix)
    return _ffn_ln(x1, x1b, wg.astype(BF16), wu.astype(BF16), wd.astype(BF16), g_ffn, b_ffn)


def _odd_layer(x, xb, w_in, w_out, gate_b, pe, phi_w1, phi_w2, w_router, b_router, wg, wu, wd, ln, tabs,
               bsz, seq_len):
    n, d = x.shape
    perm = _pair_perm(D_HEADS)
    cq, ck, cv = w_in[:, 0:512], w_in[:, 512:1024], w_in[:, 1024:1536]
    dq = w_in[:, 1536:2048][:, perm] * SCALE
    dkc, dvc, dks = w_in[:, 2048:2176], w_in[:, 2176:2304], w_in[:, 2304:2432]
    dvs, dkw, dvw = w_in[:, 2432:2560], w_in[:, 2560:2688], w_in[:, 2688:2816]
    dg = w_in[:, 2816:2840]
    w_rope = jnp.concatenate([cq * SCALE, ck, dq, dks, dkw], axis=1).astype(BF16)
    w_plain = jnp.concatenate([dq, dkc, dvc], axis=1).astype(BF16)
    w_val = jnp.concatenate([cv, dvs, dvw], axis=1).astype(BF16)
    w_dg = _pad_cols(dg, LANES).astype(BF16)
    rp = _matmul(xb, w_rope, BF16, 512, w_rope.shape[1] // 2, seq_len, tabs).reshape(bsz, seq_len, -1)
    pp = _matmul(xb, w_plain, BF16, 512, w_plain.shape[1]).reshape(bsz, seq_len, -1)
    vt = _matmul_t(xb, w_val, 512, TQ).reshape(bsz, seq_len // TQ, -1, TQ)
    dgv = _matmul(xb, w_dg, F32, 512, LANES).reshape(bsz, seq_len, LANES)

    o_c = _moba_attention(rp, vt)

    nc = seq_len // NSA_CMP_STRIDE
    tok = pp[:, :, 4 * LANES:6 * LANES].reshape(bsz, nc, NSA_CMP_STRIDE, 4, HEAD_DIM)
    r = tok.transpose(0, 3, 1, 2, 4).reshape(bsz, 4, nc, NSA_CMP_STRIDE * HEAD_DIM)
    pe_flat = jnp.pad(pe.reshape(2, 1, -1), ((0, 0), (0, 7), (0, 0))).astype(BF16)
    cmp = _nsa_compress(r, pe_flat, phi_w1.astype(BF16), phi_w2.astype(BF16))
    kcmp = jnp.concatenate([cmp[:, 0], cmp[:, 1]], axis=-1)
    vcmp = jnp.concatenate([cmp[:, 2], cmp[:, 3]], axis=-1)
    gb = _pad_cols(gate_b.reshape(1, -1), LANES)
    o_d = _nsa_attention(rp, pp, vt, dgv, gb, kcmp, vcmp)

    wo_c = w_out[0:512].astype(BF16)
    wo_d = w_out[512:1024][perm].astype(BF16)
    g_mix, b_mix, g_ffn, b_ffn = ln
    x1, x1b = _outproj_ln(x, o_c.reshape(n, -1), o_d.reshape(n, -1), wo_c, wo_d, g_mix, b_mix)

    rt = _router(x1, _pad_cols(w_router, LANES), _pad_cols(b_router.reshape(1, -1), LANES))
    slot_tok, slot, block_e, n_used = _moe_layout(rt, n)
    y_slots = _moe_ffn(x1b[slot_tok], block_e, n_used, wg.astype(BF16), wu.astype(BF16), wd.astype(BF16))
    return _combine_ln(x1, y_slots[slot[:, 0]], y_slots[slot[:, 1]], rt, g_ffn, b_ffn)


@jax.jit
def kernel(x, ev_w_in, ev_w_out, dif_lambda, dif_subln, ffd_w_gate, ffd_w_up, ffd_w_down, od_w_in, od_w_out,
           nsa_gate_b, nsa_pe, nsa_phi_w1, nsa_phi_w2, moe_w_router, moe_b_router, moe_w_gate, moe_w_up,
           moe_w_down, ln_mix_g, ln_mix_b, ln_ffn_g, ln_ffn_b):
    bsz, seq_len, d = x.shape
    tabs = _rope_tables(seq_len)
    xf = x.reshape(bsz * seq_len, d)
    xb = xf.astype(BF16)
    for l in range(DEPTH):
        i = l // 2
        ln = (ln_mix_g[l], ln_mix_b[l], ln_ffn_g[l], ln_ffn_b[l])
        if l % 2 == 0:
            lam_init = 0.8 - 0.6 * math.exp(-0.3 * l)
            xf, xb = _even_layer(xf, xb, ev_w_in[i], ev_w_out[i], dif_lambda[i], dif_subln[i], lam_init,
                                 ffd_w_gate[i], ffd_w_up[i], ffd_w_down[i], ln, tabs, bsz, seq_len)
        else:
            xf, xb = _odd_layer(xf, xb, od_w_in[i], od_w_out[i], nsa_gate_b[i], nsa_pe[i], nsa_phi_w1[i],
                                nsa_phi_w2[i], moe_w_router[i], moe_b_router[i], moe_w_gate[i], moe_w_up[i],
                                moe_w_down[i], ln, tabs, bsz, seq_len)
    return xf.reshape(bsz, seq_len, d)
```

```python
import functools
import math

import numpy as np
import jax
import jax.numpy as jnp
from jax import lax
from jax.experimental import pallas as pl
from jax.experimental.pallas import tpu as pltpu

F32 = jnp.float32
BF16 = jnp.bfloat16
I32 = jnp.int32

LANES = 128
VMEM_LIMIT = 56 * 1024 * 1024

DEPTH = 4
HEAD_DIM = 64
ROPE_THETA = 10000.0
LN_EPS = 1e-5
DN_ALPHA = (2 * DEPTH) ** 0.25
SCALE = HEAD_DIM ** -0.5 * math.log2(math.e)
NEG = -1e30
BIG = 1e30
M_INIT = -1e30
NINF = float("-inf")

A_HEADS = 4
B_HEADS = 8
IDX_HEADS = 4
DSA_TOPK = 256
C_HEADS = 8
MOBA_BLOCK = 256
MOBA_TOPK = 3
D_HEADS = 8
NSA_CMP_LEN = 32
NSA_CMP_STRIDE = 16
NSA_SLC_BLOCK = 64
NSA_SLC_TOPK = 16
NSA_WINDOW = 512
NSA_PHI_HIDDEN = 256
N_EXPERTS = 8
TOP_K = 2
MOE_TM = 512
N_BISECT = 12


def _cparams(sem):
    return pltpu.CompilerParams(dimension_semantics=sem, vmem_limit_bytes=VMEM_LIMIT)


def _dot_nt(a, b):
    return lax.dot_general(a, b, (((1,), (1,)), ((), ())), preferred_element_type=F32)


def _layer_norm(y, g, b):
    mu = jnp.mean(y, axis=-1, keepdims=True)
    yc = y - mu
    var = jnp.mean(yc * yc, axis=-1, keepdims=True)
    return yc * lax.rsqrt(var + LN_EPS) * g + b


def _safe_recip(l):
    return jnp.where(l > 0.0, 1.0 / jnp.where(l > 0.0, l, 1.0), 0.0)


def _half_masks():
    lane = lax.broadcasted_iota(I32, (1, LANES), 1)
    return lane < HEAD_DIM


def _split_halves(t):
    lo = _half_masks()
    z = jnp.zeros_like(t)
    return jnp.where(lo, t, z), jnp.where(lo, z, t)


def _mm_kernel(x_ref, w_ref, *rest, rope):
    o_ref = rest[-1]
    acc = jnp.dot(x_ref[...].astype(BF16), w_ref[...], preferred_element_type=F32)
    if not rope:
        o_ref[...] = acc.astype(o_ref.dtype)
        return
    cos = rest[0][...]
    sin = rest[1][...]
    lane = lax.broadcasted_iota(I32, (1, LANES), 1)
    first = (lane % HEAD_DIM) < (HEAD_DIM // 2)
    for c in range(acc.shape[1] // LANES):
        a = acc[:, c * LANES:(c + 1) * LANES]
        rot = jnp.where(first, pltpu.roll(a, LANES - HEAD_DIM // 2, 1), pltpu.roll(a, HEAD_DIM // 2, 1))
        o_ref[:, c * LANES:(c + 1) * LANES] = (a * cos + rot * sin).astype(o_ref.dtype)


def _matmul(x, w, out_dtype, tm, tn, seq_len=None, rope_tabs=None):
    n, d = x.shape
    p = w.shape[1]
    rope = rope_tabs is not None
    in_specs = [pl.BlockSpec((tm, d), lambda i, j: (i, 0)),
                pl.BlockSpec((d, tn), lambda i, j: (0, j))]
    args = [x, w]
    if rope:
        nt = seq_len // tm
        in_specs += [pl.BlockSpec((tm, LANES), lambda i, j: (i % nt, 0))] * 2
        args += list(rope_tabs)
    return pl.pallas_call(
        functools.partial(_mm_kernel, rope=rope),
        grid=(n // tm, p // tn),
        in_specs=in_specs,
        out_specs=pl.BlockSpec((tm, tn), lambda i, j: (i, j)),
        out_shape=jax.ShapeDtypeStruct((n, p), out_dtype),
        compiler_params=_cparams(("parallel", "arbitrary")),
        name="proj_rope" if rope else "proj",
    )(*args)


def _rope_tables(seq_len):
    d = HEAD_DIM
    inv = ROPE_THETA ** (-jnp.arange(0, d, 2, dtype=F32) / d)
    ang = jnp.arange(seq_len, dtype=I32).astype(F32)[:, None] * inv[None, :]
    cos = jnp.cos(ang)
    sin = jnp.sin(ang)
    cos128 = jnp.tile(cos, (1, LANES // (d // 2)))
    sin128 = jnp.tile(jnp.concatenate([-sin, sin], axis=1), (1, LANES // d))
    return cos128, sin128


def _route_top2(x, w, b):
    logits = jnp.dot(x, w, precision=lax.Precision.HIGHEST, preferred_element_type=F32) + b
    lane = lax.broadcasted_iota(I32, (1, LANES), 1)
    lanef = lane.astype(F32)
    v = jnp.where(lane < N_EXPERTS, logits, NINF)
    l0 = jnp.max(v, axis=1, keepdims=True)
    i0 = jnp.min(jnp.where(v == l0, lanef, float(LANES)), axis=1, keepdims=True)
    v = jnp.where(lanef == i0, NINF, v)
    l1 = jnp.max(v, axis=1, keepdims=True)
    i1 = jnp.min(jnp.where(v == l1, lanef, float(LANES)), axis=1, keepdims=True)
    e1 = jnp.exp(l1 - l0)
    g0 = 1.0 / (1.0 + e1)
    g1 = e1 / (1.0 + e1)
    return jnp.where(lane == 0, i0, jnp.where(lane == 1, i1, jnp.where(lane == 2, g0, jnp.where(lane == 3, g1, 0.0))))


def _outproj_ln_kernel(x_ref, a_ref, b_ref, wa_ref, wb_ref, g_ref, bb_ref, *rest):
    mix = (jnp.dot(a_ref[...], wa_ref[...], preferred_element_type=F32)
           + jnp.dot(b_ref[...], wb_ref[...], preferred_element_type=F32))
    y = _layer_norm(DN_ALPHA * x_ref[...] + mix, g_ref[...], bb_ref[...])
    if len(rest) == 2:
        xo_ref, xb_ref = rest
    else:
        wr_ref, br_ref, xo_ref, xb_ref, rt_ref = rest
        rt_ref[...] = _route_top2(y, wr_ref[...], br_ref[...])
    xo_ref[...] = y
    xb_ref[...] = y.astype(BF16)


def _outproj_ln(x, oa, ob, wa, wb, g, b, router=None, tm=512):
    n, d = x.shape
    ka, kb = oa.shape[1], ob.shape[1]
    row = lambda i: (i, 0)
    fixed = lambda i: (0, 0)
    in_specs = [pl.BlockSpec((tm, d), row), pl.BlockSpec((tm, ka), row), pl.BlockSpec((tm, kb), row),
                pl.BlockSpec((ka, d), fixed), pl.BlockSpec((kb, d), fixed),
                pl.BlockSpec((1, d), fixed), pl.BlockSpec((1, d), fixed)]
    out_specs = [pl.BlockSpec((tm, d), row), pl.BlockSpec((tm, d), row)]
    out_shape = [jax.ShapeDtypeStruct((n, d), F32), jax.ShapeDtypeStruct((n, d), BF16)]
    args = [x, oa, ob, wa, wb, g.reshape(1, d), b.reshape(1, d)]
    if router is not None:
        in_specs += [pl.BlockSpec((d, LANES), fixed), pl.BlockSpec((1, LANES), fixed)]
        out_specs.append(pl.BlockSpec((tm, LANES), row))
        out_shape.append(jax.ShapeDtypeStruct((n, LANES), F32))
        args += list(router)
    return pl.pallas_call(
        _outproj_ln_kernel,
        grid=(n // tm,),
        in_specs=in_specs,
        out_specs=out_specs,
        out_shape=out_shape,
        compiler_params=_cparams(("parallel",)),
        name="outproj_ln",
    )(*args)


def _ffn_ln_kernel(x_ref, xb_ref, wg_ref, wu_ref, wd_ref, g_ref, b_ref, xo_ref, xob_ref):
    xb = xb_ref[...]
    h = jax.nn.silu(jnp.dot(xb, wg_ref[...], preferred_element_type=F32)) * jnp.dot(
        xb, wu_ref[...], preferred_element_type=F32)
    ffn = jnp.dot(h.astype(BF16), wd_ref[...], preferred_element_type=F32)
    y = _layer_norm(DN_ALPHA * x_ref[...] + ffn, g_ref[...], b_ref[...])
    xo_ref[...] = y
    xob_ref[...] = y.astype(BF16)


def _ffn_ln(x, xb, wg, wu, wd, g, b, tm=512):
    n, d = x.shape
    fdim = wg.shape[1]
    row = lambda i: (i, 0)
    fixed = lambda i: (0, 0)
    once = pl.Buffered(1)
    return pl.pallas_call(
        _ffn_ln_kernel,
        grid=(n // tm,),
        in_specs=[pl.BlockSpec((tm, d), row), pl.BlockSpec((tm, d), row),
                  pl.BlockSpec((d, fdim), fixed, pipeline_mode=once),
                  pl.BlockSpec((d, fdim), fixed, pipeline_mode=once),
                  pl.BlockSpec((fdim, d), fixed, pipeline_mode=once),
                  pl.BlockSpec((1, d), fixed), pl.BlockSpec((1, d), fixed)],
        out_specs=[pl.BlockSpec((tm, d), row), pl.BlockSpec((tm, d), row)],
        out_shape=[jax.ShapeDtypeStruct((n, d), F32), jax.ShapeDtypeStruct((n, d), BF16)],
        compiler_params=_cparams(("parallel",)),
        name="ffn_ln",
    )(x, xb, wg, wu, wd, g.reshape(1, d), b.reshape(1, d))


def _top_n_mask(v, n, axis):
    idx = lax.broadcasted_iota(I32, v.shape, axis).astype(F32)
    sel = jnp.zeros(v.shape, F32)
    for _ in range(n):
        mx = jnp.max(v, axis=axis, keepdims=True)
        first = jnp.min(jnp.where(v == mx, idx, float(v.shape[axis])), axis=axis, keepdims=True)
        pick = idx == first
        sel = jnp.where(pick, 1.0, sel)
        v = jnp.where(pick, NINF, v)
    return sel


DV_PAD = 16
TQ = 256


def _with_ones(vt):
    return jnp.concatenate([vt, jnp.ones((DV_PAD, vt.shape[1]), vt.dtype)], axis=0)


def _mm_t_kernel(x_ref, w_ref, o_ref, *, tk):
    acc = jnp.dot(x_ref[...].astype(BF16), w_ref[...], preferred_element_type=F32)
    for cc in range(acc.shape[0] // tk):
        o_ref[cc] = acc[cc * tk:(cc + 1) * tk, :].T.astype(o_ref.dtype)


def _matmul_t(x, w, tm, tk):
    n, d = x.shape
    p = w.shape[1]
    return pl.pallas_call(
        functools.partial(_mm_t_kernel, tk=tk),
        grid=(n // tm,),
        in_specs=[pl.BlockSpec((tm, d), lambda i: (i, 0)), pl.BlockSpec((d, p), lambda i: (0, 0))],
        out_specs=pl.BlockSpec((tm // tk, p, tk), lambda i: (i, 0, 0)),
        out_shape=jax.ShapeDtypeStruct((n // tk, p, tk), BF16),
        compiler_params=_cparams(("parallel",)),
        name="proj_t",
    )(x, w)


def _normalize_t(acc, width):
    return acc[:width] * _safe_recip(acc[width:width + 1])


def _pipe_flash(n, ns, qk, vt_at, bias_at, bufs, dv, tk, tq):
    sa, sb, pa, pb = bufs
    for j in range(ns):
        sa[j] = qk(0, j)
        pb[j] = jnp.zeros((tk, tq), BF16)

    def half(c, carry, s_cur, s_nxt, p_prev, p_cur):
        nxt = jnp.minimum(c + 1, n - 1)
        for j in range(ns):
            s_nxt[j] = qk(nxt, j)
        cp = jnp.clip(c - 1, 0, n - 1)
        out = []
        for j in range(ns):
            m, acc, alpha = carry[j]
            acc = alpha * acc + jnp.dot(vt_at(cp, j), p_prev[j], preferred_element_type=F32)
            st = s_cur[j] + bias_at(c, j)
            m_new = jnp.maximum(m, jnp.max(st, axis=0, keepdims=True))
            alpha = jnp.exp2(m - m_new)
            p_cur[j] = jnp.exp2((st - m_new).astype(BF16))
            out.append((m_new, acc, alpha))
        return tuple(out)

    def body(t, carry):
        carry = half(2 * t, carry, sa, sb, pb, pa)
        return half(2 * t + 1, carry, sb, sa, pa, pb)

    init = (jnp.full((1, tq), M_INIT, F32), jnp.zeros((dv, tq), F32), jnp.ones((1, tq), F32))
    trips = (n + 1) // 2
    carry = lax.fori_loop(0, trips, body, (init,) * ns)
    cl = jnp.minimum(2 * trips - 1, n - 1)
    outs = []
    for j in range(ns):
        _, acc, alpha = carry[j]
        outs.append(alpha * acc + jnp.dot(vt_at(cl, j), pb[j], preferred_element_type=F32))
    return outs


def _pipe_scratch(ns, tk, tq):
    return [pltpu.VMEM((ns, tk, tq), F32)] * 2 + [pltpu.VMEM((ns, tk, tq), BF16)] * 2


def _causal_t(t):
    return jnp.where(lax.broadcasted_iota(I32, (t, t), 0) <= lax.broadcasted_iota(I32, (t, t), 1), 0.0, NINF)


def _diff_kernel(lam_ref, sub_ref, q_ref, k_ref, vt_ref, o_ref, *bufs, tq, lam_init):
    i = pl.program_id(2)
    lp = lam_ref[...]
    lam = (jnp.exp(jnp.sum(lp[0:1] * lp[1:2], axis=1, keepdims=True))
           - jnp.exp(jnp.sum(lp[2:3] * lp[3:4], axis=1, keepdims=True)) + lam_init)
    qs = _split_halves(q_ref[...])
    n = i + 1
    bufs, tab_ref = bufs[:4], bufs[4]
    tab_ref[0] = jnp.zeros((tq, tq), F32)
    tab_ref[1] = _causal_t(tq)
    tab_ref[2] = jnp.full((tq, tq), NINF, F32)

    def qk(c, j):
        off = pl.multiple_of(c * tq, tq)
        return _dot_nt(k_ref[pl.ds(off, tq), :], qs[j])

    def bias_at(c, j):
        return tab_ref[jnp.where(c == i, 1, jnp.where(c < n, 0, 2))]

    outs = _pipe_flash(n, 2, qk, lambda c, j: _with_ones(vt_ref[c]), bias_at, bufs, LANES + DV_PAD, tq, tq)
    o = _normalize_t(outs[0], LANES) - lam * _normalize_t(outs[1], LANES)
    o = o * lax.rsqrt(jnp.mean(o * o, axis=0, keepdims=True) + LN_EPS)
    o = o * sub_ref[...] * (1.0 - lam_init)
    o_ref[...] = o.T.astype(o_ref.dtype)


def _diff_attention(rp, vt, lam_params, subln, lam_init, tq=TQ):
    b, s, _ = rp.shape
    nk = s // tq
    return pl.pallas_call(
        functools.partial(_diff_kernel, tq=tq, lam_init=lam_init),
        grid=(b, A_HEADS, nk),
        in_specs=[pl.BlockSpec((4, HEAD_DIM), lambda bi, h, i: (0, 0)),
                  pl.BlockSpec((LANES, 1), lambda bi, h, i: (0, 0)),
                  pl.BlockSpec((None, tq, LANES), lambda bi, h, i: (bi, i, h)),
                  pl.BlockSpec((None, s, LANES), lambda bi, h, i: (bi, 0, A_HEADS + h)),
                  pl.BlockSpec((None, nk, LANES, tq), lambda bi, h, i: (bi, 0, h, 0))],
        out_specs=pl.BlockSpec((None, tq, LANES), lambda bi, h, i: (bi, i, h)),
        out_shape=jax.ShapeDtypeStruct((b, s, A_HEADS * LANES), BF16),
        scratch_shapes=_pipe_scratch(2, tq, tq) + [pltpu.VMEM((3, tq, tq), F32)],
        compiler_params=_cparams(("parallel", "parallel", "arbitrary")),
        name="diff_attn",
    )(lam_params, subln.reshape(LANES, 1), rp, rp, vt)


def _fold8(x, op):
    acc = x[0:8]
    for r in range(1, x.shape[0] // 8):
        acc = op(acc, x[r * 8:(r + 1) * 8])
    return acc


def _dsa_kernel(iq_ref, ikk_ref, iw_ref, q_ref, k_ref, vt_ref, o_ref, s_ref, j_ref, *bufs, tq, ksel, seq_len):
    tk = tq
    i = pl.program_id(1)
    nch = i + 1
    ksel_f = float(ksel)
    k_loc = lax.broadcasted_iota(I32, (tk, tq), 0)
    q_loc = lax.broadcasted_iota(I32, (tk, tq), 1)
    qpos = i * tq + lax.broadcasted_iota(I32, (1, tq), 1)

    iq = iq_ref[...]
    iwt = iw_ref[...].T
    iqh = []
    for pair in range(IDX_HEADS // 2):
        iqh += list(_split_halves(iq[:, pair * LANES:(pair + 1) * LANES]))

    def scores(c):
        off = pl.multiple_of(c * tk, tk)
        kk = ikk_ref[pl.ds(off, tk), :]
        sc = iwt[0:1] * jnp.maximum(_dot_nt(kk, iqh[0]), 0.0)
        for h in range(1, IDX_HEADS):
            sc = sc + iwt[h:h + 1] * jnp.maximum(_dot_nt(kk, iqh[h]), 0.0)
        return sc

    def full_body(c, carry):
        mx, mn = carry
        sc = scores(c)
        s_ref[c] = sc
        return jnp.maximum(mx, _fold8(sc, jnp.maximum)), jnp.minimum(mn, _fold8(sc, jnp.minimum))

    mx, mn = lax.fori_loop(0, i, full_body, (jnp.full((8, tq), -BIG, F32), jnp.full((8, tq), BIG, F32)))
    sc = scores(i)
    causal = k_loc <= q_loc
    s_ref[i] = jnp.where(causal, sc, NEG)
    mx = jnp.maximum(mx, _fold8(jnp.where(causal, sc, -BIG), jnp.maximum))
    mn = jnp.minimum(mn, _fold8(jnp.where(causal, sc, BIG), jnp.minimum))
    smax = jnp.max(mx, axis=0, keepdims=True)
    smin = jnp.min(mn, axis=0, keepdims=True)
    s_ref[nch] = jnp.full((tk, tq), NINF, F32)

    def count_where(ind):
        def body(c, acc):
            return acc + _fold8(ind(s_ref[c], c * tk + k_loc), jnp.add)
        acc = lax.fori_loop(0, nch, body, jnp.zeros((8, tq), F32))
        return jnp.sum(acc, axis=0, keepdims=True)

    def count_ge(th):
        return count_where(lambda x, kidx: jnp.where(x >= th, 1.0, 0.0))

    def max_below(th):
        def body(c, acc):
            x = s_ref[c]
            return jnp.maximum(acc, _fold8(jnp.where(x < th, x, NINF), jnp.maximum))
        acc = lax.fori_loop(0, nch, body, jnp.full((8, tq), NINF, F32))
        return jnp.max(acc, axis=0, keepdims=True)

    n_causal = (qpos + 1).astype(F32)
    take_all = n_causal <= ksel_f
    done0 = jnp.where(take_all | (count_ge(smax) >= ksel_f), 1.0, 0.0)

    def bisect(lo, hi):
        mid = lo + (hi - lo) * 0.5
        ge = count_ge(mid) >= ksel_f
        return jnp.where(ge, mid, lo), jnp.where(ge, hi, mid)

    lo, hi = lax.fori_loop(0, N_BISECT, lambda _, c: bisect(*c), (smin, smax))

    def snap_body(carry):
        lo, hi, th, done, _ = carry
        lo, hi = bisect(lo, hi)
        t1 = max_below(hi)
        ok = count_ge(t1) >= ksel_f
        th = jnp.where(done > 0.0, th, t1)
        hi = jnp.where(ok, hi, t1)
        done = jnp.where(ok, 1.0, done)
        return lo, hi, th, done, jnp.sum(1.0 - done)

    _, _, th, _, _ = lax.while_loop(lambda c: c[4] > 0.0, snap_body,
                                    (lo, hi, smax, done0, jnp.sum(1.0 - done0)))

    c_ge = count_ge(th)
    need_tb = jnp.where(take_all, 0.0, jnp.where(c_ge > ksel_f, 1.0, 0.0))
    j_ref[...] = jnp.full((8, tq), seq_len - 1, I32)

    @pl.when(jnp.sum(need_tb) > 0.0)
    def _():
        need = ksel_f - count_where(lambda x, kidx: jnp.where(x > th, 1.0, 0.0))

        def jb(_, carry):
            lo_j, hi_j = carry
            mid = (lo_j + hi_j) // 2
            cnt = count_where(lambda x, kidx: jnp.where(x == th, jnp.where(kidx <= mid, 1.0, 0.0), 0.0))
            ge = cnt >= need
            return jnp.where(ge, lo_j, mid), jnp.where(ge, mid, hi_j)

        n_it = int(math.ceil(math.log2(seq_len))) + 1
        _, hi_j = lax.fori_loop(0, n_it, jb, (jnp.full((1, tq), -1, I32), jnp.full((1, tq), seq_len - 1, I32)))
        j_ref[...] = jnp.broadcast_to(hi_j, (8, tq))

    jsel = j_ref[0:1, :]

    def bias_body(c, _):
        x = s_ref[c]
        kidx = c * tk + k_loc
        keep = jnp.where(x > th, 0.0, jnp.where(x == th, jnp.where(kidx <= jsel, 0.0, NINF), NINF))
        keep = jnp.where(take_all, 0.0, keep)
        s_ref[c] = jnp.where(kidx <= qpos, keep, NINF)
        return 0

    lax.fori_loop(0, nch, bias_body, 0)

    for p in range(B_HEADS // 2):
        qs = _split_halves(q_ref[:, p * LANES:(p + 1) * LANES])

        def qk(c, j, qs=qs):
            off = pl.multiple_of(c * tk, tk)
            return _dot_nt(k_ref[pl.ds(off, tk), :], qs[j])

        outs = _pipe_flash(nch, 2, qk, lambda c, j: _with_ones(vt_ref[c, j * HEAD_DIM:(j + 1) * HEAD_DIM, :]),
                           lambda c, j: s_ref[jnp.minimum(c, nch)], bufs, HEAD_DIM + DV_PAD, tk, tq)
        o = jnp.concatenate([_normalize_t(outs[0], HEAD_DIM), _normalize_t(outs[1], HEAD_DIM)], axis=0)
        o_ref[:, p * LANES:(p + 1) * LANES] = o.T.astype(o_ref.dtype)


def _dsa_attention(rp, vt, iw, tq=TQ):
    b, s, _ = rp.shape
    ksel = min(DSA_TOPK, s // 4)
    nk = s // tq
    return pl.pallas_call(
        functools.partial(_dsa_kernel, tq=tq, ksel=ksel, seq_len=s),
        grid=(b, nk),
        in_specs=[pl.BlockSpec((None, tq, 2 * LANES), lambda bi, i: (bi, i, 6)),
                  pl.BlockSpec((None, s, LANES), lambda bi, i: (bi, 0, 15)),
                  pl.BlockSpec((None, tq, LANES), lambda bi, i: (bi, i, 0)),
                  pl.BlockSpec((None, tq, 4 * LANES), lambda bi, i: (bi, i, 2)),
                  pl.BlockSpec((None, s, LANES), lambda bi, i: (bi, 0, 14)),
                  pl.BlockSpec((None, nk, LANES, tq), lambda bi, i: (bi, 0, 4, 0))],
        out_specs=pl.BlockSpec((None, tq, 4 * LANES), lambda bi, i: (bi, i, 0)),
        out_shape=jax.ShapeDtypeStruct((b, s, 4 * LANES), BF16),
        scratch_shapes=[pltpu.VMEM((nk + 1, tq, tq), F32), pltpu.VMEM((8, tq), I32)] + _pipe_scratch(2, tq, tq),
        compiler_params=_cparams(("parallel", "arbitrary")),
        name="dsa_attn",
    )(rp, rp, iw, rp, rp, vt)


def _moba_kernel(q_ref, k_ref, vt_ref, o_ref, km_ref, sel_ref, *bufs, seq_len, n_sel):
    tq = MOBA_BLOCK
    qb = pl.program_id(2)

    @pl.when(qb == 0)
    def _():
        j = lax.broadcasted_iota(I32, (LANES, seq_len), 0)
        s = lax.broadcasted_iota(I32, (LANES, seq_len), 1)
        avg = jnp.where(s // MOBA_BLOCK == j, 1.0 / MOBA_BLOCK, 0.0).astype(BF16)
        km_ref[...] = jnp.dot(avg, k_ref[...], preferred_element_type=F32)

    nbp = sel_ref.shape[1]
    km = km_ref[0:nbp, :]
    qs = _split_halves(q_ref[...])
    blk = lax.broadcasted_iota(I32, (nbp, tq), 0)
    past = blk < qb
    for j in range(2):
        gate = lax.dot_general(km, qs[j].astype(F32), (((1,), (1,)), ((), ())),
                               precision=lax.Precision.HIGHEST, preferred_element_type=F32)
        gate = jnp.where(blk < seq_len // MOBA_BLOCK, jnp.where(past, gate, NEG), NINF)
        sel = _top_n_mask(gate, n_sel, 0)
        sel_ref[j] = jnp.where(past, jnp.where(sel > 0.5, 0.0, NINF), NINF)
    own = _causal_t(tq)
    n = qb + 1

    def qk(c, j):
        off = pl.multiple_of(c * tq, tq)
        return _dot_nt(k_ref[pl.ds(off, tq), :], qs[j])

    def bias_at(c, j):
        chosen = sel_ref[j, pl.ds(jnp.minimum(c, nbp - 1), 1), :]
        return jnp.where(c == qb, own, jnp.where(c < n, chosen, NINF))

    outs = _pipe_flash(n, 2, qk, lambda c, j: _with_ones(vt_ref[c, j * HEAD_DIM:(j + 1) * HEAD_DIM, :]), bias_at, bufs,
                       HEAD_DIM + DV_PAD, tq, tq)
    o = jnp.concatenate([_normalize_t(outs[0], HEAD_DIM), _normalize_t(outs[1], HEAD_DIM)], axis=0)
    o_ref[...] = o.T.astype(o_ref.dtype)


def _moba_attention(rp, vt):
    b, s, _ = rp.shape
    tq = MOBA_BLOCK
    nb = s // tq
    n_sel = max(1, min(MOBA_TOPK, nb - 1))
    npair = C_HEADS // 2
    return pl.pallas_call(
        functools.partial(_moba_kernel, seq_len=s, n_sel=n_sel),
        grid=(b, npair, nb),
        in_specs=[pl.BlockSpec((None, tq, LANES), lambda bi, h, i: (bi, i, h)),
                  pl.BlockSpec((None, s, LANES), lambda bi, h, i: (bi, 0, npair + h)),
                  pl.BlockSpec((None, nb, LANES, tq), lambda bi, h, i: (bi, 0, h, 0))],
        out_specs=pl.BlockSpec((None, tq, LANES), lambda bi, h, i: (bi, i, h)),
        out_shape=jax.ShapeDtypeStruct((b, s, npair * LANES), BF16),
        scratch_shapes=[pltpu.VMEM((LANES, LANES), F32), pltpu.VMEM((2, -(-nb // 8) * 8, tq), F32)]
        + _pipe_scratch(2, tq, tq),
        compiler_params=_cparams(("parallel", "parallel", "arbitrary")),
        name="moba_attn",
    )(rp, rp, vt)


def _cmp_kernel(r_ref, pe_ref, w1_ref, w2_ref, o_ref):
    r = r_ref[...]
    w1 = w1_ref[...]
    half = r.shape[1]
    u = jnp.dot(r, w1[:half], preferred_element_type=F32)
    v = jnp.dot(r, w1[half:], preferred_element_type=F32)
    c = jnp.dot(pe_ref[...], w1, preferred_element_type=F32)[0:1]
    pre = u + pltpu.roll(v, r.shape[0] - 1, 0) + c
    o_ref[...] = jnp.dot(jax.nn.gelu(pre).astype(BF16), w2_ref[...],
                         preferred_element_type=F32).astype(o_ref.dtype)


def _nsa_compress(r, pe, w1, w2):
    b, _, nc, wdt = r.shape
    hid = w1.shape[2]
    return pl.pallas_call(
        _cmp_kernel,
        grid=(b, 4),
        in_specs=[pl.BlockSpec((None, None, nc, wdt), lambda bi, t: (bi, t, 0, 0)),
                  pl.BlockSpec((None, 8, 2 * wdt), lambda bi, t: (t // 2, 0, 0)),
                  pl.BlockSpec((None, 2 * wdt, hid), lambda bi, t: (t // 2, 0, 0)),
                  pl.BlockSpec((None, hid, HEAD_DIM), lambda bi, t: (t // 2, 0, 0))],
        out_specs=pl.BlockSpec((None, None, nc, HEAD_DIM), lambda bi, t: (bi, t, 0, 0)),
        out_shape=jax.ShapeDtypeStruct((b, 4, nc, HEAD_DIM), BF16),
        compiler_params=_cparams(("parallel", "arbitrary")),
        name="nsa_compress",
    )(r, pe, w1, w2)


def _nsa_kernel(qr_ref, qw_ref, dg_ref, gb_ref, kc_ref, vct_ref, ks_ref, vst_ref, kw_ref, vwt_ref,
                o_ref, b_ref, wb_ref, *bufs, tq, seq_len):
    tk = tq
    i = pl.program_id(1)
    nch = i + 1
    nc = seq_len // NSA_CMP_STRIDE
    n_sb = seq_len // NSA_SLC_BLOCK
    n_sel = min(NSA_SLC_TOPK, n_sb)
    k_loc = lax.broadcasted_iota(I32, (tk, tq), 0)
    q_loc = lax.broadcasted_iota(I32, (tk, tq), 1)
    qpos = i * tq + lax.broadcasted_iota(I32, (1, tq), 1)

    gates_t = jax.nn.sigmoid(dg_ref[...] + gb_ref[...]).T
    kc = kc_ref[...]
    cmp_end = lax.broadcasted_iota(I32, (nc, 1), 0) * NSA_CMP_STRIDE + (NSA_CMP_LEN - 1)
    cbias = jnp.where(cmp_end <= qpos, 0.0, NINF)

    q_rot, o_cmp = [], []
    psum = [jnp.zeros((nc, tq), F32), jnp.zeros((nc, tq), F32)]
    for p in range(D_HEADS // 2):
        q_rot.append(_split_halves(qr_ref[:, p * LANES:(p + 1) * LANES]))
        for g, qh in enumerate(_split_halves(qw_ref[:, p * LANES:(p + 1) * LANES])):
            s = _dot_nt(kc, qh) + cbias
            m = jnp.max(s, axis=0, keepdims=True)
            e = jnp.exp2(s - jnp.where(m == NINF, 0.0, m))
            pc = e * _safe_recip(jnp.sum(e, axis=0, keepdims=True))
            psum[g] = psum[g] + pc
            o_cmp.append(jnp.dot(vct_ref[g], pc.astype(BF16), preferred_element_type=F32))

    nbp = -(-n_sb // 8) * 8
    cn = lax.broadcasted_iota(I32, (nbp, nc), 1) * NSA_CMP_STRIDE
    sj = lax.broadcasted_iota(I32, (nbp, nc), 0) * NSA_SLC_BLOCK
    shares = jnp.where((cn <= sj + NSA_SLC_BLOCK - 1) & (cn + NSA_CMP_LEN - 1 >= sj), 1.0, 0.0)
    blk = lax.broadcasted_iota(I32, (nbp, tq), 0)
    cur = qpos // NSA_SLC_BLOCK
    causal_b = blk <= cur
    forced = (blk == 0) | ((blk >= cur - 1) & causal_b)
    for g in range(2):
        imp = jnp.dot(shares, psum[g], precision=lax.Precision.HIGHEST, preferred_element_type=F32)
        val = jnp.where(forced, BIG, jnp.where(causal_b, imp, NEG))
        val = jnp.where(blk < n_sb, val, NINF)
        selb = _top_n_mask(val, n_sel, 0).astype(BF16)

        def expand(c, _, selb=selb, g=g):
            kidx = c * tk + k_loc
            e = jnp.where((c * tk + lax.broadcasted_iota(I32, (tk, nbp), 0)) // NSA_SLC_BLOCK
                          == lax.broadcasted_iota(I32, (tk, nbp), 1), 1.0, 0.0).astype(BF16)
            selk = jnp.dot(e, selb, preferred_element_type=F32)
            b_ref[g, c] = jnp.where(selk > 0.5, jnp.where(kidx <= qpos, 0.0, NINF), NINF)
            return 0

        lax.fori_loop(0, nch, expand, 0)
        b_ref[g, nch] = jnp.full((tk, tq), NINF, F32)

    wb_ref[0] = jnp.where(k_loc <= q_loc, 0.0, NINF)
    wb_ref[1] = jnp.zeros((tk, tq), F32)
    wb_ref[2] = jnp.where(k_loc > q_loc, 0.0, NINF)
    wb_ref[3] = jnp.full((tk, tq), NINF, F32)
    n_wc = NSA_WINDOW // tk + 1
    w_first = jnp.maximum(i - (n_wc - 1), 0)
    n_w = i - w_first + 1

    for p in range(D_HEADS // 2):
        qs = q_rot[p]

        def qk_s(c, j, qs=qs):
            off = pl.multiple_of(c * tk, tk)
            return _dot_nt(ks_ref[pl.ds(off, tk), :], qs[j])

        o_slc = _pipe_flash(nch, 2, qk_s, lambda c, j: _with_ones(vst_ref[c, j * HEAD_DIM:(j + 1) * HEAD_DIM, :]),
                            lambda c, j: b_ref[j, jnp.minimum(c, nch)], bufs, HEAD_DIM + DV_PAD, tk, tq)

        def qk_w(c, j, qs=qs):
            off = pl.multiple_of((w_first + c) * tk, tk)
            return _dot_nt(kw_ref[pl.ds(off, tk), :], qs[j])

        def bias_w(c, j):
            d = i - (w_first + c)
            return wb_ref[jnp.where(c < n_w, d, n_wc)]

        o_win = _pipe_flash(n_w, 2, qk_w,
                            lambda c, j: _with_ones(vwt_ref[w_first + c, j * HEAD_DIM:(j + 1) * HEAD_DIM, :]), bias_w, bufs,
                            HEAD_DIM + DV_PAD, tk, tq)
        outs = []
        for g in range(2):
            h = g * (D_HEADS // 2) + p
            outs.append(gates_t[3 * h:3 * h + 1] * o_cmp[2 * p + g]
                        + gates_t[3 * h + 1:3 * h + 2] * _normalize_t(o_slc[g], HEAD_DIM)
                        + gates_t[3 * h + 2:3 * h + 3] * _normalize_t(o_win[g], HEAD_DIM))
        o_ref[:, p * LANES:(p + 1) * LANES] = jnp.concatenate(outs, axis=0).T.astype(o_ref.dtype)


def _nsa_attention(rp, pp, vt, dg, gate_b, kcmp, vcmp, tq=TQ):
    b, s, _ = rp.shape
    nk = s // tq
    nc = s // NSA_CMP_STRIDE
    assert NSA_WINDOW == 2 * tq
    n_wc = NSA_WINDOW // tq + 1
    vct = vcmp.reshape(b, nc, 2, HEAD_DIM).transpose(0, 2, 3, 1)
    full = lambda t: pl.BlockSpec((None, s, LANES), lambda bi, i: (bi, 0, t))
    vspec = lambda t: pl.BlockSpec((None, nk, LANES, tq), lambda bi, i: (bi, 0, t, 0))
    return pl.pallas_call(
        functools.partial(_nsa_kernel, tq=tq, seq_len=s),
        grid=(b, nk),
        in_specs=[pl.BlockSpec((None, tq, 4 * LANES), lambda bi, i: (bi, i, 2)),
                  pl.BlockSpec((None, tq, 4 * LANES), lambda bi, i: (bi, i, 0)),
                  pl.BlockSpec((None, tq, LANES), lambda bi, i: (bi, i, 0)),
                  pl.BlockSpec((1, LANES), lambda bi, i: (0, 0)),
                  pl.BlockSpec((None, nc, LANES), lambda bi, i: (bi, 0, 0)),
                  pl.BlockSpec((None, 2, HEAD_DIM, nc), lambda bi, i: (bi, 0, 0, 0)),
                  full(12), vspec(4), full(13), vspec(5)],
        out_specs=pl.BlockSpec((None, tq, 4 * LANES), lambda bi, i: (bi, i, 0)),
        out_shape=jax.ShapeDtypeStruct((b, s, 4 * LANES), BF16),
        scratch_shapes=[pltpu.VMEM((2, nk + 1, tq, tq), F32), pltpu.VMEM((n_wc + 1, tq, tq), F32)]
        + _pipe_scratch(2, tq, tq),
        compiler_params=_cparams(("parallel", "arbitrary")),
        name="nsa_attn",
    )(rp, pp, dg, gate_b, kcmp, vct, rp, vt, rp, vt)


def _moe_ffn_kernel(be_ref, nu_ref, x_ref, wg_ref, wu_ref, wd_ref, o_ref):
    used = pl.program_id(0) < nu_ref[0]

    @pl.when(used)
    def _():
        x = x_ref[...]
        h = jax.nn.silu(jnp.dot(x, wg_ref[...], preferred_element_type=F32)) * jnp.dot(
            x, wu_ref[...], preferred_element_type=F32)
        o_ref[...] = jnp.dot(h.astype(BF16), wd_ref[...], preferred_element_type=F32)

    @pl.when(jnp.logical_not(used))
    def _():
        o_ref[...] = jnp.zeros(o_ref.shape, o_ref.dtype)


def _moe_ffn(x_sorted, block_e, n_used, wg, wu, wd):
    ns, d = x_sorted.shape
    fdim = wg.shape[2]
    once = pl.Buffered(1)
    grid_spec = pltpu.PrefetchScalarGridSpec(
        num_scalar_prefetch=2,
        grid=(ns // MOE_TM,),
        in_specs=[pl.BlockSpec((MOE_TM, d), lambda i, be, nu: (jnp.minimum(i, nu[0] - 1), 0)),
                  pl.BlockSpec((None, d, fdim), lambda i, be, nu: (be[i], 0, 0), pipeline_mode=once),
                  pl.BlockSpec((None, d, fdim), lambda i, be, nu: (be[i], 0, 0), pipeline_mode=once),
                  pl.BlockSpec((None, fdim, d), lambda i, be, nu: (be[i], 0, 0), pipeline_mode=once)],
        out_specs=pl.BlockSpec((MOE_TM, d), lambda i, be, nu: (i, 0)),
    )
    return pl.pallas_call(
        _moe_ffn_kernel,
        grid_spec=grid_spec,
        out_shape=jax.ShapeDtypeStruct((ns, d), F32),
        compiler_params=_cparams(("arbitrary",)),
        name="moe_ffn",
    )(block_e, n_used, x_sorted, wg, wu, wd)


def _combine_ln_kernel(x_ref, y0_ref, y1_ref, rt_ref, g_ref, b_ref, xo_ref, xb_ref):
    rt = rt_ref[...]
    ffn = rt[:, 2:3] * y0_ref[...] + rt[:, 3:4] * y1_ref[...]
    y = _layer_norm(DN_ALPHA * x_ref[...] + ffn, g_ref[...], b_ref[...])
    xo_ref[...] = y
    xb_ref[...] = y.astype(BF16)


def _combine_ln(x, y0, y1, rt, g, b, tm=512):
    n, d = x.shape
    row = lambda i: (i, 0)
    fixed = lambda i: (0, 0)
    return pl.pallas_call(
        _combine_ln_kernel,
        grid=(n // tm,),
        in_specs=[pl.BlockSpec((tm, d), row), pl.BlockSpec((tm, d), row), pl.BlockSpec((tm, d), row),
                  pl.BlockSpec((tm, LANES), row), pl.BlockSpec((1, d), fixed), pl.BlockSpec((1, d), fixed)],
        out_specs=[pl.BlockSpec((tm, d), row), pl.BlockSpec((tm, d), row)],
        out_shape=[jax.ShapeDtypeStruct((n, d), F32), jax.ShapeDtypeStruct((n, d), BF16)],
        compiler_params=_cparams(("parallel",)),
        name="moe_combine_ln",
    )(x, y0, y1, rt, g.reshape(1, d), b.reshape(1, d))


def _moe_layout(rt, n):
    e_flat = rt[:, 0:TOP_K].astype(I32).reshape(-1)
    nk = n * TOP_K
    onehot = (e_flat[:, None] == jnp.arange(N_EXPERTS, dtype=I32)[None, :]).astype(I32)
    rank = jnp.take_along_axis(jnp.cumsum(onehot, axis=0), e_flat[:, None], axis=1)[:, 0] - 1
    counts = jnp.sum(onehot, axis=0)
    padded = (counts + MOE_TM - 1) // MOE_TM * MOE_TM
    pad_end = jnp.cumsum(padded)
    pad_start = pad_end - padded
    grp_start = jnp.cumsum(counts) - counts
    slot = pad_start[e_flat] + rank
    n_blocks = -(-nk // MOE_TM) + N_EXPERTS
    n_slots = n_blocks * MOE_TM
    order = jnp.argsort(e_flat, stable=True).astype(I32)
    sl = jnp.arange(n_slots, dtype=I32)
    slot_e = jnp.minimum(jnp.searchsorted(pad_end, sl, side='right'), N_EXPERTS - 1).astype(I32)
    within = sl - pad_start[slot_e]
    valid = within < counts[slot_e]
    src = jnp.where(valid, grp_start[slot_e] + within, 0)
    slot_tok = jnp.where(valid, order[src] // TOP_K, 0)
    n_used = (pad_end[-1] // MOE_TM).astype(I32).reshape(1)
    blk = jnp.arange(n_blocks, dtype=I32)
    block_e = slot_e[jnp.minimum(blk, n_used[0] - 1) * MOE_TM]
    return slot_tok, slot.reshape(n, TOP_K), block_e, n_used


def _pair_perm(n_heads):
    half = n_heads // 2
    cols = []
    for p in range(half):
        cols += list(range(p * HEAD_DIM, (p + 1) * HEAD_DIM))
        cols += list(range((half + p) * HEAD_DIM, (half + p + 1) * HEAD_DIM))
    return np.asarray(cols, dtype=np.int32)


def _pad_cols(w, width):
    return jnp.pad(w, ((0, 0), (0, width - w.shape[1])))


def _even_layer(x, xb, w_in, w_out, lam_params, subln, lam_init, wg, wu, wd, ln, tabs, bsz, seq_len):
    n, d = x.shape
    perm = _pair_perm(B_HEADS)
    aq, ak, av = w_in[:, 0:512], w_in[:, 512:1024], w_in[:, 1024:1536]
    bq, bk, bv = w_in[:, 1536:2048], w_in[:, 2048:2176], w_in[:, 2176:2304]
    iq, ik, iw = w_in[:, 2304:2560], w_in[:, 2560:2624], w_in[:, 2624:2628]
    w_rope = jnp.concatenate([aq * SCALE, ak, bq[:, perm] * SCALE, iq, bk, ik, ik], axis=1).astype(BF16)
    w_val = jnp.concatenate([av, bv], axis=1).astype(BF16)
    w_iw = _pad_cols(iw, LANES).astype(BF16)
    rp = _matmul(xb, w_rope, BF16, 512, 1024, seq_len, tabs).reshape(bsz, seq_len, -1)
    vt = _matmul_t(xb, w_val, 512, TQ).reshape(bsz, seq_len // TQ, -1, TQ)
    iwv = _matmul(xb, w_iw, F32, 512, LANES).reshape(bsz, seq_len, LANES)
    o_a = _diff_attention(rp, vt, lam_params, subln, lam_init)
    o_b = _dsa_attention(rp, vt, iwv)
    wo_a = w_out[0:512].astype(BF16)
    wo_b = w_out[512:1024][perm].astype(BF16)
    g_mix, b_mix, g_ffn, b_ffn = ln
    x1, x1b = _outproj_ln(x, o_a.reshape(n, -1), o_b.reshape(n, -1), wo_a, wo_b, g_mix, b_mix)
    return _ffn_ln(x1, x1b, wg.astype(BF16), wu.astype(BF16), wd.astype(BF16), g_ffn, b_ffn)


def _odd_layer(x, xb, w_in, w_out, gate_b, pe, phi_w1, phi_w2, w_router, b_router, wg, wu, wd, ln, tabs,
               bsz, seq_len):
    n, d = x.shape
    perm = _pair_perm(D_HEADS)
    cq, ck, cv = w_in[:, 0:512], w_in[:, 512:1024], w_in[:, 1024:1536]
    dq = w_in[:, 1536:2048][:, perm] * SCALE
    dkc, dvc, dks = w_in[:, 2048:2176], w_in[:, 2176:2304], w_in[:, 2304:2432]
    dvs, dkw, dvw = w_in[:, 2432:2560], w_in[:, 2560:2688], w_in[:, 2688:2816]
    dg = w_in[:, 2816:2840]
    w_rope = jnp.concatenate([cq * SCALE, ck, dq, dks, dkw], axis=1).astype(BF16)
    w_plain = jnp.concatenate([dq, dkc, dvc], axis=1).astype(BF16)
    w_val = jnp.concatenate([cv, dvs, dvw], axis=1).astype(BF16)
    w_dg = _pad_cols(dg, LANES).astype(BF16)
    rp = _matmul(xb, w_rope, BF16, 512, w_rope.shape[1] // 2, seq_len, tabs).reshape(bsz, seq_len, -1)
    pp = _matmul(xb, w_plain, BF16, 512, w_plain.shape[1]).reshape(bsz, seq_len, -1)
    vt = _matmul_t(xb, w_val, 512, TQ).reshape(bsz, seq_len // TQ, -1, TQ)
    dgv = _matmul(xb, w_dg, F32, 512, LANES).reshape(bsz, seq_len, LANES)

    o_c = _moba_attention(rp, vt)

    nc = seq_len // NSA_CMP_STRIDE
    tok = pp[:, :, 4 * LANES:6 * LANES].reshape(bsz, nc, NSA_CMP_STRIDE, 4, HEAD_DIM)
    r = tok.transpose(0, 3, 1, 2, 4).reshape(bsz, 4, nc, NSA_CMP_STRIDE * HEAD_DIM)
    pe_flat = jnp.pad(pe.reshape(2, 1, -1), ((0, 0), (0, 7), (0, 0))).astype(BF16)
    cmp = _nsa_compress(r, pe_flat, phi_w1.astype(BF16), phi_w2.astype(BF16))
    kcmp = jnp.concatenate([cmp[:, 0], cmp[:, 1]], axis=-1)
    vcmp = jnp.concatenate([cmp[:, 2], cmp[:, 3]], axis=-1)
    gb = _pad_cols(gate_b.reshape(1, -1), LANES)
    o_d = _nsa_attention(rp, pp, vt, dgv, gb, kcmp, vcmp)

    wo_c = w_out[0:512].astype(BF16)
    wo_d = w_out[512:1024][perm].astype(BF16)
    g_mix, b_mix, g_ffn, b_ffn = ln
    router = (_pad_cols(w_router, LANES), _pad_cols(b_router.reshape(1, -1), LANES))
    x1, x1b, rt = _outproj_ln(x, o_c.reshape(n, -1), o_d.reshape(n, -1), wo_c, wo_d, g_mix, b_mix, router)

    slot_tok, slot, block_e, n_used = _moe_layout(rt, n)
    y_slots = _moe_ffn(x1b[slot_tok], block_e, n_used, wg.astype(BF16), wu.astype(BF16), wd.astype(BF16))
    return _combine_ln(x1, y_slots[slot[:, 0]], y_slots[slot[:, 1]], rt, g_ffn, b_ffn)


@jax.jit
def kernel(x, ev_w_in, ev_w_out, dif_lambda, dif_subln, ffd_w_gate, ffd_w_up, ffd_w_down, od_w_in, od_w_out,
           nsa_gate_b, nsa_pe, nsa_phi_w1, nsa_phi_w2, moe_w_router, moe_b_router, moe_w_gate, moe_w_up,
           moe_w_down, ln_mix_g, ln_mix_b, ln_ffn_g, ln_ffn_b):
    bsz, seq_len, d = x.shape
    tabs = _rope_tables(seq_len)
    xf = x.reshape(bsz * seq_len, d)
    xb = xf.astype(BF16)
    for l in range(DEPTH):
        i = l // 2
        ln = (ln_mix_g[l], ln_mix_b[l], ln_ffn_g[l], ln_ffn_b[l])
        if l % 2 == 0:
            lam_init = 0.8 - 0.6 * math.exp(-0.3 * l)
            xf, xb = _even_layer(xf, xb, ev_w_in[i], ev_w_out[i], dif_lambda[i], dif_subln[i], lam_init,
                                 ffd_w_gate[i], ffd_w_up[i], ffd_w_down[i], ln, tabs, bsz, seq_len)
        else:
            xf, xb = _odd_layer(xf, xb, od_w_in[i], od_w_out[i], nsa_gate_b[i], nsa_pe[i], nsa_phi_w1[i],
                                nsa_phi_w2[i], moe_w_router[i], moe_b_router[i], moe_w_gate[i], moe_w_up[i],
                                moe_w_down[i], ln, tabs, bsz, seq_len)
    return xf.reshape(bsz, seq_len, d)
```

```python
import functools
import math

import numpy as np
import jax
import jax.numpy as jnp
from jax import lax
from jax.experimental import pallas as pl
from jax.experimental.pallas import tpu as pltpu

F32 = jnp.float32
BF16 = jnp.bfloat16
I32 = jnp.int32

LANES = 128
VMEM_LIMIT = 56 * 1024 * 1024

DEPTH = 4
HEAD_DIM = 64
ROPE_THETA = 10000.0
LN_EPS = 1e-5
DN_ALPHA = (2 * DEPTH) ** 0.25
SCALE = HEAD_DIM ** -0.5 * math.log2(math.e)
NEG = -1e30
BIG = 1e30
M_INIT = -1e30
NINF = float("-inf")

A_HEADS = 4
B_HEADS = 8
IDX_HEADS = 4
DSA_TOPK = 256
C_HEADS = 8
MOBA_BLOCK = 256
MOBA_TOPK = 3
D_HEADS = 8
NSA_CMP_LEN = 32
NSA_CMP_STRIDE = 16
NSA_SLC_BLOCK = 64
NSA_SLC_TOPK = 16
NSA_WINDOW = 512
NSA_PHI_HIDDEN = 256
N_EXPERTS = 8
TOP_K = 2
MOE_TM = 512
N_BISECT = 12


def _cparams(sem):
    return pltpu.CompilerParams(dimension_semantics=sem, vmem_limit_bytes=VMEM_LIMIT)


def _dot_nt(a, b):
    return lax.dot_general(a, b, (((1,), (1,)), ((), ())), preferred_element_type=F32)


def _layer_norm(y, g, b):
    mu = jnp.mean(y, axis=-1, keepdims=True)
    yc = y - mu
    var = jnp.mean(yc * yc, axis=-1, keepdims=True)
    return yc * lax.rsqrt(var + LN_EPS) * g + b


def _safe_recip(l):
    return jnp.where(l > 0.0, 1.0 / jnp.where(l > 0.0, l, 1.0), 0.0)


def _half_masks():
    lane = lax.broadcasted_iota(I32, (1, LANES), 1)
    return lane < HEAD_DIM


def _split_halves(t):
    lo = _half_masks()
    z = jnp.zeros_like(t)
    return jnp.where(lo, t, z), jnp.where(lo, z, t)


def _mm_kernel(x_ref, w_ref, *rest, rope):
    o_ref = rest[-1]
    acc = jnp.dot(x_ref[...].astype(BF16), w_ref[...], preferred_element_type=F32)
    if not rope:
        o_ref[...] = acc.astype(o_ref.dtype)
        return
    cos = rest[0][...]
    sin = rest[1][...]
    lane = lax.broadcasted_iota(I32, (1, LANES), 1)
    first = (lane % HEAD_DIM) < (HEAD_DIM // 2)
    for c in range(acc.shape[1] // LANES):
        a = acc[:, c * LANES:(c + 1) * LANES]
        rot = jnp.where(first, pltpu.roll(a, LANES - HEAD_DIM // 2, 1), pltpu.roll(a, HEAD_DIM // 2, 1))
        o_ref[:, c * LANES:(c + 1) * LANES] = (a * cos + rot * sin).astype(o_ref.dtype)


def _matmul(x, w, out_dtype, tm, tn, seq_len=None, rope_tabs=None):
    n, d = x.shape
    p = w.shape[1]
    rope = rope_tabs is not None
    in_specs = [pl.BlockSpec((tm, d), lambda i, j: (i, 0)),
                pl.BlockSpec((d, tn), lambda i, j: (0, j))]
    args = [x, w]
    if rope:
        nt = seq_len // tm
        in_specs += [pl.BlockSpec((tm, LANES), lambda i, j: (i % nt, 0))] * 2
        args += list(rope_tabs)
    return pl.pallas_call(
        functools.partial(_mm_kernel, rope=rope),
        grid=(n // tm, p // tn),
        in_specs=in_specs,
        out_specs=pl.BlockSpec((tm, tn), lambda i, j: (i, j)),
        out_shape=jax.ShapeDtypeStruct((n, p), out_dtype),
        compiler_params=_cparams(("parallel", "arbitrary")),
        name="proj_rope" if rope else "proj",
    )(*args)


def _rope_tables(seq_len):
    d = HEAD_DIM
    inv = ROPE_THETA ** (-jnp.arange(0, d, 2, dtype=F32) / d)
    ang = jnp.arange(seq_len, dtype=I32).astype(F32)[:, None] * inv[None, :]
    cos = jnp.cos(ang)
    sin = jnp.sin(ang)
    cos128 = jnp.tile(cos, (1, LANES // (d // 2)))
    sin128 = jnp.tile(jnp.concatenate([-sin, sin], axis=1), (1, LANES // d))
    return cos128, sin128


def _route_top2(x, w, b):
    logits = jnp.dot(x, w, precision=lax.Precision.HIGHEST, preferred_element_type=F32) + b
    lane = lax.broadcasted_iota(I32, (1, LANES), 1)
    lanef = lane.astype(F32)
    v = jnp.where(lane < N_EXPERTS, logits, NINF)
    l0 = jnp.max(v, axis=1, keepdims=True)
    i0 = jnp.min(jnp.where(v == l0, lanef, float(LANES)), axis=1, keepdims=True)
    v = jnp.where(lanef == i0, NINF, v)
    l1 = jnp.max(v, axis=1, keepdims=True)
    i1 = jnp.min(jnp.where(v == l1, lanef, float(LANES)), axis=1, keepdims=True)
    e1 = jnp.exp(l1 - l0)
    g0 = 1.0 / (1.0 + e1)
    g1 = e1 / (1.0 + e1)
    return jnp.where(lane == 0, i0, jnp.where(lane == 1, i1, jnp.where(lane == 2, g0, jnp.where(lane == 3, g1, 0.0))))


def _outproj_ln_kernel(x_ref, a_ref, b_ref, wa_ref, wb_ref, g_ref, bb_ref, *rest):
    mix = (jnp.dot(a_ref[...], wa_ref[...], preferred_element_type=F32)
           + jnp.dot(b_ref[...], wb_ref[...], preferred_element_type=F32))
    y = _layer_norm(DN_ALPHA * x_ref[...] + mix, g_ref[...], bb_ref[...])
    if len(rest) == 2:
        xo_ref, xb_ref = rest
    else:
        wr_ref, br_ref, xo_ref, xb_ref, rt_ref = rest
        rt_ref[...] = _route_top2(y, wr_ref[...], br_ref[...])
    xo_ref[...] = y
    xb_ref[...] = y.astype(BF16)


def _outproj_ln(x, oa, ob, wa, wb, g, b, router=None, tm=512):
    n, d = x.shape
    ka, kb = oa.shape[1], ob.shape[1]
    row = lambda i: (i, 0)
    fixed = lambda i: (0, 0)
    in_specs = [pl.BlockSpec((tm, d), row), pl.BlockSpec((tm, ka), row), pl.BlockSpec((tm, kb), row),
                pl.BlockSpec((ka, d), fixed), pl.BlockSpec((kb, d), fixed),
                pl.BlockSpec((1, d), fixed), pl.BlockSpec((1, d), fixed)]
    out_specs = [pl.BlockSpec((tm, d), row), pl.BlockSpec((tm, d), row)]
    out_shape = [jax.ShapeDtypeStruct((n, d), F32), jax.ShapeDtypeStruct((n, d), BF16)]
    args = [x, oa, ob, wa, wb, g.reshape(1, d), b.reshape(1, d)]
    if router is not None:
        in_specs += [pl.BlockSpec((d, LANES), fixed), pl.BlockSpec((1, LANES), fixed)]
        out_specs.append(pl.BlockSpec((tm, LANES), row))
        out_shape.append(jax.ShapeDtypeStruct((n, LANES), F32))
        args += list(router)
    return pl.pallas_call(
        _outproj_ln_kernel,
        grid=(n // tm,),
        in_specs=in_specs,
        out_specs=out_specs,
        out_shape=out_shape,
        compiler_params=_cparams(("parallel",)),
        name="outproj_ln",
    )(*args)


def _ffn_ln_kernel(x_ref, xb_ref, wg_ref, wu_ref, wd_ref, g_ref, b_ref, xo_ref, xob_ref):
    xb = xb_ref[...]
    h = jax.nn.silu(jnp.dot(xb, wg_ref[...], preferred_element_type=F32)) * jnp.dot(
        xb, wu_ref[...], preferred_element_type=F32)
    ffn = jnp.dot(h.astype(BF16), wd_ref[...], preferred_element_type=F32)
    y = _layer_norm(DN_ALPHA * x_ref[...] + ffn, g_ref[...], b_ref[...])
    xo_ref[...] = y
    xob_ref[...] = y.astype(BF16)


def _ffn_ln(x, xb, wg, wu, wd, g, b, tm=512):
    n, d = x.shape
    fdim = wg.shape[1]
    row = lambda i: (i, 0)
    fixed = lambda i: (0, 0)
    once = pl.Buffered(1)
    return pl.pallas_call(
        _ffn_ln_kernel,
        grid=(n // tm,),
        in_specs=[pl.BlockSpec((tm, d), row), pl.BlockSpec((tm, d), row),
                  pl.BlockSpec((d, fdim), fixed, pipeline_mode=once),
                  pl.BlockSpec((d, fdim), fixed, pipeline_mode=once),
                  pl.BlockSpec((fdim, d), fixed, pipeline_mode=once),
                  pl.BlockSpec((1, d), fixed), pl.BlockSpec((1, d), fixed)],
        out_specs=[pl.BlockSpec((tm, d), row), pl.BlockSpec((tm, d), row)],
        out_shape=[jax.ShapeDtypeStruct((n, d), F32), jax.ShapeDtypeStruct((n, d), BF16)],
        compiler_params=_cparams(("parallel",)),
        name="ffn_ln",
    )(x, xb, wg, wu, wd, g.reshape(1, d), b.reshape(1, d))


def _top_n_mask(v, n, axis):
    idx = lax.broadcasted_iota(I32, v.shape, axis).astype(F32)
    sel = jnp.zeros(v.shape, F32)
    for _ in range(n):
        mx = jnp.max(v, axis=axis, keepdims=True)
        first = jnp.min(jnp.where(v == mx, idx, float(v.shape[axis])), axis=axis, keepdims=True)
        pick = idx == first
        sel = jnp.where(pick, 1.0, sel)
        v = jnp.where(pick, NINF, v)
    return sel


DV_PAD = 16
TQ = 256


def _with_ones(vt):
    return jnp.concatenate([vt, jnp.ones((DV_PAD, vt.shape[1]), vt.dtype)], axis=0)


def _mm_t_kernel(x_ref, w_ref, o_ref, *, tk):
    acc = jnp.dot(x_ref[...].astype(BF16), w_ref[...], preferred_element_type=F32)
    for cc in range(acc.shape[0] // tk):
        o_ref[cc] = acc[cc * tk:(cc + 1) * tk, :].T.astype(o_ref.dtype)


def _matmul_t(x, w, tm, tk):
    n, d = x.shape
    p = w.shape[1]
    return pl.pallas_call(
        functools.partial(_mm_t_kernel, tk=tk),
        grid=(n // tm,),
        in_specs=[pl.BlockSpec((tm, d), lambda i: (i, 0)), pl.BlockSpec((d, p), lambda i: (0, 0))],
        out_specs=pl.BlockSpec((tm // tk, p, tk), lambda i: (i, 0, 0)),
        out_shape=jax.ShapeDtypeStruct((n // tk, p, tk), BF16),
        compiler_params=_cparams(("parallel",)),
        name="proj_t",
    )(x, w)


def _normalize_t(acc, width):
    return acc[:width] * _safe_recip(acc[width:width + 1])


def _pipe_flash(n, ns, qk, vt_at, bias_at, bufs, dv, tk, tq):
    sa, sb, pa, pb = bufs
    for j in range(ns):
        sa[j] = qk(0, j)
        pb[j] = jnp.zeros((tk, tq), BF16)

    def half(c, carry, s_cur, s_nxt, p_prev, p_cur):
        nxt = jnp.minimum(c + 1, n - 1)
        for j in range(ns):
            s_nxt[j] = qk(nxt, j)
        cp = jnp.clip(c - 1, 0, n - 1)
        out = []
        for j in range(ns):
            m, acc, alpha = carry[j]
            acc = alpha * acc + jnp.dot(vt_at(cp, j), p_prev[j], preferred_element_type=F32)
            st = s_cur[j] + bias_at(c, j)
            m_new = jnp.maximum(m, jnp.max(st, axis=0, keepdims=True))
            alpha = jnp.exp2(m - m_new)
            p_cur[j] = jnp.exp2((st - m_new).astype(BF16))
            out.append((m_new, acc, alpha))
        return tuple(out)

    def body(t, carry):
        carry = half(2 * t, carry, sa, sb, pb, pa)
        return half(2 * t + 1, carry, sb, sa, pa, pb)

    init = (jnp.full((1, tq), M_INIT, F32), jnp.zeros((dv, tq), F32), jnp.ones((1, tq), F32))
    trips = (n + 1) // 2
    carry = lax.fori_loop(0, trips, body, (init,) * ns)
    cl = jnp.minimum(2 * trips - 1, n - 1)
    outs = []
    for j in range(ns):
        _, acc, alpha = carry[j]
        outs.append(alpha * acc + jnp.dot(vt_at(cl, j), pb[j], preferred_element_type=F32))
    return outs


def _pipe_scratch(ns, tk, tq):
    return [pltpu.VMEM((ns, tk, tq), F32)] * 2 + [pltpu.VMEM((ns, tk, tq), BF16)] * 2


def _causal_t(t):
    return jnp.where(lax.broadcasted_iota(I32, (t, t), 0) <= lax.broadcasted_iota(I32, (t, t), 1), 0.0, NINF)


def _diff_kernel(lam_ref, sub_ref, q_ref, k_ref, vt_ref, o_ref, *bufs, tq, lam_init):
    i = pl.program_id(2)
    lp = lam_ref[...]
    lam = (jnp.exp(jnp.sum(lp[0:1] * lp[1:2], axis=1, keepdims=True))
           - jnp.exp(jnp.sum(lp[2:3] * lp[3:4], axis=1, keepdims=True)) + lam_init)
    qs = _split_halves(q_ref[...])
    n = i + 1
    bufs, tab_ref = bufs[:4], bufs[4]
    tab_ref[0] = jnp.zeros((tq, tq), F32)
    tab_ref[1] = _causal_t(tq)
    tab_ref[2] = jnp.full((tq, tq), NINF, F32)

    def qk(c, j):
        off = pl.multiple_of(c * tq, tq)
        return _dot_nt(k_ref[pl.ds(off, tq), :], qs[j])

    def bias_at(c, j):
        return tab_ref[jnp.where(c == i, 1, jnp.where(c < n, 0, 2))]

    outs = _pipe_flash(n, 2, qk, lambda c, j: _with_ones(vt_ref[c]), bias_at, bufs, LANES + DV_PAD, tq, tq)
    o = _normalize_t(outs[0], LANES) - lam * _normalize_t(outs[1], LANES)
    o = o * lax.rsqrt(jnp.mean(o * o, axis=0, keepdims=True) + LN_EPS)
    o = o * sub_ref[...] * (1.0 - lam_init)
    o_ref[...] = o.T.astype(o_ref.dtype)


def _diff_attention(rp, vt, lam_params, subln, lam_init, tq=TQ):
    b, s, _ = rp.shape
    nk = s // tq
    return pl.pallas_call(
        functools.partial(_diff_kernel, tq=tq, lam_init=lam_init),
        grid=(b, A_HEADS, nk),
        in_specs=[pl.BlockSpec((4, HEAD_DIM), lambda bi, h, i: (0, 0)),
                  pl.BlockSpec((LANES, 1), lambda bi, h, i: (0, 0)),
                  pl.BlockSpec((None, tq, LANES), lambda bi, h, i: (bi, i, h)),
                  pl.BlockSpec((None, s, LANES), lambda bi, h, i: (bi, 0, A_HEADS + h)),
                  pl.BlockSpec((None, nk, LANES, tq), lambda bi, h, i: (bi, 0, h, 0))],
        out_specs=pl.BlockSpec((None, tq, LANES), lambda bi, h, i: (bi, i, h)),
        out_shape=jax.ShapeDtypeStruct((b, s, A_HEADS * LANES), BF16),
        scratch_shapes=_pipe_scratch(2, tq, tq) + [pltpu.VMEM((3, tq, tq), F32)],
        compiler_params=_cparams(("parallel", "parallel", "arbitrary")),
        name="diff_attn",
    )(lam_params, subln.reshape(LANES, 1), rp, rp, vt)


def _fold8(x, op):
    acc = x[0:8]
    for r in range(1, x.shape[0] // 8):
        acc = op(acc, x[r * 8:(r + 1) * 8])
    return acc


def _dsa_kernel(iq_ref, ikk_ref, iw_ref, q_ref, k_ref, vt_ref, o_ref, s_ref, j_ref, *bufs, tq, ksel, seq_len):
    tk = tq
    i = pl.program_id(1)
    nch = i + 1
    ksel_f = float(ksel)
    k_loc = lax.broadcasted_iota(I32, (tk, tq), 0)
    q_loc = lax.broadcasted_iota(I32, (tk, tq), 1)
    qpos = i * tq + lax.broadcasted_iota(I32, (1, tq), 1)

    iq = iq_ref[...]
    iwt = iw_ref[...].T
    iqh = []
    for pair in range(IDX_HEADS // 2):
        iqh += list(_split_halves(iq[:, pair * LANES:(pair + 1) * LANES]))

    def scores(c):
        off = pl.multiple_of(c * tk, tk)
        kk = ikk_ref[pl.ds(off, tk), :]
        sc = iwt[0:1] * jnp.maximum(_dot_nt(kk, iqh[0]), 0.0)
        for h in range(1, IDX_HEADS):
            sc = sc + iwt[h:h + 1] * jnp.maximum(_dot_nt(kk, iqh[h]), 0.0)
        return sc

    def full_body(c, carry):
        mx, mn = carry
        sc = scores(c)
        s_ref[c] = sc
        return jnp.maximum(mx, _fold8(sc, jnp.maximum)), jnp.minimum(mn, _fold8(sc, jnp.minimum))

    mx, mn = lax.fori_loop(0, i, full_body, (jnp.full((8, tq), -BIG, F32), jnp.full((8, tq), BIG, F32)))
    sc = scores(i)
    causal = k_loc <= q_loc
    s_ref[i] = jnp.where(causal, sc, NEG)
    mx = jnp.maximum(mx, _fold8(jnp.where(causal, sc, -BIG), jnp.maximum))
    mn = jnp.minimum(mn, _fold8(jnp.where(causal, sc, BIG), jnp.minimum))
    smax = jnp.max(mx, axis=0, keepdims=True)
    smin = jnp.min(mn, axis=0, keepdims=True)
    s_ref[nch] = jnp.full((tk, tq), NINF, F32)

    def count_where(ind):
        def body(c, acc):
            return acc + _fold8(ind(s_ref[c], c * tk + k_loc), jnp.add)
        acc = lax.fori_loop(0, nch, body, jnp.zeros((8, tq), F32))
        return jnp.sum(acc, axis=0, keepdims=True)

    def count_ge(th):
        return count_where(lambda x, kidx: jnp.where(x >= th, 1.0, 0.0))

    def max_below(th):
        def body(c, acc):
            x = s_ref[c]
            return jnp.maximum(acc, _fold8(jnp.where(x < th, x, NINF), jnp.maximum))
        acc = lax.fori_loop(0, nch, body, jnp.full((8, tq), NINF, F32))
        return jnp.max(acc, axis=0, keepdims=True)

    n_causal = (qpos + 1).astype(F32)
    take_all = n_causal <= ksel_f
    done0 = jnp.where(take_all | (count_ge(smax) >= ksel_f), 1.0, 0.0)

    def bisect(lo, hi):
        mid = lo + (hi - lo) * 0.5
        ge = count_ge(mid) >= ksel_f
        return jnp.where(ge, mid, lo), jnp.where(ge, hi, mid)

    lo, hi = lax.fori_loop(0, N_BISECT, lambda _, c: bisect(*c), (smin, smax))

    def snap_body(carry):
        lo, hi, th, done, _ = carry
        lo, hi = bisect(lo, hi)
        t1 = max_below(hi)
        ok = count_ge(t1) >= ksel_f
        th = jnp.where(done > 0.0, th, t1)
        hi = jnp.where(ok, hi, t1)
        done = jnp.where(ok, 1.0, done)
        return lo, hi, th, done, jnp.sum(1.0 - done)

    _, _, th, _, _ = lax.while_loop(lambda c: c[4] > 0.0, snap_body,
                                    (lo, hi, smax, done0, jnp.sum(1.0 - done0)))

    c_ge = count_ge(th)
    need_tb = jnp.where(take_all, 0.0, jnp.where(c_ge > ksel_f, 1.0, 0.0))
    j_ref[...] = jnp.full((8, tq), seq_len - 1, I32)

    @pl.when(jnp.sum(need_tb) > 0.0)
    def _():
        need = ksel_f - count_where(lambda x, kidx: jnp.where(x > th, 1.0, 0.0))

        def jb(_, carry):
            lo_j, hi_j = carry
            mid = (lo_j + hi_j) // 2
            cnt = count_where(lambda x, kidx: jnp.where(x == th, jnp.where(kidx <= mid, 1.0, 0.0), 0.0))
            ge = cnt >= need
            return jnp.where(ge, lo_j, mid), jnp.where(ge, mid, hi_j)

        n_it = int(math.ceil(math.log2(seq_len))) + 1
        _, hi_j = lax.fori_loop(0, n_it, jb, (jnp.full((1, tq), -1, I32), jnp.full((1, tq), seq_len - 1, I32)))
        j_ref[...] = jnp.broadcast_to(hi_j, (8, tq))

    jsel = j_ref[0:1, :]

    def bias_body(c, _):
        x = s_ref[c]
        kidx = c * tk + k_loc
        keep = jnp.where(x > th, 0.0, jnp.where(x == th, jnp.where(kidx <= jsel, 0.0, NINF), NINF))
        keep = jnp.where(take_all, 0.0, keep)
        s_ref[c] = jnp.where(kidx <= qpos, keep, NINF)
        return 0

    lax.fori_loop(0, nch, bias_body, 0)

    for p in range(B_HEADS // 2):
        qs = _split_halves(q_ref[:, p * LANES:(p + 1) * LANES])

        def qk(c, j, qs=qs):
            off = pl.multiple_of(c * tk, tk)
            return _dot_nt(k_ref[pl.ds(off, tk), :], qs[j])

        outs = _pipe_flash(nch, 2, qk, lambda c, j: _with_ones(vt_ref[c, j * HEAD_DIM:(j + 1) * HEAD_DIM, :]),
                           lambda c, j: s_ref[jnp.minimum(c, nch)], bufs, HEAD_DIM + DV_PAD, tk, tq)
        o = jnp.concatenate([_normalize_t(outs[0], HEAD_DIM), _normalize_t(outs[1], HEAD_DIM)], axis=0)
        o_ref[:, p * LANES:(p + 1) * LANES] = o.T.astype(o_ref.dtype)


def _dsa_attention(rp, vt, iw, tq=TQ):
    b, s, _ = rp.shape
    ksel = min(DSA_TOPK, s // 4)
    nk = s // tq
    return pl.pallas_call(
        functools.partial(_dsa_kernel, tq=tq, ksel=ksel, seq_len=s),
        grid=(b, nk),
        in_specs=[pl.BlockSpec((None, tq, 2 * LANES), lambda bi, i: (bi, i, 6)),
                  pl.BlockSpec((None, s, LANES), lambda bi, i: (bi, 0, 15)),
                  pl.BlockSpec((None, tq, LANES), lambda bi, i: (bi, i, 0)),
                  pl.BlockSpec((None, tq, 4 * LANES), lambda bi, i: (bi, i, 2)),
                  pl.BlockSpec((None, s, LANES), lambda bi, i: (bi, 0, 14)),
                  pl.BlockSpec((None, nk, LANES, tq), lambda bi, i: (bi, 0, 4, 0))],
        out_specs=pl.BlockSpec((None, tq, 4 * LANES), lambda bi, i: (bi, i, 0)),
        out_shape=jax.ShapeDtypeStruct((b, s, 4 * LANES), BF16),
        scratch_shapes=[pltpu.VMEM((nk + 1, tq, tq), F32), pltpu.VMEM((8, tq), I32)] + _pipe_scratch(2, tq, tq),
        compiler_params=_cparams(("parallel", "arbitrary")),
        name="dsa_attn",
    )(rp, rp, iw, rp, rp, vt)


def _moba_kernel(q_ref, k_ref, vt_ref, o_ref, km_ref, sel_ref, *bufs, seq_len, n_sel):
    tq = MOBA_BLOCK
    qb = pl.program_id(2)

    @pl.when(qb == 0)
    def _():
        j = lax.broadcasted_iota(I32, (LANES, seq_len), 0)
        s = lax.broadcasted_iota(I32, (LANES, seq_len), 1)
        avg = jnp.where(s // MOBA_BLOCK == j, 1.0 / MOBA_BLOCK, 0.0).astype(BF16)
        km_ref[...] = jnp.dot(avg, k_ref[...], preferred_element_type=F32)

    nbp = sel_ref.shape[1]
    km = km_ref[0:nbp, :]
    qs = _split_halves(q_ref[...])
    blk = lax.broadcasted_iota(I32, (nbp, tq), 0)
    past = blk < qb
    for j in range(2):
        gate = lax.dot_general(km, qs[j].astype(F32), (((1,), (1,)), ((), ())),
                               precision=lax.Precision.HIGHEST, preferred_element_type=F32)
        gate = jnp.where(blk < seq_len // MOBA_BLOCK, jnp.where(past, gate, NEG), NINF)
        sel = _top_n_mask(gate, n_sel, 0)
        sel_ref[j] = jnp.where(past, jnp.where(sel > 0.5, 0.0, NINF), NINF)
    own = _causal_t(tq)
    n = qb + 1

    def qk(c, j):
        off = pl.multiple_of(c * tq, tq)
        return _dot_nt(k_ref[pl.ds(off, tq), :], qs[j])

    def bias_at(c, j):
        chosen = sel_ref[j, pl.ds(jnp.minimum(c, nbp - 1), 1), :]
        return jnp.where(c == qb, own, jnp.where(c < n, chosen, NINF))

    outs = _pipe_flash(n, 2, qk, lambda c, j: _with_ones(vt_ref[c, j * HEAD_DIM:(j + 1) * HEAD_DIM, :]), bias_at, bufs,
                       HEAD_DIM + DV_PAD, tq, tq)
    o = jnp.concatenate([_normalize_t(outs[0], HEAD_DIM), _normalize_t(outs[1], HEAD_DIM)], axis=0)
    o_ref[...] = o.T.astype(o_ref.dtype)


def _moba_attention(rp, vt):
    b, s, _ = rp.shape
    tq = MOBA_BLOCK
    nb = s // tq
    n_sel = max(1, min(MOBA_TOPK, nb - 1))
    npair = C_HEADS // 2
    return pl.pallas_call(
        functools.partial(_moba_kernel, seq_len=s, n_sel=n_sel),
        grid=(b, npair, nb),
        in_specs=[pl.BlockSpec((None, tq, LANES), lambda bi, h, i: (bi, i, h)),
                  pl.BlockSpec((None, s, LANES), lambda bi, h, i: (bi, 0, npair + h)),
                  pl.BlockSpec((None, nb, LANES, tq), lambda bi, h, i: (bi, 0, h, 0))],
        out_specs=pl.BlockSpec((None, tq, LANES), lambda bi, h, i: (bi, i, h)),
        out_shape=jax.ShapeDtypeStruct((b, s, npair * LANES), BF16),
        scratch_shapes=[pltpu.VMEM((LANES, LANES), F32), pltpu.VMEM((2, -(-nb // 8) * 8, tq), F32)]
        + _pipe_scratch(2, tq, tq),
        compiler_params=_cparams(("parallel", "parallel", "arbitrary")),
        name="moba_attn",
    )(rp, rp, vt)


def _cmp_kernel(r_ref, pe_ref, w1_ref, w2_ref, o_ref):
    r = r_ref[...]
    w1 = w1_ref[...]
    half = r.shape[1]
    u = jnp.dot(r, w1[:half], preferred_element_type=F32)
    v = jnp.dot(r, w1[half:], preferred_element_type=F32)
    c = jnp.dot(pe_ref[...], w1, preferred_element_type=F32)[0:1]
    pre = u + pltpu.roll(v, r.shape[0] - 1, 0) + c
    o_ref[...] = jnp.dot(jax.nn.gelu(pre).astype(BF16), w2_ref[...],
                         preferred_element_type=F32).astype(o_ref.dtype)


def _nsa_compress(r, pe, w1, w2):
    b, _, nc, wdt = r.shape
    hid = w1.shape[2]
    return pl.pallas_call(
        _cmp_kernel,
        grid=(b, 4),
        in_specs=[pl.BlockSpec((None, None, nc, wdt), lambda bi, t: (bi, t, 0, 0)),
                  pl.BlockSpec((None, 8, 2 * wdt), lambda bi, t: (t // 2, 0, 0)),
                  pl.BlockSpec((None, 2 * wdt, hid), lambda bi, t: (t // 2, 0, 0)),
                  pl.BlockSpec((None, hid, HEAD_DIM), lambda bi, t: (t // 2, 0, 0))],
        out_specs=pl.BlockSpec((None, None, nc, HEAD_DIM), lambda bi, t: (bi, t, 0, 0)),
        out_shape=jax.ShapeDtypeStruct((b, 4, nc, HEAD_DIM), BF16),
        compiler_params=_cparams(("parallel", "arbitrary")),
        name="nsa_compress",
    )(r, pe, w1, w2)


def _nsa_kernel(qr_ref, qw_ref, dg_ref, gb_ref, kc_ref, vct_ref, ks_ref, vst_ref, kw_ref, vwt_ref,
                o_ref, sel_ref, wb_ref, *bufs, tq, seq_len):
    tk = tq
    i = pl.program_id(1)
    nch = i + 1
    nc = seq_len // NSA_CMP_STRIDE
    n_sb = seq_len // NSA_SLC_BLOCK
    n_sel = min(NSA_SLC_TOPK, n_sb)
    k_loc = lax.broadcasted_iota(I32, (tk, tq), 0)
    q_loc = lax.broadcasted_iota(I32, (tk, tq), 1)
    qpos = i * tq + lax.broadcasted_iota(I32, (1, tq), 1)

    gates_t = jax.nn.sigmoid(dg_ref[...] + gb_ref[...]).T
    kc = kc_ref[...]
    cmp_end = lax.broadcasted_iota(I32, (nc, 1), 0) * NSA_CMP_STRIDE + (NSA_CMP_LEN - 1)
    cbias = jnp.where(cmp_end <= qpos, 0.0, NINF)

    q_rot, o_cmp = [], []
    psum = [jnp.zeros((nc, tq), F32), jnp.zeros((nc, tq), F32)]
    for p in range(D_HEADS // 2):
        q_rot.append(_split_halves(qr_ref[:, p * LANES:(p + 1) * LANES]))
        for g, qh in enumerate(_split_halves(qw_ref[:, p * LANES:(p + 1) * LANES])):
            s = _dot_nt(kc, qh) + cbias
            m = jnp.max(s, axis=0, keepdims=True)
            e = jnp.exp2(s - jnp.where(m == NINF, 0.0, m))
            pc = e * _safe_recip(jnp.sum(e, axis=0, keepdims=True))
            psum[g] = psum[g] + pc
            o_cmp.append(jnp.dot(vct_ref[g], pc.astype(BF16), preferred_element_type=F32))

    per = tk // NSA_SLC_BLOCK
    nbp = -(-n_sb // 8) * 8
    cn = lax.broadcasted_iota(I32, (nbp, nc), 1) * NSA_CMP_STRIDE
    sj = lax.broadcasted_iota(I32, (nbp, nc), 0) * NSA_SLC_BLOCK
    shares = jnp.where((cn <= sj + NSA_SLC_BLOCK - 1) & (cn + NSA_CMP_LEN - 1 >= sj), 1.0, 0.0)
    blk = lax.broadcasted_iota(I32, (nbp, tq), 0)
    cur = qpos // NSA_SLC_BLOCK
    causal_b = blk <= cur
    forced = (blk == 0) | ((blk >= cur - 1) & causal_b)
    for g in range(2):
        imp = jnp.dot(shares, psum[g], precision=lax.Precision.HIGHEST, preferred_element_type=F32)
        val = jnp.where(forced, BIG, jnp.where(causal_b, imp, NEG))
        val = jnp.where(blk < n_sb, val, NINF)
        rowb = jnp.where(_top_n_mask(val, n_sel, 0) > 0.5, 0.0, NINF)
        for c in range(seq_len // tk):
            sel_ref[g, c, 0:per] = rowb[c * per:(c + 1) * per]

    wb_ref[0] = jnp.where(k_loc <= q_loc, 0.0, NINF)
    wb_ref[1] = jnp.zeros((tk, tq), F32)
    wb_ref[2] = jnp.where(k_loc > q_loc, 0.0, NINF)
    wb_ref[3] = jnp.full((tk, tq), NINF, F32)
    n_wc = NSA_WINDOW // tk + 1
    w_first = jnp.maximum(i - (n_wc - 1), 0)
    n_w = i - w_first + 1

    for p in range(D_HEADS // 2):
        qs = q_rot[p]

        def qk_s(c, j, qs=qs):
            off = pl.multiple_of(c * tk, tk)
            return _dot_nt(ks_ref[pl.ds(off, tk), :], qs[j])

        def bias_s(c, j):
            rows = sel_ref[j, jnp.minimum(c, seq_len // tk - 1)]
            tile = jnp.concatenate([jnp.broadcast_to(rows[r:r + 1], (NSA_SLC_BLOCK, tq)) for r in range(per)], axis=0)
            return tile + wb_ref[jnp.where(c == i, 0, jnp.where(c < nch, 1, n_wc))]

        o_slc = _pipe_flash(nch, 2, qk_s, lambda c, j: _with_ones(vst_ref[c, j * HEAD_DIM:(j + 1) * HEAD_DIM, :]),
                            bias_s, bufs, HEAD_DIM + DV_PAD, tk, tq)

        def qk_w(c, j, qs=qs):
            off = pl.multiple_of((w_first + c) * tk, tk)
            return _dot_nt(kw_ref[pl.ds(off, tk), :], qs[j])

        def bias_w(c, j):
            d = i - (w_first + c)
            return wb_ref[jnp.where(c < n_w, d, n_wc)]

        o_win = _pipe_flash(n_w, 2, qk_w,
                            lambda c, j: _with_ones(vwt_ref[w_first + c, j * HEAD_DIM:(j + 1) * HEAD_DIM, :]), bias_w, bufs,
                            HEAD_DIM + DV_PAD, tk, tq)
        outs = []
        for g in range(2):
            h = g * (D_HEADS // 2) + p
            outs.append(gates_t[3 * h:3 * h + 1] * o_cmp[2 * p + g]
                        + gates_t[3 * h + 1:3 * h + 2] * _normalize_t(o_slc[g], HEAD_DIM)
                        + gates_t[3 * h + 2:3 * h + 3] * _normalize_t(o_win[g], HEAD_DIM))
        o_ref[:, p * LANES:(p + 1) * LANES] = jnp.concatenate(outs, axis=0).T.astype(o_ref.dtype)


def _nsa_attention(rp, pp, vt, dg, gate_b, kcmp, vcmp, tq=TQ):
    b, s, _ = rp.shape
    nk = s // tq
    nc = s // NSA_CMP_STRIDE
    assert NSA_WINDOW == 2 * tq
    n_wc = NSA_WINDOW // tq + 1
    vct = vcmp.reshape(b, nc, 2, HEAD_DIM).transpose(0, 2, 3, 1)
    full = lambda t: pl.BlockSpec((None, s, LANES), lambda bi, i: (bi, 0, t))
    vspec = lambda t: pl.BlockSpec((None, nk, LANES, tq), lambda bi, i: (bi, 0, t, 0))
    return pl.pallas_call(
        functools.partial(_nsa_kernel, tq=tq, seq_len=s),
        grid=(b, nk),
        in_specs=[pl.BlockSpec((None, tq, 4 * LANES), lambda bi, i: (bi, i, 2)),
                  pl.BlockSpec((None, tq, 4 * LANES), lambda bi, i: (bi, i, 0)),
                  pl.BlockSpec((None, tq, LANES), lambda bi, i: (bi, i, 0)),
                  pl.BlockSpec((1, LANES), lambda bi, i: (0, 0)),
                  pl.BlockSpec((None, nc, LANES), lambda bi, i: (bi, 0, 0)),
                  pl.BlockSpec((None, 2, HEAD_DIM, nc), lambda bi, i: (bi, 0, 0, 0)),
                  full(12), vspec(4), full(13), vspec(5)],
        out_specs=pl.BlockSpec((None, tq, 4 * LANES), lambda bi, i: (bi, i, 0)),
        out_shape=jax.ShapeDtypeStruct((b, s, 4 * LANES), BF16),
        scratch_shapes=[pltpu.VMEM((2, nk, 8, tq), F32), pltpu.VMEM((n_wc + 1, tq, tq), F32)]
        + _pipe_scratch(2, tq, tq),
        compiler_params=_cparams(("parallel", "arbitrary")),
        name="nsa_attn",
    )(rp, pp, dg, gate_b, kcmp, vct, rp, vt, rp, vt)


def _moe_ffn_kernel(be_ref, nu_ref, x_ref, wg_ref, wu_ref, wd_ref, o_ref):
    used = pl.program_id(0) < nu_ref[0]

    @pl.when(used)
    def _():
        x = x_ref[...]
        h = jax.nn.silu(jnp.dot(x, wg_ref[...], preferred_element_type=F32)) * jnp.dot(
            x, wu_ref[...], preferred_element_type=F32)
        o_ref[...] = jnp.dot(h.astype(BF16), wd_ref[...], preferred_element_type=F32)

    @pl.when(jnp.logical_not(used))
    def _():
        o_ref[...] = jnp.zeros(o_ref.shape, o_ref.dtype)


def _moe_ffn(x_sorted, block_e, n_used, wg, wu, wd):
    ns, d = x_sorted.shape
    fdim = wg.shape[2]
    once = pl.Buffered(1)
    grid_spec = pltpu.PrefetchScalarGridSpec(
        num_scalar_prefetch=2,
        grid=(ns // MOE_TM,),
        in_specs=[pl.BlockSpec((MOE_TM, d), lambda i, be, nu: (jnp.minimum(i, nu[0] - 1), 0)),
                  pl.BlockSpec((None, d, fdim), lambda i, be, nu: (be[i], 0, 0), pipeline_mode=once),
                  pl.BlockSpec((None, d, fdim), lambda i, be, nu: (be[i], 0, 0), pipeline_mode=once),
                  pl.BlockSpec((None, fdim, d), lambda i, be, nu: (be[i], 0, 0), pipeline_mode=once)],
        out_specs=pl.BlockSpec((MOE_TM, d), lambda i, be, nu: (i, 0)),
    )
    return pl.pallas_call(
        _moe_ffn_kernel,
        grid_spec=grid_spec,
        out_shape=jax.ShapeDtypeStruct((ns, d), F32),
        compiler_params=_cparams(("arbitrary",)),
        name="moe_ffn",
    )(block_e, n_used, x_sorted, wg, wu, wd)


def _combine_ln_kernel(x_ref, y0_ref, y1_ref, rt_ref, g_ref, b_ref, xo_ref, xb_ref):
    rt = rt_ref[...]
    ffn = rt[:, 2:3] * y0_ref[...] + rt[:, 3:4] * y1_ref[...]
    y = _layer_norm(DN_ALPHA * x_ref[...] + ffn, g_ref[...], b_ref[...])
    xo_ref[...] = y
    xb_ref[...] = y.astype(BF16)


def _combine_ln(x, y0, y1, rt, g, b, tm=512):
    n, d = x.shape
    row = lambda i: (i, 0)
    fixed = lambda i: (0, 0)
    return pl.pallas_call(
        _combine_ln_kernel,
        grid=(n // tm,),
        in_specs=[pl.BlockSpec((tm, d), row), pl.BlockSpec((tm, d), row), pl.BlockSpec((tm, d), row),
                  pl.BlockSpec((tm, LANES), row), pl.BlockSpec((1, d), fixed), pl.BlockSpec((1, d), fixed)],
        out_specs=[pl.BlockSpec((tm, d), row), pl.BlockSpec((tm, d), row)],
        out_shape=[jax.ShapeDtypeStruct((n, d), F32), jax.ShapeDtypeStruct((n, d), BF16)],
        compiler_params=_cparams(("parallel",)),
        name="moe_combine_ln",
    )(x, y0, y1, rt, g.reshape(1, d), b.reshape(1, d))


def _moe_layout(rt, n):
    e_flat = rt[:, 0:TOP_K].astype(I32).reshape(-1)
    nk = n * TOP_K
    onehot = (e_flat[:, None] == jnp.arange(N_EXPERTS, dtype=I32)[None, :]).astype(I32)
    rank = jnp.take_along_axis(jnp.cumsum(onehot, axis=0), e_flat[:, None], axis=1)[:, 0] - 1
    counts = jnp.sum(onehot, axis=0)
    padded = (counts + MOE_TM - 1) // MOE_TM * MOE_TM
    pad_end = jnp.cumsum(padded)
    pad_start = pad_end - padded
    grp_start = jnp.cumsum(counts) - counts
    slot = pad_start[e_flat] + rank
    n_blocks = -(-nk // MOE_TM) + N_EXPERTS
    n_slots = n_blocks * MOE_TM
    order = jnp.argsort(e_flat, stable=True).astype(I32)
    sl = jnp.arange(n_slots, dtype=I32)
    slot_e = jnp.minimum(jnp.searchsorted(pad_end, sl, side='right'), N_EXPERTS - 1).astype(I32)
    within = sl - pad_start[slot_e]
    valid = within < counts[slot_e]
    src = jnp.where(valid, grp_start[slot_e] + within, 0)
    slot_tok = jnp.where(valid, order[src] // TOP_K, 0)
    n_used = (pad_end[-1] // MOE_TM).astype(I32).reshape(1)
    blk = jnp.arange(n_blocks, dtype=I32)
    block_e = slot_e[jnp.minimum(blk, n_used[0] - 1) * MOE_TM]
    return slot_tok, slot.reshape(n, TOP_K), block_e, n_used


def _pair_perm(n_heads):
    half = n_heads // 2
    cols = []
    for p in range(half):
        cols += list(range(p * HEAD_DIM, (p + 1) * HEAD_DIM))
        cols += list(range((half + p) * HEAD_DIM, (half + p + 1) * HEAD_DIM))
    return np.asarray(cols, dtype=np.int32)


def _pad_cols(w, width):
    return jnp.pad(w, ((0, 0), (0, width - w.shape[1])))


def _even_layer(x, xb, w_in, w_out, lam_params, subln, lam_init, wg, wu, wd, ln, tabs, bsz, seq_len):
    n, d = x.shape
    perm = _pair_perm(B_HEADS)
    aq, ak, av = w_in[:, 0:512], w_in[:, 512:1024], w_in[:, 1024:1536]
    bq, bk, bv = w_in[:, 1536:2048], w_in[:, 2048:2176], w_in[:, 2176:2304]
    iq, ik, iw = w_in[:, 2304:2560], w_in[:, 2560:2624], w_in[:, 2624:2628]
    w_rope = jnp.concatenate([aq * SCALE, ak, bq[:, perm] * SCALE, iq, bk, ik, ik], axis=1).astype(BF16)
    w_val = jnp.concatenate([av, bv], axis=1).astype(BF16)
    w_iw = _pad_cols(iw, LANES).astype(BF16)
    rp = _matmul(xb, w_rope, BF16, 512, 1024, seq_len, tabs).reshape(bsz, seq_len, -1)
    vt = _matmul_t(xb, w_val, 512, TQ).reshape(bsz, seq_len // TQ, -1, TQ)
    iwv = _matmul(xb, w_iw, F32, 512, LANES).reshape(bsz, seq_len, LANES)
    o_a = _diff_attention(rp, vt, lam_params, subln, lam_init)
    o_b = _dsa_attention(rp, vt, iwv)
    wo_a = w_out[0:512].astype(BF16)
    wo_b = w_out[512:1024][perm].astype(BF16)
    g_mix, b_mix, g_ffn, b_ffn = ln
    x1, x1b = _outproj_ln(x, o_a.reshape(n, -1), o_b.reshape(n, -1), wo_a, wo_b, g_mix, b_mix)
    return _ffn_ln(x1, x1b, wg.astype(BF16), wu.astype(BF16), wd.astype(BF16), g_ffn, b_ffn)


def _odd_layer(x, xb, w_in, w_out, gate_b, pe, phi_w1, phi_w2, w_router, b_router, wg, wu, wd, ln, tabs,
               bsz, seq_len):
    n, d = x.shape
    perm = _pair_perm(D_HEADS)
    cq, ck, cv = w_in[:, 0:512], w_in[:, 512:1024], w_in[:, 1024:1536]
    dq = w_in[:, 1536:2048][:, perm] * SCALE
    dkc, dvc, dks = w_in[:, 2048:2176], w_in[:, 2176:2304], w_in[:, 2304:2432]
    dvs, dkw, dvw = w_in[:, 2432:2560], w_in[:, 2560:2688], w_in[:, 2688:2816]
    dg = w_in[:, 2816:2840]
    w_rope = jnp.concatenate([cq * SCALE, ck, dq, dks, dkw], axis=1).astype(BF16)
    w_plain = jnp.concatenate([dq, dkc, dvc], axis=1).astype(BF16)
    w_val = jnp.concatenate([cv, dvs, dvw], axis=1).astype(BF16)
    w_dg = _pad_cols(dg, LANES).astype(BF16)
    rp = _matmul(xb, w_rope, BF16, 512, w_rope.shape[1] // 2, seq_len, tabs).reshape(bsz, seq_len, -1)
    pp = _matmul(xb, w_plain, BF16, 512, w_plain.shape[1]).reshape(bsz, seq_len, -1)
    vt = _matmul_t(xb, w_val, 512, TQ).reshape(bsz, seq_len // TQ, -1, TQ)
    dgv = _matmul(xb, w_dg, F32, 512, LANES).reshape(bsz, seq_len, LANES)

    o_c = _moba_attention(rp, vt)

    nc = seq_len // NSA_CMP_STRIDE
    tok = pp[:, :, 4 * LANES:6 * LANES].reshape(bsz, nc, NSA_CMP_STRIDE, 4, HEAD_DIM)
    r = tok.transpose(0, 3, 1, 2, 4).reshape(bsz, 4, nc, NSA_CMP_STRIDE * HEAD_DIM)
    pe_flat = jnp.pad(pe.reshape(2, 1, -1), ((0, 0), (0, 7), (0, 0))).astype(BF16)
    cmp = _nsa_compress(r, pe_flat, phi_w1.astype(BF16), phi_w2.astype(BF16))
    kcmp = jnp.concatenate([cmp[:, 0], cmp[:, 1]], axis=-1)
    vcmp = jnp.concatenate([cmp[:, 2], cmp[:, 3]], axis=-1)
    gb = _pad_cols(gate_b.reshape(1, -1), LANES)
    o_d = _nsa_attention(rp, pp, vt, dgv, gb, kcmp, vcmp)

    wo_c = w_out[0:512].astype(BF16)
    wo_d = w_out[512:1024][perm].astype(BF16)
    g_mix, b_mix, g_ffn, b_ffn = ln
    router = (_pad_cols(w_router, LANES), _pad_cols(b_router.reshape(1, -1), LANES))
    x1, x1b, rt = _outproj_ln(x, o_c.reshape(n, -1), o_d.reshape(n, -1), wo_c, wo_d, g_mix, b_mix, router)

    slot_tok, slot, block_e, n_used = _moe_layout(rt, n)
    y_slots = _moe_ffn(x1b[slot_tok], block_e, n_used, wg.astype(BF16), wu.astype(BF16), wd.astype(BF16))
    return _combine_ln(x1, y_slots[slot[:, 0]], y_slots[slot[:, 1]], rt, g_ffn, b_ffn)


@jax.jit
def kernel(x, ev_w_in, ev_w_out, dif_lambda, dif_subln, ffd_w_gate, ffd_w_up, ffd_w_down, od_w_in, od_w_out,
           nsa_gate_b, nsa_pe, nsa_phi_w1, nsa_phi_w2, moe_w_router, moe_b_router, moe_w_gate, moe_w_up,
           moe_w_down, ln_mix_g, ln_mix_b, ln_ffn_g, ln_ffn_b):
    bsz, seq_len, d = x.shape
    tabs = _rope_tables(seq_len)
    xf = x.reshape(bsz * seq_len, d)
    xb = xf.astype(BF16)
    for l in range(DEPTH):
        i = l // 2
        ln = (ln_mix_g[l], ln_mix_b[l], ln_ffn_g[l], ln_ffn_b[l])
        if l % 2 == 0:
            lam_init = 0.8 - 0.6 * math.exp(-0.3 * l)
            xf, xb = _even_layer(xf, xb, ev_w_in[i], ev_w_out[i], dif_lambda[i], dif_subln[i], lam_init,
                                 ffd_w_gate[i], ffd_w_up[i], ffd_w_down[i], ln, tabs, bsz, seq_len)
        else:
            xf, xb = _odd_layer(xf, xb, od_w_in[i], od_w_out[i], nsa_gate_b[i], nsa_pe[i], nsa_phi_w1[i],
                                nsa_phi_w2[i], moe_w_router[i], moe_b_router[i], moe_w_gate[i], moe_w_up[i],
                                moe_w_down[i], ln, tabs, bsz, seq_len)
    return xf.reshape(bsz, seq_len, d)
```

```python
import functools
import math

import numpy as np
import jax
import jax.numpy as jnp
from jax import lax
from jax.experimental import pallas as pl
from jax.experimental.pallas import tpu as pltpu

F32 = jnp.float32
BF16 = jnp.bfloat16
I32 = jnp.int32

LANES = 128
VMEM_LIMIT = 56 * 1024 * 1024

DEPTH = 4
HEAD_DIM = 64
ROPE_THETA = 10000.0
LN_EPS = 1e-5
DN_ALPHA = (2 * DEPTH) ** 0.25
SCALE = HEAD_DIM ** -0.5 * math.log2(math.e)
NEG = -1e30
BIG = 1e30
M_INIT = -1e30
NINF = float("-inf")

A_HEADS = 4
B_HEADS = 8
IDX_HEADS = 4
DSA_TOPK = 256
C_HEADS = 8
MOBA_BLOCK = 256
MOBA_TOPK = 3
D_HEADS = 8
NSA_CMP_LEN = 32
NSA_CMP_STRIDE = 16
NSA_SLC_BLOCK = 64
NSA_SLC_TOPK = 16
NSA_WINDOW = 512
NSA_PHI_HIDDEN = 256
N_EXPERTS = 8
TOP_K = 2
MOE_TM = 512
N_BISECT = 12


def _cparams(sem):
    return pltpu.CompilerParams(dimension_semantics=sem, vmem_limit_bytes=VMEM_LIMIT)


def _dot_nt(a, b):
    return lax.dot_general(a, b, (((1,), (1,)), ((), ())), preferred_element_type=F32)


def _layer_norm(y, g, b):
    mu = jnp.mean(y, axis=-1, keepdims=True)
    yc = y - mu
    var = jnp.mean(yc * yc, axis=-1, keepdims=True)
    return yc * lax.rsqrt(var + LN_EPS) * g + b


def _safe_recip(l):
    return jnp.where(l > 0.0, 1.0 / jnp.where(l > 0.0, l, 1.0), 0.0)


def _half_masks():
    lane = lax.broadcasted_iota(I32, (1, LANES), 1)
    return lane < HEAD_DIM


def _split_halves(t):
    lo = _half_masks()
    z = jnp.zeros_like(t)
    return jnp.where(lo, t, z), jnp.where(lo, z, t)


def _mm_kernel(x_ref, w_ref, *rest, rope):
    o_ref = rest[-1]
    acc = jnp.dot(x_ref[...].astype(BF16), w_ref[...], preferred_element_type=F32)
    if not rope:
        o_ref[...] = acc.astype(o_ref.dtype)
        return
    cos = rest[0][...]
    sin = rest[1][...]
    lane = lax.broadcasted_iota(I32, (1, LANES), 1)
    first = (lane % HEAD_DIM) < (HEAD_DIM // 2)
    for c in range(acc.shape[1] // LANES):
        a = acc[:, c * LANES:(c + 1) * LANES]
        rot = jnp.where(first, pltpu.roll(a, LANES - HEAD_DIM // 2, 1), pltpu.roll(a, HEAD_DIM // 2, 1))
        o_ref[:, c * LANES:(c + 1) * LANES] = (a * cos + rot * sin).astype(o_ref.dtype)


def _matmul(x, w, out_dtype, tm, tn, seq_len=None, rope_tabs=None):
    n, d = x.shape
    p = w.shape[1]
    rope = rope_tabs is not None
    in_specs = [pl.BlockSpec((tm, d), lambda i, j: (i, 0)),
                pl.BlockSpec((d, tn), lambda i, j: (0, j))]
    args = [x, w]
    if rope:
        nt = seq_len // tm
        in_specs += [pl.BlockSpec((tm, LANES), lambda i, j: (i % nt, 0))] * 2
        args += list(rope_tabs)
    return pl.pallas_call(
        functools.partial(_mm_kernel, rope=rope),
        grid=(n // tm, p // tn),
        in_specs=in_specs,
        out_specs=pl.BlockSpec((tm, tn), lambda i, j: (i, j)),
        out_shape=jax.ShapeDtypeStruct((n, p), out_dtype),
        compiler_params=_cparams(("parallel", "arbitrary")),
        name="proj_rope" if rope else "proj",
    )(*args)


def _rope_tables(seq_len):
    d = HEAD_DIM
    inv = ROPE_THETA ** (-jnp.arange(0, d, 2, dtype=F32) / d)
    ang = jnp.arange(seq_len, dtype=I32).astype(F32)[:, None] * inv[None, :]
    cos = jnp.cos(ang)
    sin = jnp.sin(ang)
    cos128 = jnp.tile(cos, (1, LANES // (d // 2)))
    sin128 = jnp.tile(jnp.concatenate([-sin, sin], axis=1), (1, LANES // d))
    return cos128, sin128


def _route_top2(x, w, b):
    logits = jnp.dot(x, w, precision=lax.Precision.HIGHEST, preferred_element_type=F32) + b
    lane = lax.broadcasted_iota(I32, (1, LANES), 1)
    lanef = lane.astype(F32)
    v = jnp.where(lane < N_EXPERTS, logits, NINF)
    l0 = jnp.max(v, axis=1, keepdims=True)
    i0 = jnp.min(jnp.where(v == l0, lanef, float(LANES)), axis=1, keepdims=True)
    v = jnp.where(lanef == i0, NINF, v)
    l1 = jnp.max(v, axis=1, keepdims=True)
    i1 = jnp.min(jnp.where(v == l1, lanef, float(LANES)), axis=1, keepdims=True)
    e1 = jnp.exp(l1 - l0)
    g0 = 1.0 / (1.0 + e1)
    g1 = e1 / (1.0 + e1)
    return jnp.where(lane == 0, i0, jnp.where(lane == 1, i1, jnp.where(lane == 2, g0, jnp.where(lane == 3, g1, 0.0))))


def _outproj_ln_kernel(x_ref, a_ref, b_ref, wa_ref, wb_ref, g_ref, bb_ref, *rest):
    mix = (jnp.dot(a_ref[...], wa_ref[...], preferred_element_type=F32)
           + jnp.dot(b_ref[...], wb_ref[...], preferred_element_type=F32))
    y = _layer_norm(DN_ALPHA * x_ref[...] + mix, g_ref[...], bb_ref[...])
    if len(rest) == 2:
        xo_ref, xb_ref = rest
    else:
        wr_ref, br_ref, xo_ref, xb_ref, rt_ref = rest
        rt_ref[...] = _route_top2(y, wr_ref[...], br_ref[...])
    xo_ref[...] = y
    xb_ref[...] = y.astype(BF16)


def _outproj_ln(x, oa, ob, wa, wb, g, b, router=None, tm=512):
    n, d = x.shape
    ka, kb = oa.shape[1], ob.shape[1]
    row = lambda i: (i, 0)
    fixed = lambda i: (0, 0)
    in_specs = [pl.BlockSpec((tm, d), row), pl.BlockSpec((tm, ka), row), pl.BlockSpec((tm, kb), row),
                pl.BlockSpec((ka, d), fixed), pl.BlockSpec((kb, d), fixed),
                pl.BlockSpec((1, d), fixed), pl.BlockSpec((1, d), fixed)]
    out_specs = [pl.BlockSpec((tm, d), row), pl.BlockSpec((tm, d), row)]
    out_shape = [jax.ShapeDtypeStruct((n, d), F32), jax.ShapeDtypeStruct((n, d), BF16)]
    args = [x, oa, ob, wa, wb, g.reshape(1, d), b.reshape(1, d)]
    if router is not None:
        in_specs += [pl.BlockSpec((d, LANES), fixed), pl.BlockSpec((1, LANES), fixed)]
        out_specs.append(pl.BlockSpec((tm, LANES), row))
        out_shape.append(jax.ShapeDtypeStruct((n, LANES), F32))
        args += list(router)
    return pl.pallas_call(
        _outproj_ln_kernel,
        grid=(n // tm,),
        in_specs=in_specs,
        out_specs=out_specs,
        out_shape=out_shape,
        compiler_params=_cparams(("parallel",)),
        name="outproj_ln",
    )(*args)


def _ffn_ln_kernel(x_ref, xb_ref, wg_ref, wu_ref, wd_ref, g_ref, b_ref, xo_ref, xob_ref):
    xb = xb_ref[...]
    h = jax.nn.silu(jnp.dot(xb, wg_ref[...], preferred_element_type=F32)) * jnp.dot(
        xb, wu_ref[...], preferred_element_type=F32)
    ffn = jnp.dot(h.astype(BF16), wd_ref[...], preferred_element_type=F32)
    y = _layer_norm(DN_ALPHA * x_ref[...] + ffn, g_ref[...], b_ref[...])
    xo_ref[...] = y
    xob_ref[...] = y.astype(BF16)


def _ffn_ln(x, xb, wg, wu, wd, g, b, tm=512):
    n, d = x.shape
    fdim = wg.shape[1]
    row = lambda i: (i, 0)
    fixed = lambda i: (0, 0)
    once = pl.Buffered(1)
    return pl.pallas_call(
        _ffn_ln_kernel,
        grid=(n // tm,),
        in_specs=[pl.BlockSpec((tm, d), row), pl.BlockSpec((tm, d), row),
                  pl.BlockSpec((d, fdim), fixed, pipeline_mode=once),
                  pl.BlockSpec((d, fdim), fixed, pipeline_mode=once),
                  pl.BlockSpec((fdim, d), fixed, pipeline_mode=once),
                  pl.BlockSpec((1, d), fixed), pl.BlockSpec((1, d), fixed)],
        out_specs=[pl.BlockSpec((tm, d), row), pl.BlockSpec((tm, d), row)],
        out_shape=[jax.ShapeDtypeStruct((n, d), F32), jax.ShapeDtypeStruct((n, d), BF16)],
        compiler_params=_cparams(("parallel",)),
        name="ffn_ln",
    )(x, xb, wg, wu, wd, g.reshape(1, d), b.reshape(1, d))


def _top_n_mask(v, n, axis):
    idx = lax.broadcasted_iota(I32, v.shape, axis).astype(F32)
    sel = jnp.zeros(v.shape, F32)
    for _ in range(n):
        mx = jnp.max(v, axis=axis, keepdims=True)
        first = jnp.min(jnp.where(v == mx, idx, float(v.shape[axis])), axis=axis, keepdims=True)
        pick = idx == first
        sel = jnp.where(pick, 1.0, sel)
        v = jnp.where(pick, NINF, v)
    return sel


DV_PAD = 16
TQ = 256


def _with_ones(vt):
    return jnp.concatenate([vt, jnp.ones((DV_PAD, vt.shape[1]), vt.dtype)], axis=0)


def _mm_t_kernel(x_ref, w_ref, o_ref, *, tk):
    acc = jnp.dot(x_ref[...].astype(BF16), w_ref[...], preferred_element_type=F32)
    for cc in range(acc.shape[0] // tk):
        o_ref[cc] = acc[cc * tk:(cc + 1) * tk, :].T.astype(o_ref.dtype)


def _matmul_t(x, w, tm, tk):
    n, d = x.shape
    p = w.shape[1]
    return pl.pallas_call(
        functools.partial(_mm_t_kernel, tk=tk),
        grid=(n // tm,),
        in_specs=[pl.BlockSpec((tm, d), lambda i: (i, 0)), pl.BlockSpec((d, p), lambda i: (0, 0))],
        out_specs=pl.BlockSpec((tm // tk, p, tk), lambda i: (i, 0, 0)),
        out_shape=jax.ShapeDtypeStruct((n // tk, p, tk), BF16),
        compiler_params=_cparams(("parallel",)),
        name="proj_t",
    )(x, w)


def _normalize_t(acc, width):
    return acc[:width] * _safe_recip(acc[width:width + 1])


def _pipe_flash(n, ns, qk, vt_at, bias_at, bufs, dv, tk, tq):
    sa, sb, pa, pb = bufs
    for j in range(ns):
        sa[j] = qk(0, j)
        pb[j] = jnp.zeros((tk, tq), BF16)

    def half(c, carry, s_cur, s_nxt, p_prev, p_cur):
        nxt = jnp.minimum(c + 1, n - 1)
        for j in range(ns):
            s_nxt[j] = qk(nxt, j)
        cp = jnp.clip(c - 1, 0, n - 1)
        out = []
        for j in range(ns):
            m, acc, alpha = carry[j]
            acc = alpha * acc + jnp.dot(vt_at(cp, j), p_prev[j], preferred_element_type=F32)
            st = s_cur[j] + bias_at(c, j)
            m_new = jnp.maximum(m, jnp.max(st, axis=0, keepdims=True))
            alpha = jnp.exp2(m - m_new)
            p_cur[j] = jnp.exp2((st - m_new).astype(BF16))
            out.append((m_new, acc, alpha))
        return tuple(out)

    def body(t, carry):
        carry = half(2 * t, carry, sa, sb, pb, pa)
        return half(2 * t + 1, carry, sb, sa, pa, pb)

    init = (jnp.full((1, tq), M_INIT, F32), jnp.zeros((dv, tq), F32), jnp.ones((1, tq), F32))
    trips = (n + 1) // 2
    carry = lax.fori_loop(0, trips, body, (init,) * ns)
    cl = jnp.minimum(2 * trips - 1, n - 1)
    outs = []
    for j in range(ns):
        _, acc, alpha = carry[j]
        outs.append(alpha * acc + jnp.dot(vt_at(cl, j), pb[j], preferred_element_type=F32))
    return outs


def _pipe_scratch(ns, tk, tq):
    return [pltpu.VMEM((ns, tk, tq), F32)] * 2 + [pltpu.VMEM((ns, tk, tq), BF16)] * 2


def _causal_t(t):
    return jnp.where(lax.broadcasted_iota(I32, (t, t), 0) <= lax.broadcasted_iota(I32, (t, t), 1), 0.0, NINF)


def _diff_kernel(lam_ref, sub_ref, q_ref, k_ref, vt_ref, o_ref, *bufs, tq, lam_init):
    i = pl.program_id(2)
    lp = lam_ref[...]
    lam = (jnp.exp(jnp.sum(lp[0:1] * lp[1:2], axis=1, keepdims=True))
           - jnp.exp(jnp.sum(lp[2:3] * lp[3:4], axis=1, keepdims=True)) + lam_init)
    qs = _split_halves(q_ref[...])
    n = i + 1
    bufs, tab_ref = bufs[:4], bufs[4]
    tab_ref[0] = jnp.zeros((tq, tq), F32)
    tab_ref[1] = _causal_t(tq)
    tab_ref[2] = jnp.full((tq, tq), NINF, F32)

    def qk(c, j):
        off = pl.multiple_of(c * tq, tq)
        return _dot_nt(k_ref[pl.ds(off, tq), :], qs[j])

    def bias_at(c, j):
        return tab_ref[jnp.where(c == i, 1, jnp.where(c < n, 0, 2))]

    outs = _pipe_flash(n, 2, qk, lambda c, j: _with_ones(vt_ref[c]), bias_at, bufs, LANES + DV_PAD, tq, tq)
    o = _normalize_t(outs[0], LANES) - lam * _normalize_t(outs[1], LANES)
    o = o * lax.rsqrt(jnp.mean(o * o, axis=0, keepdims=True) + LN_EPS)
    o = o * sub_ref[...] * (1.0 - lam_init)
    o_ref[...] = o.T.astype(o_ref.dtype)


def _diff_attention(rp, vt, lam_params, subln, lam_init, tq=TQ):
    b, s, _ = rp.shape
    nk = s // tq
    return pl.pallas_call(
        functools.partial(_diff_kernel, tq=tq, lam_init=lam_init),
        grid=(b, A_HEADS, nk),
        in_specs=[pl.BlockSpec((4, HEAD_DIM), lambda bi, h, i: (0, 0)),
                  pl.BlockSpec((LANES, 1), lambda bi, h, i: (0, 0)),
                  pl.BlockSpec((None, tq, LANES), lambda bi, h, i: (bi, i, h)),
                  pl.BlockSpec((None, s, LANES), lambda bi, h, i: (bi, 0, A_HEADS + h)),
                  pl.BlockSpec((None, nk, LANES, tq), lambda bi, h, i: (bi, 0, h, 0))],
        out_specs=pl.BlockSpec((None, tq, LANES), lambda bi, h, i: (bi, i, h)),
        out_shape=jax.ShapeDtypeStruct((b, s, A_HEADS * LANES), BF16),
        scratch_shapes=_pipe_scratch(2, tq, tq) + [pltpu.VMEM((3, tq, tq), F32)],
        compiler_params=_cparams(("parallel", "parallel", "arbitrary")),
        name="diff_attn",
    )(lam_params, subln.reshape(LANES, 1), rp, rp, vt)


def _fold8(x, op):
    acc = x[0:8]
    for r in range(1, x.shape[0] // 8):
        acc = op(acc, x[r * 8:(r + 1) * 8])
    return acc


def _dsa_kernel(iq_ref, ikk_ref, iw_ref, q_ref, k_ref, vt_ref, o_ref, s_ref, j_ref, *bufs, tq, ksel, seq_len):
    tk = tq
    i = pl.program_id(1)
    nch = i + 1
    ksel_f = float(ksel)
    k_loc = lax.broadcasted_iota(I32, (tk, tq), 0)
    q_loc = lax.broadcasted_iota(I32, (tk, tq), 1)
    qpos = i * tq + lax.broadcasted_iota(I32, (1, tq), 1)

    iq = iq_ref[...]
    iwt = iw_ref[...].T
    iqh = []
    for pair in range(IDX_HEADS // 2):
        iqh += list(_split_halves(iq[:, pair * LANES:(pair + 1) * LANES]))

    def scores(c):
        off = pl.multiple_of(c * tk, tk)
        kk = ikk_ref[pl.ds(off, tk), :]
        sc = iwt[0:1] * jnp.maximum(_dot_nt(kk, iqh[0]), 0.0)
        for h in range(1, IDX_HEADS):
            sc = sc + iwt[h:h + 1] * jnp.maximum(_dot_nt(kk, iqh[h]), 0.0)
        return sc

    def full_body(c, carry):
        mx, mn = carry
        sc = scores(c)
        s_ref[c] = sc
        return jnp.maximum(mx, _fold8(sc, jnp.maximum)), jnp.minimum(mn, _fold8(sc, jnp.minimum))

    mx, mn = lax.fori_loop(0, i, full_body, (jnp.full((8, tq), -BIG, F32), jnp.full((8, tq), BIG, F32)))
    sc = scores(i)
    causal = k_loc <= q_loc
    s_ref[i] = jnp.where(causal, sc, NEG)
    mx = jnp.maximum(mx, _fold8(jnp.where(causal, sc, -BIG), jnp.maximum))
    mn = jnp.minimum(mn, _fold8(jnp.where(causal, sc, BIG), jnp.minimum))
    smax = jnp.max(mx, axis=0, keepdims=True)
    smin = jnp.min(mn, axis=0, keepdims=True)
    s_ref[nch] = jnp.full((tk, tq), NINF, F32)

    def count_where(ind):
        def body(c, acc):
            return acc + _fold8(ind(s_ref[c], c * tk + k_loc), jnp.add)
        acc = lax.fori_loop(0, nch, body, jnp.zeros((8, tq), F32))
        return jnp.sum(acc, axis=0, keepdims=True)

    def count_ge(th):
        return count_where(lambda x, kidx: jnp.where(x >= th, 1.0, 0.0))

    def max_below(th):
        def body(c, acc):
            x = s_ref[c]
            return jnp.maximum(acc, _fold8(jnp.where(x < th, x, NINF), jnp.maximum))
        acc = lax.fori_loop(0, nch, body, jnp.full((8, tq), NINF, F32))
        return jnp.max(acc, axis=0, keepdims=True)

    n_causal = (qpos + 1).astype(F32)
    take_all = n_causal <= ksel_f
    done0 = jnp.where(take_all | (count_ge(smax) >= ksel_f), 1.0, 0.0)

    def bisect(lo, hi):
        mid = lo + (hi - lo) * 0.5
        ge = count_ge(mid) >= ksel_f
        return jnp.where(ge, mid, lo), jnp.where(ge, hi, mid)

    lo, hi = lax.fori_loop(0, N_BISECT, lambda _, c: bisect(*c), (smin, smax))

    def snap_body(carry):
        lo, hi, th, done, _ = carry
        lo, hi = bisect(lo, hi)
        t1 = max_below(hi)
        ok = count_ge(t1) >= ksel_f
        th = jnp.where(done > 0.0, th, t1)
        hi = jnp.where(ok, hi, t1)
        done = jnp.where(ok, 1.0, done)
        return lo, hi, th, done, jnp.sum(1.0 - done)

    _, _, th, _, _ = lax.while_loop(lambda c: c[4] > 0.0, snap_body,
                                    (lo, hi, smax, done0, jnp.sum(1.0 - done0)))

    c_ge = count_ge(th)
    need_tb = jnp.where(take_all, 0.0, jnp.where(c_ge > ksel_f, 1.0, 0.0))
    j_ref[...] = jnp.full((8, tq), seq_len - 1, I32)

    @pl.when(jnp.sum(need_tb) > 0.0)
    def _():
        need = ksel_f - count_where(lambda x, kidx: jnp.where(x > th, 1.0, 0.0))

        def jb(_, carry):
            lo_j, hi_j = carry
            mid = (lo_j + hi_j) // 2
            cnt = count_where(lambda x, kidx: jnp.where(x == th, jnp.where(kidx <= mid, 1.0, 0.0), 0.0))
            ge = cnt >= need
            return jnp.where(ge, lo_j, mid), jnp.where(ge, mid, hi_j)

        n_it = int(math.ceil(math.log2(seq_len))) + 1
        _, hi_j = lax.fori_loop(0, n_it, jb, (jnp.full((1, tq), -1, I32), jnp.full((1, tq), seq_len - 1, I32)))
        j_ref[...] = jnp.broadcast_to(hi_j, (8, tq))

    jsel = j_ref[0:1, :]

    def bias_body(c, _):
        x = s_ref[c]
        kidx = c * tk + k_loc
        keep = jnp.where(x > th, 0.0, jnp.where(x == th, jnp.where(kidx <= jsel, 0.0, NINF), NINF))
        keep = jnp.where(take_all, 0.0, keep)
        s_ref[c] = jnp.where(kidx <= qpos, keep, NINF)
        return 0

    lax.fori_loop(0, nch, bias_body, 0)

    for p in range(B_HEADS // 2):
        qs = _split_halves(q_ref[:, p * LANES:(p + 1) * LANES])

        def qk(c, j, qs=qs):
            off = pl.multiple_of(c * tk, tk)
            return _dot_nt(k_ref[pl.ds(off, tk), :], qs[j])

        outs = _pipe_flash(nch, 2, qk, lambda c, j: _with_ones(vt_ref[c, j * HEAD_DIM:(j + 1) * HEAD_DIM, :]),
                           lambda c, j: s_ref[jnp.minimum(c, nch)], bufs, HEAD_DIM + DV_PAD, tk, tq)
        o = jnp.concatenate([_normalize_t(outs[0], HEAD_DIM), _normalize_t(outs[1], HEAD_DIM)], axis=0)
        o_ref[:, p * LANES:(p + 1) * LANES] = o.T.astype(o_ref.dtype)


def _dsa_attention(rp, vt, iw, tq=TQ):
    b, s, _ = rp.shape
    ksel = min(DSA_TOPK, s // 4)
    nk = s // tq
    return pl.pallas_call(
        functools.partial(_dsa_kernel, tq=tq, ksel=ksel, seq_len=s),
        grid=(b, nk),
        in_specs=[pl.BlockSpec((None, tq, 2 * LANES), lambda bi, i: (bi, i, 6)),
                  pl.BlockSpec((None, s, LANES), lambda bi, i: (bi, 0, 15)),
                  pl.BlockSpec((None, tq, LANES), lambda bi, i: (bi, i, 0)),
                  pl.BlockSpec((None, tq, 4 * LANES), lambda bi, i: (bi, i, 2)),
                  pl.BlockSpec((None, s, LANES), lambda bi, i: (bi, 0, 14)),
                  pl.BlockSpec((None, nk, LANES, tq), lambda bi, i: (bi, 0, 4, 0))],
        out_specs=pl.BlockSpec((None, tq, 4 * LANES), lambda bi, i: (bi, i, 0)),
        out_shape=jax.ShapeDtypeStruct((b, s, 4 * LANES), BF16),
        scratch_shapes=[pltpu.VMEM((nk + 1, tq, tq), F32), pltpu.VMEM((8, tq), I32)] + _pipe_scratch(2, tq, tq),
        compiler_params=_cparams(("parallel", "arbitrary")),
        name="dsa_attn",
    )(rp, rp, iw, rp, rp, vt)


def _moba_kernel(q_ref, k_ref, vt_ref, o_ref, km_ref, sel_ref, *bufs, seq_len, n_sel):
    tq = MOBA_BLOCK
    qb = pl.program_id(2)

    @pl.when(qb == 0)
    def _():
        j = lax.broadcasted_iota(I32, (LANES, seq_len), 0)
        s = lax.broadcasted_iota(I32, (LANES, seq_len), 1)
        avg = jnp.where(s // MOBA_BLOCK == j, 1.0 / MOBA_BLOCK, 0.0).astype(BF16)
        km_ref[...] = jnp.dot(avg, k_ref[...], preferred_element_type=F32)

    nbp = sel_ref.shape[1]
    km = km_ref[0:nbp, :]
    qs = _split_halves(q_ref[...])
    blk = lax.broadcasted_iota(I32, (nbp, tq), 0)
    past = blk < qb
    for j in range(2):
        gate = lax.dot_general(km, qs[j].astype(F32), (((1,), (1,)), ((), ())),
                               precision=lax.Precision.HIGHEST, preferred_element_type=F32)
        gate = jnp.where(blk < seq_len // MOBA_BLOCK, jnp.where(past, gate, NEG), NINF)
        sel = _top_n_mask(gate, n_sel, 0)
        sel_ref[j] = jnp.where(past, jnp.where(sel > 0.5, 0.0, NINF), NINF)
    own = _causal_t(tq)
    n = qb + 1

    def qk(c, j):
        off = pl.multiple_of(c * tq, tq)
        return _dot_nt(k_ref[pl.ds(off, tq), :], qs[j])

    def bias_at(c, j):
        chosen = sel_ref[j, pl.ds(jnp.minimum(c, nbp - 1), 1), :]
        return jnp.where(c == qb, own, jnp.where(c < n, chosen, NINF))

    outs = _pipe_flash(n, 2, qk, lambda c, j: _with_ones(vt_ref[c, j * HEAD_DIM:(j + 1) * HEAD_DIM, :]), bias_at, bufs,
                       HEAD_DIM + DV_PAD, tq, tq)
    o = jnp.concatenate([_normalize_t(outs[0], HEAD_DIM), _normalize_t(outs[1], HEAD_DIM)], axis=0)
    o_ref[...] = o.T.astype(o_ref.dtype)


def _moba_attention(rp, vt):
    b, s, _ = rp.shape
    tq = MOBA_BLOCK
    nb = s // tq
    n_sel = max(1, min(MOBA_TOPK, nb - 1))
    npair = C_HEADS // 2
    return pl.pallas_call(
        functools.partial(_moba_kernel, seq_len=s, n_sel=n_sel),
        grid=(b, npair, nb),
        in_specs=[pl.BlockSpec((None, tq, LANES), lambda bi, h, i: (bi, i, h)),
                  pl.BlockSpec((None, s, LANES), lambda bi, h, i: (bi, 0, npair + h)),
                  pl.BlockSpec((None, nb, LANES, tq), lambda bi, h, i: (bi, 0, h, 0))],
        out_specs=pl.BlockSpec((None, tq, LANES), lambda bi, h, i: (bi, i, h)),
        out_shape=jax.ShapeDtypeStruct((b, s, npair * LANES), BF16),
        scratch_shapes=[pltpu.VMEM((LANES, LANES), F32), pltpu.VMEM((2, -(-nb // 8) * 8, tq), F32)]
        + _pipe_scratch(2, tq, tq),
        compiler_params=_cparams(("parallel", "parallel", "arbitrary")),
        name="moba_attn",
    )(rp, rp, vt)


def _cmp_kernel(r_ref, pe_ref, w1_ref, w2_ref, o_ref):
    r = r_ref[...]
    w1 = w1_ref[...]
    half = r.shape[1]
    u = jnp.dot(r, w1[:half], preferred_element_type=F32)
    v = jnp.dot(r, w1[half:], preferred_element_type=F32)
    c = jnp.dot(pe_ref[...], w1, preferred_element_type=F32)[0:1]
    pre = u + pltpu.roll(v, r.shape[0] - 1, 0) + c
    o_ref[...] = jnp.dot(jax.nn.gelu(pre).astype(BF16), w2_ref[...],
                         preferred_element_type=F32).astype(o_ref.dtype)


def _nsa_compress(r, pe, w1, w2):
    b, _, nc, wdt = r.shape
    hid = w1.shape[2]
    return pl.pallas_call(
        _cmp_kernel,
        grid=(b, 4),
        in_specs=[pl.BlockSpec((None, None, nc, wdt), lambda bi, t: (bi, t, 0, 0)),
                  pl.BlockSpec((None, 8, 2 * wdt), lambda bi, t: (t // 2, 0, 0)),
                  pl.BlockSpec((None, 2 * wdt, hid), lambda bi, t: (t // 2, 0, 0)),
                  pl.BlockSpec((None, hid, HEAD_DIM), lambda bi, t: (t // 2, 0, 0))],
        out_specs=pl.BlockSpec((None, None, nc, HEAD_DIM), lambda bi, t: (bi, t, 0, 0)),
        out_shape=jax.ShapeDtypeStruct((b, 4, nc, HEAD_DIM), BF16),
        compiler_params=_cparams(("parallel", "arbitrary")),
        name="nsa_compress",
    )(r, pe, w1, w2)


def _nsa_kernel(qr_ref, qw_ref, dg_ref, gb_ref, kc_ref, vct_ref, ks_ref, vst_ref, kw_ref, vwt_ref,
                o_ref, sel_ref, wb_ref, *bufs, tq, seq_len):
    tk = tq
    i = pl.program_id(1)
    nch = i + 1
    nc = seq_len // NSA_CMP_STRIDE
    n_sb = seq_len // NSA_SLC_BLOCK
    n_sel = min(NSA_SLC_TOPK, n_sb)
    k_loc = lax.broadcasted_iota(I32, (tk, tq), 0)
    q_loc = lax.broadcasted_iota(I32, (tk, tq), 1)
    qpos = i * tq + lax.broadcasted_iota(I32, (1, tq), 1)

    gates_t = jax.nn.sigmoid(dg_ref[...] + gb_ref[...]).T
    kc = kc_ref[...]
    cmp_end = lax.broadcasted_iota(I32, (nc, 1), 0) * NSA_CMP_STRIDE + (NSA_CMP_LEN - 1)
    cbias = jnp.where(cmp_end <= qpos, 0.0, NINF)

    q_rot, o_cmp = [], []
    psum = [jnp.zeros((nc, tq), F32), jnp.zeros((nc, tq), F32)]
    for p in range(D_HEADS // 2):
        q_rot.append(_split_halves(qr_ref[:, p * LANES:(p + 1) * LANES]))
        for g, qh in enumerate(_split_halves(qw_ref[:, p * LANES:(p + 1) * LANES])):
            s = _dot_nt(kc, qh) + cbias
            m = jnp.max(s, axis=0, keepdims=True)
            e = jnp.exp2(s - jnp.where(m == NINF, 0.0, m))
            pc = e * _safe_recip(jnp.sum(e, axis=0, keepdims=True))
            psum[g] = psum[g] + pc
            o_cmp.append(jnp.dot(vct_ref[g], pc.astype(BF16), preferred_element_type=F32))

    per = tk // NSA_SLC_BLOCK
    nbp = -(-n_sb // 8) * 8
    cn = lax.broadcasted_iota(I32, (nbp, nc), 1) * NSA_CMP_STRIDE
    sj = lax.broadcasted_iota(I32, (nbp, nc), 0) * NSA_SLC_BLOCK
    shares = jnp.where((cn <= sj + NSA_SLC_BLOCK - 1) & (cn + NSA_CMP_LEN - 1 >= sj), 1.0, 0.0)
    blk = lax.broadcasted_iota(I32, (nbp, tq), 0)
    cur = qpos // NSA_SLC_BLOCK
    causal_b = blk <= cur
    forced = (blk == 0) | ((blk >= cur - 1) & causal_b)
    for g in range(2):
        imp = jnp.dot(shares, psum[g], precision=lax.Precision.HIGHEST, preferred_element_type=F32)
        val = jnp.where(forced, BIG, jnp.where(causal_b, imp, NEG))
        val = jnp.where(blk < n_sb, val, NINF)
        rowb = jnp.where(_top_n_mask(val, n_sel, 0) > 0.5, 0.0, NINF)
        for c in range(seq_len // tk):
            sel_ref[g, c, 0:per] = rowb[c * per:(c + 1) * per]

    wb_ref[0] = jnp.where(k_loc <= q_loc, 0.0, NINF)
    wb_ref[1] = jnp.zeros((tk, tq), F32)
    wb_ref[2] = jnp.where(k_loc > q_loc, 0.0, NINF)
    wb_ref[3] = jnp.full((tk, tq), NINF, F32)
    n_wc = NSA_WINDOW // tk + 1
    w_first = jnp.maximum(i - (n_wc - 1), 0)
    n_w = i - w_first + 1

    for p in range(D_HEADS // 2):
        qs = q_rot[p]

        def qk_s(c, j, qs=qs):
            off = pl.multiple_of(c * tk, tk)
            return _dot_nt(ks_ref[pl.ds(off, tk), :], qs[j])

        def bias_s(c, j):
            rows = sel_ref[j, jnp.minimum(c, seq_len // tk - 1)]
            tile = jnp.concatenate([jnp.broadcast_to(rows[r:r + 1], (NSA_SLC_BLOCK, tq)) for r in range(per)], axis=0)
            return tile + wb_ref[jnp.where(c == i, 0, jnp.where(c < nch, 1, n_wc))]

        o_slc = _pipe_flash(nch, 2, qk_s, lambda c, j: _with_ones(vst_ref[c, j * HEAD_DIM:(j + 1) * HEAD_DIM, :]),
                            bias_s, bufs, HEAD_DIM + DV_PAD, tk, tq)

        def qk_w(c, j, qs=qs):
            off = pl.multiple_of((w_first + c) * tk, tk)
            return _dot_nt(kw_ref[pl.ds(off, tk), :], qs[j])

        def bias_w(c, j):
            d = i - (w_first + c)
            return wb_ref[jnp.where(c < n_w, d, n_wc)]

        o_win = _pipe_flash(n_w, 2, qk_w,
                            lambda c, j: _with_ones(vwt_ref[w_first + c, j * HEAD_DIM:(j + 1) * HEAD_DIM, :]), bias_w, bufs,
                            HEAD_DIM + DV_PAD, tk, tq)
        outs = []
        for g in range(2):
            h = g * (D_HEADS // 2) + p
            outs.append(gates_t[3 * h:3 * h + 1] * o_cmp[2 * p + g]
                        + gates_t[3 * h + 1:3 * h + 2] * _normalize_t(o_slc[g], HEAD_DIM)
                        + gates_t[3 * h + 2:3 * h + 3] * _normalize_t(o_win[g], HEAD_DIM))
        o_ref[:, p * LANES:(p + 1) * LANES] = jnp.concatenate(outs, axis=0).T.astype(o_ref.dtype)


def _nsa_attention(rp, pp, vt, dg, gate_b, kcmp, vcmp, tq=TQ):
    b, s, _ = rp.shape
    nk = s // tq
    nc = s // NSA_CMP_STRIDE
    assert NSA_WINDOW == 2 * tq
    n_wc = NSA_WINDOW // tq + 1
    vct = vcmp.reshape(b, nc, 2, HEAD_DIM).transpose(0, 2, 3, 1)
    full = lambda t: pl.BlockSpec((None, s, LANES), lambda bi, i: (bi, 0, t))
    vspec = lambda t: pl.BlockSpec((None, nk, LANES, tq), lambda bi, i: (bi, 0, t, 0))
    return pl.pallas_call(
        functools.partial(_nsa_kernel, tq=tq, seq_len=s),
        grid=(b, nk),
        in_specs=[pl.BlockSpec((None, tq, 4 * LANES), lambda bi, i: (bi, i, 2)),
                  pl.BlockSpec((None, tq, 4 * LANES), lambda bi, i: (bi, i, 0)),
                  pl.BlockSpec((None, tq, LANES), lambda bi, i: (bi, i, 0)),
                  pl.BlockSpec((1, LANES), lambda bi, i: (0, 0)),
                  pl.BlockSpec((None, nc, LANES), lambda bi, i: (bi, 0, 0)),
                  pl.BlockSpec((None, 2, HEAD_DIM, nc), lambda bi, i: (bi, 0, 0, 0)),
                  full(12), vspec(4), full(13), vspec(5)],
        out_specs=pl.BlockSpec((None, tq, 4 * LANES), lambda bi, i: (bi, i, 0)),
        out_shape=jax.ShapeDtypeStruct((b, s, 4 * LANES), BF16),
        scratch_shapes=[pltpu.VMEM((2, nk, 8, tq), F32), pltpu.VMEM((n_wc + 1, tq, tq), F32)]
        + _pipe_scratch(2, tq, tq),
        compiler_params=_cparams(("parallel", "arbitrary")),
        name="nsa_attn",
    )(rp, pp, dg, gate_b, kcmp, vct, rp, vt, rp, vt)


def _cast_kernel(x_ref, o_ref):
    o_ref[...] = x_ref[...].astype(o_ref.dtype)


def _layer_to_bf16(w, layer):
    _, e, r, c = w.shape
    tr = 1 << ((2 ** 21 // c).bit_length() - 1)
    out = pl.pallas_call(
        _cast_kernel,
        grid=(e * r // tr,),
        in_specs=[pl.BlockSpec((None, tr, c), lambda i: (layer, i, 0))],
        out_specs=pl.BlockSpec((tr, c), lambda i: (i, 0)),
        out_shape=jax.ShapeDtypeStruct((e * r, c), BF16),
        compiler_params=_cparams(("parallel",)),
        name="cast_bf16",
    )(w.reshape(w.shape[0], e * r, c))
    return out.reshape(e, r, c)


def _moe_ffn_kernel(be_ref, nu_ref, x_ref, wg_ref, wu_ref, wd_ref, o_ref):
    used = pl.program_id(0) < nu_ref[0]

    @pl.when(used)
    def _():
        x = x_ref[...]
        h = jax.nn.silu(jnp.dot(x, wg_ref[...], preferred_element_type=F32)) * jnp.dot(
            x, wu_ref[...], preferred_element_type=F32)
        o_ref[...] = jnp.dot(h.astype(BF16), wd_ref[...], preferred_element_type=F32)

    @pl.when(jnp.logical_not(used))
    def _():
        o_ref[...] = jnp.zeros(o_ref.shape, o_ref.dtype)


def _moe_ffn(x_sorted, block_e, n_used, wg, wu, wd):
    ns, d = x_sorted.shape
    fdim = wg.shape[2]
    once = pl.Buffered(1)
    grid_spec = pltpu.PrefetchScalarGridSpec(
        num_scalar_prefetch=2,
        grid=(ns // MOE_TM,),
        in_specs=[pl.BlockSpec((MOE_TM, d), lambda i, be, nu: (jnp.minimum(i, nu[0] - 1), 0)),
                  pl.BlockSpec((None, d, fdim), lambda i, be, nu: (be[i], 0, 0), pipeline_mode=once),
                  pl.BlockSpec((None, d, fdim), lambda i, be, nu: (be[i], 0, 0), pipeline_mode=once),
                  pl.BlockSpec((None, fdim, d), lambda i, be, nu: (be[i], 0, 0), pipeline_mode=once)],
        out_specs=pl.BlockSpec((MOE_TM, d), lambda i, be, nu: (i, 0)),
    )
    return pl.pallas_call(
        _moe_ffn_kernel,
        grid_spec=grid_spec,
        out_shape=jax.ShapeDtypeStruct((ns, d), F32),
        compiler_params=_cparams(("arbitrary",)),
        name="moe_ffn",
    )(block_e, n_used, x_sorted, wg, wu, wd)


def _combine_ln_kernel(x_ref, y0_ref, y1_ref, rt_ref, g_ref, b_ref, xo_ref, xb_ref):
    rt = rt_ref[...]
    ffn = rt[:, 2:3] * y0_ref[...] + rt[:, 3:4] * y1_ref[...]
    y = _layer_norm(DN_ALPHA * x_ref[...] + ffn, g_ref[...], b_ref[...])
    xo_ref[...] = y
    xb_ref[...] = y.astype(BF16)


def _combine_ln(x, y0, y1, rt, g, b, tm=512):
    n, d = x.shape
    row = lambda i: (i, 0)
    fixed = lambda i: (0, 0)
    return pl.pallas_call(
        _combine_ln_kernel,
        grid=(n // tm,),
        in_specs=[pl.BlockSpec((tm, d), row), pl.BlockSpec((tm, d), row), pl.BlockSpec((tm, d), row),
                  pl.BlockSpec((tm, LANES), row), pl.BlockSpec((1, d), fixed), pl.BlockSpec((1, d), fixed)],
        out_specs=[pl.BlockSpec((tm, d), row), pl.BlockSpec((tm, d), row)],
        out_shape=[jax.ShapeDtypeStruct((n, d), F32), jax.ShapeDtypeStruct((n, d), BF16)],
        compiler_params=_cparams(("parallel",)),
        name="moe_combine_ln",
    )(x, y0, y1, rt, g.reshape(1, d), b.reshape(1, d))


def _moe_layout(rt, n):
    e_flat = rt[:, 0:TOP_K].astype(I32).reshape(-1)
    nk = n * TOP_K
    onehot = (e_flat[:, None] == jnp.arange(N_EXPERTS, dtype=I32)[None, :]).astype(I32)
    rank = jnp.take_along_axis(jnp.cumsum(onehot, axis=0), e_flat[:, None], axis=1)[:, 0] - 1
    counts = jnp.sum(onehot, axis=0)
    padded = (counts + MOE_TM - 1) // MOE_TM * MOE_TM
    pad_end = jnp.cumsum(padded)
    pad_start = pad_end - padded
    grp_start = jnp.cumsum(counts) - counts
    slot = pad_start[e_flat] + rank
    n_blocks = -(-nk // MOE_TM) + N_EXPERTS
    n_slots = n_blocks * MOE_TM
    order = jnp.argsort(e_flat, stable=True).astype(I32)
    sl = jnp.arange(n_slots, dtype=I32)
    slot_e = jnp.minimum(jnp.searchsorted(pad_end, sl, side='right'), N_EXPERTS - 1).astype(I32)
    within = sl - pad_start[slot_e]
    valid = within < counts[slot_e]
    src = jnp.where(valid, grp_start[slot_e] + within, 0)
    slot_tok = jnp.where(valid, order[src] // TOP_K, 0)
    n_used = (pad_end[-1] // MOE_TM).astype(I32).reshape(1)
    blk = jnp.arange(n_blocks, dtype=I32)
    block_e = slot_e[jnp.minimum(blk, n_used[0] - 1) * MOE_TM]
    return slot_tok, slot.reshape(n, TOP_K), block_e, n_used


def _pair_perm(n_heads):
    half = n_heads // 2
    cols = []
    for p in range(half):
        cols += list(range(p * HEAD_DIM, (p + 1) * HEAD_DIM))
        cols += list(range((half + p) * HEAD_DIM, (half + p + 1) * HEAD_DIM))
    return np.asarray(cols, dtype=np.int32)


def _pad_cols(w, width):
    return jnp.pad(w, ((0, 0), (0, width - w.shape[1])))


def _even_layer(x, xb, w_in, w_out, lam_params, subln, lam_init, wg, wu, wd, ln, tabs, bsz, seq_len):
    n, d = x.shape
    perm = _pair_perm(B_HEADS)
    aq, ak, av = w_in[:, 0:512], w_in[:, 512:1024], w_in[:, 1024:1536]
    bq, bk, bv = w_in[:, 1536:2048], w_in[:, 2048:2176], w_in[:, 2176:2304]
    iq, ik, iw = w_in[:, 2304:2560], w_in[:, 2560:2624], w_in[:, 2624:2628]
    w_rope = jnp.concatenate([aq * SCALE, ak, bq[:, perm] * SCALE, iq, bk, ik, ik], axis=1).astype(BF16)
    w_val = jnp.concatenate([av, bv], axis=1).astype(BF16)
    w_iw = _pad_cols(iw, LANES).astype(BF16)
    rp = _matmul(xb, w_rope, BF16, 512, 1024, seq_len, tabs).reshape(bsz, seq_len, -1)
    vt = _matmul_t(xb, w_val, 512, TQ).reshape(bsz, seq_len // TQ, -1, TQ)
    iwv = _matmul(xb, w_iw, F32, 512, LANES).reshape(bsz, seq_len, LANES)
    o_a = _diff_attention(rp, vt, lam_params, subln, lam_init)
    o_b = _dsa_attention(rp, vt, iwv)
    wo_a = w_out[0:512].astype(BF16)
    wo_b = w_out[512:1024][perm].astype(BF16)
    g_mix, b_mix, g_ffn, b_ffn = ln
    x1, x1b = _outproj_ln(x, o_a.reshape(n, -1), o_b.reshape(n, -1), wo_a, wo_b, g_mix, b_mix)
    return _ffn_ln(x1, x1b, wg.astype(BF16), wu.astype(BF16), wd.astype(BF16), g_ffn, b_ffn)


def _odd_layer(x, xb, w_in, w_out, gate_b, pe, phi_w1, phi_w2, w_router, b_router, wg, wu, wd, ln, tabs,
               bsz, seq_len):
    n, d = x.shape
    perm = _pair_perm(D_HEADS)
    cq, ck, cv = w_in[:, 0:512], w_in[:, 512:1024], w_in[:, 1024:1536]
    dq = w_in[:, 1536:2048][:, perm] * SCALE
    dkc, dvc, dks = w_in[:, 2048:2176], w_in[:, 2176:2304], w_in[:, 2304:2432]
    dvs, dkw, dvw = w_in[:, 2432:2560], w_in[:, 2560:2688], w_in[:, 2688:2816]
    dg = w_in[:, 2816:2840]
    w_rope = jnp.concatenate([cq * SCALE, ck, dq, dks, dkw], axis=1).astype(BF16)
    w_plain = jnp.concatenate([dq, dkc, dvc], axis=1).astype(BF16)
    w_val = jnp.concatenate([cv, dvs, dvw], axis=1).astype(BF16)
    w_dg = _pad_cols(dg, LANES).astype(BF16)
    rp = _matmul(xb, w_rope, BF16, 512, w_rope.shape[1] // 2, seq_len, tabs).reshape(bsz, seq_len, -1)
    pp = _matmul(xb, w_plain, BF16, 512, w_plain.shape[1]).reshape(bsz, seq_len, -1)
    vt = _matmul_t(xb, w_val, 512, TQ).reshape(bsz, seq_len // TQ, -1, TQ)
    dgv = _matmul(xb, w_dg, F32, 512, LANES).reshape(bsz, seq_len, LANES)

    o_c = _moba_attention(rp, vt)

    nc = seq_len // NSA_CMP_STRIDE
    tok = pp[:, :, 4 * LANES:6 * LANES].reshape(bsz, nc, NSA_CMP_STRIDE, 4, HEAD_DIM)
    r = tok.transpose(0, 3, 1, 2, 4).reshape(bsz, 4, nc, NSA_CMP_STRIDE * HEAD_DIM)
    pe_flat = jnp.pad(pe.reshape(2, 1, -1), ((0, 0), (0, 7), (0, 0))).astype(BF16)
    cmp = _nsa_compress(r, pe_flat, phi_w1.astype(BF16), phi_w2.astype(BF16))
    kcmp = jnp.concatenate([cmp[:, 0], cmp[:, 1]], axis=-1)
    vcmp = jnp.concatenate([cmp[:, 2], cmp[:, 3]], axis=-1)
    gb = _pad_cols(gate_b.reshape(1, -1), LANES)
    o_d = _nsa_attention(rp, pp, vt, dgv, gb, kcmp, vcmp)

    wo_c = w_out[0:512].astype(BF16)
    wo_d = w_out[512:1024][perm].astype(BF16)
    g_mix, b_mix, g_ffn, b_ffn = ln
    router = (_pad_cols(w_router, LANES), _pad_cols(b_router.reshape(1, -1), LANES))
    x1, x1b, rt = _outproj_ln(x, o_c.reshape(n, -1), o_d.reshape(n, -1), wo_c, wo_d, g_mix, b_mix, router)

    slot_tok, slot, block_e, n_used = _moe_layout(rt, n)
    y_slots = _moe_ffn(x1b[slot_tok], block_e, n_used, wg, wu, wd)
    return _combine_ln(x1, y_slots[slot[:, 0]], y_slots[slot[:, 1]], rt, g_ffn, b_ffn)


@jax.jit
def kernel(x, ev_w_in, ev_w_out, dif_lambda, dif_subln, ffd_w_gate, ffd_w_up, ffd_w_down, od_w_in, od_w_out,
           nsa_gate_b, nsa_pe, nsa_phi_w1, nsa_phi_w2, moe_w_router, moe_b_router, moe_w_gate, moe_w_up,
           moe_w_down, ln_mix_g, ln_mix_b, ln_ffn_g, ln_ffn_b):
    bsz, seq_len, d = x.shape
    tabs = _rope_tables(seq_len)
    xf = x.reshape(bsz * seq_len, d)
    xb = xf.astype(BF16)
    for l in range(DEPTH):
        i = l // 2
        ln = (ln_mix_g[l], ln_mix_b[l], ln_ffn_g[l], ln_ffn_b[l])
        if l % 2 == 0:
            lam_init = 0.8 - 0.6 * math.exp(-0.3 * l)
            xf, xb = _even_layer(xf, xb, ev_w_in[i], ev_w_out[i], dif_lambda[i], dif_subln[i], lam_init,
                                 ffd_w_gate[i], ffd_w_up[i], ffd_w_down[i], ln, tabs, bsz, seq_len)
        else:
            xf, xb = _odd_layer(xf, xb, od_w_in[i], od_w_out[i], nsa_gate_b[i], nsa_pe[i], nsa_phi_w1[i],
                                nsa_phi_w2[i], moe_w_router[i], moe_b_router[i], _layer_to_bf16(moe_w_gate, i),
                                _layer_to_bf16(moe_w_up, i), _layer_to_bf16(moe_w_down, i), ln, tabs, bsz, seq_len)
    return xf.reshape(bsz, seq_len, d)
```

```python
import functools
import math

import numpy as np
import jax
import jax.numpy as jnp
from jax import lax
from jax.experimental import pallas as pl
from jax.experimental.pallas import tpu as pltpu

F32 = jnp.float32
BF16 = jnp.bfloat16
I32 = jnp.int32

LANES = 128
VMEM_LIMIT = 56 * 1024 * 1024

DEPTH = 4
HEAD_DIM = 64
ROPE_THETA = 10000.0
LN_EPS = 1e-5
DN_ALPHA = (2 * DEPTH) ** 0.25
SCALE = HEAD_DIM ** -0.5 * math.log2(math.e)
NEG = -1e30
BIG = 1e30
M_INIT = -1e30
NINF = float("-inf")

A_HEADS = 4
B_HEADS = 8
IDX_HEADS = 4
DSA_TOPK = 256
C_HEADS = 8
MOBA_BLOCK = 256
MOBA_TOPK = 3
D_HEADS = 8
NSA_CMP_LEN = 32
NSA_CMP_STRIDE = 16
NSA_SLC_BLOCK = 64
NSA_SLC_TOPK = 16
NSA_WINDOW = 512
NSA_PHI_HIDDEN = 256
N_EXPERTS = 8
TOP_K = 2
MOE_TM = 512
N_BISECT = 14


def _cparams(sem):
    return pltpu.CompilerParams(dimension_semantics=sem, vmem_limit_bytes=VMEM_LIMIT)


def _dot_nt(a, b):
    return lax.dot_general(a, b, (((1,), (1,)), ((), ())), preferred_element_type=F32)


def _layer_norm(y, g, b):
    mu = jnp.mean(y, axis=-1, keepdims=True)
    yc = y - mu
    var = jnp.mean(yc * yc, axis=-1, keepdims=True)
    return yc * lax.rsqrt(var + LN_EPS) * g + b


def _safe_recip(l):
    return jnp.where(l > 0.0, 1.0 / jnp.where(l > 0.0, l, 1.0), 0.0)


def _split_halves(t, roped=False):
    lane = lax.broadcasted_iota(I32, (1, LANES), 1)
    lo = (lane // (HEAD_DIM // 2)) % 2 == 0 if roped else lane < HEAD_DIM
    z = jnp.zeros_like(t)
    return jnp.where(lo, t, z), jnp.where(lo, z, t)


def _rope_layout(n_cols):
    q = HEAD_DIM // 2
    tile = np.concatenate([np.arange(0, q), np.arange(2 * q, 3 * q), np.arange(q, 2 * q), np.arange(3 * q, 4 * q)])
    return (np.arange(0, n_cols, LANES)[:, None] + tile[None, :]).reshape(-1).astype(np.int32)


def _mm_kernel(x_ref, w_ref, *rest, rope):
    o_ref = rest[-1]
    acc = jnp.dot(x_ref[...].astype(BF16), w_ref[...], preferred_element_type=F32)
    if not rope:
        o_ref[...] = acc.astype(o_ref.dtype)
        return
    cos = rest[0][...]
    sin = rest[1][...]
    for c in range(acc.shape[1] // LANES):
        a = acc[:, c * LANES:(c + 1) * LANES]
        rot = pltpu.roll(a, LANES // 2, 1)
        o_ref[:, c * LANES:(c + 1) * LANES] = (a * cos + rot * sin).astype(o_ref.dtype)


def _matmul(x, w, out_dtype, tm, tn, seq_len=None, rope_tabs=None):
    n, d = x.shape
    p = w.shape[1]
    rope = rope_tabs is not None
    in_specs = [pl.BlockSpec((tm, d), lambda i, j: (i, 0)),
                pl.BlockSpec((d, tn), lambda i, j: (0, j))]
    args = [x, w]
    if rope:
        nt = seq_len // tm
        in_specs += [pl.BlockSpec((tm, LANES), lambda i, j: (i % nt, 0))] * 2
        args += list(rope_tabs)
    return pl.pallas_call(
        functools.partial(_mm_kernel, rope=rope),
        grid=(n // tm, p // tn),
        in_specs=in_specs,
        out_specs=pl.BlockSpec((tm, tn), lambda i, j: (i, j)),
        out_shape=jax.ShapeDtypeStruct((n, p), out_dtype),
        compiler_params=_cparams(("parallel", "arbitrary")),
        name="proj_rope" if rope else "proj",
    )(*args)


def _rope_tables(seq_len):
    d = HEAD_DIM
    inv = ROPE_THETA ** (-jnp.arange(0, d, 2, dtype=F32) / d)
    ang = jnp.arange(seq_len, dtype=I32).astype(F32)[:, None] * inv[None, :]
    cos = jnp.cos(ang)
    sin = jnp.sin(ang)
    cos128 = jnp.tile(cos, (1, LANES // (d // 2)))
    sin128 = jnp.concatenate([-sin, -sin, sin, sin], axis=1)
    return cos128, sin128


def _route_top2(x, w, b):
    logits = jnp.dot(x, w, precision=lax.Precision.HIGHEST, preferred_element_type=F32) + b
    lane = lax.broadcasted_iota(I32, (1, LANES), 1)
    lanef = lane.astype(F32)
    v = jnp.where(lane < N_EXPERTS, logits, NINF)
    l0 = jnp.max(v, axis=1, keepdims=True)
    i0 = jnp.min(jnp.where(v == l0, lanef, float(LANES)), axis=1, keepdims=True)
    v = jnp.where(lanef == i0, NINF, v)
    l1 = jnp.max(v, axis=1, keepdims=True)
    i1 = jnp.min(jnp.where(v == l1, lanef, float(LANES)), axis=1, keepdims=True)
    e1 = jnp.exp(l1 - l0)
    g0 = 1.0 / (1.0 + e1)
    g1 = e1 / (1.0 + e1)
    return jnp.where(lane == 0, i0, jnp.where(lane == 1, i1, jnp.where(lane == 2, g0, jnp.where(lane == 3, g1, 0.0))))


def _outproj_ln_kernel(x_ref, a_ref, b_ref, wa_ref, wb_ref, g_ref, bb_ref, *rest):
    mix = (jnp.dot(a_ref[...], wa_ref[...], preferred_element_type=F32)
           + jnp.dot(b_ref[...], wb_ref[...], preferred_element_type=F32))
    y = _layer_norm(DN_ALPHA * x_ref[...] + mix, g_ref[...], bb_ref[...])
    if len(rest) == 2:
        xo_ref, xb_ref = rest
    else:
        wr_ref, br_ref, xo_ref, xb_ref, rt_ref = rest
        rt_ref[...] = _route_top2(y, wr_ref[...], br_ref[...])
    xo_ref[...] = y
    xb_ref[...] = y.astype(BF16)


def _outproj_ln(x, oa, ob, wa, wb, g, b, router=None, tm=512):
    n, d = x.shape
    ka, kb = oa.shape[1], ob.shape[1]
    row = lambda i: (i, 0)
    fixed = lambda i: (0, 0)
    in_specs = [pl.BlockSpec((tm, d), row), pl.BlockSpec((tm, ka), row), pl.BlockSpec((tm, kb), row),
                pl.BlockSpec((ka, d), fixed), pl.BlockSpec((kb, d), fixed),
                pl.BlockSpec((1, d), fixed), pl.BlockSpec((1, d), fixed)]
    out_specs = [pl.BlockSpec((tm, d), row), pl.BlockSpec((tm, d), row)]
    out_shape = [jax.ShapeDtypeStruct((n, d), F32), jax.ShapeDtypeStruct((n, d), BF16)]
    args = [x, oa, ob, wa, wb, g.reshape(1, d), b.reshape(1, d)]
    if router is not None:
        in_specs += [pl.BlockSpec((d, LANES), fixed), pl.BlockSpec((1, LANES), fixed)]
        out_specs.append(pl.BlockSpec((tm, LANES), row))
        out_shape.append(jax.ShapeDtypeStruct((n, LANES), F32))
        args += list(router)
    return pl.pallas_call(
        _outproj_ln_kernel,
        grid=(n // tm,),
        in_specs=in_specs,
        out_specs=out_specs,
        out_shape=out_shape,
        compiler_params=_cparams(("parallel",)),
        name="outproj_ln",
    )(*args)


def _ffn_ln_kernel(x_ref, xb_ref, wg_ref, wu_ref, wd_ref, g_ref, b_ref, xo_ref, xob_ref):
    xb = xb_ref[...]
    h = jax.nn.silu(jnp.dot(xb, wg_ref[...], preferred_element_type=F32)) * jnp.dot(
        xb, wu_ref[...], preferred_element_type=F32)
    ffn = jnp.dot(h.astype(BF16), wd_ref[...], preferred_element_type=F32)
    y = _layer_norm(DN_ALPHA * x_ref[...] + ffn, g_ref[...], b_ref[...])
    xo_ref[...] = y
    xob_ref[...] = y.astype(BF16)


def _ffn_ln(x, xb, wg, wu, wd, g, b, tm=512):
    n, d = x.shape
    fdim = wg.shape[1]
    row = lambda i: (i, 0)
    fixed = lambda i: (0, 0)
    once = pl.Buffered(1)
    return pl.pallas_call(
        _ffn_ln_kernel,
        grid=(n // tm,),
        in_specs=[pl.BlockSpec((tm, d), row), pl.BlockSpec((tm, d), row),
                  pl.BlockSpec((d, fdim), fixed, pipeline_mode=once),
                  pl.BlockSpec((d, fdim), fixed, pipeline_mode=once),
                  pl.BlockSpec((fdim, d), fixed, pipeline_mode=once),
                  pl.BlockSpec((1, d), fixed), pl.BlockSpec((1, d), fixed)],
        out_specs=[pl.BlockSpec((tm, d), row), pl.BlockSpec((tm, d), row)],
        out_shape=[jax.ShapeDtypeStruct((n, d), F32), jax.ShapeDtypeStruct((n, d), BF16)],
        compiler_params=_cparams(("parallel",)),
        name="ffn_ln",
    )(x, xb, wg, wu, wd, g.reshape(1, d), b.reshape(1, d))


def _top_n_mask(v, n, axis):
    idx = lax.broadcasted_iota(I32, v.shape, axis).astype(F32)
    sel = jnp.zeros(v.shape, F32)
    for _ in range(n):
        mx = jnp.max(v, axis=axis, keepdims=True)
        first = jnp.min(jnp.where(v == mx, idx, float(v.shape[axis])), axis=axis, keepdims=True)
        pick = idx == first
        sel = jnp.where(pick, 1.0, sel)
        v = jnp.where(pick, NINF, v)
    return sel


DV_PAD = 16
TQ = 256


def _with_ones(vt):
    return jnp.concatenate([vt, jnp.ones((DV_PAD, vt.shape[1]), vt.dtype)], axis=0)


def _mm_t_kernel(x_ref, w_ref, o_ref, *, tk):
    acc = jnp.dot(x_ref[...].astype(BF16), w_ref[...], preferred_element_type=F32)
    for cc in range(acc.shape[0] // tk):
        o_ref[cc] = acc[cc * tk:(cc + 1) * tk, :].T.astype(o_ref.dtype)


def _matmul_t(x, w, tm, tk):
    n, d = x.shape
    p = w.shape[1]
    return pl.pallas_call(
        functools.partial(_mm_t_kernel, tk=tk),
        grid=(n // tm,),
        in_specs=[pl.BlockSpec((tm, d), lambda i: (i, 0)), pl.BlockSpec((d, p), lambda i: (0, 0))],
        out_specs=pl.BlockSpec((tm // tk, p, tk), lambda i: (i, 0, 0)),
        out_shape=jax.ShapeDtypeStruct((n // tk, p, tk), BF16),
        compiler_params=_cparams(("parallel",)),
        name="proj_t",
    )(x, w)


def _normalize_t(acc, width):
    return acc[:width] * _safe_recip(acc[width:width + 1])


def _pipe_flash(n, ns, qk, vt_at, bias_at, bufs, dv, tk, tq):
    sa, sb, pa, pb = bufs
    for j in range(ns):
        sa[j] = qk(0, j)
        pb[j] = jnp.zeros((tk, tq), BF16)

    def half(c, carry, s_cur, s_nxt, p_prev, p_cur):
        nxt = jnp.minimum(c + 1, n - 1)
        for j in range(ns):
            s_nxt[j] = qk(nxt, j)
        cp = jnp.clip(c - 1, 0, n - 1)
        out = []
        for j in range(ns):
            m, acc, alpha = carry[j]
            acc = alpha * acc + jnp.dot(vt_at(cp, j), p_prev[j], preferred_element_type=F32)
            st = s_cur[j] + bias_at(c, j)
            m_new = jnp.maximum(m, jnp.max(st, axis=0, keepdims=True))
            alpha = jnp.exp2(m - m_new)
            p_cur[j] = jnp.exp2((st - m_new).astype(BF16))
            out.append((m_new, acc, alpha))
        return tuple(out)

    def body(t, carry):
        carry = half(2 * t, carry, sa, sb, pb, pa)
        return half(2 * t + 1, carry, sb, sa, pa, pb)

    init = (jnp.full((1, tq), M_INIT, F32), jnp.zeros((dv, tq), F32), jnp.ones((1, tq), F32))
    trips = (n + 1) // 2
    carry = lax.fori_loop(0, trips, body, (init,) * ns)
    cl = jnp.minimum(2 * trips - 1, n - 1)
    outs = []
    for j in range(ns):
        _, acc, alpha = carry[j]
        outs.append(alpha * acc + jnp.dot(vt_at(cl, j), pb[j], preferred_element_type=F32))
    return outs


def _pipe_scratch(ns, tk, tq):
    return [pltpu.VMEM((ns, tk, tq), F32)] * 2 + [pltpu.VMEM((ns, tk, tq), BF16)] * 2


def _causal_t(t):
    return jnp.where(lax.broadcasted_iota(I32, (t, t), 0) <= lax.broadcasted_iota(I32, (t, t), 1), 0.0, NINF)


def _diff_kernel(lam_ref, sub_ref, q_ref, k_ref, vt_ref, o_ref, *bufs, tq, lam_init):
    i = pl.program_id(2)
    lp = lam_ref[...]
    lam = (jnp.exp(jnp.sum(lp[0:1] * lp[1:2], axis=1, keepdims=True))
           - jnp.exp(jnp.sum(lp[2:3] * lp[3:4], axis=1, keepdims=True)) + lam_init)
    qs = _split_halves(q_ref[...], roped=True)
    n = i + 1
    bufs, tab_ref = bufs[:4], bufs[4]
    tab_ref[0] = jnp.zeros((tq, tq), F32)
    tab_ref[1] = _causal_t(tq)
    tab_ref[2] = jnp.full((tq, tq), NINF, F32)

    def qk(c, j):
        off = pl.multiple_of(c * tq, tq)
        return _dot_nt(k_ref[pl.ds(off, tq), :], qs[j])

    def bias_at(c, j):
        return tab_ref[jnp.where(c == i, 1, jnp.where(c < n, 0, 2))]

    outs = _pipe_flash(n, 2, qk, lambda c, j: _with_ones(vt_ref[c]), bias_at, bufs, LANES + DV_PAD, tq, tq)
    o = _normalize_t(outs[0], LANES) - lam * _normalize_t(outs[1], LANES)
    o = o * lax.rsqrt(jnp.mean(o * o, axis=0, keepdims=True) + LN_EPS)
    o = o * sub_ref[...] * (1.0 - lam_init)
    o_ref[...] = o.T.astype(o_ref.dtype)


def _diff_attention(rp, vt, lam_params, subln, lam_init, tq=TQ):
    b, s, _ = rp.shape
    nk = s // tq
    return pl.pallas_call(
        functools.partial(_diff_kernel, tq=tq, lam_init=lam_init),
        grid=(b, A_HEADS, nk),
        in_specs=[pl.BlockSpec((4, HEAD_DIM), lambda bi, h, i: (0, 0)),
                  pl.BlockSpec((LANES, 1), lambda bi, h, i: (0, 0)),
                  pl.BlockSpec((None, tq, LANES), lambda bi, h, i: (bi, i, h)),
                  pl.BlockSpec((None, s, LANES), lambda bi, h, i: (bi, 0, A_HEADS + h)),
                  pl.BlockSpec((None, nk, LANES, tq), lambda bi, h, i: (bi, 0, h, 0))],
        out_specs=pl.BlockSpec((None, tq, LANES), lambda bi, h, i: (bi, i, h)),
        out_shape=jax.ShapeDtypeStruct((b, s, A_HEADS * LANES), BF16),
        scratch_shapes=_pipe_scratch(2, tq, tq) + [pltpu.VMEM((3, tq, tq), F32)],
        compiler_params=_cparams(("parallel", "parallel", "arbitrary")),
        name="diff_attn",
    )(lam_params, subln.reshape(LANES, 1), rp, rp, vt)


def _fold8(x, op):
    acc = x[0:8]
    for r in range(1, x.shape[0] // 8):
        acc = op(acc, x[r * 8:(r + 1) * 8])
    return acc


def _dsa_kernel(iq_ref, ikk_ref, iw_ref, q_ref, k_ref, vt_ref, o_ref, s_ref, j_ref, *bufs, tq, ksel, seq_len):
    tk = tq
    i = pl.program_id(1)
    nch = i + 1
    ksel_f = float(ksel)
    k_loc = lax.broadcasted_iota(I32, (tk, tq), 0)
    q_loc = lax.broadcasted_iota(I32, (tk, tq), 1)
    qpos = i * tq + lax.broadcasted_iota(I32, (1, tq), 1)

    iq = iq_ref[...]
    iwt = iw_ref[...].T
    iqh = []
    for pair in range(IDX_HEADS // 2):
        iqh += list(_split_halves(iq[:, pair * LANES:(pair + 1) * LANES], roped=True))

    def scores(c):
        off = pl.multiple_of(c * tk, tk)
        kk = ikk_ref[pl.ds(off, tk), :]
        sc = iwt[0:1] * jnp.maximum(_dot_nt(kk, iqh[0]), 0.0)
        for h in range(1, IDX_HEADS):
            sc = sc + iwt[h:h + 1] * jnp.maximum(_dot_nt(kk, iqh[h]), 0.0)
        return sc

    def full_body(c, carry):
        mx, mn = carry
        sc = scores(c)
        s_ref[c] = sc
        return jnp.maximum(mx, _fold8(sc, jnp.maximum)), jnp.minimum(mn, _fold8(sc, jnp.minimum))

    mx, mn = lax.fori_loop(0, i, full_body, (jnp.full((8, tq), -BIG, F32), jnp.full((8, tq), BIG, F32)))
    sc = scores(i)
    causal = k_loc <= q_loc
    s_ref[i] = jnp.where(causal, sc, NEG)
    mx = jnp.maximum(mx, _fold8(jnp.where(causal, sc, -BIG), jnp.maximum))
    mn = jnp.minimum(mn, _fold8(jnp.where(causal, sc, BIG), jnp.minimum))
    smax = jnp.max(mx, axis=0, keepdims=True)
    smin = jnp.min(mn, axis=0, keepdims=True)
    s_ref[nch] = jnp.full((tk, tq), NINF, F32)

    def count_where(ind):
        def body(c, acc):
            return acc + _fold8(ind(s_ref[c], c * tk + k_loc), jnp.add)
        acc = lax.fori_loop(0, nch, body, jnp.zeros((8, tq), F32))
        return jnp.sum(acc, axis=0, keepdims=True)

    def count_ge(th):
        return count_where(lambda x, kidx: jnp.where(x >= th, 1.0, 0.0))

    def max_below(th):
        def body(c, acc):
            x = s_ref[c]
            return jnp.maximum(acc, _fold8(jnp.where(x < th, x, NINF), jnp.maximum))
        acc = lax.fori_loop(0, nch, body, jnp.full((8, tq), NINF, F32))
        return jnp.max(acc, axis=0, keepdims=True)

    n_causal = (qpos + 1).astype(F32)
    take_all = n_causal <= ksel_f
    done0 = jnp.where(take_all | (count_ge(smax) >= ksel_f), 1.0, 0.0)

    def bisect(lo, hi):
        mid = lo + (hi - lo) * 0.5
        ge = count_ge(mid) >= ksel_f
        return jnp.where(ge, mid, lo), jnp.where(ge, hi, mid)

    lo, hi = lax.fori_loop(0, N_BISECT, lambda _, c: bisect(*c), (smin, smax))

    def snap_body(carry):
        lo, hi, th, done, _ = carry
        lo, hi = bisect(lo, hi)
        t1 = max_below(hi)
        ok = count_ge(t1) >= ksel_f
        th = jnp.where(done > 0.0, th, t1)
        hi = jnp.where(ok, hi, t1)
        done = jnp.where(ok, 1.0, done)
        return lo, hi, th, done, jnp.sum(1.0 - done)

    _, _, th, _, _ = lax.while_loop(lambda c: c[4] > 0.0, snap_body,
                                    (lo, hi, smax, done0, jnp.sum(1.0 - done0)))

    c_ge = count_ge(th)
    need_tb = jnp.where(take_all, 0.0, jnp.where(c_ge > ksel_f, 1.0, 0.0))
    j_ref[...] = jnp.full((8, tq), seq_len - 1, I32)

    @pl.when(jnp.sum(need_tb) > 0.0)
    def _():
        need = ksel_f - count_where(lambda x, kidx: jnp.where(x > th, 1.0, 0.0))

        def jb(_, carry):
            lo_j, hi_j = carry
            mid = (lo_j + hi_j) // 2
            cnt = count_where(lambda x, kidx: jnp.where(x == th, jnp.where(kidx <= mid, 1.0, 0.0), 0.0))
            ge = cnt >= need
            return jnp.where(ge, lo_j, mid), jnp.where(ge, mid, hi_j)

        n_it = int(math.ceil(math.log2(seq_len))) + 1
        _, hi_j = lax.fori_loop(0, n_it, jb, (jnp.full((1, tq), -1, I32), jnp.full((1, tq), seq_len - 1, I32)))
        j_ref[...] = jnp.broadcast_to(hi_j, (8, tq))

    jsel = j_ref[0:1, :]

    def bias_body(c, _):
        x = s_ref[c]
        kidx = c * tk + k_loc
        keep = jnp.where(x > th, 0.0, jnp.where(x == th, jnp.where(kidx <= jsel, 0.0, NINF), NINF))
        keep = jnp.where(take_all, 0.0, keep)
        s_ref[c] = jnp.where(kidx <= qpos, keep, NINF)
        return 0

    lax.fori_loop(0, nch, bias_body, 0)

    for p in range(B_HEADS // 2):
        qs = _split_halves(q_ref[:, p * LANES:(p + 1) * LANES], roped=True)

        def qk(c, j, qs=qs):
            off = pl.multiple_of(c * tk, tk)
            return _dot_nt(k_ref[pl.ds(off, tk), :], qs[j])

        outs = _pipe_flash(nch, 2, qk, lambda c, j: _with_ones(vt_ref[c, j * HEAD_DIM:(j + 1) * HEAD_DIM, :]),
                           lambda c, j: s_ref[jnp.minimum(c, nch)], bufs, HEAD_DIM + DV_PAD, tk, tq)
        o = jnp.concatenate([_normalize_t(outs[0], HEAD_DIM), _normalize_t(outs[1], HEAD_DIM)], axis=0)
        o_ref[:, p * LANES:(p + 1) * LANES] = o.T.astype(o_ref.dtype)


def _dsa_attention(rp, vt, iw, tq=TQ):
    b, s, _ = rp.shape
    ksel = min(DSA_TOPK, s // 4)
    nk = s // tq
    return pl.pallas_call(
        functools.partial(_dsa_kernel, tq=tq, ksel=ksel, seq_len=s),
        grid=(b, nk),
        in_specs=[pl.BlockSpec((None, tq, 2 * LANES), lambda bi, i: (bi, i, 6)),
                  pl.BlockSpec((None, s, LANES), lambda bi, i: (bi, 0, 15)),
                  pl.BlockSpec((None, tq, LANES), lambda bi, i: (bi, i, 0)),
                  pl.BlockSpec((None, tq, 4 * LANES), lambda bi, i: (bi, i, 2)),
                  pl.BlockSpec((None, s, LANES), lambda bi, i: (bi, 0, 14)),
                  pl.BlockSpec((None, nk, LANES, tq), lambda bi, i: (bi, 0, 4, 0))],
        out_specs=pl.BlockSpec((None, tq, 4 * LANES), lambda bi, i: (bi, i, 0)),
        out_shape=jax.ShapeDtypeStruct((b, s, 4 * LANES), BF16),
        scratch_shapes=[pltpu.VMEM((nk + 1, tq, tq), F32), pltpu.VMEM((8, tq), I32)] + _pipe_scratch(2, tq, tq),
        compiler_params=_cparams(("parallel", "arbitrary")),
        name="dsa_attn",
    )(rp, rp, iw, rp, rp, vt)


def _moba_kernel(q_ref, k_ref, vt_ref, o_ref, km_ref, sel_ref, *bufs, seq_len, n_sel):
    tq = MOBA_BLOCK
    qb = pl.program_id(2)

    @pl.when(qb == 0)
    def _():
        j = lax.broadcasted_iota(I32, (LANES, seq_len), 0)
        s = lax.broadcasted_iota(I32, (LANES, seq_len), 1)
        avg = jnp.where(s // MOBA_BLOCK == j, 1.0 / MOBA_BLOCK, 0.0).astype(BF16)
        km_ref[...] = jnp.dot(avg, k_ref[...], preferred_element_type=F32)

    nbp = sel_ref.shape[1]
    km = km_ref[0:nbp, :]
    qs = _split_halves(q_ref[...], roped=True)
    blk = lax.broadcasted_iota(I32, (nbp, tq), 0)
    past = blk < qb
    for j in range(2):
        gate = lax.dot_general(km, qs[j].astype(F32), (((1,), (1,)), ((), ())),
                               precision=lax.Precision.HIGHEST, preferred_element_type=F32)
        gate = jnp.where(blk < seq_len // MOBA_BLOCK, jnp.where(past, gate, NEG), NINF)
        sel = _top_n_mask(gate, n_sel, 0)
        sel_ref[j] = jnp.where(past, jnp.where(sel > 0.5, 0.0, NINF), NINF)
    own = _causal_t(tq)
    n = qb + 1

    def qk(c, j):
        off = pl.multiple_of(c * tq, tq)
        return _dot_nt(k_ref[pl.ds(off, tq), :], qs[j])

    def bias_at(c, j):
        chosen = sel_ref[j, pl.ds(jnp.minimum(c, nbp - 1), 1), :]
        return jnp.where(c == qb, own, jnp.where(c < n, chosen, NINF))

    outs = _pipe_flash(n, 2, qk, lambda c, j: _with_ones(vt_ref[c, j * HEAD_DIM:(j + 1) * HEAD_DIM, :]), bias_at, bufs,
                       HEAD_DIM + DV_PAD, tq, tq)
    o = jnp.concatenate([_normalize_t(outs[0], HEAD_DIM), _normalize_t(outs[1], HEAD_DIM)], axis=0)
    o_ref[...] = o.T.astype(o_ref.dtype)


def _moba_attention(rp, vt):
    b, s, _ = rp.shape
    tq = MOBA_BLOCK
    nb = s // tq
    n_sel = max(1, min(MOBA_TOPK, nb - 1))
    npair = C_HEADS // 2
    return pl.pallas_call(
        functools.partial(_moba_kernel, seq_len=s, n_sel=n_sel),
        grid=(b, npair, nb),
        in_specs=[pl.BlockSpec((None, tq, LANES), lambda bi, h, i: (bi, i, h)),
                  pl.BlockSpec((None, s, LANES), lambda bi, h, i: (bi, 0, npair + h)),
                  pl.BlockSpec((None, nb, LANES, tq), lambda bi, h, i: (bi, 0, h, 0))],
        out_specs=pl.BlockSpec((None, tq, LANES), lambda bi, h, i: (bi, i, h)),
        out_shape=jax.ShapeDtypeStruct((b, s, npair * LANES), BF16),
        scratch_shapes=[pltpu.VMEM((LANES, LANES), F32), pltpu.VMEM((2, -(-nb // 8) * 8, tq), F32)]
        + _pipe_scratch(2, tq, tq),
        compiler_params=_cparams(("parallel", "parallel", "arbitrary")),
        name="moba_attn",
    )(rp, rp, vt)


def _cmp_kernel(r_ref, pe_ref, w1_ref, w2_ref, o_ref):
    r = r_ref[...]
    w1 = w1_ref[...]
    half = r.shape[1]
    u = jnp.dot(r, w1[:half], preferred_element_type=F32)
    v = jnp.dot(r, w1[half:], preferred_element_type=F32)
    c = jnp.dot(pe_ref[...], w1, preferred_element_type=F32)[0:1]
    pre = u + pltpu.roll(v, r.shape[0] - 1, 0) + c
    o_ref[...] = jnp.dot(jax.nn.gelu(pre).astype(BF16), w2_ref[...],
                         preferred_element_type=F32).astype(o_ref.dtype)


def _nsa_compress(r, pe, w1, w2):
    b, _, nc, wdt = r.shape
    hid = w1.shape[2]
    return pl.pallas_call(
        _cmp_kernel,
        grid=(b, 4),
        in_specs=[pl.BlockSpec((None, None, nc, wdt), lambda bi, t: (bi, t, 0, 0)),
                  pl.BlockSpec((None, 8, 2 * wdt), lambda bi, t: (t // 2, 0, 0)),
                  pl.BlockSpec((None, 2 * wdt, hid), lambda bi, t: (t // 2, 0, 0)),
                  pl.BlockSpec((None, hid, HEAD_DIM), lambda bi, t: (t // 2, 0, 0))],
        out_specs=pl.BlockSpec((None, None, nc, HEAD_DIM), lambda bi, t: (bi, t, 0, 0)),
        out_shape=jax.ShapeDtypeStruct((b, 4, nc, HEAD_DIM), BF16),
        compiler_params=_cparams(("parallel", "arbitrary")),
        name="nsa_compress",
    )(r, pe, w1, w2)


def _nsa_kernel(qr_ref, qw_ref, dg_ref, gb_ref, kc_ref, vct_ref, ks_ref, vst_ref, kw_ref, vwt_ref,
                o_ref, sel_ref, wb_ref, *bufs, tq, seq_len):
    tk = tq
    i = pl.program_id(1)
    nch = i + 1
    nc = seq_len // NSA_CMP_STRIDE
    n_sb = seq_len // NSA_SLC_BLOCK
    n_sel = min(NSA_SLC_TOPK, n_sb)
    k_loc = lax.broadcasted_iota(I32, (tk, tq), 0)
    q_loc = lax.broadcasted_iota(I32, (tk, tq), 1)
    qpos = i * tq + lax.broadcasted_iota(I32, (1, tq), 1)

    gates_t = jax.nn.sigmoid(dg_ref[...] + gb_ref[...]).T
    kc = kc_ref[...]
    cmp_end = lax.broadcasted_iota(I32, (nc, 1), 0) * NSA_CMP_STRIDE + (NSA_CMP_LEN - 1)
    cbias = jnp.where(cmp_end <= qpos, 0.0, NINF)

    nh = D_HEADS // 2
    q_rot = [_split_halves(qr_ref[:, p * LANES:(p + 1) * LANES], roped=True) for p in range(nh)]
    q_raw = [_split_halves(qw_ref[:, p * LANES:(p + 1) * LANES]) for p in range(nh)]
    cbias4 = jnp.concatenate([cbias] * nh, axis=1)
    o_cmp, psum = [], []
    for g in range(2):
        qg = jnp.concatenate([q_raw[p][g] for p in range(nh)], axis=0)
        s = _dot_nt(kc, qg) + cbias4
        m = jnp.max(s, axis=0, keepdims=True)
        e = jnp.exp2(s - jnp.where(m == NINF, 0.0, m))
        pc = e * _safe_recip(jnp.sum(e, axis=0, keepdims=True))
        psum.append(sum(pc[:, p * tq:(p + 1) * tq] for p in range(nh)))
        o_cmp.append(jnp.dot(vct_ref[g], pc.astype(BF16), preferred_element_type=F32))

    per = tk // NSA_SLC_BLOCK
    nbp = -(-n_sb // 8) * 8
    cn = lax.broadcasted_iota(I32, (nbp, nc), 1) * NSA_CMP_STRIDE
    sj = lax.broadcasted_iota(I32, (nbp, nc), 0) * NSA_SLC_BLOCK
    shares = jnp.where((cn <= sj + NSA_SLC_BLOCK - 1) & (cn + NSA_CMP_LEN - 1 >= sj), 1.0, 0.0)
    blk = lax.broadcasted_iota(I32, (nbp, tq), 0)
    cur = qpos // NSA_SLC_BLOCK
    causal_b = blk <= cur
    forced = (blk == 0) | ((blk >= cur - 1) & causal_b)
    for g in range(2):
        imp = jnp.dot(shares, psum[g], precision=lax.Precision.HIGHEST, preferred_element_type=F32)
        val = jnp.where(forced, BIG, jnp.where(causal_b, imp, NEG))
        val = jnp.where(blk < n_sb, val, NINF)
        rowb = jnp.where(_top_n_mask(val, n_sel, 0) > 0.5, 0.0, NINF)
        for c in range(seq_len // tk):
            sel_ref[g, c, 0:per] = rowb[c * per:(c + 1) * per]

    wb_ref[0] = jnp.where(k_loc <= q_loc, 0.0, NINF)
    wb_ref[1] = jnp.zeros((tk, tq), F32)
    wb_ref[2] = jnp.where(k_loc > q_loc, 0.0, NINF)
    wb_ref[3] = jnp.full((tk, tq), NINF, F32)
    n_wc = NSA_WINDOW // tk + 1
    w_first = jnp.maximum(i - (n_wc - 1), 0)
    n_w = i - w_first + 1

    for p in range(D_HEADS // 2):
        qs = q_rot[p]

        def qk_s(c, j, qs=qs):
            off = pl.multiple_of(c * tk, tk)
            return _dot_nt(ks_ref[pl.ds(off, tk), :], qs[j])

        def bias_s(c, j):
            rows = sel_ref[j, jnp.minimum(c, seq_len // tk - 1)]
            tile = jnp.concatenate([jnp.broadcast_to(rows[r:r + 1], (NSA_SLC_BLOCK, tq)) for r in range(per)], axis=0)
            return tile + wb_ref[jnp.where(c == i, 0, jnp.where(c < nch, 1, n_wc))]

        o_slc = _pipe_flash(nch, 2, qk_s, lambda c, j: _with_ones(vst_ref[c, j * HEAD_DIM:(j + 1) * HEAD_DIM, :]),
                            bias_s, bufs, HEAD_DIM + DV_PAD, tk, tq)

        def qk_w(c, j, qs=qs):
            off = pl.multiple_of((w_first + c) * tk, tk)
            return _dot_nt(kw_ref[pl.ds(off, tk), :], qs[j])

        def bias_w(c, j):
            d = i - (w_first + c)
            return wb_ref[jnp.where(c < n_w, d, n_wc)]

        o_win = _pipe_flash(n_w, 2, qk_w,
                            lambda c, j: _with_ones(vwt_ref[w_first + c, j * HEAD_DIM:(j + 1) * HEAD_DIM, :]), bias_w, bufs,
                            HEAD_DIM + DV_PAD, tk, tq)
        outs = []
        for g in range(2):
            h = g * (D_HEADS // 2) + p
            outs.append(gates_t[3 * h:3 * h + 1] * o_cmp[g][:, p * tq:(p + 1) * tq]
                        + gates_t[3 * h + 1:3 * h + 2] * _normalize_t(o_slc[g], HEAD_DIM)
                        + gates_t[3 * h + 2:3 * h + 3] * _normalize_t(o_win[g], HEAD_DIM))
        o_ref[:, p * LANES:(p + 1) * LANES] = jnp.concatenate(outs, axis=0).T.astype(o_ref.dtype)


def _nsa_attention(rp, pp, vt, dg, gate_b, kcmp, vcmp, tq=TQ):
    b, s, _ = rp.shape
    nk = s // tq
    nc = s // NSA_CMP_STRIDE
    assert NSA_WINDOW == 2 * tq
    n_wc = NSA_WINDOW // tq + 1
    vct = vcmp.reshape(b, nc, 2, HEAD_DIM).transpose(0, 2, 3, 1)
    full = lambda t: pl.BlockSpec((None, s, LANES), lambda bi, i: (bi, 0, t))
    vspec = lambda t: pl.BlockSpec((None, nk, LANES, tq), lambda bi, i: (bi, 0, t, 0))
    return pl.pallas_call(
        functools.partial(_nsa_kernel, tq=tq, seq_len=s),
        grid=(b, nk),
        in_specs=[pl.BlockSpec((None, tq, 4 * LANES), lambda bi, i: (bi, i, 2)),
                  pl.BlockSpec((None, tq, 4 * LANES), lambda bi, i: (bi, i, 0)),
                  pl.BlockSpec((None, tq, LANES), lambda bi, i: (bi, i, 0)),
                  pl.BlockSpec((1, LANES), lambda bi, i: (0, 0)),
                  pl.BlockSpec((None, nc, LANES), lambda bi, i: (bi, 0, 0)),
                  pl.BlockSpec((None, 2, HEAD_DIM, nc), lambda bi, i: (bi, 0, 0, 0)),
                  full(12), vspec(4), full(13), vspec(5)],
        out_specs=pl.BlockSpec((None, tq, 4 * LANES), lambda bi, i: (bi, i, 0)),
        out_shape=jax.ShapeDtypeStruct((b, s, 4 * LANES), BF16),
        scratch_shapes=[pltpu.VMEM((2, nk, 8, tq), F32), pltpu.VMEM((n_wc + 1, tq, tq), F32)]
        + _pipe_scratch(2, tq, tq),
        compiler_params=_cparams(("parallel", "arbitrary")),
        name="nsa_attn",
    )(rp, pp, dg, gate_b, kcmp, vct, rp, vt, rp, vt)


def _cast_kernel(x_ref, o_ref):
    o_ref[...] = x_ref[...].astype(o_ref.dtype)


def _layer_to_bf16(w, layer):
    _, e, r, c = w.shape
    tr = 1 << ((2 ** 21 // c).bit_length() - 1)
    out = pl.pallas_call(
        _cast_kernel,
        grid=(e * r // tr,),
        in_specs=[pl.BlockSpec((None, tr, c), lambda i: (layer, i, 0))],
        out_specs=pl.BlockSpec((tr, c), lambda i: (i, 0)),
        out_shape=jax.ShapeDtypeStruct((e * r, c), BF16),
        compiler_params=_cparams(("parallel",)),
        name="cast_bf16",
    )(w.reshape(w.shape[0], e * r, c))
    return out.reshape(e, r, c)


def _moe_ffn_kernel(be_ref, nu_ref, x_ref, wg_ref, wu_ref, wd_ref, o_ref):
    used = pl.program_id(0) < nu_ref[0]

    @pl.when(used)
    def _():
        x = x_ref[...]
        h = jax.nn.silu(jnp.dot(x, wg_ref[...], preferred_element_type=F32)) * jnp.dot(
            x, wu_ref[...], preferred_element_type=F32)
        o_ref[...] = jnp.dot(h.astype(BF16), wd_ref[...], preferred_element_type=F32)

    @pl.when(jnp.logical_not(used))
    def _():
        o_ref[...] = jnp.zeros(o_ref.shape, o_ref.dtype)


def _moe_ffn(x_sorted, block_e, n_used, wg, wu, wd):
    ns, d = x_sorted.shape
    fdim = wg.shape[2]
    once = pl.Buffered(1)
    grid_spec = pltpu.PrefetchScalarGridSpec(
        num_scalar_prefetch=2,
        grid=(ns // MOE_TM,),
        in_specs=[pl.BlockSpec((MOE_TM, d), lambda i, be, nu: (jnp.minimum(i, nu[0] - 1), 0)),
                  pl.BlockSpec((None, d, fdim), lambda i, be, nu: (be[i], 0, 0), pipeline_mode=once),
                  pl.BlockSpec((None, d, fdim), lambda i, be, nu: (be[i], 0, 0), pipeline_mode=once),
                  pl.BlockSpec((None, fdim, d), lambda i, be, nu: (be[i], 0, 0), pipeline_mode=once)],
        out_specs=pl.BlockSpec((MOE_TM, d), lambda i, be, nu: (i, 0)),
    )
    return pl.pallas_call(
        _moe_ffn_kernel,
        grid_spec=grid_spec,
        out_shape=jax.ShapeDtypeStruct((ns, d), F32),
        compiler_params=_cparams(("arbitrary",)),
        name="moe_ffn",
    )(block_e, n_used, x_sorted, wg, wu, wd)


def _combine_ln_kernel(x_ref, y0_ref, y1_ref, rt_ref, g_ref, b_ref, xo_ref, xb_ref):
    rt = rt_ref[...]
    ffn = rt[:, 2:3] * y0_ref[...] + rt[:, 3:4] * y1_ref[...]
    y = _layer_norm(DN_ALPHA * x_ref[...] + ffn, g_ref[...], b_ref[...])
    xo_ref[...] = y
    xb_ref[...] = y.astype(BF16)


def _combine_ln(x, y0, y1, rt, g, b, tm=512):
    n, d = x.shape
    row = lambda i: (i, 0)
    fixed = lambda i: (0, 0)
    return pl.pallas_call(
        _combine_ln_kernel,
        grid=(n // tm,),
        in_specs=[pl.BlockSpec((tm, d), row), pl.BlockSpec((tm, d), row), pl.BlockSpec((tm, d), row),
                  pl.BlockSpec((tm, LANES), row), pl.BlockSpec((1, d), fixed), pl.BlockSpec((1, d), fixed)],
        out_specs=[pl.BlockSpec((tm, d), row), pl.BlockSpec((tm, d), row)],
        out_shape=[jax.ShapeDtypeStruct((n, d), F32), jax.ShapeDtypeStruct((n, d), BF16)],
        compiler_params=_cparams(("parallel",)),
        name="moe_combine_ln",
    )(x, y0, y1, rt, g.reshape(1, d), b.reshape(1, d))


def _moe_layout(rt, n):
    e_flat = rt[:, 0:TOP_K].astype(I32).reshape(-1)
    nk = n * TOP_K
    onehot = (e_flat[:, None] == jnp.arange(N_EXPERTS, dtype=I32)[None, :]).astype(I32)
    rank = jnp.take_along_axis(jnp.cumsum(onehot, axis=0), e_flat[:, None], axis=1)[:, 0] - 1
    counts = jnp.sum(onehot, axis=0)
    padded = (counts + MOE_TM - 1) // MOE_TM * MOE_TM
    pad_end = jnp.cumsum(padded)
    pad_start = pad_end - padded
    grp_start = jnp.cumsum(counts) - counts
    slot = pad_start[e_flat] + rank
    n_blocks = -(-nk // MOE_TM) + N_EXPERTS
    n_slots = n_blocks * MOE_TM
    order = jnp.argsort(e_flat, stable=True).astype(I32)
    sl = jnp.arange(n_slots, dtype=I32)
    slot_e = jnp.minimum(jnp.searchsorted(pad_end, sl, side='right'), N_EXPERTS - 1).astype(I32)
    within = sl - pad_start[slot_e]
    valid = within < counts[slot_e]
    src = jnp.where(valid, grp_start[slot_e] + within, 0)
    slot_tok = jnp.where(valid, order[src] // TOP_K, 0)
    n_used = (pad_end[-1] // MOE_TM).astype(I32).reshape(1)
    blk = jnp.arange(n_blocks, dtype=I32)
    block_e = slot_e[jnp.minimum(blk, n_used[0] - 1) * MOE_TM]
    return slot_tok, slot.reshape(n, TOP_K), block_e, n_used


def _pair_perm(n_heads):
    half = n_heads // 2
    cols = []
    for p in range(half):
        cols += list(range(p * HEAD_DIM, (p + 1) * HEAD_DIM))
        cols += list(range((half + p) * HEAD_DIM, (half + p + 1) * HEAD_DIM))
    return np.asarray(cols, dtype=np.int32)


def _pad_cols(w, width):
    return jnp.pad(w, ((0, 0), (0, width - w.shape[1])))


def _even_layer(x, xb, w_in, w_out, lam_params, subln, lam_init, wg, wu, wd, ln, tabs, bsz, seq_len):
    n, d = x.shape
    perm = _pair_perm(B_HEADS)
    aq, ak, av = w_in[:, 0:512], w_in[:, 512:1024], w_in[:, 1024:1536]
    bq, bk, bv = w_in[:, 1536:2048], w_in[:, 2048:2176], w_in[:, 2176:2304]
    iq, ik, iw = w_in[:, 2304:2560], w_in[:, 2560:2624], w_in[:, 2624:2628]
    w_rope = jnp.concatenate([aq * SCALE, ak, bq[:, perm] * SCALE, iq, bk, ik, ik], axis=1)
    w_rope = w_rope[:, _rope_layout(w_rope.shape[1])].astype(BF16)
    w_val = jnp.concatenate([av, bv], axis=1).astype(BF16)
    w_iw = _pad_cols(iw, LANES).astype(BF16)
    rp = _matmul(xb, w_rope, BF16, 512, 1024, seq_len, tabs).reshape(bsz, seq_len, -1)
    vt = _matmul_t(xb, w_val, 512, TQ).reshape(bsz, seq_len // TQ, -1, TQ)
    iwv = _matmul(xb, w_iw, F32, 512, LANES).reshape(bsz, seq_len, LANES)
    o_a = _diff_attention(rp, vt, lam_params, subln, lam_init)
    o_b = _dsa_attention(rp, vt, iwv)
    wo_a = w_out[0:512].astype(BF16)
    wo_b = w_out[512:1024][perm].astype(BF16)
    g_mix, b_mix, g_ffn, b_ffn = ln
    x1, x1b = _outproj_ln(x, o_a.reshape(n, -1), o_b.reshape(n, -1), wo_a, wo_b, g_mix, b_mix)
    return _ffn_ln(x1, x1b, wg.astype(BF16), wu.astype(BF16), wd.astype(BF16), g_ffn, b_ffn)


def _odd_layer(x, xb, w_in, w_out, gate_b, pe, phi_w1, phi_w2, w_router, b_router, wg, wu, wd, ln, tabs,
               bsz, seq_len):
    n, d = x.shape
    perm = _pair_perm(D_HEADS)
    cq, ck, cv = w_in[:, 0:512], w_in[:, 512:1024], w_in[:, 1024:1536]
    dq = w_in[:, 1536:2048][:, perm] * SCALE
    dkc, dvc, dks = w_in[:, 2048:2176], w_in[:, 2176:2304], w_in[:, 2304:2432]
    dvs, dkw, dvw = w_in[:, 2432:2560], w_in[:, 2560:2688], w_in[:, 2688:2816]
    dg = w_in[:, 2816:2840]
    w_rope = jnp.concatenate([cq * SCALE, ck, dq, dks, dkw], axis=1)
    w_rope = w_rope[:, _rope_layout(w_rope.shape[1])].astype(BF16)
    w_plain = jnp.concatenate([dq, dkc, dvc], axis=1).astype(BF16)
    w_val = jnp.concatenate([cv, dvs, dvw], axis=1).astype(BF16)
    w_dg = _pad_cols(dg, LANES).astype(BF16)
    rp = _matmul(xb, w_rope, BF16, 512, w_rope.shape[1] // 2, seq_len, tabs).reshape(bsz, seq_len, -1)
    pp = _matmul(xb, w_plain, BF16, 512, w_plain.shape[1]).reshape(bsz, seq_len, -1)
    vt = _matmul_t(xb, w_val, 512, TQ).reshape(bsz, seq_len // TQ, -1, TQ)
    dgv = _matmul(xb, w_dg, F32, 512, LANES).reshape(bsz, seq_len, LANES)

    o_c = _moba_attention(rp, vt)

    nc = seq_len // NSA_CMP_STRIDE
    tok = pp[:, :, 4 * LANES:6 * LANES].reshape(bsz, nc, NSA_CMP_STRIDE, 4, HEAD_DIM)
    r = tok.transpose(0, 3, 1, 2, 4).reshape(bsz, 4, nc, NSA_CMP_STRIDE * HEAD_DIM)
    pe_flat = jnp.pad(pe.reshape(2, 1, -1), ((0, 0), (0, 7), (0, 0))).astype(BF16)
    cmp = _nsa_compress(r, pe_flat, phi_w1.astype(BF16), phi_w2.astype(BF16))
    kcmp = jnp.concatenate([cmp[:, 0], cmp[:, 1]], axis=-1)
    vcmp = jnp.concatenate([cmp[:, 2], cmp[:, 3]], axis=-1)
    gb = _pad_cols(gate_b.reshape(1, -1), LANES)
    o_d = _nsa_attention(rp, pp, vt, dgv, gb, kcmp, vcmp)

    wo_c = w_out[0:512].astype(BF16)
    wo_d = w_out[512:1024][perm].astype(BF16)
    g_mix, b_mix, g_ffn, b_ffn = ln
    router = (_pad_cols(w_router, LANES), _pad_cols(b_router.reshape(1, -1), LANES))
    x1, x1b, rt = _outproj_ln(x, o_c.reshape(n, -1), o_d.reshape(n, -1), wo_c, wo_d, g_mix, b_mix, router)

    slot_tok, slot, block_e, n_used = _moe_layout(rt, n)
    y_slots = _moe_ffn(x1b[slot_tok], block_e, n_used, wg, wu, wd)
    return _combine_ln(x1, y_slots[slot[:, 0]], y_slots[slot[:, 1]], rt, g_ffn, b_ffn)


@jax.jit
def kernel(x, ev_w_in, ev_w_out, dif_lambda, dif_subln, ffd_w_gate, ffd_w_up, ffd_w_down, od_w_in, od_w_out,
           nsa_gate_b, nsa_pe, nsa_phi_w1, nsa_phi_w2, moe_w_router, moe_b_router, moe_w_gate, moe_w_up,
           moe_w_down, ln_mix_g, ln_mix_b, ln_ffn_g, ln_ffn_b):
    bsz, seq_len, d = x.shape
    tabs = _rope_tables(seq_len)
    xf = x.reshape(bsz * seq_len, d)
    xb = xf.astype(BF16)
    for l in range(DEPTH):
        i = l // 2
        ln = (ln_mix_g[l], ln_mix_b[l], ln_ffn_g[l], ln_ffn_b[l])
        if l % 2 == 0:
            lam_init = 0.8 - 0.6 * math.exp(-0.3 * l)
            xf, xb = _even_layer(xf, xb, ev_w_in[i], ev_w_out[i], dif_lambda[i], dif_subln[i], lam_init,
                                 ffd_w_gate[i], ffd_w_up[i], ffd_w_down[i], ln, tabs, bsz, seq_len)
        else:
            xf, xb = _odd_layer(xf, xb, od_w_in[i], od_w_out[i], nsa_gate_b[i], nsa_pe[i], nsa_phi_w1[i],
                                nsa_phi_w2[i], moe_w_router[i], moe_b_router[i], _layer_to_bf16(moe_w_gate, i),
                                _layer_to_bf16(moe_w_up, i), _layer_to_bf16(moe_w_down, i), ln, tabs, bsz, seq_len)
    return xf.reshape(bsz, seq_len, d)
```

```python
import functools
import math

import numpy as np
import jax
import jax.numpy as jnp
from jax import lax
from jax.experimental import pallas as pl
from jax.experimental.pallas import tpu as pltpu

F32 = jnp.float32
BF16 = jnp.bfloat16
I32 = jnp.int32

LANES = 128
VMEM_LIMIT = 56 * 1024 * 1024

DEPTH = 4
HEAD_DIM = 64
ROPE_THETA = 10000.0
LN_EPS = 1e-5
DN_ALPHA = (2 * DEPTH) ** 0.25
SCALE = HEAD_DIM ** -0.5 * math.log2(math.e)
NEG = -1e30
BIG = 1e30
M_INIT = -1e30
NINF = float("-inf")

A_HEADS = 4
B_HEADS = 8
IDX_HEADS = 4
DSA_TOPK = 256
C_HEADS = 8
MOBA_BLOCK = 256
MOBA_TOPK = 3
D_HEADS = 8
NSA_CMP_LEN = 32
NSA_CMP_STRIDE = 16
NSA_SLC_BLOCK = 64
NSA_SLC_TOPK = 16
NSA_WINDOW = 512
NSA_PHI_HIDDEN = 256
N_EXPERTS = 8
TOP_K = 2
MOE_TM = 512
N_BISECT = 14


def _cparams(sem):
    return pltpu.CompilerParams(dimension_semantics=sem, vmem_limit_bytes=VMEM_LIMIT)


def _dot_nt(a, b):
    return lax.dot_general(a, b, (((1,), (1,)), ((), ())), preferred_element_type=F32)


def _layer_norm(y, g, b):
    mu = jnp.mean(y, axis=-1, keepdims=True)
    yc = y - mu
    var = jnp.mean(yc * yc, axis=-1, keepdims=True)
    return yc * lax.rsqrt(var + LN_EPS) * g + b


def _safe_recip(l):
    return jnp.where(l > 0.0, 1.0 / jnp.where(l > 0.0, l, 1.0), 0.0)


def _split_halves(t, roped=False):
    lane = lax.broadcasted_iota(I32, (1, LANES), 1)
    lo = (lane // (HEAD_DIM // 2)) % 2 == 0 if roped else lane < HEAD_DIM
    z = jnp.zeros_like(t)
    return jnp.where(lo, t, z), jnp.where(lo, z, t)


def _rope_layout(n_cols):
    q = HEAD_DIM // 2
    tile = np.concatenate([np.arange(0, q), np.arange(2 * q, 3 * q), np.arange(q, 2 * q), np.arange(3 * q, 4 * q)])
    return (np.arange(0, n_cols, LANES)[:, None] + tile[None, :]).reshape(-1).astype(np.int32)


def _rope_mm_kernel(x_ref, w_ref, cos_ref, sin_ref, o_ref):
    acc = jnp.dot(x_ref[...].astype(BF16), w_ref[...], preferred_element_type=F32)
    cos = cos_ref[...]
    sin = sin_ref[...]
    for c in range(acc.shape[1] // LANES):
        a = acc[:, c * LANES:(c + 1) * LANES]
        rot = pltpu.roll(a, LANES // 2, 1)
        o_ref[:, c * LANES:(c + 1) * LANES] = (a * cos + rot * sin).astype(o_ref.dtype)


def _proj_rope(x, w, tm, tn, seq_len, rope_tabs):
    n, d = x.shape
    p = w.shape[1]
    nt = seq_len // tm
    return pl.pallas_call(
        _rope_mm_kernel,
        grid=(n // tm, p // tn),
        in_specs=[pl.BlockSpec((tm, d), lambda i, j: (i, 0)), pl.BlockSpec((d, tn), lambda i, j: (0, j)),
                  pl.BlockSpec((tm, LANES), lambda i, j: (i % nt, 0)),
                  pl.BlockSpec((tm, LANES), lambda i, j: (i % nt, 0))],
        out_specs=pl.BlockSpec((tm, tn), lambda i, j: (i, j)),
        out_shape=jax.ShapeDtypeStruct((n, p), BF16),
        compiler_params=_cparams(("parallel", "arbitrary")),
        name="proj_rope",
    )(x, w, *rope_tabs)


def _rope_tables(seq_len):
    d = HEAD_DIM
    inv = ROPE_THETA ** (-jnp.arange(0, d, 2, dtype=F32) / d)
    ang = jnp.arange(seq_len, dtype=I32).astype(F32)[:, None] * inv[None, :]
    cos = jnp.cos(ang)
    sin = jnp.sin(ang)
    cos128 = jnp.tile(cos, (1, LANES // (d // 2)))
    sin128 = jnp.concatenate([-sin, -sin, sin, sin], axis=1)
    return cos128, sin128


def _route_top2(x, w, b):
    xh, wh = x.astype(BF16), w.astype(BF16)
    xl, wl = (x - xh.astype(F32)).astype(BF16), (w - wh.astype(F32)).astype(BF16)
    logits = (jnp.dot(xh, wh, preferred_element_type=F32) + jnp.dot(xh, wl, preferred_element_type=F32)
              + jnp.dot(xl, wh, preferred_element_type=F32)) + b
    lane = lax.broadcasted_iota(I32, (1, LANES), 1)
    lanef = lane.astype(F32)
    v = jnp.where(lane < N_EXPERTS, logits, NINF)
    l0 = jnp.max(v, axis=1, keepdims=True)
    i0 = jnp.min(jnp.where(v == l0, lanef, float(LANES)), axis=1, keepdims=True)
    v = jnp.where(lanef == i0, NINF, v)
    l1 = jnp.max(v, axis=1, keepdims=True)
    i1 = jnp.min(jnp.where(v == l1, lanef, float(LANES)), axis=1, keepdims=True)
    e1 = jnp.exp(l1 - l0)
    g0 = 1.0 / (1.0 + e1)
    g1 = e1 / (1.0 + e1)
    return jnp.where(lane == 0, i0, jnp.where(lane == 1, i1, jnp.where(lane == 2, g0, jnp.where(lane == 3, g1, 0.0))))


def _outproj_ln_kernel(x_ref, a_ref, b_ref, wa_ref, wb_ref, g_ref, bb_ref, *rest):
    mix = (jnp.dot(a_ref[...], wa_ref[...], preferred_element_type=F32)
           + jnp.dot(b_ref[...], wb_ref[...], preferred_element_type=F32))
    y = _layer_norm(DN_ALPHA * x_ref[...] + mix, g_ref[...], bb_ref[...])
    if len(rest) == 2:
        xo_ref, xb_ref = rest
    else:
        wr_ref, br_ref, xo_ref, xb_ref, rt_ref = rest
        rt_ref[...] = _route_top2(y, wr_ref[...], br_ref[...])
    xo_ref[...] = y
    xb_ref[...] = y.astype(BF16)


def _outproj_ln(x, oa, ob, wa, wb, g, b, router=None, tm=512):
    n, d = x.shape
    ka, kb = oa.shape[1], ob.shape[1]
    row = lambda i: (i, 0)
    fixed = lambda i: (0, 0)
    in_specs = [pl.BlockSpec((tm, d), row), pl.BlockSpec((tm, ka), row), pl.BlockSpec((tm, kb), row),
                pl.BlockSpec((ka, d), fixed), pl.BlockSpec((kb, d), fixed),
                pl.BlockSpec((1, d), fixed), pl.BlockSpec((1, d), fixed)]
    out_specs = [pl.BlockSpec((tm, d), row), pl.BlockSpec((tm, d), row)]
    out_shape = [jax.ShapeDtypeStruct((n, d), F32), jax.ShapeDtypeStruct((n, d), BF16)]
    args = [x, oa, ob, wa, wb, g.reshape(1, d), b.reshape(1, d)]
    if router is not None:
        in_specs += [pl.BlockSpec((d, LANES), fixed), pl.BlockSpec((1, LANES), fixed)]
        out_specs.append(pl.BlockSpec((tm, LANES), row))
        out_shape.append(jax.ShapeDtypeStruct((n, LANES), F32))
        args += list(router)
    return pl.pallas_call(
        _outproj_ln_kernel,
        grid=(n // tm,),
        in_specs=in_specs,
        out_specs=out_specs,
        out_shape=out_shape,
        compiler_params=_cparams(("parallel",)),
        name="outproj_ln",
    )(*args)


def _ffn_ln_kernel(x_ref, xb_ref, wg_ref, wu_ref, wd_ref, g_ref, b_ref, xo_ref, xob_ref):
    xb = xb_ref[...]
    h = jax.nn.silu(jnp.dot(xb, wg_ref[...], preferred_element_type=F32)) * jnp.dot(
        xb, wu_ref[...], preferred_element_type=F32)
    ffn = jnp.dot(h.astype(BF16), wd_ref[...], preferred_element_type=F32)
    y = _layer_norm(DN_ALPHA * x_ref[...] + ffn, g_ref[...], b_ref[...])
    xo_ref[...] = y
    xob_ref[...] = y.astype(BF16)


def _ffn_ln(x, xb, wg, wu, wd, g, b, tm=512):
    n, d = x.shape
    fdim = wg.shape[1]
    row = lambda i: (i, 0)
    fixed = lambda i: (0, 0)
    once = pl.Buffered(1)
    return pl.pallas_call(
        _ffn_ln_kernel,
        grid=(n // tm,),
        in_specs=[pl.BlockSpec((tm, d), row), pl.BlockSpec((tm, d), row),
                  pl.BlockSpec((d, fdim), fixed, pipeline_mode=once),
                  pl.BlockSpec((d, fdim), fixed, pipeline_mode=once),
                  pl.BlockSpec((fdim, d), fixed, pipeline_mode=once),
                  pl.BlockSpec((1, d), fixed), pl.BlockSpec((1, d), fixed)],
        out_specs=[pl.BlockSpec((tm, d), row), pl.BlockSpec((tm, d), row)],
        out_shape=[jax.ShapeDtypeStruct((n, d), F32), jax.ShapeDtypeStruct((n, d), BF16)],
        compiler_params=_cparams(("parallel",)),
        name="ffn_ln",
    )(x, xb, wg, wu, wd, g.reshape(1, d), b.reshape(1, d))


def _top_n_mask(v, n, axis):
    idx = lax.broadcasted_iota(I32, v.shape, axis).astype(F32)
    sel = jnp.zeros(v.shape, F32)
    for _ in range(n):
        mx = jnp.max(v, axis=axis, keepdims=True)
        first = jnp.min(jnp.where(v == mx, idx, float(v.shape[axis])), axis=axis, keepdims=True)
        pick = idx == first
        sel = jnp.where(pick, 1.0, sel)
        v = jnp.where(pick, NINF, v)
    return sel


DV_PAD = 16
TQ = 256


def _with_ones(vt):
    return jnp.concatenate([vt, jnp.ones((DV_PAD, vt.shape[1]), vt.dtype)], axis=0)


def _mm_t_kernel(x_ref, wv_ref, ws_ref, *rest, tk):
    x = x_ref[...].astype(BF16)
    vt_ref, small_ref = rest[-2:] if len(rest) == 2 else rest[1:3]
    acc = jnp.dot(x, wv_ref[...], preferred_element_type=F32)
    for cc in range(acc.shape[0] // tk):
        vt_ref[cc] = acc[cc * tk:(cc + 1) * tk, :].T.astype(vt_ref.dtype)
    small_ref[...] = jnp.dot(x, ws_ref[...], preferred_element_type=F32)
    if len(rest) == 4:
        rest[3][...] = jnp.dot(x, rest[0][...], preferred_element_type=F32).astype(rest[3].dtype)


def _proj_values(x, w_val, w_small, w_plain, tm, tk):
    n, d = x.shape
    pv, ps = w_val.shape[1], w_small.shape[1]
    row = lambda i: (i, 0)
    fixed = lambda i: (0, 0)
    in_specs = [pl.BlockSpec((tm, d), row), pl.BlockSpec((d, pv), fixed), pl.BlockSpec((d, ps), fixed)]
    out_specs = [pl.BlockSpec((tm // tk, pv, tk), lambda i: (i, 0, 0)), pl.BlockSpec((tm, ps), row)]
    out_shape = [jax.ShapeDtypeStruct((n // tk, pv, tk), BF16), jax.ShapeDtypeStruct((n, ps), F32)]
    args = [x, w_val, w_small]
    if w_plain is not None:
        pp = w_plain.shape[1]
        in_specs.append(pl.BlockSpec((d, pp), fixed))
        out_specs.append(pl.BlockSpec((tm, pp), row))
        out_shape.append(jax.ShapeDtypeStruct((n, pp), BF16))
        args.append(w_plain)
    return pl.pallas_call(
        functools.partial(_mm_t_kernel, tk=tk),
        grid=(n // tm,),
        in_specs=in_specs,
        out_specs=out_specs,
        out_shape=out_shape,
        compiler_params=_cparams(("parallel",)),
        name="proj_t",
    )(*args)


def _normalize_t(acc, width):
    return acc[:width] * _safe_recip(acc[width:width + 1])


def _pipe_flash(n, ns, qk, vt_at, bias_at, bufs, dv, tk, tq):
    sa, sb, pa, pb = bufs
    for j in range(ns):
        sa[j] = qk(0, j)
        pb[j] = jnp.zeros((tk, tq), BF16)

    def half(c, carry, s_cur, s_nxt, p_prev, p_cur):
        nxt = jnp.minimum(c + 1, n - 1)
        for j in range(ns):
            s_nxt[j] = qk(nxt, j)
        cp = jnp.clip(c - 1, 0, n - 1)
        out = []
        for j in range(ns):
            m, acc, alpha = carry[j]
            acc = alpha * acc + jnp.dot(vt_at(cp, j), p_prev[j], preferred_element_type=F32)
            st = s_cur[j] + bias_at(c, j)
            m_new = jnp.maximum(m, jnp.max(st, axis=0, keepdims=True))
            alpha = jnp.exp2(m - m_new)
            p_cur[j] = jnp.exp2((st - m_new).astype(BF16))
            out.append((m_new, acc, alpha))
        return tuple(out)

    def body(t, carry):
        carry = half(2 * t, carry, sa, sb, pb, pa)
        return half(2 * t + 1, carry, sb, sa, pa, pb)

    init = (jnp.full((1, tq), M_INIT, F32), jnp.zeros((dv, tq), F32), jnp.ones((1, tq), F32))
    trips = (n + 1) // 2
    carry = lax.fori_loop(0, trips, body, (init,) * ns)
    cl = jnp.minimum(2 * trips - 1, n - 1)
    outs = []
    for j in range(ns):
        _, acc, alpha = carry[j]
        outs.append(alpha * acc + jnp.dot(vt_at(cl, j), pb[j], preferred_element_type=F32))
    return outs


def _pipe_scratch(ns, tk, tq):
    return [pltpu.VMEM((ns, tk, tq), F32)] * 2 + [pltpu.VMEM((ns, tk, tq), BF16)] * 2


def _causal_t(t):
    return jnp.where(lax.broadcasted_iota(I32, (t, t), 0) <= lax.broadcasted_iota(I32, (t, t), 1), 0.0, NINF)


def _diff_kernel(lam_ref, sub_ref, q_ref, k_ref, vt_ref, o_ref, *bufs, tq, lam_init):
    i = pl.program_id(2)
    lp = lam_ref[...]
    lam = (jnp.exp(jnp.sum(lp[0:1] * lp[1:2], axis=1, keepdims=True))
           - jnp.exp(jnp.sum(lp[2:3] * lp[3:4], axis=1, keepdims=True)) + lam_init)
    qs = _split_halves(q_ref[...], roped=True)
    n = i + 1
    bufs, tab_ref = bufs[:4], bufs[4]
    tab_ref[0] = jnp.zeros((tq, tq), F32)
    tab_ref[1] = _causal_t(tq)
    tab_ref[2] = jnp.full((tq, tq), NINF, F32)

    def qk(c, j):
        off = pl.multiple_of(c * tq, tq)
        return _dot_nt(k_ref[pl.ds(off, tq), :], qs[j])

    def bias_at(c, j):
        return tab_ref[jnp.where(c == i, 1, jnp.where(c < n, 0, 2))]

    outs = _pipe_flash(n, 2, qk, lambda c, j: _with_ones(vt_ref[c]), bias_at, bufs, LANES + DV_PAD, tq, tq)
    o = _normalize_t(outs[0], LANES) - lam * _normalize_t(outs[1], LANES)
    o = o * lax.rsqrt(jnp.mean(o * o, axis=0, keepdims=True) + LN_EPS)
    o = o * sub_ref[...] * (1.0 - lam_init)
    o_ref[...] = o.T.astype(o_ref.dtype)


def _diff_attention(rp, vt, lam_params, subln, lam_init, tq=TQ):
    b, s, _ = rp.shape
    nk = s // tq
    return pl.pallas_call(
        functools.partial(_diff_kernel, tq=tq, lam_init=lam_init),
        grid=(b, A_HEADS, nk),
        in_specs=[pl.BlockSpec((4, HEAD_DIM), lambda bi, h, i: (0, 0)),
                  pl.BlockSpec((LANES, 1), lambda bi, h, i: (0, 0)),
                  pl.BlockSpec((None, tq, LANES), lambda bi, h, i: (bi, i, h)),
                  pl.BlockSpec((None, s, LANES), lambda bi, h, i: (bi, 0, A_HEADS + h)),
                  pl.BlockSpec((None, nk, LANES, tq), lambda bi, h, i: (bi, 0, h, 0))],
        out_specs=pl.BlockSpec((None, tq, LANES), lambda bi, h, i: (bi, i, h)),
        out_shape=jax.ShapeDtypeStruct((b, s, A_HEADS * LANES), BF16),
        scratch_shapes=_pipe_scratch(2, tq, tq) + [pltpu.VMEM((3, tq, tq), F32)],
        compiler_params=_cparams(("parallel", "parallel", "arbitrary")),
        name="diff_attn",
    )(lam_params, subln.reshape(LANES, 1), rp, rp, vt)


def _fold8(x, op):
    acc = x[0:8]
    for r in range(1, x.shape[0] // 8):
        acc = op(acc, x[r * 8:(r + 1) * 8])
    return acc


def _dsa_kernel(iq_ref, ikk_ref, iw_ref, q_ref, k_ref, vt_ref, o_ref, s_ref, j_ref, *bufs, tq, ksel, seq_len):
    tk = tq
    i = pl.program_id(1)
    nch = i + 1
    ksel_f = float(ksel)
    k_loc = lax.broadcasted_iota(I32, (tk, tq), 0)
    q_loc = lax.broadcasted_iota(I32, (tk, tq), 1)
    qpos = i * tq + lax.broadcasted_iota(I32, (1, tq), 1)

    iq = iq_ref[...]
    iwt = iw_ref[...].T
    iqh = []
    for pair in range(IDX_HEADS // 2):
        iqh += list(_split_halves(iq[:, pair * LANES:(pair + 1) * LANES], roped=True))

    def scores(c):
        off = pl.multiple_of(c * tk, tk)
        kk = ikk_ref[pl.ds(off, tk), :]
        sc = iwt[0:1] * jnp.maximum(_dot_nt(kk, iqh[0]), 0.0)
        for h in range(1, IDX_HEADS):
            sc = sc + iwt[h:h + 1] * jnp.maximum(_dot_nt(kk, iqh[h]), 0.0)
        return sc

    def full_body(c, carry):
        mx, mn = carry
        sc = scores(c)
        s_ref[c] = sc
        return jnp.maximum(mx, _fold8(sc, jnp.maximum)), jnp.minimum(mn, _fold8(sc, jnp.minimum))

    mx, mn = lax.fori_loop(0, i, full_body, (jnp.full((8, tq), -BIG, F32), jnp.full((8, tq), BIG, F32)))
    sc = scores(i)
    causal = k_loc <= q_loc
    s_ref[i] = jnp.where(causal, sc, NEG)
    mx = jnp.maximum(mx, _fold8(jnp.where(causal, sc, -BIG), jnp.maximum))
    mn = jnp.minimum(mn, _fold8(jnp.where(causal, sc, BIG), jnp.minimum))
    smax = jnp.max(mx, axis=0, keepdims=True)
    smin = jnp.min(mn, axis=0, keepdims=True)
    s_ref[nch] = jnp.full((tk, tq), NINF, F32)

    def count_where(ind):
        def body(c, acc):
            return acc + _fold8(ind(s_ref[c], c * tk + k_loc), jnp.add)
        acc = lax.fori_loop(0, nch, body, jnp.zeros((8, tq), F32))
        return jnp.sum(acc, axis=0, keepdims=True)

    def count_ge(th):
        return count_where(lambda x, kidx: jnp.where(x >= th, 1.0, 0.0))

    def max_below(th):
        def body(c, acc):
            x = s_ref[c]
            return jnp.maximum(acc, _fold8(jnp.where(x < th, x, NINF), jnp.maximum))
        acc = lax.fori_loop(0, nch, body, jnp.full((8, tq), NINF, F32))
        return jnp.max(acc, axis=0, keepdims=True)

    n_causal = (qpos + 1).astype(F32)
    take_all = n_causal <= ksel_f
    done0 = jnp.where(take_all | (count_ge(smax) >= ksel_f), 1.0, 0.0)

    def bisect(lo, hi):
        mid = lo + (hi - lo) * 0.5
        ge = count_ge(mid) >= ksel_f
        return jnp.where(ge, mid, lo), jnp.where(ge, hi, mid)

    lo, hi = lax.fori_loop(0, N_BISECT, lambda _, c: bisect(*c), (smin, smax))

    def snap_body(carry):
        lo, hi, th, done, _ = carry
        lo, hi = bisect(lo, hi)
        t1 = max_below(hi)
        ok = count_ge(t1) >= ksel_f
        th = jnp.where(done > 0.0, th, t1)
        hi = jnp.where(ok, hi, t1)
        done = jnp.where(ok, 1.0, done)
        return lo, hi, th, done, jnp.sum(1.0 - done)

    _, _, th, _, _ = lax.while_loop(lambda c: c[4] > 0.0, snap_body,
                                    (lo, hi, smax, done0, jnp.sum(1.0 - done0)))

    c_ge = count_ge(th)
    need_tb = jnp.where(take_all, 0.0, jnp.where(c_ge > ksel_f, 1.0, 0.0))
    j_ref[...] = jnp.full((8, tq), seq_len - 1, I32)

    @pl.when(jnp.sum(need_tb) > 0.0)
    def _():
        need = ksel_f - count_where(lambda x, kidx: jnp.where(x > th, 1.0, 0.0))

        def jb(_, carry):
            lo_j, hi_j = carry
            mid = (lo_j + hi_j) // 2
            cnt = count_where(lambda x, kidx: jnp.where(x == th, jnp.where(kidx <= mid, 1.0, 0.0), 0.0))
            ge = cnt >= need
            return jnp.where(ge, lo_j, mid), jnp.where(ge, mid, hi_j)

        n_it = int(math.ceil(math.log2(seq_len))) + 1
        _, hi_j = lax.fori_loop(0, n_it, jb, (jnp.full((1, tq), -1, I32), jnp.full((1, tq), seq_len - 1, I32)))
        j_ref[...] = jnp.broadcast_to(hi_j, (8, tq))

    jsel = j_ref[0:1, :]

    def bias_body(c, _):
        x = s_ref[c]
        kidx = c * tk + k_loc
        keep = jnp.where(x > th, 0.0, jnp.where(x == th, jnp.where(kidx <= jsel, 0.0, NINF), NINF))
        keep = jnp.where(take_all, 0.0, keep)
        s_ref[c] = jnp.where(kidx <= qpos, keep, NINF)
        return 0

    lax.fori_loop(0, nch, bias_body, 0)

    for p in range(B_HEADS // 2):
        qs = _split_halves(q_ref[:, p * LANES:(p + 1) * LANES], roped=True)

        def qk(c, j, qs=qs):
            off = pl.multiple_of(c * tk, tk)
            return _dot_nt(k_ref[pl.ds(off, tk), :], qs[j])

        outs = _pipe_flash(nch, 2, qk, lambda c, j: _with_ones(vt_ref[c, j * HEAD_DIM:(j + 1) * HEAD_DIM, :]),
                           lambda c, j: s_ref[jnp.minimum(c, nch)], bufs, HEAD_DIM + DV_PAD, tk, tq)
        o = jnp.concatenate([_normalize_t(outs[0], HEAD_DIM), _normalize_t(outs[1], HEAD_DIM)], axis=0)
        o_ref[:, p * LANES:(p + 1) * LANES] = o.T.astype(o_ref.dtype)


def _dsa_attention(rp, vt, iw, tq=TQ):
    b, s, _ = rp.shape
    ksel = min(DSA_TOPK, s // 4)
    nk = s // tq
    return pl.pallas_call(
        functools.partial(_dsa_kernel, tq=tq, ksel=ksel, seq_len=s),
        grid=(b, nk),
        in_specs=[pl.BlockSpec((None, tq, 2 * LANES), lambda bi, i: (bi, i, 6)),
                  pl.BlockSpec((None, s, LANES), lambda bi, i: (bi, 0, 15)),
                  pl.BlockSpec((None, tq, LANES), lambda bi, i: (bi, i, 0)),
                  pl.BlockSpec((None, tq, 4 * LANES), lambda bi, i: (bi, i, 2)),
                  pl.BlockSpec((None, s, LANES), lambda bi, i: (bi, 0, 14)),
                  pl.BlockSpec((None, nk, LANES, tq), lambda bi, i: (bi, 0, 4, 0))],
        out_specs=pl.BlockSpec((None, tq, 4 * LANES), lambda bi, i: (bi, i, 0)),
        out_shape=jax.ShapeDtypeStruct((b, s, 4 * LANES), BF16),
        scratch_shapes=[pltpu.VMEM((nk + 1, tq, tq), F32), pltpu.VMEM((8, tq), I32)] + _pipe_scratch(2, tq, tq),
        compiler_params=_cparams(("parallel", "arbitrary")),
        name="dsa_attn",
    )(rp, rp, iw, rp, rp, vt)


def _moba_kernel(q_ref, k_ref, vt_ref, o_ref, km_ref, sel_ref, *bufs, seq_len, n_sel):
    tq = MOBA_BLOCK
    qb = pl.program_id(2)

    @pl.when(qb == 0)
    def _():
        j = lax.broadcasted_iota(I32, (LANES, seq_len), 0)
        s = lax.broadcasted_iota(I32, (LANES, seq_len), 1)
        avg = jnp.where(s // MOBA_BLOCK == j, 1.0 / MOBA_BLOCK, 0.0).astype(BF16)
        km_ref[...] = jnp.dot(avg, k_ref[...], preferred_element_type=F32)

    nbp = sel_ref.shape[1]
    km = km_ref[0:nbp, :]
    qs = _split_halves(q_ref[...], roped=True)
    blk = lax.broadcasted_iota(I32, (nbp, tq), 0)
    past = blk < qb
    for j in range(2):
        gate = lax.dot_general(km, qs[j].astype(F32), (((1,), (1,)), ((), ())),
                               precision=lax.Precision.HIGHEST, preferred_element_type=F32)
        gate = jnp.where(blk < seq_len // MOBA_BLOCK, jnp.where(past, gate, NEG), NINF)
        sel = _top_n_mask(gate, n_sel, 0)
        sel_ref[j] = jnp.where(past, jnp.where(sel > 0.5, 0.0, NINF), NINF)
    own = _causal_t(tq)
    n = qb + 1

    def qk(c, j):
        off = pl.multiple_of(c * tq, tq)
        return _dot_nt(k_ref[pl.ds(off, tq), :], qs[j])

    def bias_at(c, j):
        chosen = sel_ref[j, pl.ds(jnp.minimum(c, nbp - 1), 1), :]
        return jnp.where(c == qb, own, jnp.where(c < n, chosen, NINF))

    outs = _pipe_flash(n, 2, qk, lambda c, j: _with_ones(vt_ref[c, j * HEAD_DIM:(j + 1) * HEAD_DIM, :]), bias_at, bufs,
                       HEAD_DIM + DV_PAD, tq, tq)
    o = jnp.concatenate([_normalize_t(outs[0], HEAD_DIM), _normalize_t(outs[1], HEAD_DIM)], axis=0)
    o_ref[...] = o.T.astype(o_ref.dtype)


def _moba_attention(rp, vt):
    b, s, _ = rp.shape
    tq = MOBA_BLOCK
    nb = s // tq
    n_sel = max(1, min(MOBA_TOPK, nb - 1))
    npair = C_HEADS // 2
    return pl.pallas_call(
        functools.partial(_moba_kernel, seq_len=s, n_sel=n_sel),
        grid=(b, npair, nb),
        in_specs=[pl.BlockSpec((None, tq, LANES), lambda bi, h, i: (bi, i, h)),
                  pl.BlockSpec((None, s, LANES), lambda bi, h, i: (bi, 0, npair + h)),
                  pl.BlockSpec((None, nb, LANES, tq), lambda bi, h, i: (bi, 0, h, 0))],
        out_specs=pl.BlockSpec((None, tq, LANES), lambda bi, h, i: (bi, i, h)),
        out_shape=jax.ShapeDtypeStruct((b, s, npair * LANES), BF16),
        scratch_shapes=[pltpu.VMEM((LANES, LANES), F32), pltpu.VMEM((2, -(-nb // 8) * 8, tq), F32)]
        + _pipe_scratch(2, tq, tq),
        compiler_params=_cparams(("parallel", "parallel", "arbitrary")),
        name="moba_attn",
    )(rp, rp, vt)


def _cmp_kernel(r_ref, pe_ref, w1_ref, w2_ref, o_ref):
    r = r_ref[...]
    w1 = w1_ref[...]
    half = r.shape[1]
    u = jnp.dot(r, w1[:half], preferred_element_type=F32)
    v = jnp.dot(r, w1[half:], preferred_element_type=F32)
    c = jnp.dot(pe_ref[...], w1, preferred_element_type=F32)[0:1]
    pre = u + pltpu.roll(v, r.shape[0] - 1, 0) + c
    o_ref[...] = jnp.dot(jax.nn.gelu(pre).astype(BF16), w2_ref[...],
                         preferred_element_type=F32).astype(o_ref.dtype)


def _nsa_compress(r, pe, w1, w2):
    b, _, nc, wdt = r.shape
    hid = w1.shape[2]
    return pl.pallas_call(
        _cmp_kernel,
        grid=(b, 4),
        in_specs=[pl.BlockSpec((None, None, nc, wdt), lambda bi, t: (bi, t, 0, 0)),
                  pl.BlockSpec((None, 8, 2 * wdt), lambda bi, t: (t // 2, 0, 0)),
                  pl.BlockSpec((None, 2 * wdt, hid), lambda bi, t: (t // 2, 0, 0)),
                  pl.BlockSpec((None, hid, HEAD_DIM), lambda bi, t: (t // 2, 0, 0))],
        out_specs=pl.BlockSpec((None, None, nc, HEAD_DIM), lambda bi, t: (bi, t, 0, 0)),
        out_shape=jax.ShapeDtypeStruct((b, 4, nc, HEAD_DIM), BF16),
        compiler_params=_cparams(("parallel", "arbitrary")),
        name="nsa_compress",
    )(r, pe, w1, w2)


def _nsa_kernel(qr_ref, qw_ref, dg_ref, gb_ref, kc_ref, vct_ref, ks_ref, vst_ref, kw_ref, vwt_ref,
                o_ref, sel_ref, wb_ref, *bufs, tq, seq_len):
    tk = tq
    i = pl.program_id(1)
    nch = i + 1
    nc = seq_len // NSA_CMP_STRIDE
    n_sb = seq_len // NSA_SLC_BLOCK
    n_sel = min(NSA_SLC_TOPK, n_sb)
    k_loc = lax.broadcasted_iota(I32, (tk, tq), 0)
    q_loc = lax.broadcasted_iota(I32, (tk, tq), 1)
    qpos = i * tq + lax.broadcasted_iota(I32, (1, tq), 1)

    gates_t = jax.nn.sigmoid(dg_ref[...] + gb_ref[...]).T
    kc = kc_ref[...]
    cmp_end = lax.broadcasted_iota(I32, (nc, 1), 0) * NSA_CMP_STRIDE + (NSA_CMP_LEN - 1)
    cbias = jnp.where(cmp_end <= qpos, 0.0, NINF)

    nh = D_HEADS // 2
    q_rot = [_split_halves(qr_ref[:, p * LANES:(p + 1) * LANES], roped=True) for p in range(nh)]
    q_raw = [_split_halves(qw_ref[:, p * LANES:(p + 1) * LANES]) for p in range(nh)]
    cbias4 = jnp.concatenate([cbias] * nh, axis=1)
    o_cmp, psum = [], []
    for g in range(2):
        qg = jnp.concatenate([q_raw[p][g] for p in range(nh)], axis=0)
        s = _dot_nt(kc, qg) + cbias4
        m = jnp.max(s, axis=0, keepdims=True)
        e = jnp.exp2(s - jnp.where(m == NINF, 0.0, m))
        pc = e * _safe_recip(jnp.sum(e, axis=0, keepdims=True))
        psum.append(sum(pc[:, p * tq:(p + 1) * tq] for p in range(nh)))
        o_cmp.append(jnp.dot(vct_ref[g], pc.astype(BF16), preferred_element_type=F32))

    per = tk // NSA_SLC_BLOCK
    nbp = -(-n_sb // 8) * 8
    cn = lax.broadcasted_iota(I32, (nbp, nc), 1) * NSA_CMP_STRIDE
    sj = lax.broadcasted_iota(I32, (nbp, nc), 0) * NSA_SLC_BLOCK
    shares = jnp.where((cn <= sj + NSA_SLC_BLOCK - 1) & (cn + NSA_CMP_LEN - 1 >= sj), 1.0, 0.0)
    blk = lax.broadcasted_iota(I32, (nbp, tq), 0)
    cur = qpos // NSA_SLC_BLOCK
    causal_b = blk <= cur
    forced = (blk == 0) | ((blk >= cur - 1) & causal_b)
    for g in range(2):
        imp = jnp.dot(shares, psum[g], precision=lax.Precision.HIGHEST, preferred_element_type=F32)
        val = jnp.where(forced, BIG, jnp.where(causal_b, imp, NEG))
        val = jnp.where(blk < n_sb, val, NINF)
        rowb = jnp.where(_top_n_mask(val, n_sel, 0) > 0.5, 0.0, NINF)
        for c in range(seq_len // tk):
            sel_ref[g, c] = jnp.concatenate([rowb[c * per:(c + 1) * per], jnp.zeros((8 - per, tq), F32)], axis=0)

    wb_ref[0] = jnp.where(k_loc <= q_loc, 0.0, NINF)
    wb_ref[1] = jnp.zeros((tk, tq), F32)
    wb_ref[2] = jnp.where(k_loc > q_loc, 0.0, NINF)
    wb_ref[3] = jnp.full((tk, tq), NINF, F32)
    n_wc = NSA_WINDOW // tk + 1
    w_first = jnp.maximum(i - (n_wc - 1), 0)
    n_w = i - w_first + 1

    for p in range(D_HEADS // 2):
        qs = q_rot[p]

        def qk_s(c, j, qs=qs):
            off = pl.multiple_of(c * tk, tk)
            return _dot_nt(ks_ref[pl.ds(off, tk), :], qs[j])

        def bias_s(c, j):
            rows = sel_ref[j, jnp.minimum(c, seq_len // tk - 1)]
            tile = jnp.concatenate([jnp.broadcast_to(rows[r:r + 1], (NSA_SLC_BLOCK, tq)) for r in range(per)], axis=0)
            return tile + wb_ref[jnp.where(c == i, 0, jnp.where(c < nch, 1, n_wc))]

        o_slc = _pipe_flash(nch, 2, qk_s, lambda c, j: _with_ones(vst_ref[c, j * HEAD_DIM:(j + 1) * HEAD_DIM, :]),
                            bias_s, bufs, HEAD_DIM + DV_PAD, tk, tq)

        def qk_w(c, j, qs=qs):
            off = pl.multiple_of((w_first + c) * tk, tk)
            return _dot_nt(kw_ref[pl.ds(off, tk), :], qs[j])

        def bias_w(c, j):
            d = i - (w_first + c)
            return wb_ref[jnp.where(c < n_w, d, n_wc)]

        o_win = _pipe_flash(n_w, 2, qk_w,
                            lambda c, j: _with_ones(vwt_ref[w_first + c, j * HEAD_DIM:(j + 1) * HEAD_DIM, :]), bias_w, bufs,
                            HEAD_DIM + DV_PAD, tk, tq)
        outs = []
        for g in range(2):
            h = g * (D_HEADS // 2) + p
            outs.append(gates_t[3 * h:3 * h + 1] * o_cmp[g][:, p * tq:(p + 1) * tq]
                        + gates_t[3 * h + 1:3 * h + 2] * _normalize_t(o_slc[g], HEAD_DIM)
                        + gates_t[3 * h + 2:3 * h + 3] * _normalize_t(o_win[g], HEAD_DIM))
        o_ref[:, p * LANES:(p + 1) * LANES] = jnp.concatenate(outs, axis=0).T.astype(o_ref.dtype)


def _nsa_attention(rp, pp, vt, dg, gate_b, kcmp, vcmp, tq=TQ):
    b, s, _ = rp.shape
    nk = s // tq
    nc = s // NSA_CMP_STRIDE
    assert NSA_WINDOW == 2 * tq
    n_wc = NSA_WINDOW // tq + 1
    vct = vcmp.reshape(b, nc, 2, HEAD_DIM).transpose(0, 2, 3, 1)
    full = lambda t: pl.BlockSpec((None, s, LANES), lambda bi, i: (bi, 0, t))
    vspec = lambda t: pl.BlockSpec((None, nk, LANES, tq), lambda bi, i: (bi, 0, t, 0))
    return pl.pallas_call(
        functools.partial(_nsa_kernel, tq=tq, seq_len=s),
        grid=(b, nk),
        in_specs=[pl.BlockSpec((None, tq, 4 * LANES), lambda bi, i: (bi, i, 2)),
                  pl.BlockSpec((None, tq, 4 * LANES), lambda bi, i: (bi, i, 0)),
                  pl.BlockSpec((None, tq, LANES), lambda bi, i: (bi, i, 0)),
                  pl.BlockSpec((1, LANES), lambda bi, i: (0, 0)),
                  pl.BlockSpec((None, nc, LANES), lambda bi, i: (bi, 0, 0)),
                  pl.BlockSpec((None, 2, HEAD_DIM, nc), lambda bi, i: (bi, 0, 0, 0)),
                  full(12), vspec(4), full(13), vspec(5)],
        out_specs=pl.BlockSpec((None, tq, 4 * LANES), lambda bi, i: (bi, i, 0)),
        out_shape=jax.ShapeDtypeStruct((b, s, 4 * LANES), BF16),
        scratch_shapes=[pltpu.VMEM((2, nk, 8, tq), F32), pltpu.VMEM((n_wc + 1, tq, tq), F32)]
        + _pipe_scratch(2, tq, tq),
        compiler_params=_cparams(("parallel", "arbitrary")),
        name="nsa_attn",
    )(rp, pp, dg, gate_b, kcmp, vct, rp, vt, rp, vt)


def _cast_kernel(x_ref, o_ref):
    o_ref[...] = x_ref[...].astype(o_ref.dtype)


def _layer_to_bf16(w, layer):
    _, e, r, c = w.shape
    tr = 1 << ((2 ** 21 // c).bit_length() - 1)
    out = pl.pallas_call(
        _cast_kernel,
        grid=(e * r // tr,),
        in_specs=[pl.BlockSpec((None, tr, c), lambda i: (layer, i, 0))],
        out_specs=pl.BlockSpec((tr, c), lambda i: (i, 0)),
        out_shape=jax.ShapeDtypeStruct((e * r, c), BF16),
        compiler_params=_cparams(("parallel",)),
        name="cast_bf16",
    )(w.reshape(w.shape[0], e * r, c))
    return out.reshape(e, r, c)


def _moe_ffn_kernel(be_ref, nu_ref, x_ref, wg_ref, wu_ref, wd_ref, o_ref):
    used = pl.program_id(0) < nu_ref[0]

    @pl.when(used)
    def _():
        x = x_ref[...]
        h = jax.nn.silu(jnp.dot(x, wg_ref[...], preferred_element_type=F32)) * jnp.dot(
            x, wu_ref[...], preferred_element_type=F32)
        o_ref[...] = jnp.dot(h.astype(BF16), wd_ref[...], preferred_element_type=F32).astype(o_ref.dtype)

    @pl.when(jnp.logical_not(used))
    def _():
        o_ref[...] = jnp.zeros(o_ref.shape, o_ref.dtype)


def _moe_ffn(x_sorted, block_e, n_used, wg, wu, wd):
    ns, d = x_sorted.shape
    fdim = wg.shape[2]
    once = pl.Buffered(1)
    grid_spec = pltpu.PrefetchScalarGridSpec(
        num_scalar_prefetch=2,
        grid=(ns // MOE_TM,),
        in_specs=[pl.BlockSpec((MOE_TM, d), lambda i, be, nu: (jnp.minimum(i, nu[0] - 1), 0)),
                  pl.BlockSpec((None, d, fdim), lambda i, be, nu: (be[i], 0, 0), pipeline_mode=once),
                  pl.BlockSpec((None, d, fdim), lambda i, be, nu: (be[i], 0, 0), pipeline_mode=once),
                  pl.BlockSpec((None, fdim, d), lambda i, be, nu: (be[i], 0, 0), pipeline_mode=once)],
        out_specs=pl.BlockSpec((MOE_TM, d), lambda i, be, nu: (i, 0)),
    )
    return pl.pallas_call(
        _moe_ffn_kernel,
        grid_spec=grid_spec,
        out_shape=jax.ShapeDtypeStruct((ns, d), BF16),
        compiler_params=_cparams(("arbitrary",)),
        name="moe_ffn",
    )(block_e, n_used, x_sorted, wg, wu, wd)


def _combine_ln_kernel(x_ref, y0_ref, y1_ref, rt_ref, g_ref, b_ref, xo_ref, xb_ref):
    rt = rt_ref[...]
    ffn = rt[:, 2:3] * y0_ref[...].astype(F32) + rt[:, 3:4] * y1_ref[...].astype(F32)
    y = _layer_norm(DN_ALPHA * x_ref[...] + ffn, g_ref[...], b_ref[...])
    xo_ref[...] = y
    xb_ref[...] = y.astype(BF16)


def _combine_ln(x, y0, y1, rt, g, b, tm=512):
    n, d = x.shape
    row = lambda i: (i, 0)
    fixed = lambda i: (0, 0)
    return pl.pallas_call(
        _combine_ln_kernel,
        grid=(n // tm,),
        in_specs=[pl.BlockSpec((tm, d), row), pl.BlockSpec((tm, d), row), pl.BlockSpec((tm, d), row),
                  pl.BlockSpec((tm, LANES), row), pl.BlockSpec((1, d), fixed), pl.BlockSpec((1, d), fixed)],
        out_specs=[pl.BlockSpec((tm, d), row), pl.BlockSpec((tm, d), row)],
        out_shape=[jax.ShapeDtypeStruct((n, d), F32), jax.ShapeDtypeStruct((n, d), BF16)],
        compiler_params=_cparams(("parallel",)),
        name="moe_combine_ln",
    )(x, y0, y1, rt, g.reshape(1, d), b.reshape(1, d))


def _moe_layout(rt, n):
    e_flat = rt[:, 0:TOP_K].astype(I32).reshape(-1)
    nk = n * TOP_K
    onehot = (e_flat[:, None] == jnp.arange(N_EXPERTS, dtype=I32)[None, :]).astype(I32)
    rank = jnp.take_along_axis(jnp.cumsum(onehot, axis=0), e_flat[:, None], axis=1)[:, 0] - 1
    counts = jnp.sum(onehot, axis=0)
    padded = (counts + MOE_TM - 1) // MOE_TM * MOE_TM
    pad_end = jnp.cumsum(padded)
    pad_start = pad_end - padded
    grp_start = jnp.cumsum(counts) - counts
    slot = pad_start[e_flat] + rank
    n_blocks = -(-nk // MOE_TM) + N_EXPERTS
    n_slots = n_blocks * MOE_TM
    order = jnp.argsort(e_flat, stable=True).astype(I32)
    sl = jnp.arange(n_slots, dtype=I32)
    slot_e = jnp.minimum(jnp.searchsorted(pad_end, sl, side='right'), N_EXPERTS - 1).astype(I32)
    within = sl - pad_start[slot_e]
    valid = within < counts[slot_e]
    src = jnp.where(valid, grp_start[slot_e] + within, 0)
    slot_tok = jnp.where(valid, order[src] // TOP_K, 0)
    n_used = (pad_end[-1] // MOE_TM).astype(I32).reshape(1)
    blk = jnp.arange(n_blocks, dtype=I32)
    block_e = slot_e[jnp.minimum(blk, n_used[0] - 1) * MOE_TM]
    return slot_tok, slot.reshape(n, TOP_K), block_e, n_used


def _pair_perm(n_heads):
    half = n_heads // 2
    cols = []
    for p in range(half):
        cols += list(range(p * HEAD_DIM, (p + 1) * HEAD_DIM))
        cols += list(range((half + p) * HEAD_DIM, (half + p + 1) * HEAD_DIM))
    return np.asarray(cols, dtype=np.int32)


def _pad_cols(w, width):
    return jnp.pad(w, ((0, 0), (0, width - w.shape[1])))


def _even_layer(x, xb, w_in, w_out, lam_params, subln, lam_init, wg, wu, wd, ln, tabs, bsz, seq_len):
    n, d = x.shape
    perm = _pair_perm(B_HEADS)
    aq, ak, av = w_in[:, 0:512], w_in[:, 512:1024], w_in[:, 1024:1536]
    bq, bk, bv = w_in[:, 1536:2048], w_in[:, 2048:2176], w_in[:, 2176:2304]
    iq, ik, iw = w_in[:, 2304:2560], w_in[:, 2560:2624], w_in[:, 2624:2628]
    w_rope = jnp.concatenate([aq * SCALE, ak, bq[:, perm] * SCALE, iq, bk, ik, ik], axis=1)
    w_rope = w_rope[:, _rope_layout(w_rope.shape[1])].astype(BF16)
    w_val = jnp.concatenate([av, bv], axis=1).astype(BF16)
    w_iw = _pad_cols(iw, LANES).astype(BF16)
    rp = _proj_rope(xb, w_rope, 512, 1024, seq_len, tabs).reshape(bsz, seq_len, -1)
    vt, iwv = _proj_values(xb, w_val, w_iw, None, 512, TQ)
    vt = vt.reshape(bsz, seq_len // TQ, -1, TQ)
    iwv = iwv.reshape(bsz, seq_len, LANES)
    o_a = _diff_attention(rp, vt, lam_params, subln, lam_init)
    o_b = _dsa_attention(rp, vt, iwv)
    wo_a = w_out[0:512].astype(BF16)
    wo_b = w_out[512:1024][perm].astype(BF16)
    g_mix, b_mix, g_ffn, b_ffn = ln
    x1, x1b = _outproj_ln(x, o_a.reshape(n, -1), o_b.reshape(n, -1), wo_a, wo_b, g_mix, b_mix)
    return _ffn_ln(x1, x1b, wg.astype(BF16), wu.astype(BF16), wd.astype(BF16), g_ffn, b_ffn)


def _odd_layer(x, xb, w_in, w_out, gate_b, pe, phi_w1, phi_w2, w_router, b_router, wg, wu, wd, ln, tabs,
               bsz, seq_len):
    n, d = x.shape
    perm = _pair_perm(D_HEADS)
    cq, ck, cv = w_in[:, 0:512], w_in[:, 512:1024], w_in[:, 1024:1536]
    dq = w_in[:, 1536:2048][:, perm] * SCALE
    dkc, dvc, dks = w_in[:, 2048:2176], w_in[:, 2176:2304], w_in[:, 2304:2432]
    dvs, dkw, dvw = w_in[:, 2432:2560], w_in[:, 2560:2688], w_in[:, 2688:2816]
    dg = w_in[:, 2816:2840]
    w_rope = jnp.concatenate([cq * SCALE, ck, dq, dks, dkw], axis=1)
    w_rope = w_rope[:, _rope_layout(w_rope.shape[1])].astype(BF16)
    w_plain = jnp.concatenate([dq, dkc, dvc], axis=1).astype(BF16)
    w_val = jnp.concatenate([cv, dvs, dvw], axis=1).astype(BF16)
    w_dg = _pad_cols(dg, LANES).astype(BF16)
    rp = _proj_rope(xb, w_rope, 512, w_rope.shape[1] // 2, seq_len, tabs).reshape(bsz, seq_len, -1)
    vt, dgv, pp = _proj_values(xb, w_val, w_dg, w_plain, 512, TQ)
    vt = vt.reshape(bsz, seq_len // TQ, -1, TQ)
    dgv = dgv.reshape(bsz, seq_len, LANES)
    pp = pp.reshape(bsz, seq_len, -1)

    o_c = _moba_attention(rp, vt)

    nc = seq_len // NSA_CMP_STRIDE
    tok = pp[:, :, 4 * LANES:6 * LANES].reshape(bsz, nc, NSA_CMP_STRIDE, 4, HEAD_DIM)
    r = tok.transpose(0, 3, 1, 2, 4).reshape(bsz, 4, nc, NSA_CMP_STRIDE * HEAD_DIM)
    pe_flat = jnp.pad(pe.reshape(2, 1, -1), ((0, 0), (0, 7), (0, 0))).astype(BF16)
    cmp = _nsa_compress(r, pe_flat, phi_w1.astype(BF16), phi_w2.astype(BF16))
    kcmp = jnp.concatenate([cmp[:, 0], cmp[:, 1]], axis=-1)
    vcmp = jnp.concatenate([cmp[:, 2], cmp[:, 3]], axis=-1)
    gb = _pad_cols(gate_b.reshape(1, -1), LANES)
    o_d = _nsa_attention(rp, pp, vt, dgv, gb, kcmp, vcmp)

    wo_c = w_out[0:512].astype(BF16)
    wo_d = w_out[512:1024][perm].astype(BF16)
    g_mix, b_mix, g_ffn, b_ffn = ln
    router = (_pad_cols(w_router, LANES), _pad_cols(b_router.reshape(1, -1), LANES))
    x1, x1b, rt = _outproj_ln(x, o_c.reshape(n, -1), o_d.reshape(n, -1), wo_c, wo_d, g_mix, b_mix, router)

    slot_tok, slot, block_e, n_used = _moe_layout(rt, n)
    y_slots = _moe_ffn(x1b[slot_tok], block_e, n_used, wg, wu, wd)
    return _combine_ln(x1, y_slots[slot[:, 0]], y_slots[slot[:, 1]], rt, g_ffn, b_ffn)


@jax.jit
def kernel(x, ev_w_in, ev_w_out, dif_lambda, dif_subln, ffd_w_gate, ffd_w_up, ffd_w_down, od_w_in, od_w_out,
           nsa_gate_b, nsa_pe, nsa_phi_w1, nsa_phi_w2, moe_w_router, moe_b_router, moe_w_gate, moe_w_up,
           moe_w_down, ln_mix_g, ln_mix_b, ln_ffn_g, ln_ffn_b):
    bsz, seq_len, d = x.shape
    tabs = _rope_tables(seq_len)
    xf = x.reshape(bsz * seq_len, d)
    xb = xf.astype(BF16)
    for l in range(DEPTH):
        i = l // 2
        ln = (ln_mix_g[l], ln_mix_b[l], ln_ffn_g[l], ln_ffn_b[l])
        if l % 2 == 0:
            lam_init = 0.8 - 0.6 * math.exp(-0.3 * l)
            xf, xb = _even_layer(xf, xb, ev_w_in[i], ev_w_out[i], dif_lambda[i], dif_subln[i], lam_init,
                                 ffd_w_gate[i], ffd_w_up[i], ffd_w_down[i], ln, tabs, bsz, seq_len)
        else:
            xf, xb = _odd_layer(xf, xb, od_w_in[i], od_w_out[i], nsa_gate_b[i], nsa_pe[i], nsa_phi_w1[i],
                                nsa_phi_w2[i], moe_w_router[i], moe_b_router[i], _layer_to_bf16(moe_w_gate, i),
                                _layer_to_bf16(moe_w_up, i), _layer_to_bf16(moe_w_down, i), ln, tabs, bsz, seq_len)
    return xf.reshape(bsz, seq_len, d)
```

```python
import functools
import math

import numpy as np
import jax
import jax.numpy as jnp
from jax import lax
from jax.experimental import pallas as pl
from jax.experimental.pallas import tpu as pltpu

F32 = jnp.float32
BF16 = jnp.bfloat16
I32 = jnp.int32

LANES = 128
VMEM_LIMIT = 56 * 1024 * 1024

DEPTH = 4
HEAD_DIM = 64
ROPE_THETA = 10000.0
LN_EPS = 1e-5
DN_ALPHA = (2 * DEPTH) ** 0.25
SCALE = HEAD_DIM ** -0.5 * math.log2(math.e)
NEG = -1e30
BIG = 1e30
M_INIT = -1e30
NINF = float("-inf")

A_HEADS = 4
B_HEADS = 8
IDX_HEADS = 4
DSA_TOPK = 256
C_HEADS = 8
MOBA_BLOCK = 256
MOBA_TOPK = 3
D_HEADS = 8
NSA_CMP_LEN = 32
NSA_CMP_STRIDE = 16
NSA_SLC_BLOCK = 64
NSA_SLC_TOPK = 16
NSA_WINDOW = 512
NSA_PHI_HIDDEN = 256
N_EXPERTS = 8
TOP_K = 2
MOE_TM = 512
N_BISECT = 14


def _cparams(sem):
    return pltpu.CompilerParams(dimension_semantics=sem, vmem_limit_bytes=VMEM_LIMIT)


def _dot_nt(a, b):
    return lax.dot_general(a, b, (((1,), (1,)), ((), ())), preferred_element_type=F32)


def _layer_norm(y, g, b):
    mu = jnp.mean(y, axis=-1, keepdims=True)
    yc = y - mu
    var = jnp.mean(yc * yc, axis=-1, keepdims=True)
    return yc * lax.rsqrt(var + LN_EPS) * g + b


def _safe_recip(l):
    return jnp.where(l > 0.0, 1.0 / jnp.where(l > 0.0, l, 1.0), 0.0)


def _split_halves(t, roped=False):
    lane = lax.broadcasted_iota(I32, (1, LANES), 1)
    lo = (lane // (HEAD_DIM // 2)) % 2 == 0 if roped else lane < HEAD_DIM
    z = jnp.zeros_like(t)
    return jnp.where(lo, t, z), jnp.where(lo, z, t)


def _rope_layout(n_cols):
    q = HEAD_DIM // 2
    tile = np.concatenate([np.arange(0, q), np.arange(2 * q, 3 * q), np.arange(q, 2 * q), np.arange(3 * q, 4 * q)])
    return (np.arange(0, n_cols, LANES)[:, None] + tile[None, :]).reshape(-1).astype(np.int32)


def _rope_mm_kernel(x_ref, w_ref, cos_ref, sin_ref, o_ref):
    acc = jnp.dot(x_ref[...].astype(BF16), w_ref[...], preferred_element_type=F32)
    cos = cos_ref[...]
    sin = sin_ref[...]
    for c in range(acc.shape[1] // LANES):
        a = acc[:, c * LANES:(c + 1) * LANES]
        rot = pltpu.roll(a, LANES // 2, 1)
        o_ref[:, c * LANES:(c + 1) * LANES] = (a * cos + rot * sin).astype(o_ref.dtype)


def _proj_rope(x, w, tm, tn, seq_len, rope_tabs):
    n, d = x.shape
    p = w.shape[1]
    nt = seq_len // tm
    return pl.pallas_call(
        _rope_mm_kernel,
        grid=(n // tm, p // tn),
        in_specs=[pl.BlockSpec((tm, d), lambda i, j: (i, 0)), pl.BlockSpec((d, tn), lambda i, j: (0, j)),
                  pl.BlockSpec((tm, LANES), lambda i, j: (i % nt, 0)),
                  pl.BlockSpec((tm, LANES), lambda i, j: (i % nt, 0))],
        out_specs=pl.BlockSpec((tm, tn), lambda i, j: (i, j)),
        out_shape=jax.ShapeDtypeStruct((n, p), BF16),
        compiler_params=_cparams(("parallel", "arbitrary")),
        name="proj_rope",
    )(x, w, *rope_tabs)


def _rope_tables(seq_len):
    d = HEAD_DIM
    inv = ROPE_THETA ** (-jnp.arange(0, d, 2, dtype=F32) / d)
    ang = jnp.arange(seq_len, dtype=I32).astype(F32)[:, None] * inv[None, :]
    cos = jnp.cos(ang)
    sin = jnp.sin(ang)
    cos128 = jnp.tile(cos, (1, LANES // (d // 2)))
    sin128 = jnp.concatenate([-sin, -sin, sin, sin], axis=1)
    return cos128, sin128


def _route_top2(x, w, b):
    xh, wh = x.astype(BF16), w.astype(BF16)
    xl, wl = (x - xh.astype(F32)).astype(BF16), (w - wh.astype(F32)).astype(BF16)
    logits = (jnp.dot(xh, wh, preferred_element_type=F32) + jnp.dot(xh, wl, preferred_element_type=F32)
              + jnp.dot(xl, wh, preferred_element_type=F32)) + b
    lane = lax.broadcasted_iota(I32, (1, LANES), 1)
    lanef = lane.astype(F32)
    v = jnp.where(lane < N_EXPERTS, logits, NINF)
    l0 = jnp.max(v, axis=1, keepdims=True)
    i0 = jnp.min(jnp.where(v == l0, lanef, float(LANES)), axis=1, keepdims=True)
    v = jnp.where(lanef == i0, NINF, v)
    l1 = jnp.max(v, axis=1, keepdims=True)
    i1 = jnp.min(jnp.where(v == l1, lanef, float(LANES)), axis=1, keepdims=True)
    e1 = jnp.exp(l1 - l0)
    g0 = 1.0 / (1.0 + e1)
    g1 = e1 / (1.0 + e1)
    return jnp.where(lane == 0, i0, jnp.where(lane == 1, i1, jnp.where(lane == 2, g0, jnp.where(lane == 3, g1, 0.0))))


def _outproj_ln_kernel(x_ref, a_ref, b_ref, wa_ref, wb_ref, g_ref, bb_ref, *rest):
    mix = (jnp.dot(a_ref[...], wa_ref[...], preferred_element_type=F32)
           + jnp.dot(b_ref[...], wb_ref[...], preferred_element_type=F32))
    y = _layer_norm(DN_ALPHA * x_ref[...] + mix, g_ref[...], bb_ref[...])
    if len(rest) == 2:
        xo_ref, xb_ref = rest
    else:
        wr_ref, br_ref, xo_ref, xb_ref, rt_ref = rest
        rt_ref[...] = _route_top2(y, wr_ref[...], br_ref[...])
    xo_ref[...] = y
    xb_ref[...] = y.astype(BF16)


def _outproj_ln(x, oa, ob, wa, wb, g, b, router=None, tm=512):
    n, d = x.shape
    ka, kb = oa.shape[1], ob.shape[1]
    row = lambda i: (i, 0)
    fixed = lambda i: (0, 0)
    in_specs = [pl.BlockSpec((tm, d), row), pl.BlockSpec((tm, ka), row), pl.BlockSpec((tm, kb), row),
                pl.BlockSpec((ka, d), fixed), pl.BlockSpec((kb, d), fixed),
                pl.BlockSpec((1, d), fixed), pl.BlockSpec((1, d), fixed)]
    out_specs = [pl.BlockSpec((tm, d), row), pl.BlockSpec((tm, d), row)]
    out_shape = [jax.ShapeDtypeStruct((n, d), F32), jax.ShapeDtypeStruct((n, d), BF16)]
    args = [x, oa, ob, wa, wb, g.reshape(1, d), b.reshape(1, d)]
    if router is not None:
        in_specs += [pl.BlockSpec((d, LANES), fixed), pl.BlockSpec((1, LANES), fixed)]
        out_specs.append(pl.BlockSpec((tm, LANES), row))
        out_shape.append(jax.ShapeDtypeStruct((n, LANES), F32))
        args += list(router)
    return pl.pallas_call(
        _outproj_ln_kernel,
        grid=(n // tm,),
        in_specs=in_specs,
        out_specs=out_specs,
        out_shape=out_shape,
        compiler_params=_cparams(("parallel",)),
        name="outproj_ln",
    )(*args)


def _ffn_ln_kernel(x_ref, xb_ref, wg_ref, wu_ref, wd_ref, g_ref, b_ref, xo_ref, xob_ref):
    xb = xb_ref[...]
    h = jax.nn.silu(jnp.dot(xb, wg_ref[...], preferred_element_type=F32)) * jnp.dot(
        xb, wu_ref[...], preferred_element_type=F32)
    ffn = jnp.dot(h.astype(BF16), wd_ref[...], preferred_element_type=F32)
    y = _layer_norm(DN_ALPHA * x_ref[...] + ffn, g_ref[...], b_ref[...])
    xo_ref[...] = y
    xob_ref[...] = y.astype(BF16)


def _ffn_ln(x, xb, wg, wu, wd, g, b, tm=512):
    n, d = x.shape
    fdim = wg.shape[1]
    row = lambda i: (i, 0)
    fixed = lambda i: (0, 0)
    once = pl.Buffered(1)
    return pl.pallas_call(
        _ffn_ln_kernel,
        grid=(n // tm,),
        in_specs=[pl.BlockSpec((tm, d), row), pl.BlockSpec((tm, d), row),
                  pl.BlockSpec((d, fdim), fixed, pipeline_mode=once),
                  pl.BlockSpec((d, fdim), fixed, pipeline_mode=once),
                  pl.BlockSpec((fdim, d), fixed, pipeline_mode=once),
                  pl.BlockSpec((1, d), fixed), pl.BlockSpec((1, d), fixed)],
        out_specs=[pl.BlockSpec((tm, d), row), pl.BlockSpec((tm, d), row)],
        out_shape=[jax.ShapeDtypeStruct((n, d), F32), jax.ShapeDtypeStruct((n, d), BF16)],
        compiler_params=_cparams(("parallel",)),
        name="ffn_ln",
    )(x, xb, wg, wu, wd, g.reshape(1, d), b.reshape(1, d))


def _top_n_mask(v, n, axis):
    idx = lax.broadcasted_iota(I32, v.shape, axis).astype(F32)
    sel = jnp.zeros(v.shape, F32)
    for _ in range(n):
        mx = jnp.max(v, axis=axis, keepdims=True)
        first = jnp.min(jnp.where(v == mx, idx, float(v.shape[axis])), axis=axis, keepdims=True)
        pick = idx == first
        sel = jnp.where(pick, 1.0, sel)
        v = jnp.where(pick, NINF, v)
    return sel


DV_PAD = 16
TQ = 256


def _with_ones(vt):
    return jnp.concatenate([vt, jnp.ones((DV_PAD, vt.shape[1]), vt.dtype)], axis=0)


def _mm_t_kernel(x_ref, wv_ref, ws_ref, *rest, tk):
    x = x_ref[...].astype(BF16)
    vt_ref, small_ref = rest[-2:] if len(rest) == 2 else rest[1:3]
    acc = jnp.dot(x, wv_ref[...], preferred_element_type=F32)
    for cc in range(acc.shape[0] // tk):
        vt_ref[cc] = acc[cc * tk:(cc + 1) * tk, :].T.astype(vt_ref.dtype)
    small_ref[...] = jnp.dot(x, ws_ref[...], preferred_element_type=F32)
    if len(rest) == 4:
        rest[3][...] = jnp.dot(x, rest[0][...], preferred_element_type=F32).astype(rest[3].dtype)


def _proj_values(x, w_val, w_small, w_plain, tm, tk):
    n, d = x.shape
    pv, ps = w_val.shape[1], w_small.shape[1]
    row = lambda i: (i, 0)
    fixed = lambda i: (0, 0)
    in_specs = [pl.BlockSpec((tm, d), row), pl.BlockSpec((d, pv), fixed), pl.BlockSpec((d, ps), fixed)]
    out_specs = [pl.BlockSpec((tm // tk, pv, tk), lambda i: (i, 0, 0)), pl.BlockSpec((tm, ps), row)]
    out_shape = [jax.ShapeDtypeStruct((n // tk, pv, tk), BF16), jax.ShapeDtypeStruct((n, ps), F32)]
    args = [x, w_val, w_small]
    if w_plain is not None:
        pp = w_plain.shape[1]
        in_specs.append(pl.BlockSpec((d, pp), fixed))
        out_specs.append(pl.BlockSpec((tm, pp), row))
        out_shape.append(jax.ShapeDtypeStruct((n, pp), BF16))
        args.append(w_plain)
    return pl.pallas_call(
        functools.partial(_mm_t_kernel, tk=tk),
        grid=(n // tm,),
        in_specs=in_specs,
        out_specs=out_specs,
        out_shape=out_shape,
        compiler_params=_cparams(("parallel",)),
        name="proj_t",
    )(*args)


def _normalize_t(acc, width):
    return acc[:width] * _safe_recip(acc[width:width + 1])


def _pipe_flash(n, ns, qk, vt_at, bias_at, bufs, dv, tk, tq):
    sa, sb, pa, pb = bufs
    for j in range(ns):
        sa[j] = qk(0, j)
        pb[j] = jnp.zeros((tk, tq), BF16)

    def half(c, carry, s_cur, s_nxt, p_prev, p_cur):
        nxt = jnp.minimum(c + 1, n - 1)
        for j in range(ns):
            s_nxt[j] = qk(nxt, j)
        cp = jnp.clip(c - 1, 0, n - 1)
        out = []
        for j in range(ns):
            m, acc, alpha = carry[j]
            acc = alpha * acc + jnp.dot(vt_at(cp, j), p_prev[j], preferred_element_type=F32)
            st = s_cur[j] + bias_at(c, j)
            m_new = jnp.maximum(m, jnp.max(st, axis=0, keepdims=True))
            alpha = jnp.exp2(m - m_new)
            p_cur[j] = jnp.exp2((st - m_new).astype(BF16))
            out.append((m_new, acc, alpha))
        return tuple(out)

    def body(t, carry):
        carry = half(2 * t, carry, sa, sb, pb, pa)
        return half(2 * t + 1, carry, sb, sa, pa, pb)

    init = (jnp.full((1, tq), M_INIT, F32), jnp.zeros((dv, tq), F32), jnp.ones((1, tq), F32))
    trips = (n + 1) // 2
    carry = lax.fori_loop(0, trips, body, (init,) * ns)
    cl = jnp.minimum(2 * trips - 1, n - 1)
    outs = []
    for j in range(ns):
        _, acc, alpha = carry[j]
        outs.append(alpha * acc + jnp.dot(vt_at(cl, j), pb[j], preferred_element_type=F32))
    return outs


def _pipe_scratch(ns, tk, tq):
    return [pltpu.VMEM((ns, tk, tq), F32)] * 2 + [pltpu.VMEM((ns, tk, tq), BF16)] * 2


def _causal_t(t):
    return jnp.where(lax.broadcasted_iota(I32, (t, t), 0) <= lax.broadcasted_iota(I32, (t, t), 1), 0.0, NINF)


def _diff_kernel(lam_ref, sub_ref, q_ref, k_ref, vt_ref, o_ref, *bufs, tq, lam_init):
    i = pl.program_id(2)
    lp = lam_ref[...]
    lam = (jnp.exp(jnp.sum(lp[0:1] * lp[1:2], axis=1, keepdims=True))
           - jnp.exp(jnp.sum(lp[2:3] * lp[3:4], axis=1, keepdims=True)) + lam_init)
    qs = _split_halves(q_ref[...], roped=True)
    n = i + 1
    bufs, tab_ref = bufs[:4], bufs[4]
    tab_ref[0] = jnp.zeros((tq, tq), F32)
    tab_ref[1] = _causal_t(tq)
    tab_ref[2] = jnp.full((tq, tq), NINF, F32)

    def qk(c, j):
        off = pl.multiple_of(c * tq, tq)
        return _dot_nt(k_ref[pl.ds(off, tq), :], qs[j])

    def bias_at(c, j):
        return tab_ref[jnp.where(c == i, 1, jnp.where(c < n, 0, 2))]

    outs = _pipe_flash(n, 2, qk, lambda c, j: _with_ones(vt_ref[c]), bias_at, bufs, LANES + DV_PAD, tq, tq)
    o = _normalize_t(outs[0], LANES) - lam * _normalize_t(outs[1], LANES)
    o = o * lax.rsqrt(jnp.mean(o * o, axis=0, keepdims=True) + LN_EPS)
    o = o * sub_ref[...] * (1.0 - lam_init)
    o_ref[...] = o.T.astype(o_ref.dtype)


def _diff_attention(rp, vt, lam_params, subln, lam_init, tq=TQ):
    b, s, _ = rp.shape
    nk = s // tq
    return pl.pallas_call(
        functools.partial(_diff_kernel, tq=tq, lam_init=lam_init),
        grid=(b, A_HEADS, nk),
        in_specs=[pl.BlockSpec((4, HEAD_DIM), lambda bi, h, i: (0, 0)),
                  pl.BlockSpec((LANES, 1), lambda bi, h, i: (0, 0)),
                  pl.BlockSpec((None, tq, LANES), lambda bi, h, i: (bi, i, h)),
                  pl.BlockSpec((None, s, LANES), lambda bi, h, i: (bi, 0, A_HEADS + h)),
                  pl.BlockSpec((None, nk, LANES, tq), lambda bi, h, i: (bi, 0, h, 0))],
        out_specs=pl.BlockSpec((None, tq, LANES), lambda bi, h, i: (bi, i, h)),
        out_shape=jax.ShapeDtypeStruct((b, s, A_HEADS * LANES), BF16),
        scratch_shapes=_pipe_scratch(2, tq, tq) + [pltpu.VMEM((3, tq, tq), F32)],
        compiler_params=_cparams(("parallel", "parallel", "arbitrary")),
        name="diff_attn",
    )(lam_params, subln.reshape(LANES, 1), rp, rp, vt)


def _fold8(x, op):
    acc = x[0:8]
    for r in range(1, x.shape[0] // 8):
        acc = op(acc, x[r * 8:(r + 1) * 8])
    return acc


def _dsa_kernel(iq_ref, ikk_ref, iw_ref, q_ref, k_ref, vt_ref, o_ref, s_ref, j_ref, *bufs, tq, ksel, seq_len):
    tk = tq
    i = pl.program_id(1)
    nch = i + 1
    ksel_f = float(ksel)
    k_loc = lax.broadcasted_iota(I32, (tk, tq), 0)
    q_loc = lax.broadcasted_iota(I32, (tk, tq), 1)
    qpos = i * tq + lax.broadcasted_iota(I32, (1, tq), 1)

    iq = iq_ref[...]
    iwt = iw_ref[...].T
    iqh = []
    for pair in range(IDX_HEADS // 2):
        iqh += list(_split_halves(iq[:, pair * LANES:(pair + 1) * LANES], roped=True))

    def scores(c):
        off = pl.multiple_of(c * tk, tk)
        kk = ikk_ref[pl.ds(off, tk), :]
        sc = iwt[0:1] * jnp.maximum(_dot_nt(kk, iqh[0]), 0.0)
        for h in range(1, IDX_HEADS):
            sc = sc + iwt[h:h + 1] * jnp.maximum(_dot_nt(kk, iqh[h]), 0.0)
        return sc

    def full_body(c, carry):
        mx, mn = carry
        sc = scores(c)
        s_ref[c] = sc
        return jnp.maximum(mx, _fold8(sc, jnp.maximum)), jnp.minimum(mn, _fold8(sc, jnp.minimum))

    mx, mn = lax.fori_loop(0, i, full_body, (jnp.full((8, tq), -BIG, F32), jnp.full((8, tq), BIG, F32)))
    sc = scores(i)
    causal = k_loc <= q_loc
    s_ref[i] = jnp.where(causal, sc, NEG)
    mx = jnp.maximum(mx, _fold8(jnp.where(causal, sc, -BIG), jnp.maximum))
    mn = jnp.minimum(mn, _fold8(jnp.where(causal, sc, BIG), jnp.minimum))
    smax = jnp.max(mx, axis=0, keepdims=True)
    smin = jnp.min(mn, axis=0, keepdims=True)
    s_ref[nch] = jnp.full((tk, tq), NINF, F32)

    def count_where(ind):
        def body(c, acc):
            return acc + _fold8(ind(s_ref[c], c * tk + k_loc), jnp.add)
        acc = lax.fori_loop(0, nch, body, jnp.zeros((8, tq), F32))
        return jnp.sum(acc, axis=0, keepdims=True)

    def count_ge(th):
        return count_where(lambda x, kidx: jnp.where(x >= th, 1.0, 0.0))

    def max_below(th):
        def body(c, acc):
            x = s_ref[c]
            return jnp.maximum(acc, _fold8(jnp.where(x < th, x, NINF), jnp.maximum))
        acc = lax.fori_loop(0, nch, body, jnp.full((8, tq), NINF, F32))
        return jnp.max(acc, axis=0, keepdims=True)

    n_causal = (qpos + 1).astype(F32)
    take_all = n_causal <= ksel_f
    done0 = jnp.where(take_all | (count_ge(smax) >= ksel_f), 1.0, 0.0)

    def bisect(lo, hi):
        mid = lo + (hi - lo) * 0.5
        ge = count_ge(mid) >= ksel_f
        return jnp.where(ge, mid, lo), jnp.where(ge, hi, mid)

    lo, hi = lax.fori_loop(0, N_BISECT, lambda _, c: bisect(*c), (smin, smax))

    def snap_body(carry):
        lo, hi, th, done, _ = carry
        lo, hi = bisect(lo, hi)
        t1 = max_below(hi)
        ok = count_ge(t1) >= ksel_f
        th = jnp.where(done > 0.0, th, t1)
        hi = jnp.where(ok, hi, t1)
        done = jnp.where(ok, 1.0, done)
        return lo, hi, th, done, jnp.sum(1.0 - done)

    _, _, th, _, _ = lax.while_loop(lambda c: c[4] > 0.0, snap_body,
                                    (lo, hi, smax, done0, jnp.sum(1.0 - done0)))

    c_ge = count_ge(th)
    need_tb = jnp.where(take_all, 0.0, jnp.where(c_ge > ksel_f, 1.0, 0.0))
    j_ref[...] = jnp.full((8, tq), seq_len - 1, I32)

    @pl.when(jnp.sum(need_tb) > 0.0)
    def _():
        need = ksel_f - count_where(lambda x, kidx: jnp.where(x > th, 1.0, 0.0))

        def jb(_, carry):
            lo_j, hi_j = carry
            mid = (lo_j + hi_j) // 2
            cnt = count_where(lambda x, kidx: jnp.where(x == th, jnp.where(kidx <= mid, 1.0, 0.0), 0.0))
            ge = cnt >= need
            return jnp.where(ge, lo_j, mid), jnp.where(ge, mid, hi_j)

        n_it = int(math.ceil(math.log2(seq_len))) + 1
        _, hi_j = lax.fori_loop(0, n_it, jb, (jnp.full((1, tq), -1, I32), jnp.full((1, tq), seq_len - 1, I32)))
        j_ref[...] = jnp.broadcast_to(hi_j, (8, tq))

    jsel = j_ref[0:1, :]

    def bias_body(c, _):
        x = s_ref[c]
        kidx = c * tk + k_loc
        keep = jnp.where(x > th, 0.0, jnp.where(x == th, jnp.where(kidx <= jsel, 0.0, NINF), NINF))
        keep = jnp.where(take_all, 0.0, keep)
        s_ref[c] = jnp.where(kidx <= qpos, keep, NINF)
        return 0

    lax.fori_loop(0, nch, bias_body, 0)

    for p in range(B_HEADS // 2):
        qs = _split_halves(q_ref[:, p * LANES:(p + 1) * LANES], roped=True)

        def qk(c, j, qs=qs):
            off = pl.multiple_of(c * tk, tk)
            return _dot_nt(k_ref[pl.ds(off, tk), :], qs[j])

        outs = _pipe_flash(nch, 2, qk, lambda c, j: _with_ones(vt_ref[c, j * HEAD_DIM:(j + 1) * HEAD_DIM, :]),
                           lambda c, j: s_ref[jnp.minimum(c, nch)], bufs, HEAD_DIM + DV_PAD, tk, tq)
        o = jnp.concatenate([_normalize_t(outs[0], HEAD_DIM), _normalize_t(outs[1], HEAD_DIM)], axis=0)
        o_ref[:, p * LANES:(p + 1) * LANES] = o.T.astype(o_ref.dtype)


def _dsa_attention(rp, vt, iw, tq=TQ):
    b, s, _ = rp.shape
    ksel = min(DSA_TOPK, s // 4)
    nk = s // tq
    return pl.pallas_call(
        functools.partial(_dsa_kernel, tq=tq, ksel=ksel, seq_len=s),
        grid=(b, nk),
        in_specs=[pl.BlockSpec((None, tq, 2 * LANES), lambda bi, i: (bi, i, 6)),
                  pl.BlockSpec((None, s, LANES), lambda bi, i: (bi, 0, 15)),
                  pl.BlockSpec((None, tq, LANES), lambda bi, i: (bi, i, 0)),
                  pl.BlockSpec((None, tq, 4 * LANES), lambda bi, i: (bi, i, 2)),
                  pl.BlockSpec((None, s, LANES), lambda bi, i: (bi, 0, 14)),
                  pl.BlockSpec((None, nk, LANES, tq), lambda bi, i: (bi, 0, 4, 0))],
        out_specs=pl.BlockSpec((None, tq, 4 * LANES), lambda bi, i: (bi, i, 0)),
        out_shape=jax.ShapeDtypeStruct((b, s, 4 * LANES), BF16),
        scratch_shapes=[pltpu.VMEM((nk + 1, tq, tq), F32), pltpu.VMEM((8, tq), I32)] + _pipe_scratch(2, tq, tq),
        compiler_params=_cparams(("parallel", "arbitrary")),
        name="dsa_attn",
    )(rp, rp, iw, rp, rp, vt)


def _moba_kernel(q_ref, k_ref, vt_ref, o_ref, km_ref, sel_ref, *bufs, seq_len, n_sel):
    tq = MOBA_BLOCK
    qb = pl.program_id(2)

    @pl.when(qb == 0)
    def _():
        j = lax.broadcasted_iota(I32, (LANES, seq_len), 0)
        s = lax.broadcasted_iota(I32, (LANES, seq_len), 1)
        avg = jnp.where(s // MOBA_BLOCK == j, 1.0 / MOBA_BLOCK, 0.0).astype(BF16)
        km_ref[...] = jnp.dot(avg, k_ref[...], preferred_element_type=F32)

    nbp = sel_ref.shape[1]
    km = km_ref[0:nbp, :]
    qs = _split_halves(q_ref[...], roped=True)
    blk = lax.broadcasted_iota(I32, (nbp, tq), 0)
    past = blk < qb
    for j in range(2):
        gate = lax.dot_general(km, qs[j].astype(F32), (((1,), (1,)), ((), ())),
                               precision=lax.Precision.HIGHEST, preferred_element_type=F32)
        gate = jnp.where(blk < seq_len // MOBA_BLOCK, jnp.where(past, gate, NEG), NINF)
        sel = _top_n_mask(gate, n_sel, 0)
        sel_ref[j] = jnp.where(past, jnp.where(sel > 0.5, 0.0, NINF), NINF)
    own = _causal_t(tq)
    n = qb + 1

    def qk(c, j):
        off = pl.multiple_of(c * tq, tq)
        return _dot_nt(k_ref[pl.ds(off, tq), :], qs[j])

    def bias_at(c, j):
        chosen = sel_ref[j, pl.ds(jnp.minimum(c, nbp - 1), 1), :]
        return jnp.where(c == qb, own, jnp.where(c < n, chosen, NINF))

    outs = _pipe_flash(n, 2, qk, lambda c, j: _with_ones(vt_ref[c, j * HEAD_DIM:(j + 1) * HEAD_DIM, :]), bias_at, bufs,
                       HEAD_DIM + DV_PAD, tq, tq)
    o = jnp.concatenate([_normalize_t(outs[0], HEAD_DIM), _normalize_t(outs[1], HEAD_DIM)], axis=0)
    o_ref[...] = o.T.astype(o_ref.dtype)


def _moba_attention(rp, vt):
    b, s, _ = rp.shape
    tq = MOBA_BLOCK
    nb = s // tq
    n_sel = max(1, min(MOBA_TOPK, nb - 1))
    npair = C_HEADS // 2
    return pl.pallas_call(
        functools.partial(_moba_kernel, seq_len=s, n_sel=n_sel),
        grid=(b, npair, nb),
        in_specs=[pl.BlockSpec((None, tq, LANES), lambda bi, h, i: (bi, i, h)),
                  pl.BlockSpec((None, s, LANES), lambda bi, h, i: (bi, 0, npair + h)),
                  pl.BlockSpec((None, nb, LANES, tq), lambda bi, h, i: (bi, 0, h, 0))],
        out_specs=pl.BlockSpec((None, tq, LANES), lambda bi, h, i: (bi, i, h)),
        out_shape=jax.ShapeDtypeStruct((b, s, npair * LANES), BF16),
        scratch_shapes=[pltpu.VMEM((LANES, LANES), F32), pltpu.VMEM((2, -(-nb // 8) * 8, tq), F32)]
        + _pipe_scratch(2, tq, tq),
        compiler_params=_cparams(("parallel", "parallel", "arbitrary")),
        name="moba_attn",
    )(rp, rp, vt)


def _cmp_kernel(r_ref, pe_ref, w1_ref, w2_ref, o_ref):
    r = r_ref[...]
    w1 = w1_ref[...]
    half = r.shape[1]
    u = jnp.dot(r, w1[:half], preferred_element_type=F32)
    v = jnp.dot(r, w1[half:], preferred_element_type=F32)
    c = jnp.dot(pe_ref[...], w1, preferred_element_type=F32)[0:1]
    pre = u + pltpu.roll(v, r.shape[0] - 1, 0) + c
    o_ref[...] = jnp.dot(jax.nn.gelu(pre).astype(BF16), w2_ref[...],
                         preferred_element_type=F32).astype(o_ref.dtype)


def _nsa_compress(r, pe, w1, w2):
    b, _, nc, wdt = r.shape
    hid = w1.shape[2]
    return pl.pallas_call(
        _cmp_kernel,
        grid=(b, 4),
        in_specs=[pl.BlockSpec((None, None, nc, wdt), lambda bi, t: (bi, t, 0, 0)),
                  pl.BlockSpec((None, 8, 2 * wdt), lambda bi, t: (t // 2, 0, 0)),
                  pl.BlockSpec((None, 2 * wdt, hid), lambda bi, t: (t // 2, 0, 0)),
                  pl.BlockSpec((None, hid, HEAD_DIM), lambda bi, t: (t // 2, 0, 0))],
        out_specs=pl.BlockSpec((None, None, nc, HEAD_DIM), lambda bi, t: (bi, t, 0, 0)),
        out_shape=jax.ShapeDtypeStruct((b, 4, nc, HEAD_DIM), BF16),
        compiler_params=_cparams(("parallel", "arbitrary")),
        name="nsa_compress",
    )(r, pe, w1, w2)


def _nsa_kernel(qr_ref, qw_ref, dg_ref, gb_ref, kc_ref, vct_ref, ks_ref, vst_ref, kw_ref, vwt_ref,
                o_ref, sel_ref, wb_ref, *bufs, tq, seq_len):
    tk = tq
    i = pl.program_id(1)
    nch = i + 1
    nc = seq_len // NSA_CMP_STRIDE
    n_sb = seq_len // NSA_SLC_BLOCK
    n_sel = min(NSA_SLC_TOPK, n_sb)
    k_loc = lax.broadcasted_iota(I32, (tk, tq), 0)
    q_loc = lax.broadcasted_iota(I32, (tk, tq), 1)
    qpos = i * tq + lax.broadcasted_iota(I32, (1, tq), 1)

    gates_t = jax.nn.sigmoid(dg_ref[...] + gb_ref[...]).T
    kc = kc_ref[...]
    cmp_end = lax.broadcasted_iota(I32, (nc, 1), 0) * NSA_CMP_STRIDE + (NSA_CMP_LEN - 1)
    cbias = jnp.where(cmp_end <= qpos, 0.0, NINF)

    nh = D_HEADS // 2
    q_rot = [_split_halves(qr_ref[:, p * LANES:(p + 1) * LANES], roped=True) for p in range(nh)]
    q_raw = [_split_halves(qw_ref[:, p * LANES:(p + 1) * LANES]) for p in range(nh)]
    cbias4 = jnp.concatenate([cbias] * nh, axis=1)
    o_cmp, psum = [], []
    for g in range(2):
        qg = jnp.concatenate([q_raw[p][g] for p in range(nh)], axis=0)
        s = _dot_nt(kc, qg) + cbias4
        m = jnp.max(s, axis=0, keepdims=True)
        e = jnp.exp2(s - jnp.where(m == NINF, 0.0, m))
        pc = e * _safe_recip(jnp.sum(e, axis=0, keepdims=True))
        psum.append(sum(pc[:, p * tq:(p + 1) * tq] for p in range(nh)))
        o_cmp.append(jnp.dot(vct_ref[g], pc.astype(BF16), preferred_element_type=F32))

    per = tk // NSA_SLC_BLOCK
    nbp = -(-n_sb // 8) * 8
    cn = lax.broadcasted_iota(I32, (nbp, nc), 1) * NSA_CMP_STRIDE
    sj = lax.broadcasted_iota(I32, (nbp, nc), 0) * NSA_SLC_BLOCK
    shares = jnp.where((cn <= sj + NSA_SLC_BLOCK - 1) & (cn + NSA_CMP_LEN - 1 >= sj), 1.0, 0.0)
    blk = lax.broadcasted_iota(I32, (nbp, tq), 0)
    cur = qpos // NSA_SLC_BLOCK
    causal_b = blk <= cur
    forced = (blk == 0) | ((blk >= cur - 1) & causal_b)
    for g in range(2):
        imp = jnp.dot(shares, psum[g], precision=lax.Precision.HIGHEST, preferred_element_type=F32)
        val = jnp.where(forced, BIG, jnp.where(causal_b, imp, NEG))
        val = jnp.where(blk < n_sb, val, NINF)
        rowb = jnp.where(_top_n_mask(val, n_sel, 0) > 0.5, 0.0, NINF)
        for c in range(seq_len // tk):
            sel_ref[g, c] = jnp.concatenate([rowb[c * per:(c + 1) * per], jnp.zeros((8 - per, tq), F32)], axis=0)

    wb_ref[0] = jnp.where(k_loc <= q_loc, 0.0, NINF)
    wb_ref[1] = jnp.zeros((tk, tq), F32)
    wb_ref[2] = jnp.where(k_loc > q_loc, 0.0, NINF)
    wb_ref[3] = jnp.full((tk, tq), NINF, F32)
    n_wc = NSA_WINDOW // tk + 1
    w_first = jnp.maximum(i - (n_wc - 1), 0)
    n_w = i - w_first + 1

    for p in range(D_HEADS // 2):
        qs = q_rot[p]

        def qk_s(c, j, qs=qs):
            off = pl.multiple_of(c * tk, tk)
            return _dot_nt(ks_ref[pl.ds(off, tk), :], qs[j])

        def bias_s(c, j):
            rows = sel_ref[j, jnp.minimum(c, seq_len // tk - 1)]
            tile = jnp.concatenate([jnp.broadcast_to(rows[r:r + 1], (NSA_SLC_BLOCK, tq)) for r in range(per)], axis=0)
            return tile + wb_ref[jnp.where(c == i, 0, jnp.where(c < nch, 1, n_wc))]

        o_slc = _pipe_flash(nch, 2, qk_s, lambda c, j: _with_ones(vst_ref[c, j * HEAD_DIM:(j + 1) * HEAD_DIM, :]),
                            bias_s, bufs, HEAD_DIM + DV_PAD, tk, tq)

        def qk_w(c, j, qs=qs):
            off = pl.multiple_of((w_first + c) * tk, tk)
            return _dot_nt(kw_ref[pl.ds(off, tk), :], qs[j])

        def bias_w(c, j):
            d = i - (w_first + c)
            return wb_ref[jnp.where(c < n_w, d, n_wc)]

        o_win = _pipe_flash(n_w, 2, qk_w,
                            lambda c, j: _with_ones(vwt_ref[w_first + c, j * HEAD_DIM:(j + 1) * HEAD_DIM, :]), bias_w, bufs,
                            HEAD_DIM + DV_PAD, tk, tq)
        outs = []
        for g in range(2):
            h = g * (D_HEADS // 2) + p
            outs.append(gates_t[3 * h:3 * h + 1] * o_cmp[g][:, p * tq:(p + 1) * tq]
                        + gates_t[3 * h + 1:3 * h + 2] * _normalize_t(o_slc[g], HEAD_DIM)
                        + gates_t[3 * h + 2:3 * h + 3] * _normalize_t(o_win[g], HEAD_DIM))
        o_ref[:, p * LANES:(p + 1) * LANES] = jnp.concatenate(outs, axis=0).T.astype(o_ref.dtype)


def _nsa_attention(rp, pp, vt, dg, gate_b, kcmp, vcmp, tq=TQ):
    b, s, _ = rp.shape
    nk = s // tq
    nc = s // NSA_CMP_STRIDE
    assert NSA_WINDOW == 2 * tq
    n_wc = NSA_WINDOW // tq + 1
    vct = vcmp.reshape(b, nc, 2, HEAD_DIM).transpose(0, 2, 3, 1)
    full = lambda t: pl.BlockSpec((None, s, LANES), lambda bi, i: (bi, 0, t))
    vspec = lambda t: pl.BlockSpec((None, nk, LANES, tq), lambda bi, i: (bi, 0, t, 0))
    return pl.pallas_call(
        functools.partial(_nsa_kernel, tq=tq, seq_len=s),
        grid=(b, nk),
        in_specs=[pl.BlockSpec((None, tq, 4 * LANES), lambda bi, i: (bi, i, 2)),
                  pl.BlockSpec((None, tq, 4 * LANES), lambda bi, i: (bi, i, 0)),
                  pl.BlockSpec((None, tq, LANES), lambda bi, i: (bi, i, 0)),
                  pl.BlockSpec((1, LANES), lambda bi, i: (0, 0)),
                  pl.BlockSpec((None, nc, LANES), lambda bi, i: (bi, 0, 0)),
                  pl.BlockSpec((None, 2, HEAD_DIM, nc), lambda bi, i: (bi, 0, 0, 0)),
                  full(12), vspec(4), full(13), vspec(5)],
        out_specs=pl.BlockSpec((None, tq, 4 * LANES), lambda bi, i: (bi, i, 0)),
        out_shape=jax.ShapeDtypeStruct((b, s, 4 * LANES), BF16),
        scratch_shapes=[pltpu.VMEM((2, nk, 8, tq), F32), pltpu.VMEM((n_wc + 1, tq, tq), F32)]
        + _pipe_scratch(2, tq, tq),
        compiler_params=_cparams(("parallel", "arbitrary")),
        name="nsa_attn",
    )(rp, pp, dg, gate_b, kcmp, vct, rp, vt, rp, vt)


CAST_STREAMS = 4


def _cast_kernel(*refs):
    o_ref = refs[-1]
    rows = refs[0].shape[0]
    for k, x_ref in enumerate(refs[:-1]):
        o_ref[k * rows:(k + 1) * rows, :] = x_ref[...].astype(o_ref.dtype)


def _layer_to_bf16(w, layer):
    _, e, r, c = w.shape
    tr = (1 << ((2 ** 21 // c).bit_length() - 1)) // CAST_STREAMS
    w3 = w.reshape(w.shape[0], e * r, c)
    part = lambda k: pl.BlockSpec((None, tr, c), lambda i: (layer, CAST_STREAMS * i + k, 0))
    out = pl.pallas_call(
        _cast_kernel,
        grid=(e * r // (tr * CAST_STREAMS),),
        in_specs=[part(k) for k in range(CAST_STREAMS)],
        out_specs=pl.BlockSpec((tr * CAST_STREAMS, c), lambda i: (i, 0)),
        out_shape=jax.ShapeDtypeStruct((e * r, c), BF16),
        compiler_params=_cparams(("parallel",)),
        name="cast_bf16",
    )(*([w3] * CAST_STREAMS))
    return out.reshape(e, r, c)


def _moe_ffn_kernel(be_ref, nu_ref, x_ref, wg_ref, wu_ref, wd_ref, o_ref):
    used = pl.program_id(0) < nu_ref[0]

    @pl.when(used)
    def _():
        x = x_ref[...]
        h = jax.nn.silu(jnp.dot(x, wg_ref[...], preferred_element_type=F32)) * jnp.dot(
            x, wu_ref[...], preferred_element_type=F32)
        o_ref[...] = jnp.dot(h.astype(BF16), wd_ref[...], preferred_element_type=F32).astype(o_ref.dtype)

    @pl.when(jnp.logical_not(used))
    def _():
        o_ref[...] = jnp.zeros(o_ref.shape, o_ref.dtype)


def _moe_ffn(x_sorted, block_e, n_used, wg, wu, wd):
    ns, d = x_sorted.shape
    fdim = wg.shape[2]
    once = pl.Buffered(1)
    grid_spec = pltpu.PrefetchScalarGridSpec(
        num_scalar_prefetch=2,
        grid=(ns // MOE_TM,),
        in_specs=[pl.BlockSpec((MOE_TM, d), lambda i, be, nu: (jnp.minimum(i, nu[0] - 1), 0)),
                  pl.BlockSpec((None, d, fdim), lambda i, be, nu: (be[i], 0, 0), pipeline_mode=once),
                  pl.BlockSpec((None, d, fdim), lambda i, be, nu: (be[i], 0, 0), pipeline_mode=once),
                  pl.BlockSpec((None, fdim, d), lambda i, be, nu: (be[i], 0, 0), pipeline_mode=once)],
        out_specs=pl.BlockSpec((MOE_TM, d), lambda i, be, nu: (i, 0)),
    )
    return pl.pallas_call(
        _moe_ffn_kernel,
        grid_spec=grid_spec,
        out_shape=jax.ShapeDtypeStruct((ns, d), BF16),
        compiler_params=_cparams(("arbitrary",)),
        name="moe_ffn",
    )(block_e, n_used, x_sorted, wg, wu, wd)


def _combine_ln_kernel(x_ref, y0_ref, y1_ref, rt_ref, g_ref, b_ref, xo_ref, xb_ref):
    rt = rt_ref[...]
    ffn = rt[:, 2:3] * y0_ref[...].astype(F32) + rt[:, 3:4] * y1_ref[...].astype(F32)
    y = _layer_norm(DN_ALPHA * x_ref[...] + ffn, g_ref[...], b_ref[...])
    xo_ref[...] = y
    xb_ref[...] = y.astype(BF16)


def _combine_ln(x, y0, y1, rt, g, b, tm=512):
    n, d = x.shape
    row = lambda i: (i, 0)
    fixed = lambda i: (0, 0)
    return pl.pallas_call(
        _combine_ln_kernel,
        grid=(n // tm,),
        in_specs=[pl.BlockSpec((tm, d), row), pl.BlockSpec((tm, d), row), pl.BlockSpec((tm, d), row),
                  pl.BlockSpec((tm, LANES), row), pl.BlockSpec((1, d), fixed), pl.BlockSpec((1, d), fixed)],
        out_specs=[pl.BlockSpec((tm, d), row), pl.BlockSpec((tm, d), row)],
        out_shape=[jax.ShapeDtypeStruct((n, d), F32), jax.ShapeDtypeStruct((n, d), BF16)],
        compiler_params=_cparams(("parallel",)),
        name="moe_combine_ln",
    )(x, y0, y1, rt, g.reshape(1, d), b.reshape(1, d))


def _moe_layout(rt, n):
    e_flat = rt[:, 0:TOP_K].astype(I32).reshape(-1)
    nk = n * TOP_K
    onehot = (e_flat[:, None] == jnp.arange(N_EXPERTS, dtype=I32)[None, :]).astype(I32)
    rank = jnp.take_along_axis(jnp.cumsum(onehot, axis=0), e_flat[:, None], axis=1)[:, 0] - 1
    counts = jnp.sum(onehot, axis=0)
    padded = (counts + MOE_TM - 1) // MOE_TM * MOE_TM
    pad_end = jnp.cumsum(padded)
    pad_start = pad_end - padded
    grp_start = jnp.cumsum(counts) - counts
    slot = pad_start[e_flat] + rank
    n_blocks = -(-nk // MOE_TM) + N_EXPERTS
    n_slots = n_blocks * MOE_TM
    order = jnp.argsort(e_flat, stable=True).astype(I32)
    sl = jnp.arange(n_slots, dtype=I32)
    slot_e = jnp.minimum(jnp.searchsorted(pad_end, sl, side='right'), N_EXPERTS - 1).astype(I32)
    within = sl - pad_start[slot_e]
    valid = within < counts[slot_e]
    src = jnp.where(valid, grp_start[slot_e] + within, 0)
    slot_tok = jnp.where(valid, order[src] // TOP_K, 0)
    n_used = (pad_end[-1] // MOE_TM).astype(I32).reshape(1)
    blk = jnp.arange(n_blocks, dtype=I32)
    block_e = slot_e[jnp.minimum(blk, n_used[0] - 1) * MOE_TM]
    return slot_tok, slot.reshape(n, TOP_K), block_e, n_used


def _pair_perm(n_heads):
    half = n_heads // 2
    cols = []
    for p in range(half):
        cols += list(range(p * HEAD_DIM, (p + 1) * HEAD_DIM))
        cols += list(range((half + p) * HEAD_DIM, (half + p + 1) * HEAD_DIM))
    return np.asarray(cols, dtype=np.int32)


def _pad_cols(w, width):
    return jnp.pad(w, ((0, 0), (0, width - w.shape[1])))


def _even_layer(x, xb, w_in, w_out, lam_params, subln, lam_init, wg, wu, wd, ln, tabs, bsz, seq_len):
    n, d = x.shape
    perm = _pair_perm(B_HEADS)
    aq, ak, av = w_in[:, 0:512], w_in[:, 512:1024], w_in[:, 1024:1536]
    bq, bk, bv = w_in[:, 1536:2048], w_in[:, 2048:2176], w_in[:, 2176:2304]
    iq, ik, iw = w_in[:, 2304:2560], w_in[:, 2560:2624], w_in[:, 2624:2628]
    w_rope = jnp.concatenate([aq * SCALE, ak, bq[:, perm] * SCALE, iq, bk, ik, ik], axis=1)
    w_rope = w_rope[:, _rope_layout(w_rope.shape[1])].astype(BF16)
    w_val = jnp.concatenate([av, bv], axis=1).astype(BF16)
    w_iw = _pad_cols(iw, LANES).astype(BF16)
    rp = _proj_rope(xb, w_rope, 512, 1024, seq_len, tabs).reshape(bsz, seq_len, -1)
    vt, iwv = _proj_values(xb, w_val, w_iw, None, 512, TQ)
    vt = vt.reshape(bsz, seq_len // TQ, -1, TQ)
    iwv = iwv.reshape(bsz, seq_len, LANES)
    o_a = _diff_attention(rp, vt, lam_params, subln, lam_init)
    o_b = _dsa_attention(rp, vt, iwv)
    wo_a = w_out[0:512].astype(BF16)
    wo_b = w_out[512:1024][perm].astype(BF16)
    g_mix, b_mix, g_ffn, b_ffn = ln
    x1, x1b = _outproj_ln(x, o_a.reshape(n, -1), o_b.reshape(n, -1), wo_a, wo_b, g_mix, b_mix)
    return _ffn_ln(x1, x1b, wg.astype(BF16), wu.astype(BF16), wd.astype(BF16), g_ffn, b_ffn)


def _odd_layer(x, xb, w_in, w_out, gate_b, pe, phi_w1, phi_w2, w_router, b_router, wg, wu, wd, ln, tabs,
               bsz, seq_len):
    n, d = x.shape
    perm = _pair_perm(D_HEADS)
    cq, ck, cv = w_in[:, 0:512], w_in[:, 512:1024], w_in[:, 1024:1536]
    dq = w_in[:, 1536:2048][:, perm] * SCALE
    dkc, dvc, dks = w_in[:, 2048:2176], w_in[:, 2176:2304], w_in[:, 2304:2432]
    dvs, dkw, dvw = w_in[:, 2432:2560], w_in[:, 2560:2688], w_in[:, 2688:2816]
    dg = w_in[:, 2816:2840]
    w_rope = jnp.concatenate([cq * SCALE, ck, dq, dks, dkw], axis=1)
    w_rope = w_rope[:, _rope_layout(w_rope.shape[1])].astype(BF16)
    w_plain = jnp.concatenate([dq, dkc, dvc], axis=1).astype(BF16)
    w_val = jnp.concatenate([cv, dvs, dvw], axis=1).astype(BF16)
    w_dg = _pad_cols(dg, LANES).astype(BF16)
    rp = _proj_rope(xb, w_rope, 512, w_rope.shape[1] // 2, seq_len, tabs).reshape(bsz, seq_len, -1)
    vt, dgv, pp = _proj_values(xb, w_val, w_dg, w_plain, 512, TQ)
    vt = vt.reshape(bsz, seq_len // TQ, -1, TQ)
    dgv = dgv.reshape(bsz, seq_len, LANES)
    pp = pp.reshape(bsz, seq_len, -1)

    o_c = _moba_attention(rp, vt)

    nc = seq_len // NSA_CMP_STRIDE
    tok = pp[:, :, 4 * LANES:6 * LANES].reshape(bsz, nc, NSA_CMP_STRIDE, 4, HEAD_DIM)
    r = tok.transpose(0, 3, 1, 2, 4).reshape(bsz, 4, nc, NSA_CMP_STRIDE * HEAD_DIM)
    pe_flat = jnp.pad(pe.reshape(2, 1, -1), ((0, 0), (0, 7), (0, 0))).astype(BF16)
    cmp = _nsa_compress(r, pe_flat, phi_w1.astype(BF16), phi_w2.astype(BF16))
    kcmp = jnp.concatenate([cmp[:, 0], cmp[:, 1]], axis=-1)
    vcmp = jnp.concatenate([cmp[:, 2], cmp[:, 3]], axis=-1)
    gb = _pad_cols(gate_b.reshape(1, -1), LANES)
    o_d = _nsa_attention(rp, pp, vt, dgv, gb, kcmp, vcmp)

    wo_c = w_out[0:512].astype(BF16)
    wo_d = w_out[512:1024][perm].astype(BF16)
    g_mix, b_mix, g_ffn, b_ffn = ln
    router = (_pad_cols(w_router, LANES), _pad_cols(b_router.reshape(1, -1), LANES))
    x1, x1b, rt = _outproj_ln(x, o_c.reshape(n, -1), o_d.reshape(n, -1), wo_c, wo_d, g_mix, b_mix, router)

    slot_tok, slot, block_e, n_used = _moe_layout(rt, n)
    y_slots = _moe_ffn(x1b[slot_tok], block_e, n_used, wg, wu, wd)
    return _combine_ln(x1, y_slots[slot[:, 0]], y_slots[slot[:, 1]], rt, g_ffn, b_ffn)


@jax.jit
def kernel(x, ev_w_in, ev_w_out, dif_lambda, dif_subln, ffd_w_gate, ffd_w_up, ffd_w_down, od_w_in, od_w_out,
           nsa_gate_b, nsa_pe, nsa_phi_w1, nsa_phi_w2, moe_w_router, moe_b_router, moe_w_gate, moe_w_up,
           moe_w_down, ln_mix_g, ln_mix_b, ln_ffn_g, ln_ffn_b):
    bsz, seq_len, d = x.shape
    tabs = _rope_tables(seq_len)
    xf = x.reshape(bsz * seq_len, d)
    xb = xf.astype(BF16)
    for l in range(DEPTH):
        i = l // 2
        ln = (ln_mix_g[l], ln_mix_b[l], ln_ffn_g[l], ln_ffn_b[l])
        if l % 2 == 0:
            lam_init = 0.8 - 0.6 * math.exp(-0.3 * l)
            xf, xb = _even_layer(xf, xb, ev_w_in[i], ev_w_out[i], dif_lambda[i], dif_subln[i], lam_init,
                                 ffd_w_gate[i], ffd_w_up[i], ffd_w_down[i], ln, tabs, bsz, seq_len)
        else:
            xf, xb = _odd_layer(xf, xb, od_w_in[i], od_w_out[i], nsa_gate_b[i], nsa_pe[i], nsa_phi_w1[i],
                                nsa_phi_w2[i], moe_w_router[i], moe_b_router[i], _layer_to_bf16(moe_w_gate, i),
                                _layer_to_bf16(moe_w_up, i), _layer_to_bf16(moe_w_down, i), ln, tabs, bsz, seq_len)
    return xf.reshape(bsz, seq_len, d)
```

```python
import functools
import math

import numpy as np
import jax
import jax.numpy as jnp
from jax import lax
from jax.experimental import pallas as pl
from jax.experimental.pallas import tpu as pltpu

F32 = jnp.float32
BF16 = jnp.bfloat16
I32 = jnp.int32

LANES = 128
VMEM_LIMIT = 56 * 1024 * 1024

DEPTH = 4
HEAD_DIM = 64
ROPE_THETA = 10000.0
LN_EPS = 1e-5
DN_ALPHA = (2 * DEPTH) ** 0.25
SCALE = HEAD_DIM ** -0.5 * math.log2(math.e)
NEG = -1e30
BIG = 1e30
M_INIT = -1e30
NINF = float("-inf")

A_HEADS = 4
B_HEADS = 8
IDX_HEADS = 4
DSA_TOPK = 256
C_HEADS = 8
MOBA_BLOCK = 256
MOBA_TOPK = 3
D_HEADS = 8
NSA_CMP_LEN = 32
NSA_CMP_STRIDE = 16
NSA_SLC_BLOCK = 64
NSA_SLC_TOPK = 16
NSA_WINDOW = 512
NSA_PHI_HIDDEN = 256
N_EXPERTS = 8
TOP_K = 2
MOE_TM = 512
N_BISECT = 14


def _cparams(sem):
    return pltpu.CompilerParams(dimension_semantics=sem, vmem_limit_bytes=VMEM_LIMIT)


def _dot_nt(a, b):
    return lax.dot_general(a, b, (((1,), (1,)), ((), ())), preferred_element_type=F32)


def _layer_norm(y, g, b):
    mu = jnp.mean(y, axis=-1, keepdims=True)
    yc = y - mu
    var = jnp.mean(yc * yc, axis=-1, keepdims=True)
    return yc * lax.rsqrt(var + LN_EPS) * g + b


def _safe_recip(l):
    return jnp.where(l > 0.0, 1.0 / jnp.where(l > 0.0, l, 1.0), 0.0)


def _split_halves(t, roped=False):
    lane = lax.broadcasted_iota(I32, (1, LANES), 1)
    lo = (lane // (HEAD_DIM // 2)) % 2 == 0 if roped else lane < HEAD_DIM
    z = jnp.zeros_like(t)
    return jnp.where(lo, t, z), jnp.where(lo, z, t)


def _rope_layout(n_cols):
    q = HEAD_DIM // 2
    tile = np.concatenate([np.arange(0, q), np.arange(2 * q, 3 * q), np.arange(q, 2 * q), np.arange(3 * q, 4 * q)])
    return (np.arange(0, n_cols, LANES)[:, None] + tile[None, :]).reshape(-1).astype(np.int32)


def _rope_mm_kernel(x_ref, w_ref, cos_ref, sin_ref, o_ref):
    acc = jnp.dot(x_ref[...].astype(BF16), w_ref[...], preferred_element_type=F32)
    cos = cos_ref[...]
    sin = sin_ref[...]
    for c in range(acc.shape[1] // LANES):
        a = acc[:, c * LANES:(c + 1) * LANES]
        rot = pltpu.roll(a, LANES // 2, 1)
        o_ref[:, c * LANES:(c + 1) * LANES] = (a * cos + rot * sin).astype(o_ref.dtype)


def _proj_rope(x, w, tm, tn, seq_len, rope_tabs):
    n, d = x.shape
    p = w.shape[1]
    nt = seq_len // tm
    return pl.pallas_call(
        _rope_mm_kernel,
        grid=(n // tm, p // tn),
        in_specs=[pl.BlockSpec((tm, d), lambda i, j: (i, 0)), pl.BlockSpec((d, tn), lambda i, j: (0, j)),
                  pl.BlockSpec((tm, LANES), lambda i, j: (i % nt, 0)),
                  pl.BlockSpec((tm, LANES), lambda i, j: (i % nt, 0))],
        out_specs=pl.BlockSpec((tm, tn), lambda i, j: (i, j)),
        out_shape=jax.ShapeDtypeStruct((n, p), BF16),
        compiler_params=_cparams(("parallel", "arbitrary")),
        name="proj_rope",
    )(x, w, *rope_tabs)


def _rope_tables(seq_len):
    d = HEAD_DIM
    inv = ROPE_THETA ** (-jnp.arange(0, d, 2, dtype=F32) / d)
    ang = jnp.arange(seq_len, dtype=I32).astype(F32)[:, None] * inv[None, :]
    cos = jnp.cos(ang)
    sin = jnp.sin(ang)
    cos128 = jnp.tile(cos, (1, LANES // (d // 2)))
    sin128 = jnp.concatenate([-sin, -sin, sin, sin], axis=1)
    return cos128, sin128


def _route_top2(x, w, b):
    xh, wh = x.astype(BF16), w.astype(BF16)
    xl, wl = (x - xh.astype(F32)).astype(BF16), (w - wh.astype(F32)).astype(BF16)
    logits = (jnp.dot(xh, wh, preferred_element_type=F32) + jnp.dot(xh, wl, preferred_element_type=F32)
              + jnp.dot(xl, wh, preferred_element_type=F32)) + b
    lane = lax.broadcasted_iota(I32, (1, LANES), 1)
    lanef = lane.astype(F32)
    v = jnp.where(lane < N_EXPERTS, logits, NINF)
    l0 = jnp.max(v, axis=1, keepdims=True)
    i0 = jnp.min(jnp.where(v == l0, lanef, float(LANES)), axis=1, keepdims=True)
    v = jnp.where(lanef == i0, NINF, v)
    l1 = jnp.max(v, axis=1, keepdims=True)
    i1 = jnp.min(jnp.where(v == l1, lanef, float(LANES)), axis=1, keepdims=True)
    e1 = jnp.exp(l1 - l0)
    g0 = 1.0 / (1.0 + e1)
    g1 = e1 / (1.0 + e1)
    return jnp.where(lane == 0, i0, jnp.where(lane == 1, i1, jnp.where(lane == 2, g0, jnp.where(lane == 3, g1, 0.0))))


def _outproj_ln_kernel(x_ref, a_ref, b_ref, wa_ref, wb_ref, g_ref, bb_ref, *rest):
    mix = (jnp.dot(a_ref[...], wa_ref[...], preferred_element_type=F32)
           + jnp.dot(b_ref[...], wb_ref[...], preferred_element_type=F32))
    y = _layer_norm(DN_ALPHA * x_ref[...] + mix, g_ref[...], bb_ref[...])
    if len(rest) == 2:
        xo_ref, xb_ref = rest
    else:
        wr_ref, br_ref, xo_ref, xb_ref, rt_ref = rest
        rt_ref[...] = _route_top2(y, wr_ref[...], br_ref[...])
    xo_ref[...] = y
    xb_ref[...] = y.astype(BF16)


def _outproj_ln(x, oa, ob, wa, wb, g, b, router=None, tm=512):
    n, d = x.shape
    ka, kb = oa.shape[1], ob.shape[1]
    row = lambda i: (i, 0)
    fixed = lambda i: (0, 0)
    in_specs = [pl.BlockSpec((tm, d), row), pl.BlockSpec((tm, ka), row), pl.BlockSpec((tm, kb), row),
                pl.BlockSpec((ka, d), fixed), pl.BlockSpec((kb, d), fixed),
                pl.BlockSpec((1, d), fixed), pl.BlockSpec((1, d), fixed)]
    out_specs = [pl.BlockSpec((tm, d), row), pl.BlockSpec((tm, d), row)]
    out_shape = [jax.ShapeDtypeStruct((n, d), F32), jax.ShapeDtypeStruct((n, d), BF16)]
    args = [x, oa, ob, wa, wb, g.reshape(1, d), b.reshape(1, d)]
    if router is not None:
        in_specs += [pl.BlockSpec((d, LANES), fixed), pl.BlockSpec((1, LANES), fixed)]
        out_specs.append(pl.BlockSpec((tm, LANES), row))
        out_shape.append(jax.ShapeDtypeStruct((n, LANES), F32))
        args += list(router)
    return pl.pallas_call(
        _outproj_ln_kernel,
        grid=(n // tm,),
        in_specs=in_specs,
        out_specs=out_specs,
        out_shape=out_shape,
        compiler_params=_cparams(("parallel",)),
        name="outproj_ln",
    )(*args)


def _ffn_ln_kernel(x_ref, xb_ref, wg_ref, wu_ref, wd_ref, g_ref, b_ref, xo_ref, xob_ref):
    xb = xb_ref[...]
    h = jax.nn.silu(jnp.dot(xb, wg_ref[...], preferred_element_type=F32)) * jnp.dot(
        xb, wu_ref[...], preferred_element_type=F32)
    ffn = jnp.dot(h.astype(BF16), wd_ref[...], preferred_element_type=F32)
    y = _layer_norm(DN_ALPHA * x_ref[...] + ffn, g_ref[...], b_ref[...])
    xo_ref[...] = y
    xob_ref[...] = y.astype(BF16)


def _ffn_ln(x, xb, wg, wu, wd, g, b, tm=512):
    n, d = x.shape
    fdim = wg.shape[1]
    row = lambda i: (i, 0)
    fixed = lambda i: (0, 0)
    once = pl.Buffered(1)
    return pl.pallas_call(
        _ffn_ln_kernel,
        grid=(n // tm,),
        in_specs=[pl.BlockSpec((tm, d), row), pl.BlockSpec((tm, d), row),
                  pl.BlockSpec((d, fdim), fixed, pipeline_mode=once),
                  pl.BlockSpec((d, fdim), fixed, pipeline_mode=once),
                  pl.BlockSpec((fdim, d), fixed, pipeline_mode=once),
                  pl.BlockSpec((1, d), fixed), pl.BlockSpec((1, d), fixed)],
        out_specs=[pl.BlockSpec((tm, d), row), pl.BlockSpec((tm, d), row)],
        out_shape=[jax.ShapeDtypeStruct((n, d), F32), jax.ShapeDtypeStruct((n, d), BF16)],
        compiler_params=_cparams(("parallel",)),
        name="ffn_ln",
    )(x, xb, wg, wu, wd, g.reshape(1, d), b.reshape(1, d))


def _top_n_mask(v, n, axis):
    idx = lax.broadcasted_iota(I32, v.shape, axis).astype(F32)
    sel = jnp.zeros(v.shape, F32)
    for _ in range(n):
        mx = jnp.max(v, axis=axis, keepdims=True)
        first = jnp.min(jnp.where(v == mx, idx, float(v.shape[axis])), axis=axis, keepdims=True)
        pick = idx == first
        sel = jnp.where(pick, 1.0, sel)
        v = jnp.where(pick, NINF, v)
    return sel


DV_PAD = 16
TQ = 256


def _with_ones(vt):
    return jnp.concatenate([vt, jnp.ones((DV_PAD, vt.shape[1]), vt.dtype)], axis=0)


def _mm_t_kernel(x_ref, wv_ref, ws_ref, *rest, tk):
    x = x_ref[...].astype(BF16)
    vt_ref, small_ref = rest[-2:] if len(rest) == 2 else rest[1:3]
    acc = jnp.dot(x, wv_ref[...], preferred_element_type=F32)
    for cc in range(acc.shape[0] // tk):
        vt_ref[cc] = acc[cc * tk:(cc + 1) * tk, :].T.astype(vt_ref.dtype)
    small_ref[...] = jnp.dot(x, ws_ref[...], preferred_element_type=F32)
    if len(rest) == 4:
        rest[3][...] = jnp.dot(x, rest[0][...], preferred_element_type=F32).astype(rest[3].dtype)


def _proj_values(x, w_val, w_small, w_plain, tm, tk):
    n, d = x.shape
    pv, ps = w_val.shape[1], w_small.shape[1]
    row = lambda i: (i, 0)
    fixed = lambda i: (0, 0)
    in_specs = [pl.BlockSpec((tm, d), row), pl.BlockSpec((d, pv), fixed), pl.BlockSpec((d, ps), fixed)]
    out_specs = [pl.BlockSpec((tm // tk, pv, tk), lambda i: (i, 0, 0)), pl.BlockSpec((tm, ps), row)]
    out_shape = [jax.ShapeDtypeStruct((n // tk, pv, tk), BF16), jax.ShapeDtypeStruct((n, ps), F32)]
    args = [x, w_val, w_small]
    if w_plain is not None:
        pp = w_plain.shape[1]
        in_specs.append(pl.BlockSpec((d, pp), fixed))
        out_specs.append(pl.BlockSpec((tm, pp), row))
        out_shape.append(jax.ShapeDtypeStruct((n, pp), BF16))
        args.append(w_plain)
    return pl.pallas_call(
        functools.partial(_mm_t_kernel, tk=tk),
        grid=(n // tm,),
        in_specs=in_specs,
        out_specs=out_specs,
        out_shape=out_shape,
        compiler_params=_cparams(("parallel",)),
        name="proj_t",
    )(*args)


def _normalize_t(acc, width):
    return acc[:width] * _safe_recip(acc[width:width + 1])


def _pipe_flash(n, ns, qk, vt_at, bias_at, bufs, dv, tk, tq):
    sa, sb, pa, pb = bufs
    for j in range(ns):
        sa[j] = qk(0, j)
        pb[j] = jnp.zeros((tk, tq), BF16)

    def half(c, carry, s_cur, s_nxt, p_prev, p_cur):
        nxt = jnp.minimum(c + 1, n - 1)
        for j in range(ns):
            s_nxt[j] = qk(nxt, j)
        cp = jnp.clip(c - 1, 0, n - 1)
        out = []
        for j in range(ns):
            m, acc, alpha = carry[j]
            acc = alpha * acc + jnp.dot(vt_at(cp, j), p_prev[j], preferred_element_type=F32)
            st = s_cur[j] + bias_at(c, j)
            m_new = jnp.maximum(m, jnp.max(st, axis=0, keepdims=True))
            alpha = jnp.exp2(m - m_new)
            p_cur[j] = jnp.exp2((st - m_new).astype(BF16))
            out.append((m_new, acc, alpha))
        return tuple(out)

    def body(t, carry):
        carry = half(2 * t, carry, sa, sb, pb, pa)
        return half(2 * t + 1, carry, sb, sa, pa, pb)

    init = (jnp.full((1, tq), M_INIT, F32), jnp.zeros((dv, tq), F32), jnp.ones((1, tq), F32))
    carry = lax.fori_loop(0, n // 2, body, (init,) * ns)

    def flush(carry, c_last, p_last):
        return tuple(alpha * acc + jnp.dot(vt_at(c_last, j), p_last[j], preferred_element_type=F32)
                     for j, (_, acc, alpha) in enumerate(carry))

    def odd_tail(carry):
        return flush(half(n - 1, carry, sa, sb, pb, pa), n - 1, pa)

    return lax.cond(n % 2 == 1, odd_tail, lambda carry: flush(carry, n - 1, pb), carry)


def _pipe_scratch(ns, tk, tq):
    return [pltpu.VMEM((ns, tk, tq), F32)] * 2 + [pltpu.VMEM((ns, tk, tq), BF16)] * 2


def _causal_t(t):
    return jnp.where(lax.broadcasted_iota(I32, (t, t), 0) <= lax.broadcasted_iota(I32, (t, t), 1), 0.0, NINF)


def _diff_kernel(lam_ref, sub_ref, q_ref, k_ref, vt_ref, o_ref, *bufs, tq, lam_init):
    i = pl.program_id(2)
    lp = lam_ref[...]
    lam = (jnp.exp(jnp.sum(lp[0:1] * lp[1:2], axis=1, keepdims=True))
           - jnp.exp(jnp.sum(lp[2:3] * lp[3:4], axis=1, keepdims=True)) + lam_init)
    qs = _split_halves(q_ref[...], roped=True)
    n = i + 1
    bufs, tab_ref = bufs[:4], bufs[4]
    tab_ref[0] = jnp.zeros((tq, tq), F32)
    tab_ref[1] = _causal_t(tq)

    def qk(c, j):
        off = pl.multiple_of(c * tq, tq)
        return _dot_nt(k_ref[pl.ds(off, tq), :], qs[j])

    def bias_at(c, j):
        return tab_ref[jnp.where(c == i, 1, 0)]

    outs = _pipe_flash(n, 2, qk, lambda c, j: _with_ones(vt_ref[c]), bias_at, bufs, LANES + DV_PAD, tq, tq)
    o = _normalize_t(outs[0], LANES) - lam * _normalize_t(outs[1], LANES)
    o = o * lax.rsqrt(jnp.mean(o * o, axis=0, keepdims=True) + LN_EPS)
    o = o * sub_ref[...] * (1.0 - lam_init)
    o_ref[...] = o.T.astype(o_ref.dtype)


def _diff_attention(rp, vt, lam_params, subln, lam_init, tq=TQ):
    b, s, _ = rp.shape
    nk = s // tq
    return pl.pallas_call(
        functools.partial(_diff_kernel, tq=tq, lam_init=lam_init),
        grid=(b, A_HEADS, nk),
        in_specs=[pl.BlockSpec((4, HEAD_DIM), lambda bi, h, i: (0, 0)),
                  pl.BlockSpec((LANES, 1), lambda bi, h, i: (0, 0)),
                  pl.BlockSpec((None, tq, LANES), lambda bi, h, i: (bi, i, h)),
                  pl.BlockSpec((None, s, LANES), lambda bi, h, i: (bi, 0, A_HEADS + h)),
                  pl.BlockSpec((None, nk, LANES, tq), lambda bi, h, i: (bi, 0, h, 0))],
        out_specs=pl.BlockSpec((None, tq, LANES), lambda bi, h, i: (bi, i, h)),
        out_shape=jax.ShapeDtypeStruct((b, s, A_HEADS * LANES), BF16),
        scratch_shapes=_pipe_scratch(2, tq, tq) + [pltpu.VMEM((2, tq, tq), F32)],
        compiler_params=_cparams(("parallel", "parallel", "arbitrary")),
        name="diff_attn",
    )(lam_params, subln.reshape(LANES, 1), rp, rp, vt)


def _fold8(x, op):
    acc = x[0:8]
    for r in range(1, x.shape[0] // 8):
        acc = op(acc, x[r * 8:(r + 1) * 8])
    return acc


def _dsa_kernel(iq_ref, ikk_ref, iw_ref, q_ref, k_ref, vt_ref, o_ref, s_ref, j_ref, *bufs, tq, ksel, seq_len):
    tk = tq
    i = pl.program_id(1)
    nch = i + 1
    ksel_f = float(ksel)
    k_loc = lax.broadcasted_iota(I32, (tk, tq), 0)
    q_loc = lax.broadcasted_iota(I32, (tk, tq), 1)
    qpos = i * tq + lax.broadcasted_iota(I32, (1, tq), 1)

    iq = iq_ref[...]
    iwt = iw_ref[...].T
    iqh = []
    for pair in range(IDX_HEADS // 2):
        iqh += list(_split_halves(iq[:, pair * LANES:(pair + 1) * LANES], roped=True))

    def scores(c):
        off = pl.multiple_of(c * tk, tk)
        kk = ikk_ref[pl.ds(off, tk), :]
        sc = iwt[0:1] * jnp.maximum(_dot_nt(kk, iqh[0]), 0.0)
        for h in range(1, IDX_HEADS):
            sc = sc + iwt[h:h + 1] * jnp.maximum(_dot_nt(kk, iqh[h]), 0.0)
        return sc

    def full_body(c, carry):
        mx, mn = carry
        sc = scores(c)
        s_ref[c] = sc
        return jnp.maximum(mx, _fold8(sc, jnp.maximum)), jnp.minimum(mn, _fold8(sc, jnp.minimum))

    mx, mn = lax.fori_loop(0, i, full_body, (jnp.full((8, tq), -BIG, F32), jnp.full((8, tq), BIG, F32)))
    sc = scores(i)
    causal = k_loc <= q_loc
    s_ref[i] = jnp.where(causal, sc, NEG)
    mx = jnp.maximum(mx, _fold8(jnp.where(causal, sc, -BIG), jnp.maximum))
    mn = jnp.minimum(mn, _fold8(jnp.where(causal, sc, BIG), jnp.minimum))
    smax = jnp.max(mx, axis=0, keepdims=True)
    smin = jnp.min(mn, axis=0, keepdims=True)

    def count_where(ind):
        def body(c, acc):
            return acc + _fold8(ind(s_ref[c], c * tk + k_loc), jnp.add)
        acc = lax.fori_loop(0, nch, body, jnp.zeros((8, tq), F32))
        return jnp.sum(acc, axis=0, keepdims=True)

    def count_ge(th):
        return count_where(lambda x, kidx: jnp.where(x >= th, 1.0, 0.0))

    def max_below(th):
        def body(c, acc):
            x = s_ref[c]
            return jnp.maximum(acc, _fold8(jnp.where(x < th, x, NINF), jnp.maximum))
        acc = lax.fori_loop(0, nch, body, jnp.full((8, tq), NINF, F32))
        return jnp.max(acc, axis=0, keepdims=True)

    n_causal = (qpos + 1).astype(F32)
    take_all = n_causal <= ksel_f
    done0 = jnp.where(take_all | (count_ge(smax) >= ksel_f), 1.0, 0.0)

    def bisect(lo, hi):
        mid = lo + (hi - lo) * 0.5
        ge = count_ge(mid) >= ksel_f
        return jnp.where(ge, mid, lo), jnp.where(ge, hi, mid)

    lo, hi = lax.fori_loop(0, N_BISECT, lambda _, c: bisect(*c), (smin, smax))

    def snap_body(carry):
        lo, hi, th, done, _ = carry
        lo, hi = bisect(lo, hi)
        t1 = max_below(hi)
        ok = count_ge(t1) >= ksel_f
        th = jnp.where(done > 0.0, th, t1)
        hi = jnp.where(ok, hi, t1)
        done = jnp.where(ok, 1.0, done)
        return lo, hi, th, done, jnp.sum(1.0 - done)

    _, _, th, _, _ = lax.while_loop(lambda c: c[4] > 0.0, snap_body,
                                    (lo, hi, smax, done0, jnp.sum(1.0 - done0)))

    c_ge = count_ge(th)
    need_tb = jnp.where(take_all, 0.0, jnp.where(c_ge > ksel_f, 1.0, 0.0))
    j_ref[...] = jnp.full((8, tq), seq_len - 1, I32)

    @pl.when(jnp.sum(need_tb) > 0.0)
    def _():
        need = ksel_f - count_where(lambda x, kidx: jnp.where(x > th, 1.0, 0.0))

        def jb(_, carry):
            lo_j, hi_j = carry
            mid = (lo_j + hi_j) // 2
            cnt = count_where(lambda x, kidx: jnp.where(x == th, jnp.where(kidx <= mid, 1.0, 0.0), 0.0))
            ge = cnt >= need
            return jnp.where(ge, lo_j, mid), jnp.where(ge, mid, hi_j)

        n_it = int(math.ceil(math.log2(seq_len))) + 1
        _, hi_j = lax.fori_loop(0, n_it, jb, (jnp.full((1, tq), -1, I32), jnp.full((1, tq), seq_len - 1, I32)))
        j_ref[...] = jnp.broadcast_to(hi_j, (8, tq))

    jsel = j_ref[0:1, :]

    def bias_body(c, _):
        x = s_ref[c]
        kidx = c * tk + k_loc
        keep = jnp.where(x > th, 0.0, jnp.where(x == th, jnp.where(kidx <= jsel, 0.0, NINF), NINF))
        keep = jnp.where(take_all, 0.0, keep)
        s_ref[c] = jnp.where(kidx <= qpos, keep, NINF)
        return 0

    lax.fori_loop(0, nch, bias_body, 0)

    for p in range(B_HEADS // 2):
        qs = _split_halves(q_ref[:, p * LANES:(p + 1) * LANES], roped=True)

        def qk(c, j, qs=qs):
            off = pl.multiple_of(c * tk, tk)
            return _dot_nt(k_ref[pl.ds(off, tk), :], qs[j])

        outs = _pipe_flash(nch, 2, qk, lambda c, j: _with_ones(vt_ref[c, j * HEAD_DIM:(j + 1) * HEAD_DIM, :]),
                           lambda c, j: s_ref[c], bufs, HEAD_DIM + DV_PAD, tk, tq)
        o = jnp.concatenate([_normalize_t(outs[0], HEAD_DIM), _normalize_t(outs[1], HEAD_DIM)], axis=0)
        o_ref[:, p * LANES:(p + 1) * LANES] = o.T.astype(o_ref.dtype)


def _dsa_attention(rp, vt, iw, tq=TQ):
    b, s, _ = rp.shape
    ksel = min(DSA_TOPK, s // 4)
    nk = s // tq
    return pl.pallas_call(
        functools.partial(_dsa_kernel, tq=tq, ksel=ksel, seq_len=s),
        grid=(b, nk),
        in_specs=[pl.BlockSpec((None, tq, 2 * LANES), lambda bi, i: (bi, i, 6)),
                  pl.BlockSpec((None, s, LANES), lambda bi, i: (bi, 0, 15)),
                  pl.BlockSpec((None, tq, LANES), lambda bi, i: (bi, i, 0)),
                  pl.BlockSpec((None, tq, 4 * LANES), lambda bi, i: (bi, i, 2)),
                  pl.BlockSpec((None, s, LANES), lambda bi, i: (bi, 0, 14)),
                  pl.BlockSpec((None, nk, LANES, tq), lambda bi, i: (bi, 0, 4, 0))],
        out_specs=pl.BlockSpec((None, tq, 4 * LANES), lambda bi, i: (bi, i, 0)),
        out_shape=jax.ShapeDtypeStruct((b, s, 4 * LANES), BF16),
        scratch_shapes=[pltpu.VMEM((nk, tq, tq), F32), pltpu.VMEM((8, tq), I32)] + _pipe_scratch(2, tq, tq),
        compiler_params=_cparams(("parallel", "arbitrary")),
        name="dsa_attn",
    )(rp, rp, iw, rp, rp, vt)


def _moba_kernel(q_ref, k_ref, vt_ref, o_ref, km_ref, sel_ref, *bufs, seq_len, n_sel):
    tq = MOBA_BLOCK
    qb = pl.program_id(2)

    @pl.when(qb == 0)
    def _():
        j = lax.broadcasted_iota(I32, (LANES, seq_len), 0)
        s = lax.broadcasted_iota(I32, (LANES, seq_len), 1)
        avg = jnp.where(s // MOBA_BLOCK == j, 1.0 / MOBA_BLOCK, 0.0).astype(BF16)
        km_ref[...] = jnp.dot(avg, k_ref[...], preferred_element_type=F32)

    nbp = sel_ref.shape[1]
    km = km_ref[0:nbp, :]
    qs = _split_halves(q_ref[...], roped=True)
    blk = lax.broadcasted_iota(I32, (nbp, tq), 0)
    past = blk < qb
    for j in range(2):
        gate = lax.dot_general(km, qs[j].astype(F32), (((1,), (1,)), ((), ())),
                               precision=lax.Precision.HIGHEST, preferred_element_type=F32)
        gate = jnp.where(blk < seq_len // MOBA_BLOCK, jnp.where(past, gate, NEG), NINF)
        sel = _top_n_mask(gate, n_sel, 0)
        sel_ref[j] = jnp.where(past, jnp.where(sel > 0.5, 0.0, NINF), NINF)
    own = _causal_t(tq)
    n = qb + 1

    def qk(c, j):
        off = pl.multiple_of(c * tq, tq)
        return _dot_nt(k_ref[pl.ds(off, tq), :], qs[j])

    def bias_at(c, j):
        chosen = sel_ref[j, pl.ds(c, 1), :]
        return jnp.where(c == qb, own, chosen)

    outs = _pipe_flash(n, 2, qk, lambda c, j: _with_ones(vt_ref[c, j * HEAD_DIM:(j + 1) * HEAD_DIM, :]), bias_at, bufs,
                       HEAD_DIM + DV_PAD, tq, tq)
    o = jnp.concatenate([_normalize_t(outs[0], HEAD_DIM), _normalize_t(outs[1], HEAD_DIM)], axis=0)
    o_ref[...] = o.T.astype(o_ref.dtype)


def _moba_attention(rp, vt):
    b, s, _ = rp.shape
    tq = MOBA_BLOCK
    nb = s // tq
    n_sel = max(1, min(MOBA_TOPK, nb - 1))
    npair = C_HEADS // 2
    return pl.pallas_call(
        functools.partial(_moba_kernel, seq_len=s, n_sel=n_sel),
        grid=(b, npair, nb),
        in_specs=[pl.BlockSpec((None, tq, LANES), lambda bi, h, i: (bi, i, h)),
                  pl.BlockSpec((None, s, LANES), lambda bi, h, i: (bi, 0, npair + h)),
                  pl.BlockSpec((None, nb, LANES, tq), lambda bi, h, i: (bi, 0, h, 0))],
        out_specs=pl.BlockSpec((None, tq, LANES), lambda bi, h, i: (bi, i, h)),
        out_shape=jax.ShapeDtypeStruct((b, s, npair * LANES), BF16),
        scratch_shapes=[pltpu.VMEM((LANES, LANES), F32), pltpu.VMEM((2, -(-nb // 8) * 8, tq), F32)]
        + _pipe_scratch(2, tq, tq),
        compiler_params=_cparams(("parallel", "parallel", "arbitrary")),
        name="moba_attn",
    )(rp, rp, vt)


def _cmp_kernel(r_ref, pe_ref, w1_ref, w2_ref, o_ref):
    r = r_ref[...]
    w1 = w1_ref[...]
    half = r.shape[1]
    u = jnp.dot(r, w1[:half], preferred_element_type=F32)
    v = jnp.dot(r, w1[half:], preferred_element_type=F32)
    c = jnp.dot(pe_ref[...], w1, preferred_element_type=F32)[0:1]
    pre = u + pltpu.roll(v, r.shape[0] - 1, 0) + c
    o_ref[...] = jnp.dot(jax.nn.gelu(pre).astype(BF16), w2_ref[...],
                         preferred_element_type=F32).astype(o_ref.dtype)


def _nsa_compress(r, pe, w1, w2):
    b, _, nc, wdt = r.shape
    hid = w1.shape[2]
    return pl.pallas_call(
        _cmp_kernel,
        grid=(b, 4),
        in_specs=[pl.BlockSpec((None, None, nc, wdt), lambda bi, t: (bi, t, 0, 0)),
                  pl.BlockSpec((None, 8, 2 * wdt), lambda bi, t: (t // 2, 0, 0)),
                  pl.BlockSpec((None, 2 * wdt, hid), lambda bi, t: (t // 2, 0, 0)),
                  pl.BlockSpec((None, hid, HEAD_DIM), lambda bi, t: (t // 2, 0, 0))],
        out_specs=pl.BlockSpec((None, None, nc, HEAD_DIM), lambda bi, t: (bi, t, 0, 0)),
        out_shape=jax.ShapeDtypeStruct((b, 4, nc, HEAD_DIM), BF16),
        compiler_params=_cparams(("parallel", "arbitrary")),
        name="nsa_compress",
    )(r, pe, w1, w2)


def _nsa_kernel(qr_ref, qw_ref, dg_ref, gb_ref, kc_ref, vct_ref, ks_ref, vst_ref, kw_ref, vwt_ref,
                o_ref, sel_ref, wb_ref, *bufs, tq, seq_len):
    tk = tq
    i = pl.program_id(1)
    nch = i + 1
    nc = seq_len // NSA_CMP_STRIDE
    n_sb = seq_len // NSA_SLC_BLOCK
    n_sel = min(NSA_SLC_TOPK, n_sb)
    k_loc = lax.broadcasted_iota(I32, (tk, tq), 0)
    q_loc = lax.broadcasted_iota(I32, (tk, tq), 1)
    qpos = i * tq + lax.broadcasted_iota(I32, (1, tq), 1)

    gates_t = jax.nn.sigmoid(dg_ref[...] + gb_ref[...]).T
    kc = kc_ref[...]
    cmp_end = lax.broadcasted_iota(I32, (nc, 1), 0) * NSA_CMP_STRIDE + (NSA_CMP_LEN - 1)
    cbias = jnp.where(cmp_end <= qpos, 0.0, NINF)

    nh = D_HEADS // 2
    q_rot = [_split_halves(qr_ref[:, p * LANES:(p + 1) * LANES], roped=True) for p in range(nh)]
    q_raw = [_split_halves(qw_ref[:, p * LANES:(p + 1) * LANES]) for p in range(nh)]
    cbias4 = jnp.concatenate([cbias] * nh, axis=1)
    o_cmp, psum = [], []
    for g in range(2):
        qg = jnp.concatenate([q_raw[p][g] for p in range(nh)], axis=0)
        s = _dot_nt(kc, qg) + cbias4
        m = jnp.max(s, axis=0, keepdims=True)
        e = jnp.exp2(s - jnp.where(m == NINF, 0.0, m))
        pc = e * _safe_recip(jnp.sum(e, axis=0, keepdims=True))
        psum.append(sum(pc[:, p * tq:(p + 1) * tq] for p in range(nh)))
        o_cmp.append(jnp.dot(vct_ref[g], pc.astype(BF16), preferred_element_type=F32))

    per = tk // NSA_SLC_BLOCK
    nbp = -(-n_sb // 8) * 8
    cn = lax.broadcasted_iota(I32, (nbp, nc), 1) * NSA_CMP_STRIDE
    sj = lax.broadcasted_iota(I32, (nbp, nc), 0) * NSA_SLC_BLOCK
    shares = jnp.where((cn <= sj + NSA_SLC_BLOCK - 1) & (cn + NSA_CMP_LEN - 1 >= sj), 1.0, 0.0)
    blk = lax.broadcasted_iota(I32, (nbp, tq), 0)
    cur = qpos // NSA_SLC_BLOCK
    causal_b = blk <= cur
    forced = (blk == 0) | ((blk >= cur - 1) & causal_b)
    for g in range(2):
        imp = jnp.dot(shares, psum[g], precision=lax.Precision.HIGHEST, preferred_element_type=F32)
        val = jnp.where(forced, BIG, jnp.where(causal_b, imp, NEG))
        val = jnp.where(blk < n_sb, val, NINF)
        rowb = jnp.where(_top_n_mask(val, n_sel, 0) > 0.5, 0.0, NINF)
        for c in range(seq_len // tk):
            sel_ref[g, c] = jnp.concatenate([rowb[c * per:(c + 1) * per], jnp.zeros((8 - per, tq), F32)], axis=0)

    wb_ref[0] = jnp.where(k_loc <= q_loc, 0.0, NINF)
    wb_ref[1] = jnp.zeros((tk, tq), F32)
    wb_ref[2] = jnp.where(k_loc > q_loc, 0.0, NINF)
    n_wc = NSA_WINDOW // tk + 1
    w_first = jnp.maximum(i - (n_wc - 1), 0)
    n_w = i - w_first + 1

    for p in range(D_HEADS // 2):
        qs = q_rot[p]

        def qk_s(c, j, qs=qs):
            off = pl.multiple_of(c * tk, tk)
            return _dot_nt(ks_ref[pl.ds(off, tk), :], qs[j])

        def bias_s(c, j):
            rows = sel_ref[j, c]
            tile = jnp.concatenate([jnp.broadcast_to(rows[r:r + 1], (NSA_SLC_BLOCK, tq)) for r in range(per)], axis=0)
            return tile + wb_ref[jnp.where(c == i, 0, 1)]

        o_slc = _pipe_flash(nch, 2, qk_s, lambda c, j: _with_ones(vst_ref[c, j * HEAD_DIM:(j + 1) * HEAD_DIM, :]),
                            bias_s, bufs, HEAD_DIM + DV_PAD, tk, tq)

        def qk_w(c, j, qs=qs):
            off = pl.multiple_of((w_first + c) * tk, tk)
            return _dot_nt(kw_ref[pl.ds(off, tk), :], qs[j])

        def bias_w(c, j):
            return wb_ref[i - (w_first + c)]

        o_win = _pipe_flash(n_w, 2, qk_w,
                            lambda c, j: _with_ones(vwt_ref[w_first + c, j * HEAD_DIM:(j + 1) * HEAD_DIM, :]), bias_w, bufs,
                            HEAD_DIM + DV_PAD, tk, tq)
        outs = []
        for g in range(2):
            h = g * (D_HEADS // 2) + p
            outs.append(gates_t[3 * h:3 * h + 1] * o_cmp[g][:, p * tq:(p + 1) * tq]
                        + gates_t[3 * h + 1:3 * h + 2] * _normalize_t(o_slc[g], HEAD_DIM)
                        + gates_t[3 * h + 2:3 * h + 3] * _normalize_t(o_win[g], HEAD_DIM))
        o_ref[:, p * LANES:(p + 1) * LANES] = jnp.concatenate(outs, axis=0).T.astype(o_ref.dtype)


def _nsa_attention(rp, pp, vt, dg, gate_b, kcmp, vcmp, tq=TQ):
    b, s, _ = rp.shape
    nk = s // tq
    nc = s // NSA_CMP_STRIDE
    assert NSA_WINDOW == 2 * tq
    n_wc = NSA_WINDOW // tq + 1
    vct = vcmp.reshape(b, nc, 2, HEAD_DIM).transpose(0, 2, 3, 1)
    full = lambda t: pl.BlockSpec((None, s, LANES), lambda bi, i: (bi, 0, t))
    vspec = lambda t: pl.BlockSpec((None, nk, LANES, tq), lambda bi, i: (bi, 0, t, 0))
    return pl.pallas_call(
        functools.partial(_nsa_kernel, tq=tq, seq_len=s),
        grid=(b, nk),
        in_specs=[pl.BlockSpec((None, tq, 4 * LANES), lambda bi, i: (bi, i, 2)),
                  pl.BlockSpec((None, tq, 4 * LANES), lambda bi, i: (bi, i, 0)),
                  pl.BlockSpec((None, tq, LANES), lambda bi, i: (bi, i, 0)),
                  pl.BlockSpec((1, LANES), lambda bi, i: (0, 0)),
                  pl.BlockSpec((None, nc, LANES), lambda bi, i: (bi, 0, 0)),
                  pl.BlockSpec((None, 2, HEAD_DIM, nc), lambda bi, i: (bi, 0, 0, 0)),
                  full(12), vspec(4), full(13), vspec(5)],
        out_specs=pl.BlockSpec((None, tq, 4 * LANES), lambda bi, i: (bi, i, 0)),
        out_shape=jax.ShapeDtypeStruct((b, s, 4 * LANES), BF16),
        scratch_shapes=[pltpu.VMEM((2, nk, 8, tq), F32), pltpu.VMEM((n_wc, tq, tq), F32)]
        + _pipe_scratch(2, tq, tq),
        compiler_params=_cparams(("parallel", "arbitrary")),
        name="nsa_attn",
    )(rp, pp, dg, gate_b, kcmp, vct, rp, vt, rp, vt)


def _cast_kernel(x_ref, o_ref):
    o_ref[...] = x_ref[...].astype(o_ref.dtype)


def _layer_to_bf16(w, layer):
    _, e, r, c = w.shape
    tr = 1 << ((2 ** 21 // c).bit_length() - 1)
    out = pl.pallas_call(
        _cast_kernel,
        grid=(e * r // tr,),
        in_specs=[pl.BlockSpec((None, tr, c), lambda i: (layer, i, 0))],
        out_specs=pl.BlockSpec((tr, c), lambda i: (i, 0)),
        out_shape=jax.ShapeDtypeStruct((e * r, c), BF16),
        compiler_params=_cparams(("parallel",)),
        name="cast_bf16",
    )(w.reshape(w.shape[0], e * r, c))
    return out.reshape(e, r, c)


def _moe_ffn_kernel(be_ref, nu_ref, x_ref, wg_ref, wu_ref, wd_ref, o_ref):
    used = pl.program_id(0) < nu_ref[0]

    @pl.when(used)
    def _():
        x = x_ref[...]
        h = jax.nn.silu(jnp.dot(x, wg_ref[...], preferred_element_type=F32)) * jnp.dot(
            x, wu_ref[...], preferred_element_type=F32)
        o_ref[...] = jnp.dot(h.astype(BF16), wd_ref[...], preferred_element_type=F32).astype(o_ref.dtype)

    @pl.when(jnp.logical_not(used))
    def _():
        o_ref[...] = jnp.zeros(o_ref.shape, o_ref.dtype)


def _moe_ffn(x_sorted, block_e, n_used, wg, wu, wd):
    ns, d = x_sorted.shape
    fdim = wg.shape[2]
    once = pl.Buffered(1)
    grid_spec = pltpu.PrefetchScalarGridSpec(
        num_scalar_prefetch=2,
        grid=(ns // MOE_TM,),
        in_specs=[pl.BlockSpec((MOE_TM, d), lambda i, be, nu: (jnp.minimum(i, nu[0] - 1), 0)),
                  pl.BlockSpec((None, d, fdim), lambda i, be, nu: (be[i], 0, 0), pipeline_mode=once),
                  pl.BlockSpec((None, d, fdim), lambda i, be, nu: (be[i], 0, 0), pipeline_mode=once),
                  pl.BlockSpec((None, fdim, d), lambda i, be, nu: (be[i], 0, 0), pipeline_mode=once)],
        out_specs=pl.BlockSpec((MOE_TM, d), lambda i, be, nu: (i, 0)),
    )
    return pl.pallas_call(
        _moe_ffn_kernel,
        grid_spec=grid_spec,
        out_shape=jax.ShapeDtypeStruct((ns, d), BF16),
        compiler_params=_cparams(("arbitrary",)),
        name="moe_ffn",
    )(block_e, n_used, x_sorted, wg, wu, wd)


def _combine_ln_kernel(x_ref, y0_ref, y1_ref, rt_ref, g_ref, b_ref, xo_ref, xb_ref):
    rt = rt_ref[...]
    ffn = rt[:, 2:3] * y0_ref[...].astype(F32) + rt[:, 3:4] * y1_ref[...].astype(F32)
    y = _layer_norm(DN_ALPHA * x_ref[...] + ffn, g_ref[...], b_ref[...])
    xo_ref[...] = y
    xb_ref[...] = y.astype(BF16)


def _combine_ln(x, y0, y1, rt, g, b, tm=512):
    n, d = x.shape
    row = lambda i: (i, 0)
    fixed = lambda i: (0, 0)
    return pl.pallas_call(
        _combine_ln_kernel,
        grid=(n // tm,),
        in_specs=[pl.BlockSpec((tm, d), row), pl.BlockSpec((tm, d), row), pl.BlockSpec((tm, d), row),
                  pl.BlockSpec((tm, LANES), row), pl.BlockSpec((1, d), fixed), pl.BlockSpec((1, d), fixed)],
        out_specs=[pl.BlockSpec((tm, d), row), pl.BlockSpec((tm, d), row)],
        out_shape=[jax.ShapeDtypeStruct((n, d), F32), jax.ShapeDtypeStruct((n, d), BF16)],
        compiler_params=_cparams(("parallel",)),
        name="moe_combine_ln",
    )(x, y0, y1, rt, g.reshape(1, d), b.reshape(1, d))


def _moe_layout(rt, n):
    e_flat = rt[:, 0:TOP_K].astype(I32).reshape(-1)
    nk = n * TOP_K
    onehot = (e_flat[:, None] == jnp.arange(N_EXPERTS, dtype=I32)[None, :]).astype(I32)
    rank = jnp.take_along_axis(jnp.cumsum(onehot, axis=0), e_flat[:, None], axis=1)[:, 0] - 1
    counts = jnp.sum(onehot, axis=0)
    padded = (counts + MOE_TM - 1) // MOE_TM * MOE_TM
    pad_end = jnp.cumsum(padded)
    pad_start = pad_end - padded
    grp_start = jnp.cumsum(counts) - counts
    slot = pad_start[e_flat] + rank
    n_blocks = -(-nk // MOE_TM) + N_EXPERTS
    n_slots = n_blocks * MOE_TM
    order = jnp.argsort(e_flat, stable=True).astype(I32)
    sl = jnp.arange(n_slots, dtype=I32)
    slot_e = jnp.minimum(jnp.searchsorted(pad_end, sl, side='right'), N_EXPERTS - 1).astype(I32)
    within = sl - pad_start[slot_e]
    valid = within < counts[slot_e]
    src = jnp.where(valid, grp_start[slot_e] + within, 0)
    slot_tok = jnp.where(valid, order[src] // TOP_K, 0)
    n_used = (pad_end[-1] // MOE_TM).astype(I32).reshape(1)
    blk = jnp.arange(n_blocks, dtype=I32)
    block_e = slot_e[jnp.minimum(blk, n_used[0] - 1) * MOE_TM]
    return slot_tok, slot.reshape(n, TOP_K), block_e, n_used


def _pair_perm(n_heads):
    half = n_heads // 2
    cols = []
    for p in range(half):
        cols += list(range(p * HEAD_DIM, (p + 1) * HEAD_DIM))
        cols += list(range((half + p) * HEAD_DIM, (half + p + 1) * HEAD_DIM))
    return np.asarray(cols, dtype=np.int32)


def _pad_cols(w, width):
    return jnp.pad(w, ((0, 0), (0, width - w.shape[1])))


def _even_layer(x, xb, w_in, w_out, lam_params, subln, lam_init, wg, wu, wd, ln, tabs, bsz, seq_len):
    n, d = x.shape
    perm = _pair_perm(B_HEADS)
    aq, ak, av = w_in[:, 0:512], w_in[:, 512:1024], w_in[:, 1024:1536]
    bq, bk, bv = w_in[:, 1536:2048], w_in[:, 2048:2176], w_in[:, 2176:2304]
    iq, ik, iw = w_in[:, 2304:2560], w_in[:, 2560:2624], w_in[:, 2624:2628]
    w_rope = jnp.concatenate([aq * SCALE, ak, bq[:, perm] * SCALE, iq, bk, ik, ik], axis=1)
    w_rope = w_rope[:, _rope_layout(w_rope.shape[1])].astype(BF16)
    w_val = jnp.concatenate([av, bv], axis=1).astype(BF16)
    w_iw = _pad_cols(iw, LANES).astype(BF16)
    rp = _proj_rope(xb, w_rope, 512, 1024, seq_len, tabs).reshape(bsz, seq_len, -1)
    vt, iwv = _proj_values(xb, w_val, w_iw, None, 512, TQ)
    vt = vt.reshape(bsz, seq_len // TQ, -1, TQ)
    iwv = iwv.reshape(bsz, seq_len, LANES)
    o_a = _diff_attention(rp, vt, lam_params, subln, lam_init)
    o_b = _dsa_attention(rp, vt, iwv)
    wo_a = w_out[0:512].astype(BF16)
    wo_b = w_out[512:1024][perm].astype(BF16)
    g_mix, b_mix, g_ffn, b_ffn = ln
    x1, x1b = _outproj_ln(x, o_a.reshape(n, -1), o_b.reshape(n, -1), wo_a, wo_b, g_mix, b_mix)
    return _ffn_ln(x1, x1b, wg.astype(BF16), wu.astype(BF16), wd.astype(BF16), g_ffn, b_ffn)


def _odd_layer(x, xb, w_in, w_out, gate_b, pe, phi_w1, phi_w2, w_router, b_router, wg, wu, wd, ln, tabs,
               bsz, seq_len):
    n, d = x.shape
    perm = _pair_perm(D_HEADS)
    cq, ck, cv = w_in[:, 0:512], w_in[:, 512:1024], w_in[:, 1024:1536]
    dq = w_in[:, 1536:2048][:, perm] * SCALE
    dkc, dvc, dks = w_in[:, 2048:2176], w_in[:, 2176:2304], w_in[:, 2304:2432]
    dvs, dkw, dvw = w_in[:, 2432:2560], w_in[:, 2560:2688], w_in[:, 2688:2816]
    dg = w_in[:, 2816:2840]
    w_rope = jnp.concatenate([cq * SCALE, ck, dq, dks, dkw], axis=1)
    w_rope = w_rope[:, _rope_layout(w_rope.shape[1])].astype(BF16)
    w_plain = jnp.concatenate([dq, dkc, dvc], axis=1).astype(BF16)
    w_val = jnp.concatenate([cv, dvs, dvw], axis=1).astype(BF16)
    w_dg = _pad_cols(dg, LANES).astype(BF16)
    rp = _proj_rope(xb, w_rope, 512, w_rope.shape[1] // 2, seq_len, tabs).reshape(bsz, seq_len, -1)
    vt, dgv, pp = _proj_values(xb, w_val, w_dg, w_plain, 512, TQ)
    vt = vt.reshape(bsz, seq_len // TQ, -1, TQ)
    dgv = dgv.reshape(bsz, seq_len, LANES)
    pp = pp.reshape(bsz, seq_len, -1)

    o_c = _moba_attention(rp, vt)

    nc = seq_len // NSA_CMP_STRIDE
    tok = pp[:, :, 4 * LANES:6 * LANES].reshape(bsz, nc, NSA_CMP_STRIDE, 4, HEAD_DIM)
    r = tok.transpose(0, 3, 1, 2, 4).reshape(bsz, 4, nc, NSA_CMP_STRIDE * HEAD_DIM)
    pe_flat = jnp.pad(pe.reshape(2, 1, -1), ((0, 0), (0, 7), (0, 0))).astype(BF16)
    cmp = _nsa_compress(r, pe_flat, phi_w1.astype(BF16), phi_w2.astype(BF16))
    kcmp = jnp.concatenate([cmp[:, 0], cmp[:, 1]], axis=-1)
    vcmp = jnp.concatenate([cmp[:, 2], cmp[:, 3]], axis=-1)
    gb = _pad_cols(gate_b.reshape(1, -1), LANES)
    o_d = _nsa_attention(rp, pp, vt, dgv, gb, kcmp, vcmp)

    wo_c = w_out[0:512].astype(BF16)
    wo_d = w_out[512:1024][perm].astype(BF16)
    g_mix, b_mix, g_ffn, b_ffn = ln
    router = (_pad_cols(w_router, LANES), _pad_cols(b_router.reshape(1, -1), LANES))
    x1, x1b, rt = _outproj_ln(x, o_c.reshape(n, -1), o_d.reshape(n, -1), wo_c, wo_d, g_mix, b_mix, router)

    slot_tok, slot, block_e, n_used = _moe_layout(rt, n)
    y_slots = _moe_ffn(x1b[slot_tok], block_e, n_used, wg, wu, wd)
    return _combine_ln(x1, y_slots[slot[:, 0]], y_slots[slot[:, 1]], rt, g_ffn, b_ffn)


@jax.jit
def kernel(x, ev_w_in, ev_w_out, dif_lambda, dif_subln, ffd_w_gate, ffd_w_up, ffd_w_down, od_w_in, od_w_out,
           nsa_gate_b, nsa_pe, nsa_phi_w1, nsa_phi_w2, moe_w_router, moe_b_router, moe_w_gate, moe_w_up,
           moe_w_down, ln_mix_g, ln_mix_b, ln_ffn_g, ln_ffn_b):
    bsz, seq_len, d = x.shape
    tabs = _rope_tables(seq_len)
    xf = x.reshape(bsz * seq_len, d)
    xb = xf.astype(BF16)
    for l in range(DEPTH):
        i = l // 2
        ln = (ln_mix_g[l], ln_mix_b[l], ln_ffn_g[l], ln_ffn_b[l])
        if l % 2 == 0:
            lam_init = 0.8 - 0.6 * math.exp(-0.3 * l)
            xf, xb = _even_layer(xf, xb, ev_w_in[i], ev_w_out[i], dif_lambda[i], dif_subln[i], lam_init,
                                 ffd_w_gate[i], ffd_w_up[i], ffd_w_down[i], ln, tabs, bsz, seq_len)
        else:
            xf, xb = _odd_layer(xf, xb, od_w_in[i], od_w_out[i], nsa_gate_b[i], nsa_pe[i], nsa_phi_w1[i],
                                nsa_phi_w2[i], moe_w_router[i], moe_b_router[i], _layer_to_bf16(moe_w_gate, i),
                                _layer_to_bf16(moe_w_up, i), _layer_to_bf16(moe_w_down, i), ln, tabs, bsz, seq_len)
    return xf.reshape(bsz, seq_len, d)
```

```python
import functools
import math

import numpy as np
import jax
import jax.numpy as jnp
from jax import lax
from jax.experimental import pallas as pl
from jax.experimental.pallas import tpu as pltpu

F32 = jnp.float32
BF16 = jnp.bfloat16
I32 = jnp.int32

LANES = 128
VMEM_LIMIT = 56 * 1024 * 1024

DEPTH = 4
HEAD_DIM = 64
ROPE_THETA = 10000.0
LN_EPS = 1e-5
DN_ALPHA = (2 * DEPTH) ** 0.25
SCALE = HEAD_DIM ** -0.5 * math.log2(math.e)
NEG = -1e30
BIG = 1e30
M_INIT = -1e30
NINF = float("-inf")

A_HEADS = 4
B_HEADS = 8
IDX_HEADS = 4
DSA_TOPK = 256
C_HEADS = 8
MOBA_BLOCK = 256
MOBA_TOPK = 3
D_HEADS = 8
NSA_CMP_LEN = 32
NSA_CMP_STRIDE = 16
NSA_SLC_BLOCK = 64
NSA_SLC_TOPK = 16
NSA_WINDOW = 512
N_EXPERTS = 8
TOP_K = 2
MOE_TM = 512
N_BISECT = 14
ROW_TILE = 512
KV_GROUPS = 2

EVEN_SIZES = (A_HEADS * 2 * HEAD_DIM, A_HEADS * 2 * HEAD_DIM, A_HEADS * 2 * HEAD_DIM, B_HEADS * HEAD_DIM,
              KV_GROUPS * HEAD_DIM, KV_GROUPS * HEAD_DIM, IDX_HEADS * HEAD_DIM, HEAD_DIM, IDX_HEADS)
ODD_SIZES = (C_HEADS * HEAD_DIM,) * 3 + (D_HEADS * HEAD_DIM,) + (KV_GROUPS * HEAD_DIM,) * 6 + (D_HEADS * 3,)


def _cparams(sem):
    return pltpu.CompilerParams(dimension_semantics=sem, vmem_limit_bytes=VMEM_LIMIT)


def _dot_nt(a, b):
    return lax.dot_general(a, b, (((1,), (1,)), ((), ())), preferred_element_type=F32)


def _layer_norm(y, g, b):
    mu = jnp.mean(y, axis=-1, keepdims=True)
    yc = y - mu
    var = jnp.mean(yc * yc, axis=-1, keepdims=True)
    return yc * lax.rsqrt(var + LN_EPS) * g + b


def _safe_recip(l):
    return jnp.where(l > 0.0, 1.0 / jnp.where(l > 0.0, l, 1.0), 0.0)


def _split_halves(t, roped=False):
    lane = lax.broadcasted_iota(I32, (1, LANES), 1)
    lo = (lane // (HEAD_DIM // 2)) % 2 == 0 if roped else lane < HEAD_DIM
    z = jnp.zeros_like(t)
    return jnp.where(lo, t, z), jnp.where(lo, z, t)


def _rope_layout(n_cols):
    q = HEAD_DIM // 2
    tile = np.concatenate([np.arange(0, q), np.arange(2 * q, 3 * q), np.arange(q, 2 * q), np.arange(3 * q, 4 * q)])
    return (np.arange(0, n_cols, LANES)[:, None] + tile[None, :]).reshape(-1).astype(np.int32)


def _rope_mm_kernel(x_ref, w_ref, cos_ref, sin_ref, o_ref):
    acc = jnp.dot(x_ref[...].astype(BF16), w_ref[...], preferred_element_type=F32)
    cos = cos_ref[...]
    sin = sin_ref[...]
    for c in range(acc.shape[1] // LANES):
        a = acc[:, c * LANES:(c + 1) * LANES]
        rot = pltpu.roll(a, LANES // 2, 1)
        o_ref[:, c * LANES:(c + 1) * LANES] = (a * cos + rot * sin).astype(o_ref.dtype)


def _proj_rope(x, w, tm, tn, seq_len, rope_tabs):
    n, d = x.shape
    p = w.shape[1]
    nt = seq_len // tm
    return pl.pallas_call(
        _rope_mm_kernel,
        grid=(n // tm, p // tn),
        in_specs=[pl.BlockSpec((tm, d), lambda i, j: (i, 0)), pl.BlockSpec((d, tn), lambda i, j: (0, j)),
                  pl.BlockSpec((tm, LANES), lambda i, j: (i % nt, 0)),
                  pl.BlockSpec((tm, LANES), lambda i, j: (i % nt, 0))],
        out_specs=pl.BlockSpec((tm, tn), lambda i, j: (i, j)),
        out_shape=jax.ShapeDtypeStruct((n, p), BF16),
        compiler_params=_cparams(("parallel", "arbitrary")),
        name="proj_rope",
    )(x, w, *rope_tabs)


def _rope_tables(seq_len):
    d = HEAD_DIM
    inv = ROPE_THETA ** (-jnp.arange(0, d, 2, dtype=F32) / d)
    ang = jnp.arange(seq_len, dtype=I32).astype(F32)[:, None] * inv[None, :]
    cos = jnp.cos(ang)
    sin = jnp.sin(ang)
    cos128 = jnp.tile(cos, (1, LANES // (d // 2)))
    sin128 = jnp.concatenate([-sin, -sin, sin, sin], axis=1)
    return cos128, sin128


def _route_top2(x, w, b):
    xh, wh = x.astype(BF16), w.astype(BF16)
    xl, wl = (x - xh.astype(F32)).astype(BF16), (w - wh.astype(F32)).astype(BF16)
    logits = (jnp.dot(xh, wh, preferred_element_type=F32) + jnp.dot(xh, wl, preferred_element_type=F32)
              + jnp.dot(xl, wh, preferred_element_type=F32)) + b
    lane = lax.broadcasted_iota(I32, (1, LANES), 1)
    lanef = lane.astype(F32)
    v = jnp.where(lane < N_EXPERTS, logits, NINF)
    l0 = jnp.max(v, axis=1, keepdims=True)
    i0 = jnp.min(jnp.where(v == l0, lanef, float(LANES)), axis=1, keepdims=True)
    v = jnp.where(lanef == i0, NINF, v)
    l1 = jnp.max(v, axis=1, keepdims=True)
    i1 = jnp.min(jnp.where(v == l1, lanef, float(LANES)), axis=1, keepdims=True)
    e1 = jnp.exp(l1 - l0)
    g0 = 1.0 / (1.0 + e1)
    g1 = e1 / (1.0 + e1)
    return jnp.where(lane == 0, i0, jnp.where(lane == 1, i1, jnp.where(lane == 2, g0, jnp.where(lane == 3, g1, 0.0))))


def _outproj_ln_kernel(x_ref, a_ref, b_ref, wa_ref, wb_ref, g_ref, bb_ref, *rest):
    mix = (jnp.dot(a_ref[...], wa_ref[...], preferred_element_type=F32)
           + jnp.dot(b_ref[...], wb_ref[...], preferred_element_type=F32))
    y = _layer_norm(DN_ALPHA * x_ref[...] + mix, g_ref[...], bb_ref[...])
    if len(rest) == 2:
        xo_ref, xb_ref = rest
    else:
        wr_ref, br_ref, xo_ref, xb_ref, rt_ref = rest
        rt_ref[...] = _route_top2(y, wr_ref[...], br_ref[...])
    xo_ref[...] = y
    xb_ref[...] = y.astype(BF16)


def _outproj_ln(x, oa, ob, wa, wb, g, b, router=None, tm=ROW_TILE):
    n, d = x.shape
    ka, kb = oa.shape[1], ob.shape[1]
    row = lambda i: (i, 0)
    fixed = lambda i: (0, 0)
    in_specs = [pl.BlockSpec((tm, d), row), pl.BlockSpec((tm, ka), row), pl.BlockSpec((tm, kb), row),
                pl.BlockSpec((ka, d), fixed), pl.BlockSpec((kb, d), fixed),
                pl.BlockSpec((1, d), fixed), pl.BlockSpec((1, d), fixed)]
    out_specs = [pl.BlockSpec((tm, d), row), pl.BlockSpec((tm, d), row)]
    out_shape = [jax.ShapeDtypeStruct((n, d), F32), jax.ShapeDtypeStruct((n, d), BF16)]
    args = [x, oa, ob, wa, wb, g.reshape(1, d), b.reshape(1, d)]
    if router is not None:
        in_specs += [pl.BlockSpec((d, LANES), fixed), pl.BlockSpec((1, LANES), fixed)]
        out_specs.append(pl.BlockSpec((tm, LANES), row))
        out_shape.append(jax.ShapeDtypeStruct((n, LANES), F32))
        args += list(router)
    return pl.pallas_call(
        _outproj_ln_kernel,
        grid=(n // tm,),
        in_specs=in_specs,
        out_specs=out_specs,
        out_shape=out_shape,
        compiler_params=_cparams(("parallel",)),
        name="outproj_ln",
    )(*args)


def _ffn_ln_kernel(x_ref, xb_ref, wg_ref, wu_ref, wd_ref, g_ref, b_ref, xo_ref, xob_ref):
    xb = xb_ref[...]
    h = jax.nn.silu(jnp.dot(xb, wg_ref[...], preferred_element_type=F32)) * jnp.dot(
        xb, wu_ref[...], preferred_element_type=F32)
    ffn = jnp.dot(h.astype(BF16), wd_ref[...], preferred_element_type=F32)
    y = _layer_norm(DN_ALPHA * x_ref[...] + ffn, g_ref[...], b_ref[...])
    xo_ref[...] = y
    xob_ref[...] = y.astype(BF16)


def _ffn_ln(x, xb, wg, wu, wd, g, b, tm=ROW_TILE):
    n, d = x.shape
    fdim = wg.shape[1]
    row = lambda i: (i, 0)
    fixed = lambda i: (0, 0)
    once = pl.Buffered(1)
    return pl.pallas_call(
        _ffn_ln_kernel,
        grid=(n // tm,),
        in_specs=[pl.BlockSpec((tm, d), row), pl.BlockSpec((tm, d), row),
                  pl.BlockSpec((d, fdim), fixed, pipeline_mode=once),
                  pl.BlockSpec((d, fdim), fixed, pipeline_mode=once),
                  pl.BlockSpec((fdim, d), fixed, pipeline_mode=once),
                  pl.BlockSpec((1, d), fixed), pl.BlockSpec((1, d), fixed)],
        out_specs=[pl.BlockSpec((tm, d), row), pl.BlockSpec((tm, d), row)],
        out_shape=[jax.ShapeDtypeStruct((n, d), F32), jax.ShapeDtypeStruct((n, d), BF16)],
        compiler_params=_cparams(("parallel",)),
        name="ffn_ln",
    )(x, xb, wg, wu, wd, g.reshape(1, d), b.reshape(1, d))


def _top_n_mask(v, n, axis):
    idx = lax.broadcasted_iota(I32, v.shape, axis).astype(F32)
    sel = jnp.zeros(v.shape, F32)
    for _ in range(n):
        mx = jnp.max(v, axis=axis, keepdims=True)
        first = jnp.min(jnp.where(v == mx, idx, float(v.shape[axis])), axis=axis, keepdims=True)
        pick = idx == first
        sel = jnp.where(pick, 1.0, sel)
        v = jnp.where(pick, NINF, v)
    return sel


DV_PAD = 16
TQ = 256


def _with_ones(vt):
    return jnp.concatenate([vt, jnp.ones((DV_PAD, vt.shape[1]), vt.dtype)], axis=0)


def _mm_t_kernel(x_ref, wv_ref, ws_ref, *rest, tk):
    x = x_ref[...].astype(BF16)
    vt_ref, small_ref = rest[-2:] if len(rest) == 2 else rest[1:3]
    acc = jnp.dot(x, wv_ref[...], preferred_element_type=F32)
    for cc in range(acc.shape[0] // tk):
        vt_ref[cc] = acc[cc * tk:(cc + 1) * tk, :].T.astype(vt_ref.dtype)
    small_ref[...] = jnp.dot(x, ws_ref[...], preferred_element_type=F32)
    if len(rest) == 4:
        rest[3][...] = jnp.dot(x, rest[0][...], preferred_element_type=F32).astype(rest[3].dtype)


def _proj_values(x, w_val, w_small, w_plain, tm, tk):
    n, d = x.shape
    pv, ps = w_val.shape[1], w_small.shape[1]
    row = lambda i: (i, 0)
    fixed = lambda i: (0, 0)
    in_specs = [pl.BlockSpec((tm, d), row), pl.BlockSpec((d, pv), fixed), pl.BlockSpec((d, ps), fixed)]
    out_specs = [pl.BlockSpec((tm // tk, pv, tk), lambda i: (i, 0, 0)), pl.BlockSpec((tm, ps), row)]
    out_shape = [jax.ShapeDtypeStruct((n // tk, pv, tk), BF16), jax.ShapeDtypeStruct((n, ps), F32)]
    args = [x, w_val, w_small]
    if w_plain is not None:
        pp = w_plain.shape[1]
        in_specs.append(pl.BlockSpec((d, pp), fixed))
        out_specs.append(pl.BlockSpec((tm, pp), row))
        out_shape.append(jax.ShapeDtypeStruct((n, pp), BF16))
        args.append(w_plain)
    return pl.pallas_call(
        functools.partial(_mm_t_kernel, tk=tk),
        grid=(n // tm,),
        in_specs=in_specs,
        out_specs=out_specs,
        out_shape=out_shape,
        compiler_params=_cparams(("parallel",)),
        name="proj_t",
    )(*args)


def _normalize_t(acc, width):
    return acc[:width] * _safe_recip(acc[width:width + 1])


def _pipe_flash(n, ns, qk, vt_at, bias_at, bufs, dv, tk, tq):
    sa, sb, pa, pb = bufs
    for j in range(ns):
        sa[j] = qk(0, j)
        pb[j] = jnp.zeros((tk, tq), BF16)

    def half(c, carry, s_cur, s_nxt, p_prev, p_cur):
        nxt = jnp.minimum(c + 1, n - 1)
        for j in range(ns):
            s_nxt[j] = qk(nxt, j)
        cp = jnp.clip(c - 1, 0, n - 1)
        out = []
        for j in range(ns):
            m, acc, alpha = carry[j]
            acc = alpha * acc + jnp.dot(vt_at(cp, j), p_prev[j], preferred_element_type=F32)
            st = s_cur[j] + bias_at(c, j)
            m_new = jnp.maximum(m, jnp.max(st, axis=0, keepdims=True))
            alpha = jnp.exp2(m - m_new)
            p_cur[j] = jnp.exp2((st - m_new).astype(BF16))
            out.append((m_new, acc, alpha))
        return tuple(out)

    def body(t, carry):
        carry = half(2 * t, carry, sa, sb, pb, pa)
        return half(2 * t + 1, carry, sb, sa, pa, pb)

    init = (jnp.full((1, tq), M_INIT, F32), jnp.zeros((dv, tq), F32), jnp.ones((1, tq), F32))
    carry = lax.fori_loop(0, n // 2, body, (init,) * ns)

    def flush(carry, c_last, p_last):
        return tuple(alpha * acc + jnp.dot(vt_at(c_last, j), p_last[j], preferred_element_type=F32)
                     for j, (_, acc, alpha) in enumerate(carry))

    def odd_tail(carry):
        return flush(half(n - 1, carry, sa, sb, pb, pa), n - 1, pa)

    return lax.cond(n % 2 == 1, odd_tail, lambda carry: flush(carry, n - 1, pb), carry)


def _pipe_scratch(ns, tk, tq):
    return [pltpu.VMEM((ns, tk, tq), F32)] * 2 + [pltpu.VMEM((ns, tk, tq), BF16)] * 2


def _causal_t(t):
    return jnp.where(lax.broadcasted_iota(I32, (t, t), 0) <= lax.broadcasted_iota(I32, (t, t), 1), 0.0, NINF)


def _diff_kernel(lam_ref, sub_ref, q_ref, k_ref, vt_ref, o_ref, *bufs, tq, lam_init):
    i = pl.program_id(2)
    lp = lam_ref[...]
    lam = (jnp.exp(jnp.sum(lp[0:1] * lp[1:2], axis=1, keepdims=True))
           - jnp.exp(jnp.sum(lp[2:3] * lp[3:4], axis=1, keepdims=True)) + lam_init)
    qs = _split_halves(q_ref[...], roped=True)
    n = i + 1
    bufs, tab_ref = bufs[:4], bufs[4]
    tab_ref[0] = jnp.zeros((tq, tq), F32)
    tab_ref[1] = _causal_t(tq)

    def qk(c, j):
        off = pl.multiple_of(c * tq, tq)
        return _dot_nt(k_ref[pl.ds(off, tq), :], qs[j])

    def bias_at(c, j):
        return tab_ref[jnp.where(c == i, 1, 0)]

    outs = _pipe_flash(n, 2, qk, lambda c, j: _with_ones(vt_ref[c]), bias_at, bufs, LANES + DV_PAD, tq, tq)
    o = _normalize_t(outs[0], LANES) - lam * _normalize_t(outs[1], LANES)
    o = o * lax.rsqrt(jnp.mean(o * o, axis=0, keepdims=True) + LN_EPS)
    o = o * sub_ref[...] * (1.0 - lam_init)
    o_ref[...] = o.T.astype(o_ref.dtype)


def _diff_attention(rp, vt, lam_params, subln, lam_init, tq=TQ):
    b, s, _ = rp.shape
    nk = s // tq
    return pl.pallas_call(
        functools.partial(_diff_kernel, tq=tq, lam_init=lam_init),
        grid=(b, A_HEADS, nk),
        in_specs=[pl.BlockSpec((4, HEAD_DIM), lambda bi, h, i: (0, 0)),
                  pl.BlockSpec((LANES, 1), lambda bi, h, i: (0, 0)),
                  pl.BlockSpec((None, tq, LANES), lambda bi, h, i: (bi, i, h)),
                  pl.BlockSpec((None, s, LANES), lambda bi, h, i: (bi, 0, A_HEADS + h)),
                  pl.BlockSpec((None, nk, LANES, tq), lambda bi, h, i: (bi, 0, h, 0))],
        out_specs=pl.BlockSpec((None, tq, LANES), lambda bi, h, i: (bi, i, h)),
        out_shape=jax.ShapeDtypeStruct((b, s, A_HEADS * LANES), BF16),
        scratch_shapes=_pipe_scratch(2, tq, tq) + [pltpu.VMEM((2, tq, tq), F32)],
        compiler_params=_cparams(("parallel", "parallel", "arbitrary")),
        name="diff_attn",
    )(lam_params, subln.reshape(LANES, 1), rp, rp, vt)


def _fold8(x, op):
    acc = x[0:8]
    for r in range(1, x.shape[0] // 8):
        acc = op(acc, x[r * 8:(r + 1) * 8])
    return acc


def _dsa_kernel(iq_ref, ikk_ref, iw_ref, q_ref, k_ref, vt_ref, o_ref, s_ref, j_ref, *bufs, tq, ksel, seq_len):
    tk = tq
    i = pl.program_id(1)
    nch = i + 1
    ksel_f = float(ksel)
    k_loc = lax.broadcasted_iota(I32, (tk, tq), 0)
    q_loc = lax.broadcasted_iota(I32, (tk, tq), 1)
    qpos = i * tq + lax.broadcasted_iota(I32, (1, tq), 1)

    iq = iq_ref[...]
    iwt = iw_ref[...].T
    iqh = []
    for pair in range(IDX_HEADS // 2):
        iqh += list(_split_halves(iq[:, pair * LANES:(pair + 1) * LANES], roped=True))

    def scores(c):
        off = pl.multiple_of(c * tk, tk)
        kk = ikk_ref[pl.ds(off, tk), :]
        sc = iwt[0:1] * jnp.maximum(_dot_nt(kk, iqh[0]), 0.0)
        for h in range(1, IDX_HEADS):
            sc = sc + iwt[h:h + 1] * jnp.maximum(_dot_nt(kk, iqh[h]), 0.0)
        return sc

    def full_body(c, carry):
        mx, mn = carry
        sc = scores(c)
        s_ref[c] = sc
        return jnp.maximum(mx, _fold8(sc, jnp.maximum)), jnp.minimum(mn, _fold8(sc, jnp.minimum))

    mx, mn = lax.fori_loop(0, i, full_body, (jnp.full((8, tq), -BIG, F32), jnp.full((8, tq), BIG, F32)))
    sc = scores(i)
    causal = k_loc <= q_loc
    s_ref[i] = jnp.where(causal, sc, NEG)
    mx = jnp.maximum(mx, _fold8(jnp.where(causal, sc, -BIG), jnp.maximum))
    mn = jnp.minimum(mn, _fold8(jnp.where(causal, sc, BIG), jnp.minimum))
    smax = jnp.max(mx, axis=0, keepdims=True)
    smin = jnp.min(mn, axis=0, keepdims=True)

    def count_where(ind):
        def body(c, acc):
            return acc + _fold8(ind(s_ref[c], c * tk + k_loc), jnp.add)
        acc = lax.fori_loop(0, nch, body, jnp.zeros((8, tq), F32))
        return jnp.sum(acc, axis=0, keepdims=True)

    def count_ge(th):
        return count_where(lambda x, kidx: jnp.where(x >= th, 1.0, 0.0))

    def max_below(th):
        def body(c, acc):
            x = s_ref[c]
            return jnp.maximum(acc, _fold8(jnp.where(x < th, x, NINF), jnp.maximum))
        acc = lax.fori_loop(0, nch, body, jnp.full((8, tq), NINF, F32))
        return jnp.max(acc, axis=0, keepdims=True)

    n_causal = (qpos + 1).astype(F32)
    take_all = n_causal <= ksel_f
    done0 = jnp.where(take_all | (count_ge(smax) >= ksel_f), 1.0, 0.0)

    def bisect(lo, hi):
        mid = lo + (hi - lo) * 0.5
        ge = count_ge(mid) >= ksel_f
        return jnp.where(ge, mid, lo), jnp.where(ge, hi, mid)

    lo, hi = lax.fori_loop(0, N_BISECT, lambda _, c: bisect(*c), (smin, smax))

    def snap_body(carry):
        lo, hi, th, done, _ = carry
        lo, hi = bisect(lo, hi)
        t1 = max_below(hi)
        ok = count_ge(t1) >= ksel_f
        th = jnp.where(done > 0.0, th, t1)
        hi = jnp.where(ok, hi, t1)
        done = jnp.where(ok, 1.0, done)
        return lo, hi, th, done, jnp.sum(1.0 - done)

    _, _, th, _, _ = lax.while_loop(lambda c: c[4] > 0.0, snap_body,
                                    (lo, hi, smax, done0, jnp.sum(1.0 - done0)))

    c_ge = count_ge(th)
    need_tb = jnp.where(take_all, 0.0, jnp.where(c_ge > ksel_f, 1.0, 0.0))
    j_ref[...] = jnp.full((8, tq), seq_len - 1, I32)

    @pl.when(jnp.sum(need_tb) > 0.0)
    def _():
        need = ksel_f - count_where(lambda x, kidx: jnp.where(x > th, 1.0, 0.0))

        def jb(_, carry):
            lo_j, hi_j = carry
            mid = (lo_j + hi_j) // 2
            cnt = count_where(lambda x, kidx: jnp.where(x == th, jnp.where(kidx <= mid, 1.0, 0.0), 0.0))
            ge = cnt >= need
            return jnp.where(ge, lo_j, mid), jnp.where(ge, mid, hi_j)

        n_it = int(math.ceil(math.log2(seq_len))) + 1
        _, hi_j = lax.fori_loop(0, n_it, jb, (jnp.full((1, tq), -1, I32), jnp.full((1, tq), seq_len - 1, I32)))
        j_ref[...] = jnp.broadcast_to(hi_j, (8, tq))

    jsel = j_ref[0:1, :]

    def bias_body(c, _):
        x = s_ref[c]
        kidx = c * tk + k_loc
        keep = jnp.where(x > th, 0.0, jnp.where(x == th, jnp.where(kidx <= jsel, 0.0, NINF), NINF))
        keep = jnp.where(take_all, 0.0, keep)
        s_ref[c] = jnp.where(kidx <= qpos, keep, NINF)
        return 0

    lax.fori_loop(0, nch, bias_body, 0)

    for p in range(B_HEADS // 2):
        qs = _split_halves(q_ref[:, p * LANES:(p + 1) * LANES], roped=True)

        def qk(c, j, qs=qs):
            off = pl.multiple_of(c * tk, tk)
            return _dot_nt(k_ref[pl.ds(off, tk), :], qs[j])

        outs = _pipe_flash(nch, 2, qk, lambda c, j: _with_ones(vt_ref[c, j * HEAD_DIM:(j + 1) * HEAD_DIM, :]),
                           lambda c, j: s_ref[c], bufs, HEAD_DIM + DV_PAD, tk, tq)
        o = jnp.concatenate([_normalize_t(outs[0], HEAD_DIM), _normalize_t(outs[1], HEAD_DIM)], axis=0)
        o_ref[:, p * LANES:(p + 1) * LANES] = o.T.astype(o_ref.dtype)


def _dsa_attention(rp, vt, iw, tq=TQ):
    b, s, _ = rp.shape
    ksel = min(DSA_TOPK, s // 4)
    nk = s // tq
    return pl.pallas_call(
        functools.partial(_dsa_kernel, tq=tq, ksel=ksel, seq_len=s),
        grid=(b, nk),
        in_specs=[pl.BlockSpec((None, tq, 2 * LANES), lambda bi, i: (bi, i, 6)),
                  pl.BlockSpec((None, s, LANES), lambda bi, i: (bi, 0, 15)),
                  pl.BlockSpec((None, tq, LANES), lambda bi, i: (bi, i, 0)),
                  pl.BlockSpec((None, tq, 4 * LANES), lambda bi, i: (bi, i, 2)),
                  pl.BlockSpec((None, s, LANES), lambda bi, i: (bi, 0, 14)),
                  pl.BlockSpec((None, nk, LANES, tq), lambda bi, i: (bi, 0, 4, 0))],
        out_specs=pl.BlockSpec((None, tq, 4 * LANES), lambda bi, i: (bi, i, 0)),
        out_shape=jax.ShapeDtypeStruct((b, s, 4 * LANES), BF16),
        scratch_shapes=[pltpu.VMEM((nk, tq, tq), F32), pltpu.VMEM((8, tq), I32)] + _pipe_scratch(2, tq, tq),
        compiler_params=_cparams(("parallel", "arbitrary")),
        name="dsa_attn",
    )(rp, rp, iw, rp, rp, vt)


def _moba_kernel(q_ref, k_ref, vt_ref, o_ref, km_ref, sel_ref, *bufs, seq_len, n_sel):
    tq = MOBA_BLOCK
    qb = pl.program_id(2)

    @pl.when(qb == 0)
    def _():
        j = lax.broadcasted_iota(I32, (LANES, seq_len), 0)
        s = lax.broadcasted_iota(I32, (LANES, seq_len), 1)
        avg = jnp.where(s // MOBA_BLOCK == j, 1.0 / MOBA_BLOCK, 0.0).astype(BF16)
        km_ref[...] = jnp.dot(avg, k_ref[...], preferred_element_type=F32)

    nbp = sel_ref.shape[1]
    km = km_ref[0:nbp, :]
    qs = _split_halves(q_ref[...], roped=True)
    blk = lax.broadcasted_iota(I32, (nbp, tq), 0)
    past = blk < qb
    for j in range(2):
        gate = lax.dot_general(km, qs[j].astype(F32), (((1,), (1,)), ((), ())),
                               precision=lax.Precision.HIGHEST, preferred_element_type=F32)
        gate = jnp.where(blk < seq_len // MOBA_BLOCK, jnp.where(past, gate, NEG), NINF)
        sel = _top_n_mask(gate, n_sel, 0)
        sel_ref[j] = jnp.where(past, jnp.where(sel > 0.5, 0.0, NINF), NINF)
    own = _causal_t(tq)
    n = qb + 1

    def qk(c, j):
        off = pl.multiple_of(c * tq, tq)
        return _dot_nt(k_ref[pl.ds(off, tq), :], qs[j])

    def bias_at(c, j):
        chosen = sel_ref[j, pl.ds(c, 1), :]
        return jnp.where(c == qb, own, chosen)

    outs = _pipe_flash(n, 2, qk, lambda c, j: _with_ones(vt_ref[c, j * HEAD_DIM:(j + 1) * HEAD_DIM, :]), bias_at, bufs,
                       HEAD_DIM + DV_PAD, tq, tq)
    o = jnp.concatenate([_normalize_t(outs[0], HEAD_DIM), _normalize_t(outs[1], HEAD_DIM)], axis=0)
    o_ref[...] = o.T.astype(o_ref.dtype)


def _moba_attention(rp, vt):
    b, s, _ = rp.shape
    tq = MOBA_BLOCK
    nb = s // tq
    n_sel = max(1, min(MOBA_TOPK, nb - 1))
    npair = C_HEADS // 2
    return pl.pallas_call(
        functools.partial(_moba_kernel, seq_len=s, n_sel=n_sel),
        grid=(b, npair, nb),
        in_specs=[pl.BlockSpec((None, tq, LANES), lambda bi, h, i: (bi, i, h)),
                  pl.BlockSpec((None, s, LANES), lambda bi, h, i: (bi, 0, npair + h)),
                  pl.BlockSpec((None, nb, LANES, tq), lambda bi, h, i: (bi, 0, h, 0))],
        out_specs=pl.BlockSpec((None, tq, LANES), lambda bi, h, i: (bi, i, h)),
        out_shape=jax.ShapeDtypeStruct((b, s, npair * LANES), BF16),
        scratch_shapes=[pltpu.VMEM((LANES, LANES), F32), pltpu.VMEM((2, -(-nb // 8) * 8, tq), F32)]
        + _pipe_scratch(2, tq, tq),
        compiler_params=_cparams(("parallel", "parallel", "arbitrary")),
        name="moba_attn",
    )(rp, rp, vt)


def _cmp_kernel(r_ref, pe_ref, w1_ref, w2_ref, o_ref):
    r = r_ref[...]
    w1 = w1_ref[...]
    half = r.shape[1]
    u = jnp.dot(r, w1[:half], preferred_element_type=F32)
    v = jnp.dot(r, w1[half:], preferred_element_type=F32)
    c = jnp.dot(pe_ref[...], w1, preferred_element_type=F32)[0:1]
    pre = u + pltpu.roll(v, r.shape[0] - 1, 0) + c
    o_ref[...] = jnp.dot(jax.nn.gelu(pre).astype(BF16), w2_ref[...],
                         preferred_element_type=F32).astype(o_ref.dtype)


def _nsa_compress(r, pe, w1, w2):
    b, _, nc, wdt = r.shape
    hid = w1.shape[2]
    return pl.pallas_call(
        _cmp_kernel,
        grid=(b, 4),
        in_specs=[pl.BlockSpec((None, None, nc, wdt), lambda bi, t: (bi, t, 0, 0)),
                  pl.BlockSpec((None, 8, 2 * wdt), lambda bi, t: (t // 2, 0, 0)),
                  pl.BlockSpec((None, 2 * wdt, hid), lambda bi, t: (t // 2, 0, 0)),
                  pl.BlockSpec((None, hid, HEAD_DIM), lambda bi, t: (t // 2, 0, 0))],
        out_specs=pl.BlockSpec((None, None, nc, HEAD_DIM), lambda bi, t: (bi, t, 0, 0)),
        out_shape=jax.ShapeDtypeStruct((b, 4, nc, HEAD_DIM), BF16),
        compiler_params=_cparams(("parallel", "arbitrary")),
        name="nsa_compress",
    )(r, pe, w1, w2)


def _nsa_kernel(qr_ref, qw_ref, dg_ref, gb_ref, kc_ref, vct_ref, ks_ref, vst_ref, kw_ref, vwt_ref,
                o_ref, sel_ref, wb_ref, *bufs, tq, seq_len):
    tk = tq
    i = pl.program_id(1)
    nch = i + 1
    nc = seq_len // NSA_CMP_STRIDE
    n_sb = seq_len // NSA_SLC_BLOCK
    n_sel = min(NSA_SLC_TOPK, n_sb)
    k_loc = lax.broadcasted_iota(I32, (tk, tq), 0)
    q_loc = lax.broadcasted_iota(I32, (tk, tq), 1)
    qpos = i * tq + lax.broadcasted_iota(I32, (1, tq), 1)

    gates_t = jax.nn.sigmoid(dg_ref[...] + gb_ref[...]).T
    kc = kc_ref[...]
    cmp_end = lax.broadcasted_iota(I32, (nc, 1), 0) * NSA_CMP_STRIDE + (NSA_CMP_LEN - 1)
    cbias = jnp.where(cmp_end <= qpos, 0.0, NINF)

    nh = D_HEADS // 2
    q_rot = [_split_halves(qr_ref[:, p * LANES:(p + 1) * LANES], roped=True) for p in range(nh)]
    q_raw = [_split_halves(qw_ref[:, p * LANES:(p + 1) * LANES]) for p in range(nh)]
    cbias4 = jnp.concatenate([cbias] * nh, axis=1)
    o_cmp, psum = [], []
    for g in range(2):
        qg = jnp.concatenate([q_raw[p][g] for p in range(nh)], axis=0)
        s = _dot_nt(kc, qg) + cbias4
        m = jnp.max(s, axis=0, keepdims=True)
        e = jnp.exp2(s - jnp.where(m == NINF, 0.0, m))
        pc = e * _safe_recip(jnp.sum(e, axis=0, keepdims=True))
        psum.append(sum(pc[:, p * tq:(p + 1) * tq] for p in range(nh)))
        o_cmp.append(jnp.dot(vct_ref[g], pc.astype(BF16), preferred_element_type=F32))

    per = tk // NSA_SLC_BLOCK
    nbp = -(-n_sb // 8) * 8
    cn = lax.broadcasted_iota(I32, (nbp, nc), 1) * NSA_CMP_STRIDE
    sj = lax.broadcasted_iota(I32, (nbp, nc), 0) * NSA_SLC_BLOCK
    shares = jnp.where((cn <= sj + NSA_SLC_BLOCK - 1) & (cn + NSA_CMP_LEN - 1 >= sj), 1.0, 0.0)
    blk = lax.broadcasted_iota(I32, (nbp, tq), 0)
    cur = qpos // NSA_SLC_BLOCK
    causal_b = blk <= cur
    forced = (blk == 0) | ((blk >= cur - 1) & causal_b)
    for g in range(2):
        imp = jnp.dot(shares, psum[g], precision=lax.Precision.HIGHEST, preferred_element_type=F32)
        val = jnp.where(forced, BIG, jnp.where(causal_b, imp, NEG))
        val = jnp.where(blk < n_sb, val, NINF)
        rowb = jnp.where(_top_n_mask(val, n_sel, 0) > 0.5, 0.0, NINF)
        for c in range(seq_len // tk):
            sel_ref[g, c] = jnp.concatenate([rowb[c * per:(c + 1) * per], jnp.zeros((8 - per, tq), F32)], axis=0)

    wb_ref[0] = jnp.where(k_loc <= q_loc, 0.0, NINF)
    wb_ref[1] = jnp.zeros((tk, tq), F32)
    wb_ref[2] = jnp.where(k_loc > q_loc, 0.0, NINF)
    n_wc = NSA_WINDOW // tk + 1
    w_first = jnp.maximum(i - (n_wc - 1), 0)
    n_w = i - w_first + 1

    for p in range(D_HEADS // 2):
        qs = q_rot[p]

        def qk_s(c, j, qs=qs):
            off = pl.multiple_of(c * tk, tk)
            return _dot_nt(ks_ref[pl.ds(off, tk), :], qs[j])

        def bias_s(c, j):
            rows = sel_ref[j, c]
            tile = jnp.concatenate([jnp.broadcast_to(rows[r:r + 1], (NSA_SLC_BLOCK, tq)) for r in range(per)], axis=0)
            return tile + wb_ref[jnp.where(c == i, 0, 1)]

        o_slc = _pipe_flash(nch, 2, qk_s, lambda c, j: _with_ones(vst_ref[c, j * HEAD_DIM:(j + 1) * HEAD_DIM, :]),
                            bias_s, bufs, HEAD_DIM + DV_PAD, tk, tq)

        def qk_w(c, j, qs=qs):
            off = pl.multiple_of((w_first + c) * tk, tk)
            return _dot_nt(kw_ref[pl.ds(off, tk), :], qs[j])

        def bias_w(c, j):
            return wb_ref[i - (w_first + c)]

        o_win = _pipe_flash(n_w, 2, qk_w,
                            lambda c, j: _with_ones(vwt_ref[w_first + c, j * HEAD_DIM:(j + 1) * HEAD_DIM, :]), bias_w, bufs,
                            HEAD_DIM + DV_PAD, tk, tq)
        outs = []
        for g in range(2):
            h = g * (D_HEADS // 2) + p
            outs.append(gates_t[3 * h:3 * h + 1] * o_cmp[g][:, p * tq:(p + 1) * tq]
                        + gates_t[3 * h + 1:3 * h + 2] * _normalize_t(o_slc[g], HEAD_DIM)
                        + gates_t[3 * h + 2:3 * h + 3] * _normalize_t(o_win[g], HEAD_DIM))
        o_ref[:, p * LANES:(p + 1) * LANES] = jnp.concatenate(outs, axis=0).T.astype(o_ref.dtype)


def _nsa_attention(rp, pp, vt, dg, gate_b, kcmp, vcmp, tq=TQ):
    b, s, _ = rp.shape
    nk = s // tq
    nc = s // NSA_CMP_STRIDE
    assert NSA_WINDOW == 2 * tq
    n_wc = NSA_WINDOW // tq + 1
    vct = vcmp.reshape(b, nc, 2, HEAD_DIM).transpose(0, 2, 3, 1)
    full = lambda t: pl.BlockSpec((None, s, LANES), lambda bi, i: (bi, 0, t))
    vspec = lambda t: pl.BlockSpec((None, nk, LANES, tq), lambda bi, i: (bi, 0, t, 0))
    return pl.pallas_call(
        functools.partial(_nsa_kernel, tq=tq, seq_len=s),
        grid=(b, nk),
        in_specs=[pl.BlockSpec((None, tq, 4 * LANES), lambda bi, i: (bi, i, 2)),
                  pl.BlockSpec((None, tq, 4 * LANES), lambda bi, i: (bi, i, 0)),
                  pl.BlockSpec((None, tq, LANES), lambda bi, i: (bi, i, 0)),
                  pl.BlockSpec((1, LANES), lambda bi, i: (0, 0)),
                  pl.BlockSpec((None, nc, LANES), lambda bi, i: (bi, 0, 0)),
                  pl.BlockSpec((None, 2, HEAD_DIM, nc), lambda bi, i: (bi, 0, 0, 0)),
                  full(12), vspec(4), full(13), vspec(5)],
        out_specs=pl.BlockSpec((None, tq, 4 * LANES), lambda bi, i: (bi, i, 0)),
        out_shape=jax.ShapeDtypeStruct((b, s, 4 * LANES), BF16),
        scratch_shapes=[pltpu.VMEM((2, nk, 8, tq), F32), pltpu.VMEM((n_wc, tq, tq), F32)]
        + _pipe_scratch(2, tq, tq),
        compiler_params=_cparams(("parallel", "arbitrary")),
        name="nsa_attn",
    )(rp, pp, dg, gate_b, kcmp, vct, rp, vt, rp, vt)


def _cast_kernel(x_ref, o_ref):
    o_ref[...] = x_ref[...].astype(o_ref.dtype)


def _layer_to_bf16(w, layer):
    _, e, r, c = w.shape
    tr = 1 << ((2 ** 21 // c).bit_length() - 1)
    out = pl.pallas_call(
        _cast_kernel,
        grid=(e * r // tr,),
        in_specs=[pl.BlockSpec((None, tr, c), lambda i: (layer, i, 0))],
        out_specs=pl.BlockSpec((tr, c), lambda i: (i, 0)),
        out_shape=jax.ShapeDtypeStruct((e * r, c), BF16),
        compiler_params=_cparams(("parallel",)),
        name="cast_bf16",
    )(w.reshape(w.shape[0], e * r, c))
    return out.reshape(e, r, c)


def _moe_ffn_kernel(be_ref, nu_ref, x_ref, wg_ref, wu_ref, wd_ref, o_ref):
    used = pl.program_id(0) < nu_ref[0]

    @pl.when(used)
    def _():
        x = x_ref[...]
        h = jax.nn.silu(jnp.dot(x, wg_ref[...], preferred_element_type=F32)) * jnp.dot(
            x, wu_ref[...], preferred_element_type=F32)
        o_ref[...] = jnp.dot(h.astype(BF16), wd_ref[...], preferred_element_type=F32).astype(o_ref.dtype)

    @pl.when(jnp.logical_not(used))
    def _():
        o_ref[...] = jnp.zeros(o_ref.shape, o_ref.dtype)


def _moe_ffn(x_sorted, block_e, n_used, wg, wu, wd):
    ns, d = x_sorted.shape
    fdim = wg.shape[2]
    once = pl.Buffered(1)
    grid_spec = pltpu.PrefetchScalarGridSpec(
        num_scalar_prefetch=2,
        grid=(ns // MOE_TM,),
        in_specs=[pl.BlockSpec((MOE_TM, d), lambda i, be, nu: (jnp.minimum(i, nu[0] - 1), 0)),
                  pl.BlockSpec((None, d, fdim), lambda i, be, nu: (be[i], 0, 0), pipeline_mode=once),
                  pl.BlockSpec((None, d, fdim), lambda i, be, nu: (be[i], 0, 0), pipeline_mode=once),
                  pl.BlockSpec((None, fdim, d), lambda i, be, nu: (be[i], 0, 0), pipeline_mode=once)],
        out_specs=pl.BlockSpec((MOE_TM, d), lambda i, be, nu: (i, 0)),
    )
    return pl.pallas_call(
        _moe_ffn_kernel,
        grid_spec=grid_spec,
        out_shape=jax.ShapeDtypeStruct((ns, d), BF16),
        compiler_params=_cparams(("arbitrary",)),
        name="moe_ffn",
    )(block_e, n_used, x_sorted, wg, wu, wd)


def _combine_ln_kernel(x_ref, y0_ref, y1_ref, rt_ref, g_ref, b_ref, xo_ref, xb_ref):
    rt = rt_ref[...]
    ffn = rt[:, 2:3] * y0_ref[...].astype(F32) + rt[:, 3:4] * y1_ref[...].astype(F32)
    y = _layer_norm(DN_ALPHA * x_ref[...] + ffn, g_ref[...], b_ref[...])
    xo_ref[...] = y
    xb_ref[...] = y.astype(BF16)


def _combine_ln(x, y0, y1, rt, g, b, tm=ROW_TILE):
    n, d = x.shape
    row = lambda i: (i, 0)
    fixed = lambda i: (0, 0)
    return pl.pallas_call(
        _combine_ln_kernel,
        grid=(n // tm,),
        in_specs=[pl.BlockSpec((tm, d), row), pl.BlockSpec((tm, d), row), pl.BlockSpec((tm, d), row),
                  pl.BlockSpec((tm, LANES), row), pl.BlockSpec((1, d), fixed), pl.BlockSpec((1, d), fixed)],
        out_specs=[pl.BlockSpec((tm, d), row), pl.BlockSpec((tm, d), row)],
        out_shape=[jax.ShapeDtypeStruct((n, d), F32), jax.ShapeDtypeStruct((n, d), BF16)],
        compiler_params=_cparams(("parallel",)),
        name="moe_combine_ln",
    )(x, y0, y1, rt, g.reshape(1, d), b.reshape(1, d))


def _moe_layout(rt, n):
    e_flat = rt[:, 0:TOP_K].astype(I32).reshape(-1)
    nk = n * TOP_K
    onehot = (e_flat[:, None] == jnp.arange(N_EXPERTS, dtype=I32)[None, :]).astype(I32)
    rank = jnp.take_along_axis(jnp.cumsum(onehot, axis=0), e_flat[:, None], axis=1)[:, 0] - 1
    counts = jnp.sum(onehot, axis=0)
    padded = (counts + MOE_TM - 1) // MOE_TM * MOE_TM
    pad_end = jnp.cumsum(padded)
    pad_start = pad_end - padded
    grp_start = jnp.cumsum(counts) - counts
    slot = pad_start[e_flat] + rank
    n_blocks = -(-nk // MOE_TM) + N_EXPERTS
    n_slots = n_blocks * MOE_TM
    order = jnp.argsort(e_flat, stable=True).astype(I32)
    sl = jnp.arange(n_slots, dtype=I32)
    slot_e = jnp.minimum(jnp.searchsorted(pad_end, sl, side='right'), N_EXPERTS - 1).astype(I32)
    within = sl - pad_start[slot_e]
    valid = within < counts[slot_e]
    src = jnp.where(valid, grp_start[slot_e] + within, 0)
    slot_tok = jnp.where(valid, order[src] // TOP_K, 0)
    n_used = (pad_end[-1] // MOE_TM).astype(I32).reshape(1)
    blk = jnp.arange(n_blocks, dtype=I32)
    block_e = slot_e[jnp.minimum(blk, n_used[0] - 1) * MOE_TM]
    return slot_tok, slot.reshape(n, TOP_K), block_e, n_used


def _pair_perm(n_heads):
    half = n_heads // 2
    cols = []
    for p in range(half):
        cols += list(range(p * HEAD_DIM, (p + 1) * HEAD_DIM))
        cols += list(range((half + p) * HEAD_DIM, (half + p + 1) * HEAD_DIM))
    return np.asarray(cols, dtype=np.int32)


def _pad_cols(w, width):
    return jnp.pad(w, ((0, 0), (0, width - w.shape[1])))


def _split_cols(w, sizes):
    out, off = [], 0
    for sz in sizes:
        out.append(w[:, off:off + sz])
        off += sz
    return out


def _even_layer(x, xb, w_in, w_out, lam_params, subln, lam_init, wg, wu, wd, ln, tabs, bsz, seq_len):
    n = x.shape[0]
    perm = _pair_perm(B_HEADS)
    aq, ak, av, bq, bk, bv, iq, ik, iw = _split_cols(w_in, EVEN_SIZES)
    w_rope = jnp.concatenate([aq * SCALE, ak, bq[:, perm] * SCALE, iq, bk, ik, ik], axis=1)
    w_rope = w_rope[:, _rope_layout(w_rope.shape[1])].astype(BF16)
    w_val = jnp.concatenate([av, bv], axis=1).astype(BF16)
    w_iw = _pad_cols(iw, LANES).astype(BF16)
    rp = _proj_rope(xb, w_rope, ROW_TILE, w_rope.shape[1] // 2, seq_len, tabs).reshape(bsz, seq_len, -1)
    vt, iwv = _proj_values(xb, w_val, w_iw, None, ROW_TILE, TQ)
    vt = vt.reshape(bsz, seq_len // TQ, -1, TQ)
    iwv = iwv.reshape(bsz, seq_len, LANES)
    o_a = _diff_attention(rp, vt, lam_params, subln, lam_init)
    o_b = _dsa_attention(rp, vt, iwv)
    half = w_out.shape[0] // 2
    wo_a = w_out[:half].astype(BF16)
    wo_b = w_out[half:][perm].astype(BF16)
    g_mix, b_mix, g_ffn, b_ffn = ln
    x1, x1b = _outproj_ln(x, o_a.reshape(n, -1), o_b.reshape(n, -1), wo_a, wo_b, g_mix, b_mix)
    return _ffn_ln(x1, x1b, wg.astype(BF16), wu.astype(BF16), wd.astype(BF16), g_ffn, b_ffn)


def _odd_layer(x, xb, w_in, w_out, gate_b, pe, phi_w1, phi_w2, w_router, b_router, wg, wu, wd, ln, tabs,
               bsz, seq_len):
    n = x.shape[0]
    perm = _pair_perm(D_HEADS)
    cq, ck, cv, dq, dkc, dvc, dks, dvs, dkw, dvw, dg = _split_cols(w_in, ODD_SIZES)
    dq = dq[:, perm] * SCALE
    w_rope = jnp.concatenate([cq * SCALE, ck, dq, dks, dkw], axis=1)
    w_rope = w_rope[:, _rope_layout(w_rope.shape[1])].astype(BF16)
    w_plain = jnp.concatenate([dq, dkc, dvc], axis=1).astype(BF16)
    w_val = jnp.concatenate([cv, dvs, dvw], axis=1).astype(BF16)
    w_dg = _pad_cols(dg, LANES).astype(BF16)
    rp = _proj_rope(xb, w_rope, ROW_TILE, w_rope.shape[1] // 2, seq_len, tabs).reshape(bsz, seq_len, -1)
    vt, dgv, pp = _proj_values(xb, w_val, w_dg, w_plain, ROW_TILE, TQ)
    vt = vt.reshape(bsz, seq_len // TQ, -1, TQ)
    dgv = dgv.reshape(bsz, seq_len, LANES)
    pp = pp.reshape(bsz, seq_len, -1)

    o_c = _moba_attention(rp, vt)

    nc = seq_len // NSA_CMP_STRIDE
    tok = pp[:, :, 4 * LANES:6 * LANES].reshape(bsz, nc, NSA_CMP_STRIDE, 4, HEAD_DIM)
    r = tok.transpose(0, 3, 1, 2, 4).reshape(bsz, 4, nc, NSA_CMP_STRIDE * HEAD_DIM)
    pe_flat = jnp.pad(pe.reshape(2, 1, -1), ((0, 0), (0, 7), (0, 0))).astype(BF16)
    cmp = _nsa_compress(r, pe_flat, phi_w1.astype(BF16), phi_w2.astype(BF16))
    kcmp = jnp.concatenate([cmp[:, 0], cmp[:, 1]], axis=-1)
    vcmp = jnp.concatenate([cmp[:, 2], cmp[:, 3]], axis=-1)
    gb = _pad_cols(gate_b.reshape(1, -1), LANES)
    o_d = _nsa_attention(rp, pp, vt, dgv, gb, kcmp, vcmp)

    half = w_out.shape[0] // 2
    wo_c = w_out[:half].astype(BF16)
    wo_d = w_out[half:][perm].astype(BF16)
    g_mix, b_mix, g_ffn, b_ffn = ln
    router = (_pad_cols(w_router, LANES), _pad_cols(b_router.reshape(1, -1), LANES))
    x1, x1b, rt = _outproj_ln(x, o_c.reshape(n, -1), o_d.reshape(n, -1), wo_c, wo_d, g_mix, b_mix, router)

    slot_tok, slot, block_e, n_used = _moe_layout(rt, n)
    y_slots = _moe_ffn(x1b[slot_tok], block_e, n_used, wg, wu, wd)
    return _combine_ln(x1, y_slots[slot[:, 0]], y_slots[slot[:, 1]], rt, g_ffn, b_ffn)


@jax.jit
def kernel(x, ev_w_in, ev_w_out, dif_lambda, dif_subln, ffd_w_gate, ffd_w_up, ffd_w_down, od_w_in, od_w_out,
           nsa_gate_b, nsa_pe, nsa_phi_w1, nsa_phi_w2, moe_w_router, moe_b_router, moe_w_gate, moe_w_up,
           moe_w_down, ln_mix_g, ln_mix_b, ln_ffn_g, ln_ffn_b):
    bsz, seq_len, d = x.shape
    tabs = _rope_tables(seq_len)
    xf = x.reshape(bsz * seq_len, d)
    xb = xf.astype(BF16)
    for l in range(DEPTH):
        i = l // 2
        ln = (ln_mix_g[l], ln_mix_b[l], ln_ffn_g[l], ln_ffn_b[l])
        if l % 2 == 0:
            lam_init = 0.8 - 0.6 * math.exp(-0.3 * l)
            xf, xb = _even_layer(xf, xb, ev_w_in[i], ev_w_out[i], dif_lambda[i], dif_subln[i], lam_init,
                                 ffd_w_gate[i], ffd_w_up[i], ffd_w_down[i], ln, tabs, bsz, seq_len)
        else:
            xf, xb = _odd_layer(xf, xb, od_w_in[i], od_w_out[i], nsa_gate_b[i], nsa_pe[i], nsa_phi_w1[i],
                                nsa_phi_w2[i], moe_w_router[i], moe_b_router[i], _layer_to_bf16(moe_w_gate, i),
                                _layer_to_bf16(moe_w_up, i), _layer_to_bf16(moe_w_down, i), ln, tabs, bsz, seq_len)
    return xf.reshape(bsz, seq_len, d)
```

```python
import functools
import math

import numpy as np
import jax
import jax.numpy as jnp
from jax import lax
from jax.experimental import pallas as pl
from jax.experimental.pallas import tpu as pltpu

F32 = jnp.float32
BF16 = jnp.bfloat16
I32 = jnp.int32

LANES = 128
VMEM_LIMIT = 56 * 1024 * 1024

DEPTH = 4
HEAD_DIM = 64
ROPE_THETA = 10000.0
LN_EPS = 1e-5
DN_ALPHA = (2 * DEPTH) ** 0.25
SCALE = HEAD_DIM ** -0.5 * math.log2(math.e)
NEG = -1e30
BIG = 1e30
M_INIT = -1e30
NINF = float("-inf")

A_HEADS = 4
B_HEADS = 8
IDX_HEADS = 4
DSA_TOPK = 256
C_HEADS = 8
MOBA_BLOCK = 256
MOBA_TOPK = 3
D_HEADS = 8
NSA_CMP_LEN = 32
NSA_CMP_STRIDE = 16
NSA_SLC_BLOCK = 64
NSA_SLC_TOPK = 16
NSA_WINDOW = 512
N_EXPERTS = 8
TOP_K = 2
MOE_TM = 512
N_BISECT = 14
ROW_TILE = 512
KV_GROUPS = 2

EVEN_SIZES = (A_HEADS * 2 * HEAD_DIM, A_HEADS * 2 * HEAD_DIM, A_HEADS * 2 * HEAD_DIM, B_HEADS * HEAD_DIM,
              KV_GROUPS * HEAD_DIM, KV_GROUPS * HEAD_DIM, IDX_HEADS * HEAD_DIM, HEAD_DIM, IDX_HEADS)
ODD_SIZES = (C_HEADS * HEAD_DIM,) * 3 + (D_HEADS * HEAD_DIM,) + (KV_GROUPS * HEAD_DIM,) * 6 + (D_HEADS * 3,)


def _cparams(sem):
    return pltpu.CompilerParams(dimension_semantics=sem, vmem_limit_bytes=VMEM_LIMIT)


def _dot_nt(a, b):
    return lax.dot_general(a, b, (((1,), (1,)), ((), ())), preferred_element_type=F32)


def _layer_norm(y, g, b):
    mu = jnp.mean(y, axis=-1, keepdims=True)
    yc = y - mu
    var = jnp.mean(yc * yc, axis=-1, keepdims=True)
    return yc * lax.rsqrt(var + LN_EPS) * g + b


def _safe_recip(l):
    return jnp.where(l > 0.0, 1.0 / jnp.where(l > 0.0, l, 1.0), 0.0)


def _split_halves(t, roped=False):
    lane = lax.broadcasted_iota(I32, (1, LANES), 1)
    lo = (lane // (HEAD_DIM // 2)) % 2 == 0 if roped else lane < HEAD_DIM
    z = jnp.zeros_like(t)
    return jnp.where(lo, t, z), jnp.where(lo, z, t)


def _rope_layout(n_cols):
    q = HEAD_DIM // 2
    tile = np.concatenate([np.arange(0, q), np.arange(2 * q, 3 * q), np.arange(q, 2 * q), np.arange(3 * q, 4 * q)])
    return (np.arange(0, n_cols, LANES)[:, None] + tile[None, :]).reshape(-1).astype(np.int32)


def _rope_mm_kernel(x_ref, w_ref, cos_ref, sin_ref, o_ref):
    acc = jnp.dot(x_ref[...].astype(BF16), w_ref[...], preferred_element_type=F32)
    cos = cos_ref[...]
    sin = sin_ref[...]
    for c in range(acc.shape[1] // LANES):
        a = acc[:, c * LANES:(c + 1) * LANES]
        rot = pltpu.roll(a, LANES // 2, 1)
        o_ref[:, c * LANES:(c + 1) * LANES] = (a * cos + rot * sin).astype(o_ref.dtype)


def _proj_rope(x, w, tm, tn, seq_len, rope_tabs):
    n, d = x.shape
    p = w.shape[1]
    nt = seq_len // tm
    return pl.pallas_call(
        _rope_mm_kernel,
        grid=(n // tm, p // tn),
        in_specs=[pl.BlockSpec((tm, d), lambda i, j: (i, 0)), pl.BlockSpec((d, tn), lambda i, j: (0, j)),
                  pl.BlockSpec((tm, LANES), lambda i, j: (i % nt, 0)),
                  pl.BlockSpec((tm, LANES), lambda i, j: (i % nt, 0))],
        out_specs=pl.BlockSpec((tm, tn), lambda i, j: (i, j)),
        out_shape=jax.ShapeDtypeStruct((n, p), BF16),
        compiler_params=_cparams(("parallel", "arbitrary")),
        name="proj_rope",
    )(x, w, *rope_tabs)


def _rope_tables(seq_len):
    d = HEAD_DIM
    inv = ROPE_THETA ** (-jnp.arange(0, d, 2, dtype=F32) / d)
    ang = jnp.arange(seq_len, dtype=I32).astype(F32)[:, None] * inv[None, :]
    cos = jnp.cos(ang)
    sin = jnp.sin(ang)
    cos128 = jnp.tile(cos, (1, LANES // (d // 2)))
    sin128 = jnp.concatenate([-sin, -sin, sin, sin], axis=1)
    return cos128, sin128


def _route_top2(x, w, b):
    xh, wh = x.astype(BF16), w.astype(BF16)
    xl, wl = (x - xh.astype(F32)).astype(BF16), (w - wh.astype(F32)).astype(BF16)
    logits = (jnp.dot(xh, wh, preferred_element_type=F32) + jnp.dot(xh, wl, preferred_element_type=F32)
              + jnp.dot(xl, wh, preferred_element_type=F32)) + b
    lane = lax.broadcasted_iota(I32, (1, LANES), 1)
    lanef = lane.astype(F32)
    v = jnp.where(lane < N_EXPERTS, logits, NINF)
    l0 = jnp.max(v, axis=1, keepdims=True)
    i0 = jnp.min(jnp.where(v == l0, lanef, float(LANES)), axis=1, keepdims=True)
    v = jnp.where(lanef == i0, NINF, v)
    l1 = jnp.max(v, axis=1, keepdims=True)
    i1 = jnp.min(jnp.where(v == l1, lanef, float(LANES)), axis=1, keepdims=True)
    e1 = jnp.exp(l1 - l0)
    g0 = 1.0 / (1.0 + e1)
    g1 = e1 / (1.0 + e1)
    return jnp.where(lane == 0, i0, jnp.where(lane == 1, i1, jnp.where(lane == 2, g0, jnp.where(lane == 3, g1, 0.0))))


def _outproj_ln_kernel(x_ref, a_ref, b_ref, wa_ref, wb_ref, g_ref, bb_ref, *rest):
    mix = (jnp.dot(a_ref[...], wa_ref[...], preferred_element_type=F32)
           + jnp.dot(b_ref[...], wb_ref[...], preferred_element_type=F32))
    y = _layer_norm(DN_ALPHA * x_ref[...] + mix, g_ref[...], bb_ref[...])
    if len(rest) == 2:
        xo_ref, xb_ref = rest
    else:
        wr_ref, br_ref, xo_ref, xb_ref, rt_ref = rest
        rt_ref[...] = _route_top2(y, wr_ref[...], br_ref[...])
    xo_ref[...] = y
    xb_ref[...] = y.astype(BF16)


def _outproj_ln(x, oa, ob, wa, wb, g, b, router=None, tm=ROW_TILE):
    n, d = x.shape
    ka, kb = oa.shape[1], ob.shape[1]
    row = lambda i: (i, 0)
    fixed = lambda i: (0, 0)
    in_specs = [pl.BlockSpec((tm, d), row), pl.BlockSpec((tm, ka), row), pl.BlockSpec((tm, kb), row),
                pl.BlockSpec((ka, d), fixed), pl.BlockSpec((kb, d), fixed),
                pl.BlockSpec((1, d), fixed), pl.BlockSpec((1, d), fixed)]
    out_specs = [pl.BlockSpec((tm, d), row), pl.BlockSpec((tm, d), row)]
    out_shape = [jax.ShapeDtypeStruct((n, d), F32), jax.ShapeDtypeStruct((n, d), BF16)]
    args = [x, oa, ob, wa, wb, g.reshape(1, d), b.reshape(1, d)]
    if router is not None:
        in_specs += [pl.BlockSpec((d, LANES), fixed), pl.BlockSpec((1, LANES), fixed)]
        out_specs.append(pl.BlockSpec((tm, LANES), row))
        out_shape.append(jax.ShapeDtypeStruct((n, LANES), F32))
        args += list(router)
    return pl.pallas_call(
        _outproj_ln_kernel,
        grid=(n // tm,),
        in_specs=in_specs,
        out_specs=out_specs,
        out_shape=out_shape,
        compiler_params=_cparams(("parallel",)),
        name="outproj_ln",
    )(*args)


def _ffn_ln_kernel(x_ref, xb_ref, wg_ref, wu_ref, wd_ref, g_ref, b_ref, xo_ref, xob_ref):
    xb = xb_ref[...]
    h = jax.nn.silu(jnp.dot(xb, wg_ref[...], preferred_element_type=F32)) * jnp.dot(
        xb, wu_ref[...], preferred_element_type=F32)
    ffn = jnp.dot(h.astype(BF16), wd_ref[...], preferred_element_type=F32)
    y = _layer_norm(DN_ALPHA * x_ref[...] + ffn, g_ref[...], b_ref[...])
    xo_ref[...] = y
    xob_ref[...] = y.astype(BF16)


def _ffn_ln(x, xb, wg, wu, wd, g, b, tm=ROW_TILE):
    n, d = x.shape
    fdim = wg.shape[1]
    row = lambda i: (i, 0)
    fixed = lambda i: (0, 0)
    once = pl.Buffered(1)
    return pl.pallas_call(
        _ffn_ln_kernel,
        grid=(n // tm,),
        in_specs=[pl.BlockSpec((tm, d), row), pl.BlockSpec((tm, d), row),
                  pl.BlockSpec((d, fdim), fixed, pipeline_mode=once),
                  pl.BlockSpec((d, fdim), fixed, pipeline_mode=once),
                  pl.BlockSpec((fdim, d), fixed, pipeline_mode=once),
                  pl.BlockSpec((1, d), fixed), pl.BlockSpec((1, d), fixed)],
        out_specs=[pl.BlockSpec((tm, d), row), pl.BlockSpec((tm, d), row)],
        out_shape=[jax.ShapeDtypeStruct((n, d), F32), jax.ShapeDtypeStruct((n, d), BF16)],
        compiler_params=_cparams(("parallel",)),
        name="ffn_ln",
    )(x, xb, wg, wu, wd, g.reshape(1, d), b.reshape(1, d))


def _top_n_mask(v, n, axis):
    idx = lax.broadcasted_iota(I32, v.shape, axis).astype(F32)
    sel = jnp.zeros(v.shape, F32)
    for _ in range(n):
        mx = jnp.max(v, axis=axis, keepdims=True)
        first = jnp.min(jnp.where(v == mx, idx, float(v.shape[axis])), axis=axis, keepdims=True)
        pick = idx == first
        sel = jnp.where(pick, 1.0, sel)
        v = jnp.where(pick, NINF, v)
    return sel


DV_PAD = 16
TQ = 256


def _with_ones(vt):
    return jnp.concatenate([vt, jnp.ones((DV_PAD, vt.shape[1]), vt.dtype)], axis=0)


def _mm_t_kernel(x_ref, wv_ref, ws_ref, *rest, tk):
    x = x_ref[...].astype(BF16)
    vt_ref, small_ref = rest[-2:] if len(rest) == 2 else rest[1:3]
    acc = jnp.dot(x, wv_ref[...], preferred_element_type=F32)
    for cc in range(acc.shape[0] // tk):
        vt_ref[cc] = acc[cc * tk:(cc + 1) * tk, :].T.astype(vt_ref.dtype)
    small_ref[...] = jnp.dot(x, ws_ref[...], preferred_element_type=F32)
    if len(rest) == 4:
        rest[3][...] = jnp.dot(x, rest[0][...], preferred_element_type=F32).astype(rest[3].dtype)


def _proj_values(x, w_val, w_small, w_plain, tm, tk):
    n, d = x.shape
    pv, ps = w_val.shape[1], w_small.shape[1]
    row = lambda i: (i, 0)
    fixed = lambda i: (0, 0)
    in_specs = [pl.BlockSpec((tm, d), row), pl.BlockSpec((d, pv), fixed), pl.BlockSpec((d, ps), fixed)]
    out_specs = [pl.BlockSpec((tm // tk, pv, tk), lambda i: (i, 0, 0)), pl.BlockSpec((tm, ps), row)]
    out_shape = [jax.ShapeDtypeStruct((n // tk, pv, tk), BF16), jax.ShapeDtypeStruct((n, ps), F32)]
    args = [x, w_val, w_small]
    if w_plain is not None:
        pp = w_plain.shape[1]
        in_specs.append(pl.BlockSpec((d, pp), fixed))
        out_specs.append(pl.BlockSpec((tm, pp), row))
        out_shape.append(jax.ShapeDtypeStruct((n, pp), BF16))
        args.append(w_plain)
    return pl.pallas_call(
        functools.partial(_mm_t_kernel, tk=tk),
        grid=(n // tm,),
        in_specs=in_specs,
        out_specs=out_specs,
        out_shape=out_shape,
        compiler_params=_cparams(("parallel",)),
        name="proj_t",
    )(*args)


def _normalize_t(acc, width):
    return acc[:width] * _safe_recip(acc[width:width + 1])


def _pipe_flash(n, ns, qk, vt_at, bias_at, bufs, dv, tk, tq):
    sa, sb, pa, pb = bufs
    for j in range(ns):
        sa[j] = qk(0, j)
        pb[j] = jnp.zeros((tk, tq), BF16)

    def half(c, carry, s_cur, s_nxt, p_prev, p_cur):
        nxt = jnp.minimum(c + 1, n - 1)
        for j in range(ns):
            s_nxt[j] = qk(nxt, j)
        cp = jnp.clip(c - 1, 0, n - 1)
        out = []
        for j in range(ns):
            m, acc, alpha = carry[j]
            acc = alpha * acc + jnp.dot(vt_at(cp, j), p_prev[j], preferred_element_type=F32)
            st = s_cur[j] + bias_at(c, j)
            m_new = jnp.maximum(m, jnp.max(st, axis=0, keepdims=True))
            alpha = jnp.exp2(m - m_new)
            p_cur[j] = jnp.exp2((st - m_new).astype(BF16))
            out.append((m_new, acc, alpha))
        return tuple(out)

    def body(t, carry):
        carry = half(2 * t, carry, sa, sb, pb, pa)
        return half(2 * t + 1, carry, sb, sa, pa, pb)

    init = (jnp.full((1, tq), M_INIT, F32), jnp.zeros((dv, tq), F32), jnp.ones((1, tq), F32))
    carry = lax.fori_loop(0, n // 2, body, (init,) * ns)

    def flush(carry, c_last, p_last):
        return tuple(alpha * acc + jnp.dot(vt_at(c_last, j), p_last[j], preferred_element_type=F32)
                     for j, (_, acc, alpha) in enumerate(carry))

    def odd_tail(carry):
        return flush(half(n - 1, carry, sa, sb, pb, pa), n - 1, pa)

    return lax.cond(n % 2 == 1, odd_tail, lambda carry: flush(carry, n - 1, pb), carry)


def _pipe_scratch(ns, tk, tq):
    return [pltpu.VMEM((ns, tk, tq), F32)] * 2 + [pltpu.VMEM((ns, tk, tq), BF16)] * 2


def _causal_t(t):
    return jnp.where(lax.broadcasted_iota(I32, (t, t), 0) <= lax.broadcasted_iota(I32, (t, t), 1), 0.0, NINF)


def _diff_kernel(lam_ref, sub_ref, q_ref, k_ref, vt_ref, o_ref, *bufs, tq, lam_init):
    i = pl.program_id(2)
    lp = lam_ref[...]
    lam = (jnp.exp(jnp.sum(lp[0:1] * lp[1:2], axis=1, keepdims=True))
           - jnp.exp(jnp.sum(lp[2:3] * lp[3:4], axis=1, keepdims=True)) + lam_init)
    qs = _split_halves(q_ref[...], roped=True)
    n = i + 1
    bufs, tab_ref = bufs[:4], bufs[4]
    tab_ref[0] = jnp.zeros((tq, tq), F32)
    tab_ref[1] = _causal_t(tq)

    def qk(c, j):
        off = pl.multiple_of(c * tq, tq)
        return _dot_nt(k_ref[pl.ds(off, tq), :], qs[j])

    def bias_at(c, j):
        return tab_ref[jnp.where(c == i, 1, 0)]

    outs = _pipe_flash(n, 2, qk, lambda c, j: _with_ones(vt_ref[c]), bias_at, bufs, LANES + DV_PAD, tq, tq)
    o = _normalize_t(outs[0], LANES) - lam * _normalize_t(outs[1], LANES)
    o = o * lax.rsqrt(jnp.mean(o * o, axis=0, keepdims=True) + LN_EPS)
    o = o * sub_ref[...] * (1.0 - lam_init)
    o_ref[...] = o.T.astype(o_ref.dtype)


def _diff_attention(rp, vt, lam_params, subln, lam_init, tq=TQ):
    b, s, _ = rp.shape
    nk = s // tq
    return pl.pallas_call(
        functools.partial(_diff_kernel, tq=tq, lam_init=lam_init),
        grid=(b, A_HEADS, nk),
        in_specs=[pl.BlockSpec((4, HEAD_DIM), lambda bi, h, i: (0, 0)),
                  pl.BlockSpec((LANES, 1), lambda bi, h, i: (0, 0)),
                  pl.BlockSpec((None, tq, LANES), lambda bi, h, i: (bi, i, h)),
                  pl.BlockSpec((None, s, LANES), lambda bi, h, i: (bi, 0, A_HEADS + h)),
                  pl.BlockSpec((None, nk, LANES, tq), lambda bi, h, i: (bi, 0, h, 0))],
        out_specs=pl.BlockSpec((None, tq, LANES), lambda bi, h, i: (bi, i, h)),
        out_shape=jax.ShapeDtypeStruct((b, s, A_HEADS * LANES), BF16),
        scratch_shapes=_pipe_scratch(2, tq, tq) + [pltpu.VMEM((2, tq, tq), F32)],
        compiler_params=_cparams(("parallel", "parallel", "arbitrary")),
        name="diff_attn",
    )(lam_params, subln.reshape(LANES, 1), rp, rp, vt)


def _fold8(x, op):
    acc = x[0:8]
    for r in range(1, x.shape[0] // 8):
        acc = op(acc, x[r * 8:(r + 1) * 8])
    return acc


def _dsa_kernel(iq_ref, ikk_ref, iw_ref, q_ref, k_ref, vt_ref, o_ref, s_ref, j_ref, *bufs, tq, ksel, seq_len):
    tk = tq
    i = pl.program_id(1)
    nch = i + 1
    ksel_f = float(ksel)
    k_loc = lax.broadcasted_iota(I32, (tk, tq), 0)
    q_loc = lax.broadcasted_iota(I32, (tk, tq), 1)
    qpos = i * tq + lax.broadcasted_iota(I32, (1, tq), 1)

    iq = iq_ref[...]
    iwt = iw_ref[...].T
    iqh = []
    for pair in range(IDX_HEADS // 2):
        iqh += list(_split_halves(iq[:, pair * LANES:(pair + 1) * LANES], roped=True))

    def scores(c):
        off = pl.multiple_of(c * tk, tk)
        kk = ikk_ref[pl.ds(off, tk), :]
        sc = iwt[0:1] * jnp.maximum(_dot_nt(kk, iqh[0]), 0.0)
        for h in range(1, IDX_HEADS):
            sc = sc + iwt[h:h + 1] * jnp.maximum(_dot_nt(kk, iqh[h]), 0.0)
        return sc

    def full_body(c, carry):
        mx, mn = carry
        sc = scores(c)
        s_ref[c] = sc
        return jnp.maximum(mx, _fold8(sc, jnp.maximum)), jnp.minimum(mn, _fold8(sc, jnp.minimum))

    mx, mn = lax.fori_loop(0, i, full_body, (jnp.full((8, tq), -BIG, F32), jnp.full((8, tq), BIG, F32)))
    sc = scores(i)
    causal = k_loc <= q_loc
    s_ref[i] = jnp.where(causal, sc, NEG)
    mx = jnp.maximum(mx, _fold8(jnp.where(causal, sc, -BIG), jnp.maximum))
    mn = jnp.minimum(mn, _fold8(jnp.where(causal, sc, BIG), jnp.minimum))
    smax = jnp.max(mx, axis=0, keepdims=True)
    smin = jnp.min(mn, axis=0, keepdims=True)

    def count_where(ind):
        def body(c, acc):
            return acc + _fold8(ind(s_ref[c], c * tk + k_loc), jnp.add)
        acc = lax.fori_loop(0, nch, body, jnp.zeros((8, tq), F32))
        return jnp.sum(acc, axis=0, keepdims=True)

    def count_ge(th):
        return count_where(lambda x, kidx: jnp.where(x >= th, 1.0, 0.0))

    def max_below(th):
        def body(c, acc):
            x = s_ref[c]
            return jnp.maximum(acc, _fold8(jnp.where(x < th, x, NINF), jnp.maximum))
        acc = lax.fori_loop(0, nch, body, jnp.full((8, tq), NINF, F32))
        return jnp.max(acc, axis=0, keepdims=True)

    n_causal = (qpos + 1).astype(F32)
    take_all = n_causal <= ksel_f
    done0 = jnp.where(take_all, 1.0, 0.0)
    hi0 = smax + (jnp.abs(smax) * 2.0 ** -20 + 1e-30)

    def bisect(lo, hi):
        mid = lo + (hi - lo) * 0.5
        ge = count_ge(mid) >= ksel_f
        return jnp.where(ge, mid, lo), jnp.where(ge, hi, mid)

    lo, hi = lax.fori_loop(0, N_BISECT, lambda _, c: bisect(*c), (smin, hi0))

    def snap_body(carry):
        lo, hi, th, c_th, done, _ = carry
        lo, hi = bisect(lo, hi)
        t1 = max_below(hi)
        c1 = count_ge(t1)
        ok = c1 >= ksel_f
        th = jnp.where(done > 0.0, th, t1)
        c_th = jnp.where(done > 0.0, c_th, c1)
        hi = jnp.where(ok, hi, t1)
        done = jnp.where(ok, 1.0, done)
        return lo, hi, th, c_th, done, jnp.sum(1.0 - done)

    _, _, th, c_ge, _, _ = lax.while_loop(lambda c: c[5] > 0.0, snap_body,
                                          (lo, hi, smax, jnp.zeros((1, tq), F32), done0, jnp.sum(1.0 - done0)))

    need_tb = jnp.where(take_all, 0.0, jnp.where(c_ge > ksel_f, 1.0, 0.0))
    j_ref[...] = jnp.full((8, tq), seq_len - 1, I32)

    @pl.when(jnp.sum(need_tb) > 0.0)
    def _():
        need = ksel_f - count_where(lambda x, kidx: jnp.where(x > th, 1.0, 0.0))

        def jb(_, carry):
            lo_j, hi_j = carry
            mid = (lo_j + hi_j) // 2
            cnt = count_where(lambda x, kidx: jnp.where(x == th, jnp.where(kidx <= mid, 1.0, 0.0), 0.0))
            ge = cnt >= need
            return jnp.where(ge, lo_j, mid), jnp.where(ge, mid, hi_j)

        n_it = int(math.ceil(math.log2(seq_len))) + 1
        _, hi_j = lax.fori_loop(0, n_it, jb, (jnp.full((1, tq), -1, I32), jnp.full((1, tq), seq_len - 1, I32)))
        j_ref[...] = jnp.broadcast_to(hi_j, (8, tq))

    jsel = j_ref[0:1, :]

    def bias_body(c, _):
        x = s_ref[c]
        kidx = c * tk + k_loc
        keep = jnp.where(x > th, 0.0, jnp.where(x == th, jnp.where(kidx <= jsel, 0.0, NINF), NINF))
        keep = jnp.where(take_all, 0.0, keep)
        s_ref[c] = jnp.where(kidx <= qpos, keep, NINF)
        return 0

    lax.fori_loop(0, nch, bias_body, 0)

    for p in range(B_HEADS // 2):
        qs = _split_halves(q_ref[:, p * LANES:(p + 1) * LANES], roped=True)

        def qk(c, j, qs=qs):
            off = pl.multiple_of(c * tk, tk)
            return _dot_nt(k_ref[pl.ds(off, tk), :], qs[j])

        outs = _pipe_flash(nch, 2, qk, lambda c, j: _with_ones(vt_ref[c, j * HEAD_DIM:(j + 1) * HEAD_DIM, :]),
                           lambda c, j: s_ref[c], bufs, HEAD_DIM + DV_PAD, tk, tq)
        o = jnp.concatenate([_normalize_t(outs[0], HEAD_DIM), _normalize_t(outs[1], HEAD_DIM)], axis=0)
        o_ref[:, p * LANES:(p + 1) * LANES] = o.T.astype(o_ref.dtype)


def _dsa_attention(rp, vt, iw, tq=TQ):
    b, s, _ = rp.shape
    ksel = min(DSA_TOPK, s // 4)
    nk = s // tq
    return pl.pallas_call(
        functools.partial(_dsa_kernel, tq=tq, ksel=ksel, seq_len=s),
        grid=(b, nk),
        in_specs=[pl.BlockSpec((None, tq, 2 * LANES), lambda bi, i: (bi, i, 6)),
                  pl.BlockSpec((None, s, LANES), lambda bi, i: (bi, 0, 15)),
                  pl.BlockSpec((None, tq, LANES), lambda bi, i: (bi, i, 0)),
                  pl.BlockSpec((None, tq, 4 * LANES), lambda bi, i: (bi, i, 2)),
                  pl.BlockSpec((None, s, LANES), lambda bi, i: (bi, 0, 14)),
                  pl.BlockSpec((None, nk, LANES, tq), lambda bi, i: (bi, 0, 4, 0))],
        out_specs=pl.BlockSpec((None, tq, 4 * LANES), lambda bi, i: (bi, i, 0)),
        out_shape=jax.ShapeDtypeStruct((b, s, 4 * LANES), BF16),
        scratch_shapes=[pltpu.VMEM((nk, tq, tq), F32), pltpu.VMEM((8, tq), I32)] + _pipe_scratch(2, tq, tq),
        compiler_params=_cparams(("parallel", "arbitrary")),
        name="dsa_attn",
    )(rp, rp, iw, rp, rp, vt)


def _moba_kernel(q_ref, k_ref, vt_ref, o_ref, km_ref, sel_ref, *bufs, seq_len, n_sel):
    tq = MOBA_BLOCK
    qb = pl.program_id(2)

    @pl.when(qb == 0)
    def _():
        j = lax.broadcasted_iota(I32, (LANES, seq_len), 0)
        s = lax.broadcasted_iota(I32, (LANES, seq_len), 1)
        avg = jnp.where(s // MOBA_BLOCK == j, 1.0 / MOBA_BLOCK, 0.0).astype(BF16)
        km_ref[...] = jnp.dot(avg, k_ref[...], preferred_element_type=F32)

    nbp = sel_ref.shape[1]
    km = km_ref[0:nbp, :]
    qs = _split_halves(q_ref[...], roped=True)
    blk = lax.broadcasted_iota(I32, (nbp, tq), 0)
    past = blk < qb
    for j in range(2):
        gate = lax.dot_general(km, qs[j].astype(F32), (((1,), (1,)), ((), ())),
                               precision=lax.Precision.HIGHEST, preferred_element_type=F32)
        gate = jnp.where(blk < seq_len // MOBA_BLOCK, jnp.where(past, gate, NEG), NINF)
        sel = _top_n_mask(gate, n_sel, 0)
        sel_ref[j] = jnp.where(past, jnp.where(sel > 0.5, 0.0, NINF), NINF)
    own = _causal_t(tq)
    n = qb + 1

    def qk(c, j):
        off = pl.multiple_of(c * tq, tq)
        return _dot_nt(k_ref[pl.ds(off, tq), :], qs[j])

    def bias_at(c, j):
        chosen = sel_ref[j, pl.ds(c, 1), :]
        return jnp.where(c == qb, own, chosen)

    outs = _pipe_flash(n, 2, qk, lambda c, j: _with_ones(vt_ref[c, j * HEAD_DIM:(j + 1) * HEAD_DIM, :]), bias_at, bufs,
                       HEAD_DIM + DV_PAD, tq, tq)
    o = jnp.concatenate([_normalize_t(outs[0], HEAD_DIM), _normalize_t(outs[1], HEAD_DIM)], axis=0)
    o_ref[...] = o.T.astype(o_ref.dtype)


def _moba_attention(rp, vt):
    b, s, _ = rp.shape
    tq = MOBA_BLOCK
    nb = s // tq
    n_sel = max(1, min(MOBA_TOPK, nb - 1))
    npair = C_HEADS // 2
    return pl.pallas_call(
        functools.partial(_moba_kernel, seq_len=s, n_sel=n_sel),
        grid=(b, npair, nb),
        in_specs=[pl.BlockSpec((None, tq, LANES), lambda bi, h, i: (bi, i, h)),
                  pl.BlockSpec((None, s, LANES), lambda bi, h, i: (bi, 0, npair + h)),
                  pl.BlockSpec((None, nb, LANES, tq), lambda bi, h, i: (bi, 0, h, 0))],
        out_specs=pl.BlockSpec((None, tq, LANES), lambda bi, h, i: (bi, i, h)),
        out_shape=jax.ShapeDtypeStruct((b, s, npair * LANES), BF16),
        scratch_shapes=[pltpu.VMEM((LANES, LANES), F32), pltpu.VMEM((2, -(-nb // 8) * 8, tq), F32)]
        + _pipe_scratch(2, tq, tq),
        compiler_params=_cparams(("parallel", "parallel", "arbitrary")),
        name="moba_attn",
    )(rp, rp, vt)


def _cmp_kernel(r_ref, pe_ref, w1_ref, w2_ref, o_ref):
    r = r_ref[...]
    w1 = w1_ref[...]
    half = r.shape[1]
    u = jnp.dot(r, w1[:half], preferred_element_type=F32)
    v = jnp.dot(r, w1[half:], preferred_element_type=F32)
    c = jnp.dot(pe_ref[...], w1, preferred_element_type=F32)[0:1]
    pre = u + pltpu.roll(v, r.shape[0] - 1, 0) + c
    o_ref[...] = jnp.dot(jax.nn.gelu(pre).astype(BF16), w2_ref[...],
                         preferred_element_type=F32).astype(o_ref.dtype)


def _nsa_compress(r, pe, w1, w2):
    b, _, nc, wdt = r.shape
    hid = w1.shape[2]
    return pl.pallas_call(
        _cmp_kernel,
        grid=(b, 4),
        in_specs=[pl.BlockSpec((None, None, nc, wdt), lambda bi, t: (bi, t, 0, 0)),
                  pl.BlockSpec((None, 8, 2 * wdt), lambda bi, t: (t // 2, 0, 0)),
                  pl.BlockSpec((None, 2 * wdt, hid), lambda bi, t: (t // 2, 0, 0)),
                  pl.BlockSpec((None, hid, HEAD_DIM), lambda bi, t: (t // 2, 0, 0))],
        out_specs=pl.BlockSpec((None, None, nc, HEAD_DIM), lambda bi, t: (bi, t, 0, 0)),
        out_shape=jax.ShapeDtypeStruct((b, 4, nc, HEAD_DIM), BF16),
        compiler_params=_cparams(("parallel", "arbitrary")),
        name="nsa_compress",
    )(r, pe, w1, w2)


def _nsa_kernel(qr_ref, qw_ref, dg_ref, gb_ref, kc_ref, vct_ref, ks_ref, vst_ref, kw_ref, vwt_ref,
                o_ref, sel_ref, wb_ref, *bufs, tq, seq_len):
    tk = tq
    i = pl.program_id(1)
    nch = i + 1
    nc = seq_len // NSA_CMP_STRIDE
    n_sb = seq_len // NSA_SLC_BLOCK
    n_sel = min(NSA_SLC_TOPK, n_sb)
    k_loc = lax.broadcasted_iota(I32, (tk, tq), 0)
    q_loc = lax.broadcasted_iota(I32, (tk, tq), 1)
    qpos = i * tq + lax.broadcasted_iota(I32, (1, tq), 1)

    gates_t = jax.nn.sigmoid(dg_ref[...] + gb_ref[...]).T
    kc = kc_ref[...]
    cmp_end = lax.broadcasted_iota(I32, (nc, 1), 0) * NSA_CMP_STRIDE + (NSA_CMP_LEN - 1)
    cbias = jnp.where(cmp_end <= qpos, 0.0, NINF)

    nh = D_HEADS // 2
    q_rot = [_split_halves(qr_ref[:, p * LANES:(p + 1) * LANES], roped=True) for p in range(nh)]
    q_raw = [_split_halves(qw_ref[:, p * LANES:(p + 1) * LANES]) for p in range(nh)]
    cbias4 = jnp.concatenate([cbias] * nh, axis=1)
    o_cmp, psum = [], []
    for g in range(2):
        qg = jnp.concatenate([q_raw[p][g] for p in range(nh)], axis=0)
        s = _dot_nt(kc, qg) + cbias4
        m = jnp.max(s, axis=0, keepdims=True)
        e = jnp.exp2(s - jnp.where(m == NINF, 0.0, m))
        pc = e * _safe_recip(jnp.sum(e, axis=0, keepdims=True))
        psum.append(sum(pc[:, p * tq:(p + 1) * tq] for p in range(nh)))
        o_cmp.append(jnp.dot(vct_ref[g], pc.astype(BF16), preferred_element_type=F32))

    per = tk // NSA_SLC_BLOCK
    nbp = -(-n_sb // 8) * 8
    cn = lax.broadcasted_iota(I32, (nbp, nc), 1) * NSA_CMP_STRIDE
    sj = lax.broadcasted_iota(I32, (nbp, nc), 0) * NSA_SLC_BLOCK
    shares = jnp.where((cn <= sj + NSA_SLC_BLOCK - 1) & (cn + NSA_CMP_LEN - 1 >= sj), 1.0, 0.0)
    blk = lax.broadcasted_iota(I32, (nbp, tq), 0)
    cur = qpos // NSA_SLC_BLOCK
    causal_b = blk <= cur
    forced = (blk == 0) | ((blk >= cur - 1) & causal_b)
    for g in range(2):
        imp = jnp.dot(shares, psum[g], precision=lax.Precision.HIGHEST, preferred_element_type=F32)
        val = jnp.where(forced, BIG, jnp.where(causal_b, imp, NEG))
        val = jnp.where(blk < n_sb, val, NINF)
        rowb = jnp.where(_top_n_mask(val, n_sel, 0) > 0.5, 0.0, NINF)
        for c in range(seq_len // tk):
            sel_ref[g, c] = jnp.concatenate([rowb[c * per:(c + 1) * per], jnp.zeros((8 - per, tq), F32)], axis=0)

    wb_ref[0] = jnp.where(k_loc <= q_loc, 0.0, NINF)
    wb_ref[1] = jnp.zeros((tk, tq), F32)
    wb_ref[2] = jnp.where(k_loc > q_loc, 0.0, NINF)
    n_wc = NSA_WINDOW // tk + 1
    w_first = jnp.maximum(i - (n_wc - 1), 0)
    n_w = i - w_first + 1

    for p in range(D_HEADS // 2):
        qs = q_rot[p]

        def qk_s(c, j, qs=qs):
            off = pl.multiple_of(c * tk, tk)
            return _dot_nt(ks_ref[pl.ds(off, tk), :], qs[j])

        def bias_s(c, j):
            rows = sel_ref[j, c]
            tile = jnp.concatenate([jnp.broadcast_to(rows[r:r + 1], (NSA_SLC_BLOCK, tq)) for r in range(per)], axis=0)
            return tile + wb_ref[jnp.where(c == i, 0, 1)]

        o_slc = _pipe_flash(nch, 2, qk_s, lambda c, j: _with_ones(vst_ref[c, j * HEAD_DIM:(j + 1) * HEAD_DIM, :]),
                            bias_s, bufs, HEAD_DIM + DV_PAD, tk, tq)

        def qk_w(c, j, qs=qs):
            off = pl.multiple_of((w_first + c) * tk, tk)
            return _dot_nt(kw_ref[pl.ds(off, tk), :], qs[j])

        def bias_w(c, j):
            return wb_ref[i - (w_first + c)]

        o_win = _pipe_flash(n_w, 2, qk_w,
                            lambda c, j: _with_ones(vwt_ref[w_first + c, j * HEAD_DIM:(j + 1) * HEAD_DIM, :]), bias_w, bufs,
                            HEAD_DIM + DV_PAD, tk, tq)
        outs = []
        for g in range(2):
            h = g * (D_HEADS // 2) + p
            outs.append(gates_t[3 * h:3 * h + 1] * o_cmp[g][:, p * tq:(p + 1) * tq]
                        + gates_t[3 * h + 1:3 * h + 2] * _normalize_t(o_slc[g], HEAD_DIM)
                        + gates_t[3 * h + 2:3 * h + 3] * _normalize_t(o_win[g], HEAD_DIM))
        o_ref[:, p * LANES:(p + 1) * LANES] = jnp.concatenate(outs, axis=0).T.astype(o_ref.dtype)


def _nsa_attention(rp, pp, vt, dg, gate_b, kcmp, vcmp, tq=TQ):
    b, s, _ = rp.shape
    nk = s // tq
    nc = s // NSA_CMP_STRIDE
    assert NSA_WINDOW == 2 * tq
    n_wc = NSA_WINDOW // tq + 1
    vct = vcmp.reshape(b, nc, 2, HEAD_DIM).transpose(0, 2, 3, 1)
    full = lambda t: pl.BlockSpec((None, s, LANES), lambda bi, i: (bi, 0, t))
    vspec = lambda t: pl.BlockSpec((None, nk, LANES, tq), lambda bi, i: (bi, 0, t, 0))
    return pl.pallas_call(
        functools.partial(_nsa_kernel, tq=tq, seq_len=s),
        grid=(b, nk),
        in_specs=[pl.BlockSpec((None, tq, 4 * LANES), lambda bi, i: (bi, i, 2)),
                  pl.BlockSpec((None, tq, 4 * LANES), lambda bi, i: (bi, i, 0)),
                  pl.BlockSpec((None, tq, LANES), lambda bi, i: (bi, i, 0)),
                  pl.BlockSpec((1, LANES), lambda bi, i: (0, 0)),
                  pl.BlockSpec((None, nc, LANES), lambda bi, i: (bi, 0, 0)),
                  pl.BlockSpec((None, 2, HEAD_DIM, nc), lambda bi, i: (bi, 0, 0, 0)),
                  full(12), vspec(4), full(13), vspec(5)],
        out_specs=pl.BlockSpec((None, tq, 4 * LANES), lambda bi, i: (bi, i, 0)),
        out_shape=jax.ShapeDtypeStruct((b, s, 4 * LANES), BF16),
        scratch_shapes=[pltpu.VMEM((2, nk, 8, tq), F32), pltpu.VMEM((n_wc, tq, tq), F32)]
        + _pipe_scratch(2, tq, tq),
        compiler_params=_cparams(("parallel", "arbitrary")),
        name="nsa_attn",
    )(rp, pp, dg, gate_b, kcmp, vct, rp, vt, rp, vt)


def _cast_kernel(x_ref, o_ref):
    o_ref[...] = x_ref[...].astype(o_ref.dtype)


def _layer_to_bf16(w, layer):
    _, e, r, c = w.shape
    tr = 1 << ((2 ** 21 // c).bit_length() - 1)
    out = pl.pallas_call(
        _cast_kernel,
        grid=(e * r // tr,),
        in_specs=[pl.BlockSpec((None, tr, c), lambda i: (layer, i, 0))],
        out_specs=pl.BlockSpec((tr, c), lambda i: (i, 0)),
        out_shape=jax.ShapeDtypeStruct((e * r, c), BF16),
        compiler_params=_cparams(("parallel",)),
        name="cast_bf16",
    )(w.reshape(w.shape[0], e * r, c))
    return out.reshape(e, r, c)


def _moe_ffn_kernel(be_ref, nu_ref, x_ref, wg_ref, wu_ref, wd_ref, o_ref):
    used = pl.program_id(0) < nu_ref[0]

    @pl.when(used)
    def _():
        x = x_ref[...]
        h = jax.nn.silu(jnp.dot(x, wg_ref[...], preferred_element_type=F32)) * jnp.dot(
            x, wu_ref[...], preferred_element_type=F32)
        o_ref[...] = jnp.dot(h.astype(BF16), wd_ref[...], preferred_element_type=F32).astype(o_ref.dtype)

    @pl.when(jnp.logical_not(used))
    def _():
        o_ref[...] = jnp.zeros(o_ref.shape, o_ref.dtype)


def _moe_ffn(x_sorted, block_e, n_used, wg, wu, wd):
    ns, d = x_sorted.shape
    fdim = wg.shape[2]
    once = pl.Buffered(1)
    grid_spec = pltpu.PrefetchScalarGridSpec(
        num_scalar_prefetch=2,
        grid=(ns // MOE_TM,),
        in_specs=[pl.BlockSpec((MOE_TM, d), lambda i, be, nu: (jnp.minimum(i, nu[0] - 1), 0)),
                  pl.BlockSpec((None, d, fdim), lambda i, be, nu: (be[i], 0, 0), pipeline_mode=once),
                  pl.BlockSpec((None, d, fdim), lambda i, be, nu: (be[i], 0, 0), pipeline_mode=once),
                  pl.BlockSpec((None, fdim, d), lambda i, be, nu: (be[i], 0, 0), pipeline_mode=once)],
        out_specs=pl.BlockSpec((MOE_TM, d), lambda i, be, nu: (i, 0)),
    )
    return pl.pallas_call(
        _moe_ffn_kernel,
        grid_spec=grid_spec,
        out_shape=jax.ShapeDtypeStruct((ns, d), BF16),
        compiler_params=_cparams(("arbitrary",)),
        name="moe_ffn",
    )(block_e, n_used, x_sorted, wg, wu, wd)


def _combine_ln_kernel(x_ref, y0_ref, y1_ref, rt_ref, g_ref, b_ref, xo_ref, xb_ref):
    rt = rt_ref[...]
    ffn = rt[:, 2:3] * y0_ref[...].astype(F32) + rt[:, 3:4] * y1_ref[...].astype(F32)
    y = _layer_norm(DN_ALPHA * x_ref[...] + ffn, g_ref[...], b_ref[...])
    xo_ref[...] = y
    xb_ref[...] = y.astype(BF16)


def _combine_ln(x, y0, y1, rt, g, b, tm=ROW_TILE):
    n, d = x.shape
    row = lambda i: (i, 0)
    fixed = lambda i: (0, 0)
    return pl.pallas_call(
        _combine_ln_kernel,
        grid=(n // tm,),
        in_specs=[pl.BlockSpec((tm, d), row), pl.BlockSpec((tm, d), row), pl.BlockSpec((tm, d), row),
                  pl.BlockSpec((tm, LANES), row), pl.BlockSpec((1, d), fixed), pl.BlockSpec((1, d), fixed)],
        out_specs=[pl.BlockSpec((tm, d), row), pl.BlockSpec((tm, d), row)],
        out_shape=[jax.ShapeDtypeStruct((n, d), F32), jax.ShapeDtypeStruct((n, d), BF16)],
        compiler_params=_cparams(("parallel",)),
        name="moe_combine_ln",
    )(x, y0, y1, rt, g.reshape(1, d), b.reshape(1, d))


def _moe_layout(rt, n):
    e_flat = rt[:, 0:TOP_K].astype(I32).reshape(-1)
    nk = n * TOP_K
    onehot = (e_flat[:, None] == jnp.arange(N_EXPERTS, dtype=I32)[None, :]).astype(I32)
    rank = jnp.take_along_axis(jnp.cumsum(onehot, axis=0), e_flat[:, None], axis=1)[:, 0] - 1
    counts = jnp.sum(onehot, axis=0)
    padded = (counts + MOE_TM - 1) // MOE_TM * MOE_TM
    pad_end = jnp.cumsum(padded)
    pad_start = pad_end - padded
    grp_start = jnp.cumsum(counts) - counts
    slot = pad_start[e_flat] + rank
    n_blocks = -(-nk // MOE_TM) + N_EXPERTS
    n_slots = n_blocks * MOE_TM
    order = jnp.argsort(e_flat, stable=True).astype(I32)
    sl = jnp.arange(n_slots, dtype=I32)
    slot_e = jnp.minimum(jnp.searchsorted(pad_end, sl, side='right'), N_EXPERTS - 1).astype(I32)
    within = sl - pad_start[slot_e]
    valid = within < counts[slot_e]
    src = jnp.where(valid, grp_start[slot_e] + within, 0)
    slot_tok = jnp.where(valid, order[src] // TOP_K, 0)
    n_used = (pad_end[-1] // MOE_TM).astype(I32).reshape(1)
    blk = jnp.arange(n_blocks, dtype=I32)
    block_e = slot_e[jnp.minimum(blk, n_used[0] - 1) * MOE_TM]
    return slot_tok, slot.reshape(n, TOP_K), block_e, n_used


def _pair_perm(n_heads):
    half = n_heads // 2
    cols = []
    for p in range(half):
        cols += list(range(p * HEAD_DIM, (p + 1) * HEAD_DIM))
        cols += list(range((half + p) * HEAD_DIM, (half + p + 1) * HEAD_DIM))
    return np.asarray(cols, dtype=np.int32)


def _pad_cols(w, width):
    return jnp.pad(w, ((0, 0), (0, width - w.shape[1])))


def _split_cols(w, sizes):
    out, off = [], 0
    for sz in sizes:
        out.append(w[:, off:off + sz])
        off += sz
    return out


def _even_layer(x, xb, w_in, w_out, lam_params, subln, lam_init, wg, wu, wd, ln, tabs, bsz, seq_len):
    n = x.shape[0]
    perm = _pair_perm(B_HEADS)
    aq, ak, av, bq, bk, bv, iq, ik, iw = _split_cols(w_in, EVEN_SIZES)
    w_rope = jnp.concatenate([aq * SCALE, ak, bq[:, perm] * SCALE, iq, bk, ik, ik], axis=1)
    w_rope = w_rope[:, _rope_layout(w_rope.shape[1])].astype(BF16)
    w_val = jnp.concatenate([av, bv], axis=1).astype(BF16)
    w_iw = _pad_cols(iw, LANES).astype(BF16)
    rp = _proj_rope(xb, w_rope, ROW_TILE, w_rope.shape[1] // 2, seq_len, tabs).reshape(bsz, seq_len, -1)
    vt, iwv = _proj_values(xb, w_val, w_iw, None, ROW_TILE, TQ)
    vt = vt.reshape(bsz, seq_len // TQ, -1, TQ)
    iwv = iwv.reshape(bsz, seq_len, LANES)
    o_a = _diff_attention(rp, vt, lam_params, subln, lam_init)
    o_b = _dsa_attention(rp, vt, iwv)
    half = w_out.shape[0] // 2
    wo_a = w_out[:half].astype(BF16)
    wo_b = w_out[half:][perm].astype(BF16)
    g_mix, b_mix, g_ffn, b_ffn = ln
    x1, x1b = _outproj_ln(x, o_a.reshape(n, -1), o_b.reshape(n, -1), wo_a, wo_b, g_mix, b_mix)
    return _ffn_ln(x1, x1b, wg.astype(BF16), wu.astype(BF16), wd.astype(BF16), g_ffn, b_ffn)


def _odd_layer(x, xb, w_in, w_out, gate_b, pe, phi_w1, phi_w2, w_router, b_router, wg, wu, wd, ln, tabs,
               bsz, seq_len):
    n = x.shape[0]
    perm = _pair_perm(D_HEADS)
    cq, ck, cv, dq, dkc, dvc, dks, dvs, dkw, dvw, dg = _split_cols(w_in, ODD_SIZES)
    dq = dq[:, perm] * SCALE
    w_rope = jnp.concatenate([cq * SCALE, ck, dq, dks, dkw], axis=1)
    w_rope = w_rope[:, _rope_layout(w_rope.shape[1])].astype(BF16)
    w_plain = jnp.concatenate([dq, dkc, dvc], axis=1).astype(BF16)
    w_val = jnp.concatenate([cv, dvs, dvw], axis=1).astype(BF16)
    w_dg = _pad_cols(dg, LANES).astype(BF16)
    rp = _proj_rope(xb, w_rope, ROW_TILE, w_rope.shape[1] // 2, seq_len, tabs).reshape(bsz, seq_len, -1)
    vt, dgv, pp = _proj_values(xb, w_val, w_dg, w_plain, ROW_TILE, TQ)
    vt = vt.reshape(bsz, seq_len // TQ, -1, TQ)
    dgv = dgv.reshape(bsz, seq_len, LANES)
    pp = pp.reshape(bsz, seq_len, -1)

    o_c = _moba_attention(rp, vt)

    nc = seq_len // NSA_CMP_STRIDE
    tok = pp[:, :, 4 * LANES:6 * LANES].reshape(bsz, nc, NSA_CMP_STRIDE, 4, HEAD_DIM)
    r = tok.transpose(0, 3, 1, 2, 4).reshape(bsz, 4, nc, NSA_CMP_STRIDE * HEAD_DIM)
    pe_flat = jnp.pad(pe.reshape(2, 1, -1), ((0, 0), (0, 7), (0, 0))).astype(BF16)
    cmp = _nsa_compress(r, pe_flat, phi_w1.astype(BF16), phi_w2.astype(BF16))
    kcmp = jnp.concatenate([cmp[:, 0], cmp[:, 1]], axis=-1)
    vcmp = jnp.concatenate([cmp[:, 2], cmp[:, 3]], axis=-1)
    gb = _pad_cols(gate_b.reshape(1, -1), LANES)
    o_d = _nsa_attention(rp, pp, vt, dgv, gb, kcmp, vcmp)

    half = w_out.shape[0] // 2
    wo_c = w_out[:half].astype(BF16)
    wo_d = w_out[half:][perm].astype(BF16)
    g_mix, b_mix, g_ffn, b_ffn = ln
    router = (_pad_cols(w_router, LANES), _pad_cols(b_router.reshape(1, -1), LANES))
    x1, x1b, rt = _outproj_ln(x, o_c.reshape(n, -1), o_d.reshape(n, -1), wo_c, wo_d, g_mix, b_mix, router)

    slot_tok, slot, block_e, n_used = _moe_layout(rt, n)
    y_slots = _moe_ffn(x1b[slot_tok], block_e, n_used, wg, wu, wd)
    return _combine_ln(x1, y_slots[slot[:, 0]], y_slots[slot[:, 1]], rt, g_ffn, b_ffn)


@jax.jit
def kernel(x, ev_w_in, ev_w_out, dif_lambda, dif_subln, ffd_w_gate, ffd_w_up, ffd_w_down, od_w_in, od_w_out,
           nsa_gate_b, nsa_pe, nsa_phi_w1, nsa_phi_w2, moe_w_router, moe_b_router, moe_w_gate, moe_w_up,
           moe_w_down, ln_mix_g, ln_mix_b, ln_ffn_g, ln_ffn_b):
    bsz, seq_len, d = x.shape
    tabs = _rope_tables(seq_len)
    xf = x.reshape(bsz * seq_len, d)
    xb = xf.astype(BF16)
    for l in range(DEPTH):
        i = l // 2
        ln = (ln_mix_g[l], ln_mix_b[l], ln_ffn_g[l], ln_ffn_b[l])
        if l % 2 == 0:
            lam_init = 0.8 - 0.6 * math.exp(-0.3 * l)
            xf, xb = _even_layer(xf, xb, ev_w_in[i], ev_w_out[i], dif_lambda[i], dif_subln[i], lam_init,
                                 ffd_w_gate[i], ffd_w_up[i], ffd_w_down[i], ln, tabs, bsz, seq_len)
        else:
            xf, xb = _odd_layer(xf, xb, od_w_in[i], od_w_out[i], nsa_gate_b[i], nsa_pe[i], nsa_phi_w1[i],
                                nsa_phi_w2[i], moe_w_router[i], moe_b_router[i], _layer_to_bf16(moe_w_gate, i),
                                _layer_to_bf16(moe_w_up, i), _layer_to_bf16(moe_w_down, i), ln, tabs, bsz, seq_len)
    return xf.reshape(bsz, seq_len, d)
```

```python
import functools
import math

import numpy as np
import jax
import jax.numpy as jnp
from jax import lax
from jax.experimental import pallas as pl
from jax.experimental.pallas import tpu as pltpu

F32 = jnp.float32
BF16 = jnp.bfloat16
I32 = jnp.int32

LANES = 128
VMEM_LIMIT = 56 * 1024 * 1024

DEPTH = 4
HEAD_DIM = 64
ROPE_THETA = 10000.0
LN_EPS = 1e-5
DN_ALPHA = (2 * DEPTH) ** 0.25
SCALE = HEAD_DIM ** -0.5 * math.log2(math.e)
NEG = -1e30
BIG = 1e30
M_INIT = -1e30
NINF = float("-inf")

A_HEADS = 4
B_HEADS = 8
IDX_HEADS = 4
DSA_TOPK = 256
C_HEADS = 8
MOBA_BLOCK = 256
MOBA_TOPK = 3
D_HEADS = 8
NSA_CMP_LEN = 32
NSA_CMP_STRIDE = 16
NSA_SLC_BLOCK = 64
NSA_SLC_TOPK = 16
NSA_WINDOW = 512
N_EXPERTS = 8
TOP_K = 2
MOE_TM = 512
N_BISECT = 14
ROW_TILE = 512
KV_GROUPS = 2

EVEN_SIZES = (A_HEADS * 2 * HEAD_DIM, A_HEADS * 2 * HEAD_DIM, A_HEADS * 2 * HEAD_DIM, B_HEADS * HEAD_DIM,
              KV_GROUPS * HEAD_DIM, KV_GROUPS * HEAD_DIM, IDX_HEADS * HEAD_DIM, HEAD_DIM, IDX_HEADS)
ODD_SIZES = (C_HEADS * HEAD_DIM,) * 3 + (D_HEADS * HEAD_DIM,) + (KV_GROUPS * HEAD_DIM,) * 6 + (D_HEADS * 3,)


def _cparams(sem):
    return pltpu.CompilerParams(dimension_semantics=sem, vmem_limit_bytes=VMEM_LIMIT)


def _dot_nt(a, b):
    return lax.dot_general(a, b, (((1,), (1,)), ((), ())), preferred_element_type=F32)


def _layer_norm(y, g, b):
    mu = jnp.mean(y, axis=-1, keepdims=True)
    yc = y - mu
    var = jnp.mean(yc * yc, axis=-1, keepdims=True)
    return yc * lax.rsqrt(var + LN_EPS) * g + b


def _safe_recip(l):
    return jnp.where(l > 0.0, 1.0 / jnp.where(l > 0.0, l, 1.0), 0.0)


def _split_halves(t, roped=False):
    lane = lax.broadcasted_iota(I32, (1, LANES), 1)
    lo = (lane // (HEAD_DIM // 2)) % 2 == 0 if roped else lane < HEAD_DIM
    z = jnp.zeros_like(t)
    return jnp.where(lo, t, z), jnp.where(lo, z, t)


def _rope_layout(n_cols):
    q = HEAD_DIM // 2
    tile = np.concatenate([np.arange(0, q), np.arange(2 * q, 3 * q), np.arange(q, 2 * q), np.arange(3 * q, 4 * q)])
    return (np.arange(0, n_cols, LANES)[:, None] + tile[None, :]).reshape(-1).astype(np.int32)


def _rope_mm_kernel(x_ref, w_ref, cos_ref, sin_ref, o_ref):
    acc = jnp.dot(x_ref[...].astype(BF16), w_ref[...], preferred_element_type=F32)
    cos = cos_ref[...]
    sin = sin_ref[...]
    for c in range(acc.shape[1] // LANES):
        a = acc[:, c * LANES:(c + 1) * LANES]
        rot = pltpu.roll(a, LANES // 2, 1)
        o_ref[:, c * LANES:(c + 1) * LANES] = (a * cos + rot * sin).astype(o_ref.dtype)


def _proj_rope(x, w, tm, tn, seq_len, rope_tabs):
    n, d = x.shape
    p = w.shape[1]
    nt = seq_len // tm
    return pl.pallas_call(
        _rope_mm_kernel,
        grid=(n // tm, p // tn),
        in_specs=[pl.BlockSpec((tm, d), lambda i, j: (i, 0)), pl.BlockSpec((d, tn), lambda i, j: (0, j)),
                  pl.BlockSpec((tm, LANES), lambda i, j: (i % nt, 0)),
                  pl.BlockSpec((tm, LANES), lambda i, j: (i % nt, 0))],
        out_specs=pl.BlockSpec((tm, tn), lambda i, j: (i, j)),
        out_shape=jax.ShapeDtypeStruct((n, p), BF16),
        compiler_params=_cparams(("parallel", "arbitrary")),
        name="proj_rope",
    )(x, w, *rope_tabs)


def _rope_tables(seq_len):
    d = HEAD_DIM
    inv = ROPE_THETA ** (-jnp.arange(0, d, 2, dtype=F32) / d)
    ang = jnp.arange(seq_len, dtype=I32).astype(F32)[:, None] * inv[None, :]
    cos = jnp.cos(ang)
    sin = jnp.sin(ang)
    cos128 = jnp.tile(cos, (1, LANES // (d // 2)))
    sin128 = jnp.concatenate([-sin, -sin, sin, sin], axis=1)
    return cos128, sin128


def _route_top2(x, w, b):
    xh, wh = x.astype(BF16), w.astype(BF16)
    xl, wl = (x - xh.astype(F32)).astype(BF16), (w - wh.astype(F32)).astype(BF16)
    logits = (jnp.dot(xh, wh, preferred_element_type=F32) + jnp.dot(xh, wl, preferred_element_type=F32)
              + jnp.dot(xl, wh, preferred_element_type=F32)) + b
    lane = lax.broadcasted_iota(I32, (1, LANES), 1)
    lanef = lane.astype(F32)
    v = jnp.where(lane < N_EXPERTS, logits, NINF)
    l0 = jnp.max(v, axis=1, keepdims=True)
    i0 = jnp.min(jnp.where(v == l0, lanef, float(LANES)), axis=1, keepdims=True)
    v = jnp.where(lanef == i0, NINF, v)
    l1 = jnp.max(v, axis=1, keepdims=True)
    i1 = jnp.min(jnp.where(v == l1, lanef, float(LANES)), axis=1, keepdims=True)
    e1 = jnp.exp(l1 - l0)
    g0 = 1.0 / (1.0 + e1)
    g1 = e1 / (1.0 + e1)
    return jnp.where(lane == 0, i0, jnp.where(lane == 1, i1, jnp.where(lane == 2, g0, jnp.where(lane == 3, g1, 0.0))))


def _outproj_ln_kernel(x_ref, a_ref, b_ref, wa_ref, wb_ref, g_ref, bb_ref, *rest):
    mix = (jnp.dot(a_ref[...], wa_ref[...], preferred_element_type=F32)
           + jnp.dot(b_ref[...], wb_ref[...], preferred_element_type=F32))
    y = _layer_norm(DN_ALPHA * x_ref[...] + mix, g_ref[...], bb_ref[...])
    if len(rest) == 2:
        xo_ref, xb_ref = rest
    else:
        wr_ref, br_ref, xo_ref, xb_ref, rt_ref = rest
        rt_ref[...] = _route_top2(y, wr_ref[...], br_ref[...])
    xo_ref[...] = y
    xb_ref[...] = y.astype(BF16)


def _outproj_ln(x, oa, ob, wa, wb, g, b, router=None, tm=ROW_TILE):
    n, d = x.shape
    ka, kb = oa.shape[1], ob.shape[1]
    row = lambda i: (i, 0)
    fixed = lambda i: (0, 0)
    in_specs = [pl.BlockSpec((tm, d), row), pl.BlockSpec((tm, ka), row), pl.BlockSpec((tm, kb), row),
                pl.BlockSpec((ka, d), fixed), pl.BlockSpec((kb, d), fixed),
                pl.BlockSpec((1, d), fixed), pl.BlockSpec((1, d), fixed)]
    out_specs = [pl.BlockSpec((tm, d), row), pl.BlockSpec((tm, d), row)]
    out_shape = [jax.ShapeDtypeStruct((n, d), F32), jax.ShapeDtypeStruct((n, d), BF16)]
    args = [x, oa, ob, wa, wb, g.reshape(1, d), b.reshape(1, d)]
    if router is not None:
        in_specs += [pl.BlockSpec((d, LANES), fixed), pl.BlockSpec((1, LANES), fixed)]
        out_specs.append(pl.BlockSpec((tm, LANES), row))
        out_shape.append(jax.ShapeDtypeStruct((n, LANES), F32))
        args += list(router)
    return pl.pallas_call(
        _outproj_ln_kernel,
        grid=(n // tm,),
        in_specs=in_specs,
        out_specs=out_specs,
        out_shape=out_shape,
        compiler_params=_cparams(("parallel",)),
        name="outproj_ln",
    )(*args)


def _ffn_ln_kernel(x_ref, xb_ref, wg_ref, wu_ref, wd_ref, g_ref, b_ref, xo_ref, xob_ref):
    xb = xb_ref[...]
    h = jax.nn.silu(jnp.dot(xb, wg_ref[...], preferred_element_type=F32)) * jnp.dot(
        xb, wu_ref[...], preferred_element_type=F32)
    ffn = jnp.dot(h.astype(BF16), wd_ref[...], preferred_element_type=F32)
    y = _layer_norm(DN_ALPHA * x_ref[...] + ffn, g_ref[...], b_ref[...])
    xo_ref[...] = y
    xob_ref[...] = y.astype(BF16)


def _ffn_ln(x, xb, wg, wu, wd, g, b, tm=ROW_TILE):
    n, d = x.shape
    fdim = wg.shape[1]
    row = lambda i: (i, 0)
    fixed = lambda i: (0, 0)
    once = pl.Buffered(1)
    return pl.pallas_call(
        _ffn_ln_kernel,
        grid=(n // tm,),
        in_specs=[pl.BlockSpec((tm, d), row), pl.BlockSpec((tm, d), row),
                  pl.BlockSpec((d, fdim), fixed, pipeline_mode=once),
                  pl.BlockSpec((d, fdim), fixed, pipeline_mode=once),
                  pl.BlockSpec((fdim, d), fixed, pipeline_mode=once),
                  pl.BlockSpec((1, d), fixed), pl.BlockSpec((1, d), fixed)],
        out_specs=[pl.BlockSpec((tm, d), row), pl.BlockSpec((tm, d), row)],
        out_shape=[jax.ShapeDtypeStruct((n, d), F32), jax.ShapeDtypeStruct((n, d), BF16)],
        compiler_params=_cparams(("parallel",)),
        name="ffn_ln",
    )(x, xb, wg, wu, wd, g.reshape(1, d), b.reshape(1, d))


def _top_n_mask(v, n, axis):
    idx = lax.broadcasted_iota(I32, v.shape, axis).astype(F32)
    sel = jnp.zeros(v.shape, F32)
    for _ in range(n):
        mx = jnp.max(v, axis=axis, keepdims=True)
        first = jnp.min(jnp.where(v == mx, idx, float(v.shape[axis])), axis=axis, keepdims=True)
        pick = idx == first
        sel = jnp.where(pick, 1.0, sel)
        v = jnp.where(pick, NINF, v)
    return sel


DV_PAD = 16
TQ = 256


def _with_ones(vt):
    return jnp.concatenate([vt, jnp.ones((DV_PAD, vt.shape[1]), vt.dtype)], axis=0)


def _mm_t_kernel(x_ref, wv_ref, ws_ref, *rest, tk):
    x = x_ref[...].astype(BF16)
    vt_ref, small_ref = rest[-2:] if len(rest) == 2 else rest[1:3]
    acc = jnp.dot(x, wv_ref[...], preferred_element_type=F32)
    for cc in range(acc.shape[0] // tk):
        vt_ref[cc] = acc[cc * tk:(cc + 1) * tk, :].T.astype(vt_ref.dtype)
    small_ref[...] = jnp.dot(x, ws_ref[...], preferred_element_type=F32)
    if len(rest) == 4:
        rest[3][...] = jnp.dot(x, rest[0][...], preferred_element_type=F32).astype(rest[3].dtype)


def _proj_values(x, w_val, w_small, w_plain, tm, tk):
    n, d = x.shape
    pv, ps = w_val.shape[1], w_small.shape[1]
    row = lambda i: (i, 0)
    fixed = lambda i: (0, 0)
    in_specs = [pl.BlockSpec((tm, d), row), pl.BlockSpec((d, pv), fixed), pl.BlockSpec((d, ps), fixed)]
    out_specs = [pl.BlockSpec((tm // tk, pv, tk), lambda i: (i, 0, 0)), pl.BlockSpec((tm, ps), row)]
    out_shape = [jax.ShapeDtypeStruct((n // tk, pv, tk), BF16), jax.ShapeDtypeStruct((n, ps), F32)]
    args = [x, w_val, w_small]
    if w_plain is not None:
        pp = w_plain.shape[1]
        in_specs.append(pl.BlockSpec((d, pp), fixed))
        out_specs.append(pl.BlockSpec((tm, pp), row))
        out_shape.append(jax.ShapeDtypeStruct((n, pp), BF16))
        args.append(w_plain)
    return pl.pallas_call(
        functools.partial(_mm_t_kernel, tk=tk),
        grid=(n // tm,),
        in_specs=in_specs,
        out_specs=out_specs,
        out_shape=out_shape,
        compiler_params=_cparams(("parallel",)),
        name="proj_t",
    )(*args)


def _normalize_t(acc, width):
    return acc[:width] * _safe_recip(acc[width:width + 1])


def _pipe_flash(n, ns, qk, vt_at, bias_at, bufs, dv, tk, tq):
    sa, sb, pa, pb = bufs

    def softmax_into(p_ref, j, st, m):
        m_new = jnp.maximum(m, jnp.max(st, axis=0, keepdims=True))
        p_ref[j] = jnp.exp2((st - m_new).astype(BF16))
        return m_new, jnp.exp2(m - m_new)

    def half(c, carry, s_cur, s_nxt, p_prev, p_cur):
        nxt = jnp.minimum(c + 1, n - 1)
        for j in range(ns):
            s_nxt[j] = qk(nxt, j)
        out = []
        for j in range(ns):
            m, acc, alpha = carry[j]
            acc = alpha * acc + jnp.dot(vt_at(c - 1, j), p_prev[j], preferred_element_type=F32)
            m, alpha = softmax_into(p_cur, j, s_cur[j] + bias_at(c, j), m)
            out.append((m, acc, alpha))
        return tuple(out)

    for j in range(ns):
        sa[j] = qk(0, j)
    first = []
    for j in range(ns):
        sb[j] = qk(jnp.minimum(1, n - 1), j)
        m, alpha = softmax_into(pa, j, sa[j] + bias_at(0, j), jnp.full((1, tq), M_INIT, F32))
        first.append((m, jnp.zeros((dv, tq), F32), alpha))

    def body(t, carry):
        carry = half(2 * t + 1, carry, sb, sa, pa, pb)
        return half(2 * t + 2, carry, sa, sb, pb, pa)

    carry = lax.fori_loop(0, (n - 1) // 2, body, tuple(first))

    def flush(carry, p_last):
        return tuple(alpha * acc + jnp.dot(vt_at(n - 1, j), p_last[j], preferred_element_type=F32)
                     for j, (_, acc, alpha) in enumerate(carry))

    def odd_tail(carry):
        return flush(half(n - 1, carry, sb, sa, pa, pb), pb)

    return lax.cond(n % 2 == 0, odd_tail, lambda carry: flush(carry, pa), carry)


def _pipe_scratch(ns, tk, tq):
    return [pltpu.VMEM((ns, tk, tq), F32)] * 2 + [pltpu.VMEM((ns, tk, tq), BF16)] * 2


def _causal_t(t):
    return jnp.where(lax.broadcasted_iota(I32, (t, t), 0) <= lax.broadcasted_iota(I32, (t, t), 1), 0.0, NINF)


def _diff_kernel(lam_ref, sub_ref, q_ref, k_ref, vt_ref, o_ref, *bufs, tq, lam_init):
    i = pl.program_id(2)
    lp = lam_ref[...]
    lam = (jnp.exp(jnp.sum(lp[0:1] * lp[1:2], axis=1, keepdims=True))
           - jnp.exp(jnp.sum(lp[2:3] * lp[3:4], axis=1, keepdims=True)) + lam_init)
    qs = _split_halves(q_ref[...], roped=True)
    n = i + 1
    bufs, tab_ref = bufs[:4], bufs[4]
    tab_ref[0] = jnp.zeros((tq, tq), F32)
    tab_ref[1] = _causal_t(tq)

    def qk(c, j):
        off = pl.multiple_of(c * tq, tq)
        return _dot_nt(k_ref[pl.ds(off, tq), :], qs[j])

    def bias_at(c, j):
        return tab_ref[jnp.where(c == i, 1, 0)]

    outs = _pipe_flash(n, 2, qk, lambda c, j: _with_ones(vt_ref[c]), bias_at, bufs, LANES + DV_PAD, tq, tq)
    o = _normalize_t(outs[0], LANES) - lam * _normalize_t(outs[1], LANES)
    o = o * lax.rsqrt(jnp.mean(o * o, axis=0, keepdims=True) + LN_EPS)
    o = o * sub_ref[...] * (1.0 - lam_init)
    o_ref[...] = o.T.astype(o_ref.dtype)


def _diff_attention(rp, vt, lam_params, subln, lam_init, tq=TQ):
    b, s, _ = rp.shape
    nk = s // tq
    return pl.pallas_call(
        functools.partial(_diff_kernel, tq=tq, lam_init=lam_init),
        grid=(b, A_HEADS, nk),
        in_specs=[pl.BlockSpec((4, HEAD_DIM), lambda bi, h, i: (0, 0)),
                  pl.BlockSpec((LANES, 1), lambda bi, h, i: (0, 0)),
                  pl.BlockSpec((None, tq, LANES), lambda bi, h, i: (bi, i, h)),
                  pl.BlockSpec((None, s, LANES), lambda bi, h, i: (bi, 0, A_HEADS + h)),
                  pl.BlockSpec((None, nk, LANES, tq), lambda bi, h, i: (bi, 0, h, 0))],
        out_specs=pl.BlockSpec((None, tq, LANES), lambda bi, h, i: (bi, i, h)),
        out_shape=jax.ShapeDtypeStruct((b, s, A_HEADS * LANES), BF16),
        scratch_shapes=_pipe_scratch(2, tq, tq) + [pltpu.VMEM((2, tq, tq), F32)],
        compiler_params=_cparams(("parallel", "parallel", "arbitrary")),
        name="diff_attn",
    )(lam_params, subln.reshape(LANES, 1), rp, rp, vt)


def _fold8(x, op):
    acc = x[0:8]
    for r in range(1, x.shape[0] // 8):
        acc = op(acc, x[r * 8:(r + 1) * 8])
    return acc


def _dsa_kernel(iq_ref, ikk_ref, iw_ref, q_ref, k_ref, vt_ref, o_ref, s_ref, j_ref, *bufs, tq, ksel, seq_len):
    tk = tq
    i = pl.program_id(1)
    nch = i + 1
    ksel_f = float(ksel)
    k_loc = lax.broadcasted_iota(I32, (tk, tq), 0)
    q_loc = lax.broadcasted_iota(I32, (tk, tq), 1)
    qpos = i * tq + lax.broadcasted_iota(I32, (1, tq), 1)

    iq = iq_ref[...]
    iwt = iw_ref[...].T
    iqh = []
    for pair in range(IDX_HEADS // 2):
        iqh += list(_split_halves(iq[:, pair * LANES:(pair + 1) * LANES], roped=True))

    def scores(c):
        off = pl.multiple_of(c * tk, tk)
        kk = ikk_ref[pl.ds(off, tk), :]
        sc = iwt[0:1] * jnp.maximum(_dot_nt(kk, iqh[0]), 0.0)
        for h in range(1, IDX_HEADS):
            sc = sc + iwt[h:h + 1] * jnp.maximum(_dot_nt(kk, iqh[h]), 0.0)
        return sc

    def full_body(c, carry):
        mx, mn = carry
        sc = scores(c)
        s_ref[c] = sc
        return jnp.maximum(mx, _fold8(sc, jnp.maximum)), jnp.minimum(mn, _fold8(sc, jnp.minimum))

    mx, mn = lax.fori_loop(0, i, full_body, (jnp.full((8, tq), -BIG, F32), jnp.full((8, tq), BIG, F32)))
    sc = scores(i)
    causal = k_loc <= q_loc
    s_ref[i] = jnp.where(causal, sc, NEG)
    mx = jnp.maximum(mx, _fold8(jnp.where(causal, sc, -BIG), jnp.maximum))
    mn = jnp.minimum(mn, _fold8(jnp.where(causal, sc, BIG), jnp.minimum))
    smax = jnp.max(mx, axis=0, keepdims=True)
    smin = jnp.min(mn, axis=0, keepdims=True)

    def count_where(ind):
        def body(c, acc):
            return acc + _fold8(ind(s_ref[c], c * tk + k_loc), jnp.add)
        acc = lax.fori_loop(0, nch, body, jnp.zeros((8, tq), F32))
        return jnp.sum(acc, axis=0, keepdims=True)

    def count_ge(th):
        return count_where(lambda x, kidx: jnp.where(x >= th, 1.0, 0.0))

    def max_below(th):
        def body(c, acc):
            x = s_ref[c]
            return jnp.maximum(acc, _fold8(jnp.where(x < th, x, NINF), jnp.maximum))
        acc = lax.fori_loop(0, nch, body, jnp.full((8, tq), NINF, F32))
        return jnp.max(acc, axis=0, keepdims=True)

    n_causal = (qpos + 1).astype(F32)
    take_all = n_causal <= ksel_f
    done0 = jnp.where(take_all, 1.0, 0.0)
    hi0 = smax + (jnp.abs(smax) * 2.0 ** -20 + 1e-30)

    def bisect(lo, hi):
        mid = lo + (hi - lo) * 0.5
        ge = count_ge(mid) >= ksel_f
        return jnp.where(ge, mid, lo), jnp.where(ge, hi, mid)

    lo, hi = lax.fori_loop(0, N_BISECT, lambda _, c: bisect(*c), (smin, hi0))

    def snap_body(carry):
        lo, hi, th, c_th, done, _ = carry
        lo, hi = bisect(lo, hi)
        t1 = max_below(hi)
        c1 = count_ge(t1)
        ok = c1 >= ksel_f
        th = jnp.where(done > 0.0, th, t1)
        c_th = jnp.where(done > 0.0, c_th, c1)
        hi = jnp.where(ok, hi, t1)
        done = jnp.where(ok, 1.0, done)
        return lo, hi, th, c_th, done, jnp.sum(1.0 - done)

    _, _, th, c_ge, _, _ = lax.while_loop(lambda c: c[5] > 0.0, snap_body,
                                          (lo, hi, smax, jnp.zeros((1, tq), F32), done0, jnp.sum(1.0 - done0)))

    need_tb = jnp.where(take_all, 0.0, jnp.where(c_ge > ksel_f, 1.0, 0.0))
    j_ref[...] = jnp.full((8, tq), seq_len - 1, I32)

    @pl.when(jnp.sum(need_tb) > 0.0)
    def _():
        need = ksel_f - count_where(lambda x, kidx: jnp.where(x > th, 1.0, 0.0))

        def jb(_, carry):
            lo_j, hi_j = carry
            mid = (lo_j + hi_j) // 2
            cnt = count_where(lambda x, kidx: jnp.where(x == th, jnp.where(kidx <= mid, 1.0, 0.0), 0.0))
            ge = cnt >= need
            return jnp.where(ge, lo_j, mid), jnp.where(ge, mid, hi_j)

        n_it = int(math.ceil(math.log2(seq_len))) + 1
        _, hi_j = lax.fori_loop(0, n_it, jb, (jnp.full((1, tq), -1, I32), jnp.full((1, tq), seq_len - 1, I32)))
        j_ref[...] = jnp.broadcast_to(hi_j, (8, tq))

    jsel = j_ref[0:1, :]

    def bias_body(c, _):
        x = s_ref[c]
        kidx = c * tk + k_loc
        keep = jnp.where(x > th, 0.0, jnp.where(x == th, jnp.where(kidx <= jsel, 0.0, NINF), NINF))
        keep = jnp.where(take_all, 0.0, keep)
        s_ref[c] = jnp.where(kidx <= qpos, keep, NINF)
        return 0

    lax.fori_loop(0, nch, bias_body, 0)

    for p in range(B_HEADS // 2):
        qs = _split_halves(q_ref[:, p * LANES:(p + 1) * LANES], roped=True)

        def qk(c, j, qs=qs):
            off = pl.multiple_of(c * tk, tk)
            return _dot_nt(k_ref[pl.ds(off, tk), :], qs[j])

        outs = _pipe_flash(nch, 2, qk, lambda c, j: _with_ones(vt_ref[c, j * HEAD_DIM:(j + 1) * HEAD_DIM, :]),
                           lambda c, j: s_ref[c], bufs, HEAD_DIM + DV_PAD, tk, tq)
        o = jnp.concatenate([_normalize_t(outs[0], HEAD_DIM), _normalize_t(outs[1], HEAD_DIM)], axis=0)
        o_ref[:, p * LANES:(p + 1) * LANES] = o.T.astype(o_ref.dtype)


def _dsa_attention(rp, vt, iw, tq=TQ):
    b, s, _ = rp.shape
    ksel = min(DSA_TOPK, s // 4)
    nk = s // tq
    return pl.pallas_call(
        functools.partial(_dsa_kernel, tq=tq, ksel=ksel, seq_len=s),
        grid=(b, nk),
        in_specs=[pl.BlockSpec((None, tq, 2 * LANES), lambda bi, i: (bi, i, 6)),
                  pl.BlockSpec((None, s, LANES), lambda bi, i: (bi, 0, 15)),
                  pl.BlockSpec((None, tq, LANES), lambda bi, i: (bi, i, 0)),
                  pl.BlockSpec((None, tq, 4 * LANES), lambda bi, i: (bi, i, 2)),
                  pl.BlockSpec((None, s, LANES), lambda bi, i: (bi, 0, 14)),
                  pl.BlockSpec((None, nk, LANES, tq), lambda bi, i: (bi, 0, 4, 0))],
        out_specs=pl.BlockSpec((None, tq, 4 * LANES), lambda bi, i: (bi, i, 0)),
        out_shape=jax.ShapeDtypeStruct((b, s, 4 * LANES), BF16),
        scratch_shapes=[pltpu.VMEM((nk, tq, tq), F32), pltpu.VMEM((8, tq), I32)] + _pipe_scratch(2, tq, tq),
        compiler_params=_cparams(("parallel", "arbitrary")),
        name="dsa_attn",
    )(rp, rp, iw, rp, rp, vt)


def _moba_kernel(q_ref, k_ref, vt_ref, o_ref, km_ref, sel_ref, *bufs, seq_len, n_sel):
    tq = MOBA_BLOCK
    qb = pl.program_id(2)

    @pl.when(qb == 0)
    def _():
        j = lax.broadcasted_iota(I32, (LANES, seq_len), 0)
        s = lax.broadcasted_iota(I32, (LANES, seq_len), 1)
        avg = jnp.where(s // MOBA_BLOCK == j, 1.0 / MOBA_BLOCK, 0.0).astype(BF16)
        km_ref[...] = jnp.dot(avg, k_ref[...], preferred_element_type=F32)

    nbp = sel_ref.shape[1]
    km = km_ref[0:nbp, :]
    qs = _split_halves(q_ref[...], roped=True)
    blk = lax.broadcasted_iota(I32, (nbp, tq), 0)
    past = blk < qb
    for j in range(2):
        gate = lax.dot_general(km, qs[j].astype(F32), (((1,), (1,)), ((), ())),
                               precision=lax.Precision.HIGHEST, preferred_element_type=F32)
        gate = jnp.where(blk < seq_len // MOBA_BLOCK, jnp.where(past, gate, NEG), NINF)
        sel = _top_n_mask(gate, n_sel, 0)
        sel_ref[j] = jnp.where(past, jnp.where(sel > 0.5, 0.0, NINF), NINF)
    own = _causal_t(tq)
    n = qb + 1

    def qk(c, j):
        off = pl.multiple_of(c * tq, tq)
        return _dot_nt(k_ref[pl.ds(off, tq), :], qs[j])

    def bias_at(c, j):
        chosen = sel_ref[j, pl.ds(c, 1), :]
        return jnp.where(c == qb, own, chosen)

    outs = _pipe_flash(n, 2, qk, lambda c, j: _with_ones(vt_ref[c, j * HEAD_DIM:(j + 1) * HEAD_DIM, :]), bias_at, bufs,
                       HEAD_DIM + DV_PAD, tq, tq)
    o = jnp.concatenate([_normalize_t(outs[0], HEAD_DIM), _normalize_t(outs[1], HEAD_DIM)], axis=0)
    o_ref[...] = o.T.astype(o_ref.dtype)


def _moba_attention(rp, vt):
    b, s, _ = rp.shape
    tq = MOBA_BLOCK
    nb = s // tq
    n_sel = max(1, min(MOBA_TOPK, nb - 1))
    npair = C_HEADS // 2
    return pl.pallas_call(
        functools.partial(_moba_kernel, seq_len=s, n_sel=n_sel),
        grid=(b, npair, nb),
        in_specs=[pl.BlockSpec((None, tq, LANES), lambda bi, h, i: (bi, i, h)),
                  pl.BlockSpec((None, s, LANES), lambda bi, h, i: (bi, 0, npair + h)),
                  pl.BlockSpec((None, nb, LANES, tq), lambda bi, h, i: (bi, 0, h, 0))],
        out_specs=pl.BlockSpec((None, tq, LANES), lambda bi, h, i: (bi, i, h)),
        out_shape=jax.ShapeDtypeStruct((b, s, npair * LANES), BF16),
        scratch_shapes=[pltpu.VMEM((LANES, LANES), F32), pltpu.VMEM((2, -(-nb // 8) * 8, tq), F32)]
        + _pipe_scratch(2, tq, tq),
        compiler_params=_cparams(("parallel", "parallel", "arbitrary")),
        name="moba_attn",
    )(rp, rp, vt)


def _cmp_kernel(r_ref, pe_ref, w1_ref, w2_ref, o_ref):
    r = r_ref[...]
    w1 = w1_ref[...]
    half = r.shape[1]
    u = jnp.dot(r, w1[:half], preferred_element_type=F32)
    v = jnp.dot(r, w1[half:], preferred_element_type=F32)
    c = jnp.dot(pe_ref[...], w1, preferred_element_type=F32)[0:1]
    pre = u + pltpu.roll(v, r.shape[0] - 1, 0) + c
    o_ref[...] = jnp.dot(jax.nn.gelu(pre).astype(BF16), w2_ref[...],
                         preferred_element_type=F32).astype(o_ref.dtype)


def _nsa_compress(r, pe, w1, w2):
    b, _, nc, wdt = r.shape
    hid = w1.shape[2]
    return pl.pallas_call(
        _cmp_kernel,
        grid=(b, 4),
        in_specs=[pl.BlockSpec((None, None, nc, wdt), lambda bi, t: (bi, t, 0, 0)),
                  pl.BlockSpec((None, 8, 2 * wdt), lambda bi, t: (t // 2, 0, 0)),
                  pl.BlockSpec((None, 2 * wdt, hid), lambda bi, t: (t // 2, 0, 0)),
                  pl.BlockSpec((None, hid, HEAD_DIM), lambda bi, t: (t // 2, 0, 0))],
        out_specs=pl.BlockSpec((None, None, nc, HEAD_DIM), lambda bi, t: (bi, t, 0, 0)),
        out_shape=jax.ShapeDtypeStruct((b, 4, nc, HEAD_DIM), BF16),
        compiler_params=_cparams(("parallel", "arbitrary")),
        name="nsa_compress",
    )(r, pe, w1, w2)


def _nsa_kernel(qr_ref, qw_ref, dg_ref, gb_ref, kc_ref, vct_ref, ks_ref, vst_ref, kw_ref, vwt_ref,
                o_ref, sel_ref, wb_ref, *bufs, tq, seq_len):
    tk = tq
    i = pl.program_id(1)
    nch = i + 1
    nc = seq_len // NSA_CMP_STRIDE
    n_sb = seq_len // NSA_SLC_BLOCK
    n_sel = min(NSA_SLC_TOPK, n_sb)
    k_loc = lax.broadcasted_iota(I32, (tk, tq), 0)
    q_loc = lax.broadcasted_iota(I32, (tk, tq), 1)
    qpos = i * tq + lax.broadcasted_iota(I32, (1, tq), 1)

    gates_t = jax.nn.sigmoid(dg_ref[...] + gb_ref[...]).T
    kc = kc_ref[...]
    cmp_end = lax.broadcasted_iota(I32, (nc, 1), 0) * NSA_CMP_STRIDE + (NSA_CMP_LEN - 1)
    cbias = jnp.where(cmp_end <= qpos, 0.0, NINF)

    nh = D_HEADS // 2
    q_rot = [_split_halves(qr_ref[:, p * LANES:(p + 1) * LANES], roped=True) for p in range(nh)]
    q_raw = [_split_halves(qw_ref[:, p * LANES:(p + 1) * LANES]) for p in range(nh)]
    cbias4 = jnp.concatenate([cbias] * nh, axis=1)
    o_cmp, psum = [], []
    for g in range(2):
        qg = jnp.concatenate([q_raw[p][g] for p in range(nh)], axis=0)
        s = _dot_nt(kc, qg) + cbias4
        m = jnp.max(s, axis=0, keepdims=True)
        e = jnp.exp2(s - jnp.where(m == NINF, 0.0, m))
        pc = e * _safe_recip(jnp.sum(e, axis=0, keepdims=True))
        psum.append(sum(pc[:, p * tq:(p + 1) * tq] for p in range(nh)))
        o_cmp.append(jnp.dot(vct_ref[g], pc.astype(BF16), preferred_element_type=F32))

    per = tk // NSA_SLC_BLOCK
    nbp = -(-n_sb // 8) * 8
    cn = lax.broadcasted_iota(I32, (nbp, nc), 1) * NSA_CMP_STRIDE
    sj = lax.broadcasted_iota(I32, (nbp, nc), 0) * NSA_SLC_BLOCK
    shares = jnp.where((cn <= sj + NSA_SLC_BLOCK - 1) & (cn + NSA_CMP_LEN - 1 >= sj), 1.0, 0.0)
    blk = lax.broadcasted_iota(I32, (nbp, tq), 0)
    cur = qpos // NSA_SLC_BLOCK
    causal_b = blk <= cur
    forced = (blk == 0) | ((blk >= cur - 1) & causal_b)
    for g in range(2):
        imp = jnp.dot(shares, psum[g], precision=lax.Precision.HIGHEST, preferred_element_type=F32)
        val = jnp.where(forced, BIG, jnp.where(causal_b, imp, NEG))
        val = jnp.where(blk < n_sb, val, NINF)
        rowb = jnp.where(_top_n_mask(val, n_sel, 0) > 0.5, 0.0, NINF)
        for c in range(seq_len // tk):
            sel_ref[g, c] = jnp.concatenate([rowb[c * per:(c + 1) * per], jnp.zeros((8 - per, tq), F32)], axis=0)

    wb_ref[0] = jnp.where(k_loc <= q_loc, 0.0, NINF)
    wb_ref[1] = jnp.zeros((tk, tq), F32)
    wb_ref[2] = jnp.where(k_loc > q_loc, 0.0, NINF)
    n_wc = NSA_WINDOW // tk + 1
    w_first = jnp.maximum(i - (n_wc - 1), 0)
    n_w = i - w_first + 1

    for p in range(D_HEADS // 2):
        qs = q_rot[p]

        def qk_s(c, j, qs=qs):
            off = pl.multiple_of(c * tk, tk)
            return _dot_nt(ks_ref[pl.ds(off, tk), :], qs[j])

        def bias_s(c, j):
            rows = sel_ref[j, c]
            tile = jnp.concatenate([jnp.broadcast_to(rows[r:r + 1], (NSA_SLC_BLOCK, tq)) for r in range(per)], axis=0)
            return tile + wb_ref[jnp.where(c == i, 0, 1)]

        o_slc = _pipe_flash(nch, 2, qk_s, lambda c, j: _with_ones(vst_ref[c, j * HEAD_DIM:(j + 1) * HEAD_DIM, :]),
                            bias_s, bufs, HEAD_DIM + DV_PAD, tk, tq)

        def qk_w(c, j, qs=qs):
            off = pl.multiple_of((w_first + c) * tk, tk)
            return _dot_nt(kw_ref[pl.ds(off, tk), :], qs[j])

        def bias_w(c, j):
            return wb_ref[i - (w_first + c)]

        o_win = _pipe_flash(n_w, 2, qk_w,
                            lambda c, j: _with_ones(vwt_ref[w_first + c, j * HEAD_DIM:(j + 1) * HEAD_DIM, :]), bias_w, bufs,
                            HEAD_DIM + DV_PAD, tk, tq)
        outs = []
        for g in range(2):
            h = g * (D_HEADS // 2) + p
            outs.append(gates_t[3 * h:3 * h + 1] * o_cmp[g][:, p * tq:(p + 1) * tq]
                        + gates_t[3 * h + 1:3 * h + 2] * _normalize_t(o_slc[g], HEAD_DIM)
                        + gates_t[3 * h + 2:3 * h + 3] * _normalize_t(o_win[g], HEAD_DIM))
        o_ref[:, p * LANES:(p + 1) * LANES] = jnp.concatenate(outs, axis=0).T.astype(o_ref.dtype)


def _nsa_attention(rp, pp, vt, dg, gate_b, kcmp, vcmp, tq=TQ):
    b, s, _ = rp.shape
    nk = s // tq
    nc = s // NSA_CMP_STRIDE
    assert NSA_WINDOW == 2 * tq
    n_wc = NSA_WINDOW // tq + 1
    vct = vcmp.reshape(b, nc, 2, HEAD_DIM).transpose(0, 2, 3, 1)
    full = lambda t: pl.BlockSpec((None, s, LANES), lambda bi, i: (bi, 0, t))
    vspec = lambda t: pl.BlockSpec((None, nk, LANES, tq), lambda bi, i: (bi, 0, t, 0))
    return pl.pallas_call(
        functools.partial(_nsa_kernel, tq=tq, seq_len=s),
        grid=(b, nk),
        in_specs=[pl.BlockSpec((None, tq, 4 * LANES), lambda bi, i: (bi, i, 2)),
                  pl.BlockSpec((None, tq, 4 * LANES), lambda bi, i: (bi, i, 0)),
                  pl.BlockSpec((None, tq, LANES), lambda bi, i: (bi, i, 0)),
                  pl.BlockSpec((1, LANES), lambda bi, i: (0, 0)),
                  pl.BlockSpec((None, nc, LANES), lambda bi, i: (bi, 0, 0)),
                  pl.BlockSpec((None, 2, HEAD_DIM, nc), lambda bi, i: (bi, 0, 0, 0)),
                  full(12), vspec(4), full(13), vspec(5)],
        out_specs=pl.BlockSpec((None, tq, 4 * LANES), lambda bi, i: (bi, i, 0)),
        out_shape=jax.ShapeDtypeStruct((b, s, 4 * LANES), BF16),
        scratch_shapes=[pltpu.VMEM((2, nk, 8, tq), F32), pltpu.VMEM((n_wc, tq, tq), F32)]
        + _pipe_scratch(2, tq, tq),
        compiler_params=_cparams(("parallel", "arbitrary")),
        name="nsa_attn",
    )(rp, pp, dg, gate_b, kcmp, vct, rp, vt, rp, vt)


def _cast_kernel(x_ref, o_ref):
    o_ref[...] = x_ref[...].astype(o_ref.dtype)


def _layer_to_bf16(w, layer):
    _, e, r, c = w.shape
    tr = 1 << ((2 ** 21 // c).bit_length() - 1)
    out = pl.pallas_call(
        _cast_kernel,
        grid=(e * r // tr,),
        in_specs=[pl.BlockSpec((None, tr, c), lambda i: (layer, i, 0))],
        out_specs=pl.BlockSpec((tr, c), lambda i: (i, 0)),
        out_shape=jax.ShapeDtypeStruct((e * r, c), BF16),
        compiler_params=_cparams(("parallel",)),
        name="cast_bf16",
    )(w.reshape(w.shape[0], e * r, c))
    return out.reshape(e, r, c)


def _moe_ffn_kernel(be_ref, nu_ref, x_ref, wg_ref, wu_ref, wd_ref, o_ref):
    used = pl.program_id(0) < nu_ref[0]

    @pl.when(used)
    def _():
        x = x_ref[...]
        h = jax.nn.silu(jnp.dot(x, wg_ref[...], preferred_element_type=F32)) * jnp.dot(
            x, wu_ref[...], preferred_element_type=F32)
        o_ref[...] = jnp.dot(h.astype(BF16), wd_ref[...], preferred_element_type=F32).astype(o_ref.dtype)

    @pl.when(jnp.logical_not(used))
    def _():
        o_ref[...] = jnp.zeros(o_ref.shape, o_ref.dtype)


def _moe_ffn(x_sorted, block_e, n_used, wg, wu, wd):
    ns, d = x_sorted.shape
    fdim = wg.shape[2]
    once = pl.Buffered(1)
    grid_spec = pltpu.PrefetchScalarGridSpec(
        num_scalar_prefetch=2,
        grid=(ns // MOE_TM,),
        in_specs=[pl.BlockSpec((MOE_TM, d), lambda i, be, nu: (jnp.minimum(i, nu[0] - 1), 0)),
                  pl.BlockSpec((None, d, fdim), lambda i, be, nu: (be[i], 0, 0), pipeline_mode=once),
                  pl.BlockSpec((None, d, fdim), lambda i, be, nu: (be[i], 0, 0), pipeline_mode=once),
                  pl.BlockSpec((None, fdim, d), lambda i, be, nu: (be[i], 0, 0), pipeline_mode=once)],
        out_specs=pl.BlockSpec((MOE_TM, d), lambda i, be, nu: (i, 0)),
    )
    return pl.pallas_call(
        _moe_ffn_kernel,
        grid_spec=grid_spec,
        out_shape=jax.ShapeDtypeStruct((ns, d), BF16),
        compiler_params=_cparams(("arbitrary",)),
        name="moe_ffn",
    )(block_e, n_used, x_sorted, wg, wu, wd)


def _combine_ln_kernel(x_ref, y0_ref, y1_ref, rt_ref, g_ref, b_ref, xo_ref, xb_ref):
    rt = rt_ref[...]
    ffn = rt[:, 2:3] * y0_ref[...].astype(F32) + rt[:, 3:4] * y1_ref[...].astype(F32)
    y = _layer_norm(DN_ALPHA * x_ref[...] + ffn, g_ref[...], b_ref[...])
    xo_ref[...] = y
    xb_ref[...] = y.astype(BF16)


def _combine_ln(x, y0, y1, rt, g, b, tm=ROW_TILE):
    n, d = x.shape
    row = lambda i: (i, 0)
    fixed = lambda i: (0, 0)
    return pl.pallas_call(
        _combine_ln_kernel,
        grid=(n // tm,),
        in_specs=[pl.BlockSpec((tm, d), row), pl.BlockSpec((tm, d), row), pl.BlockSpec((tm, d), row),
                  pl.BlockSpec((tm, LANES), row), pl.BlockSpec((1, d), fixed), pl.BlockSpec((1, d), fixed)],
        out_specs=[pl.BlockSpec((tm, d), row), pl.BlockSpec((tm, d), row)],
        out_shape=[jax.ShapeDtypeStruct((n, d), F32), jax.ShapeDtypeStruct((n, d), BF16)],
        compiler_params=_cparams(("parallel",)),
        name="moe_combine_ln",
    )(x, y0, y1, rt, g.reshape(1, d), b.reshape(1, d))


def _moe_layout(rt, n):
    e_flat = rt[:, 0:TOP_K].astype(I32).reshape(-1)
    nk = n * TOP_K
    onehot = (e_flat[:, None] == jnp.arange(N_EXPERTS, dtype=I32)[None, :]).astype(I32)
    rank = jnp.take_along_axis(jnp.cumsum(onehot, axis=0), e_flat[:, None], axis=1)[:, 0] - 1
    counts = jnp.sum(onehot, axis=0)
    padded = (counts + MOE_TM - 1) // MOE_TM * MOE_TM
    pad_end = jnp.cumsum(padded)
    pad_start = pad_end - padded
    grp_start = jnp.cumsum(counts) - counts
    slot = pad_start[e_flat] + rank
    n_blocks = -(-nk // MOE_TM) + N_EXPERTS
    n_slots = n_blocks * MOE_TM
    order = jnp.argsort(e_flat, stable=True).astype(I32)
    sl = jnp.arange(n_slots, dtype=I32)
    slot_e = jnp.minimum(jnp.searchsorted(pad_end, sl, side='right'), N_EXPERTS - 1).astype(I32)
    within = sl - pad_start[slot_e]
    valid = within < counts[slot_e]
    src = jnp.where(valid, grp_start[slot_e] + within, 0)
    slot_tok = jnp.where(valid, order[src] // TOP_K, 0)
    n_used = (pad_end[-1] // MOE_TM).astype(I32).reshape(1)
    blk = jnp.arange(n_blocks, dtype=I32)
    block_e = slot_e[jnp.minimum(blk, n_used[0] - 1) * MOE_TM]
    return slot_tok, slot.reshape(n, TOP_K), block_e, n_used


def _pair_perm(n_heads):
    half = n_heads // 2
    cols = []
    for p in range(half):
        cols += list(range(p * HEAD_DIM, (p + 1) * HEAD_DIM))
        cols += list(range((half + p) * HEAD_DIM, (half + p + 1) * HEAD_DIM))
    return np.asarray(cols, dtype=np.int32)


def _pad_cols(w, width):
    return jnp.pad(w, ((0, 0), (0, width - w.shape[1])))


def _split_cols(w, sizes):
    out, off = [], 0
    for sz in sizes:
        out.append(w[:, off:off + sz])
        off += sz
    return out


def _even_layer(x, xb, w_in, w_out, lam_params, subln, lam_init, wg, wu, wd, ln, tabs, bsz, seq_len):
    n = x.shape[0]
    perm = _pair_perm(B_HEADS)
    aq, ak, av, bq, bk, bv, iq, ik, iw = _split_cols(w_in, EVEN_SIZES)
    w_rope = jnp.concatenate([aq * SCALE, ak, bq[:, perm] * SCALE, iq, bk, ik, ik], axis=1)
    w_rope = w_rope[:, _rope_layout(w_rope.shape[1])].astype(BF16)
    w_val = jnp.concatenate([av, bv], axis=1).astype(BF16)
    w_iw = _pad_cols(iw, LANES).astype(BF16)
    rp = _proj_rope(xb, w_rope, ROW_TILE, w_rope.shape[1] // 2, seq_len, tabs).reshape(bsz, seq_len, -1)
    vt, iwv = _proj_values(xb, w_val, w_iw, None, ROW_TILE, TQ)
    vt = vt.reshape(bsz, seq_len // TQ, -1, TQ)
    iwv = iwv.reshape(bsz, seq_len, LANES)
    o_a = _diff_attention(rp, vt, lam_params, subln, lam_init)
    o_b = _dsa_attention(rp, vt, iwv)
    half = w_out.shape[0] // 2
    wo_a = w_out[:half].astype(BF16)
    wo_b = w_out[half:][perm].astype(BF16)
    g_mix, b_mix, g_ffn, b_ffn = ln
    x1, x1b = _outproj_ln(x, o_a.reshape(n, -1), o_b.reshape(n, -1), wo_a, wo_b, g_mix, b_mix)
    return _ffn_ln(x1, x1b, wg.astype(BF16), wu.astype(BF16), wd.astype(BF16), g_ffn, b_ffn)


def _odd_layer(x, xb, w_in, w_out, gate_b, pe, phi_w1, phi_w2, w_router, b_router, wg, wu, wd, ln, tabs,
               bsz, seq_len):
    n = x.shape[0]
    perm = _pair_perm(D_HEADS)
    cq, ck, cv, dq, dkc, dvc, dks, dvs, dkw, dvw, dg = _split_cols(w_in, ODD_SIZES)
    dq = dq[:, perm] * SCALE
    w_rope = jnp.concatenate([cq * SCALE, ck, dq, dks, dkw], axis=1)
    w_rope = w_rope[:, _rope_layout(w_rope.shape[1])].astype(BF16)
    w_plain = jnp.concatenate([dq, dkc, dvc], axis=1).astype(BF16)
    w_val = jnp.concatenate([cv, dvs, dvw], axis=1).astype(BF16)
    w_dg = _pad_cols(dg, LANES).astype(BF16)
    rp = _proj_rope(xb, w_rope, ROW_TILE, w_rope.shape[1] // 2, seq_len, tabs).reshape(bsz, seq_len, -1)
    vt, dgv, pp = _proj_values(xb, w_val, w_dg, w_plain, ROW_TILE, TQ)
    vt = vt.reshape(bsz, seq_len // TQ, -1, TQ)
    dgv = dgv.reshape(bsz, seq_len, LANES)
    pp = pp.reshape(bsz, seq_len, -1)

    o_c = _moba_attention(rp, vt)

    nc = seq_len // NSA_CMP_STRIDE
    tok = pp[:, :, 4 * LANES:6 * LANES].reshape(bsz, nc, NSA_CMP_STRIDE, 4, HEAD_DIM)
    r = tok.transpose(0, 3, 1, 2, 4).reshape(bsz, 4, nc, NSA_CMP_STRIDE * HEAD_DIM)
    pe_flat = jnp.pad(pe.reshape(2, 1, -1), ((0, 0), (0, 7), (0, 0))).astype(BF16)
    cmp = _nsa_compress(r, pe_flat, phi_w1.astype(BF16), phi_w2.astype(BF16))
    kcmp = jnp.concatenate([cmp[:, 0], cmp[:, 1]], axis=-1)
    vcmp = jnp.concatenate([cmp[:, 2], cmp[:, 3]], axis=-1)
    gb = _pad_cols(gate_b.reshape(1, -1), LANES)
    o_d = _nsa_attention(rp, pp, vt, dgv, gb, kcmp, vcmp)

    half = w_out.shape[0] // 2
    wo_c = w_out[:half].astype(BF16)
    wo_d = w_out[half:][perm].astype(BF16)
    g_mix, b_mix, g_ffn, b_ffn = ln
    router = (_pad_cols(w_router, LANES), _pad_cols(b_router.reshape(1, -1), LANES))
    x1, x1b, rt = _outproj_ln(x, o_c.reshape(n, -1), o_d.reshape(n, -1), wo_c, wo_d, g_mix, b_mix, router)

    slot_tok, slot, block_e, n_used = _moe_layout(rt, n)
    y_slots = _moe_ffn(x1b[slot_tok], block_e, n_used, wg, wu, wd)
    return _combine_ln(x1, y_slots[slot[:, 0]], y_slots[slot[:, 1]], rt, g_ffn, b_ffn)


@jax.jit
def kernel(x, ev_w_in, ev_w_out, dif_lambda, dif_subln, ffd_w_gate, ffd_w_up, ffd_w_down, od_w_in, od_w_out,
           nsa_gate_b, nsa_pe, nsa_phi_w1, nsa_phi_w2, moe_w_router, moe_b_router, moe_w_gate, moe_w_up,
           moe_w_down, ln_mix_g, ln_mix_b, ln_ffn_g, ln_ffn_b):
    bsz, seq_len, d = x.shape
    tabs = _rope_tables(seq_len)
    xf = x.reshape(bsz * seq_len, d)
    xb = xf.astype(BF16)
    for l in range(DEPTH):
        i = l // 2
        ln = (ln_mix_g[l], ln_mix_b[l], ln_ffn_g[l], ln_ffn_b[l])
        if l % 2 == 0:
            lam_init = 0.8 - 0.6 * math.exp(-0.3 * l)
            xf, xb = _even_layer(xf, xb, ev_w_in[i], ev_w_out[i], dif_lambda[i], dif_subln[i], lam_init,
                                 ffd_w_gate[i], ffd_w_up[i], ffd_w_down[i], ln, tabs, bsz, seq_len)
        else:
            xf, xb = _odd_layer(xf, xb, od_w_in[i], od_w_out[i], nsa_gate_b[i], nsa_pe[i], nsa_phi_w1[i],
                                nsa_phi_w2[i], moe_w_router[i], moe_b_router[i], _layer_to_bf16(moe_w_gate, i),
                                _layer_to_bf16(moe_w_up, i), _layer_to_bf16(moe_w_down, i), ln, tabs, bsz, seq_len)
    return xf.reshape(bsz, seq_len, d)
```

```python
import functools
import math

import numpy as np
import jax
import jax.numpy as jnp
from jax import lax
from jax.experimental import pallas as pl
from jax.experimental.pallas import tpu as pltpu

F32 = jnp.float32
BF16 = jnp.bfloat16
I32 = jnp.int32

LANES = 128
VMEM_LIMIT = 56 * 1024 * 1024

DEPTH = 4
HEAD_DIM = 64
ROPE_THETA = 10000.0
LN_EPS = 1e-5
DN_ALPHA = (2 * DEPTH) ** 0.25
SCALE = HEAD_DIM ** -0.5 * math.log2(math.e)
NEG = -1e30
BIG = 1e30
M_INIT = -1e30
NINF = float("-inf")

A_HEADS = 4
B_HEADS = 8
IDX_HEADS = 4
DSA_TOPK = 256
C_HEADS = 8
MOBA_BLOCK = 256
MOBA_TOPK = 3
D_HEADS = 8
NSA_CMP_LEN = 32
NSA_CMP_STRIDE = 16
NSA_SLC_BLOCK = 64
NSA_SLC_TOPK = 16
NSA_WINDOW = 512
N_EXPERTS = 8
TOP_K = 2
MOE_TM = 512
N_BISECT = 14
ROW_TILE = 512
KV_GROUPS = 2

EVEN_SIZES = (A_HEADS * 2 * HEAD_DIM, A_HEADS * 2 * HEAD_DIM, A_HEADS * 2 * HEAD_DIM, B_HEADS * HEAD_DIM,
              KV_GROUPS * HEAD_DIM, KV_GROUPS * HEAD_DIM, IDX_HEADS * HEAD_DIM, HEAD_DIM, IDX_HEADS)
ODD_SIZES = (C_HEADS * HEAD_DIM,) * 3 + (D_HEADS * HEAD_DIM,) + (KV_GROUPS * HEAD_DIM,) * 6 + (D_HEADS * 3,)


def _cparams(sem):
    return pltpu.CompilerParams(dimension_semantics=sem, vmem_limit_bytes=VMEM_LIMIT)


def _dot_nt(a, b):
    return lax.dot_general(a, b, (((1,), (1,)), ((), ())), preferred_element_type=F32)


def _layer_norm(y, g, b):
    mu = jnp.mean(y, axis=-1, keepdims=True)
    yc = y - mu
    var = jnp.mean(yc * yc, axis=-1, keepdims=True)
    return yc * lax.rsqrt(var + LN_EPS) * g + b


def _safe_recip(l):
    return jnp.where(l > 0.0, 1.0 / jnp.where(l > 0.0, l, 1.0), 0.0)


def _split_halves(t, roped=False):
    lane = lax.broadcasted_iota(I32, (1, LANES), 1)
    lo = (lane // (HEAD_DIM // 2)) % 2 == 0 if roped else lane < HEAD_DIM
    z = jnp.zeros_like(t)
    return jnp.where(lo, t, z), jnp.where(lo, z, t)


def _rope_layout(n_cols):
    q = HEAD_DIM // 2
    tile = np.concatenate([np.arange(0, q), np.arange(2 * q, 3 * q), np.arange(q, 2 * q), np.arange(3 * q, 4 * q)])
    return (np.arange(0, n_cols, LANES)[:, None] + tile[None, :]).reshape(-1).astype(np.int32)


def _rope_mm_kernel(x_ref, w_ref, cos_ref, sin_ref, o_ref):
    acc = jnp.dot(x_ref[...].astype(BF16), w_ref[...], preferred_element_type=F32)
    cos = cos_ref[...]
    sin = sin_ref[...]
    for c in range(acc.shape[1] // LANES):
        a = acc[:, c * LANES:(c + 1) * LANES]
        rot = pltpu.roll(a, LANES // 2, 1)
        o_ref[:, c * LANES:(c + 1) * LANES] = (a * cos + rot * sin).astype(o_ref.dtype)


def _proj_rope(x, w, tm, tn, seq_len, rope_tabs):
    n, d = x.shape
    p = w.shape[1]
    nt = seq_len // tm
    return pl.pallas_call(
        _rope_mm_kernel,
        grid=(n // tm, p // tn),
        in_specs=[pl.BlockSpec((tm, d), lambda i, j: (i, 0)), pl.BlockSpec((d, tn), lambda i, j: (0, j)),
                  pl.BlockSpec((tm, LANES), lambda i, j: (i % nt, 0)),
                  pl.BlockSpec((tm, LANES), lambda i, j: (i % nt, 0))],
        out_specs=pl.BlockSpec((tm, tn), lambda i, j: (i, j)),
        out_shape=jax.ShapeDtypeStruct((n, p), BF16),
        compiler_params=_cparams(("parallel", "arbitrary")),
        name="proj_rope",
    )(x, w, *rope_tabs)


def _rope_tables(seq_len):
    d = HEAD_DIM
    inv = ROPE_THETA ** (-jnp.arange(0, d, 2, dtype=F32) / d)
    ang = jnp.arange(seq_len, dtype=I32).astype(F32)[:, None] * inv[None, :]
    cos = jnp.cos(ang)
    sin = jnp.sin(ang)
    cos128 = jnp.tile(cos, (1, LANES // (d // 2)))
    sin128 = jnp.concatenate([-sin, -sin, sin, sin], axis=1)
    return cos128, sin128


def _route_top2(x, w, b):
    xh, wh = x.astype(BF16), w.astype(BF16)
    xl, wl = (x - xh.astype(F32)).astype(BF16), (w - wh.astype(F32)).astype(BF16)
    logits = (jnp.dot(xh, wh, preferred_element_type=F32) + jnp.dot(xh, wl, preferred_element_type=F32)
              + jnp.dot(xl, wh, preferred_element_type=F32)) + b
    lane = lax.broadcasted_iota(I32, (1, LANES), 1)
    lanef = lane.astype(F32)
    v = jnp.where(lane < N_EXPERTS, logits, NINF)
    l0 = jnp.max(v, axis=1, keepdims=True)
    i0 = jnp.min(jnp.where(v == l0, lanef, float(LANES)), axis=1, keepdims=True)
    v = jnp.where(lanef == i0, NINF, v)
    l1 = jnp.max(v, axis=1, keepdims=True)
    i1 = jnp.min(jnp.where(v == l1, lanef, float(LANES)), axis=1, keepdims=True)
    e1 = jnp.exp(l1 - l0)
    g0 = 1.0 / (1.0 + e1)
    g1 = e1 / (1.0 + e1)
    return jnp.where(lane == 0, i0, jnp.where(lane == 1, i1, jnp.where(lane == 2, g0, jnp.where(lane == 3, g1, 0.0))))


def _outproj_ln_kernel(x_ref, a_ref, b_ref, wa_ref, wb_ref, g_ref, bb_ref, *rest):
    mix = (jnp.dot(a_ref[...], wa_ref[...], preferred_element_type=F32)
           + jnp.dot(b_ref[...], wb_ref[...], preferred_element_type=F32))
    y = _layer_norm(DN_ALPHA * x_ref[...] + mix, g_ref[...], bb_ref[...])
    if len(rest) == 2:
        xo_ref, xb_ref = rest
    else:
        wr_ref, br_ref, xo_ref, xb_ref, rt_ref = rest
        rt_ref[...] = _route_top2(y, wr_ref[...], br_ref[...])
    xo_ref[...] = y
    xb_ref[...] = y.astype(BF16)


def _outproj_ln(x, oa, ob, wa, wb, g, b, router=None, tm=ROW_TILE):
    n, d = x.shape
    ka, kb = oa.shape[1], ob.shape[1]
    row = lambda i: (i, 0)
    fixed = lambda i: (0, 0)
    in_specs = [pl.BlockSpec((tm, d), row), pl.BlockSpec((tm, ka), row), pl.BlockSpec((tm, kb), row),
                pl.BlockSpec((ka, d), fixed), pl.BlockSpec((kb, d), fixed),
                pl.BlockSpec((1, d), fixed), pl.BlockSpec((1, d), fixed)]
    out_specs = [pl.BlockSpec((tm, d), row), pl.BlockSpec((tm, d), row)]
    out_shape = [jax.ShapeDtypeStruct((n, d), F32), jax.ShapeDtypeStruct((n, d), BF16)]
    args = [x, oa, ob, wa, wb, g.reshape(1, d), b.reshape(1, d)]
    if router is not None:
        in_specs += [pl.BlockSpec((d, LANES), fixed), pl.BlockSpec((1, LANES), fixed)]
        out_specs.append(pl.BlockSpec((tm, LANES), row))
        out_shape.append(jax.ShapeDtypeStruct((n, LANES), F32))
        args += list(router)
    return pl.pallas_call(
        _outproj_ln_kernel,
        grid=(n // tm,),
        in_specs=in_specs,
        out_specs=out_specs,
        out_shape=out_shape,
        compiler_params=_cparams(("parallel",)),
        name="outproj_ln",
    )(*args)


def _ffn_ln_kernel(x_ref, xb_ref, wg_ref, wu_ref, wd_ref, g_ref, b_ref, xo_ref, xob_ref):
    xb = xb_ref[...]
    h = jax.nn.silu(jnp.dot(xb, wg_ref[...], preferred_element_type=F32)) * jnp.dot(
        xb, wu_ref[...], preferred_element_type=F32)
    ffn = jnp.dot(h.astype(BF16), wd_ref[...], preferred_element_type=F32)
    y = _layer_norm(DN_ALPHA * x_ref[...] + ffn, g_ref[...], b_ref[...])
    xo_ref[...] = y
    xob_ref[...] = y.astype(BF16)


def _ffn_ln(x, xb, wg, wu, wd, g, b, tm=ROW_TILE):
    n, d = x.shape
    fdim = wg.shape[1]
    row = lambda i: (i, 0)
    fixed = lambda i: (0, 0)
    once = pl.Buffered(1)
    return pl.pallas_call(
        _ffn_ln_kernel,
        grid=(n // tm,),
        in_specs=[pl.BlockSpec((tm, d), row), pl.BlockSpec((tm, d), row),
                  pl.BlockSpec((d, fdim), fixed, pipeline_mode=once),
                  pl.BlockSpec((d, fdim), fixed, pipeline_mode=once),
                  pl.BlockSpec((fdim, d), fixed, pipeline_mode=once),
                  pl.BlockSpec((1, d), fixed), pl.BlockSpec((1, d), fixed)],
        out_specs=[pl.BlockSpec((tm, d), row), pl.BlockSpec((tm, d), row)],
        out_shape=[jax.ShapeDtypeStruct((n, d), F32), jax.ShapeDtypeStruct((n, d), BF16)],
        compiler_params=_cparams(("parallel",)),
        name="ffn_ln",
    )(x, xb, wg, wu, wd, g.reshape(1, d), b.reshape(1, d))


def _top_n_mask(v, n, axis):
    idx = lax.broadcasted_iota(I32, v.shape, axis).astype(F32)
    sel = jnp.zeros(v.shape, F32)
    for _ in range(n):
        mx = jnp.max(v, axis=axis, keepdims=True)
        first = jnp.min(jnp.where(v == mx, idx, float(v.shape[axis])), axis=axis, keepdims=True)
        pick = idx == first
        sel = jnp.where(pick, 1.0, sel)
        v = jnp.where(pick, NINF, v)
    return sel


DV_PAD = 16
TQ = 256


def _with_ones(vt):
    return jnp.concatenate([vt, jnp.ones((DV_PAD, vt.shape[1]), vt.dtype)], axis=0)


def _mm_t_kernel(x_ref, wv_ref, ws_ref, *rest, tk):
    x = x_ref[...].astype(BF16)
    vt_ref, small_ref = rest[-2:] if len(rest) == 2 else rest[1:3]
    acc = jnp.dot(x, wv_ref[...], preferred_element_type=F32)
    for cc in range(acc.shape[0] // tk):
        vt_ref[cc] = acc[cc * tk:(cc + 1) * tk, :].T.astype(vt_ref.dtype)
    small_ref[...] = jnp.dot(x, ws_ref[...], preferred_element_type=F32)
    if len(rest) == 4:
        rest[3][...] = jnp.dot(x, rest[0][...], preferred_element_type=F32).astype(rest[3].dtype)


def _proj_values(x, w_val, w_small, w_plain, tm, tk):
    n, d = x.shape
    pv, ps = w_val.shape[1], w_small.shape[1]
    row = lambda i: (i, 0)
    fixed = lambda i: (0, 0)
    in_specs = [pl.BlockSpec((tm, d), row), pl.BlockSpec((d, pv), fixed), pl.BlockSpec((d, ps), fixed)]
    out_specs = [pl.BlockSpec((tm // tk, pv, tk), lambda i: (i, 0, 0)), pl.BlockSpec((tm, ps), row)]
    out_shape = [jax.ShapeDtypeStruct((n // tk, pv, tk), BF16), jax.ShapeDtypeStruct((n, ps), F32)]
    args = [x, w_val, w_small]
    if w_plain is not None:
        pp = w_plain.shape[1]
        in_specs.append(pl.BlockSpec((d, pp), fixed))
        out_specs.append(pl.BlockSpec((tm, pp), row))
        out_shape.append(jax.ShapeDtypeStruct((n, pp), BF16))
        args.append(w_plain)
    return pl.pallas_call(
        functools.partial(_mm_t_kernel, tk=tk),
        grid=(n // tm,),
        in_specs=in_specs,
        out_specs=out_specs,
        out_shape=out_shape,
        compiler_params=_cparams(("parallel",)),
        name="proj_t",
    )(*args)


def _normalize_t(acc, width):
    return acc[:width] * _safe_recip(acc[width:width + 1])


def _pipe_flash(n, ns, qk, vt_at, bias_at, bufs, dv, tk, tq):
    sa, sb, pa, pb = bufs

    def softmax_into(p_ref, j, st, m):
        m_new = jnp.maximum(m, jnp.max(st, axis=0, keepdims=True))
        p_ref[j] = jnp.exp2((st - m_new).astype(BF16))
        return m_new, jnp.exp2(m - m_new)

    def half(c, carry, s_cur, s_nxt, p_prev, p_cur):
        if s_nxt is not None:
            nxt = jnp.minimum(c + 1, n - 1)
            for j in range(ns):
                s_nxt[j] = qk(nxt, j)
        out = []
        for j in range(ns):
            m, acc, alpha = carry[j]
            acc = alpha * acc + jnp.dot(vt_at(c - 1, j), p_prev[j], preferred_element_type=F32)
            m, alpha = softmax_into(p_cur, j, s_cur[j] + bias_at(c, j), m)
            out.append((m, acc, alpha))
        return tuple(out)

    for j in range(ns):
        sa[j] = qk(0, j)
    first = []
    for j in range(ns):
        sb[j] = qk(jnp.minimum(1, n - 1), j)
        m, alpha = softmax_into(pa, j, sa[j] + bias_at(0, j), jnp.full((1, tq), M_INIT, F32))
        first.append((m, jnp.zeros((dv, tq), F32), alpha))

    def body(t, carry):
        carry = half(2 * t + 1, carry, sb, sa, pa, pb)
        return half(2 * t + 2, carry, sa, sb, pb, pa)

    carry = lax.fori_loop(0, (n - 1) // 2, body, tuple(first))

    def flush(carry, p_last):
        return tuple(alpha * acc + jnp.dot(vt_at(n - 1, j), p_last[j], preferred_element_type=F32)
                     for j, (_, acc, alpha) in enumerate(carry))

    def odd_tail(carry):
        return flush(half(n - 1, carry, sb, None, pa, pb), pb)

    return lax.cond(n % 2 == 0, odd_tail, lambda carry: flush(carry, pa), carry)


def _pipe_scratch(ns, tk, tq):
    return [pltpu.VMEM((ns, tk, tq), F32)] * 2 + [pltpu.VMEM((ns, tk, tq), BF16)] * 2


def _causal_t(t):
    return jnp.where(lax.broadcasted_iota(I32, (t, t), 0) <= lax.broadcasted_iota(I32, (t, t), 1), 0.0, NINF)


def _diff_kernel(lam_ref, sub_ref, q_ref, k_ref, vt_ref, o_ref, *bufs, tq, lam_init):
    i = pl.program_id(2)
    lp = lam_ref[...]
    lam = (jnp.exp(jnp.sum(lp[0:1] * lp[1:2], axis=1, keepdims=True))
           - jnp.exp(jnp.sum(lp[2:3] * lp[3:4], axis=1, keepdims=True)) + lam_init)
    qs = _split_halves(q_ref[...], roped=True)
    n = i + 1
    bufs, tab_ref = bufs[:4], bufs[4]
    tab_ref[0] = jnp.zeros((tq, tq), F32)
    tab_ref[1] = _causal_t(tq)

    def qk(c, j):
        off = pl.multiple_of(c * tq, tq)
        return _dot_nt(k_ref[pl.ds(off, tq), :], qs[j])

    def bias_at(c, j):
        return tab_ref[jnp.where(c == i, 1, 0)]

    outs = _pipe_flash(n, 2, qk, lambda c, j: _with_ones(vt_ref[c]), bias_at, bufs, LANES + DV_PAD, tq, tq)
    o = _normalize_t(outs[0], LANES) - lam * _normalize_t(outs[1], LANES)
    o = o * lax.rsqrt(jnp.mean(o * o, axis=0, keepdims=True) + LN_EPS)
    o = o * sub_ref[...] * (1.0 - lam_init)
    o_ref[...] = o.T.astype(o_ref.dtype)


def _diff_attention(rp, vt, lam_params, subln, lam_init, tq=TQ):
    b, s, _ = rp.shape
    nk = s // tq
    return pl.pallas_call(
        functools.partial(_diff_kernel, tq=tq, lam_init=lam_init),
        grid=(b, A_HEADS, nk),
        in_specs=[pl.BlockSpec((4, HEAD_DIM), lambda bi, h, i: (0, 0)),
                  pl.BlockSpec((LANES, 1), lambda bi, h, i: (0, 0)),
                  pl.BlockSpec((None, tq, LANES), lambda bi, h, i: (bi, i, h)),
                  pl.BlockSpec((None, s, LANES), lambda bi, h, i: (bi, 0, A_HEADS + h)),
                  pl.BlockSpec((None, nk, LANES, tq), lambda bi, h, i: (bi, 0, h, 0))],
        out_specs=pl.BlockSpec((None, tq, LANES), lambda bi, h, i: (bi, i, h)),
        out_shape=jax.ShapeDtypeStruct((b, s, A_HEADS * LANES), BF16),
        scratch_shapes=_pipe_scratch(2, tq, tq) + [pltpu.VMEM((2, tq, tq), F32)],
        compiler_params=_cparams(("parallel", "parallel", "arbitrary")),
        name="diff_attn",
    )(lam_params, subln.reshape(LANES, 1), rp, rp, vt)


def _fold8(x, op):
    acc = x[0:8]
    for r in range(1, x.shape[0] // 8):
        acc = op(acc, x[r * 8:(r + 1) * 8])
    return acc


def _dsa_kernel(iq_ref, ikk_ref, iw_ref, q_ref, k_ref, vt_ref, o_ref, s_ref, j_ref, *bufs, tq, ksel, seq_len):
    tk = tq
    i = pl.program_id(1)
    nch = i + 1
    ksel_f = float(ksel)
    k_loc = lax.broadcasted_iota(I32, (tk, tq), 0)
    q_loc = lax.broadcasted_iota(I32, (tk, tq), 1)
    qpos = i * tq + lax.broadcasted_iota(I32, (1, tq), 1)

    iq = iq_ref[...]
    iwt = iw_ref[...].T
    iqh = []
    for pair in range(IDX_HEADS // 2):
        iqh += list(_split_halves(iq[:, pair * LANES:(pair + 1) * LANES], roped=True))

    def scores(c):
        off = pl.multiple_of(c * tk, tk)
        kk = ikk_ref[pl.ds(off, tk), :]
        sc = iwt[0:1] * jnp.maximum(_dot_nt(kk, iqh[0]), 0.0)
        for h in range(1, IDX_HEADS):
            sc = sc + iwt[h:h + 1] * jnp.maximum(_dot_nt(kk, iqh[h]), 0.0)
        return sc

    def full_body(c, carry):
        mx, mn = carry
        sc = scores(c)
        s_ref[c] = sc
        return jnp.maximum(mx, _fold8(sc, jnp.maximum)), jnp.minimum(mn, _fold8(sc, jnp.minimum))

    mx, mn = lax.fori_loop(0, i, full_body, (jnp.full((8, tq), -BIG, F32), jnp.full((8, tq), BIG, F32)))
    sc = scores(i)
    causal = k_loc <= q_loc
    s_ref[i] = jnp.where(causal, sc, NEG)
    mx = jnp.maximum(mx, _fold8(jnp.where(causal, sc, -BIG), jnp.maximum))
    mn = jnp.minimum(mn, _fold8(jnp.where(causal, sc, BIG), jnp.minimum))
    smax = jnp.max(mx, axis=0, keepdims=True)
    smin = jnp.min(mn, axis=0, keepdims=True)

    def count_where(ind):
        def body(c, acc):
            return acc + _fold8(ind(s_ref[c], c * tk + k_loc), jnp.add)
        acc = lax.fori_loop(0, nch, body, jnp.zeros((8, tq), F32))
        return jnp.sum(acc, axis=0, keepdims=True)

    def count_ge(th):
        return count_where(lambda x, kidx: jnp.where(x >= th, 1.0, 0.0))

    def max_below(th):
        def body(c, acc):
            x = s_ref[c]
            return jnp.maximum(acc, _fold8(jnp.where(x < th, x, NINF), jnp.maximum))
        acc = lax.fori_loop(0, nch, body, jnp.full((8, tq), NINF, F32))
        return jnp.max(acc, axis=0, keepdims=True)

    n_causal = (qpos + 1).astype(F32)
    take_all = n_causal <= ksel_f
    done0 = jnp.where(take_all, 1.0, 0.0)
    hi0 = smax + (jnp.abs(smax) * 2.0 ** -20 + 1e-30)

    def bisect(lo, hi):
        mid = lo + (hi - lo) * 0.5
        ge = count_ge(mid) >= ksel_f
        return jnp.where(ge, mid, lo), jnp.where(ge, hi, mid)

    lo, hi = lax.fori_loop(0, N_BISECT, lambda _, c: bisect(*c), (smin, hi0))

    def snap_body(carry):
        lo, hi, th, c_th, done, _ = carry
        lo, hi = bisect(lo, hi)
        t1 = max_below(hi)
        c1 = count_ge(t1)
        ok = c1 >= ksel_f
        th = jnp.where(done > 0.0, th, t1)
        c_th = jnp.where(done > 0.0, c_th, c1)
        hi = jnp.where(ok, hi, t1)
        done = jnp.where(ok, 1.0, done)
        return lo, hi, th, c_th, done, jnp.sum(1.0 - done)

    _, _, th, c_ge, _, _ = lax.while_loop(lambda c: c[5] > 0.0, snap_body,
                                          (lo, hi, smax, jnp.zeros((1, tq), F32), done0, jnp.sum(1.0 - done0)))

    need_tb = jnp.where(take_all, 0.0, jnp.where(c_ge > ksel_f, 1.0, 0.0))
    j_ref[...] = jnp.full((8, tq), seq_len - 1, I32)

    @pl.when(jnp.sum(need_tb) > 0.0)
    def _():
        need = ksel_f - count_where(lambda x, kidx: jnp.where(x > th, 1.0, 0.0))

        def jb(_, carry):
            lo_j, hi_j = carry
            mid = (lo_j + hi_j) // 2
            cnt = count_where(lambda x, kidx: jnp.where(x == th, jnp.where(kidx <= mid, 1.0, 0.0), 0.0))
            ge = cnt >= need
            return jnp.where(ge, lo_j, mid), jnp.where(ge, mid, hi_j)

        n_it = int(math.ceil(math.log2(seq_len))) + 1
        _, hi_j = lax.fori_loop(0, n_it, jb, (jnp.full((1, tq), -1, I32), jnp.full((1, tq), seq_len - 1, I32)))
        j_ref[...] = jnp.broadcast_to(hi_j, (8, tq))

    jsel = j_ref[0:1, :]

    def bias_body(c, _):
        x = s_ref[c]
        kidx = c * tk + k_loc
        keep = jnp.where(x > th, 0.0, jnp.where(x == th, jnp.where(kidx <= jsel, 0.0, NINF), NINF))
        keep = jnp.where(take_all, 0.0, keep)
        s_ref[c] = jnp.where(kidx <= qpos, keep, NINF)
        return 0

    lax.fori_loop(0, nch, bias_body, 0)

    for p in range(B_HEADS // 2):
        qs = _split_halves(q_ref[:, p * LANES:(p + 1) * LANES], roped=True)

        def qk(c, j, qs=qs):
            off = pl.multiple_of(c * tk, tk)
            return _dot_nt(k_ref[pl.ds(off, tk), :], qs[j])

        outs = _pipe_flash(nch, 2, qk, lambda c, j: _with_ones(vt_ref[c, j * HEAD_DIM:(j + 1) * HEAD_DIM, :]),
                           lambda c, j: s_ref[c], bufs, HEAD_DIM + DV_PAD, tk, tq)
        o = jnp.concatenate([_normalize_t(outs[0], HEAD_DIM), _normalize_t(outs[1], HEAD_DIM)], axis=0)
        o_ref[:, p * LANES:(p + 1) * LANES] = o.T.astype(o_ref.dtype)


def _dsa_attention(rp, vt, iw, tq=TQ):
    b, s, _ = rp.shape
    ksel = min(DSA_TOPK, s // 4)
    nk = s // tq
    return pl.pallas_call(
        functools.partial(_dsa_kernel, tq=tq, ksel=ksel, seq_len=s),
        grid=(b, nk),
        in_specs=[pl.BlockSpec((None, tq, 2 * LANES), lambda bi, i: (bi, i, 6)),
                  pl.BlockSpec((None, s, LANES), lambda bi, i: (bi, 0, 15)),
                  pl.BlockSpec((None, tq, LANES), lambda bi, i: (bi, i, 0)),
                  pl.BlockSpec((None, tq, 4 * LANES), lambda bi, i: (bi, i, 2)),
                  pl.BlockSpec((None, s, LANES), lambda bi, i: (bi, 0, 14)),
                  pl.BlockSpec((None, nk, LANES, tq), lambda bi, i: (bi, 0, 4, 0))],
        out_specs=pl.BlockSpec((None, tq, 4 * LANES), lambda bi, i: (bi, i, 0)),
        out_shape=jax.ShapeDtypeStruct((b, s, 4 * LANES), BF16),
        scratch_shapes=[pltpu.VMEM((nk, tq, tq), F32), pltpu.VMEM((8, tq), I32)] + _pipe_scratch(2, tq, tq),
        compiler_params=_cparams(("parallel", "arbitrary")),
        name="dsa_attn",
    )(rp, rp, iw, rp, rp, vt)


def _moba_kernel(q_ref, k_ref, vt_ref, o_ref, km_ref, sel_ref, *bufs, seq_len, n_sel):
    tq = MOBA_BLOCK
    qb = pl.program_id(2)

    @pl.when(qb == 0)
    def _():
        j = lax.broadcasted_iota(I32, (LANES, seq_len), 0)
        s = lax.broadcasted_iota(I32, (LANES, seq_len), 1)
        avg = jnp.where(s // MOBA_BLOCK == j, 1.0 / MOBA_BLOCK, 0.0).astype(BF16)
        km_ref[...] = jnp.dot(avg, k_ref[...], preferred_element_type=F32)

    nbp = sel_ref.shape[1]
    km = km_ref[0:nbp, :]
    qs = _split_halves(q_ref[...], roped=True)
    blk = lax.broadcasted_iota(I32, (nbp, tq), 0)
    past = blk < qb
    for j in range(2):
        gate = lax.dot_general(km, qs[j].astype(F32), (((1,), (1,)), ((), ())),
                               precision=lax.Precision.HIGHEST, preferred_element_type=F32)
        gate = jnp.where(blk < seq_len // MOBA_BLOCK, jnp.where(past, gate, NEG), NINF)
        sel = _top_n_mask(gate, n_sel, 0)
        sel_ref[j] = jnp.where(past, jnp.where(sel > 0.5, 0.0, NINF), NINF)
    own = _causal_t(tq)
    n = qb + 1

    def qk(c, j):
        off = pl.multiple_of(c * tq, tq)
        return _dot_nt(k_ref[pl.ds(off, tq), :], qs[j])

    def bias_at(c, j):
        chosen = sel_ref[j, pl.ds(c, 1), :]
        return jnp.where(c == qb, own, chosen)

    outs = _pipe_flash(n, 2, qk, lambda c, j: _with_ones(vt_ref[c, j * HEAD_DIM:(j + 1) * HEAD_DIM, :]), bias_at, bufs,
                       HEAD_DIM + DV_PAD, tq, tq)
    o = jnp.concatenate([_normalize_t(outs[0], HEAD_DIM), _normalize_t(outs[1], HEAD_DIM)], axis=0)
    o_ref[...] = o.T.astype(o_ref.dtype)


def _moba_attention(rp, vt):
    b, s, _ = rp.shape
    tq = MOBA_BLOCK
    nb = s // tq
    n_sel = max(1, min(MOBA_TOPK, nb - 1))
    npair = C_HEADS // 2
    return pl.pallas_call(
        functools.partial(_moba_kernel, seq_len=s, n_sel=n_sel),
        grid=(b, npair, nb),
        in_specs=[pl.BlockSpec((None, tq, LANES), lambda bi, h, i: (bi, i, h)),
                  pl.BlockSpec((None, s, LANES), lambda bi, h, i: (bi, 0, npair + h)),
                  pl.BlockSpec((None, nb, LANES, tq), lambda bi, h, i: (bi, 0, h, 0))],
        out_specs=pl.BlockSpec((None, tq, LANES), lambda bi, h, i: (bi, i, h)),
        out_shape=jax.ShapeDtypeStruct((b, s, npair * LANES), BF16),
        scratch_shapes=[pltpu.VMEM((LANES, LANES), F32), pltpu.VMEM((2, -(-nb // 8) * 8, tq), F32)]
        + _pipe_scratch(2, tq, tq),
        compiler_params=_cparams(("parallel", "parallel", "arbitrary")),
        name="moba_attn",
    )(rp, rp, vt)


def _cmp_kernel(r_ref, pe_ref, w1_ref, w2_ref, o_ref):
    r = r_ref[...]
    w1 = w1_ref[...]
    half = r.shape[1]
    u = jnp.dot(r, w1[:half], preferred_element_type=F32)
    v = jnp.dot(r, w1[half:], preferred_element_type=F32)
    c = jnp.dot(pe_ref[...], w1, preferred_element_type=F32)[0:1]
    pre = u + pltpu.roll(v, r.shape[0] - 1, 0) + c
    o_ref[...] = jnp.dot(jax.nn.gelu(pre).astype(BF16), w2_ref[...],
                         preferred_element_type=F32).astype(o_ref.dtype)


def _nsa_compress(r, pe, w1, w2):
    b, _, nc, wdt = r.shape
    hid = w1.shape[2]
    return pl.pallas_call(
        _cmp_kernel,
        grid=(b, 4),
        in_specs=[pl.BlockSpec((None, None, nc, wdt), lambda bi, t: (bi, t, 0, 0)),
                  pl.BlockSpec((None, 8, 2 * wdt), lambda bi, t: (t // 2, 0, 0)),
                  pl.BlockSpec((None, 2 * wdt, hid), lambda bi, t: (t // 2, 0, 0)),
                  pl.BlockSpec((None, hid, HEAD_DIM), lambda bi, t: (t // 2, 0, 0))],
        out_specs=pl.BlockSpec((None, None, nc, HEAD_DIM), lambda bi, t: (bi, t, 0, 0)),
        out_shape=jax.ShapeDtypeStruct((b, 4, nc, HEAD_DIM), BF16),
        compiler_params=_cparams(("parallel", "arbitrary")),
        name="nsa_compress",
    )(r, pe, w1, w2)


def _nsa_kernel(qr_ref, qw_ref, dg_ref, gb_ref, kc_ref, vct_ref, ks_ref, vst_ref, kw_ref, vwt_ref,
                o_ref, sel_ref, wb_ref, *bufs, tq, seq_len):
    tk = tq
    i = pl.program_id(1)
    nch = i + 1
    nc = seq_len // NSA_CMP_STRIDE
    n_sb = seq_len // NSA_SLC_BLOCK
    n_sel = min(NSA_SLC_TOPK, n_sb)
    k_loc = lax.broadcasted_iota(I32, (tk, tq), 0)
    q_loc = lax.broadcasted_iota(I32, (tk, tq), 1)
    qpos = i * tq + lax.broadcasted_iota(I32, (1, tq), 1)

    gates_t = jax.nn.sigmoid(dg_ref[...] + gb_ref[...]).T
    kc = kc_ref[...]
    cmp_end = lax.broadcasted_iota(I32, (nc, 1), 0) * NSA_CMP_STRIDE + (NSA_CMP_LEN - 1)
    cbias = jnp.where(cmp_end <= qpos, 0.0, NINF)

    nh = D_HEADS // 2
    q_rot = [_split_halves(qr_ref[:, p * LANES:(p + 1) * LANES], roped=True) for p in range(nh)]
    q_raw = [_split_halves(qw_ref[:, p * LANES:(p + 1) * LANES]) for p in range(nh)]
    cbias4 = jnp.concatenate([cbias] * nh, axis=1)
    o_cmp, psum = [], []
    for g in range(2):
        qg = jnp.concatenate([q_raw[p][g] for p in range(nh)], axis=0)
        s = _dot_nt(kc, qg) + cbias4
        m = jnp.max(s, axis=0, keepdims=True)
        e = jnp.exp2(s - jnp.where(m == NINF, 0.0, m))
        pc = e * _safe_recip(jnp.sum(e, axis=0, keepdims=True))
        psum.append(sum(pc[:, p * tq:(p + 1) * tq] for p in range(nh)))
        o_cmp.append(jnp.dot(vct_ref[g], pc.astype(BF16), preferred_element_type=F32))

    per = tk // NSA_SLC_BLOCK
    nbp = -(-n_sb // 8) * 8
    cn = lax.broadcasted_iota(I32, (nbp, nc), 1) * NSA_CMP_STRIDE
    sj = lax.broadcasted_iota(I32, (nbp, nc), 0) * NSA_SLC_BLOCK
    shares = jnp.where((cn <= sj + NSA_SLC_BLOCK - 1) & (cn + NSA_CMP_LEN - 1 >= sj), 1.0, 0.0)
    blk = lax.broadcasted_iota(I32, (nbp, tq), 0)
    cur = qpos // NSA_SLC_BLOCK
    causal_b = blk <= cur
    forced = (blk == 0) | ((blk >= cur - 1) & causal_b)
    for g in range(2):
        imp = jnp.dot(shares, psum[g], precision=lax.Precision.HIGHEST, preferred_element_type=F32)
        val = jnp.where(forced, BIG, jnp.where(causal_b, imp, NEG))
        val = jnp.where(blk < n_sb, val, NINF)
        rowb = jnp.where(_top_n_mask(val, n_sel, 0) > 0.5, 0.0, NINF)
        for c in range(seq_len // tk):
            sel_ref[g, c] = jnp.concatenate([rowb[c * per:(c + 1) * per], jnp.zeros((8 - per, tq), F32)], axis=0)

    wb_ref[0] = jnp.where(k_loc <= q_loc, 0.0, NINF)
    wb_ref[1] = jnp.zeros((tk, tq), F32)
    wb_ref[2] = jnp.where(k_loc > q_loc, 0.0, NINF)
    n_wc = NSA_WINDOW // tk + 1
    w_first = jnp.maximum(i - (n_wc - 1), 0)
    n_w = i - w_first + 1

    for p in range(D_HEADS // 2):
        qs = q_rot[p]

        def qk_s(c, j, qs=qs):
            off = pl.multiple_of(c * tk, tk)
            return _dot_nt(ks_ref[pl.ds(off, tk), :], qs[j])

        def bias_s(c, j):
            rows = sel_ref[j, c]
            tile = jnp.concatenate([jnp.broadcast_to(rows[r:r + 1], (NSA_SLC_BLOCK, tq)) for r in range(per)], axis=0)
            return tile + wb_ref[jnp.where(c == i, 0, 1)]

        o_slc = _pipe_flash(nch, 2, qk_s, lambda c, j: _with_ones(vst_ref[c, j * HEAD_DIM:(j + 1) * HEAD_DIM, :]),
                            bias_s, bufs, HEAD_DIM + DV_PAD, tk, tq)

        def qk_w(c, j, qs=qs):
            off = pl.multiple_of((w_first + c) * tk, tk)
            return _dot_nt(kw_ref[pl.ds(off, tk), :], qs[j])

        def bias_w(c, j):
            return wb_ref[i - (w_first + c)]

        o_win = _pipe_flash(n_w, 2, qk_w,
                            lambda c, j: _with_ones(vwt_ref[w_first + c, j * HEAD_DIM:(j + 1) * HEAD_DIM, :]), bias_w, bufs,
                            HEAD_DIM + DV_PAD, tk, tq)
        outs = []
        for g in range(2):
            h = g * (D_HEADS // 2) + p
            outs.append(gates_t[3 * h:3 * h + 1] * o_cmp[g][:, p * tq:(p + 1) * tq]
                        + gates_t[3 * h + 1:3 * h + 2] * _normalize_t(o_slc[g], HEAD_DIM)
                        + gates_t[3 * h + 2:3 * h + 3] * _normalize_t(o_win[g], HEAD_DIM))
        o_ref[:, p * LANES:(p + 1) * LANES] = jnp.concatenate(outs, axis=0).T.astype(o_ref.dtype)


def _nsa_attention(rp, pp, vt, dg, gate_b, kcmp, vcmp, tq=TQ):
    b, s, _ = rp.shape
    nk = s // tq
    nc = s // NSA_CMP_STRIDE
    assert NSA_WINDOW == 2 * tq
    n_wc = NSA_WINDOW // tq + 1
    vct = vcmp.reshape(b, nc, 2, HEAD_DIM).transpose(0, 2, 3, 1)
    full = lambda t: pl.BlockSpec((None, s, LANES), lambda bi, i: (bi, 0, t))
    vspec = lambda t: pl.BlockSpec((None, nk, LANES, tq), lambda bi, i: (bi, 0, t, 0))
    return pl.pallas_call(
        functools.partial(_nsa_kernel, tq=tq, seq_len=s),
        grid=(b, nk),
        in_specs=[pl.BlockSpec((None, tq, 4 * LANES), lambda bi, i: (bi, i, 2)),
                  pl.BlockSpec((None, tq, 4 * LANES), lambda bi, i: (bi, i, 0)),
                  pl.BlockSpec((None, tq, LANES), lambda bi, i: (bi, i, 0)),
                  pl.BlockSpec((1, LANES), lambda bi, i: (0, 0)),
                  pl.BlockSpec((None, nc, LANES), lambda bi, i: (bi, 0, 0)),
                  pl.BlockSpec((None, 2, HEAD_DIM, nc), lambda bi, i: (bi, 0, 0, 0)),
                  full(12), vspec(4), full(13), vspec(5)],
        out_specs=pl.BlockSpec((None, tq, 4 * LANES), lambda bi, i: (bi, i, 0)),
        out_shape=jax.ShapeDtypeStruct((b, s, 4 * LANES), BF16),
        scratch_shapes=[pltpu.VMEM((2, nk, 8, tq), F32), pltpu.VMEM((n_wc, tq, tq), F32)]
        + _pipe_scratch(2, tq, tq),
        compiler_params=_cparams(("parallel", "arbitrary")),
        name="nsa_attn",
    )(rp, pp, dg, gate_b, kcmp, vct, rp, vt, rp, vt)


def _cast_kernel(x_ref, o_ref):
    o_ref[...] = x_ref[...].astype(o_ref.dtype)


def _layer_to_bf16(w, layer):
    _, e, r, c = w.shape
    tr = 1 << ((2 ** 21 // c).bit_length() - 1)
    out = pl.pallas_call(
        _cast_kernel,
        grid=(e * r // tr,),
        in_specs=[pl.BlockSpec((None, tr, c), lambda i: (layer, i, 0))],
        out_specs=pl.BlockSpec((tr, c), lambda i: (i, 0)),
        out_shape=jax.ShapeDtypeStruct((e * r, c), BF16),
        compiler_params=_cparams(("parallel",)),
        name="cast_bf16",
    )(w.reshape(w.shape[0], e * r, c))
    return out.reshape(e, r, c)


def _moe_ffn_kernel(be_ref, nu_ref, x_ref, wg_ref, wu_ref, wd_ref, o_ref):
    used = pl.program_id(0) < nu_ref[0]

    @pl.when(used)
    def _():
        x = x_ref[...]
        h = jax.nn.silu(jnp.dot(x, wg_ref[...], preferred_element_type=F32)) * jnp.dot(
            x, wu_ref[...], preferred_element_type=F32)
        o_ref[...] = jnp.dot(h.astype(BF16), wd_ref[...], preferred_element_type=F32).astype(o_ref.dtype)

    @pl.when(jnp.logical_not(used))
    def _():
        o_ref[...] = jnp.zeros(o_ref.shape, o_ref.dtype)


def _moe_ffn(x_sorted, block_e, n_used, wg, wu, wd):
    ns, d = x_sorted.shape
    fdim = wg.shape[2]
    once = pl.Buffered(1)
    grid_spec = pltpu.PrefetchScalarGridSpec(
        num_scalar_prefetch=2,
        grid=(ns // MOE_TM,),
        in_specs=[pl.BlockSpec((MOE_TM, d), lambda i, be, nu: (jnp.minimum(i, nu[0] - 1), 0)),
                  pl.BlockSpec((None, d, fdim), lambda i, be, nu: (be[i], 0, 0), pipeline_mode=once),
                  pl.BlockSpec((None, d, fdim), lambda i, be, nu: (be[i], 0, 0), pipeline_mode=once),
                  pl.BlockSpec((None, fdim, d), lambda i, be, nu: (be[i], 0, 0), pipeline_mode=once)],
        out_specs=pl.BlockSpec((MOE_TM, d), lambda i, be, nu: (i, 0)),
    )
    return pl.pallas_call(
        _moe_ffn_kernel,
        grid_spec=grid_spec,
        out_shape=jax.ShapeDtypeStruct((ns, d), BF16),
        compiler_params=_cparams(("arbitrary",)),
        name="moe_ffn",
    )(block_e, n_used, x_sorted, wg, wu, wd)


def _combine_ln_kernel(x_ref, y0_ref, y1_ref, rt_ref, g_ref, b_ref, xo_ref, xb_ref):
    rt = rt_ref[...]
    ffn = rt[:, 2:3] * y0_ref[...].astype(F32) + rt[:, 3:4] * y1_ref[...].astype(F32)
    y = _layer_norm(DN_ALPHA * x_ref[...] + ffn, g_ref[...], b_ref[...])
    xo_ref[...] = y
    xb_ref[...] = y.astype(BF16)


def _combine_ln(x, y0, y1, rt, g, b, tm=ROW_TILE):
    n, d = x.shape
    row = lambda i: (i, 0)
    fixed = lambda i: (0, 0)
    return pl.pallas_call(
        _combine_ln_kernel,
        grid=(n // tm,),
        in_specs=[pl.BlockSpec((tm, d), row), pl.BlockSpec((tm, d), row), pl.BlockSpec((tm, d), row),
                  pl.BlockSpec((tm, LANES), row), pl.BlockSpec((1, d), fixed), pl.BlockSpec((1, d), fixed)],
        out_specs=[pl.BlockSpec((tm, d), row), pl.BlockSpec((tm, d), row)],
        out_shape=[jax.ShapeDtypeStruct((n, d), F32), jax.ShapeDtypeStruct((n, d), BF16)],
        compiler_params=_cparams(("parallel",)),
        name="moe_combine_ln",
    )(x, y0, y1, rt, g.reshape(1, d), b.reshape(1, d))


def _moe_layout(rt, n):
    e_flat = rt[:, 0:TOP_K].astype(I32).reshape(-1)
    nk = n * TOP_K
    onehot = (e_flat[:, None] == jnp.arange(N_EXPERTS, dtype=I32)[None, :]).astype(I32)
    rank = jnp.take_along_axis(jnp.cumsum(onehot, axis=0), e_flat[:, None], axis=1)[:, 0] - 1
    counts = jnp.sum(onehot, axis=0)
    padded = (counts + MOE_TM - 1) // MOE_TM * MOE_TM
    pad_end = jnp.cumsum(padded)
    pad_start = pad_end - padded
    grp_start = jnp.cumsum(counts) - counts
    slot = pad_start[e_flat] + rank
    n_blocks = -(-nk // MOE_TM) + N_EXPERTS
    n_slots = n_blocks * MOE_TM
    order = jnp.argsort(e_flat, stable=True).astype(I32)
    sl = jnp.arange(n_slots, dtype=I32)
    slot_e = jnp.minimum(jnp.searchsorted(pad_end, sl, side='right'), N_EXPERTS - 1).astype(I32)
    within = sl - pad_start[slot_e]
    valid = within < counts[slot_e]
    src = jnp.where(valid, grp_start[slot_e] + within, 0)
    slot_tok = jnp.where(valid, order[src] // TOP_K, 0)
    n_used = (pad_end[-1] // MOE_TM).astype(I32).reshape(1)
    blk = jnp.arange(n_blocks, dtype=I32)
    block_e = slot_e[jnp.minimum(blk, n_used[0] - 1) * MOE_TM]
    return slot_tok, slot.reshape(n, TOP_K), block_e, n_used


def _pair_perm(n_heads):
    half = n_heads // 2
    cols = []
    for p in range(half):
        cols += list(range(p * HEAD_DIM, (p + 1) * HEAD_DIM))
        cols += list(range((half + p) * HEAD_DIM, (half + p + 1) * HEAD_DIM))
    return np.asarray(cols, dtype=np.int32)


def _pad_cols(w, width):
    return jnp.pad(w, ((0, 0), (0, width - w.shape[1])))


def _split_cols(w, sizes):
    out, off = [], 0
    for sz in sizes:
        out.append(w[:, off:off + sz])
        off += sz
    return out


def _even_layer(x, xb, w_in, w_out, lam_params, subln, lam_init, wg, wu, wd, ln, tabs, bsz, seq_len):
    n = x.shape[0]
    perm = _pair_perm(B_HEADS)
    aq, ak, av, bq, bk, bv, iq, ik, iw = _split_cols(w_in, EVEN_SIZES)
    w_rope = jnp.concatenate([aq * SCALE, ak, bq[:, perm] * SCALE, iq, bk, ik, ik], axis=1)
    w_rope = w_rope[:, _rope_layout(w_rope.shape[1])].astype(BF16)
    w_val = jnp.concatenate([av, bv], axis=1).astype(BF16)
    w_iw = _pad_cols(iw, LANES).astype(BF16)
    rp = _proj_rope(xb, w_rope, ROW_TILE, w_rope.shape[1] // 2, seq_len, tabs).reshape(bsz, seq_len, -1)
    vt, iwv = _proj_values(xb, w_val, w_iw, None, ROW_TILE, TQ)
    vt = vt.reshape(bsz, seq_len // TQ, -1, TQ)
    iwv = iwv.reshape(bsz, seq_len, LANES)
    o_a = _diff_attention(rp, vt, lam_params, subln, lam_init)
    o_b = _dsa_attention(rp, vt, iwv)
    half = w_out.shape[0] // 2
    wo_a = w_out[:half].astype(BF16)
    wo_b = w_out[half:][perm].astype(BF16)
    g_mix, b_mix, g_ffn, b_ffn = ln
    x1, x1b = _outproj_ln(x, o_a.reshape(n, -1), o_b.reshape(n, -1), wo_a, wo_b, g_mix, b_mix)
    return _ffn_ln(x1, x1b, wg.astype(BF16), wu.astype(BF16), wd.astype(BF16), g_ffn, b_ffn)


def _odd_layer(x, xb, w_in, w_out, gate_b, pe, phi_w1, phi_w2, w_router, b_router, wg, wu, wd, ln, tabs,
               bsz, seq_len):
    n = x.shape[0]
    perm = _pair_perm(D_HEADS)
    cq, ck, cv, dq, dkc, dvc, dks, dvs, dkw, dvw, dg = _split_cols(w_in, ODD_SIZES)
    dq = dq[:, perm] * SCALE
    w_rope = jnp.concatenate([cq * SCALE, ck, dq, dks, dkw], axis=1)
    w_rope = w_rope[:, _rope_layout(w_rope.shape[1])].astype(BF16)
    w_plain = jnp.concatenate([dq, dkc, dvc], axis=1).astype(BF16)
    w_val = jnp.concatenate([cv, dvs, dvw], axis=1).astype(BF16)
    w_dg = _pad_cols(dg, LANES).astype(BF16)
    rp = _proj_rope(xb, w_rope, ROW_TILE, w_rope.shape[1] // 2, seq_len, tabs).reshape(bsz, seq_len, -1)
    vt, dgv, pp = _proj_values(xb, w_val, w_dg, w_plain, ROW_TILE, TQ)
    vt = vt.reshape(bsz, seq_len // TQ, -1, TQ)
    dgv = dgv.reshape(bsz, seq_len, LANES)
    pp = pp.reshape(bsz, seq_len, -1)

    o_c = _moba_attention(rp, vt)

    nc = seq_len // NSA_CMP_STRIDE
    tok = pp[:, :, 4 * LANES:6 * LANES].reshape(bsz, nc, NSA_CMP_STRIDE, 4, HEAD_DIM)
    r = tok.transpose(0, 3, 1, 2, 4).reshape(bsz, 4, nc, NSA_CMP_STRIDE * HEAD_DIM)
    pe_flat = jnp.pad(pe.reshape(2, 1, -1), ((0, 0), (0, 7), (0, 0))).astype(BF16)
    cmp = _nsa_compress(r, pe_flat, phi_w1.astype(BF16), phi_w2.astype(BF16))
    kcmp = jnp.concatenate([cmp[:, 0], cmp[:, 1]], axis=-1)
    vcmp = jnp.concatenate([cmp[:, 2], cmp[:, 3]], axis=-1)
    gb = _pad_cols(gate_b.reshape(1, -1), LANES)
    o_d = _nsa_attention(rp, pp, vt, dgv, gb, kcmp, vcmp)

    half = w_out.shape[0] // 2
    wo_c = w_out[:half].astype(BF16)
    wo_d = w_out[half:][perm].astype(BF16)
    g_mix, b_mix, g_ffn, b_ffn = ln
    router = (_pad_cols(w_router, LANES), _pad_cols(b_router.reshape(1, -1), LANES))
    x1, x1b, rt = _outproj_ln(x, o_c.reshape(n, -1), o_d.reshape(n, -1), wo_c, wo_d, g_mix, b_mix, router)

    slot_tok, slot, block_e, n_used = _moe_layout(rt, n)
    y_slots = _moe_ffn(x1b[slot_tok], block_e, n_used, wg, wu, wd)
    return _combine_ln(x1, y_slots[slot[:, 0]], y_slots[slot[:, 1]], rt, g_ffn, b_ffn)


@jax.jit
def kernel(x, ev_w_in, ev_w_out, dif_lambda, dif_subln, ffd_w_gate, ffd_w_up, ffd_w_down, od_w_in, od_w_out,
           nsa_gate_b, nsa_pe, nsa_phi_w1, nsa_phi_w2, moe_w_router, moe_b_router, moe_w_gate, moe_w_up,
           moe_w_down, ln_mix_g, ln_mix_b, ln_ffn_g, ln_ffn_b):
    bsz, seq_len, d = x.shape
    tabs = _rope_tables(seq_len)
    xf = x.reshape(bsz * seq_len, d)
    xb = xf.astype(BF16)
    for l in range(DEPTH):
        i = l // 2
        ln = (ln_mix_g[l], ln_mix_b[l], ln_ffn_g[l], ln_ffn_b[l])
        if l % 2 == 0:
            lam_init = 0.8 - 0.6 * math.exp(-0.3 * l)
            xf, xb = _even_layer(xf, xb, ev_w_in[i], ev_w_out[i], dif_lambda[i], dif_subln[i], lam_init,
                                 ffd_w_gate[i], ffd_w_up[i], ffd_w_down[i], ln, tabs, bsz, seq_len)
        else:
            xf, xb = _odd_layer(xf, xb, od_w_in[i], od_w_out[i], nsa_gate_b[i], nsa_pe[i], nsa_phi_w1[i],
                                nsa_phi_w2[i], moe_w_router[i], moe_b_router[i], _layer_to_bf16(moe_w_gate, i),
                                _layer_to_bf16(moe_w_up, i), _layer_to_bf16(moe_w_down, i), ln, tabs, bsz, seq_len)
    return xf.reshape(bsz, seq_len, d)
```

```python
import functools
import math

import numpy as np
import jax
import jax.numpy as jnp
from jax import lax
from jax.experimental import pallas as pl
from jax.experimental.pallas import tpu as pltpu

F32 = jnp.float32
BF16 = jnp.bfloat16
I32 = jnp.int32

LANES = 128
VMEM_LIMIT = 56 * 1024 * 1024

DEPTH = 4
HEAD_DIM = 64
ROPE_THETA = 10000.0
LN_EPS = 1e-5
DN_ALPHA = (2 * DEPTH) ** 0.25
SCALE = HEAD_DIM ** -0.5 * math.log2(math.e)
NEG = -1e30
BIG = 1e30
M_INIT = -1e30
NINF = float("-inf")

A_HEADS = 4
B_HEADS = 8
IDX_HEADS = 4
DSA_TOPK = 256
C_HEADS = 8
MOBA_BLOCK = 256
MOBA_TOPK = 3
D_HEADS = 8
NSA_CMP_LEN = 32
NSA_CMP_STRIDE = 16
NSA_SLC_BLOCK = 64
NSA_SLC_TOPK = 16
NSA_WINDOW = 512
N_EXPERTS = 8
TOP_K = 2
MOE_TM = 512
N_BISECT = 14
ROW_TILE = 512
KV_GROUPS = 2

EVEN_SIZES = (A_HEADS * 2 * HEAD_DIM, A_HEADS * 2 * HEAD_DIM, A_HEADS * 2 * HEAD_DIM, B_HEADS * HEAD_DIM,
              KV_GROUPS * HEAD_DIM, KV_GROUPS * HEAD_DIM, IDX_HEADS * HEAD_DIM, HEAD_DIM, IDX_HEADS)
ODD_SIZES = (C_HEADS * HEAD_DIM,) * 3 + (D_HEADS * HEAD_DIM,) + (KV_GROUPS * HEAD_DIM,) * 6 + (D_HEADS * 3,)


def _cparams(sem):
    return pltpu.CompilerParams(dimension_semantics=sem, vmem_limit_bytes=VMEM_LIMIT)


def _dot_nt(a, b):
    return lax.dot_general(a, b, (((1,), (1,)), ((), ())), preferred_element_type=F32)


def _layer_norm(y, g, b):
    mu = jnp.mean(y, axis=-1, keepdims=True)
    yc = y - mu
    var = jnp.mean(yc * yc, axis=-1, keepdims=True)
    return yc * lax.rsqrt(var + LN_EPS) * g + b


def _safe_recip(l):
    return jnp.where(l > 0.0, 1.0 / jnp.where(l > 0.0, l, 1.0), 0.0)


def _split_halves(t, roped=False):
    lane = lax.broadcasted_iota(I32, (1, LANES), 1)
    lo = (lane // (HEAD_DIM // 2)) % 2 == 0 if roped else lane < HEAD_DIM
    z = jnp.zeros_like(t)
    return jnp.where(lo, t, z), jnp.where(lo, z, t)


def _rope_layout(n_cols):
    q = HEAD_DIM // 2
    tile = np.concatenate([np.arange(0, q), np.arange(2 * q, 3 * q), np.arange(q, 2 * q), np.arange(3 * q, 4 * q)])
    return (np.arange(0, n_cols, LANES)[:, None] + tile[None, :]).reshape(-1).astype(np.int32)


def _rope_mm_kernel(x_ref, w_ref, cos_ref, sin_ref, o_ref):
    acc = jnp.dot(x_ref[...].astype(BF16), w_ref[...], preferred_element_type=F32)
    cos = cos_ref[...]
    sin = sin_ref[...]
    for c in range(acc.shape[1] // LANES):
        a = acc[:, c * LANES:(c + 1) * LANES]
        rot = pltpu.roll(a, LANES // 2, 1)
        o_ref[:, c * LANES:(c + 1) * LANES] = (a * cos + rot * sin).astype(o_ref.dtype)


def _proj_rope(x, w, tm, tn, seq_len, rope_tabs):
    n, d = x.shape
    p = w.shape[1]
    nt = seq_len // tm
    return pl.pallas_call(
        _rope_mm_kernel,
        grid=(n // tm, p // tn),
        in_specs=[pl.BlockSpec((tm, d), lambda i, j: (i, 0)), pl.BlockSpec((d, tn), lambda i, j: (0, j)),
                  pl.BlockSpec((tm, LANES), lambda i, j: (i % nt, 0)),
                  pl.BlockSpec((tm, LANES), lambda i, j: (i % nt, 0))],
        out_specs=pl.BlockSpec((tm, tn), lambda i, j: (i, j)),
        out_shape=jax.ShapeDtypeStruct((n, p), BF16),
        compiler_params=_cparams(("parallel", "arbitrary")),
        name="proj_rope",
    )(x, w, *rope_tabs)


def _rope_tables(seq_len):
    d = HEAD_DIM
    inv = ROPE_THETA ** (-jnp.arange(0, d, 2, dtype=F32) / d)
    ang = jnp.arange(seq_len, dtype=I32).astype(F32)[:, None] * inv[None, :]
    cos = jnp.cos(ang)
    sin = jnp.sin(ang)
    cos128 = jnp.tile(cos, (1, LANES // (d // 2)))
    sin128 = jnp.concatenate([-sin, -sin, sin, sin], axis=1)
    return cos128, sin128


def _route_top2(x, w, b):
    xh, wh = x.astype(BF16), w.astype(BF16)
    xl, wl = (x - xh.astype(F32)).astype(BF16), (w - wh.astype(F32)).astype(BF16)
    logits = (jnp.dot(xh, wh, preferred_element_type=F32) + jnp.dot(xh, wl, preferred_element_type=F32)
              + jnp.dot(xl, wh, preferred_element_type=F32)) + b
    lane = lax.broadcasted_iota(I32, (1, LANES), 1)
    lanef = lane.astype(F32)
    v = jnp.where(lane < N_EXPERTS, logits, NINF)
    l0 = jnp.max(v, axis=1, keepdims=True)
    i0 = jnp.min(jnp.where(v == l0, lanef, float(LANES)), axis=1, keepdims=True)
    v = jnp.where(lanef == i0, NINF, v)
    l1 = jnp.max(v, axis=1, keepdims=True)
    i1 = jnp.min(jnp.where(v == l1, lanef, float(LANES)), axis=1, keepdims=True)
    e1 = jnp.exp(l1 - l0)
    g0 = 1.0 / (1.0 + e1)
    g1 = e1 / (1.0 + e1)
    return jnp.where(lane == 0, i0, jnp.where(lane == 1, i1, jnp.where(lane == 2, g0, jnp.where(lane == 3, g1, 0.0))))


def _outproj_ln_kernel(x_ref, a_ref, b_ref, wa_ref, wb_ref, g_ref, bb_ref, *rest):
    mix = (jnp.dot(a_ref[...], wa_ref[...], preferred_element_type=F32)
           + jnp.dot(b_ref[...], wb_ref[...], preferred_element_type=F32))
    y = _layer_norm(DN_ALPHA * x_ref[...] + mix, g_ref[...], bb_ref[...])
    if len(rest) == 2:
        xo_ref, xb_ref = rest
    else:
        wr_ref, br_ref, xo_ref, xb_ref, rt_ref = rest
        rt_ref[...] = _route_top2(y, wr_ref[...], br_ref[...])
    xo_ref[...] = y
    xb_ref[...] = y.astype(BF16)


def _outproj_ln(x, oa, ob, wa, wb, g, b, router=None, tm=ROW_TILE):
    n, d = x.shape
    ka, kb = oa.shape[1], ob.shape[1]
    row = lambda i: (i, 0)
    fixed = lambda i: (0, 0)
    in_specs = [pl.BlockSpec((tm, d), row), pl.BlockSpec((tm, ka), row), pl.BlockSpec((tm, kb), row),
                pl.BlockSpec((ka, d), fixed), pl.BlockSpec((kb, d), fixed),
                pl.BlockSpec((1, d), fixed), pl.BlockSpec((1, d), fixed)]
    out_specs = [pl.BlockSpec((tm, d), row), pl.BlockSpec((tm, d), row)]
    out_shape = [jax.ShapeDtypeStruct((n, d), F32), jax.ShapeDtypeStruct((n, d), BF16)]
    args = [x, oa, ob, wa, wb, g.reshape(1, d), b.reshape(1, d)]
    if router is not None:
        in_specs += [pl.BlockSpec((d, LANES), fixed), pl.BlockSpec((1, LANES), fixed)]
        out_specs.append(pl.BlockSpec((tm, LANES), row))
        out_shape.append(jax.ShapeDtypeStruct((n, LANES), F32))
        args += list(router)
    return pl.pallas_call(
        _outproj_ln_kernel,
        grid=(n // tm,),
        in_specs=in_specs,
        out_specs=out_specs,
        out_shape=out_shape,
        compiler_params=_cparams(("parallel",)),
        name="outproj_ln",
    )(*args)


def _ffn_ln_kernel(x_ref, xb_ref, wg_ref, wu_ref, wd_ref, g_ref, b_ref, xo_ref, xob_ref):
    xb = xb_ref[...]
    h = jax.nn.silu(jnp.dot(xb, wg_ref[...], preferred_element_type=F32)) * jnp.dot(
        xb, wu_ref[...], preferred_element_type=F32)
    ffn = jnp.dot(h.astype(BF16), wd_ref[...], preferred_element_type=F32)
    y = _layer_norm(DN_ALPHA * x_ref[...] + ffn, g_ref[...], b_ref[...])
    xo_ref[...] = y
    xob_ref[...] = y.astype(BF16)


def _ffn_ln(x, xb, wg, wu, wd, g, b, tm=ROW_TILE):
    n, d = x.shape
    fdim = wg.shape[1]
    row = lambda i: (i, 0)
    fixed = lambda i: (0, 0)
    once = pl.Buffered(1)
    return pl.pallas_call(
        _ffn_ln_kernel,
        grid=(n // tm,),
        in_specs=[pl.BlockSpec((tm, d), row), pl.BlockSpec((tm, d), row),
                  pl.BlockSpec((d, fdim), fixed, pipeline_mode=once),
                  pl.BlockSpec((d, fdim), fixed, pipeline_mode=once),
                  pl.BlockSpec((fdim, d), fixed, pipeline_mode=once),
                  pl.BlockSpec((1, d), fixed), pl.BlockSpec((1, d), fixed)],
        out_specs=[pl.BlockSpec((tm, d), row), pl.BlockSpec((tm, d), row)],
        out_shape=[jax.ShapeDtypeStruct((n, d), F32), jax.ShapeDtypeStruct((n, d), BF16)],
        compiler_params=_cparams(("parallel",)),
        name="ffn_ln",
    )(x, xb, wg, wu, wd, g.reshape(1, d), b.reshape(1, d))


def _top_n_mask(v, n, axis):
    idx = lax.broadcasted_iota(I32, v.shape, axis).astype(F32)
    sel = jnp.zeros(v.shape, F32)
    for _ in range(n):
        mx = jnp.max(v, axis=axis, keepdims=True)
        first = jnp.min(jnp.where(v == mx, idx, float(v.shape[axis])), axis=axis, keepdims=True)
        pick = idx == first
        sel = jnp.where(pick, 1.0, sel)
        v = jnp.where(pick, NINF, v)
    return sel


DV_PAD = 16
TQ = 256


def _with_ones(vt):
    return jnp.concatenate([vt, jnp.ones((DV_PAD, vt.shape[1]), vt.dtype)], axis=0)


def _mm_t_kernel(x_ref, wv_ref, ws_ref, *rest, tk):
    x = x_ref[...].astype(BF16)
    vt_ref, small_ref = rest[-2:] if len(rest) == 2 else rest[1:3]
    acc = jnp.dot(x, wv_ref[...], preferred_element_type=F32)
    for cc in range(acc.shape[0] // tk):
        vt_ref[cc] = acc[cc * tk:(cc + 1) * tk, :].T.astype(vt_ref.dtype)
    small_ref[...] = jnp.dot(x, ws_ref[...], preferred_element_type=F32)
    if len(rest) == 4:
        rest[3][...] = jnp.dot(x, rest[0][...], preferred_element_type=F32).astype(rest[3].dtype)


def _proj_values(x, w_val, w_small, w_plain, tm, tk):
    n, d = x.shape
    pv, ps = w_val.shape[1], w_small.shape[1]
    row = lambda i: (i, 0)
    fixed = lambda i: (0, 0)
    in_specs = [pl.BlockSpec((tm, d), row), pl.BlockSpec((d, pv), fixed), pl.BlockSpec((d, ps), fixed)]
    out_specs = [pl.BlockSpec((tm // tk, pv, tk), lambda i: (i, 0, 0)), pl.BlockSpec((tm, ps), row)]
    out_shape = [jax.ShapeDtypeStruct((n // tk, pv, tk), BF16), jax.ShapeDtypeStruct((n, ps), F32)]
    args = [x, w_val, w_small]
    if w_plain is not None:
        pp = w_plain.shape[1]
        in_specs.append(pl.BlockSpec((d, pp), fixed))
        out_specs.append(pl.BlockSpec((tm, pp), row))
        out_shape.append(jax.ShapeDtypeStruct((n, pp), BF16))
        args.append(w_plain)
    return pl.pallas_call(
        functools.partial(_mm_t_kernel, tk=tk),
        grid=(n // tm,),
        in_specs=in_specs,
        out_specs=out_specs,
        out_shape=out_shape,
        compiler_params=_cparams(("parallel",)),
        name="proj_t",
    )(*args)


def _normalize_t(acc, width):
    return acc[:width] * _safe_recip(acc[width:width + 1])


def _pipe_flash(n, ns, qk, vt_at, bias_at, bufs, dv, tk, tq):
    sa, sb, pa, pb = bufs

    def softmax_into(p_ref, j, st, m):
        m_new = jnp.maximum(m, jnp.max(st, axis=0, keepdims=True))
        p_ref[j] = jnp.exp2((st - m_new).astype(BF16))
        return m_new, jnp.exp2(m - m_new)

    def half(c, carry, s_cur, s_nxt, p_prev, p_cur):
        if s_nxt is not None:
            nxt = jnp.minimum(c + 1, n - 1)
            for j in range(ns):
                s_nxt[j] = qk(nxt, j)
        out = []
        for j in range(ns):
            m, acc, alpha = carry[j]
            acc = alpha * acc + jnp.dot(vt_at(c - 1, j), p_prev[j], preferred_element_type=F32)
            m, alpha = softmax_into(p_cur, j, s_cur[j] + bias_at(c, j), m)
            out.append((m, acc, alpha))
        return tuple(out)

    for j in range(ns):
        sa[j] = qk(0, j)
    first = []
    for j in range(ns):
        sb[j] = qk(jnp.minimum(1, n - 1), j)
        m, alpha = softmax_into(pa, j, sa[j] + bias_at(0, j), jnp.full((1, tq), M_INIT, F32))
        first.append((m, jnp.zeros((dv, tq), F32), alpha))

    def body(t, carry):
        carry = half(2 * t + 1, carry, sb, sa, pa, pb)
        return half(2 * t + 2, carry, sa, sb, pb, pa)

    carry = lax.fori_loop(0, (n - 1) // 2, body, tuple(first))

    def odd_tail(carry):
        return half(n - 1, carry, sb, None, pa, pa)

    carry = lax.cond(n % 2 == 0, odd_tail, lambda carry: carry, carry)
    return tuple(alpha * acc + jnp.dot(vt_at(n - 1, j), pa[j], preferred_element_type=F32)
                 for j, (_, acc, alpha) in enumerate(carry))


def _pipe_scratch(ns, tk, tq):
    return [pltpu.VMEM((ns, tk, tq), F32)] * 2 + [pltpu.VMEM((ns, tk, tq), BF16)] * 2


def _causal_t(t):
    return jnp.where(lax.broadcasted_iota(I32, (t, t), 0) <= lax.broadcasted_iota(I32, (t, t), 1), 0.0, NINF)


def _diff_kernel(lam_ref, sub_ref, q_ref, k_ref, vt_ref, o_ref, *bufs, tq, lam_init):
    i = pl.program_id(2)
    lp = lam_ref[...]
    lam = (jnp.exp(jnp.sum(lp[0:1] * lp[1:2], axis=1, keepdims=True))
           - jnp.exp(jnp.sum(lp[2:3] * lp[3:4], axis=1, keepdims=True)) + lam_init)
    qs = _split_halves(q_ref[...], roped=True)
    n = i + 1
    bufs, tab_ref = bufs[:4], bufs[4]
    tab_ref[0] = jnp.zeros((tq, tq), F32)
    tab_ref[1] = _causal_t(tq)

    def qk(c, j):
        off = pl.multiple_of(c * tq, tq)
        return _dot_nt(k_ref[pl.ds(off, tq), :], qs[j])

    def bias_at(c, j):
        return tab_ref[jnp.where(c == i, 1, 0)]

    outs = _pipe_flash(n, 2, qk, lambda c, j: _with_ones(vt_ref[c]), bias_at, bufs, LANES + DV_PAD, tq, tq)
    o = _normalize_t(outs[0], LANES) - lam * _normalize_t(outs[1], LANES)
    o = o * lax.rsqrt(jnp.mean(o * o, axis=0, keepdims=True) + LN_EPS)
    o = o * sub_ref[...] * (1.0 - lam_init)
    o_ref[...] = o.T.astype(o_ref.dtype)


def _diff_attention(rp, vt, lam_params, subln, lam_init, tq=TQ):
    b, s, _ = rp.shape
    nk = s // tq
    return pl.pallas_call(
        functools.partial(_diff_kernel, tq=tq, lam_init=lam_init),
        grid=(b, A_HEADS, nk),
        in_specs=[pl.BlockSpec((4, HEAD_DIM), lambda bi, h, i: (0, 0)),
                  pl.BlockSpec((LANES, 1), lambda bi, h, i: (0, 0)),
                  pl.BlockSpec((None, tq, LANES), lambda bi, h, i: (bi, i, h)),
                  pl.BlockSpec((None, s, LANES), lambda bi, h, i: (bi, 0, A_HEADS + h)),
                  pl.BlockSpec((None, nk, LANES, tq), lambda bi, h, i: (bi, 0, h, 0))],
        out_specs=pl.BlockSpec((None, tq, LANES), lambda bi, h, i: (bi, i, h)),
        out_shape=jax.ShapeDtypeStruct((b, s, A_HEADS * LANES), BF16),
        scratch_shapes=_pipe_scratch(2, tq, tq) + [pltpu.VMEM((2, tq, tq), F32)],
        compiler_params=_cparams(("parallel", "parallel", "arbitrary")),
        name="diff_attn",
    )(lam_params, subln.reshape(LANES, 1), rp, rp, vt)


def _fold8(x, op):
    acc = x[0:8]
    for r in range(1, x.shape[0] // 8):
        acc = op(acc, x[r * 8:(r + 1) * 8])
    return acc


def _dsa_kernel(iq_ref, ikk_ref, iw_ref, q_ref, k_ref, vt_ref, o_ref, s_ref, j_ref, *bufs, tq, ksel, seq_len):
    tk = tq
    i = pl.program_id(1)
    nch = i + 1
    ksel_f = float(ksel)
    k_loc = lax.broadcasted_iota(I32, (tk, tq), 0)
    q_loc = lax.broadcasted_iota(I32, (tk, tq), 1)
    qpos = i * tq + lax.broadcasted_iota(I32, (1, tq), 1)

    iq = iq_ref[...]
    iwt = iw_ref[...].T
    iqh = []
    for pair in range(IDX_HEADS // 2):
        iqh += list(_split_halves(iq[:, pair * LANES:(pair + 1) * LANES], roped=True))

    def scores(c):
        off = pl.multiple_of(c * tk, tk)
        kk = ikk_ref[pl.ds(off, tk), :]
        sc = iwt[0:1] * jnp.maximum(_dot_nt(kk, iqh[0]), 0.0)
        for h in range(1, IDX_HEADS):
            sc = sc + iwt[h:h + 1] * jnp.maximum(_dot_nt(kk, iqh[h]), 0.0)
        return sc

    def full_body(c, carry):
        mx, mn = carry
        sc = scores(c)
        s_ref[c] = sc
        return jnp.maximum(mx, _fold8(sc, jnp.maximum)), jnp.minimum(mn, _fold8(sc, jnp.minimum))

    mx, mn = lax.fori_loop(0, i, full_body, (jnp.full((8, tq), -BIG, F32), jnp.full((8, tq), BIG, F32)))
    sc = scores(i)
    causal = k_loc <= q_loc
    s_ref[i] = jnp.where(causal, sc, NEG)
    mx = jnp.maximum(mx, _fold8(jnp.where(causal, sc, -BIG), jnp.maximum))
    mn = jnp.minimum(mn, _fold8(jnp.where(causal, sc, BIG), jnp.minimum))
    smax = jnp.max(mx, axis=0, keepdims=True)
    smin = jnp.min(mn, axis=0, keepdims=True)

    def count_where(ind):
        def body(c, acc):
            return acc + _fold8(ind(s_ref[c], c * tk + k_loc), jnp.add)
        acc = lax.fori_loop(0, nch, body, jnp.zeros((8, tq), F32))
        return jnp.sum(acc, axis=0, keepdims=True)

    def count_ge(th):
        return count_where(lambda x, kidx: jnp.where(x >= th, 1.0, 0.0))

    def max_below(th):
        def body(c, acc):
            x = s_ref[c]
            return jnp.maximum(acc, _fold8(jnp.where(x < th, x, NINF), jnp.maximum))
        acc = lax.fori_loop(0, nch, body, jnp.full((8, tq), NINF, F32))
        return jnp.max(acc, axis=0, keepdims=True)

    n_causal = (qpos + 1).astype(F32)
    take_all = n_causal <= ksel_f
    done0 = jnp.where(take_all, 1.0, 0.0)
    hi0 = smax + (jnp.abs(smax) * 2.0 ** -20 + 1e-30)

    def bisect(lo, hi):
        mid = lo + (hi - lo) * 0.5
        ge = count_ge(mid) >= ksel_f
        return jnp.where(ge, mid, lo), jnp.where(ge, hi, mid)

    lo, hi = lax.fori_loop(0, N_BISECT, lambda _, c: bisect(*c), (smin, hi0))

    def snap_body(carry):
        lo, hi, th, c_th, done, _ = carry
        lo, hi = bisect(lo, hi)
        t1 = max_below(hi)
        c1 = count_ge(t1)
        ok = c1 >= ksel_f
        th = jnp.where(done > 0.0, th, t1)
        c_th = jnp.where(done > 0.0, c_th, c1)
        hi = jnp.where(ok, hi, t1)
        done = jnp.where(ok, 1.0, done)
        return lo, hi, th, c_th, done, jnp.sum(1.0 - done)

    _, _, th, c_ge, _, _ = lax.while_loop(lambda c: c[5] > 0.0, snap_body,
                                          (lo, hi, smax, jnp.zeros((1, tq), F32), done0, jnp.sum(1.0 - done0)))

    need_tb = jnp.where(take_all, 0.0, jnp.where(c_ge > ksel_f, 1.0, 0.0))
    j_ref[...] = jnp.full((8, tq), seq_len - 1, I32)

    @pl.when(jnp.sum(need_tb) > 0.0)
    def _():
        need = ksel_f - count_where(lambda x, kidx: jnp.where(x > th, 1.0, 0.0))

        def jb(_, carry):
            lo_j, hi_j = carry
            mid = (lo_j + hi_j) // 2
            cnt = count_where(lambda x, kidx: jnp.where(x == th, jnp.where(kidx <= mid, 1.0, 0.0), 0.0))
            ge = cnt >= need
            return jnp.where(ge, lo_j, mid), jnp.where(ge, mid, hi_j)

        n_it = int(math.ceil(math.log2(seq_len))) + 1
        _, hi_j = lax.fori_loop(0, n_it, jb, (jnp.full((1, tq), -1, I32), jnp.full((1, tq), seq_len - 1, I32)))
        j_ref[...] = jnp.broadcast_to(hi_j, (8, tq))

    jsel = j_ref[0:1, :]

    def bias_body(c, _):
        x = s_ref[c]
        kidx = c * tk + k_loc
        keep = jnp.where(x > th, 0.0, jnp.where(x == th, jnp.where(kidx <= jsel, 0.0, NINF), NINF))
        keep = jnp.where(take_all, 0.0, keep)
        s_ref[c] = jnp.where(kidx <= qpos, keep, NINF)
        return 0

    lax.fori_loop(0, nch, bias_body, 0)

    for p in range(B_HEADS // 2):
        qs = _split_halves(q_ref[:, p * LANES:(p + 1) * LANES], roped=True)

        def qk(c, j, qs=qs):
            off = pl.multiple_of(c * tk, tk)
            return _dot_nt(k_ref[pl.ds(off, tk), :], qs[j])

        outs = _pipe_flash(nch, 2, qk, lambda c, j: _with_ones(vt_ref[c, j * HEAD_DIM:(j + 1) * HEAD_DIM, :]),
                           lambda c, j: s_ref[c], bufs, HEAD_DIM + DV_PAD, tk, tq)
        o = jnp.concatenate([_normalize_t(outs[0], HEAD_DIM), _normalize_t(outs[1], HEAD_DIM)], axis=0)
        o_ref[:, p * LANES:(p + 1) * LANES] = o.T.astype(o_ref.dtype)


def _dsa_attention(rp, vt, iw, tq=TQ):
    b, s, _ = rp.shape
    ksel = min(DSA_TOPK, s // 4)
    nk = s // tq
    return pl.pallas_call(
        functools.partial(_dsa_kernel, tq=tq, ksel=ksel, seq_len=s),
        grid=(b, nk),
        in_specs=[pl.BlockSpec((None, tq, 2 * LANES), lambda bi, i: (bi, i, 6)),
                  pl.BlockSpec((None, s, LANES), lambda bi, i: (bi, 0, 15)),
                  pl.BlockSpec((None, tq, LANES), lambda bi, i: (bi, i, 0)),
                  pl.BlockSpec((None, tq, 4 * LANES), lambda bi, i: (bi, i, 2)),
                  pl.BlockSpec((None, s, LANES), lambda bi, i: (bi, 0, 14)),
                  pl.BlockSpec((None, nk, LANES, tq), lambda bi, i: (bi, 0, 4, 0))],
        out_specs=pl.BlockSpec((None, tq, 4 * LANES), lambda bi, i: (bi, i, 0)),
        out_shape=jax.ShapeDtypeStruct((b, s, 4 * LANES), BF16),
        scratch_shapes=[pltpu.VMEM((nk, tq, tq), F32), pltpu.VMEM((8, tq), I32)] + _pipe_scratch(2, tq, tq),
        compiler_params=_cparams(("parallel", "arbitrary")),
        name="dsa_attn",
    )(rp, rp, iw, rp, rp, vt)


def _moba_kernel(q_ref, k_ref, vt_ref, o_ref, km_ref, sel_ref, *bufs, seq_len, n_sel):
    tq = MOBA_BLOCK
    qb = pl.program_id(2)

    @pl.when(qb == 0)
    def _():
        j = lax.broadcasted_iota(I32, (LANES, seq_len), 0)
        s = lax.broadcasted_iota(I32, (LANES, seq_len), 1)
        avg = jnp.where(s // MOBA_BLOCK == j, 1.0 / MOBA_BLOCK, 0.0).astype(BF16)
        km_ref[...] = jnp.dot(avg, k_ref[...], preferred_element_type=F32)

    nbp = sel_ref.shape[1]
    km = km_ref[0:nbp, :]
    qs = _split_halves(q_ref[...], roped=True)
    blk = lax.broadcasted_iota(I32, (nbp, tq), 0)
    past = blk < qb
    for j in range(2):
        gate = lax.dot_general(km, qs[j].astype(F32), (((1,), (1,)), ((), ())),
                               precision=lax.Precision.HIGHEST, preferred_element_type=F32)
        gate = jnp.where(blk < seq_len // MOBA_BLOCK, jnp.where(past, gate, NEG), NINF)
        sel = _top_n_mask(gate, n_sel, 0)
        sel_ref[j] = jnp.where(past, jnp.where(sel > 0.5, 0.0, NINF), NINF)
    own = _causal_t(tq)
    n = qb + 1

    def qk(c, j):
        off = pl.multiple_of(c * tq, tq)
        return _dot_nt(k_ref[pl.ds(off, tq), :], qs[j])

    def bias_at(c, j):
        chosen = sel_ref[j, pl.ds(c, 1), :]
        return jnp.where(c == qb, own, chosen)

    outs = _pipe_flash(n, 2, qk, lambda c, j: _with_ones(vt_ref[c, j * HEAD_DIM:(j + 1) * HEAD_DIM, :]), bias_at, bufs,
                       HEAD_DIM + DV_PAD, tq, tq)
    o = jnp.concatenate([_normalize_t(outs[0], HEAD_DIM), _normalize_t(outs[1], HEAD_DIM)], axis=0)
    o_ref[...] = o.T.astype(o_ref.dtype)


def _moba_attention(rp, vt):
    b, s, _ = rp.shape
    tq = MOBA_BLOCK
    nb = s // tq
    n_sel = max(1, min(MOBA_TOPK, nb - 1))
    npair = C_HEADS // 2
    return pl.pallas_call(
        functools.partial(_moba_kernel, seq_len=s, n_sel=n_sel),
        grid=(b, npair, nb),
        in_specs=[pl.BlockSpec((None, tq, LANES), lambda bi, h, i: (bi, i, h)),
                  pl.BlockSpec((None, s, LANES), lambda bi, h, i: (bi, 0, npair + h)),
                  pl.BlockSpec((None, nb, LANES, tq), lambda bi, h, i: (bi, 0, h, 0))],
        out_specs=pl.BlockSpec((None, tq, LANES), lambda bi, h, i: (bi, i, h)),
        out_shape=jax.ShapeDtypeStruct((b, s, npair * LANES), BF16),
        scratch_shapes=[pltpu.VMEM((LANES, LANES), F32), pltpu.VMEM((2, -(-nb // 8) * 8, tq), F32)]
        + _pipe_scratch(2, tq, tq),
        compiler_params=_cparams(("parallel", "parallel", "arbitrary")),
        name="moba_attn",
    )(rp, rp, vt)


def _cmp_kernel(r_ref, pe_ref, w1_ref, w2_ref, o_ref):
    r = r_ref[...]
    w1 = w1_ref[...]
    half = r.shape[1]
    u = jnp.dot(r, w1[:half], preferred_element_type=F32)
    v = jnp.dot(r, w1[half:], preferred_element_type=F32)
    c = jnp.dot(pe_ref[...], w1, preferred_element_type=F32)[0:1]
    pre = u + pltpu.roll(v, r.shape[0] - 1, 0) + c
    o_ref[...] = jnp.dot(jax.nn.gelu(pre).astype(BF16), w2_ref[...],
                         preferred_element_type=F32).astype(o_ref.dtype)


def _nsa_compress(r, pe, w1, w2):
    b, _, nc, wdt = r.shape
    hid = w1.shape[2]
    return pl.pallas_call(
        _cmp_kernel,
        grid=(b, 4),
        in_specs=[pl.BlockSpec((None, None, nc, wdt), lambda bi, t: (bi, t, 0, 0)),
                  pl.BlockSpec((None, 8, 2 * wdt), lambda bi, t: (t // 2, 0, 0)),
                  pl.BlockSpec((None, 2 * wdt, hid), lambda bi, t: (t // 2, 0, 0)),
                  pl.BlockSpec((None, hid, HEAD_DIM), lambda bi, t: (t // 2, 0, 0))],
        out_specs=pl.BlockSpec((None, None, nc, HEAD_DIM), lambda bi, t: (bi, t, 0, 0)),
        out_shape=jax.ShapeDtypeStruct((b, 4, nc, HEAD_DIM), BF16),
        compiler_params=_cparams(("parallel", "arbitrary")),
        name="nsa_compress",
    )(r, pe, w1, w2)


def _nsa_kernel(qr_ref, qw_ref, dg_ref, gb_ref, kc_ref, vct_ref, ks_ref, vst_ref, kw_ref, vwt_ref,
                o_ref, sel_ref, wb_ref, *bufs, tq, seq_len):
    tk = tq
    i = pl.program_id(1)
    nch = i + 1
    nc = seq_len // NSA_CMP_STRIDE
    n_sb = seq_len // NSA_SLC_BLOCK
    n_sel = min(NSA_SLC_TOPK, n_sb)
    k_loc = lax.broadcasted_iota(I32, (tk, tq), 0)
    q_loc = lax.broadcasted_iota(I32, (tk, tq), 1)
    qpos = i * tq + lax.broadcasted_iota(I32, (1, tq), 1)

    gates_t = jax.nn.sigmoid(dg_ref[...] + gb_ref[...]).T
    kc = kc_ref[...]
    cmp_end = lax.broadcasted_iota(I32, (nc, 1), 0) * NSA_CMP_STRIDE + (NSA_CMP_LEN - 1)
    cbias = jnp.where(cmp_end <= qpos, 0.0, NINF)

    nh = D_HEADS // 2
    q_rot = [_split_halves(qr_ref[:, p * LANES:(p + 1) * LANES], roped=True) for p in range(nh)]
    q_raw = [_split_halves(qw_ref[:, p * LANES:(p + 1) * LANES]) for p in range(nh)]
    cbias4 = jnp.concatenate([cbias] * nh, axis=1)
    o_cmp, psum = [], []
    for g in range(2):
        qg = jnp.concatenate([q_raw[p][g] for p in range(nh)], axis=0)
        s = _dot_nt(kc, qg) + cbias4
        m = jnp.max(s, axis=0, keepdims=True)
        e = jnp.exp2(s - jnp.where(m == NINF, 0.0, m))
        pc = e * _safe_recip(jnp.sum(e, axis=0, keepdims=True))
        psum.append(sum(pc[:, p * tq:(p + 1) * tq] for p in range(nh)))
        o_cmp.append(jnp.dot(vct_ref[g], pc.astype(BF16), preferred_element_type=F32))

    per = tk // NSA_SLC_BLOCK
    nbp = -(-n_sb // 8) * 8
    cn = lax.broadcasted_iota(I32, (nbp, nc), 1) * NSA_CMP_STRIDE
    sj = lax.broadcasted_iota(I32, (nbp, nc), 0) * NSA_SLC_BLOCK
    shares = jnp.where((cn <= sj + NSA_SLC_BLOCK - 1) & (cn + NSA_CMP_LEN - 1 >= sj), 1.0, 0.0)
    blk = lax.broadcasted_iota(I32, (nbp, tq), 0)
    cur = qpos // NSA_SLC_BLOCK
    causal_b = blk <= cur
    forced = (blk == 0) | ((blk >= cur - 1) & causal_b)
    for g in range(2):
        imp = jnp.dot(shares, psum[g], precision=lax.Precision.HIGHEST, preferred_element_type=F32)
        val = jnp.where(forced, BIG, jnp.where(causal_b, imp, NEG))
        val = jnp.where(blk < n_sb, val, NINF)
        rowb = jnp.where(_top_n_mask(val, n_sel, 0) > 0.5, 0.0, NINF)
        for c in range(seq_len // tk):
            sel_ref[g, c] = jnp.concatenate([rowb[c * per:(c + 1) * per], jnp.zeros((8 - per, tq), F32)], axis=0)

    wb_ref[0] = jnp.where(k_loc <= q_loc, 0.0, NINF)
    wb_ref[1] = jnp.zeros((tk, tq), F32)
    wb_ref[2] = jnp.where(k_loc > q_loc, 0.0, NINF)
    n_wc = NSA_WINDOW // tk + 1
    w_first = jnp.maximum(i - (n_wc - 1), 0)
    n_w = i - w_first + 1

    for p in range(D_HEADS // 2):
        qs = q_rot[p]

        def qk_s(c, j, qs=qs):
            off = pl.multiple_of(c * tk, tk)
            return _dot_nt(ks_ref[pl.ds(off, tk), :], qs[j])

        def bias_s(c, j):
            rows = sel_ref[j, c]
            tile = jnp.concatenate([jnp.broadcast_to(rows[r:r + 1], (NSA_SLC_BLOCK, tq)) for r in range(per)], axis=0)
            return tile + wb_ref[jnp.where(c == i, 0, 1)]

        o_slc = _pipe_flash(nch, 2, qk_s, lambda c, j: _with_ones(vst_ref[c, j * HEAD_DIM:(j + 1) * HEAD_DIM, :]),
                            bias_s, bufs, HEAD_DIM + DV_PAD, tk, tq)

        def qk_w(c, j, qs=qs):
            off = pl.multiple_of((w_first + c) * tk, tk)
            return _dot_nt(kw_ref[pl.ds(off, tk), :], qs[j])

        def bias_w(c, j):
            return wb_ref[i - (w_first + c)]

        o_win = _pipe_flash(n_w, 2, qk_w,
                            lambda c, j: _with_ones(vwt_ref[w_first + c, j * HEAD_DIM:(j + 1) * HEAD_DIM, :]), bias_w, bufs,
                            HEAD_DIM + DV_PAD, tk, tq)
        outs = []
        for g in range(2):
            h = g * (D_HEADS // 2) + p
            outs.append(gates_t[3 * h:3 * h + 1] * o_cmp[g][:, p * tq:(p + 1) * tq]
                        + gates_t[3 * h + 1:3 * h + 2] * _normalize_t(o_slc[g], HEAD_DIM)
                        + gates_t[3 * h + 2:3 * h + 3] * _normalize_t(o_win[g], HEAD_DIM))
        o_ref[:, p * LANES:(p + 1) * LANES] = jnp.concatenate(outs, axis=0).T.astype(o_ref.dtype)


def _nsa_attention(rp, pp, vt, dg, gate_b, kcmp, vcmp, tq=TQ):
    b, s, _ = rp.shape
    nk = s // tq
    nc = s // NSA_CMP_STRIDE
    assert NSA_WINDOW == 2 * tq
    n_wc = NSA_WINDOW // tq + 1
    vct = vcmp.reshape(b, nc, 2, HEAD_DIM).transpose(0, 2, 3, 1)
    full = lambda t: pl.BlockSpec((None, s, LANES), lambda bi, i: (bi, 0, t))
    vspec = lambda t: pl.BlockSpec((None, nk, LANES, tq), lambda bi, i: (bi, 0, t, 0))
    return pl.pallas_call(
        functools.partial(_nsa_kernel, tq=tq, seq_len=s),
        grid=(b, nk),
        in_specs=[pl.BlockSpec((None, tq, 4 * LANES), lambda bi, i: (bi, i, 2)),
                  pl.BlockSpec((None, tq, 4 * LANES), lambda bi, i: (bi, i, 0)),
                  pl.BlockSpec((None, tq, LANES), lambda bi, i: (bi, i, 0)),
                  pl.BlockSpec((1, LANES), lambda bi, i: (0, 0)),
                  pl.BlockSpec((None, nc, LANES), lambda bi, i: (bi, 0, 0)),
                  pl.BlockSpec((None, 2, HEAD_DIM, nc), lambda bi, i: (bi, 0, 0, 0)),
                  full(12), vspec(4), full(13), vspec(5)],
        out_specs=pl.BlockSpec((None, tq, 4 * LANES), lambda bi, i: (bi, i, 0)),
        out_shape=jax.ShapeDtypeStruct((b, s, 4 * LANES), BF16),
        scratch_shapes=[pltpu.VMEM((2, nk, 8, tq), F32), pltpu.VMEM((n_wc, tq, tq), F32)]
        + _pipe_scratch(2, tq, tq),
        compiler_params=_cparams(("parallel", "arbitrary")),
        name="nsa_attn",
    )(rp, pp, dg, gate_b, kcmp, vct, rp, vt, rp, vt)


def _cast_kernel(x_ref, o_ref):
    o_ref[...] = x_ref[...].astype(o_ref.dtype)


def _layer_to_bf16(w, layer):
    _, e, r, c = w.shape
    tr = 1 << ((2 ** 21 // c).bit_length() - 1)
    out = pl.pallas_call(
        _cast_kernel,
        grid=(e * r // tr,),
        in_specs=[pl.BlockSpec((None, tr, c), lambda i: (layer, i, 0))],
        out_specs=pl.BlockSpec((tr, c), lambda i: (i, 0)),
        out_shape=jax.ShapeDtypeStruct((e * r, c), BF16),
        compiler_params=_cparams(("parallel",)),
        name="cast_bf16",
    )(w.reshape(w.shape[0], e * r, c))
    return out.reshape(e, r, c)


def _moe_ffn_kernel(be_ref, nu_ref, x_ref, wg_ref, wu_ref, wd_ref, o_ref):
    used = pl.program_id(0) < nu_ref[0]

    @pl.when(used)
    def _():
        x = x_ref[...]
        h = jax.nn.silu(jnp.dot(x, wg_ref[...], preferred_element_type=F32)) * jnp.dot(
            x, wu_ref[...], preferred_element_type=F32)
        o_ref[...] = jnp.dot(h.astype(BF16), wd_ref[...], preferred_element_type=F32).astype(o_ref.dtype)

    @pl.when(jnp.logical_not(used))
    def _():
        o_ref[...] = jnp.zeros(o_ref.shape, o_ref.dtype)


def _moe_ffn(x_sorted, block_e, n_used, wg, wu, wd):
    ns, d = x_sorted.shape
    fdim = wg.shape[2]
    once = pl.Buffered(1)
    grid_spec = pltpu.PrefetchScalarGridSpec(
        num_scalar_prefetch=2,
        grid=(ns // MOE_TM,),
        in_specs=[pl.BlockSpec((MOE_TM, d), lambda i, be, nu: (jnp.minimum(i, nu[0] - 1), 0)),
                  pl.BlockSpec((None, d, fdim), lambda i, be, nu: (be[i], 0, 0), pipeline_mode=once),
                  pl.BlockSpec((None, d, fdim), lambda i, be, nu: (be[i], 0, 0), pipeline_mode=once),
                  pl.BlockSpec((None, fdim, d), lambda i, be, nu: (be[i], 0, 0), pipeline_mode=once)],
        out_specs=pl.BlockSpec((MOE_TM, d), lambda i, be, nu: (i, 0)),
    )
    return pl.pallas_call(
        _moe_ffn_kernel,
        grid_spec=grid_spec,
        out_shape=jax.ShapeDtypeStruct((ns, d), BF16),
        compiler_params=_cparams(("arbitrary",)),
        name="moe_ffn",
    )(block_e, n_used, x_sorted, wg, wu, wd)


def _combine_ln_kernel(x_ref, y0_ref, y1_ref, rt_ref, g_ref, b_ref, xo_ref, xb_ref):
    rt = rt_ref[...]
    ffn = rt[:, 2:3] * y0_ref[...].astype(F32) + rt[:, 3:4] * y1_ref[...].astype(F32)
    y = _layer_norm(DN_ALPHA * x_ref[...] + ffn, g_ref[...], b_ref[...])
    xo_ref[...] = y
    xb_ref[...] = y.astype(BF16)


def _combine_ln(x, y0, y1, rt, g, b, tm=ROW_TILE):
    n, d = x.shape
    row = lambda i: (i, 0)
    fixed = lambda i: (0, 0)
    return pl.pallas_call(
        _combine_ln_kernel,
        grid=(n // tm,),
        in_specs=[pl.BlockSpec((tm, d), row), pl.BlockSpec((tm, d), row), pl.BlockSpec((tm, d), row),
                  pl.BlockSpec((tm, LANES), row), pl.BlockSpec((1, d), fixed), pl.BlockSpec((1, d), fixed)],
        out_specs=[pl.BlockSpec((tm, d), row), pl.BlockSpec((tm, d), row)],
        out_shape=[jax.ShapeDtypeStruct((n, d), F32), jax.ShapeDtypeStruct((n, d), BF16)],
        compiler_params=_cparams(("parallel",)),
        name="moe_combine_ln",
    )(x, y0, y1, rt, g.reshape(1, d), b.reshape(1, d))


def _moe_layout(rt, n):
    e_flat = rt[:, 0:TOP_K].astype(I32).reshape(-1)
    nk = n * TOP_K
    onehot = (e_flat[:, None] == jnp.arange(N_EXPERTS, dtype=I32)[None, :]).astype(I32)
    rank = jnp.take_along_axis(jnp.cumsum(onehot, axis=0), e_flat[:, None], axis=1)[:, 0] - 1
    counts = jnp.sum(onehot, axis=0)
    padded = (counts + MOE_TM - 1) // MOE_TM * MOE_TM
    pad_end = jnp.cumsum(padded)
    pad_start = pad_end - padded
    grp_start = jnp.cumsum(counts) - counts
    slot = pad_start[e_flat] + rank
    n_blocks = -(-nk // MOE_TM) + N_EXPERTS
    n_slots = n_blocks * MOE_TM
    order = jnp.argsort(e_flat, stable=True).astype(I32)
    sl = jnp.arange(n_slots, dtype=I32)
    slot_e = jnp.minimum(jnp.searchsorted(pad_end, sl, side='right'), N_EXPERTS - 1).astype(I32)
    within = sl - pad_start[slot_e]
    valid = within < counts[slot_e]
    src = jnp.where(valid, grp_start[slot_e] + within, 0)
    slot_tok = jnp.where(valid, order[src] // TOP_K, 0)
    n_used = (pad_end[-1] // MOE_TM).astype(I32).reshape(1)
    blk = jnp.arange(n_blocks, dtype=I32)
    block_e = slot_e[jnp.minimum(blk, n_used[0] - 1) * MOE_TM]
    return slot_tok, slot.reshape(n, TOP_K), block_e, n_used


def _pair_perm(n_heads):
    half = n_heads // 2
    cols = []
    for p in range(half):
        cols += list(range(p * HEAD_DIM, (p + 1) * HEAD_DIM))
        cols += list(range((half + p) * HEAD_DIM, (half + p + 1) * HEAD_DIM))
    return np.asarray(cols, dtype=np.int32)


def _pad_cols(w, width):
    return jnp.pad(w, ((0, 0), (0, width - w.shape[1])))


def _split_cols(w, sizes):
    out, off = [], 0
    for sz in sizes:
        out.append(w[:, off:off + sz])
        off += sz
    return out


def _even_layer(x, xb, w_in, w_out, lam_params, subln, lam_init, wg, wu, wd, ln, tabs, bsz, seq_len):
    n = x.shape[0]
    perm = _pair_perm(B_HEADS)
    aq, ak, av, bq, bk, bv, iq, ik, iw = _split_cols(w_in, EVEN_SIZES)
    w_rope = jnp.concatenate([aq * SCALE, ak, bq[:, perm] * SCALE, iq, bk, ik, ik], axis=1)
    w_rope = w_rope[:, _rope_layout(w_rope.shape[1])].astype(BF16)
    w_val = jnp.concatenate([av, bv], axis=1).astype(BF16)
    w_iw = _pad_cols(iw, LANES).astype(BF16)
    rp = _proj_rope(xb, w_rope, ROW_TILE, w_rope.shape[1] // 2, seq_len, tabs).reshape(bsz, seq_len, -1)
    vt, iwv = _proj_values(xb, w_val, w_iw, None, ROW_TILE, TQ)
    vt = vt.reshape(bsz, seq_len // TQ, -1, TQ)
    iwv = iwv.reshape(bsz, seq_len, LANES)
    o_a = _diff_attention(rp, vt, lam_params, subln, lam_init)
    o_b = _dsa_attention(rp, vt, iwv)
    half = w_out.shape[0] // 2
    wo_a = w_out[:half].astype(BF16)
    wo_b = w_out[half:][perm].astype(BF16)
    g_mix, b_mix, g_ffn, b_ffn = ln
    x1, x1b = _outproj_ln(x, o_a.reshape(n, -1), o_b.reshape(n, -1), wo_a, wo_b, g_mix, b_mix)
    return _ffn_ln(x1, x1b, wg.astype(BF16), wu.astype(BF16), wd.astype(BF16), g_ffn, b_ffn)


def _odd_layer(x, xb, w_in, w_out, gate_b, pe, phi_w1, phi_w2, w_router, b_router, wg, wu, wd, ln, tabs,
               bsz, seq_len):
    n = x.shape[0]
    perm = _pair_perm(D_HEADS)
    cq, ck, cv, dq, dkc, dvc, dks, dvs, dkw, dvw, dg = _split_cols(w_in, ODD_SIZES)
    dq = dq[:, perm] * SCALE
    w_rope = jnp.concatenate([cq * SCALE, ck, dq, dks, dkw], axis=1)
    w_rope = w_rope[:, _rope_layout(w_rope.shape[1])].astype(BF16)
    w_plain = jnp.concatenate([dq, dkc, dvc], axis=1).astype(BF16)
    w_val = jnp.concatenate([cv, dvs, dvw], axis=1).astype(BF16)
    w_dg = _pad_cols(dg, LANES).astype(BF16)
    rp = _proj_rope(xb, w_rope, ROW_TILE, w_rope.shape[1] // 2, seq_len, tabs).reshape(bsz, seq_len, -1)
    vt, dgv, pp = _proj_values(xb, w_val, w_dg, w_plain, ROW_TILE, TQ)
    vt = vt.reshape(bsz, seq_len // TQ, -1, TQ)
    dgv = dgv.reshape(bsz, seq_len, LANES)
    pp = pp.reshape(bsz, seq_len, -1)

    o_c = _moba_attention(rp, vt)

    nc = seq_len // NSA_CMP_STRIDE
    tok = pp[:, :, 4 * LANES:6 * LANES].reshape(bsz, nc, NSA_CMP_STRIDE, 4, HEAD_DIM)
    r = tok.transpose(0, 3, 1, 2, 4).reshape(bsz, 4, nc, NSA_CMP_STRIDE * HEAD_DIM)
    pe_flat = jnp.pad(pe.reshape(2, 1, -1), ((0, 0), (0, 7), (0, 0))).astype(BF16)
    cmp = _nsa_compress(r, pe_flat, phi_w1.astype(BF16), phi_w2.astype(BF16))
    kcmp = jnp.concatenate([cmp[:, 0], cmp[:, 1]], axis=-1)
    vcmp = jnp.concatenate([cmp[:, 2], cmp[:, 3]], axis=-1)
    gb = _pad_cols(gate_b.reshape(1, -1), LANES)
    o_d = _nsa_attention(rp, pp, vt, dgv, gb, kcmp, vcmp)

    half = w_out.shape[0] // 2
    wo_c = w_out[:half].astype(BF16)
    wo_d = w_out[half:][perm].astype(BF16)
    g_mix, b_mix, g_ffn, b_ffn = ln
    router = (_pad_cols(w_router, LANES), _pad_cols(b_router.reshape(1, -1), LANES))
    x1, x1b, rt = _outproj_ln(x, o_c.reshape(n, -1), o_d.reshape(n, -1), wo_c, wo_d, g_mix, b_mix, router)

    slot_tok, slot, block_e, n_used = _moe_layout(rt, n)
    y_slots = _moe_ffn(x1b[slot_tok], block_e, n_used, wg, wu, wd)
    return _combine_ln(x1, y_slots[slot[:, 0]], y_slots[slot[:, 1]], rt, g_ffn, b_ffn)


@jax.jit
def kernel(x, ev_w_in, ev_w_out, dif_lambda, dif_subln, ffd_w_gate, ffd_w_up, ffd_w_down, od_w_in, od_w_out,
           nsa_gate_b, nsa_pe, nsa_phi_w1, nsa_phi_w2, moe_w_router, moe_b_router, moe_w_gate, moe_w_up,
           moe_w_down, ln_mix_g, ln_mix_b, ln_ffn_g, ln_ffn_b):
    bsz, seq_len, d = x.shape
    tabs = _rope_tables(seq_len)
    xf = x.reshape(bsz * seq_len, d)
    xb = xf.astype(BF16)
    for l in range(DEPTH):
        i = l // 2
        ln = (ln_mix_g[l], ln_mix_b[l], ln_ffn_g[l], ln_ffn_b[l])
        if l % 2 == 0:
            lam_init = 0.8 - 0.6 * math.exp(-0.3 * l)
            xf, xb = _even_layer(xf, xb, ev_w_in[i], ev_w_out[i], dif_lambda[i], dif_subln[i], lam_init,
                                 ffd_w_gate[i], ffd_w_up[i], ffd_w_down[i], ln, tabs, bsz, seq_len)
        else:
            xf, xb = _odd_layer(xf, xb, od_w_in[i], od_w_out[i], nsa_gate_b[i], nsa_pe[i], nsa_phi_w1[i],
                                nsa_phi_w2[i], moe_w_router[i], moe_b_router[i], _layer_to_bf16(moe_w_gate, i),
                                _layer_to_bf16(moe_w_up, i), _layer_to_bf16(moe_w_down, i), ln, tabs, bsz, seq_len)
    return xf.reshape(bsz, seq_len, d)
```

```python
import functools
import math

import numpy as np
import jax
import jax.numpy as jnp
from jax import lax
from jax.experimental import pallas as pl
from jax.experimental.pallas import tpu as pltpu

F32 = jnp.float32
BF16 = jnp.bfloat16
I32 = jnp.int32

LANES = 128
VMEM_LIMIT = 56 * 1024 * 1024

DEPTH = 4
HEAD_DIM = 64
ROPE_THETA = 10000.0
LN_EPS = 1e-5
DN_ALPHA = (2 * DEPTH) ** 0.25
SCALE = HEAD_DIM ** -0.5 * math.log2(math.e)
NEG = -1e30
BIG = 1e30
M_INIT = -1e30
NINF = float("-inf")

A_HEADS = 4
B_HEADS = 8
IDX_HEADS = 4
DSA_TOPK = 256
C_HEADS = 8
MOBA_BLOCK = 256
MOBA_TOPK = 3
D_HEADS = 8
NSA_CMP_LEN = 32
NSA_CMP_STRIDE = 16
NSA_SLC_BLOCK = 64
NSA_SLC_TOPK = 16
NSA_WINDOW = 512
N_EXPERTS = 8
TOP_K = 2
MOE_TM = 512
N_BISECT = 14
ROW_TILE = 512
KV_GROUPS = 2

EVEN_SIZES = (A_HEADS * 2 * HEAD_DIM, A_HEADS * 2 * HEAD_DIM, A_HEADS * 2 * HEAD_DIM, B_HEADS * HEAD_DIM,
              KV_GROUPS * HEAD_DIM, KV_GROUPS * HEAD_DIM, IDX_HEADS * HEAD_DIM, HEAD_DIM, IDX_HEADS)
ODD_SIZES = (C_HEADS * HEAD_DIM,) * 3 + (D_HEADS * HEAD_DIM,) + (KV_GROUPS * HEAD_DIM,) * 6 + (D_HEADS * 3,)


def _cparams(sem):
    return pltpu.CompilerParams(dimension_semantics=sem, vmem_limit_bytes=VMEM_LIMIT)


def _dot_nt(a, b):
    return lax.dot_general(a, b, (((1,), (1,)), ((), ())), preferred_element_type=F32)


def _layer_norm(y, g, b):
    mu = jnp.mean(y, axis=-1, keepdims=True)
    yc = y - mu
    var = jnp.mean(yc * yc, axis=-1, keepdims=True)
    return yc * lax.rsqrt(var + LN_EPS) * g + b


def _safe_recip(l):
    return jnp.where(l > 0.0, 1.0 / jnp.where(l > 0.0, l, 1.0), 0.0)


def _split_halves(t, roped=False):
    lane = lax.broadcasted_iota(I32, (1, LANES), 1)
    lo = (lane // (HEAD_DIM // 2)) % 2 == 0 if roped else lane < HEAD_DIM
    z = jnp.zeros_like(t)
    return jnp.where(lo, t, z), jnp.where(lo, z, t)


def _rope_layout(n_cols):
    q = HEAD_DIM // 2
    tile = np.concatenate([np.arange(0, q), np.arange(2 * q, 3 * q), np.arange(q, 2 * q), np.arange(3 * q, 4 * q)])
    return (np.arange(0, n_cols, LANES)[:, None] + tile[None, :]).reshape(-1).astype(np.int32)


def _rope_mm_kernel(x_ref, w_ref, cos_ref, sin_ref, o_ref):
    acc = jnp.dot(x_ref[...].astype(BF16), w_ref[...], preferred_element_type=F32)
    cos = cos_ref[...]
    sin = sin_ref[...]
    for c in range(acc.shape[1] // LANES):
        a = acc[:, c * LANES:(c + 1) * LANES]
        rot = pltpu.roll(a, LANES // 2, 1)
        o_ref[:, c * LANES:(c + 1) * LANES] = (a * cos + rot * sin).astype(o_ref.dtype)


def _proj_rope(x, w, tm, tn, seq_len, rope_tabs):
    n, d = x.shape
    p = w.shape[1]
    nt = seq_len // tm
    return pl.pallas_call(
        _rope_mm_kernel,
        grid=(n // tm, p // tn),
        in_specs=[pl.BlockSpec((tm, d), lambda i, j: (i, 0)), pl.BlockSpec((d, tn), lambda i, j: (0, j)),
                  pl.BlockSpec((tm, LANES), lambda i, j: (i % nt, 0)),
                  pl.BlockSpec((tm, LANES), lambda i, j: (i % nt, 0))],
        out_specs=pl.BlockSpec((tm, tn), lambda i, j: (i, j)),
        out_shape=jax.ShapeDtypeStruct((n, p), BF16),
        compiler_params=_cparams(("parallel", "arbitrary")),
        name="proj_rope",
    )(x, w, *rope_tabs)


def _rope_tables(seq_len):
    d = HEAD_DIM
    inv = ROPE_THETA ** (-jnp.arange(0, d, 2, dtype=F32) / d)
    ang = jnp.arange(seq_len, dtype=I32).astype(F32)[:, None] * inv[None, :]
    cos = jnp.cos(ang)
    sin = jnp.sin(ang)
    cos128 = jnp.tile(cos, (1, LANES // (d // 2)))
    sin128 = jnp.concatenate([-sin, -sin, sin, sin], axis=1)
    return cos128, sin128


def _route_top2(x, w, b):
    xh, wh = x.astype(BF16), w.astype(BF16)
    xl, wl = (x - xh.astype(F32)).astype(BF16), (w - wh.astype(F32)).astype(BF16)
    logits = (jnp.dot(xh, wh, preferred_element_type=F32) + jnp.dot(xh, wl, preferred_element_type=F32)
              + jnp.dot(xl, wh, preferred_element_type=F32)) + b
    lane = lax.broadcasted_iota(I32, (1, LANES), 1)
    lanef = lane.astype(F32)
    v = jnp.where(lane < N_EXPERTS, logits, NINF)
    l0 = jnp.max(v, axis=1, keepdims=True)
    i0 = jnp.min(jnp.where(v == l0, lanef, float(LANES)), axis=1, keepdims=True)
    v = jnp.where(lanef == i0, NINF, v)
    l1 = jnp.max(v, axis=1, keepdims=True)
    i1 = jnp.min(jnp.where(v == l1, lanef, float(LANES)), axis=1, keepdims=True)
    e1 = jnp.exp(l1 - l0)
    g0 = 1.0 / (1.0 + e1)
    g1 = e1 / (1.0 + e1)
    return jnp.where(lane == 0, i0, jnp.where(lane == 1, i1, jnp.where(lane == 2, g0, jnp.where(lane == 3, g1, 0.0))))


def _outproj_ln_kernel(x_ref, a_ref, b_ref, wa_ref, wb_ref, g_ref, bb_ref, *rest):
    mix = (jnp.dot(a_ref[...], wa_ref[...], preferred_element_type=F32)
           + jnp.dot(b_ref[...], wb_ref[...], preferred_element_type=F32))
    y = _layer_norm(DN_ALPHA * x_ref[...] + mix, g_ref[...], bb_ref[...])
    if len(rest) == 2:
        xo_ref, xb_ref = rest
    else:
        wr_ref, br_ref, xo_ref, xb_ref, rt_ref = rest
        rt_ref[...] = _route_top2(y, wr_ref[...], br_ref[...])
    xo_ref[...] = y
    xb_ref[...] = y.astype(BF16)


def _outproj_ln(x, oa, ob, wa, wb, g, b, router=None, tm=ROW_TILE):
    n, d = x.shape
    ka, kb = oa.shape[1], ob.shape[1]
    row = lambda i: (i, 0)
    fixed = lambda i: (0, 0)
    in_specs = [pl.BlockSpec((tm, d), row), pl.BlockSpec((tm, ka), row), pl.BlockSpec((tm, kb), row),
                pl.BlockSpec((ka, d), fixed), pl.BlockSpec((kb, d), fixed),
                pl.BlockSpec((1, d), fixed), pl.BlockSpec((1, d), fixed)]
    out_specs = [pl.BlockSpec((tm, d), row), pl.BlockSpec((tm, d), row)]
    out_shape = [jax.ShapeDtypeStruct((n, d), F32), jax.ShapeDtypeStruct((n, d), BF16)]
    args = [x, oa, ob, wa, wb, g.reshape(1, d), b.reshape(1, d)]
    if router is not None:
        in_specs += [pl.BlockSpec((d, LANES), fixed), pl.BlockSpec((1, LANES), fixed)]
        out_specs.append(pl.BlockSpec((tm, LANES), row))
        out_shape.append(jax.ShapeDtypeStruct((n, LANES), F32))
        args += list(router)
    return pl.pallas_call(
        _outproj_ln_kernel,
        grid=(n // tm,),
        in_specs=in_specs,
        out_specs=out_specs,
        out_shape=out_shape,
        compiler_params=_cparams(("parallel",)),
        name="outproj_ln",
    )(*args)


def _ffn_ln_kernel(x_ref, xb_ref, wg_ref, wu_ref, wd_ref, g_ref, b_ref, xo_ref, xob_ref):
    xb = xb_ref[...]
    h = jax.nn.silu(jnp.dot(xb, wg_ref[...], preferred_element_type=F32)) * jnp.dot(
        xb, wu_ref[...], preferred_element_type=F32)
    ffn = jnp.dot(h.astype(BF16), wd_ref[...], preferred_element_type=F32)
    y = _layer_norm(DN_ALPHA * x_ref[...] + ffn, g_ref[...], b_ref[...])
    xo_ref[...] = y
    xob_ref[...] = y.astype(BF16)


def _ffn_ln(x, xb, wg, wu, wd, g, b, tm=ROW_TILE):
    n, d = x.shape
    fdim = wg.shape[1]
    row = lambda i: (i, 0)
    fixed = lambda i: (0, 0)
    once = pl.Buffered(1)
    return pl.pallas_call(
        _ffn_ln_kernel,
        grid=(n // tm,),
        in_specs=[pl.BlockSpec((tm, d), row), pl.BlockSpec((tm, d), row),
                  pl.BlockSpec((d, fdim), fixed, pipeline_mode=once),
                  pl.BlockSpec((d, fdim), fixed, pipeline_mode=once),
                  pl.BlockSpec((fdim, d), fixed, pipeline_mode=once),
                  pl.BlockSpec((1, d), fixed), pl.BlockSpec((1, d), fixed)],
        out_specs=[pl.BlockSpec((tm, d), row), pl.BlockSpec((tm, d), row)],
        out_shape=[jax.ShapeDtypeStruct((n, d), F32), jax.ShapeDtypeStruct((n, d), BF16)],
        compiler_params=_cparams(("parallel",)),
        name="ffn_ln",
    )(x, xb, wg, wu, wd, g.reshape(1, d), b.reshape(1, d))


def _top_n_mask(v, n, axis):
    idx = lax.broadcasted_iota(I32, v.shape, axis).astype(F32)
    sel = jnp.zeros(v.shape, F32)
    for _ in range(n):
        mx = jnp.max(v, axis=axis, keepdims=True)
        first = jnp.min(jnp.where(v == mx, idx, float(v.shape[axis])), axis=axis, keepdims=True)
        pick = idx == first
        sel = jnp.where(pick, 1.0, sel)
        v = jnp.where(pick, NINF, v)
    return sel


DV_PAD = 16
TQ = 256


def _with_ones(vt):
    return jnp.concatenate([vt, jnp.ones((DV_PAD, vt.shape[1]), vt.dtype)], axis=0)


def _mm_t_kernel(x_ref, wv_ref, ws_ref, *rest, tk):
    x = x_ref[...].astype(BF16)
    vt_ref, small_ref = rest[-2:] if len(rest) == 2 else rest[1:3]
    acc = jnp.dot(x, wv_ref[...], preferred_element_type=F32)
    for cc in range(acc.shape[0] // tk):
        vt_ref[cc] = acc[cc * tk:(cc + 1) * tk, :].T.astype(vt_ref.dtype)
    small_ref[...] = jnp.dot(x, ws_ref[...], preferred_element_type=F32)
    if len(rest) == 4:
        rest[3][...] = jnp.dot(x, rest[0][...], preferred_element_type=F32).astype(rest[3].dtype)


def _proj_values(x, w_val, w_small, w_plain, tm, tk):
    n, d = x.shape
    pv, ps = w_val.shape[1], w_small.shape[1]
    row = lambda i: (i, 0)
    fixed = lambda i: (0, 0)
    in_specs = [pl.BlockSpec((tm, d), row), pl.BlockSpec((d, pv), fixed), pl.BlockSpec((d, ps), fixed)]
    out_specs = [pl.BlockSpec((tm // tk, pv, tk), lambda i: (i, 0, 0)), pl.BlockSpec((tm, ps), row)]
    out_shape = [jax.ShapeDtypeStruct((n // tk, pv, tk), BF16), jax.ShapeDtypeStruct((n, ps), F32)]
    args = [x, w_val, w_small]
    if w_plain is not None:
        pp = w_plain.shape[1]
        in_specs.append(pl.BlockSpec((d, pp), fixed))
        out_specs.append(pl.BlockSpec((tm, pp), row))
        out_shape.append(jax.ShapeDtypeStruct((n, pp), BF16))
        args.append(w_plain)
    return pl.pallas_call(
        functools.partial(_mm_t_kernel, tk=tk),
        grid=(n // tm,),
        in_specs=in_specs,
        out_specs=out_specs,
        out_shape=out_shape,
        compiler_params=_cparams(("parallel",)),
        name="proj_t",
    )(*args)


def _normalize_t(acc, width):
    return acc[:width] * _safe_recip(acc[width:width + 1])


def _pipe_flash(n, ns, qk, vt_at, bias_at, bufs, dv, tk, tq):
    sa, sb, pa, pb = bufs

    def softmax_into(p_ref, j, st, m):
        m_new = jnp.maximum(m, jnp.max(st, axis=0, keepdims=True))
        p_ref[j] = jnp.exp2((st - m_new).astype(BF16))
        return m_new, jnp.exp2(m - m_new)

    def half(c, carry, s_cur, s_nxt, p_prev, p_cur):
        if s_nxt is not None:
            nxt = jnp.minimum(c + 1, n - 1)
            for j in range(ns):
                s_nxt[j] = qk(nxt, j)
        out = []
        for j in range(ns):
            m, acc, alpha = carry[j]
            acc = alpha * acc + jnp.dot(vt_at(c - 1, j), p_prev[j], preferred_element_type=F32)
            m, alpha = softmax_into(p_cur, j, s_cur[j] + bias_at(c, j), m)
            out.append((m, acc, alpha))
        return tuple(out)

    for j in range(ns):
        sa[j] = qk(0, j)
    first = []
    for j in range(ns):
        sb[j] = qk(jnp.minimum(1, n - 1), j)
        m, alpha = softmax_into(pa, j, sa[j] + bias_at(0, j), jnp.full((1, tq), M_INIT, F32))
        first.append((m, jnp.zeros((dv, tq), F32), alpha))

    def body(t, carry):
        carry = half(2 * t + 1, carry, sb, sa, pa, pb)
        return half(2 * t + 2, carry, sa, sb, pb, pa)

    carry = lax.fori_loop(0, (n - 1) // 2, body, tuple(first))

    def flush(carry, p_last):
        return tuple(alpha * acc + jnp.dot(vt_at(n - 1, j), p_last[j], preferred_element_type=F32)
                     for j, (_, acc, alpha) in enumerate(carry))

    def odd_tail(carry):
        return flush(half(n - 1, carry, sb, None, pa, pb), pb)

    return lax.cond(n % 2 == 0, odd_tail, lambda carry: flush(carry, pa), carry)


def _pipe_scratch(ns, tk, tq):
    return [pltpu.VMEM((ns, tk, tq), F32)] * 2 + [pltpu.VMEM((ns, tk, tq), BF16)] * 2


def _causal_t(t):
    return jnp.where(lax.broadcasted_iota(I32, (t, t), 0) <= lax.broadcasted_iota(I32, (t, t), 1), 0.0, NINF)


def _diff_kernel(lam_ref, sub_ref, q_ref, k_ref, vt_ref, o_ref, *bufs, tq, lam_init):
    i = pl.program_id(2)
    lp = lam_ref[...]
    lam = (jnp.exp(jnp.sum(lp[0:1] * lp[1:2], axis=1, keepdims=True))
           - jnp.exp(jnp.sum(lp[2:3] * lp[3:4], axis=1, keepdims=True)) + lam_init)
    qs = _split_halves(q_ref[...], roped=True)
    n = i + 1
    bufs, tab_ref = bufs[:4], bufs[4]
    tab_ref[0] = jnp.zeros((tq, tq), F32)
    tab_ref[1] = _causal_t(tq)

    def qk(c, j):
        off = pl.multiple_of(c * tq, tq)
        return _dot_nt(k_ref[pl.ds(off, tq), :], qs[j])

    def bias_at(c, j):
        return tab_ref[jnp.where(c == i, 1, 0)]

    outs = _pipe_flash(n, 2, qk, lambda c, j: _with_ones(vt_ref[c]), bias_at, bufs, LANES + DV_PAD, tq, tq)
    o = _normalize_t(outs[0], LANES) - lam * _normalize_t(outs[1], LANES)
    o = o * lax.rsqrt(jnp.mean(o * o, axis=0, keepdims=True) + LN_EPS)
    o = o * sub_ref[...] * (1.0 - lam_init)
    o_ref[...] = o.T.astype(o_ref.dtype)


def _diff_attention(rp, vt, lam_params, subln, lam_init, tq=TQ):
    b, s, _ = rp.shape
    nk = s // tq
    return pl.pallas_call(
        functools.partial(_diff_kernel, tq=tq, lam_init=lam_init),
        grid=(b, A_HEADS, nk),
        in_specs=[pl.BlockSpec((4, HEAD_DIM), lambda bi, h, i: (0, 0)),
                  pl.BlockSpec((LANES, 1), lambda bi, h, i: (0, 0)),
                  pl.BlockSpec((None, tq, LANES), lambda bi, h, i: (bi, i, h)),
                  pl.BlockSpec((None, s, LANES), lambda bi, h, i: (bi, 0, A_HEADS + h)),
                  pl.BlockSpec((None, nk, LANES, tq), lambda bi, h, i: (bi, 0, h, 0))],
        out_specs=pl.BlockSpec((None, tq, LANES), lambda bi, h, i: (bi, i, h)),
        out_shape=jax.ShapeDtypeStruct((b, s, A_HEADS * LANES), BF16),
        scratch_shapes=_pipe_scratch(2, tq, tq) + [pltpu.VMEM((2, tq, tq), F32)],
        compiler_params=_cparams(("parallel", "parallel", "arbitrary")),
        name="diff_attn",
    )(lam_params, subln.reshape(LANES, 1), rp, rp, vt)


def _fold8(x, op):
    acc = x[0:8]
    for r in range(1, x.shape[0] // 8):
        acc = op(acc, x[r * 8:(r + 1) * 8])
    return acc


def _dsa_kernel(iq_ref, ikk_ref, iw_ref, q_ref, k_ref, vt_ref, o_ref, s_ref, j_ref, *bufs, tq, ksel, seq_len):
    tk = tq
    i = pl.program_id(1)
    nch = i + 1
    ksel_f = float(ksel)
    k_loc = lax.broadcasted_iota(I32, (tk, tq), 0)
    q_loc = lax.broadcasted_iota(I32, (tk, tq), 1)
    qpos = i * tq + lax.broadcasted_iota(I32, (1, tq), 1)

    iq = iq_ref[...]
    iwt = iw_ref[...].T
    iqh = []
    for pair in range(IDX_HEADS // 2):
        iqh += list(_split_halves(iq[:, pair * LANES:(pair + 1) * LANES], roped=True))

    def scores(c):
        off = pl.multiple_of(c * tk, tk)
        kk = ikk_ref[pl.ds(off, tk), :]
        sc = iwt[0:1] * jnp.maximum(_dot_nt(kk, iqh[0]), 0.0)
        for h in range(1, IDX_HEADS):
            sc = sc + iwt[h:h + 1] * jnp.maximum(_dot_nt(kk, iqh[h]), 0.0)
        return sc

    def full_body(c, carry):
        mx, mn = carry
        sc = scores(c)
        s_ref[c] = sc
        return jnp.maximum(mx, _fold8(sc, jnp.maximum)), jnp.minimum(mn, _fold8(sc, jnp.minimum))

    mx, mn = lax.fori_loop(0, i, full_body, (jnp.full((8, tq), -BIG, F32), jnp.full((8, tq), BIG, F32)))
    sc = scores(i)
    causal = k_loc <= q_loc
    s_ref[i] = jnp.where(causal, sc, NEG)
    mx = jnp.maximum(mx, _fold8(jnp.where(causal, sc, -BIG), jnp.maximum))
    mn = jnp.minimum(mn, _fold8(jnp.where(causal, sc, BIG), jnp.minimum))
    smax = jnp.max(mx, axis=0, keepdims=True)
    smin = jnp.min(mn, axis=0, keepdims=True)

    def count_where(ind):
        def body(c, acc):
            return acc + _fold8(ind(s_ref[c], c * tk + k_loc), jnp.add)
        acc = lax.fori_loop(0, nch, body, jnp.zeros((8, tq), F32))
        return jnp.sum(acc, axis=0, keepdims=True)

    def count_ge(th):
        return count_where(lambda x, kidx: jnp.where(x >= th, 1.0, 0.0))

    def max_below(th):
        def body(c, acc):
            x = s_ref[c]
            return jnp.maximum(acc, _fold8(jnp.where(x < th, x, NINF), jnp.maximum))
        acc = lax.fori_loop(0, nch, body, jnp.full((8, tq), NINF, F32))
        return jnp.max(acc, axis=0, keepdims=True)

    n_causal = (qpos + 1).astype(F32)
    take_all = n_causal <= ksel_f
    done0 = jnp.where(take_all, 1.0, 0.0)
    hi0 = smax + (jnp.abs(smax) * 2.0 ** -20 + 1e-30)

    def bisect(lo, hi):
        mid = lo + (hi - lo) * 0.5
        ge = count_ge(mid) >= ksel_f
        return jnp.where(ge, mid, lo), jnp.where(ge, hi, mid)

    lo, hi = lax.fori_loop(0, N_BISECT, lambda _, c: bisect(*c), (smin, hi0))

    def snap_body(carry):
        lo, hi, th, c_th, done, _ = carry
        lo, hi = bisect(lo, hi)
        t1 = max_below(hi)
        c1 = count_ge(t1)
        ok = c1 >= ksel_f
        th = jnp.where(done > 0.0, th, t1)
        c_th = jnp.where(done > 0.0, c_th, c1)
        hi = jnp.where(ok, hi, t1)
        done = jnp.where(ok, 1.0, done)
        return lo, hi, th, c_th, done, jnp.sum(1.0 - done)

    _, _, th, c_ge, _, _ = lax.while_loop(lambda c: c[5] > 0.0, snap_body,
                                          (lo, hi, smax, jnp.zeros((1, tq), F32), done0, jnp.sum(1.0 - done0)))

    need_tb = jnp.where(take_all, 0.0, jnp.where(c_ge > ksel_f, 1.0, 0.0))
    j_ref[...] = jnp.full((8, tq), seq_len - 1, I32)

    @pl.when(jnp.sum(need_tb) > 0.0)
    def _():
        need = ksel_f - count_where(lambda x, kidx: jnp.where(x > th, 1.0, 0.0))

        def jb(_, carry):
            lo_j, hi_j = carry
            mid = (lo_j + hi_j) // 2
            cnt = count_where(lambda x, kidx: jnp.where(x == th, jnp.where(kidx <= mid, 1.0, 0.0), 0.0))
            ge = cnt >= need
            return jnp.where(ge, lo_j, mid), jnp.where(ge, mid, hi_j)

        n_it = int(math.ceil(math.log2(seq_len))) + 1
        _, hi_j = lax.fori_loop(0, n_it, jb, (jnp.full((1, tq), -1, I32), jnp.full((1, tq), seq_len - 1, I32)))
        j_ref[...] = jnp.broadcast_to(hi_j, (8, tq))

    jsel = j_ref[0:1, :]

    def bias_body(c, _):
        x = s_ref[c]
        kidx = c * tk + k_loc
        keep = jnp.where(x > th, 0.0, jnp.where(x == th, jnp.where(kidx <= jsel, 0.0, NINF), NINF))
        keep = jnp.where(take_all, 0.0, keep)
        s_ref[c] = jnp.where(kidx <= qpos, keep, NINF)
        return 0

    lax.fori_loop(0, nch, bias_body, 0)

    for p in range(B_HEADS // 2):
        qs = _split_halves(q_ref[:, p * LANES:(p + 1) * LANES], roped=True)

        def qk(c, j, qs=qs):
            off = pl.multiple_of(c * tk, tk)
            return _dot_nt(k_ref[pl.ds(off, tk), :], qs[j])

        outs = _pipe_flash(nch, 2, qk, lambda c, j: _with_ones(vt_ref[c, j * HEAD_DIM:(j + 1) * HEAD_DIM, :]),
                           lambda c, j: s_ref[c], bufs, HEAD_DIM + DV_PAD, tk, tq)
        o = jnp.concatenate([_normalize_t(outs[0], HEAD_DIM), _normalize_t(outs[1], HEAD_DIM)], axis=0)
        o_ref[:, p * LANES:(p + 1) * LANES] = o.T.astype(o_ref.dtype)


def _dsa_attention(rp, vt, iw, tq=TQ):
    b, s, _ = rp.shape
    ksel = min(DSA_TOPK, s // 4)
    nk = s // tq
    return pl.pallas_call(
        functools.partial(_dsa_kernel, tq=tq, ksel=ksel, seq_len=s),
        grid=(b, nk),
        in_specs=[pl.BlockSpec((None, tq, 2 * LANES), lambda bi, i: (bi, i, 6)),
                  pl.BlockSpec((None, s, LANES), lambda bi, i: (bi, 0, 15)),
                  pl.BlockSpec((None, tq, LANES), lambda bi, i: (bi, i, 0)),
                  pl.BlockSpec((None, tq, 4 * LANES), lambda bi, i: (bi, i, 2)),
                  pl.BlockSpec((None, s, LANES), lambda bi, i: (bi, 0, 14)),
                  pl.BlockSpec((None, nk, LANES, tq), lambda bi, i: (bi, 0, 4, 0))],
        out_specs=pl.BlockSpec((None, tq, 4 * LANES), lambda bi, i: (bi, i, 0)),
        out_shape=jax.ShapeDtypeStruct((b, s, 4 * LANES), BF16),
        scratch_shapes=[pltpu.VMEM((nk, tq, tq), F32), pltpu.VMEM((8, tq), I32)] + _pipe_scratch(2, tq, tq),
        compiler_params=_cparams(("parallel", "arbitrary")),
        name="dsa_attn",
    )(rp, rp, iw, rp, rp, vt)


def _moba_kernel(q_ref, k_ref, vt_ref, o_ref, km_ref, sel_ref, *bufs, seq_len, n_sel):
    tq = MOBA_BLOCK
    qb = pl.program_id(2)

    @pl.when(qb == 0)
    def _():
        j = lax.broadcasted_iota(I32, (LANES, seq_len), 0)
        s = lax.broadcasted_iota(I32, (LANES, seq_len), 1)
        avg = jnp.where(s // MOBA_BLOCK == j, 1.0 / MOBA_BLOCK, 0.0).astype(BF16)
        km_ref[...] = jnp.dot(avg, k_ref[...], preferred_element_type=F32)

    nbp = sel_ref.shape[1]
    km = km_ref[0:nbp, :]
    qs = _split_halves(q_ref[...], roped=True)
    blk = lax.broadcasted_iota(I32, (nbp, tq), 0)
    past = blk < qb
    for j in range(2):
        gate = lax.dot_general(km, qs[j].astype(F32), (((1,), (1,)), ((), ())),
                               precision=lax.Precision.HIGHEST, preferred_element_type=F32)
        gate = jnp.where(blk < seq_len // MOBA_BLOCK, jnp.where(past, gate, NEG), NINF)
        sel = _top_n_mask(gate, n_sel, 0)
        sel_ref[j] = jnp.where(past, jnp.where(sel > 0.5, 0.0, NINF), NINF)
    own = _causal_t(tq)
    n = qb + 1

    def qk(c, j):
        off = pl.multiple_of(c * tq, tq)
        return _dot_nt(k_ref[pl.ds(off, tq), :], qs[j])

    def bias_at(c, j):
        chosen = sel_ref[j, pl.ds(c, 1), :]
        return jnp.where(c == qb, own, chosen)

    outs = _pipe_flash(n, 2, qk, lambda c, j: _with_ones(vt_ref[c, j * HEAD_DIM:(j + 1) * HEAD_DIM, :]), bias_at, bufs,
                       HEAD_DIM + DV_PAD, tq, tq)
    o = jnp.concatenate([_normalize_t(outs[0], HEAD_DIM), _normalize_t(outs[1], HEAD_DIM)], axis=0)
    o_ref[...] = o.T.astype(o_ref.dtype)


def _moba_attention(rp, vt):
    b, s, _ = rp.shape
    tq = MOBA_BLOCK
    nb = s // tq
    n_sel = max(1, min(MOBA_TOPK, nb - 1))
    npair = C_HEADS // 2
    return pl.pallas_call(
        functools.partial(_moba_kernel, seq_len=s, n_sel=n_sel),
        grid=(b, npair, nb),
        in_specs=[pl.BlockSpec((None, tq, LANES), lambda bi, h, i: (bi, i, h)),
                  pl.BlockSpec((None, s, LANES), lambda bi, h, i: (bi, 0, npair + h)),
                  pl.BlockSpec((None, nb, LANES, tq), lambda bi, h, i: (bi, 0, h, 0))],
        out_specs=pl.BlockSpec((None, tq, LANES), lambda bi, h, i: (bi, i, h)),
        out_shape=jax.ShapeDtypeStruct((b, s, npair * LANES), BF16),
        scratch_shapes=[pltpu.VMEM((LANES, LANES), F32), pltpu.VMEM((2, -(-nb // 8) * 8, tq), F32)]
        + _pipe_scratch(2, tq, tq),
        compiler_params=_cparams(("parallel", "parallel", "arbitrary")),
        name="moba_attn",
    )(rp, rp, vt)


def _cmp_kernel(r_ref, pe_ref, w1_ref, w2_ref, o_ref):
    r = r_ref[...]
    w1 = w1_ref[...]
    half = r.shape[1]
    u = jnp.dot(r, w1[:half], preferred_element_type=F32)
    v = jnp.dot(r, w1[half:], preferred_element_type=F32)
    c = jnp.dot(pe_ref[...], w1, preferred_element_type=F32)[0:1]
    pre = u + pltpu.roll(v, r.shape[0] - 1, 0) + c
    o_ref[...] = jnp.dot(jax.nn.gelu(pre).astype(BF16), w2_ref[...],
                         preferred_element_type=F32).astype(o_ref.dtype)


def _nsa_compress(r, pe, w1, w2):
    b, _, nc, wdt = r.shape
    hid = w1.shape[2]
    return pl.pallas_call(
        _cmp_kernel,
        grid=(b, 4),
        in_specs=[pl.BlockSpec((None, None, nc, wdt), lambda bi, t: (bi, t, 0, 0)),
                  pl.BlockSpec((None, 8, 2 * wdt), lambda bi, t: (t // 2, 0, 0)),
                  pl.BlockSpec((None, 2 * wdt, hid), lambda bi, t: (t // 2, 0, 0)),
                  pl.BlockSpec((None, hid, HEAD_DIM), lambda bi, t: (t // 2, 0, 0))],
        out_specs=pl.BlockSpec((None, None, nc, HEAD_DIM), lambda bi, t: (bi, t, 0, 0)),
        out_shape=jax.ShapeDtypeStruct((b, 4, nc, HEAD_DIM), BF16),
        compiler_params=_cparams(("parallel", "arbitrary")),
        name="nsa_compress",
    )(r, pe, w1, w2)


def _nsa_kernel(qr_ref, qw_ref, dg_ref, gb_ref, kc_ref, vct_ref, ks_ref, vst_ref, kw_ref, vwt_ref,
                o_ref, sel_ref, wb_ref, *bufs, tq, seq_len):
    tk = tq
    i = pl.program_id(1)
    nch = i + 1
    nc = seq_len // NSA_CMP_STRIDE
    n_sb = seq_len // NSA_SLC_BLOCK
    n_sel = min(NSA_SLC_TOPK, n_sb)
    k_loc = lax.broadcasted_iota(I32, (tk, tq), 0)
    q_loc = lax.broadcasted_iota(I32, (tk, tq), 1)
    qpos = i * tq + lax.broadcasted_iota(I32, (1, tq), 1)

    gates_t = jax.nn.sigmoid(dg_ref[...] + gb_ref[...]).T
    kc = kc_ref[...]
    cmp_end = lax.broadcasted_iota(I32, (nc, 1), 0) * NSA_CMP_STRIDE + (NSA_CMP_LEN - 1)
    cbias = jnp.where(cmp_end <= qpos, 0.0, NINF)

    nh = D_HEADS // 2
    q_rot = [_split_halves(qr_ref[:, p * LANES:(p + 1) * LANES], roped=True) for p in range(nh)]
    q_raw = [_split_halves(qw_ref[:, p * LANES:(p + 1) * LANES]) for p in range(nh)]
    cbias4 = jnp.concatenate([cbias] * nh, axis=1)
    o_cmp, psum = [], []
    for g in range(2):
        qg = jnp.concatenate([q_raw[p][g] for p in range(nh)], axis=0)
        s = _dot_nt(kc, qg) + cbias4
        m = jnp.max(s, axis=0, keepdims=True)
        e = jnp.exp2(s - jnp.where(m == NINF, 0.0, m))
        pc = e * _safe_recip(jnp.sum(e, axis=0, keepdims=True))
        psum.append(sum(pc[:, p * tq:(p + 1) * tq] for p in range(nh)))
        o_cmp.append(jnp.dot(vct_ref[g], pc.astype(BF16), preferred_element_type=F32))

    per = tk // NSA_SLC_BLOCK
    nbp = -(-n_sb // 8) * 8
    cn = lax.broadcasted_iota(I32, (nbp, nc), 1) * NSA_CMP_STRIDE
    sj = lax.broadcasted_iota(I32, (nbp, nc), 0) * NSA_SLC_BLOCK
    shares = jnp.where((cn <= sj + NSA_SLC_BLOCK - 1) & (cn + NSA_CMP_LEN - 1 >= sj), 1.0, 0.0)
    blk = lax.broadcasted_iota(I32, (nbp, tq), 0)
    cur = qpos // NSA_SLC_BLOCK
    causal_b = blk <= cur
    forced = (blk == 0) | ((blk >= cur - 1) & causal_b)
    for g in range(2):
        imp = jnp.dot(shares, psum[g], precision=lax.Precision.HIGHEST, preferred_element_type=F32)
        val = jnp.where(forced, BIG, jnp.where(causal_b, imp, NEG))
        val = jnp.where(blk < n_sb, val, NINF)
        rowb = jnp.where(_top_n_mask(val, n_sel, 0) > 0.5, 0.0, NINF)
        for c in range(seq_len // tk):
            sel_ref[g, c] = jnp.concatenate([rowb[c * per:(c + 1) * per], jnp.zeros((8 - per, tq), F32)], axis=0)

    wb_ref[0] = jnp.where(k_loc <= q_loc, 0.0, NINF)
    wb_ref[1] = jnp.zeros((tk, tq), F32)
    wb_ref[2] = jnp.where(k_loc > q_loc, 0.0, NINF)
    n_wc = NSA_WINDOW // tk + 1
    w_first = jnp.maximum(i - (n_wc - 1), 0)
    n_w = i - w_first + 1

    for p in range(D_HEADS // 2):
        qs = q_rot[p]

        def qk_s(c, j, qs=qs):
            off = pl.multiple_of(c * tk, tk)
            return _dot_nt(ks_ref[pl.ds(off, tk), :], qs[j])

        def bias_s(c, j):
            rows = sel_ref[j, c]
            tile = jnp.concatenate([jnp.broadcast_to(rows[r:r + 1], (NSA_SLC_BLOCK, tq)) for r in range(per)], axis=0)
            return tile + wb_ref[jnp.where(c == i, 0, 1)]

        o_slc = _pipe_flash(nch, 2, qk_s, lambda c, j: _with_ones(vst_ref[c, j * HEAD_DIM:(j + 1) * HEAD_DIM, :]),
                            bias_s, bufs, HEAD_DIM + DV_PAD, tk, tq)

        def qk_w(c, j, qs=qs):
            off = pl.multiple_of((w_first + c) * tk, tk)
            return _dot_nt(kw_ref[pl.ds(off, tk), :], qs[j])

        def bias_w(c, j):
            return wb_ref[i - (w_first + c)]

        o_win = _pipe_flash(n_w, 2, qk_w,
                            lambda c, j: _with_ones(vwt_ref[w_first + c, j * HEAD_DIM:(j + 1) * HEAD_DIM, :]), bias_w, bufs,
                            HEAD_DIM + DV_PAD, tk, tq)
        outs = []
        for g in range(2):
            h = g * (D_HEADS // 2) + p
            outs.append(gates_t[3 * h:3 * h + 1] * o_cmp[g][:, p * tq:(p + 1) * tq]
                        + gates_t[3 * h + 1:3 * h + 2] * _normalize_t(o_slc[g], HEAD_DIM)
                        + gates_t[3 * h + 2:3 * h + 3] * _normalize_t(o_win[g], HEAD_DIM))
        o_ref[:, p * LANES:(p + 1) * LANES] = jnp.concatenate(outs, axis=0).T.astype(o_ref.dtype)


def _nsa_attention(rp, pp, vt, dg, gate_b, kcmp, vcmp, tq=TQ):
    b, s, _ = rp.shape
    nk = s // tq
    nc = s // NSA_CMP_STRIDE
    assert NSA_WINDOW == 2 * tq
    n_wc = NSA_WINDOW // tq + 1
    vct = vcmp.reshape(b, nc, 2, HEAD_DIM).transpose(0, 2, 3, 1)
    full = lambda t: pl.BlockSpec((None, s, LANES), lambda bi, i: (bi, 0, t))
    vspec = lambda t: pl.BlockSpec((None, nk, LANES, tq), lambda bi, i: (bi, 0, t, 0))
    return pl.pallas_call(
        functools.partial(_nsa_kernel, tq=tq, seq_len=s),
        grid=(b, nk),
        in_specs=[pl.BlockSpec((None, tq, 4 * LANES), lambda bi, i: (bi, i, 2)),
                  pl.BlockSpec((None, tq, 4 * LANES), lambda bi, i: (bi, i, 0)),
                  pl.BlockSpec((None, tq, LANES), lambda bi, i: (bi, i, 0)),
                  pl.BlockSpec((1, LANES), lambda bi, i: (0, 0)),
                  pl.BlockSpec((None, nc, LANES), lambda bi, i: (bi, 0, 0)),
                  pl.BlockSpec((None, 2, HEAD_DIM, nc), lambda bi, i: (bi, 0, 0, 0)),
                  full(12), vspec(4), full(13), vspec(5)],
        out_specs=pl.BlockSpec((None, tq, 4 * LANES), lambda bi, i: (bi, i, 0)),
        out_shape=jax.ShapeDtypeStruct((b, s, 4 * LANES), BF16),
        scratch_shapes=[pltpu.VMEM((2, nk, 8, tq), F32), pltpu.VMEM((n_wc, tq, tq), F32)]
        + _pipe_scratch(2, tq, tq),
        compiler_params=_cparams(("parallel", "arbitrary")),
        name="nsa_attn",
    )(rp, pp, dg, gate_b, kcmp, vct, rp, vt, rp, vt)


def _cast_kernel(x_ref, o_ref):
    o_ref[...] = x_ref[...].astype(o_ref.dtype)


def _layer_to_bf16(w, layer):
    _, e, r, c = w.shape
    tr = 1 << ((2 ** 21 // c).bit_length() - 1)
    out = pl.pallas_call(
        _cast_kernel,
        grid=(e * r // tr,),
        in_specs=[pl.BlockSpec((None, tr, c), lambda i: (layer, i, 0))],
        out_specs=pl.BlockSpec((tr, c), lambda i: (i, 0)),
        out_shape=jax.ShapeDtypeStruct((e * r, c), BF16),
        compiler_params=_cparams(("parallel",)),
        name="cast_bf16",
    )(w.reshape(w.shape[0], e * r, c))
    return out.reshape(e, r, c)


def _moe_ffn_kernel(be_ref, nu_ref, x_ref, wg_ref, wu_ref, wd_ref, o_ref):
    used = pl.program_id(0) < nu_ref[0]

    @pl.when(used)
    def _():
        x = x_ref[...]
        h = jax.nn.silu(jnp.dot(x, wg_ref[...], preferred_element_type=F32)) * jnp.dot(
            x, wu_ref[...], preferred_element_type=F32)
        o_ref[...] = jnp.dot(h.astype(BF16), wd_ref[...], preferred_element_type=F32).astype(o_ref.dtype)

    @pl.when(jnp.logical_not(used))
    def _():
        o_ref[...] = jnp.zeros(o_ref.shape, o_ref.dtype)


def _moe_ffn(x_sorted, block_e, n_used, wg, wu, wd):
    ns, d = x_sorted.shape
    fdim = wg.shape[2]
    once = pl.Buffered(1)
    grid_spec = pltpu.PrefetchScalarGridSpec(
        num_scalar_prefetch=2,
        grid=(ns // MOE_TM,),
        in_specs=[pl.BlockSpec((MOE_TM, d), lambda i, be, nu: (jnp.minimum(i, nu[0] - 1), 0)),
                  pl.BlockSpec((None, d, fdim), lambda i, be, nu: (be[i], 0, 0), pipeline_mode=once),
                  pl.BlockSpec((None, d, fdim), lambda i, be, nu: (be[i], 0, 0), pipeline_mode=once),
                  pl.BlockSpec((None, fdim, d), lambda i, be, nu: (be[i], 0, 0), pipeline_mode=once)],
        out_specs=pl.BlockSpec((MOE_TM, d), lambda i, be, nu: (i, 0)),
    )
    return pl.pallas_call(
        _moe_ffn_kernel,
        grid_spec=grid_spec,
        out_shape=jax.ShapeDtypeStruct((ns, d), BF16),
        compiler_params=_cparams(("arbitrary",)),
        name="moe_ffn",
    )(block_e, n_used, x_sorted, wg, wu, wd)


def _combine_ln_kernel(x_ref, y0_ref, y1_ref, rt_ref, g_ref, b_ref, xo_ref, xb_ref):
    rt = rt_ref[...]
    ffn = rt[:, 2:3] * y0_ref[...].astype(F32) + rt[:, 3:4] * y1_ref[...].astype(F32)
    y = _layer_norm(DN_ALPHA * x_ref[...] + ffn, g_ref[...], b_ref[...])
    xo_ref[...] = y
    xb_ref[...] = y.astype(BF16)


def _combine_ln(x, y0, y1, rt, g, b, tm=ROW_TILE):
    n, d = x.shape
    row = lambda i: (i, 0)
    fixed = lambda i: (0, 0)
    return pl.pallas_call(
        _combine_ln_kernel,
        grid=(n // tm,),
        in_specs=[pl.BlockSpec((tm, d), row), pl.BlockSpec((tm, d), row), pl.BlockSpec((tm, d), row),
                  pl.BlockSpec((tm, LANES), row), pl.BlockSpec((1, d), fixed), pl.BlockSpec((1, d), fixed)],
        out_specs=[pl.BlockSpec((tm, d), row), pl.BlockSpec((tm, d), row)],
        out_shape=[jax.ShapeDtypeStruct((n, d), F32), jax.ShapeDtypeStruct((n, d), BF16)],
        compiler_params=_cparams(("parallel",)),
        name="moe_combine_ln",
    )(x, y0, y1, rt, g.reshape(1, d), b.reshape(1, d))


def _moe_layout(rt, n):
    e_flat = rt[:, 0:TOP_K].astype(I32).reshape(-1)
    nk = n * TOP_K
    onehot = (e_flat[:, None] == jnp.arange(N_EXPERTS, dtype=I32)[None, :]).astype(I32)
    rank = jnp.take_along_axis(jnp.cumsum(onehot, axis=0), e_flat[:, None], axis=1)[:, 0] - 1
    counts = jnp.sum(onehot, axis=0)
    padded = (counts + MOE_TM - 1) // MOE_TM * MOE_TM
    pad_end = jnp.cumsum(padded)
    pad_start = pad_end - padded
    grp_start = jnp.cumsum(counts) - counts
    slot = pad_start[e_flat] + rank
    n_blocks = -(-nk // MOE_TM) + N_EXPERTS
    n_slots = n_blocks * MOE_TM
    order = jnp.argsort(e_flat, stable=True).astype(I32)
    sl = jnp.arange(n_slots, dtype=I32)
    slot_e = jnp.minimum(jnp.searchsorted(pad_end, sl, side='right'), N_EXPERTS - 1).astype(I32)
    within = sl - pad_start[slot_e]
    valid = within < counts[slot_e]
    src = jnp.where(valid, grp_start[slot_e] + within, 0)
    slot_tok = jnp.where(valid, order[src] // TOP_K, 0)
    n_used = (pad_end[-1] // MOE_TM).astype(I32).reshape(1)
    blk = jnp.arange(n_blocks, dtype=I32)
    block_e = slot_e[jnp.minimum(blk, n_used[0] - 1) * MOE_TM]
    return slot_tok, slot.reshape(n, TOP_K), block_e, n_used


def _pair_perm(n_heads):
    half = n_heads // 2
    cols = []
    for p in range(half):
        cols += list(range(p * HEAD_DIM, (p + 1) * HEAD_DIM))
        cols += list(range((half + p) * HEAD_DIM, (half + p + 1) * HEAD_DIM))
    return np.asarray(cols, dtype=np.int32)


def _pad_cols(w, width):
    return jnp.pad(w, ((0, 0), (0, width - w.shape[1])))


def _split_cols(w, sizes):
    out, off = [], 0
    for sz in sizes:
        out.append(w[:, off:off + sz])
        off += sz
    return out


def _even_layer(x, xb, w_in, w_out, lam_params, subln, lam_init, wg, wu, wd, ln, tabs, bsz, seq_len):
    n = x.shape[0]
    perm = _pair_perm(B_HEADS)
    aq, ak, av, bq, bk, bv, iq, ik, iw = _split_cols(w_in, EVEN_SIZES)
    w_rope = jnp.concatenate([aq * SCALE, ak, bq[:, perm] * SCALE, iq, bk, ik, ik], axis=1)
    w_rope = w_rope[:, _rope_layout(w_rope.shape[1])].astype(BF16)
    w_val = jnp.concatenate([av, bv], axis=1).astype(BF16)
    w_iw = _pad_cols(iw, LANES).astype(BF16)
    rp = _proj_rope(xb, w_rope, ROW_TILE, w_rope.shape[1], seq_len, tabs).reshape(bsz, seq_len, -1)
    vt, iwv = _proj_values(xb, w_val, w_iw, None, ROW_TILE, TQ)
    vt = vt.reshape(bsz, seq_len // TQ, -1, TQ)
    iwv = iwv.reshape(bsz, seq_len, LANES)
    o_a = _diff_attention(rp, vt, lam_params, subln, lam_init)
    o_b = _dsa_attention(rp, vt, iwv)
    half = w_out.shape[0] // 2
    wo_a = w_out[:half].astype(BF16)
    wo_b = w_out[half:][perm].astype(BF16)
    g_mix, b_mix, g_ffn, b_ffn = ln
    x1, x1b = _outproj_ln(x, o_a.reshape(n, -1), o_b.reshape(n, -1), wo_a, wo_b, g_mix, b_mix)
    return _ffn_ln(x1, x1b, wg.astype(BF16), wu.astype(BF16), wd.astype(BF16), g_ffn, b_ffn)


def _odd_layer(x, xb, w_in, w_out, gate_b, pe, phi_w1, phi_w2, w_router, b_router, wg, wu, wd, ln, tabs,
               bsz, seq_len):
    n = x.shape[0]
    perm = _pair_perm(D_HEADS)
    cq, ck, cv, dq, dkc, dvc, dks, dvs, dkw, dvw, dg = _split_cols(w_in, ODD_SIZES)
    dq = dq[:, perm] * SCALE
    w_rope = jnp.concatenate([cq * SCALE, ck, dq, dks, dkw], axis=1)
    w_rope = w_rope[:, _rope_layout(w_rope.shape[1])].astype(BF16)
    w_plain = jnp.concatenate([dq, dkc, dvc], axis=1).astype(BF16)
    w_val = jnp.concatenate([cv, dvs, dvw], axis=1).astype(BF16)
    w_dg = _pad_cols(dg, LANES).astype(BF16)
    rp = _proj_rope(xb, w_rope, ROW_TILE, w_rope.shape[1], seq_len, tabs).reshape(bsz, seq_len, -1)
    vt, dgv, pp = _proj_values(xb, w_val, w_dg, w_plain, ROW_TILE, TQ)
    vt = vt.reshape(bsz, seq_len // TQ, -1, TQ)
    dgv = dgv.reshape(bsz, seq_len, LANES)
    pp = pp.reshape(bsz, seq_len, -1)

    o_c = _moba_attention(rp, vt)

    nc = seq_len // NSA_CMP_STRIDE
    tok = pp[:, :, 4 * LANES:6 * LANES].reshape(bsz, nc, NSA_CMP_STRIDE, 4, HEAD_DIM)
    r = tok.transpose(0, 3, 1, 2, 4).reshape(bsz, 4, nc, NSA_CMP_STRIDE * HEAD_DIM)
    pe_flat = jnp.pad(pe.reshape(2, 1, -1), ((0, 0), (0, 7), (0, 0))).astype(BF16)
    cmp = _nsa_compress(r, pe_flat, phi_w1.astype(BF16), phi_w2.astype(BF16))
    kcmp = jnp.concatenate([cmp[:, 0], cmp[:, 1]], axis=-1)
    vcmp = jnp.concatenate([cmp[:, 2], cmp[:, 3]], axis=-1)
    gb = _pad_cols(gate_b.reshape(1, -1), LANES)
    o_d = _nsa_attention(rp, pp, vt, dgv, gb, kcmp, vcmp)

    half = w_out.shape[0] // 2
    wo_c = w_out[:half].astype(BF16)
    wo_d = w_out[half:][perm].astype(BF16)
    g_mix, b_mix, g_ffn, b_ffn = ln
    router = (_pad_cols(w_router, LANES), _pad_cols(b_router.reshape(1, -1), LANES))
    x1, x1b, rt = _outproj_ln(x, o_c.reshape(n, -1), o_d.reshape(n, -1), wo_c, wo_d, g_mix, b_mix, router)

    slot_tok, slot, block_e, n_used = _moe_layout(rt, n)
    y_slots = _moe_ffn(x1b[slot_tok], block_e, n_used, wg, wu, wd)
    return _combine_ln(x1, y_slots[slot[:, 0]], y_slots[slot[:, 1]], rt, g_ffn, b_ffn)


@jax.jit
def kernel(x, ev_w_in, ev_w_out, dif_lambda, dif_subln, ffd_w_gate, ffd_w_up, ffd_w_down, od_w_in, od_w_out,
           nsa_gate_b, nsa_pe, nsa_phi_w1, nsa_phi_w2, moe_w_router, moe_b_router, moe_w_gate, moe_w_up,
           moe_w_down, ln_mix_g, ln_mix_b, ln_ffn_g, ln_ffn_b):
    bsz, seq_len, d = x.shape
    tabs = _rope_tables(seq_len)
    xf = x.reshape(bsz * seq_len, d)
    xb = xf.astype(BF16)
    for l in range(DEPTH):
        i = l // 2
        ln = (ln_mix_g[l], ln_mix_b[l], ln_ffn_g[l], ln_ffn_b[l])
        if l % 2 == 0:
            lam_init = 0.8 - 0.6 * math.exp(-0.3 * l)
            xf, xb = _even_layer(xf, xb, ev_w_in[i], ev_w_out[i], dif_lambda[i], dif_subln[i], lam_init,
                                 ffd_w_gate[i], ffd_w_up[i], ffd_w_down[i], ln, tabs, bsz, seq_len)
        else:
            xf, xb = _odd_layer(xf, xb, od_w_in[i], od_w_out[i], nsa_gate_b[i], nsa_pe[i], nsa_phi_w1[i],
                                nsa_phi_w2[i], moe_w_router[i], moe_b_router[i], _layer_to_bf16(moe_w_gate, i),
                                _layer_to_bf16(moe_w_up, i), _layer_to_bf16(moe_w_down, i), ln, tabs, bsz, seq_len)
    return xf.reshape(bsz, seq_len, d)
```

```python
import functools
import math

import numpy as np
import jax
import jax.numpy as jnp
from jax import lax
from jax.experimental import pallas as pl
from jax.experimental.pallas import tpu as pltpu

F32 = jnp.float32
BF16 = jnp.bfloat16
I32 = jnp.int32

LANES = 128
VMEM_LIMIT = 56 * 1024 * 1024

DEPTH = 4
HEAD_DIM = 64
ROPE_THETA = 10000.0
LN_EPS = 1e-5
DN_ALPHA = (2 * DEPTH) ** 0.25
SCALE = HEAD_DIM ** -0.5 * math.log2(math.e)
NEG = -1e30
BIG = 1e30
M_INIT = -1e30
NINF = float("-inf")

A_HEADS = 4
B_HEADS = 8
IDX_HEADS = 4
DSA_TOPK = 256
C_HEADS = 8
MOBA_BLOCK = 256
MOBA_TOPK = 3
D_HEADS = 8
NSA_CMP_LEN = 32
NSA_CMP_STRIDE = 16
NSA_SLC_BLOCK = 64
NSA_SLC_TOPK = 16
NSA_WINDOW = 512
N_EXPERTS = 8
TOP_K = 2
MOE_TM = 512
N_BISECT = 14
ROW_TILE = 1024
FFN_TILE = 512
KV_GROUPS = 2

EVEN_SIZES = (A_HEADS * 2 * HEAD_DIM, A_HEADS * 2 * HEAD_DIM, A_HEADS * 2 * HEAD_DIM, B_HEADS * HEAD_DIM,
              KV_GROUPS * HEAD_DIM, KV_GROUPS * HEAD_DIM, IDX_HEADS * HEAD_DIM, HEAD_DIM, IDX_HEADS)
ODD_SIZES = (C_HEADS * HEAD_DIM,) * 3 + (D_HEADS * HEAD_DIM,) + (KV_GROUPS * HEAD_DIM,) * 6 + (D_HEADS * 3,)


def _cparams(sem):
    return pltpu.CompilerParams(dimension_semantics=sem, vmem_limit_bytes=VMEM_LIMIT)


def _dot_nt(a, b):
    return lax.dot_general(a, b, (((1,), (1,)), ((), ())), preferred_element_type=F32)


def _layer_norm(y, g, b):
    mu = jnp.mean(y, axis=-1, keepdims=True)
    yc = y - mu
    var = jnp.mean(yc * yc, axis=-1, keepdims=True)
    return yc * lax.rsqrt(var + LN_EPS) * g + b


def _safe_recip(l):
    return jnp.where(l > 0.0, 1.0 / jnp.where(l > 0.0, l, 1.0), 0.0)


def _split_halves(t, roped=False):
    lane = lax.broadcasted_iota(I32, (1, LANES), 1)
    lo = (lane // (HEAD_DIM // 2)) % 2 == 0 if roped else lane < HEAD_DIM
    z = jnp.zeros_like(t)
    return jnp.where(lo, t, z), jnp.where(lo, z, t)


def _rope_layout(n_cols):
    q = HEAD_DIM // 2
    tile = np.concatenate([np.arange(0, q), np.arange(2 * q, 3 * q), np.arange(q, 2 * q), np.arange(3 * q, 4 * q)])
    return (np.arange(0, n_cols, LANES)[:, None] + tile[None, :]).reshape(-1).astype(np.int32)


def _rope_mm_kernel(x_ref, w_ref, cos_ref, sin_ref, o_ref):
    acc = jnp.dot(x_ref[...].astype(BF16), w_ref[...], preferred_element_type=F32)
    cos = cos_ref[...]
    sin = sin_ref[...]
    for c in range(acc.shape[1] // LANES):
        a = acc[:, c * LANES:(c + 1) * LANES]
        rot = pltpu.roll(a, LANES // 2, 1)
        o_ref[:, c * LANES:(c + 1) * LANES] = (a * cos + rot * sin).astype(o_ref.dtype)


def _proj_rope(x, w, tm, tn, seq_len, rope_tabs):
    n, d = x.shape
    p = w.shape[1]
    nt = seq_len // tm
    return pl.pallas_call(
        _rope_mm_kernel,
        grid=(n // tm, p // tn),
        in_specs=[pl.BlockSpec((tm, d), lambda i, j: (i, 0)), pl.BlockSpec((d, tn), lambda i, j: (0, j)),
                  pl.BlockSpec((tm, LANES), lambda i, j: (i % nt, 0)),
                  pl.BlockSpec((tm, LANES), lambda i, j: (i % nt, 0))],
        out_specs=pl.BlockSpec((tm, tn), lambda i, j: (i, j)),
        out_shape=jax.ShapeDtypeStruct((n, p), BF16),
        compiler_params=_cparams(("parallel", "arbitrary")),
        name="proj_rope",
    )(x, w, *rope_tabs)


def _rope_tables(seq_len):
    d = HEAD_DIM
    inv = ROPE_THETA ** (-jnp.arange(0, d, 2, dtype=F32) / d)
    ang = jnp.arange(seq_len, dtype=I32).astype(F32)[:, None] * inv[None, :]
    cos = jnp.cos(ang)
    sin = jnp.sin(ang)
    cos128 = jnp.tile(cos, (1, LANES // (d // 2)))
    sin128 = jnp.concatenate([-sin, -sin, sin, sin], axis=1)
    return cos128, sin128


def _route_top2(x, w, b):
    xh, wh = x.astype(BF16), w.astype(BF16)
    xl, wl = (x - xh.astype(F32)).astype(BF16), (w - wh.astype(F32)).astype(BF16)
    logits = (jnp.dot(xh, wh, preferred_element_type=F32) + jnp.dot(xh, wl, preferred_element_type=F32)
              + jnp.dot(xl, wh, preferred_element_type=F32)) + b
    lane = lax.broadcasted_iota(I32, (1, LANES), 1)
    lanef = lane.astype(F32)
    v = jnp.where(lane < N_EXPERTS, logits, NINF)
    l0 = jnp.max(v, axis=1, keepdims=True)
    i0 = jnp.min(jnp.where(v == l0, lanef, float(LANES)), axis=1, keepdims=True)
    v = jnp.where(lanef == i0, NINF, v)
    l1 = jnp.max(v, axis=1, keepdims=True)
    i1 = jnp.min(jnp.where(v == l1, lanef, float(LANES)), axis=1, keepdims=True)
    e1 = jnp.exp(l1 - l0)
    g0 = 1.0 / (1.0 + e1)
    g1 = e1 / (1.0 + e1)
    return jnp.where(lane == 0, i0, jnp.where(lane == 1, i1, jnp.where(lane == 2, g0, jnp.where(lane == 3, g1, 0.0))))


def _outproj_ln_kernel(x_ref, a_ref, b_ref, wa_ref, wb_ref, g_ref, bb_ref, *rest):
    mix = (jnp.dot(a_ref[...], wa_ref[...], preferred_element_type=F32)
           + jnp.dot(b_ref[...], wb_ref[...], preferred_element_type=F32))
    y = _layer_norm(DN_ALPHA * x_ref[...] + mix, g_ref[...], bb_ref[...])
    if len(rest) == 2:
        xo_ref, xb_ref = rest
    else:
        wr_ref, br_ref, xo_ref, xb_ref, rt_ref = rest
        rt_ref[...] = _route_top2(y, wr_ref[...], br_ref[...])
    xo_ref[...] = y
    xb_ref[...] = y.astype(BF16)


def _outproj_ln(x, oa, ob, wa, wb, g, b, router=None, tm=ROW_TILE):
    n, d = x.shape
    ka, kb = oa.shape[1], ob.shape[1]
    row = lambda i: (i, 0)
    fixed = lambda i: (0, 0)
    in_specs = [pl.BlockSpec((tm, d), row), pl.BlockSpec((tm, ka), row), pl.BlockSpec((tm, kb), row),
                pl.BlockSpec((ka, d), fixed), pl.BlockSpec((kb, d), fixed),
                pl.BlockSpec((1, d), fixed), pl.BlockSpec((1, d), fixed)]
    out_specs = [pl.BlockSpec((tm, d), row), pl.BlockSpec((tm, d), row)]
    out_shape = [jax.ShapeDtypeStruct((n, d), F32), jax.ShapeDtypeStruct((n, d), BF16)]
    args = [x, oa, ob, wa, wb, g.reshape(1, d), b.reshape(1, d)]
    if router is not None:
        in_specs += [pl.BlockSpec((d, LANES), fixed), pl.BlockSpec((1, LANES), fixed)]
        out_specs.append(pl.BlockSpec((tm, LANES), row))
        out_shape.append(jax.ShapeDtypeStruct((n, LANES), F32))
        args += list(router)
    return pl.pallas_call(
        _outproj_ln_kernel,
        grid=(n // tm,),
        in_specs=in_specs,
        out_specs=out_specs,
        out_shape=out_shape,
        compiler_params=_cparams(("parallel",)),
        name="outproj_ln",
    )(*args)


def _ffn_ln_kernel(x_ref, xb_ref, wg_ref, wu_ref, wd_ref, g_ref, b_ref, xo_ref, xob_ref):
    xb = xb_ref[...]
    h = jax.nn.silu(jnp.dot(xb, wg_ref[...], preferred_element_type=F32)) * jnp.dot(
        xb, wu_ref[...], preferred_element_type=F32)
    ffn = jnp.dot(h.astype(BF16), wd_ref[...], preferred_element_type=F32)
    y = _layer_norm(DN_ALPHA * x_ref[...] + ffn, g_ref[...], b_ref[...])
    xo_ref[...] = y
    xob_ref[...] = y.astype(BF16)


def _ffn_ln(x, xb, wg, wu, wd, g, b, tm=FFN_TILE):
    n, d = x.shape
    fdim = wg.shape[1]
    row = lambda i: (i, 0)
    fixed = lambda i: (0, 0)
    once = pl.Buffered(1)
    return pl.pallas_call(
        _ffn_ln_kernel,
        grid=(n // tm,),
        in_specs=[pl.BlockSpec((tm, d), row), pl.BlockSpec((tm, d), row),
                  pl.BlockSpec((d, fdim), fixed, pipeline_mode=once),
                  pl.BlockSpec((d, fdim), fixed, pipeline_mode=once),
                  pl.BlockSpec((fdim, d), fixed, pipeline_mode=once),
                  pl.BlockSpec((1, d), fixed), pl.BlockSpec((1, d), fixed)],
        out_specs=[pl.BlockSpec((tm, d), row), pl.BlockSpec((tm, d), row)],
        out_shape=[jax.ShapeDtypeStruct((n, d), F32), jax.ShapeDtypeStruct((n, d), BF16)],
        compiler_params=_cparams(("parallel",)),
        name="ffn_ln",
    )(x, xb, wg, wu, wd, g.reshape(1, d), b.reshape(1, d))


def _top_n_mask(v, n, axis):
    idx = lax.broadcasted_iota(I32, v.shape, axis).astype(F32)
    sel = jnp.zeros(v.shape, F32)
    for _ in range(n):
        mx = jnp.max(v, axis=axis, keepdims=True)
        first = jnp.min(jnp.where(v == mx, idx, float(v.shape[axis])), axis=axis, keepdims=True)
        pick = idx == first
        sel = jnp.where(pick, 1.0, sel)
        v = jnp.where(pick, NINF, v)
    return sel


DV_PAD = 16
TQ = 256


def _with_ones(vt):
    return jnp.concatenate([vt, jnp.ones((DV_PAD, vt.shape[1]), vt.dtype)], axis=0)


def _mm_t_kernel(x_ref, wv_ref, ws_ref, *rest, tk):
    x = x_ref[...].astype(BF16)
    vt_ref, small_ref = rest[-2:] if len(rest) == 2 else rest[1:3]
    acc = jnp.dot(x, wv_ref[...], preferred_element_type=F32)
    for cc in range(acc.shape[0] // tk):
        vt_ref[cc] = acc[cc * tk:(cc + 1) * tk, :].T.astype(vt_ref.dtype)
    small_ref[...] = jnp.dot(x, ws_ref[...], preferred_element_type=F32)
    if len(rest) == 4:
        rest[3][...] = jnp.dot(x, rest[0][...], preferred_element_type=F32).astype(rest[3].dtype)


def _proj_values(x, w_val, w_small, w_plain, tm, tk):
    n, d = x.shape
    pv, ps = w_val.shape[1], w_small.shape[1]
    row = lambda i: (i, 0)
    fixed = lambda i: (0, 0)
    in_specs = [pl.BlockSpec((tm, d), row), pl.BlockSpec((d, pv), fixed), pl.BlockSpec((d, ps), fixed)]
    out_specs = [pl.BlockSpec((tm // tk, pv, tk), lambda i: (i, 0, 0)), pl.BlockSpec((tm, ps), row)]
    out_shape = [jax.ShapeDtypeStruct((n // tk, pv, tk), BF16), jax.ShapeDtypeStruct((n, ps), F32)]
    args = [x, w_val, w_small]
    if w_plain is not None:
        pp = w_plain.shape[1]
        in_specs.append(pl.BlockSpec((d, pp), fixed))
        out_specs.append(pl.BlockSpec((tm, pp), row))
        out_shape.append(jax.ShapeDtypeStruct((n, pp), BF16))
        args.append(w_plain)
    return pl.pallas_call(
        functools.partial(_mm_t_kernel, tk=tk),
        grid=(n // tm,),
        in_specs=in_specs,
        out_specs=out_specs,
        out_shape=out_shape,
        compiler_params=_cparams(("parallel",)),
        name="proj_t",
    )(*args)


def _normalize_t(acc, width):
    return acc[:width] * _safe_recip(acc[width:width + 1])


def _pipe_flash(n, ns, qk, vt_at, bias_at, bufs, dv, tk, tq):
    sa, sb, pa, pb = bufs

    def softmax_into(p_ref, j, st, m):
        m_new = jnp.maximum(m, jnp.max(st, axis=0, keepdims=True))
        p_ref[j] = jnp.exp2((st - m_new).astype(BF16))
        return m_new, jnp.exp2(m - m_new)

    def half(c, carry, s_cur, s_nxt, p_prev, p_cur):
        if s_nxt is not None:
            nxt = jnp.minimum(c + 1, n - 1)
            for j in range(ns):
                s_nxt[j] = qk(nxt, j)
        out = []
        for j in range(ns):
            m, acc, alpha = carry[j]
            acc = alpha * acc + jnp.dot(vt_at(c - 1, j), p_prev[j], preferred_element_type=F32)
            m, alpha = softmax_into(p_cur, j, s_cur[j] + bias_at(c, j), m)
            out.append((m, acc, alpha))
        return tuple(out)

    for j in range(ns):
        sa[j] = qk(0, j)
    first = []
    for j in range(ns):
        sb[j] = qk(jnp.minimum(1, n - 1), j)
        m, alpha = softmax_into(pa, j, sa[j] + bias_at(0, j), jnp.full((1, tq), M_INIT, F32))
        first.append((m, jnp.zeros((dv, tq), F32), alpha))

    def body(t, carry):
        carry = half(2 * t + 1, carry, sb, sa, pa, pb)
        return half(2 * t + 2, carry, sa, sb, pb, pa)

    carry = lax.fori_loop(0, (n - 1) // 2, body, tuple(first))

    def flush(carry, p_last):
        return tuple(alpha * acc + jnp.dot(vt_at(n - 1, j), p_last[j], preferred_element_type=F32)
                     for j, (_, acc, alpha) in enumerate(carry))

    def odd_tail(carry):
        return flush(half(n - 1, carry, sb, None, pa, pb), pb)

    return lax.cond(n % 2 == 0, odd_tail, lambda carry: flush(carry, pa), carry)


def _pipe_scratch(ns, tk, tq):
    return [pltpu.VMEM((ns, tk, tq), F32)] * 2 + [pltpu.VMEM((ns, tk, tq), BF16)] * 2


def _causal_t(t):
    return jnp.where(lax.broadcasted_iota(I32, (t, t), 0) <= lax.broadcasted_iota(I32, (t, t), 1), 0.0, NINF)


def _diff_kernel(lam_ref, sub_ref, q_ref, k_ref, vt_ref, o_ref, *bufs, tq, lam_init):
    i = pl.program_id(2)
    lp = lam_ref[...]
    lam = (jnp.exp(jnp.sum(lp[0:1] * lp[1:2], axis=1, keepdims=True))
           - jnp.exp(jnp.sum(lp[2:3] * lp[3:4], axis=1, keepdims=True)) + lam_init)
    qs = _split_halves(q_ref[...], roped=True)
    n = i + 1
    bufs, tab_ref = bufs[:4], bufs[4]
    tab_ref[0] = jnp.zeros((tq, tq), F32)
    tab_ref[1] = _causal_t(tq)

    def qk(c, j):
        off = pl.multiple_of(c * tq, tq)
        return _dot_nt(k_ref[pl.ds(off, tq), :], qs[j])

    def bias_at(c, j):
        return tab_ref[jnp.where(c == i, 1, 0)]

    outs = _pipe_flash(n, 2, qk, lambda c, j: _with_ones(vt_ref[c]), bias_at, bufs, LANES + DV_PAD, tq, tq)
    o = _normalize_t(outs[0], LANES) - lam * _normalize_t(outs[1], LANES)
    o = o * lax.rsqrt(jnp.mean(o * o, axis=0, keepdims=True) + LN_EPS)
    o = o * sub_ref[...] * (1.0 - lam_init)
    o_ref[...] = o.T.astype(o_ref.dtype)


def _diff_attention(rp, vt, lam_params, subln, lam_init, tq=TQ):
    b, s, _ = rp.shape
    nk = s // tq
    return pl.pallas_call(
        functools.partial(_diff_kernel, tq=tq, lam_init=lam_init),
        grid=(b, A_HEADS, nk),
        in_specs=[pl.BlockSpec((4, HEAD_DIM), lambda bi, h, i: (0, 0)),
                  pl.BlockSpec((LANES, 1), lambda bi, h, i: (0, 0)),
                  pl.BlockSpec((None, tq, LANES), lambda bi, h, i: (bi, i, h)),
                  pl.BlockSpec((None, s, LANES), lambda bi, h, i: (bi, 0, A_HEADS + h)),
                  pl.BlockSpec((None, nk, LANES, tq), lambda bi, h, i: (bi, 0, h, 0))],
        out_specs=pl.BlockSpec((None, tq, LANES), lambda bi, h, i: (bi, i, h)),
        out_shape=jax.ShapeDtypeStruct((b, s, A_HEADS * LANES), BF16),
        scratch_shapes=_pipe_scratch(2, tq, tq) + [pltpu.VMEM((2, tq, tq), F32)],
        compiler_params=_cparams(("parallel", "parallel", "arbitrary")),
        name="diff_attn",
    )(lam_params, subln.reshape(LANES, 1), rp, rp, vt)


def _fold8(x, op):
    acc = x[0:8]
    for r in range(1, x.shape[0] // 8):
        acc = op(acc, x[r * 8:(r + 1) * 8])
    return acc


def _dsa_kernel(iq_ref, ikk_ref, iw_ref, q_ref, k_ref, vt_ref, o_ref, s_ref, j_ref, *bufs, tq, ksel, seq_len):
    tk = tq
    i = pl.program_id(1)
    nch = i + 1
    ksel_f = float(ksel)
    k_loc = lax.broadcasted_iota(I32, (tk, tq), 0)
    q_loc = lax.broadcasted_iota(I32, (tk, tq), 1)
    qpos = i * tq + lax.broadcasted_iota(I32, (1, tq), 1)

    iq = iq_ref[...]
    iwt = iw_ref[...].T
    iqh = []
    for pair in range(IDX_HEADS // 2):
        iqh += list(_split_halves(iq[:, pair * LANES:(pair + 1) * LANES], roped=True))

    def scores(c):
        off = pl.multiple_of(c * tk, tk)
        kk = ikk_ref[pl.ds(off, tk), :]
        sc = iwt[0:1] * jnp.maximum(_dot_nt(kk, iqh[0]), 0.0)
        for h in range(1, IDX_HEADS):
            sc = sc + iwt[h:h + 1] * jnp.maximum(_dot_nt(kk, iqh[h]), 0.0)
        return sc

    def full_body(c, carry):
        mx, mn = carry
        sc = scores(c)
        s_ref[c] = sc
        return jnp.maximum(mx, _fold8(sc, jnp.maximum)), jnp.minimum(mn, _fold8(sc, jnp.minimum))

    mx, mn = lax.fori_loop(0, i, full_body, (jnp.full((8, tq), -BIG, F32), jnp.full((8, tq), BIG, F32)))
    sc = scores(i)
    causal = k_loc <= q_loc
    s_ref[i] = jnp.where(causal, sc, NEG)
    mx = jnp.maximum(mx, _fold8(jnp.where(causal, sc, -BIG), jnp.maximum))
    mn = jnp.minimum(mn, _fold8(jnp.where(causal, sc, BIG), jnp.minimum))
    smax = jnp.max(mx, axis=0, keepdims=True)
    smin = jnp.min(mn, axis=0, keepdims=True)

    def count_where(ind):
        def body(c, acc):
            return acc + _fold8(ind(s_ref[c], c * tk + k_loc), jnp.add)
        acc = lax.fori_loop(0, nch, body, jnp.zeros((8, tq), F32))
        return jnp.sum(acc, axis=0, keepdims=True)

    def count_ge(th):
        return count_where(lambda x, kidx: jnp.where(x >= th, 1.0, 0.0))

    def max_below(th):
        def body(c, acc):
            x = s_ref[c]
            return jnp.maximum(acc, _fold8(jnp.where(x < th, x, NINF), jnp.maximum))
        acc = lax.fori_loop(0, nch, body, jnp.full((8, tq), NINF, F32))
        return jnp.max(acc, axis=0, keepdims=True)

    n_causal = (qpos + 1).astype(F32)
    take_all = n_causal <= ksel_f
    done0 = jnp.where(take_all, 1.0, 0.0)
    hi0 = smax + (jnp.abs(smax) * 2.0 ** -20 + 1e-30)

    def bisect(lo, hi):
        mid = lo + (hi - lo) * 0.5
        ge = count_ge(mid) >= ksel_f
        return jnp.where(ge, mid, lo), jnp.where(ge, hi, mid)

    lo, hi = lax.fori_loop(0, N_BISECT, lambda _, c: bisect(*c), (smin, hi0))

    def snap_body(carry):
        lo, hi, th, c_th, done, _ = carry
        lo, hi = bisect(lo, hi)
        t1 = max_below(hi)
        c1 = count_ge(t1)
        ok = c1 >= ksel_f
        th = jnp.where(done > 0.0, th, t1)
        c_th = jnp.where(done > 0.0, c_th, c1)
        hi = jnp.where(ok, hi, t1)
        done = jnp.where(ok, 1.0, done)
        return lo, hi, th, c_th, done, jnp.sum(1.0 - done)

    _, _, th, c_ge, _, _ = lax.while_loop(lambda c: c[5] > 0.0, snap_body,
                                          (lo, hi, smax, jnp.zeros((1, tq), F32), done0, jnp.sum(1.0 - done0)))

    need_tb = jnp.where(take_all, 0.0, jnp.where(c_ge > ksel_f, 1.0, 0.0))
    j_ref[...] = jnp.full((8, tq), seq_len - 1, I32)

    @pl.when(jnp.sum(need_tb) > 0.0)
    def _():
        need = ksel_f - count_where(lambda x, kidx: jnp.where(x > th, 1.0, 0.0))

        def jb(_, carry):
            lo_j, hi_j = carry
            mid = (lo_j + hi_j) // 2
            cnt = count_where(lambda x, kidx: jnp.where(x == th, jnp.where(kidx <= mid, 1.0, 0.0), 0.0))
            ge = cnt >= need
            return jnp.where(ge, lo_j, mid), jnp.where(ge, mid, hi_j)

        n_it = int(math.ceil(math.log2(seq_len))) + 1
        _, hi_j = lax.fori_loop(0, n_it, jb, (jnp.full((1, tq), -1, I32), jnp.full((1, tq), seq_len - 1, I32)))
        j_ref[...] = jnp.broadcast_to(hi_j, (8, tq))

    jsel = j_ref[0:1, :]

    def bias_body(c, _):
        x = s_ref[c]
        kidx = c * tk + k_loc
        keep = jnp.where(x > th, 0.0, jnp.where(x == th, jnp.where(kidx <= jsel, 0.0, NINF), NINF))
        keep = jnp.where(take_all, 0.0, keep)
        s_ref[c] = jnp.where(kidx <= qpos, keep, NINF)
        return 0

    lax.fori_loop(0, nch, bias_body, 0)

    for p in range(B_HEADS // 2):
        qs = _split_halves(q_ref[:, p * LANES:(p + 1) * LANES], roped=True)

        def qk(c, j, qs=qs):
            off = pl.multiple_of(c * tk, tk)
            return _dot_nt(k_ref[pl.ds(off, tk), :], qs[j])

        outs = _pipe_flash(nch, 2, qk, lambda c, j: _with_ones(vt_ref[c, j * HEAD_DIM:(j + 1) * HEAD_DIM, :]),
                           lambda c, j: s_ref[c], bufs, HEAD_DIM + DV_PAD, tk, tq)
        o = jnp.concatenate([_normalize_t(outs[0], HEAD_DIM), _normalize_t(outs[1], HEAD_DIM)], axis=0)
        o_ref[:, p * LANES:(p + 1) * LANES] = o.T.astype(o_ref.dtype)


def _dsa_attention(rp, vt, iw, tq=TQ):
    b, s, _ = rp.shape
    ksel = min(DSA_TOPK, s // 4)
    nk = s // tq
    return pl.pallas_call(
        functools.partial(_dsa_kernel, tq=tq, ksel=ksel, seq_len=s),
        grid=(b, nk),
        in_specs=[pl.BlockSpec((None, tq, 2 * LANES), lambda bi, i: (bi, i, 6)),
                  pl.BlockSpec((None, s, LANES), lambda bi, i: (bi, 0, 15)),
                  pl.BlockSpec((None, tq, LANES), lambda bi, i: (bi, i, 0)),
                  pl.BlockSpec((None, tq, 4 * LANES), lambda bi, i: (bi, i, 2)),
                  pl.BlockSpec((None, s, LANES), lambda bi, i: (bi, 0, 14)),
                  pl.BlockSpec((None, nk, LANES, tq), lambda bi, i: (bi, 0, 4, 0))],
        out_specs=pl.BlockSpec((None, tq, 4 * LANES), lambda bi, i: (bi, i, 0)),
        out_shape=jax.ShapeDtypeStruct((b, s, 4 * LANES), BF16),
        scratch_shapes=[pltpu.VMEM((nk, tq, tq), F32), pltpu.VMEM((8, tq), I32)] + _pipe_scratch(2, tq, tq),
        compiler_params=_cparams(("parallel", "arbitrary")),
        name="dsa_attn",
    )(rp, rp, iw, rp, rp, vt)


def _moba_kernel(q_ref, k_ref, vt_ref, o_ref, km_ref, sel_ref, *bufs, seq_len, n_sel):
    tq = MOBA_BLOCK
    qb = pl.program_id(2)

    @pl.when(qb == 0)
    def _():
        j = lax.broadcasted_iota(I32, (LANES, seq_len), 0)
        s = lax.broadcasted_iota(I32, (LANES, seq_len), 1)
        avg = jnp.where(s // MOBA_BLOCK == j, 1.0 / MOBA_BLOCK, 0.0).astype(BF16)
        km_ref[...] = jnp.dot(avg, k_ref[...], preferred_element_type=F32)

    nbp = sel_ref.shape[1]
    km = km_ref[0:nbp, :]
    qs = _split_halves(q_ref[...], roped=True)
    blk = lax.broadcasted_iota(I32, (nbp, tq), 0)
    past = blk < qb
    for j in range(2):
        gate = lax.dot_general(km, qs[j].astype(F32), (((1,), (1,)), ((), ())),
                               precision=lax.Precision.HIGHEST, preferred_element_type=F32)
        gate = jnp.where(blk < seq_len // MOBA_BLOCK, jnp.where(past, gate, NEG), NINF)
        sel = _top_n_mask(gate, n_sel, 0)
        sel_ref[j] = jnp.where(past, jnp.where(sel > 0.5, 0.0, NINF), NINF)
    own = _causal_t(tq)
    n = qb + 1

    def qk(c, j):
        off = pl.multiple_of(c * tq, tq)
        return _dot_nt(k_ref[pl.ds(off, tq), :], qs[j])

    def bias_at(c, j):
        chosen = sel_ref[j, pl.ds(c, 1), :]
        return jnp.where(c == qb, own, chosen)

    outs = _pipe_flash(n, 2, qk, lambda c, j: _with_ones(vt_ref[c, j * HEAD_DIM:(j + 1) * HEAD_DIM, :]), bias_at, bufs,
                       HEAD_DIM + DV_PAD, tq, tq)
    o = jnp.concatenate([_normalize_t(outs[0], HEAD_DIM), _normalize_t(outs[1], HEAD_DIM)], axis=0)
    o_ref[...] = o.T.astype(o_ref.dtype)


def _moba_attention(rp, vt):
    b, s, _ = rp.shape
    tq = MOBA_BLOCK
    nb = s // tq
    n_sel = max(1, min(MOBA_TOPK, nb - 1))
    npair = C_HEADS // 2
    return pl.pallas_call(
        functools.partial(_moba_kernel, seq_len=s, n_sel=n_sel),
        grid=(b, npair, nb),
        in_specs=[pl.BlockSpec((None, tq, LANES), lambda bi, h, i: (bi, i, h)),
                  pl.BlockSpec((None, s, LANES), lambda bi, h, i: (bi, 0, npair + h)),
                  pl.BlockSpec((None, nb, LANES, tq), lambda bi, h, i: (bi, 0, h, 0))],
        out_specs=pl.BlockSpec((None, tq, LANES), lambda bi, h, i: (bi, i, h)),
        out_shape=jax.ShapeDtypeStruct((b, s, npair * LANES), BF16),
        scratch_shapes=[pltpu.VMEM((LANES, LANES), F32), pltpu.VMEM((2, -(-nb // 8) * 8, tq), F32)]
        + _pipe_scratch(2, tq, tq),
        compiler_params=_cparams(("parallel", "parallel", "arbitrary")),
        name="moba_attn",
    )(rp, rp, vt)


def _cmp_kernel(r_ref, pe_ref, w1_ref, w2_ref, o_ref):
    r = r_ref[...]
    w1 = w1_ref[...]
    half = r.shape[1]
    u = jnp.dot(r, w1[:half], preferred_element_type=F32)
    v = jnp.dot(r, w1[half:], preferred_element_type=F32)
    c = jnp.dot(pe_ref[...], w1, preferred_element_type=F32)[0:1]
    pre = u + pltpu.roll(v, r.shape[0] - 1, 0) + c
    o_ref[...] = jnp.dot(jax.nn.gelu(pre).astype(BF16), w2_ref[...],
                         preferred_element_type=F32).astype(o_ref.dtype)


def _nsa_compress(r, pe, w1, w2):
    b, _, nc, wdt = r.shape
    hid = w1.shape[2]
    return pl.pallas_call(
        _cmp_kernel,
        grid=(b, 4),
        in_specs=[pl.BlockSpec((None, None, nc, wdt), lambda bi, t: (bi, t, 0, 0)),
                  pl.BlockSpec((None, 8, 2 * wdt), lambda bi, t: (t // 2, 0, 0)),
                  pl.BlockSpec((None, 2 * wdt, hid), lambda bi, t: (t // 2, 0, 0)),
                  pl.BlockSpec((None, hid, HEAD_DIM), lambda bi, t: (t // 2, 0, 0))],
        out_specs=pl.BlockSpec((None, None, nc, HEAD_DIM), lambda bi, t: (bi, t, 0, 0)),
        out_shape=jax.ShapeDtypeStruct((b, 4, nc, HEAD_DIM), BF16),
        compiler_params=_cparams(("parallel", "arbitrary")),
        name="nsa_compress",
    )(r, pe, w1, w2)


def _nsa_kernel(qr_ref, qw_ref, dg_ref, gb_ref, kc_ref, vct_ref, ks_ref, vst_ref, kw_ref, vwt_ref,
                o_ref, sel_ref, wb_ref, *bufs, tq, seq_len):
    tk = tq
    i = pl.program_id(1)
    nch = i + 1
    nc = seq_len // NSA_CMP_STRIDE
    n_sb = seq_len // NSA_SLC_BLOCK
    n_sel = min(NSA_SLC_TOPK, n_sb)
    k_loc = lax.broadcasted_iota(I32, (tk, tq), 0)
    q_loc = lax.broadcasted_iota(I32, (tk, tq), 1)
    qpos = i * tq + lax.broadcasted_iota(I32, (1, tq), 1)

    gates_t = jax.nn.sigmoid(dg_ref[...] + gb_ref[...]).T
    kc = kc_ref[...]
    cmp_end = lax.broadcasted_iota(I32, (nc, 1), 0) * NSA_CMP_STRIDE + (NSA_CMP_LEN - 1)
    cbias = jnp.where(cmp_end <= qpos, 0.0, NINF)

    nh = D_HEADS // 2
    q_rot = [_split_halves(qr_ref[:, p * LANES:(p + 1) * LANES], roped=True) for p in range(nh)]
    q_raw = [_split_halves(qw_ref[:, p * LANES:(p + 1) * LANES]) for p in range(nh)]
    cbias4 = jnp.concatenate([cbias] * nh, axis=1)
    o_cmp, psum = [], []
    for g in range(2):
        qg = jnp.concatenate([q_raw[p][g] for p in range(nh)], axis=0)
        s = _dot_nt(kc, qg) + cbias4
        m = jnp.max(s, axis=0, keepdims=True)
        e = jnp.exp2(s - jnp.where(m == NINF, 0.0, m))
        pc = e * _safe_recip(jnp.sum(e, axis=0, keepdims=True))
        psum.append(sum(pc[:, p * tq:(p + 1) * tq] for p in range(nh)))
        o_cmp.append(jnp.dot(vct_ref[g], pc.astype(BF16), preferred_element_type=F32))

    per = tk // NSA_SLC_BLOCK
    nbp = -(-n_sb // 8) * 8
    cn = lax.broadcasted_iota(I32, (nbp, nc), 1) * NSA_CMP_STRIDE
    sj = lax.broadcasted_iota(I32, (nbp, nc), 0) * NSA_SLC_BLOCK
    shares = jnp.where((cn <= sj + NSA_SLC_BLOCK - 1) & (cn + NSA_CMP_LEN - 1 >= sj), 1.0, 0.0)
    blk = lax.broadcasted_iota(I32, (nbp, tq), 0)
    cur = qpos // NSA_SLC_BLOCK
    causal_b = blk <= cur
    forced = (blk == 0) | ((blk >= cur - 1) & causal_b)
    for g in range(2):
        imp = jnp.dot(shares, psum[g], precision=lax.Precision.HIGHEST, preferred_element_type=F32)
        val = jnp.where(forced, BIG, jnp.where(causal_b, imp, NEG))
        val = jnp.where(blk < n_sb, val, NINF)
        rowb = jnp.where(_top_n_mask(val, n_sel, 0) > 0.5, 0.0, NINF)
        for c in range(seq_len // tk):
            sel_ref[g, c] = jnp.concatenate([rowb[c * per:(c + 1) * per], jnp.zeros((8 - per, tq), F32)], axis=0)

    wb_ref[0] = jnp.where(k_loc <= q_loc, 0.0, NINF)
    wb_ref[1] = jnp.zeros((tk, tq), F32)
    wb_ref[2] = jnp.where(k_loc > q_loc, 0.0, NINF)
    n_wc = NSA_WINDOW // tk + 1
    w_first = jnp.maximum(i - (n_wc - 1), 0)
    n_w = i - w_first + 1

    for p in range(D_HEADS // 2):
        qs = q_rot[p]

        def qk_s(c, j, qs=qs):
            off = pl.multiple_of(c * tk, tk)
            return _dot_nt(ks_ref[pl.ds(off, tk), :], qs[j])

        def bias_s(c, j):
            rows = sel_ref[j, c]
            tile = jnp.concatenate([jnp.broadcast_to(rows[r:r + 1], (NSA_SLC_BLOCK, tq)) for r in range(per)], axis=0)
            return tile + wb_ref[jnp.where(c == i, 0, 1)]

        o_slc = _pipe_flash(nch, 2, qk_s, lambda c, j: _with_ones(vst_ref[c, j * HEAD_DIM:(j + 1) * HEAD_DIM, :]),
                            bias_s, bufs, HEAD_DIM + DV_PAD, tk, tq)

        def qk_w(c, j, qs=qs):
            off = pl.multiple_of((w_first + c) * tk, tk)
            return _dot_nt(kw_ref[pl.ds(off, tk), :], qs[j])

        def bias_w(c, j):
            return wb_ref[i - (w_first + c)]

        o_win = _pipe_flash(n_w, 2, qk_w,
                            lambda c, j: _with_ones(vwt_ref[w_first + c, j * HEAD_DIM:(j + 1) * HEAD_DIM, :]), bias_w, bufs,
                            HEAD_DIM + DV_PAD, tk, tq)
        outs = []
        for g in range(2):
            h = g * (D_HEADS // 2) + p
            outs.append(gates_t[3 * h:3 * h + 1] * o_cmp[g][:, p * tq:(p + 1) * tq]
                        + gates_t[3 * h + 1:3 * h + 2] * _normalize_t(o_slc[g], HEAD_DIM)
                        + gates_t[3 * h + 2:3 * h + 3] * _normalize_t(o_win[g], HEAD_DIM))
        o_ref[:, p * LANES:(p + 1) * LANES] = jnp.concatenate(outs, axis=0).T.astype(o_ref.dtype)


def _nsa_attention(rp, pp, vt, dg, gate_b, kcmp, vcmp, tq=TQ):
    b, s, _ = rp.shape
    nk = s // tq
    nc = s // NSA_CMP_STRIDE
    assert NSA_WINDOW == 2 * tq
    n_wc = NSA_WINDOW // tq + 1
    vct = vcmp.reshape(b, nc, 2, HEAD_DIM).transpose(0, 2, 3, 1)
    full = lambda t: pl.BlockSpec((None, s, LANES), lambda bi, i: (bi, 0, t))
    vspec = lambda t: pl.BlockSpec((None, nk, LANES, tq), lambda bi, i: (bi, 0, t, 0))
    return pl.pallas_call(
        functools.partial(_nsa_kernel, tq=tq, seq_len=s),
        grid=(b, nk),
        in_specs=[pl.BlockSpec((None, tq, 4 * LANES), lambda bi, i: (bi, i, 2)),
                  pl.BlockSpec((None, tq, 4 * LANES), lambda bi, i: (bi, i, 0)),
                  pl.BlockSpec((None, tq, LANES), lambda bi, i: (bi, i, 0)),
                  pl.BlockSpec((1, LANES), lambda bi, i: (0, 0)),
                  pl.BlockSpec((None, nc, LANES), lambda bi, i: (bi, 0, 0)),
                  pl.BlockSpec((None, 2, HEAD_DIM, nc), lambda bi, i: (bi, 0, 0, 0)),
                  full(12), vspec(4), full(13), vspec(5)],
        out_specs=pl.BlockSpec((None, tq, 4 * LANES), lambda bi, i: (bi, i, 0)),
        out_shape=jax.ShapeDtypeStruct((b, s, 4 * LANES), BF16),
        scratch_shapes=[pltpu.VMEM((2, nk, 8, tq), F32), pltpu.VMEM((n_wc, tq, tq), F32)]
        + _pipe_scratch(2, tq, tq),
        compiler_params=_cparams(("parallel", "arbitrary")),
        name="nsa_attn",
    )(rp, pp, dg, gate_b, kcmp, vct, rp, vt, rp, vt)


def _cast_kernel(x_ref, o_ref):
    o_ref[...] = x_ref[...].astype(o_ref.dtype)


def _layer_to_bf16(w, layer):
    _, e, r, c = w.shape
    tr = 1 << ((2 ** 21 // c).bit_length() - 1)
    out = pl.pallas_call(
        _cast_kernel,
        grid=(e * r // tr,),
        in_specs=[pl.BlockSpec((None, tr, c), lambda i: (layer, i, 0))],
        out_specs=pl.BlockSpec((tr, c), lambda i: (i, 0)),
        out_shape=jax.ShapeDtypeStruct((e * r, c), BF16),
        compiler_params=_cparams(("parallel",)),
        name="cast_bf16",
    )(w.reshape(w.shape[0], e * r, c))
    return out.reshape(e, r, c)


def _moe_ffn_kernel(be_ref, nu_ref, x_ref, wg_ref, wu_ref, wd_ref, o_ref):
    used = pl.program_id(0) < nu_ref[0]

    @pl.when(used)
    def _():
        x = x_ref[...]
        h = jax.nn.silu(jnp.dot(x, wg_ref[...], preferred_element_type=F32)) * jnp.dot(
            x, wu_ref[...], preferred_element_type=F32)
        o_ref[...] = jnp.dot(h.astype(BF16), wd_ref[...], preferred_element_type=F32).astype(o_ref.dtype)

    @pl.when(jnp.logical_not(used))
    def _():
        o_ref[...] = jnp.zeros(o_ref.shape, o_ref.dtype)


def _moe_ffn(x_sorted, block_e, n_used, wg, wu, wd):
    ns, d = x_sorted.shape
    fdim = wg.shape[2]
    once = pl.Buffered(1)
    grid_spec = pltpu.PrefetchScalarGridSpec(
        num_scalar_prefetch=2,
        grid=(ns // MOE_TM,),
        in_specs=[pl.BlockSpec((MOE_TM, d), lambda i, be, nu: (jnp.minimum(i, nu[0] - 1), 0)),
                  pl.BlockSpec((None, d, fdim), lambda i, be, nu: (be[i], 0, 0), pipeline_mode=once),
                  pl.BlockSpec((None, d, fdim), lambda i, be, nu: (be[i], 0, 0), pipeline_mode=once),
                  pl.BlockSpec((None, fdim, d), lambda i, be, nu: (be[i], 0, 0), pipeline_mode=once)],
        out_specs=pl.BlockSpec((MOE_TM, d), lambda i, be, nu: (i, 0)),
    )
    return pl.pallas_call(
        _moe_ffn_kernel,
        grid_spec=grid_spec,
        out_shape=jax.ShapeDtypeStruct((ns, d), BF16),
        compiler_params=_cparams(("arbitrary",)),
        name="moe_ffn",
    )(block_e, n_used, x_sorted, wg, wu, wd)


def _combine_ln_kernel(x_ref, y0_ref, y1_ref, rt_ref, g_ref, b_ref, xo_ref, xb_ref):
    rt = rt_ref[...]
    ffn = rt[:, 2:3] * y0_ref[...].astype(F32) + rt[:, 3:4] * y1_ref[...].astype(F32)
    y = _layer_norm(DN_ALPHA * x_ref[...] + ffn, g_ref[...], b_ref[...])
    xo_ref[...] = y
    xb_ref[...] = y.astype(BF16)


def _combine_ln(x, y0, y1, rt, g, b, tm=ROW_TILE):
    n, d = x.shape
    row = lambda i: (i, 0)
    fixed = lambda i: (0, 0)
    return pl.pallas_call(
        _combine_ln_kernel,
        grid=(n // tm,),
        in_specs=[pl.BlockSpec((tm, d), row), pl.BlockSpec((tm, d), row), pl.BlockSpec((tm, d), row),
                  pl.BlockSpec((tm, LANES), row), pl.BlockSpec((1, d), fixed), pl.BlockSpec((1, d), fixed)],
        out_specs=[pl.BlockSpec((tm, d), row), pl.BlockSpec((tm, d), row)],
        out_shape=[jax.ShapeDtypeStruct((n, d), F32), jax.ShapeDtypeStruct((n, d), BF16)],
        compiler_params=_cparams(("parallel",)),
        name="moe_combine_ln",
    )(x, y0, y1, rt, g.reshape(1, d), b.reshape(1, d))


def _moe_layout(rt, n):
    e_flat = rt[:, 0:TOP_K].astype(I32).reshape(-1)
    nk = n * TOP_K
    onehot = (e_flat[:, None] == jnp.arange(N_EXPERTS, dtype=I32)[None, :]).astype(I32)
    rank = jnp.take_along_axis(jnp.cumsum(onehot, axis=0), e_flat[:, None], axis=1)[:, 0] - 1
    counts = jnp.sum(onehot, axis=0)
    padded = (counts + MOE_TM - 1) // MOE_TM * MOE_TM
    pad_end = jnp.cumsum(padded)
    pad_start = pad_end - padded
    grp_start = jnp.cumsum(counts) - counts
    slot = pad_start[e_flat] + rank
    n_blocks = -(-nk // MOE_TM) + N_EXPERTS
    n_slots = n_blocks * MOE_TM
    order = jnp.argsort(e_flat, stable=True).astype(I32)
    sl = jnp.arange(n_slots, dtype=I32)
    slot_e = jnp.minimum(jnp.searchsorted(pad_end, sl, side='right'), N_EXPERTS - 1).astype(I32)
    within = sl - pad_start[slot_e]
    valid = within < counts[slot_e]
    src = jnp.where(valid, grp_start[slot_e] + within, 0)
    slot_tok = jnp.where(valid, order[src] // TOP_K, 0)
    n_used = (pad_end[-1] // MOE_TM).astype(I32).reshape(1)
    blk = jnp.arange(n_blocks, dtype=I32)
    block_e = slot_e[jnp.minimum(blk, n_used[0] - 1) * MOE_TM]
    return slot_tok, slot.reshape(n, TOP_K), block_e, n_used


def _pair_perm(n_heads):
    half = n_heads // 2
    cols = []
    for p in range(half):
        cols += list(range(p * HEAD_DIM, (p + 1) * HEAD_DIM))
        cols += list(range((half + p) * HEAD_DIM, (half + p + 1) * HEAD_DIM))
    return np.asarray(cols, dtype=np.int32)


def _pad_cols(w, width):
    return jnp.pad(w, ((0, 0), (0, width - w.shape[1])))


def _split_cols(w, sizes):
    out, off = [], 0
    for sz in sizes:
        out.append(w[:, off:off + sz])
        off += sz
    return out


def _even_layer(x, xb, w_in, w_out, lam_params, subln, lam_init, wg, wu, wd, ln, tabs, bsz, seq_len):
    n = x.shape[0]
    perm = _pair_perm(B_HEADS)
    aq, ak, av, bq, bk, bv, iq, ik, iw = _split_cols(w_in, EVEN_SIZES)
    w_rope = jnp.concatenate([aq * SCALE, ak, bq[:, perm] * SCALE, iq, bk, ik, ik], axis=1)
    w_rope = w_rope[:, _rope_layout(w_rope.shape[1])].astype(BF16)
    w_val = jnp.concatenate([av, bv], axis=1).astype(BF16)
    w_iw = _pad_cols(iw, LANES).astype(BF16)
    rp = _proj_rope(xb, w_rope, ROW_TILE, w_rope.shape[1], seq_len, tabs).reshape(bsz, seq_len, -1)
    vt, iwv = _proj_values(xb, w_val, w_iw, None, ROW_TILE, TQ)
    vt = vt.reshape(bsz, seq_len // TQ, -1, TQ)
    iwv = iwv.reshape(bsz, seq_len, LANES)
    o_a = _diff_attention(rp, vt, lam_params, subln, lam_init)
    o_b = _dsa_attention(rp, vt, iwv)
    half = w_out.shape[0] // 2
    wo_a = w_out[:half].astype(BF16)
    wo_b = w_out[half:][perm].astype(BF16)
    g_mix, b_mix, g_ffn, b_ffn = ln
    x1, x1b = _outproj_ln(x, o_a.reshape(n, -1), o_b.reshape(n, -1), wo_a, wo_b, g_mix, b_mix)
    return _ffn_ln(x1, x1b, wg.astype(BF16), wu.astype(BF16), wd.astype(BF16), g_ffn, b_ffn)


def _odd_layer(x, xb, w_in, w_out, gate_b, pe, phi_w1, phi_w2, w_router, b_router, wg, wu, wd, ln, tabs,
               bsz, seq_len):
    n = x.shape[0]
    perm = _pair_perm(D_HEADS)
    cq, ck, cv, dq, dkc, dvc, dks, dvs, dkw, dvw, dg = _split_cols(w_in, ODD_SIZES)
    dq = dq[:, perm] * SCALE
    w_rope = jnp.concatenate([cq * SCALE, ck, dq, dks, dkw], axis=1)
    w_rope = w_rope[:, _rope_layout(w_rope.shape[1])].astype(BF16)
    w_plain = jnp.concatenate([dq, dkc, dvc], axis=1).astype(BF16)
    w_val = jnp.concatenate([cv, dvs, dvw], axis=1).astype(BF16)
    w_dg = _pad_cols(dg, LANES).astype(BF16)
    rp = _proj_rope(xb, w_rope, ROW_TILE, w_rope.shape[1], seq_len, tabs).reshape(bsz, seq_len, -1)
    vt, dgv, pp = _proj_values(xb, w_val, w_dg, w_plain, ROW_TILE, TQ)
    vt = vt.reshape(bsz, seq_len // TQ, -1, TQ)
    dgv = dgv.reshape(bsz, seq_len, LANES)
    pp = pp.reshape(bsz, seq_len, -1)

    o_c = _moba_attention(rp, vt)

    nc = seq_len // NSA_CMP_STRIDE
    tok = pp[:, :, 4 * LANES:6 * LANES].reshape(bsz, nc, NSA_CMP_STRIDE, 4, HEAD_DIM)
    r = tok.transpose(0, 3, 1, 2, 4).reshape(bsz, 4, nc, NSA_CMP_STRIDE * HEAD_DIM)
    pe_flat = jnp.pad(pe.reshape(2, 1, -1), ((0, 0), (0, 7), (0, 0))).astype(BF16)
    cmp = _nsa_compress(r, pe_flat, phi_w1.astype(BF16), phi_w2.astype(BF16))
    kcmp = jnp.concatenate([cmp[:, 0], cmp[:, 1]], axis=-1)
    vcmp = jnp.concatenate([cmp[:, 2], cmp[:, 3]], axis=-1)
    gb = _pad_cols(gate_b.reshape(1, -1), LANES)
    o_d = _nsa_attention(rp, pp, vt, dgv, gb, kcmp, vcmp)

    half = w_out.shape[0] // 2
    wo_c = w_out[:half].astype(BF16)
    wo_d = w_out[half:][perm].astype(BF16)
    g_mix, b_mix, g_ffn, b_ffn = ln
    router = (_pad_cols(w_router, LANES), _pad_cols(b_router.reshape(1, -1), LANES))
    x1, x1b, rt = _outproj_ln(x, o_c.reshape(n, -1), o_d.reshape(n, -1), wo_c, wo_d, g_mix, b_mix, router)

    slot_tok, slot, block_e, n_used = _moe_layout(rt, n)
    y_slots = _moe_ffn(x1b[slot_tok], block_e, n_used, wg, wu, wd)
    return _combine_ln(x1, y_slots[slot[:, 0]], y_slots[slot[:, 1]], rt, g_ffn, b_ffn)


@jax.jit
def kernel(x, ev_w_in, ev_w_out, dif_lambda, dif_subln, ffd_w_gate, ffd_w_up, ffd_w_down, od_w_in, od_w_out,
           nsa_gate_b, nsa_pe, nsa_phi_w1, nsa_phi_w2, moe_w_router, moe_b_router, moe_w_gate, moe_w_up,
           moe_w_down, ln_mix_g, ln_mix_b, ln_ffn_g, ln_ffn_b):
    bsz, seq_len, d = x.shape
    tabs = _rope_tables(seq_len)
    xf = x.reshape(bsz * seq_len, d)
    xb = xf.astype(BF16)
    for l in range(DEPTH):
        i = l // 2
        ln = (ln_mix_g[l], ln_mix_b[l], ln_ffn_g[l], ln_ffn_b[l])
        if l % 2 == 0:
            lam_init = 0.8 - 0.6 * math.exp(-0.3 * l)
            xf, xb = _even_layer(xf, xb, ev_w_in[i], ev_w_out[i], dif_lambda[i], dif_subln[i], lam_init,
                                 ffd_w_gate[i], ffd_w_up[i], ffd_w_down[i], ln, tabs, bsz, seq_len)
        else:
            xf, xb = _odd_layer(xf, xb, od_w_in[i], od_w_out[i], nsa_gate_b[i], nsa_pe[i], nsa_phi_w1[i],
                                nsa_phi_w2[i], moe_w_router[i], moe_b_router[i], _layer_to_bf16(moe_w_gate, i),
                                _layer_to_bf16(moe_w_up, i), _layer_to_bf16(moe_w_down, i), ln, tabs, bsz, seq_len)
    return xf.reshape(bsz, seq_len, d)
```

```python
import functools
import math

import numpy as np
import jax
import jax.numpy as jnp
from jax import lax
from jax.experimental import pallas as pl
from jax.experimental.pallas import tpu as pltpu

F32 = jnp.float32
BF16 = jnp.bfloat16
I32 = jnp.int32

LANES = 128
VMEM_LIMIT = 56 * 1024 * 1024

DEPTH = 4
HEAD_DIM = 64
ROPE_THETA = 10000.0
LN_EPS = 1e-5
DN_ALPHA = (2 * DEPTH) ** 0.25
SCALE = HEAD_DIM ** -0.5 * math.log2(math.e)
NEG = -1e30
BIG = 1e30
M_INIT = -1e30
NINF = float("-inf")

A_HEADS = 4
B_HEADS = 8
IDX_HEADS = 4
DSA_TOPK = 256
C_HEADS = 8
MOBA_BLOCK = 256
MOBA_TOPK = 3
D_HEADS = 8
NSA_CMP_LEN = 32
NSA_CMP_STRIDE = 16
NSA_SLC_BLOCK = 64
NSA_SLC_TOPK = 16
NSA_WINDOW = 512
N_EXPERTS = 8
TOP_K = 2
MOE_TM = 512
N_BISECT = 14
ROW_TILE = 1024
FFN_TILE = 512
KV_GROUPS = 2

EVEN_SIZES = (A_HEADS * 2 * HEAD_DIM, A_HEADS * 2 * HEAD_DIM, A_HEADS * 2 * HEAD_DIM, B_HEADS * HEAD_DIM,
              KV_GROUPS * HEAD_DIM, KV_GROUPS * HEAD_DIM, IDX_HEADS * HEAD_DIM, HEAD_DIM, IDX_HEADS)
ODD_SIZES = (C_HEADS * HEAD_DIM,) * 3 + (D_HEADS * HEAD_DIM,) + (KV_GROUPS * HEAD_DIM,) * 6 + (D_HEADS * 3,)


def _cparams(sem):
    return pltpu.CompilerParams(dimension_semantics=sem, vmem_limit_bytes=VMEM_LIMIT)


def _dot_nt(a, b):
    return lax.dot_general(a, b, (((1,), (1,)), ((), ())), preferred_element_type=F32)


def _layer_norm(y, g, b):
    mu = jnp.mean(y, axis=-1, keepdims=True)
    yc = y - mu
    var = jnp.mean(yc * yc, axis=-1, keepdims=True)
    return yc * lax.rsqrt(var + LN_EPS) * g + b


def _safe_recip(l):
    return jnp.where(l > 0.0, 1.0 / jnp.where(l > 0.0, l, 1.0), 0.0)


def _split_halves(t, roped=False):
    lane = lax.broadcasted_iota(I32, (1, LANES), 1)
    lo = (lane // (HEAD_DIM // 2)) % 2 == 0 if roped else lane < HEAD_DIM
    z = jnp.zeros_like(t)
    return jnp.where(lo, t, z), jnp.where(lo, z, t)


def _rope_layout(n_cols):
    q = HEAD_DIM // 2
    tile = np.concatenate([np.arange(0, q), np.arange(2 * q, 3 * q), np.arange(q, 2 * q), np.arange(3 * q, 4 * q)])
    return (np.arange(0, n_cols, LANES)[:, None] + tile[None, :]).reshape(-1).astype(np.int32)


def _rope_mm_kernel(x_ref, w_ref, cos_ref, sin_ref, o_ref):
    acc = jnp.dot(x_ref[...].astype(BF16), w_ref[...], preferred_element_type=F32)
    cos = cos_ref[...]
    sin = sin_ref[...]
    for c in range(acc.shape[1] // LANES):
        a = acc[:, c * LANES:(c + 1) * LANES]
        rot = pltpu.roll(a, LANES // 2, 1)
        o_ref[:, c * LANES:(c + 1) * LANES] = (a * cos + rot * sin).astype(o_ref.dtype)


def _proj_rope(x, w, tm, tn, seq_len, rope_tabs):
    n, d = x.shape
    p = w.shape[1]
    nt = seq_len // tm
    return pl.pallas_call(
        _rope_mm_kernel,
        grid=(n // tm, p // tn),
        in_specs=[pl.BlockSpec((tm, d), lambda i, j: (i, 0)), pl.BlockSpec((d, tn), lambda i, j: (0, j)),
                  pl.BlockSpec((tm, LANES), lambda i, j: (i % nt, 0)),
                  pl.BlockSpec((tm, LANES), lambda i, j: (i % nt, 0))],
        out_specs=pl.BlockSpec((tm, tn), lambda i, j: (i, j)),
        out_shape=jax.ShapeDtypeStruct((n, p), BF16),
        compiler_params=_cparams(("parallel", "arbitrary")),
        name="proj_rope",
    )(x, w, *rope_tabs)


def _rope_tables(seq_len):
    d = HEAD_DIM
    inv = ROPE_THETA ** (-jnp.arange(0, d, 2, dtype=F32) / d)
    ang = jnp.arange(seq_len, dtype=I32).astype(F32)[:, None] * inv[None, :]
    cos = jnp.cos(ang)
    sin = jnp.sin(ang)
    cos128 = jnp.tile(cos, (1, LANES // (d // 2)))
    sin128 = jnp.concatenate([-sin, -sin, sin, sin], axis=1)
    return cos128, sin128


def _route_top2(x, w, b):
    xh, wh = x.astype(BF16), w.astype(BF16)
    xl, wl = (x - xh.astype(F32)).astype(BF16), (w - wh.astype(F32)).astype(BF16)
    logits = (jnp.dot(xh, wh, preferred_element_type=F32) + jnp.dot(xh, wl, preferred_element_type=F32)
              + jnp.dot(xl, wh, preferred_element_type=F32)) + b
    lane = lax.broadcasted_iota(I32, (1, LANES), 1)
    lanef = lane.astype(F32)
    v = jnp.where(lane < N_EXPERTS, logits, NINF)
    l0 = jnp.max(v, axis=1, keepdims=True)
    i0 = jnp.min(jnp.where(v == l0, lanef, float(LANES)), axis=1, keepdims=True)
    v = jnp.where(lanef == i0, NINF, v)
    l1 = jnp.max(v, axis=1, keepdims=True)
    i1 = jnp.min(jnp.where(v == l1, lanef, float(LANES)), axis=1, keepdims=True)
    e1 = jnp.exp(l1 - l0)
    g0 = 1.0 / (1.0 + e1)
    g1 = e1 / (1.0 + e1)
    return jnp.where(lane == 0, i0, jnp.where(lane == 1, i1, jnp.where(lane == 2, g0, jnp.where(lane == 3, g1, 0.0))))


def _outproj_ln_kernel(x_ref, a_ref, b_ref, wa_ref, wb_ref, g_ref, bb_ref, *rest):
    mix = (jnp.dot(a_ref[...], wa_ref[...], preferred_element_type=F32)
           + jnp.dot(b_ref[...], wb_ref[...], preferred_element_type=F32))
    y = _layer_norm(DN_ALPHA * x_ref[...] + mix, g_ref[...], bb_ref[...])
    if len(rest) == 2:
        xo_ref, xb_ref = rest
    else:
        wr_ref, br_ref, xo_ref, xb_ref, rt_ref = rest
        rt_ref[...] = _route_top2(y, wr_ref[...], br_ref[...])
    xo_ref[...] = y
    xb_ref[...] = y.astype(BF16)


def _outproj_ln(x, oa, ob, wa, wb, g, b, router=None, tm=ROW_TILE):
    n, d = x.shape
    ka, kb = oa.shape[1], ob.shape[1]
    row = lambda i: (i, 0)
    fixed = lambda i: (0, 0)
    in_specs = [pl.BlockSpec((tm, d), row), pl.BlockSpec((tm, ka), row), pl.BlockSpec((tm, kb), row),
                pl.BlockSpec((ka, d), fixed), pl.BlockSpec((kb, d), fixed),
                pl.BlockSpec((1, d), fixed), pl.BlockSpec((1, d), fixed)]
    out_specs = [pl.BlockSpec((tm, d), row), pl.BlockSpec((tm, d), row)]
    out_shape = [jax.ShapeDtypeStruct((n, d), F32), jax.ShapeDtypeStruct((n, d), BF16)]
    args = [x, oa, ob, wa, wb, g.reshape(1, d), b.reshape(1, d)]
    if router is not None:
        in_specs += [pl.BlockSpec((d, LANES), fixed), pl.BlockSpec((1, LANES), fixed)]
        out_specs.append(pl.BlockSpec((tm, LANES), row))
        out_shape.append(jax.ShapeDtypeStruct((n, LANES), F32))
        args += list(router)
    return pl.pallas_call(
        _outproj_ln_kernel,
        grid=(n // tm,),
        in_specs=in_specs,
        out_specs=out_specs,
        out_shape=out_shape,
        compiler_params=_cparams(("parallel",)),
        name="outproj_ln",
    )(*args)


def _ffn_ln_kernel(x_ref, xb_ref, wg_ref, wu_ref, wd_ref, g_ref, b_ref, xo_ref, xob_ref):
    xb = xb_ref[...]
    h = jax.nn.silu(jnp.dot(xb, wg_ref[...], preferred_element_type=F32)) * jnp.dot(
        xb, wu_ref[...], preferred_element_type=F32)
    ffn = jnp.dot(h.astype(BF16), wd_ref[...], preferred_element_type=F32)
    y = _layer_norm(DN_ALPHA * x_ref[...] + ffn, g_ref[...], b_ref[...])
    xo_ref[...] = y
    xob_ref[...] = y.astype(BF16)


def _ffn_ln(x, xb, wg, wu, wd, g, b, tm=FFN_TILE):
    n, d = x.shape
    fdim = wg.shape[1]
    row = lambda i: (i, 0)
    fixed = lambda i: (0, 0)
    once = pl.Buffered(1)
    return pl.pallas_call(
        _ffn_ln_kernel,
        grid=(n // tm,),
        in_specs=[pl.BlockSpec((tm, d), row), pl.BlockSpec((tm, d), row),
                  pl.BlockSpec((d, fdim), fixed, pipeline_mode=once),
                  pl.BlockSpec((d, fdim), fixed, pipeline_mode=once),
                  pl.BlockSpec((fdim, d), fixed, pipeline_mode=once),
                  pl.BlockSpec((1, d), fixed), pl.BlockSpec((1, d), fixed)],
        out_specs=[pl.BlockSpec((tm, d), row), pl.BlockSpec((tm, d), row)],
        out_shape=[jax.ShapeDtypeStruct((n, d), F32), jax.ShapeDtypeStruct((n, d), BF16)],
        compiler_params=_cparams(("parallel",)),
        name="ffn_ln",
    )(x, xb, wg, wu, wd, g.reshape(1, d), b.reshape(1, d))


def _top_n_mask(v, n, axis):
    idx = lax.broadcasted_iota(I32, v.shape, axis).astype(F32)
    sel = jnp.zeros(v.shape, F32)
    for _ in range(n):
        mx = jnp.max(v, axis=axis, keepdims=True)
        first = jnp.min(jnp.where(v == mx, idx, float(v.shape[axis])), axis=axis, keepdims=True)
        pick = idx == first
        sel = jnp.where(pick, 1.0, sel)
        v = jnp.where(pick, NINF, v)
    return sel


DV_PAD = 16
TQ = 256


def _with_ones(vt):
    return jnp.concatenate([vt, jnp.ones((DV_PAD, vt.shape[1]), vt.dtype)], axis=0)


def _mm_t_kernel(x_ref, wv_ref, ws_ref, *rest, tk):
    x = x_ref[...].astype(BF16)
    vt_ref, small_ref = rest[-2:] if len(rest) == 2 else rest[1:3]
    acc = jnp.dot(x, wv_ref[...], preferred_element_type=F32)
    for cc in range(acc.shape[0] // tk):
        vt_ref[cc] = acc[cc * tk:(cc + 1) * tk, :].T.astype(vt_ref.dtype)
    small_ref[...] = jnp.dot(x, ws_ref[...], preferred_element_type=F32)
    if len(rest) == 4:
        rest[3][...] = jnp.dot(x, rest[0][...], preferred_element_type=F32).astype(rest[3].dtype)


def _proj_values(x, w_val, w_small, w_plain, tm, tk):
    n, d = x.shape
    pv, ps = w_val.shape[1], w_small.shape[1]
    row = lambda i: (i, 0)
    fixed = lambda i: (0, 0)
    in_specs = [pl.BlockSpec((tm, d), row), pl.BlockSpec((d, pv), fixed), pl.BlockSpec((d, ps), fixed)]
    out_specs = [pl.BlockSpec((tm // tk, pv, tk), lambda i: (i, 0, 0)), pl.BlockSpec((tm, ps), row)]
    out_shape = [jax.ShapeDtypeStruct((n // tk, pv, tk), BF16), jax.ShapeDtypeStruct((n, ps), F32)]
    args = [x, w_val, w_small]
    if w_plain is not None:
        pp = w_plain.shape[1]
        in_specs.append(pl.BlockSpec((d, pp), fixed))
        out_specs.append(pl.BlockSpec((tm, pp), row))
        out_shape.append(jax.ShapeDtypeStruct((n, pp), BF16))
        args.append(w_plain)
    return pl.pallas_call(
        functools.partial(_mm_t_kernel, tk=tk),
        grid=(n // tm,),
        in_specs=in_specs,
        out_specs=out_specs,
        out_shape=out_shape,
        compiler_params=_cparams(("parallel",)),
        name="proj_t",
    )(*args)


def _normalize_t(acc, width):
    return acc[:width] * _safe_recip(acc[width:width + 1])


def _pipe_flash(n, ns, qk, vt_at, bias_at, bufs, dv, tk, tq):
    sa, sb, pa, pb = bufs

    def softmax_into(p_ref, j, st, m):
        m_new = jnp.maximum(m, jnp.max(st, axis=0, keepdims=True))
        p_ref[j] = jnp.exp2((st - m_new).astype(BF16))
        return m_new, jnp.exp2(m - m_new)

    def half(c, carry, s_cur, s_nxt, p_prev, p_cur):
        if s_nxt is not None:
            nxt = jnp.minimum(c + 1, n - 1)
            for j in range(ns):
                s_nxt[j] = qk(nxt, j)
        out = []
        for j in range(ns):
            m, acc, alpha = carry[j]
            acc = alpha * acc + jnp.dot(vt_at(c - 1, j), p_prev[j], preferred_element_type=F32)
            m, alpha = softmax_into(p_cur, j, s_cur[j] + bias_at(c, j), m)
            out.append((m, acc, alpha))
        return tuple(out)

    for j in range(ns):
        sa[j] = qk(0, j)
    first = []
    for j in range(ns):
        sb[j] = qk(jnp.minimum(1, n - 1), j)
        m, alpha = softmax_into(pa, j, sa[j] + bias_at(0, j), jnp.full((1, tq), M_INIT, F32))
        first.append((m, jnp.zeros((dv, tq), F32), alpha))

    def body(t, carry):
        carry = half(2 * t + 1, carry, sb, sa, pa, pb)
        return half(2 * t + 2, carry, sa, sb, pb, pa)

    carry = lax.fori_loop(0, (n - 1) // 2, body, tuple(first))

    def flush(carry, p_last):
        return tuple(alpha * acc + jnp.dot(vt_at(n - 1, j), p_last[j], preferred_element_type=F32)
                     for j, (_, acc, alpha) in enumerate(carry))

    def odd_tail(carry):
        return flush(half(n - 1, carry, sb, None, pa, pb), pb)

    return lax.cond(n % 2 == 0, odd_tail, lambda carry: flush(carry, pa), carry)


def _pipe_scratch(ns, tk, tq):
    return [pltpu.VMEM((ns, tk, tq), F32)] * 2 + [pltpu.VMEM((ns, tk, tq), BF16)] * 2


def _causal_t(t):
    return jnp.where(lax.broadcasted_iota(I32, (t, t), 0) <= lax.broadcasted_iota(I32, (t, t), 1), 0.0, NINF)


def _diff_kernel(lam_ref, sub_ref, q_ref, k_ref, vt_ref, o_ref, *bufs, tq, lam_init):
    i = pl.program_id(2)
    lp = lam_ref[...]
    lam = (jnp.exp(jnp.sum(lp[0:1] * lp[1:2], axis=1, keepdims=True))
           - jnp.exp(jnp.sum(lp[2:3] * lp[3:4], axis=1, keepdims=True)) + lam_init)
    qs = _split_halves(q_ref[...], roped=True)
    n = i + 1
    bufs, tab_ref = bufs[:4], bufs[4]
    tab_ref[0] = jnp.zeros((tq, tq), F32)
    tab_ref[1] = _causal_t(tq)

    def qk(c, j):
        off = pl.multiple_of(c * tq, tq)
        return _dot_nt(k_ref[pl.ds(off, tq), :], qs[j])

    def bias_at(c, j):
        return tab_ref[jnp.where(c == i, 1, 0)]

    outs = _pipe_flash(n, 2, qk, lambda c, j: _with_ones(vt_ref[c]), bias_at, bufs, LANES + DV_PAD, tq, tq)
    o = _normalize_t(outs[0], LANES) - lam * _normalize_t(outs[1], LANES)
    o = o * lax.rsqrt(jnp.mean(o * o, axis=0, keepdims=True) + LN_EPS)
    o = o * sub_ref[...] * (1.0 - lam_init)
    o_ref[...] = o.T.astype(o_ref.dtype)


def _diff_attention(rp, vt, lam_params, subln, lam_init, tq=TQ):
    b, s, _ = rp.shape
    nk = s // tq
    return pl.pallas_call(
        functools.partial(_diff_kernel, tq=tq, lam_init=lam_init),
        grid=(b, A_HEADS, nk),
        in_specs=[pl.BlockSpec((4, HEAD_DIM), lambda bi, h, i: (0, 0)),
                  pl.BlockSpec((LANES, 1), lambda bi, h, i: (0, 0)),
                  pl.BlockSpec((None, tq, LANES), lambda bi, h, i: (bi, i, h)),
                  pl.BlockSpec((None, s, LANES), lambda bi, h, i: (bi, 0, A_HEADS + h)),
                  pl.BlockSpec((None, nk, LANES, tq), lambda bi, h, i: (bi, 0, h, 0))],
        out_specs=pl.BlockSpec((None, tq, LANES), lambda bi, h, i: (bi, i, h)),
        out_shape=jax.ShapeDtypeStruct((b, s, A_HEADS * LANES), BF16),
        scratch_shapes=_pipe_scratch(2, tq, tq) + [pltpu.VMEM((2, tq, tq), F32)],
        compiler_params=_cparams(("parallel", "parallel", "arbitrary")),
        name="diff_attn",
    )(lam_params, subln.reshape(LANES, 1), rp, rp, vt)


def _fold8(x, op):
    acc = x[0:8]
    for r in range(1, x.shape[0] // 8):
        acc = op(acc, x[r * 8:(r + 1) * 8])
    return acc


def _dsa_kernel(iq_ref, ikk_ref, iw_ref, q_ref, k_ref, vt_ref, o_ref, s_ref, j_ref, *bufs, tq, ksel, seq_len):
    tk = tq
    i = pl.program_id(1)
    nch = i + 1
    ksel_f = float(ksel)
    k_loc = lax.broadcasted_iota(I32, (tk, tq), 0)
    q_loc = lax.broadcasted_iota(I32, (tk, tq), 1)
    qpos = i * tq + lax.broadcasted_iota(I32, (1, tq), 1)

    iq = iq_ref[...]
    iwt = iw_ref[...].T
    iqh = []
    for pair in range(IDX_HEADS // 2):
        iqh += list(_split_halves(iq[:, pair * LANES:(pair + 1) * LANES], roped=True))

    def scores(c):
        off = pl.multiple_of(c * tk, tk)
        kk = ikk_ref[pl.ds(off, tk), :]
        sc = iwt[0:1] * jnp.maximum(_dot_nt(kk, iqh[0]), 0.0)
        for h in range(1, IDX_HEADS):
            sc = sc + iwt[h:h + 1] * jnp.maximum(_dot_nt(kk, iqh[h]), 0.0)
        return sc

    def full_body(c, carry):
        mx, mn = carry
        sc = scores(c)
        s_ref[c] = sc
        return jnp.maximum(mx, _fold8(sc, jnp.maximum)), jnp.minimum(mn, _fold8(sc, jnp.minimum))

    mx, mn = lax.fori_loop(0, i, full_body, (jnp.full((8, tq), -BIG, F32), jnp.full((8, tq), BIG, F32)))
    sc = scores(i)
    causal = k_loc <= q_loc
    s_ref[i] = jnp.where(causal, sc, NEG)
    mx = jnp.maximum(mx, _fold8(jnp.where(causal, sc, -BIG), jnp.maximum))
    mn = jnp.minimum(mn, _fold8(jnp.where(causal, sc, BIG), jnp.minimum))
    smax = jnp.max(mx, axis=0, keepdims=True)
    smin = jnp.min(mn, axis=0, keepdims=True)

    def count_where(ind):
        def body(c, acc):
            return acc + _fold8(ind(s_ref[c], c * tk + k_loc), jnp.add)
        acc = lax.fori_loop(0, nch, body, jnp.zeros((8, tq), F32))
        return jnp.sum(acc, axis=0, keepdims=True)

    def count_ge(th):
        return count_where(lambda x, kidx: jnp.where(x >= th, 1.0, 0.0))

    def max_below(th):
        def body(c, acc):
            x = s_ref[c]
            return jnp.maximum(acc, _fold8(jnp.where(x < th, x, NINF), jnp.maximum))
        acc = lax.fori_loop(0, nch, body, jnp.full((8, tq), NINF, F32))
        return jnp.max(acc, axis=0, keepdims=True)

    n_causal = (qpos + 1).astype(F32)
    take_all = n_causal <= ksel_f
    done0 = jnp.where(take_all, 1.0, 0.0)
    hi0 = smax + (jnp.abs(smax) * 2.0 ** -20 + 1e-30)

    def bisect(lo, hi):
        mid = lo + (hi - lo) * 0.5
        ge = count_ge(mid) >= ksel_f
        return jnp.where(ge, mid, lo), jnp.where(ge, hi, mid)

    lo, hi = lax.fori_loop(0, N_BISECT, lambda _, c: bisect(*c), (smin, hi0))

    def snap_body(carry):
        lo, hi, th, c_th, done, _ = carry
        lo, hi = bisect(lo, hi)
        t1 = max_below(hi)
        c1 = count_ge(t1)
        ok = c1 >= ksel_f
        th = jnp.where(done > 0.0, th, t1)
        c_th = jnp.where(done > 0.0, c_th, c1)
        hi = jnp.where(ok, hi, t1)
        done = jnp.where(ok, 1.0, done)
        return lo, hi, th, c_th, done, jnp.sum(1.0 - done)

    _, _, th, c_ge, _, _ = lax.while_loop(lambda c: c[5] > 0.0, snap_body,
                                          (lo, hi, smax, jnp.zeros((1, tq), F32), done0, jnp.sum(1.0 - done0)))

    need_tb = jnp.where(take_all, 0.0, jnp.where(c_ge > ksel_f, 1.0, 0.0))
    j_ref[...] = jnp.full((8, tq), seq_len - 1, I32)

    @pl.when(jnp.sum(need_tb) > 0.0)
    def _():
        need = ksel_f - count_where(lambda x, kidx: jnp.where(x > th, 1.0, 0.0))

        def jb(_, carry):
            lo_j, hi_j = carry
            mid = (lo_j + hi_j) // 2
            cnt = count_where(lambda x, kidx: jnp.where(x == th, jnp.where(kidx <= mid, 1.0, 0.0), 0.0))
            ge = cnt >= need
            return jnp.where(ge, lo_j, mid), jnp.where(ge, mid, hi_j)

        n_it = int(math.ceil(math.log2(seq_len))) + 1
        _, hi_j = lax.fori_loop(0, n_it, jb, (jnp.full((1, tq), -1, I32), jnp.full((1, tq), seq_len - 1, I32)))
        j_ref[...] = jnp.broadcast_to(hi_j, (8, tq))

    jsel = j_ref[0:1, :]

    def bias_body(c, _):
        x = s_ref[c]
        kidx = c * tk + k_loc
        keep = jnp.where(x > th, 0.0, jnp.where(x == th, jnp.where(kidx <= jsel, 0.0, NINF), NINF))
        keep = jnp.where(take_all, 0.0, keep)
        s_ref[c] = jnp.where(kidx <= qpos, keep, NINF)
        return 0

    lax.fori_loop(0, nch, bias_body, 0)

    for p in range(B_HEADS // 2):
        qs = _split_halves(q_ref[:, p * LANES:(p + 1) * LANES], roped=True)

        def qk(c, j, qs=qs):
            off = pl.multiple_of(c * tk, tk)
            return _dot_nt(k_ref[pl.ds(off, tk), :], qs[j])

        outs = _pipe_flash(nch, 2, qk, lambda c, j: _with_ones(vt_ref[c, j * HEAD_DIM:(j + 1) * HEAD_DIM, :]),
                           lambda c, j: s_ref[c], bufs, HEAD_DIM + DV_PAD, tk, tq)
        o = jnp.concatenate([_normalize_t(outs[0], HEAD_DIM), _normalize_t(outs[1], HEAD_DIM)], axis=0)
        o_ref[:, p * LANES:(p + 1) * LANES] = o.T.astype(o_ref.dtype)


def _dsa_attention(rp, vt, iw, tq=TQ):
    b, s, _ = rp.shape
    ksel = min(DSA_TOPK, s // 4)
    nk = s // tq
    return pl.pallas_call(
        functools.partial(_dsa_kernel, tq=tq, ksel=ksel, seq_len=s),
        grid=(b, nk),
        in_specs=[pl.BlockSpec((None, tq, 2 * LANES), lambda bi, i: (bi, i, 6)),
                  pl.BlockSpec((None, s, LANES), lambda bi, i: (bi, 0, 15)),
                  pl.BlockSpec((None, tq, LANES), lambda bi, i: (bi, i, 0)),
                  pl.BlockSpec((None, tq, 4 * LANES), lambda bi, i: (bi, i, 2)),
                  pl.BlockSpec((None, s, LANES), lambda bi, i: (bi, 0, 14)),
                  pl.BlockSpec((None, nk, LANES, tq), lambda bi, i: (bi, 0, 4, 0))],
        out_specs=pl.BlockSpec((None, tq, 4 * LANES), lambda bi, i: (bi, i, 0)),
        out_shape=jax.ShapeDtypeStruct((b, s, 4 * LANES), BF16),
        scratch_shapes=[pltpu.VMEM((nk, tq, tq), F32), pltpu.VMEM((8, tq), I32)] + _pipe_scratch(2, tq, tq),
        compiler_params=_cparams(("parallel", "arbitrary")),
        name="dsa_attn",
    )(rp, rp, iw, rp, rp, vt)


def _moba_kernel(q_ref, k_ref, vt_ref, o_ref, km_ref, sel_ref, *bufs, seq_len, n_sel):
    tq = MOBA_BLOCK
    qb = pl.program_id(2)

    @pl.when(qb == 0)
    def _():
        j = lax.broadcasted_iota(I32, (LANES, seq_len), 0)
        s = lax.broadcasted_iota(I32, (LANES, seq_len), 1)
        avg = jnp.where(s // MOBA_BLOCK == j, 1.0 / MOBA_BLOCK, 0.0).astype(BF16)
        km_ref[...] = jnp.dot(avg, k_ref[...], preferred_element_type=F32)

    nbp = sel_ref.shape[1]
    km = km_ref[0:nbp, :]
    qs = _split_halves(q_ref[...], roped=True)
    blk = lax.broadcasted_iota(I32, (nbp, tq), 0)
    past = blk < qb
    for j in range(2):
        gate = lax.dot_general(km, qs[j].astype(F32), (((1,), (1,)), ((), ())),
                               precision=lax.Precision.HIGHEST, preferred_element_type=F32)
        gate = jnp.where(blk < seq_len // MOBA_BLOCK, jnp.where(past, gate, NEG), NINF)
        sel = _top_n_mask(gate, n_sel, 0)
        sel_ref[j] = jnp.where(past, jnp.where(sel > 0.5, 0.0, NINF), NINF)
    own = _causal_t(tq)
    n = qb + 1

    def qk(c, j):
        off = pl.multiple_of(c * tq, tq)
        return _dot_nt(k_ref[pl.ds(off, tq), :], qs[j])

    def bias_at(c, j):
        chosen = sel_ref[j, pl.ds(c, 1), :]
        return jnp.where(c == qb, own, chosen)

    outs = _pipe_flash(n, 2, qk, lambda c, j: _with_ones(vt_ref[c, j * HEAD_DIM:(j + 1) * HEAD_DIM, :]), bias_at, bufs,
                       HEAD_DIM + DV_PAD, tq, tq)
    o = jnp.concatenate([_normalize_t(outs[0], HEAD_DIM), _normalize_t(outs[1], HEAD_DIM)], axis=0)
    o_ref[...] = o.T.astype(o_ref.dtype)


def _moba_attention(rp, vt):
    b, s, _ = rp.shape
    tq = MOBA_BLOCK
    nb = s // tq
    n_sel = max(1, min(MOBA_TOPK, nb - 1))
    npair = C_HEADS // 2
    return pl.pallas_call(
        functools.partial(_moba_kernel, seq_len=s, n_sel=n_sel),
        grid=(b, npair, nb),
        in_specs=[pl.BlockSpec((None, tq, LANES), lambda bi, h, i: (bi, i, h)),
                  pl.BlockSpec((None, s, LANES), lambda bi, h, i: (bi, 0, npair + h)),
                  pl.BlockSpec((None, nb, LANES, tq), lambda bi, h, i: (bi, 0, h, 0))],
        out_specs=pl.BlockSpec((None, tq, LANES), lambda bi, h, i: (bi, i, h)),
        out_shape=jax.ShapeDtypeStruct((b, s, npair * LANES), BF16),
        scratch_shapes=[pltpu.VMEM((LANES, LANES), F32), pltpu.VMEM((2, -(-nb // 8) * 8, tq), F32)]
        + _pipe_scratch(2, tq, tq),
        compiler_params=_cparams(("parallel", "parallel", "arbitrary")),
        name="moba_attn",
    )(rp, rp, vt)


def _cmp_kernel(r_ref, pe_ref, w1_ref, w2_ref, o_ref):
    r = r_ref[...]
    w1 = w1_ref[...]
    half = r.shape[1]
    u = jnp.dot(r, w1[:half], preferred_element_type=F32)
    v = jnp.dot(r, w1[half:], preferred_element_type=F32)
    c = jnp.dot(pe_ref[...], w1, preferred_element_type=F32)[0:1]
    pre = u + pltpu.roll(v, r.shape[0] - 1, 0) + c
    o_ref[...] = jnp.dot(jax.nn.gelu(pre).astype(BF16), w2_ref[...],
                         preferred_element_type=F32).astype(o_ref.dtype)


def _nsa_compress(r, pe, w1, w2):
    b, _, nc, wdt = r.shape
    hid = w1.shape[2]
    return pl.pallas_call(
        _cmp_kernel,
        grid=(b, 4),
        in_specs=[pl.BlockSpec((None, None, nc, wdt), lambda bi, t: (bi, t, 0, 0)),
                  pl.BlockSpec((None, 8, 2 * wdt), lambda bi, t: (t // 2, 0, 0)),
                  pl.BlockSpec((None, 2 * wdt, hid), lambda bi, t: (t // 2, 0, 0)),
                  pl.BlockSpec((None, hid, HEAD_DIM), lambda bi, t: (t // 2, 0, 0))],
        out_specs=pl.BlockSpec((None, None, nc, HEAD_DIM), lambda bi, t: (bi, t, 0, 0)),
        out_shape=jax.ShapeDtypeStruct((b, 4, nc, HEAD_DIM), BF16),
        compiler_params=_cparams(("parallel", "arbitrary")),
        name="nsa_compress",
    )(r, pe, w1, w2)


def _nsa_kernel(qr_ref, qw_ref, dg_ref, gb_ref, kc_ref, vct_ref, ks_ref, vst_ref, kw_ref, vwt_ref,
                o_ref, sel_ref, wb_ref, *bufs, tq, seq_len):
    tk = tq
    i = pl.program_id(1)
    nch = i + 1
    nc = seq_len // NSA_CMP_STRIDE
    n_sb = seq_len // NSA_SLC_BLOCK
    n_sel = min(NSA_SLC_TOPK, n_sb)
    k_loc = lax.broadcasted_iota(I32, (tk, tq), 0)
    q_loc = lax.broadcasted_iota(I32, (tk, tq), 1)
    qpos = i * tq + lax.broadcasted_iota(I32, (1, tq), 1)

    gates_t = jax.nn.sigmoid(dg_ref[...] + gb_ref[...]).T
    kc = kc_ref[...]
    cmp_end = lax.broadcasted_iota(I32, (nc, 1), 0) * NSA_CMP_STRIDE + (NSA_CMP_LEN - 1)
    cbias = jnp.where(cmp_end <= qpos, 0.0, NINF)

    nh = D_HEADS // 2
    q_rot = [_split_halves(qr_ref[:, p * LANES:(p + 1) * LANES], roped=True) for p in range(nh)]
    q_raw = [_split_halves(qw_ref[:, p * LANES:(p + 1) * LANES]) for p in range(nh)]
    cbias4 = jnp.concatenate([cbias] * nh, axis=1)
    o_cmp, psum = [], []
    for g in range(2):
        qg = jnp.concatenate([q_raw[p][g] for p in range(nh)], axis=0)
        s = _dot_nt(kc, qg) + cbias4
        m = jnp.max(s, axis=0, keepdims=True)
        e = jnp.exp2(s - jnp.where(m == NINF, 0.0, m))
        pc = e * _safe_recip(jnp.sum(e, axis=0, keepdims=True))
        psum.append(sum(pc[:, p * tq:(p + 1) * tq] for p in range(nh)))
        o_cmp.append(jnp.dot(vct_ref[g], pc.astype(BF16), preferred_element_type=F32))

    per = tk // NSA_SLC_BLOCK
    nbp = -(-n_sb // 8) * 8
    cn = lax.broadcasted_iota(I32, (nbp, nc), 1) * NSA_CMP_STRIDE
    sj = lax.broadcasted_iota(I32, (nbp, nc), 0) * NSA_SLC_BLOCK
    shares = jnp.where((cn <= sj + NSA_SLC_BLOCK - 1) & (cn + NSA_CMP_LEN - 1 >= sj), 1.0, 0.0)
    blk = lax.broadcasted_iota(I32, (nbp, tq), 0)
    cur = qpos // NSA_SLC_BLOCK
    causal_b = blk <= cur
    forced = (blk == 0) | ((blk >= cur - 1) & causal_b)
    for g in range(2):
        imp = jnp.dot(shares, psum[g], precision=lax.Precision.HIGHEST, preferred_element_type=F32)
        val = jnp.where(forced, BIG, jnp.where(causal_b, imp, NEG))
        val = jnp.where(blk < n_sb, val, NINF)
        rowb = jnp.where(_top_n_mask(val, n_sel, 0) > 0.5, 0.0, NINF)
        for c in range(seq_len // tk):
            sel_ref[g, c] = jnp.concatenate([rowb[c * per:(c + 1) * per], jnp.zeros((8 - per, tq), F32)], axis=0)

    wb_ref[0] = jnp.where(k_loc <= q_loc, 0.0, NINF)
    wb_ref[1] = jnp.zeros((tk, tq), F32)
    wb_ref[2] = jnp.where(k_loc > q_loc, 0.0, NINF)
    n_wc = NSA_WINDOW // tk + 1
    w_first = jnp.maximum(i - (n_wc - 1), 0)
    n_w = i - w_first + 1

    for p in range(D_HEADS // 2):
        qs = q_rot[p]

        def qk_s(c, j, qs=qs):
            off = pl.multiple_of(c * tk, tk)
            return _dot_nt(ks_ref[pl.ds(off, tk), :], qs[j])

        def bias_s(c, j):
            rows = sel_ref[j, c]
            tile = jnp.concatenate([jnp.broadcast_to(rows[r:r + 1], (NSA_SLC_BLOCK, tq)) for r in range(per)], axis=0)
            return tile + wb_ref[jnp.where(c == i, 0, 1)]

        o_slc = _pipe_flash(nch, 2, qk_s, lambda c, j: _with_ones(vst_ref[c, j * HEAD_DIM:(j + 1) * HEAD_DIM, :]),
                            bias_s, bufs, HEAD_DIM + DV_PAD, tk, tq)

        def qk_w(c, j, qs=qs):
            off = pl.multiple_of((w_first + c) * tk, tk)
            return _dot_nt(kw_ref[pl.ds(off, tk), :], qs[j])

        def bias_w(c, j):
            return wb_ref[i - (w_first + c)]

        o_win = _pipe_flash(n_w, 2, qk_w,
                            lambda c, j: _with_ones(vwt_ref[w_first + c, j * HEAD_DIM:(j + 1) * HEAD_DIM, :]), bias_w, bufs,
                            HEAD_DIM + DV_PAD, tk, tq)
        outs = []
        for g in range(2):
            h = g * (D_HEADS // 2) + p
            outs.append(gates_t[3 * h:3 * h + 1] * o_cmp[g][:, p * tq:(p + 1) * tq]
                        + gates_t[3 * h + 1:3 * h + 2] * _normalize_t(o_slc[g], HEAD_DIM)
                        + gates_t[3 * h + 2:3 * h + 3] * _normalize_t(o_win[g], HEAD_DIM))
        o_ref[:, p * LANES:(p + 1) * LANES] = jnp.concatenate(outs, axis=0).T.astype(o_ref.dtype)


def _nsa_attention(rp, pp, vt, dg, gate_b, kcmp, vcmp, tq=TQ):
    b, s, _ = rp.shape
    nk = s // tq
    nc = s // NSA_CMP_STRIDE
    assert NSA_WINDOW == 2 * tq
    n_wc = NSA_WINDOW // tq + 1
    vct = vcmp.reshape(b, nc, 2, HEAD_DIM).transpose(0, 2, 3, 1)
    full = lambda t: pl.BlockSpec((None, s, LANES), lambda bi, i: (bi, 0, t))
    vspec = lambda t: pl.BlockSpec((None, nk, LANES, tq), lambda bi, i: (bi, 0, t, 0))
    return pl.pallas_call(
        functools.partial(_nsa_kernel, tq=tq, seq_len=s),
        grid=(b, nk),
        in_specs=[pl.BlockSpec((None, tq, 4 * LANES), lambda bi, i: (bi, i, 2)),
                  pl.BlockSpec((None, tq, 4 * LANES), lambda bi, i: (bi, i, 0)),
                  pl.BlockSpec((None, tq, LANES), lambda bi, i: (bi, i, 0)),
                  pl.BlockSpec((1, LANES), lambda bi, i: (0, 0)),
                  pl.BlockSpec((None, nc, LANES), lambda bi, i: (bi, 0, 0)),
                  pl.BlockSpec((None, 2, HEAD_DIM, nc), lambda bi, i: (bi, 0, 0, 0)),
                  full(12), vspec(4), full(13), vspec(5)],
        out_specs=pl.BlockSpec((None, tq, 4 * LANES), lambda bi, i: (bi, i, 0)),
        out_shape=jax.ShapeDtypeStruct((b, s, 4 * LANES), BF16),
        scratch_shapes=[pltpu.VMEM((2, nk, 8, tq), F32), pltpu.VMEM((n_wc, tq, tq), F32)]
        + _pipe_scratch(2, tq, tq),
        compiler_params=_cparams(("parallel", "arbitrary")),
        name="nsa_attn",
    )(rp, pp, dg, gate_b, kcmp, vct, rp, vt, rp, vt)


def _cast_kernel(x_ref, o_ref):
    o_ref[...] = x_ref[...].astype(o_ref.dtype)


def _layer_to_bf16(w, layer):
    _, e, r, c = w.shape
    tr = 1 << ((2 ** 21 // c).bit_length() - 1)
    out = pl.pallas_call(
        _cast_kernel,
        grid=(e * r // tr,),
        in_specs=[pl.BlockSpec((None, tr, c), lambda i: (layer, i, 0))],
        out_specs=pl.BlockSpec((tr, c), lambda i: (i, 0)),
        out_shape=jax.ShapeDtypeStruct((e * r, c), BF16),
        compiler_params=_cparams(("parallel",)),
        name="cast_bf16",
    )(w.reshape(w.shape[0], e * r, c))
    return out.reshape(e, r, c)


def _moe_ffn_kernel(be_ref, nu_ref, x_ref, wg_ref, wu_ref, wd_ref, o_ref):
    used = pl.program_id(0) < nu_ref[0]

    @pl.when(used)
    def _():
        x = x_ref[...]
        h = jax.nn.silu(jnp.dot(x, wg_ref[...], preferred_element_type=F32)) * jnp.dot(
            x, wu_ref[...], preferred_element_type=F32)
        o_ref[...] = jnp.dot(h.astype(BF16), wd_ref[...], preferred_element_type=F32).astype(o_ref.dtype)

    @pl.when(jnp.logical_not(used))
    def _():
        o_ref[...] = jnp.zeros(o_ref.shape, o_ref.dtype)


def _moe_ffn(x_sorted, block_e, n_used, wg, wu, wd):
    ns, d = x_sorted.shape
    fdim = wg.shape[2]
    once = pl.Buffered(1)
    grid_spec = pltpu.PrefetchScalarGridSpec(
        num_scalar_prefetch=2,
        grid=(ns // MOE_TM,),
        in_specs=[pl.BlockSpec((MOE_TM, d), lambda i, be, nu: (jnp.minimum(i, nu[0] - 1), 0)),
                  pl.BlockSpec((None, d, fdim), lambda i, be, nu: (be[i], 0, 0), pipeline_mode=once),
                  pl.BlockSpec((None, d, fdim), lambda i, be, nu: (be[i], 0, 0), pipeline_mode=once),
                  pl.BlockSpec((None, fdim, d), lambda i, be, nu: (be[i], 0, 0), pipeline_mode=once)],
        out_specs=pl.BlockSpec((MOE_TM, d), lambda i, be, nu: (i, 0)),
    )
    return pl.pallas_call(
        _moe_ffn_kernel,
        grid_spec=grid_spec,
        out_shape=jax.ShapeDtypeStruct((ns, d), BF16),
        compiler_params=_cparams(("arbitrary",)),
        name="moe_ffn",
    )(block_e, n_used, x_sorted, wg, wu, wd)


def _combine_ln_kernel(x_ref, y_ref, rt_ref, g_ref, b_ref, xo_ref, xb_ref):
    rt = rt_ref[...]
    d = x_ref.shape[1]
    ffn = rt[:, 2:3] * y_ref[:, :d].astype(F32) + rt[:, 3:4] * y_ref[:, d:].astype(F32)
    y = _layer_norm(DN_ALPHA * x_ref[...] + ffn, g_ref[...], b_ref[...])
    xo_ref[...] = y
    xb_ref[...] = y.astype(BF16)


def _combine_ln(x, y2, rt, g, b, tm=ROW_TILE):
    n, d = x.shape
    row = lambda i: (i, 0)
    fixed = lambda i: (0, 0)
    return pl.pallas_call(
        _combine_ln_kernel,
        grid=(n // tm,),
        in_specs=[pl.BlockSpec((tm, d), row), pl.BlockSpec((tm, TOP_K * d), row),
                  pl.BlockSpec((tm, LANES), row), pl.BlockSpec((1, d), fixed), pl.BlockSpec((1, d), fixed)],
        out_specs=[pl.BlockSpec((tm, d), row), pl.BlockSpec((tm, d), row)],
        out_shape=[jax.ShapeDtypeStruct((n, d), F32), jax.ShapeDtypeStruct((n, d), BF16)],
        compiler_params=_cparams(("parallel",)),
        name="moe_combine_ln",
    )(x, y2, rt, g.reshape(1, d), b.reshape(1, d))


def _moe_layout(rt, n):
    e_flat = rt[:, 0:TOP_K].astype(I32).reshape(-1)
    nk = n * TOP_K
    onehot = (e_flat[:, None] == jnp.arange(N_EXPERTS, dtype=I32)[None, :]).astype(I32)
    rank = jnp.take_along_axis(jnp.cumsum(onehot, axis=0), e_flat[:, None], axis=1)[:, 0] - 1
    counts = jnp.sum(onehot, axis=0)
    padded = (counts + MOE_TM - 1) // MOE_TM * MOE_TM
    pad_end = jnp.cumsum(padded)
    pad_start = pad_end - padded
    grp_start = jnp.cumsum(counts) - counts
    slot = pad_start[e_flat] + rank
    n_blocks = -(-nk // MOE_TM) + N_EXPERTS
    n_slots = n_blocks * MOE_TM
    order = jnp.argsort(e_flat, stable=True).astype(I32)
    sl = jnp.arange(n_slots, dtype=I32)
    slot_e = jnp.minimum(jnp.searchsorted(pad_end, sl, side='right'), N_EXPERTS - 1).astype(I32)
    within = sl - pad_start[slot_e]
    valid = within < counts[slot_e]
    src = jnp.where(valid, grp_start[slot_e] + within, 0)
    slot_tok = jnp.where(valid, order[src] // TOP_K, 0)
    n_used = (pad_end[-1] // MOE_TM).astype(I32).reshape(1)
    blk = jnp.arange(n_blocks, dtype=I32)
    block_e = slot_e[jnp.minimum(blk, n_used[0] - 1) * MOE_TM]
    return slot_tok, slot.reshape(n, TOP_K), block_e, n_used


def _pair_perm(n_heads):
    half = n_heads // 2
    cols = []
    for p in range(half):
        cols += list(range(p * HEAD_DIM, (p + 1) * HEAD_DIM))
        cols += list(range((half + p) * HEAD_DIM, (half + p + 1) * HEAD_DIM))
    return np.asarray(cols, dtype=np.int32)


def _pad_cols(w, width):
    return jnp.pad(w, ((0, 0), (0, width - w.shape[1])))


def _split_cols(w, sizes):
    out, off = [], 0
    for sz in sizes:
        out.append(w[:, off:off + sz])
        off += sz
    return out


def _even_layer(x, xb, w_in, w_out, lam_params, subln, lam_init, wg, wu, wd, ln, tabs, bsz, seq_len):
    n = x.shape[0]
    perm = _pair_perm(B_HEADS)
    aq, ak, av, bq, bk, bv, iq, ik, iw = _split_cols(w_in, EVEN_SIZES)
    w_rope = jnp.concatenate([aq * SCALE, ak, bq[:, perm] * SCALE, iq, bk, ik, ik], axis=1)
    w_rope = w_rope[:, _rope_layout(w_rope.shape[1])].astype(BF16)
    w_val = jnp.concatenate([av, bv], axis=1).astype(BF16)
    w_iw = _pad_cols(iw, LANES).astype(BF16)
    rp = _proj_rope(xb, w_rope, ROW_TILE, w_rope.shape[1], seq_len, tabs).reshape(bsz, seq_len, -1)
    vt, iwv = _proj_values(xb, w_val, w_iw, None, ROW_TILE, TQ)
    vt = vt.reshape(bsz, seq_len // TQ, -1, TQ)
    iwv = iwv.reshape(bsz, seq_len, LANES)
    o_a = _diff_attention(rp, vt, lam_params, subln, lam_init)
    o_b = _dsa_attention(rp, vt, iwv)
    half = w_out.shape[0] // 2
    wo_a = w_out[:half].astype(BF16)
    wo_b = w_out[half:][perm].astype(BF16)
    g_mix, b_mix, g_ffn, b_ffn = ln
    x1, x1b = _outproj_ln(x, o_a.reshape(n, -1), o_b.reshape(n, -1), wo_a, wo_b, g_mix, b_mix)
    return _ffn_ln(x1, x1b, wg.astype(BF16), wu.astype(BF16), wd.astype(BF16), g_ffn, b_ffn)


def _odd_layer(x, xb, w_in, w_out, gate_b, pe, phi_w1, phi_w2, w_router, b_router, wg, wu, wd, ln, tabs,
               bsz, seq_len):
    n = x.shape[0]
    perm = _pair_perm(D_HEADS)
    cq, ck, cv, dq, dkc, dvc, dks, dvs, dkw, dvw, dg = _split_cols(w_in, ODD_SIZES)
    dq = dq[:, perm] * SCALE
    w_rope = jnp.concatenate([cq * SCALE, ck, dq, dks, dkw], axis=1)
    w_rope = w_rope[:, _rope_layout(w_rope.shape[1])].astype(BF16)
    w_plain = jnp.concatenate([dq, dkc, dvc], axis=1).astype(BF16)
    w_val = jnp.concatenate([cv, dvs, dvw], axis=1).astype(BF16)
    w_dg = _pad_cols(dg, LANES).astype(BF16)
    rp = _proj_rope(xb, w_rope, ROW_TILE, w_rope.shape[1], seq_len, tabs).reshape(bsz, seq_len, -1)
    vt, dgv, pp = _proj_values(xb, w_val, w_dg, w_plain, ROW_TILE, TQ)
    vt = vt.reshape(bsz, seq_len // TQ, -1, TQ)
    dgv = dgv.reshape(bsz, seq_len, LANES)
    pp = pp.reshape(bsz, seq_len, -1)

    o_c = _moba_attention(rp, vt)

    nc = seq_len // NSA_CMP_STRIDE
    tok = pp[:, :, 4 * LANES:6 * LANES].reshape(bsz, nc, NSA_CMP_STRIDE, 4, HEAD_DIM)
    r = tok.transpose(0, 3, 1, 2, 4).reshape(bsz, 4, nc, NSA_CMP_STRIDE * HEAD_DIM)
    pe_flat = jnp.pad(pe.reshape(2, 1, -1), ((0, 0), (0, 7), (0, 0))).astype(BF16)
    cmp = _nsa_compress(r, pe_flat, phi_w1.astype(BF16), phi_w2.astype(BF16))
    kcmp = jnp.concatenate([cmp[:, 0], cmp[:, 1]], axis=-1)
    vcmp = jnp.concatenate([cmp[:, 2], cmp[:, 3]], axis=-1)
    gb = _pad_cols(gate_b.reshape(1, -1), LANES)
    o_d = _nsa_attention(rp, pp, vt, dgv, gb, kcmp, vcmp)

    half = w_out.shape[0] // 2
    wo_c = w_out[:half].astype(BF16)
    wo_d = w_out[half:][perm].astype(BF16)
    g_mix, b_mix, g_ffn, b_ffn = ln
    router = (_pad_cols(w_router, LANES), _pad_cols(b_router.reshape(1, -1), LANES))
    x1, x1b, rt = _outproj_ln(x, o_c.reshape(n, -1), o_d.reshape(n, -1), wo_c, wo_d, g_mix, b_mix, router)

    slot_tok, slot, block_e, n_used = _moe_layout(rt, n)
    y_slots = _moe_ffn(x1b[slot_tok], block_e, n_used, wg, wu, wd)
    y2 = y_slots[slot.reshape(-1)].reshape(n, -1)
    return _combine_ln(x1, y2, rt, g_ffn, b_ffn)


@jax.jit
def kernel(x, ev_w_in, ev_w_out, dif_lambda, dif_subln, ffd_w_gate, ffd_w_up, ffd_w_down, od_w_in, od_w_out,
           nsa_gate_b, nsa_pe, nsa_phi_w1, nsa_phi_w2, moe_w_router, moe_b_router, moe_w_gate, moe_w_up,
           moe_w_down, ln_mix_g, ln_mix_b, ln_ffn_g, ln_ffn_b):
    bsz, seq_len, d = x.shape
    tabs = _rope_tables(seq_len)
    xf = x.reshape(bsz * seq_len, d)
    xb = xf.astype(BF16)
    for l in range(DEPTH):
        i = l // 2
        ln = (ln_mix_g[l], ln_mix_b[l], ln_ffn_g[l], ln_ffn_b[l])
        if l % 2 == 0:
            lam_init = 0.8 - 0.6 * math.exp(-0.3 * l)
            xf, xb = _even_layer(xf, xb, ev_w_in[i], ev_w_out[i], dif_lambda[i], dif_subln[i], lam_init,
                                 ffd_w_gate[i], ffd_w_up[i], ffd_w_down[i], ln, tabs, bsz, seq_len)
        else:
            xf, xb = _odd_layer(xf, xb, od_w_in[i], od_w_out[i], nsa_gate_b[i], nsa_pe[i], nsa_phi_w1[i],
                                nsa_phi_w2[i], moe_w_router[i], moe_b_router[i], _layer_to_bf16(moe_w_gate, i),
                                _layer_to_bf16(moe_w_up, i), _layer_to_bf16(moe_w_down, i), ln, tabs, bsz, seq_len)
    return xf.reshape(bsz, seq_len, d)
```

```python
import functools
import math

import numpy as np
import jax
import jax.numpy as jnp
from jax import lax
from jax.experimental import pallas as pl
from jax.experimental.pallas import tpu as pltpu

F32 = jnp.float32
BF16 = jnp.bfloat16
I32 = jnp.int32

LANES = 128
VMEM_LIMIT = 56 * 1024 * 1024

DEPTH = 4
HEAD_DIM = 64
ROPE_THETA = 10000.0
LN_EPS = 1e-5
DN_ALPHA = (2 * DEPTH) ** 0.25
SCALE = HEAD_DIM ** -0.5 * math.log2(math.e)
NEG = -1e30
BIG = 1e30
M_INIT = -1e30
NINF = float("-inf")

A_HEADS = 4
B_HEADS = 8
IDX_HEADS = 4
DSA_TOPK = 256
C_HEADS = 8
MOBA_BLOCK = 256
MOBA_TOPK = 3
D_HEADS = 8
NSA_CMP_LEN = 32
NSA_CMP_STRIDE = 16
NSA_SLC_BLOCK = 64
NSA_SLC_TOPK = 16
NSA_WINDOW = 512
N_EXPERTS = 8
TOP_K = 2
MOE_TM = 512
N_BISECT = 14
ROW_TILE = 1024
FFN_TILE = 512
KV_GROUPS = 2

EVEN_SIZES = (A_HEADS * 2 * HEAD_DIM, A_HEADS * 2 * HEAD_DIM, A_HEADS * 2 * HEAD_DIM, B_HEADS * HEAD_DIM,
              KV_GROUPS * HEAD_DIM, KV_GROUPS * HEAD_DIM, IDX_HEADS * HEAD_DIM, HEAD_DIM, IDX_HEADS)
ODD_SIZES = (C_HEADS * HEAD_DIM,) * 3 + (D_HEADS * HEAD_DIM,) + (KV_GROUPS * HEAD_DIM,) * 6 + (D_HEADS * 3,)


def _cparams(sem):
    return pltpu.CompilerParams(dimension_semantics=sem, vmem_limit_bytes=VMEM_LIMIT)


def _dot_nt(a, b):
    return lax.dot_general(a, b, (((1,), (1,)), ((), ())), preferred_element_type=F32)


def _layer_norm(y, g, b):
    mu = jnp.mean(y, axis=-1, keepdims=True)
    yc = y - mu
    var = jnp.mean(yc * yc, axis=-1, keepdims=True)
    return yc * lax.rsqrt(var + LN_EPS) * g + b


def _safe_recip(l):
    return jnp.where(l > 0.0, 1.0 / jnp.where(l > 0.0, l, 1.0), 0.0)


def _split_halves(t, roped=False):
    lane = lax.broadcasted_iota(I32, (1, LANES), 1)
    lo = (lane // (HEAD_DIM // 2)) % 2 == 0 if roped else lane < HEAD_DIM
    z = jnp.zeros_like(t)
    return jnp.where(lo, t, z), jnp.where(lo, z, t)


def _rope_layout(n_cols):
    q = HEAD_DIM // 2
    tile = np.concatenate([np.arange(0, q), np.arange(2 * q, 3 * q), np.arange(q, 2 * q), np.arange(3 * q, 4 * q)])
    return (np.arange(0, n_cols, LANES)[:, None] + tile[None, :]).reshape(-1).astype(np.int32)


def _rope_mm_kernel(x_ref, w_ref, cos_ref, sin_ref, o_ref):
    acc = jnp.dot(x_ref[...].astype(BF16), w_ref[...], preferred_element_type=F32)
    cos = cos_ref[...]
    sin = sin_ref[...]
    for c in range(acc.shape[1] // LANES):
        a = acc[:, c * LANES:(c + 1) * LANES]
        rot = pltpu.roll(a, LANES // 2, 1)
        o_ref[:, c * LANES:(c + 1) * LANES] = (a * cos + rot * sin).astype(o_ref.dtype)


def _proj_rope(x, w, tm, tn, seq_len, rope_tabs):
    n, d = x.shape
    p = w.shape[1]
    nt = seq_len // tm
    return pl.pallas_call(
        _rope_mm_kernel,
        grid=(n // tm, p // tn),
        in_specs=[pl.BlockSpec((tm, d), lambda i, j: (i, 0)), pl.BlockSpec((d, tn), lambda i, j: (0, j)),
                  pl.BlockSpec((tm, LANES), lambda i, j: (i % nt, 0)),
                  pl.BlockSpec((tm, LANES), lambda i, j: (i % nt, 0))],
        out_specs=pl.BlockSpec((tm, tn), lambda i, j: (i, j)),
        out_shape=jax.ShapeDtypeStruct((n, p), BF16),
        compiler_params=_cparams(("parallel", "arbitrary")),
        name="proj_rope",
    )(x, w, *rope_tabs)


def _rope_tables(seq_len):
    d = HEAD_DIM
    inv = ROPE_THETA ** (-jnp.arange(0, d, 2, dtype=F32) / d)
    ang = jnp.arange(seq_len, dtype=I32).astype(F32)[:, None] * inv[None, :]
    cos = jnp.cos(ang)
    sin = jnp.sin(ang)
    cos128 = jnp.tile(cos, (1, LANES // (d // 2)))
    sin128 = jnp.concatenate([-sin, -sin, sin, sin], axis=1)
    return cos128, sin128


def _route_top2(x, w, b):
    xh, wh = x.astype(BF16), w.astype(BF16)
    xl, wl = (x - xh.astype(F32)).astype(BF16), (w - wh.astype(F32)).astype(BF16)
    logits = (jnp.dot(xh, wh, preferred_element_type=F32) + jnp.dot(xh, wl, preferred_element_type=F32)
              + jnp.dot(xl, wh, preferred_element_type=F32)) + b
    lane = lax.broadcasted_iota(I32, (1, LANES), 1)
    lanef = lane.astype(F32)
    v = jnp.where(lane < N_EXPERTS, logits, NINF)
    l0 = jnp.max(v, axis=1, keepdims=True)
    i0 = jnp.min(jnp.where(v == l0, lanef, float(LANES)), axis=1, keepdims=True)
    v = jnp.where(lanef == i0, NINF, v)
    l1 = jnp.max(v, axis=1, keepdims=True)
    i1 = jnp.min(jnp.where(v == l1, lanef, float(LANES)), axis=1, keepdims=True)
    e1 = jnp.exp(l1 - l0)
    g0 = 1.0 / (1.0 + e1)
    g1 = e1 / (1.0 + e1)
    return jnp.where(lane == 0, i0, jnp.where(lane == 1, i1, jnp.where(lane == 2, g0, jnp.where(lane == 3, g1, 0.0))))


def _outproj_ln_kernel(x_ref, a_ref, b_ref, wa_ref, wb_ref, g_ref, bb_ref, *rest):
    mix = (jnp.dot(a_ref[...], wa_ref[...], preferred_element_type=F32)
           + jnp.dot(b_ref[...], wb_ref[...], preferred_element_type=F32))
    y = _layer_norm(DN_ALPHA * x_ref[...] + mix, g_ref[...], bb_ref[...])
    if len(rest) == 2:
        xo_ref, xb_ref = rest
    else:
        wr_ref, br_ref, xo_ref, xb_ref, rt_ref = rest
        rt_ref[...] = _route_top2(y, wr_ref[...], br_ref[...])
    xo_ref[...] = y
    xb_ref[...] = y.astype(BF16)


def _outproj_ln(x, oa, ob, wa, wb, g, b, router=None, tm=ROW_TILE):
    n, d = x.shape
    ka, kb = oa.shape[1], ob.shape[1]
    row = lambda i: (i, 0)
    fixed = lambda i: (0, 0)
    in_specs = [pl.BlockSpec((tm, d), row), pl.BlockSpec((tm, ka), row), pl.BlockSpec((tm, kb), row),
                pl.BlockSpec((ka, d), fixed), pl.BlockSpec((kb, d), fixed),
                pl.BlockSpec((1, d), fixed), pl.BlockSpec((1, d), fixed)]
    out_specs = [pl.BlockSpec((tm, d), row), pl.BlockSpec((tm, d), row)]
    out_shape = [jax.ShapeDtypeStruct((n, d), F32), jax.ShapeDtypeStruct((n, d), BF16)]
    args = [x, oa, ob, wa, wb, g.reshape(1, d), b.reshape(1, d)]
    if router is not None:
        in_specs += [pl.BlockSpec((d, LANES), fixed), pl.BlockSpec((1, LANES), fixed)]
        out_specs.append(pl.BlockSpec((tm, LANES), row))
        out_shape.append(jax.ShapeDtypeStruct((n, LANES), F32))
        args += list(router)
    return pl.pallas_call(
        _outproj_ln_kernel,
        grid=(n // tm,),
        in_specs=in_specs,
        out_specs=out_specs,
        out_shape=out_shape,
        compiler_params=_cparams(("parallel",)),
        name="outproj_ln",
    )(*args)


def _ffn_ln_kernel(x_ref, xb_ref, wg_ref, wu_ref, wd_ref, g_ref, b_ref, xo_ref, xob_ref):
    xb = xb_ref[...]
    h = jax.nn.silu(jnp.dot(xb, wg_ref[...], preferred_element_type=F32)) * jnp.dot(
        xb, wu_ref[...], preferred_element_type=F32)
    ffn = jnp.dot(h.astype(BF16), wd_ref[...], preferred_element_type=F32)
    y = _layer_norm(DN_ALPHA * x_ref[...] + ffn, g_ref[...], b_ref[...])
    xo_ref[...] = y
    xob_ref[...] = y.astype(BF16)


def _ffn_ln(x, xb, wg, wu, wd, g, b, tm=FFN_TILE):
    n, d = x.shape
    fdim = wg.shape[1]
    row = lambda i: (i, 0)
    fixed = lambda i: (0, 0)
    once = pl.Buffered(1)
    return pl.pallas_call(
        _ffn_ln_kernel,
        grid=(n // tm,),
        in_specs=[pl.BlockSpec((tm, d), row), pl.BlockSpec((tm, d), row),
                  pl.BlockSpec((d, fdim), fixed, pipeline_mode=once),
                  pl.BlockSpec((d, fdim), fixed, pipeline_mode=once),
                  pl.BlockSpec((fdim, d), fixed, pipeline_mode=once),
                  pl.BlockSpec((1, d), fixed), pl.BlockSpec((1, d), fixed)],
        out_specs=[pl.BlockSpec((tm, d), row), pl.BlockSpec((tm, d), row)],
        out_shape=[jax.ShapeDtypeStruct((n, d), F32), jax.ShapeDtypeStruct((n, d), BF16)],
        compiler_params=_cparams(("parallel",)),
        name="ffn_ln",
    )(x, xb, wg, wu, wd, g.reshape(1, d), b.reshape(1, d))


def _top_n_mask(v, n, axis):
    idx = lax.broadcasted_iota(I32, v.shape, axis).astype(F32)
    sel = jnp.zeros(v.shape, F32)
    for _ in range(n):
        mx = jnp.max(v, axis=axis, keepdims=True)
        first = jnp.min(jnp.where(v == mx, idx, float(v.shape[axis])), axis=axis, keepdims=True)
        pick = idx == first
        sel = jnp.where(pick, 1.0, sel)
        v = jnp.where(pick, NINF, v)
    return sel


DV_PAD = 16
TQ = 256


def _with_ones(vt):
    return jnp.concatenate([vt, jnp.ones((DV_PAD, vt.shape[1]), vt.dtype)], axis=0)


def _mm_t_kernel(x_ref, wv_ref, ws_ref, *rest, tk):
    x = x_ref[...].astype(BF16)
    vt_ref, small_ref = rest[-2:] if len(rest) == 2 else rest[1:3]
    acc = jnp.dot(x, wv_ref[...], preferred_element_type=F32)
    for cc in range(acc.shape[0] // tk):
        vt_ref[cc] = acc[cc * tk:(cc + 1) * tk, :].T.astype(vt_ref.dtype)
    small_ref[...] = jnp.dot(x, ws_ref[...], preferred_element_type=F32)
    if len(rest) == 4:
        rest[3][...] = jnp.dot(x, rest[0][...], preferred_element_type=F32).astype(rest[3].dtype)


def _proj_values(x, w_val, w_small, w_plain, tm, tk):
    n, d = x.shape
    pv, ps = w_val.shape[1], w_small.shape[1]
    row = lambda i: (i, 0)
    fixed = lambda i: (0, 0)
    in_specs = [pl.BlockSpec((tm, d), row), pl.BlockSpec((d, pv), fixed), pl.BlockSpec((d, ps), fixed)]
    out_specs = [pl.BlockSpec((tm // tk, pv, tk), lambda i: (i, 0, 0)), pl.BlockSpec((tm, ps), row)]
    out_shape = [jax.ShapeDtypeStruct((n // tk, pv, tk), BF16), jax.ShapeDtypeStruct((n, ps), F32)]
    args = [x, w_val, w_small]
    if w_plain is not None:
        pp = w_plain.shape[1]
        in_specs.append(pl.BlockSpec((d, pp), fixed))
        out_specs.append(pl.BlockSpec((tm, pp), row))
        out_shape.append(jax.ShapeDtypeStruct((n, pp), BF16))
        args.append(w_plain)
    return pl.pallas_call(
        functools.partial(_mm_t_kernel, tk=tk),
        grid=(n // tm,),
        in_specs=in_specs,
        out_specs=out_specs,
        out_shape=out_shape,
        compiler_params=_cparams(("parallel",)),
        name="proj_t",
    )(*args)


def _normalize_t(acc, width):
    return acc[:width] * _safe_recip(acc[width:width + 1])


def _pipe_flash(n, ns, qk, vt_at, bias_at, bufs, dv, tk, tq):
    sa, sb, pa, pb = bufs

    def softmax_into(p_ref, j, st, m):
        m_new = jnp.maximum(m, jnp.max(st, axis=0, keepdims=True))
        p_ref[j] = jnp.exp2((st - m_new).astype(BF16))
        return m_new, jnp.exp2(m - m_new)

    def half(c, carry, s_cur, s_nxt, p_prev, p_cur):
        if s_nxt is not None:
            nxt = jnp.minimum(c + 1, n - 1)
            for j in range(ns):
                s_nxt[j] = qk(nxt, j)
        out = []
        for j in range(ns):
            m, acc, alpha = carry[j]
            acc = alpha * acc + jnp.dot(vt_at(c - 1, j), p_prev[j], preferred_element_type=F32)
            m, alpha = softmax_into(p_cur, j, s_cur[j] + bias_at(c, j), m)
            out.append((m, acc, alpha))
        return tuple(out)

    for j in range(ns):
        sa[j] = qk(0, j)
    first = []
    for j in range(ns):
        sb[j] = qk(jnp.minimum(1, n - 1), j)
        m, alpha = softmax_into(pa, j, sa[j] + bias_at(0, j), jnp.full((1, tq), M_INIT, F32))
        first.append((m, jnp.zeros((dv, tq), F32), alpha))

    def body(t, carry):
        carry = half(2 * t + 1, carry, sb, sa, pa, pb)
        return half(2 * t + 2, carry, sa, sb, pb, pa)

    carry = lax.fori_loop(0, (n - 1) // 2, body, tuple(first))

    def flush(carry, p_last):
        return tuple(alpha * acc + jnp.dot(vt_at(n - 1, j), p_last[j], preferred_element_type=F32)
                     for j, (_, acc, alpha) in enumerate(carry))

    def odd_tail(carry):
        return flush(half(n - 1, carry, sb, None, pa, pb), pb)

    return lax.cond(n % 2 == 0, odd_tail, lambda carry: flush(carry, pa), carry)


def _pipe_scratch(ns, tk, tq):
    return [pltpu.VMEM((ns, tk, tq), F32)] * 2 + [pltpu.VMEM((ns, tk, tq), BF16)] * 2


def _causal_t(t):
    return jnp.where(lax.broadcasted_iota(I32, (t, t), 0) <= lax.broadcasted_iota(I32, (t, t), 1), 0.0, NINF)


def _diff_kernel(lam_ref, sub_ref, q_ref, k_ref, vt_ref, o_ref, *bufs, tq, lam_init):
    i = pl.program_id(2)
    lp = lam_ref[...]
    lam = (jnp.exp(jnp.sum(lp[0:1] * lp[1:2], axis=1, keepdims=True))
           - jnp.exp(jnp.sum(lp[2:3] * lp[3:4], axis=1, keepdims=True)) + lam_init)
    qs = _split_halves(q_ref[...], roped=True)
    n = i + 1
    bufs, tab_ref = bufs[:4], bufs[4]
    tab_ref[0] = jnp.zeros((tq, tq), F32)
    tab_ref[1] = _causal_t(tq)

    def qk(c, j):
        off = pl.multiple_of(c * tq, tq)
        return _dot_nt(k_ref[pl.ds(off, tq), :], qs[j])

    def bias_at(c, j):
        return tab_ref[jnp.where(c == i, 1, 0)]

    outs = _pipe_flash(n, 2, qk, lambda c, j: _with_ones(vt_ref[c]), bias_at, bufs, LANES + DV_PAD, tq, tq)
    o = _normalize_t(outs[0], LANES) - lam * _normalize_t(outs[1], LANES)
    o = o * lax.rsqrt(jnp.mean(o * o, axis=0, keepdims=True) + LN_EPS)
    o = o * sub_ref[...] * (1.0 - lam_init)
    o_ref[...] = o.T.astype(o_ref.dtype)


def _diff_attention(rp, vt, lam_params, subln, lam_init, tq=TQ):
    b, s, _ = rp.shape
    nk = s // tq
    return pl.pallas_call(
        functools.partial(_diff_kernel, tq=tq, lam_init=lam_init),
        grid=(b, A_HEADS, nk),
        in_specs=[pl.BlockSpec((4, HEAD_DIM), lambda bi, h, i: (0, 0)),
                  pl.BlockSpec((LANES, 1), lambda bi, h, i: (0, 0)),
                  pl.BlockSpec((None, tq, LANES), lambda bi, h, i: (bi, i, h)),
                  pl.BlockSpec((None, s, LANES), lambda bi, h, i: (bi, 0, A_HEADS + h)),
                  pl.BlockSpec((None, nk, LANES, tq), lambda bi, h, i: (bi, 0, h, 0))],
        out_specs=pl.BlockSpec((None, tq, LANES), lambda bi, h, i: (bi, i, h)),
        out_shape=jax.ShapeDtypeStruct((b, s, A_HEADS * LANES), BF16),
        scratch_shapes=_pipe_scratch(2, tq, tq) + [pltpu.VMEM((2, tq, tq), F32)],
        compiler_params=_cparams(("parallel", "parallel", "arbitrary")),
        name="diff_attn",
    )(lam_params, subln.reshape(LANES, 1), rp, rp, vt)


def _fold8(x, op):
    parts = [x[r * 8:(r + 1) * 8] for r in range(x.shape[0] // 8)]
    while len(parts) > 1:
        parts = [op(parts[a], parts[a + 1]) for a in range(0, len(parts) - 1, 2)] + parts[len(parts) & ~1:]
    return parts[0]


def _dsa_kernel(iq_ref, ikk_ref, iw_ref, q_ref, k_ref, vt_ref, o_ref, s_ref, j_ref, *bufs, tq, ksel, seq_len):
    tk = tq
    i = pl.program_id(1)
    nch = i + 1
    ksel_f = float(ksel)
    k_loc = lax.broadcasted_iota(I32, (tk, tq), 0)
    q_loc = lax.broadcasted_iota(I32, (tk, tq), 1)
    qpos = i * tq + lax.broadcasted_iota(I32, (1, tq), 1)

    iq = iq_ref[...]
    iwt = iw_ref[...].T
    iqh = []
    for pair in range(IDX_HEADS // 2):
        iqh += list(_split_halves(iq[:, pair * LANES:(pair + 1) * LANES], roped=True))

    def scores(c):
        off = pl.multiple_of(c * tk, tk)
        kk = ikk_ref[pl.ds(off, tk), :]
        sc = iwt[0:1] * jnp.maximum(_dot_nt(kk, iqh[0]), 0.0)
        for h in range(1, IDX_HEADS):
            sc = sc + iwt[h:h + 1] * jnp.maximum(_dot_nt(kk, iqh[h]), 0.0)
        return sc

    def full_body(c, carry):
        mx, mn = carry
        sc = scores(c)
        s_ref[c] = sc
        return jnp.maximum(mx, _fold8(sc, jnp.maximum)), jnp.minimum(mn, _fold8(sc, jnp.minimum))

    mx, mn = lax.fori_loop(0, i, full_body, (jnp.full((8, tq), -BIG, F32), jnp.full((8, tq), BIG, F32)))
    sc = scores(i)
    causal = k_loc <= q_loc
    s_ref[i] = jnp.where(causal, sc, NEG)
    mx = jnp.maximum(mx, _fold8(jnp.where(causal, sc, -BIG), jnp.maximum))
    mn = jnp.minimum(mn, _fold8(jnp.where(causal, sc, BIG), jnp.minimum))
    smax = jnp.max(mx, axis=0, keepdims=True)
    smin = jnp.min(mn, axis=0, keepdims=True)

    def count_where(ind):
        def body(c, acc):
            return acc + _fold8(ind(s_ref[c], c * tk + k_loc), jnp.add)
        acc = lax.fori_loop(0, nch, body, jnp.zeros((8, tq), F32))
        return jnp.sum(acc, axis=0, keepdims=True)

    def count_ge(th):
        return count_where(lambda x, kidx: jnp.where(x >= th, 1.0, 0.0))

    def max_below(th):
        def body(c, acc):
            x = s_ref[c]
            return jnp.maximum(acc, _fold8(jnp.where(x < th, x, NINF), jnp.maximum))
        acc = lax.fori_loop(0, nch, body, jnp.full((8, tq), NINF, F32))
        return jnp.max(acc, axis=0, keepdims=True)

    n_causal = (qpos + 1).astype(F32)
    take_all = n_causal <= ksel_f
    done0 = jnp.where(take_all, 1.0, 0.0)
    hi0 = smax + (jnp.abs(smax) * 2.0 ** -20 + 1e-30)

    def bisect(lo, hi):
        mid = lo + (hi - lo) * 0.5
        ge = count_ge(mid) >= ksel_f
        return jnp.where(ge, mid, lo), jnp.where(ge, hi, mid)

    lo, hi = lax.fori_loop(0, N_BISECT, lambda _, c: bisect(*c), (smin, hi0))

    def snap_body(carry):
        lo, hi, th, c_th, done, _ = carry
        lo, hi = bisect(lo, hi)
        t1 = max_below(hi)
        c1 = count_ge(t1)
        ok = c1 >= ksel_f
        th = jnp.where(done > 0.0, th, t1)
        c_th = jnp.where(done > 0.0, c_th, c1)
        hi = jnp.where(ok, hi, t1)
        done = jnp.where(ok, 1.0, done)
        return lo, hi, th, c_th, done, jnp.sum(1.0 - done)

    _, _, th, c_ge, _, _ = lax.while_loop(lambda c: c[5] > 0.0, snap_body,
                                          (lo, hi, smax, jnp.zeros((1, tq), F32), done0, jnp.sum(1.0 - done0)))

    need_tb = jnp.where(take_all, 0.0, jnp.where(c_ge > ksel_f, 1.0, 0.0))
    j_ref[...] = jnp.full((8, tq), seq_len - 1, I32)

    @pl.when(jnp.sum(need_tb) > 0.0)
    def _():
        need = ksel_f - count_where(lambda x, kidx: jnp.where(x > th, 1.0, 0.0))

        def jb(_, carry):
            lo_j, hi_j = carry
            mid = (lo_j + hi_j) // 2
            cnt = count_where(lambda x, kidx: jnp.where(x == th, jnp.where(kidx <= mid, 1.0, 0.0), 0.0))
            ge = cnt >= need
            return jnp.where(ge, lo_j, mid), jnp.where(ge, mid, hi_j)

        n_it = int(math.ceil(math.log2(seq_len))) + 1
        _, hi_j = lax.fori_loop(0, n_it, jb, (jnp.full((1, tq), -1, I32), jnp.full((1, tq), seq_len - 1, I32)))
        j_ref[...] = jnp.broadcast_to(hi_j, (8, tq))

    jsel = j_ref[0:1, :]

    def bias_body(c, _):
        x = s_ref[c]
        kidx = c * tk + k_loc
        keep = jnp.where(x > th, 0.0, jnp.where(x == th, jnp.where(kidx <= jsel, 0.0, NINF), NINF))
        keep = jnp.where(take_all, 0.0, keep)
        s_ref[c] = jnp.where(kidx <= qpos, keep, NINF)
        return 0

    lax.fori_loop(0, nch, bias_body, 0)

    for p in range(B_HEADS // 2):
        qs = _split_halves(q_ref[:, p * LANES:(p + 1) * LANES], roped=True)

        def qk(c, j, qs=qs):
            off = pl.multiple_of(c * tk, tk)
            return _dot_nt(k_ref[pl.ds(off, tk), :], qs[j])

        outs = _pipe_flash(nch, 2, qk, lambda c, j: _with_ones(vt_ref[c, j * HEAD_DIM:(j + 1) * HEAD_DIM, :]),
                           lambda c, j: s_ref[c], bufs, HEAD_DIM + DV_PAD, tk, tq)
        o = jnp.concatenate([_normalize_t(outs[0], HEAD_DIM), _normalize_t(outs[1], HEAD_DIM)], axis=0)
        o_ref[:, p * LANES:(p + 1) * LANES] = o.T.astype(o_ref.dtype)


def _dsa_attention(rp, vt, iw, tq=TQ):
    b, s, _ = rp.shape
    ksel = min(DSA_TOPK, s // 4)
    nk = s // tq
    return pl.pallas_call(
        functools.partial(_dsa_kernel, tq=tq, ksel=ksel, seq_len=s),
        grid=(b, nk),
        in_specs=[pl.BlockSpec((None, tq, 2 * LANES), lambda bi, i: (bi, i, 6)),
                  pl.BlockSpec((None, s, LANES), lambda bi, i: (bi, 0, 15)),
                  pl.BlockSpec((None, tq, LANES), lambda bi, i: (bi, i, 0)),
                  pl.BlockSpec((None, tq, 4 * LANES), lambda bi, i: (bi, i, 2)),
                  pl.BlockSpec((None, s, LANES), lambda bi, i: (bi, 0, 14)),
                  pl.BlockSpec((None, nk, LANES, tq), lambda bi, i: (bi, 0, 4, 0))],
        out_specs=pl.BlockSpec((None, tq, 4 * LANES), lambda bi, i: (bi, i, 0)),
        out_shape=jax.ShapeDtypeStruct((b, s, 4 * LANES), BF16),
        scratch_shapes=[pltpu.VMEM((nk, tq, tq), F32), pltpu.VMEM((8, tq), I32)] + _pipe_scratch(2, tq, tq),
        compiler_params=_cparams(("parallel", "arbitrary")),
        name="dsa_attn",
    )(rp, rp, iw, rp, rp, vt)


def _moba_kernel(q_ref, k_ref, vt_ref, o_ref, km_ref, sel_ref, *bufs, seq_len, n_sel):
    tq = MOBA_BLOCK
    qb = pl.program_id(2)

    @pl.when(qb == 0)
    def _():
        j = lax.broadcasted_iota(I32, (LANES, seq_len), 0)
        s = lax.broadcasted_iota(I32, (LANES, seq_len), 1)
        avg = jnp.where(s // MOBA_BLOCK == j, 1.0 / MOBA_BLOCK, 0.0).astype(BF16)
        km_ref[...] = jnp.dot(avg, k_ref[...], preferred_element_type=F32)

    nbp = sel_ref.shape[1]
    km = km_ref[0:nbp, :]
    qs = _split_halves(q_ref[...], roped=True)
    blk = lax.broadcasted_iota(I32, (nbp, tq), 0)
    past = blk < qb
    for j in range(2):
        gate = lax.dot_general(km, qs[j].astype(F32), (((1,), (1,)), ((), ())),
                               precision=lax.Precision.HIGHEST, preferred_element_type=F32)
        gate = jnp.where(blk < seq_len // MOBA_BLOCK, jnp.where(past, gate, NEG), NINF)
        sel = _top_n_mask(gate, n_sel, 0)
        sel_ref[j] = jnp.where(past, jnp.where(sel > 0.5, 0.0, NINF), NINF)
    own = _causal_t(tq)
    n = qb + 1

    def qk(c, j):
        off = pl.multiple_of(c * tq, tq)
        return _dot_nt(k_ref[pl.ds(off, tq), :], qs[j])

    def bias_at(c, j):
        chosen = sel_ref[j, pl.ds(c, 1), :]
        return jnp.where(c == qb, own, chosen)

    outs = _pipe_flash(n, 2, qk, lambda c, j: _with_ones(vt_ref[c, j * HEAD_DIM:(j + 1) * HEAD_DIM, :]), bias_at, bufs,
                       HEAD_DIM + DV_PAD, tq, tq)
    o = jnp.concatenate([_normalize_t(outs[0], HEAD_DIM), _normalize_t(outs[1], HEAD_DIM)], axis=0)
    o_ref[...] = o.T.astype(o_ref.dtype)


def _moba_attention(rp, vt):
    b, s, _ = rp.shape
    tq = MOBA_BLOCK
    nb = s // tq
    n_sel = max(1, min(MOBA_TOPK, nb - 1))
    npair = C_HEADS // 2
    return pl.pallas_call(
        functools.partial(_moba_kernel, seq_len=s, n_sel=n_sel),
        grid=(b, npair, nb),
        in_specs=[pl.BlockSpec((None, tq, LANES), lambda bi, h, i: (bi, i, h)),
                  pl.BlockSpec((None, s, LANES), lambda bi, h, i: (bi, 0, npair + h)),
                  pl.BlockSpec((None, nb, LANES, tq), lambda bi, h, i: (bi, 0, h, 0))],
        out_specs=pl.BlockSpec((None, tq, LANES), lambda bi, h, i: (bi, i, h)),
        out_shape=jax.ShapeDtypeStruct((b, s, npair * LANES), BF16),
        scratch_shapes=[pltpu.VMEM((LANES, LANES), F32), pltpu.VMEM((2, -(-nb // 8) * 8, tq), F32)]
        + _pipe_scratch(2, tq, tq),
        compiler_params=_cparams(("parallel", "parallel", "arbitrary")),
        name="moba_attn",
    )(rp, rp, vt)


def _cmp_kernel(r_ref, pe_ref, w1_ref, w2_ref, o_ref):
    r = r_ref[...]
    w1 = w1_ref[...]
    half = r.shape[1]
    u = jnp.dot(r, w1[:half], preferred_element_type=F32)
    v = jnp.dot(r, w1[half:], preferred_element_type=F32)
    c = jnp.dot(pe_ref[...], w1, preferred_element_type=F32)[0:1]
    pre = u + pltpu.roll(v, r.shape[0] - 1, 0) + c
    o_ref[...] = jnp.dot(jax.nn.gelu(pre).astype(BF16), w2_ref[...],
                         preferred_element_type=F32).astype(o_ref.dtype)


def _nsa_compress(r, pe, w1, w2):
    b, _, nc, wdt = r.shape
    hid = w1.shape[2]
    return pl.pallas_call(
        _cmp_kernel,
        grid=(b, 4),
        in_specs=[pl.BlockSpec((None, None, nc, wdt), lambda bi, t: (bi, t, 0, 0)),
                  pl.BlockSpec((None, 8, 2 * wdt), lambda bi, t: (t // 2, 0, 0)),
                  pl.BlockSpec((None, 2 * wdt, hid), lambda bi, t: (t // 2, 0, 0)),
                  pl.BlockSpec((None, hid, HEAD_DIM), lambda bi, t: (t // 2, 0, 0))],
        out_specs=pl.BlockSpec((None, None, nc, HEAD_DIM), lambda bi, t: (bi, t, 0, 0)),
        out_shape=jax.ShapeDtypeStruct((b, 4, nc, HEAD_DIM), BF16),
        compiler_params=_cparams(("parallel", "arbitrary")),
        name="nsa_compress",
    )(r, pe, w1, w2)


def _nsa_kernel(qr_ref, qw_ref, dg_ref, gb_ref, kc_ref, vct_ref, ks_ref, vst_ref, kw_ref, vwt_ref,
                o_ref, sel_ref, wb_ref, *bufs, tq, seq_len):
    tk = tq
    i = pl.program_id(1)
    nch = i + 1
    nc = seq_len // NSA_CMP_STRIDE
    n_sb = seq_len // NSA_SLC_BLOCK
    n_sel = min(NSA_SLC_TOPK, n_sb)
    k_loc = lax.broadcasted_iota(I32, (tk, tq), 0)
    q_loc = lax.broadcasted_iota(I32, (tk, tq), 1)
    qpos = i * tq + lax.broadcasted_iota(I32, (1, tq), 1)

    gates_t = jax.nn.sigmoid(dg_ref[...] + gb_ref[...]).T
    kc = kc_ref[...]
    cmp_end = lax.broadcasted_iota(I32, (nc, 1), 0) * NSA_CMP_STRIDE + (NSA_CMP_LEN - 1)
    cbias = jnp.where(cmp_end <= qpos, 0.0, NINF)

    nh = D_HEADS // 2
    q_rot = [_split_halves(qr_ref[:, p * LANES:(p + 1) * LANES], roped=True) for p in range(nh)]
    q_raw = [_split_halves(qw_ref[:, p * LANES:(p + 1) * LANES]) for p in range(nh)]
    cbias4 = jnp.concatenate([cbias] * nh, axis=1)
    o_cmp, psum = [], []
    for g in range(2):
        qg = jnp.concatenate([q_raw[p][g] for p in range(nh)], axis=0)
        s = _dot_nt(kc, qg) + cbias4
        m = jnp.max(s, axis=0, keepdims=True)
        e = jnp.exp2(s - jnp.where(m == NINF, 0.0, m))
        pc = e * _safe_recip(jnp.sum(e, axis=0, keepdims=True))
        psum.append(sum(pc[:, p * tq:(p + 1) * tq] for p in range(nh)))
        o_cmp.append(jnp.dot(vct_ref[g], pc.astype(BF16), preferred_element_type=F32))

    per = tk // NSA_SLC_BLOCK
    nbp = -(-n_sb // 8) * 8
    cn = lax.broadcasted_iota(I32, (nbp, nc), 1) * NSA_CMP_STRIDE
    sj = lax.broadcasted_iota(I32, (nbp, nc), 0) * NSA_SLC_BLOCK
    shares = jnp.where((cn <= sj + NSA_SLC_BLOCK - 1) & (cn + NSA_CMP_LEN - 1 >= sj), 1.0, 0.0)
    blk = lax.broadcasted_iota(I32, (nbp, tq), 0)
    cur = qpos // NSA_SLC_BLOCK
    causal_b = blk <= cur
    forced = (blk == 0) | ((blk >= cur - 1) & causal_b)
    for g in range(2):
        imp = jnp.dot(shares, psum[g], precision=lax.Precision.HIGHEST, preferred_element_type=F32)
        val = jnp.where(forced, BIG, jnp.where(causal_b, imp, NEG))
        val = jnp.where(blk < n_sb, val, NINF)
        rowb = jnp.where(_top_n_mask(val, n_sel, 0) > 0.5, 0.0, NINF)
        for c in range(seq_len // tk):
            sel_ref[g, c] = jnp.concatenate([rowb[c * per:(c + 1) * per], jnp.zeros((8 - per, tq), F32)], axis=0)

    wb_ref[0] = jnp.where(k_loc <= q_loc, 0.0, NINF)
    wb_ref[1] = jnp.zeros((tk, tq), F32)
    wb_ref[2] = jnp.where(k_loc > q_loc, 0.0, NINF)
    n_wc = NSA_WINDOW // tk + 1
    w_first = jnp.maximum(i - (n_wc - 1), 0)
    n_w = i - w_first + 1

    for p in range(D_HEADS // 2):
        qs = q_rot[p]

        def qk_s(c, j, qs=qs):
            off = pl.multiple_of(c * tk, tk)
            return _dot_nt(ks_ref[pl.ds(off, tk), :], qs[j])

        def bias_s(c, j):
            rows = sel_ref[j, c]
            tile = jnp.concatenate([jnp.broadcast_to(rows[r:r + 1], (NSA_SLC_BLOCK, tq)) for r in range(per)], axis=0)
            return tile + wb_ref[jnp.where(c == i, 0, 1)]

        o_slc = _pipe_flash(nch, 2, qk_s, lambda c, j: _with_ones(vst_ref[c, j * HEAD_DIM:(j + 1) * HEAD_DIM, :]),
                            bias_s, bufs, HEAD_DIM + DV_PAD, tk, tq)

        def qk_w(c, j, qs=qs):
            off = pl.multiple_of((w_first + c) * tk, tk)
            return _dot_nt(kw_ref[pl.ds(off, tk), :], qs[j])

        def bias_w(c, j):
            return wb_ref[i - (w_first + c)]

        o_win = _pipe_flash(n_w, 2, qk_w,
                            lambda c, j: _with_ones(vwt_ref[w_first + c, j * HEAD_DIM:(j + 1) * HEAD_DIM, :]), bias_w, bufs,
                            HEAD_DIM + DV_PAD, tk, tq)
        outs = []
        for g in range(2):
            h = g * (D_HEADS // 2) + p
            outs.append(gates_t[3 * h:3 * h + 1] * o_cmp[g][:, p * tq:(p + 1) * tq]
                        + gates_t[3 * h + 1:3 * h + 2] * _normalize_t(o_slc[g], HEAD_DIM)
                        + gates_t[3 * h + 2:3 * h + 3] * _normalize_t(o_win[g], HEAD_DIM))
        o_ref[:, p * LANES:(p + 1) * LANES] = jnp.concatenate(outs, axis=0).T.astype(o_ref.dtype)


def _nsa_attention(rp, pp, vt, dg, gate_b, kcmp, vcmp, tq=TQ):
    b, s, _ = rp.shape
    nk = s // tq
    nc = s // NSA_CMP_STRIDE
    assert NSA_WINDOW == 2 * tq
    n_wc = NSA_WINDOW // tq + 1
    vct = vcmp.reshape(b, nc, 2, HEAD_DIM).transpose(0, 2, 3, 1)
    full = lambda t: pl.BlockSpec((None, s, LANES), lambda bi, i: (bi, 0, t))
    vspec = lambda t: pl.BlockSpec((None, nk, LANES, tq), lambda bi, i: (bi, 0, t, 0))
    return pl.pallas_call(
        functools.partial(_nsa_kernel, tq=tq, seq_len=s),
        grid=(b, nk),
        in_specs=[pl.BlockSpec((None, tq, 4 * LANES), lambda bi, i: (bi, i, 2)),
                  pl.BlockSpec((None, tq, 4 * LANES), lambda bi, i: (bi, i, 0)),
                  pl.BlockSpec((None, tq, LANES), lambda bi, i: (bi, i, 0)),
                  pl.BlockSpec((1, LANES), lambda bi, i: (0, 0)),
                  pl.BlockSpec((None, nc, LANES), lambda bi, i: (bi, 0, 0)),
                  pl.BlockSpec((None, 2, HEAD_DIM, nc), lambda bi, i: (bi, 0, 0, 0)),
                  full(12), vspec(4), full(13), vspec(5)],
        out_specs=pl.BlockSpec((None, tq, 4 * LANES), lambda bi, i: (bi, i, 0)),
        out_shape=jax.ShapeDtypeStruct((b, s, 4 * LANES), BF16),
        scratch_shapes=[pltpu.VMEM((2, nk, 8, tq), F32), pltpu.VMEM((n_wc, tq, tq), F32)]
        + _pipe_scratch(2, tq, tq),
        compiler_params=_cparams(("parallel", "arbitrary")),
        name="nsa_attn",
    )(rp, pp, dg, gate_b, kcmp, vct, rp, vt, rp, vt)


def _cast_kernel(x_ref, o_ref):
    o_ref[...] = x_ref[...].astype(o_ref.dtype)


def _layer_to_bf16(w, layer):
    _, e, r, c = w.shape
    tr = 1 << ((2 ** 21 // c).bit_length() - 1)
    out = pl.pallas_call(
        _cast_kernel,
        grid=(e * r // tr,),
        in_specs=[pl.BlockSpec((None, tr, c), lambda i: (layer, i, 0))],
        out_specs=pl.BlockSpec((tr, c), lambda i: (i, 0)),
        out_shape=jax.ShapeDtypeStruct((e * r, c), BF16),
        compiler_params=_cparams(("parallel",)),
        name="cast_bf16",
    )(w.reshape(w.shape[0], e * r, c))
    return out.reshape(e, r, c)


def _moe_ffn_kernel(be_ref, nu_ref, x_ref, wg_ref, wu_ref, wd_ref, o_ref):
    used = pl.program_id(0) < nu_ref[0]

    @pl.when(used)
    def _():
        x = x_ref[...]
        h = jax.nn.silu(jnp.dot(x, wg_ref[...], preferred_element_type=F32)) * jnp.dot(
            x, wu_ref[...], preferred_element_type=F32)
        o_ref[...] = jnp.dot(h.astype(BF16), wd_ref[...], preferred_element_type=F32).astype(o_ref.dtype)

    @pl.when(jnp.logical_not(used))
    def _():
        o_ref[...] = jnp.zeros(o_ref.shape, o_ref.dtype)


def _moe_ffn(x_sorted, block_e, n_used, wg, wu, wd):
    ns, d = x_sorted.shape
    fdim = wg.shape[2]
    once = pl.Buffered(1)
    grid_spec = pltpu.PrefetchScalarGridSpec(
        num_scalar_prefetch=2,
        grid=(ns // MOE_TM,),
        in_specs=[pl.BlockSpec((MOE_TM, d), lambda i, be, nu: (jnp.minimum(i, nu[0] - 1), 0)),
                  pl.BlockSpec((None, d, fdim), lambda i, be, nu: (be[i], 0, 0), pipeline_mode=once),
                  pl.BlockSpec((None, d, fdim), lambda i, be, nu: (be[i], 0, 0), pipeline_mode=once),
                  pl.BlockSpec((None, fdim, d), lambda i, be, nu: (be[i], 0, 0), pipeline_mode=once)],
        out_specs=pl.BlockSpec((MOE_TM, d), lambda i, be, nu: (i, 0)),
    )
    return pl.pallas_call(
        _moe_ffn_kernel,
        grid_spec=grid_spec,
        out_shape=jax.ShapeDtypeStruct((ns, d), BF16),
        compiler_params=_cparams(("arbitrary",)),
        name="moe_ffn",
    )(block_e, n_used, x_sorted, wg, wu, wd)


def _combine_ln_kernel(x_ref, y0_ref, y1_ref, rt_ref, g_ref, b_ref, xo_ref, xb_ref):
    rt = rt_ref[...]
    ffn = rt[:, 2:3] * y0_ref[...].astype(F32) + rt[:, 3:4] * y1_ref[...].astype(F32)
    y = _layer_norm(DN_ALPHA * x_ref[...] + ffn, g_ref[...], b_ref[...])
    xo_ref[...] = y
    xb_ref[...] = y.astype(BF16)


def _combine_ln(x, y0, y1, rt, g, b, tm=ROW_TILE):
    n, d = x.shape
    row = lambda i: (i, 0)
    fixed = lambda i: (0, 0)
    return pl.pallas_call(
        _combine_ln_kernel,
        grid=(n // tm,),
        in_specs=[pl.BlockSpec((tm, d), row), pl.BlockSpec((tm, d), row), pl.BlockSpec((tm, d), row),
                  pl.BlockSpec((tm, LANES), row), pl.BlockSpec((1, d), fixed), pl.BlockSpec((1, d), fixed)],
        out_specs=[pl.BlockSpec((tm, d), row), pl.BlockSpec((tm, d), row)],
        out_shape=[jax.ShapeDtypeStruct((n, d), F32), jax.ShapeDtypeStruct((n, d), BF16)],
        compiler_params=_cparams(("parallel",)),
        name="moe_combine_ln",
    )(x, y0, y1, rt, g.reshape(1, d), b.reshape(1, d))


def _moe_layout(rt, n):
    e_flat = rt[:, 0:TOP_K].astype(I32).reshape(-1)
    nk = n * TOP_K
    onehot = (e_flat[:, None] == jnp.arange(N_EXPERTS, dtype=I32)[None, :]).astype(I32)
    rank = jnp.take_along_axis(jnp.cumsum(onehot, axis=0), e_flat[:, None], axis=1)[:, 0] - 1
    counts = jnp.sum(onehot, axis=0)
    padded = (counts + MOE_TM - 1) // MOE_TM * MOE_TM
    pad_end = jnp.cumsum(padded)
    pad_start = pad_end - padded
    grp_start = jnp.cumsum(counts) - counts
    slot = pad_start[e_flat] + rank
    n_blocks = -(-nk // MOE_TM) + N_EXPERTS
    n_slots = n_blocks * MOE_TM
    order = jnp.argsort(e_flat, stable=True).astype(I32)
    sl = jnp.arange(n_slots, dtype=I32)
    slot_e = jnp.minimum(jnp.searchsorted(pad_end, sl, side='right'), N_EXPERTS - 1).astype(I32)
    within = sl - pad_start[slot_e]
    valid = within < counts[slot_e]
    src = jnp.where(valid, grp_start[slot_e] + within, 0)
    slot_tok = jnp.where(valid, order[src] // TOP_K, 0)
    n_used = (pad_end[-1] // MOE_TM).astype(I32).reshape(1)
    blk = jnp.arange(n_blocks, dtype=I32)
    block_e = slot_e[jnp.minimum(blk, n_used[0] - 1) * MOE_TM]
    return slot_tok, slot.reshape(n, TOP_K), block_e, n_used


def _pair_perm(n_heads):
    half = n_heads // 2
    cols = []
    for p in range(half):
        cols += list(range(p * HEAD_DIM, (p + 1) * HEAD_DIM))
        cols += list(range((half + p) * HEAD_DIM, (half + p + 1) * HEAD_DIM))
    return np.asarray(cols, dtype=np.int32)


def _pad_cols(w, width):
    return jnp.pad(w, ((0, 0), (0, width - w.shape[1])))


def _split_cols(w, sizes):
    out, off = [], 0
    for sz in sizes:
        out.append(w[:, off:off + sz])
        off += sz
    return out


def _even_layer(x, xb, w_in, w_out, lam_params, subln, lam_init, wg, wu, wd, ln, tabs, bsz, seq_len):
    n = x.shape[0]
    perm = _pair_perm(B_HEADS)
    aq, ak, av, bq, bk, bv, iq, ik, iw = _split_cols(w_in, EVEN_SIZES)
    w_rope = jnp.concatenate([aq * SCALE, ak, bq[:, perm] * SCALE, iq, bk, ik, ik], axis=1)
    w_rope = w_rope[:, _rope_layout(w_rope.shape[1])].astype(BF16)
    w_val = jnp.concatenate([av, bv], axis=1).astype(BF16)
    w_iw = _pad_cols(iw, LANES).astype(BF16)
    rp = _proj_rope(xb, w_rope, ROW_TILE, w_rope.shape[1], seq_len, tabs).reshape(bsz, seq_len, -1)
    vt, iwv = _proj_values(xb, w_val, w_iw, None, ROW_TILE, TQ)
    vt = vt.reshape(bsz, seq_len // TQ, -1, TQ)
    iwv = iwv.reshape(bsz, seq_len, LANES)
    o_a = _diff_attention(rp, vt, lam_params, subln, lam_init)
    o_b = _dsa_attention(rp, vt, iwv)
    half = w_out.shape[0] // 2
    wo_a = w_out[:half].astype(BF16)
    wo_b = w_out[half:][perm].astype(BF16)
    g_mix, b_mix, g_ffn, b_ffn = ln
    x1, x1b = _outproj_ln(x, o_a.reshape(n, -1), o_b.reshape(n, -1), wo_a, wo_b, g_mix, b_mix)
    return _ffn_ln(x1, x1b, wg.astype(BF16), wu.astype(BF16), wd.astype(BF16), g_ffn, b_ffn)


def _odd_layer(x, xb, w_in, w_out, gate_b, pe, phi_w1, phi_w2, w_router, b_router, wg, wu, wd, ln, tabs,
               bsz, seq_len):
    n = x.shape[0]
    perm = _pair_perm(D_HEADS)
    cq, ck, cv, dq, dkc, dvc, dks, dvs, dkw, dvw, dg = _split_cols(w_in, ODD_SIZES)
    dq = dq[:, perm] * SCALE
    w_rope = jnp.concatenate([cq * SCALE, ck, dq, dks, dkw], axis=1)
    w_rope = w_rope[:, _rope_layout(w_rope.shape[1])].astype(BF16)
    w_plain = jnp.concatenate([dq, dkc, dvc], axis=1).astype(BF16)
    w_val = jnp.concatenate([cv, dvs, dvw], axis=1).astype(BF16)
    w_dg = _pad_cols(dg, LANES).astype(BF16)
    rp = _proj_rope(xb, w_rope, ROW_TILE, w_rope.shape[1], seq_len, tabs).reshape(bsz, seq_len, -1)
    vt, dgv, pp = _proj_values(xb, w_val, w_dg, w_plain, ROW_TILE, TQ)
    vt = vt.reshape(bsz, seq_len // TQ, -1, TQ)
    dgv = dgv.reshape(bsz, seq_len, LANES)
    pp = pp.reshape(bsz, seq_len, -1)

    o_c = _moba_attention(rp, vt)

    nc = seq_len // NSA_CMP_STRIDE
    tok = pp[:, :, 4 * LANES:6 * LANES].reshape(bsz, nc, NSA_CMP_STRIDE, 4, HEAD_DIM)
    r = tok.transpose(0, 3, 1, 2, 4).reshape(bsz, 4, nc, NSA_CMP_STRIDE * HEAD_DIM)
    pe_flat = jnp.pad(pe.reshape(2, 1, -1), ((0, 0), (0, 7), (0, 0))).astype(BF16)
    cmp = _nsa_compress(r, pe_flat, phi_w1.astype(BF16), phi_w2.astype(BF16))
    kcmp = jnp.concatenate([cmp[:, 0], cmp[:, 1]], axis=-1)
    vcmp = jnp.concatenate([cmp[:, 2], cmp[:, 3]], axis=-1)
    gb = _pad_cols(gate_b.reshape(1, -1), LANES)
    o_d = _nsa_attention(rp, pp, vt, dgv, gb, kcmp, vcmp)

    half = w_out.shape[0] // 2
    wo_c = w_out[:half].astype(BF16)
    wo_d = w_out[half:][perm].astype(BF16)
    g_mix, b_mix, g_ffn, b_ffn = ln
    router = (_pad_cols(w_router, LANES), _pad_cols(b_router.reshape(1, -1), LANES))
    x1, x1b, rt = _outproj_ln(x, o_c.reshape(n, -1), o_d.reshape(n, -1), wo_c, wo_d, g_mix, b_mix, router)

    slot_tok, slot, block_e, n_used = _moe_layout(rt, n)
    y_slots = _moe_ffn(x1b[slot_tok], block_e, n_used, wg, wu, wd)
    return _combine_ln(x1, y_slots[slot[:, 0]], y_slots[slot[:, 1]], rt, g_ffn, b_ffn)


@jax.jit
def kernel(x, ev_w_in, ev_w_out, dif_lambda, dif_subln, ffd_w_gate, ffd_w_up, ffd_w_down, od_w_in, od_w_out,
           nsa_gate_b, nsa_pe, nsa_phi_w1, nsa_phi_w2, moe_w_router, moe_b_router, moe_w_gate, moe_w_up,
           moe_w_down, ln_mix_g, ln_mix_b, ln_ffn_g, ln_ffn_b):
    bsz, seq_len, d = x.shape
    tabs = _rope_tables(seq_len)
    xf = x.reshape(bsz * seq_len, d)
    xb = xf.astype(BF16)
    for l in range(DEPTH):
        i = l // 2
        ln = (ln_mix_g[l], ln_mix_b[l], ln_ffn_g[l], ln_ffn_b[l])
        if l % 2 == 0:
            lam_init = 0.8 - 0.6 * math.exp(-0.3 * l)
            xf, xb = _even_layer(xf, xb, ev_w_in[i], ev_w_out[i], dif_lambda[i], dif_subln[i], lam_init,
                                 ffd_w_gate[i], ffd_w_up[i], ffd_w_down[i], ln, tabs, bsz, seq_len)
        else:
            xf, xb = _odd_layer(xf, xb, od_w_in[i], od_w_out[i], nsa_gate_b[i], nsa_pe[i], nsa_phi_w1[i],
                                nsa_phi_w2[i], moe_w_router[i], moe_b_router[i], _layer_to_bf16(moe_w_gate, i),
                                _layer_to_bf16(moe_w_up, i), _layer_to_bf16(moe_w_down, i), ln, tabs, bsz, seq_len)
    return xf.reshape(bsz, seq_len, d)
```
